```python
import jax, jax.numpy as jnp
from jax import lax
import numpy as np

D_MODEL = 1024
BATCH = 8
SEQ = 8192
DEPTH = 4

HEAD_DIM = 64
SB_HEADS = D_MODEL // 256
FOX_HEADS = D_MODEL // 128
SGU_GROUPS = D_MODEL // 256
SGU_DIM = 64
SGU_CHUNK = 128
Q_BLOCK = 128
D_FF = 4 * D_MODEL
EPS = 1e-6

SB_W = SB_HEADS * HEAD_DIM
FOX_W = FOX_HEADS * HEAD_DIM
SGU_W = SGU_GROUPS * SGU_DIM
MIX_W = SB_W + FOX_W + SGU_W
IN_SIZES = [SB_W, SB_W, SB_W, FOX_W, FOX_W, FOX_W, FOX_HEADS, SGU_W, SGU_W]
IN_W = sum(IN_SIZES)
IN_SPLITS = [int(s) for s in np.cumsum(IN_SIZES)[:-1]]

kernel_name = "hybrid_sb_fox_sgu_adaln"


def rmsnorm(x, g):
    xf = x.astype(jnp.float32)
    y = xf * lax.rsqrt(jnp.mean(xf * xf, axis=-1, keepdims=True) + EPS)
    return (y * g).astype(x.dtype)


def modulate(h, shift, scale):
    return h * (1.0 + scale[:, None, :]) + shift[:, None, :]


def split_heads(t, n_heads):
    b, s, _ = t.shape
    return t.reshape(b, s, n_heads, -1).transpose(0, 2, 1, 3)


def merge_heads(t):
    b, h, s, d = t.shape
    return t.transpose(0, 2, 1, 3).reshape(b, s, h * d)


def to_blocks(t):
    b, h, s, d = t.shape
    return jnp.moveaxis(t.reshape(b, h, s // Q_BLOCK, Q_BLOCK, d), 2, 0)


def from_blocks(o):
    n, b, h, q, d = o.shape
    return jnp.moveaxis(o, 0, 2).reshape(b, h, n * q, d)


def stick_breaking_attention(q, k, v):
    s_len = k.shape[2]
    scale = HEAD_DIM ** -0.5
    s_pos = jnp.arange(s_len)

    def block(args):
        qb, i = args
        z = jnp.einsum('bhqd,bhsd->bhqs', qb, k).astype(jnp.float32) * scale
        t_pos = i * Q_BLOCK + jnp.arange(Q_BLOCK)
        mask = s_pos[None, :] < t_pos[:, None]
        log_1mb = jnp.where(mask, jax.nn.log_sigmoid(-z), 0.0)
        between = lax.cumsum(log_1mb, axis=3, reverse=True) - log_1mb
        a = jnp.where(mask, jnp.exp(jax.nn.log_sigmoid(z) + between), 0.0)
        return jnp.einsum('bhqs,bhsd->bhqd', a.astype(v.dtype), v)

    o = lax.map(block, (to_blocks(q), jnp.arange(s_len // Q_BLOCK)))
    return from_blocks(o)


def forgetting_attention(q, k, v, log_f):
    b, h, s_len, _ = k.shape
    n_blk = s_len // Q_BLOCK
    scale = HEAD_DIM ** -0.5
    cum_f = jnp.cumsum(log_f, axis=-1)
    cum_f_blocks = jnp.moveaxis(cum_f.reshape(b, h, n_blk, Q_BLOCK), 2, 0)
    s_pos = jnp.arange(s_len)

    def block(args):
        qb, fq, i = args
        z = jnp.einsum('bhqd,bhsd->bhqs', qb, k).astype(jnp.float32) * scale
        z = z + fq[..., :, None] - cum_f[:, :, None, :]
        t_pos = i * Q_BLOCK + jnp.arange(Q_BLOCK)
        mask = s_pos[None, :] <= t_pos[:, None]
        p = jax.nn.softmax(jnp.where(mask, z, -jnp.inf), axis=-1)
        return jnp.einsum('bhqs,bhsd->bhqd', p.astype(v.dtype), v)

    o = lax.map(block, (to_blocks(q), cum_f_blocks, jnp.arange(n_blk)))
    return from_blocks(o)


def spatial_gating(u, vv, norm_g, w_s, b_s):
    b, s_len, _ = u.shape
    n_chunk = s_len // SGU_CHUNK
    vv = rmsnorm(vv.reshape(b, s_len, SGU_GROUPS, SGU_DIM), norm_g)
    vv = vv.reshape(b, n_chunk, SGU_CHUNK, SGU_GROUPS, SGU_DIM)
    w = jnp.tril(w_s)
    mixed = jnp.einsum('gts,bnsgd->bntgd', w, vv) + b_s.T[None, None, :, :, None]
    return u * mixed.reshape(b, s_len, SGU_W)


def _fwd_setup_inputs(seed: int = 0) -> dict:
    key = jax.random.key(seed)
    ks = jax.random.split(key, 16)
    nrm = jax.random.normal
    f32 = jnp.float32
    return {
        "x": nrm(ks[0], (BATCH, SEQ, D_MODEL), f32),
        "c": nrm(ks[1], (BATCH, D_MODEL), f32),
        "ada_w": nrm(ks[2], (DEPTH, D_MODEL, 6 * D_MODEL), f32) * (0.5 * D_MODEL ** -0.5),
        "ada_b": nrm(ks[3], (DEPTH, 6 * D_MODEL), f32) * 0.02,
        "norm1_g": 1.0 + 0.01 * nrm(ks[4], (DEPTH, D_MODEL), f32),
        "norm2_g": 1.0 + 0.01 * nrm(ks[5], (DEPTH, D_MODEL), f32),
        "w_in": nrm(ks[6], (DEPTH, D_MODEL, IN_W), f32) * D_MODEL ** -0.5,
        "b_forget": 2.0 + 0.5 * nrm(ks[7], (DEPTH, FOX_HEADS), f32),
        "q_norm_g": 1.0 + 0.01 * nrm(ks[8], (DEPTH, HEAD_DIM), f32),
        "k_norm_g": 1.0 + 0.01 * nrm(ks[9], (DEPTH, HEAD_DIM), f32),
        "sgu_norm_g": 1.0 + 0.01 * nrm(ks[10], (DEPTH, SGU_GROUPS, SGU_DIM), f32),
        "sgu_w": nrm(ks[11], (DEPTH, SGU_GROUPS, SGU_CHUNK, SGU_CHUNK), f32) * SGU_CHUNK ** -0.5,
        "sgu_b": 1.0 + 0.1 * nrm(ks[12], (DEPTH, SGU_GROUPS, SGU_CHUNK), f32),
        "w_out": nrm(ks[13], (DEPTH, MIX_W, D_MODEL), f32) * MIX_W ** -0.5,
        "mlp_w1": nrm(ks[14], (DEPTH, D_MODEL, D_FF), f32) * D_MODEL ** -0.5,
        "mlp_w2": nrm(ks[15], (DEPTH, D_FF, D_MODEL), f32) * D_FF ** -0.5,
    }


def _fwd_reference(x, c, ada_w, ada_b, norm1_g, norm2_g, w_in, b_forget, q_norm_g, k_norm_g,
              sgu_norm_g, sgu_w, sgu_b, w_out, mlp_w1, mlp_w2):
    cond = jax.nn.silu(c)
    for l in range(DEPTH):
        mod = cond @ ada_w[l] + ada_b[l]
        sh1, sc1, g1, sh2, sc2, g2 = jnp.split(mod, 6, axis=-1)

        h = modulate(rmsnorm(x, norm1_g[l]), sh1, sc1)
        proj = h @ w_in[l]
        qa, ka, va, qb, kb, vb, fl, uc, vc = jnp.split(proj, IN_SPLITS, axis=-1)

        o_sb = stick_breaking_attention(split_heads(qa, SB_HEADS), split_heads(ka, SB_HEADS),
                                        split_heads(va, SB_HEADS))

        q_fox = rmsnorm(split_heads(qb, FOX_HEADS), q_norm_g[l])
        k_fox = rmsnorm(split_heads(kb, FOX_HEADS), k_norm_g[l])
        log_f = jax.nn.log_sigmoid(fl.astype(jnp.float32) + b_forget[l].astype(jnp.float32))
        o_fox = forgetting_attention(q_fox, k_fox, split_heads(vb, FOX_HEADS),
                                     log_f.transpose(0, 2, 1))

        o_sgu = spatial_gating(jax.nn.gelu(uc), jax.nn.gelu(vc), sgu_norm_g[l], sgu_w[l], sgu_b[l])

        mixed = jnp.concatenate([merge_heads(o_sb), merge_heads(o_fox), o_sgu], axis=-1)
        x = x + g1[:, None, :] * (mixed @ w_out[l])

        h = modulate(rmsnorm(x, norm2_g[l]), sh2, sc2)
        x = x + g2[:, None, :] * (jnp.square(jax.nn.relu(h @ mlp_w1[l])) @ mlp_w2[l])
    return x


import jax as _jax
import jax.numpy as _jnp

TWIN_FORMAT = 'train_step'
FWD_PARAMS = ['x', 'c', 'ada_w', 'ada_b', 'norm1_g', 'norm2_g', 'w_in', 'b_forget', 'q_norm_g', 'k_norm_g', 'sgu_norm_g', 'sgu_w', 'sgu_b', 'w_out', 'mlp_w1', 'mlp_w2']
TWIN_WEIGHTS = ['ada_w', 'ada_b', 'norm1_g', 'norm2_g', 'w_in', 'b_forget', 'q_norm_g', 'k_norm_g', 'sgu_norm_g', 'sgu_w', 'sgu_b', 'w_out', 'mlp_w1', 'mlp_w2']
TWIN_DIFF_INPUT = 'x'
TWIN_INPUTS = ['x', 'c', 'ada_w', 'ada_b', 'norm1_g', 'norm2_g', 'w_in', 'b_forget', 'q_norm_g', 'k_norm_g', 'sgu_norm_g', 'sgu_w', 'sgu_b', 'w_out', 'mlp_w1', 'mlp_w2', 'loss_target', 'm_ada_w', 'm_ada_b', 'm_norm1_g', 'm_norm2_g', 'm_w_in', 'm_b_forget', 'm_q_norm_g', 'm_k_norm_g', 'm_sgu_norm_g', 'm_sgu_w', 'm_sgu_b', 'm_w_out', 'm_mlp_w1', 'm_mlp_w2', 'v_ada_w', 'v_ada_b', 'v_norm1_g', 'v_norm2_g', 'v_w_in', 'v_b_forget', 'v_q_norm_g', 'v_k_norm_g', 'v_sgu_norm_g', 'v_sgu_w', 'v_sgu_b', 'v_w_out', 'v_mlp_w1', 'v_mlp_w2']
TWIN_OUTPUTS = ['loss', 'grad_x', 'grad_ada_w', 'grad_ada_b', 'grad_norm1_g', 'grad_norm2_g', 'grad_w_in', 'grad_b_forget', 'grad_q_norm_g', 'grad_k_norm_g', 'grad_sgu_norm_g', 'grad_sgu_w', 'grad_sgu_b', 'grad_w_out', 'grad_mlp_w1', 'grad_mlp_w2', 'delta_ada_w', 'delta_ada_b', 'delta_norm1_g', 'delta_norm2_g', 'delta_w_in', 'delta_b_forget', 'delta_q_norm_g', 'delta_k_norm_g', 'delta_sgu_norm_g', 'delta_sgu_w', 'delta_sgu_b', 'delta_w_out', 'delta_mlp_w1', 'delta_mlp_w2', 'new_m_ada_w', 'new_m_ada_b', 'new_m_norm1_g', 'new_m_norm2_g', 'new_m_w_in', 'new_m_b_forget', 'new_m_q_norm_g', 'new_m_k_norm_g', 'new_m_sgu_norm_g', 'new_m_sgu_w', 'new_m_sgu_b', 'new_m_w_out', 'new_m_mlp_w1', 'new_m_mlp_w2', 'new_v_ada_w', 'new_v_ada_b', 'new_v_norm1_g', 'new_v_norm2_g', 'new_v_w_in', 'new_v_b_forget', 'new_v_q_norm_g', 'new_v_k_norm_g', 'new_v_sgu_norm_g', 'new_v_sgu_w', 'new_v_sgu_b', 'new_v_w_out', 'new_v_mlp_w1', 'new_v_mlp_w2']
TWIN_LEAF_KINDS = {'loss': 'loss', 'grad_x': 'grad_x', 'grad_ada_w': 'grad_w', 'grad_ada_b': 'grad_w', 'grad_norm1_g': 'grad_w', 'grad_norm2_g': 'grad_w', 'grad_w_in': 'grad_w', 'grad_b_forget': 'grad_w', 'grad_q_norm_g': 'grad_w', 'grad_k_norm_g': 'grad_w', 'grad_sgu_norm_g': 'grad_w', 'grad_sgu_w': 'grad_w', 'grad_sgu_b': 'grad_w', 'grad_w_out': 'grad_w', 'grad_mlp_w1': 'grad_w', 'grad_mlp_w2': 'grad_w', 'delta_ada_w': 'delta_w', 'delta_ada_b': 'delta_w', 'delta_norm1_g': 'delta_w', 'delta_norm2_g': 'delta_w', 'delta_w_in': 'delta_w', 'delta_b_forget': 'delta_w', 'delta_q_norm_g': 'delta_w', 'delta_k_norm_g': 'delta_w', 'delta_sgu_norm_g': 'delta_w', 'delta_sgu_w': 'delta_w', 'delta_sgu_b': 'delta_w', 'delta_w_out': 'delta_w', 'delta_mlp_w1': 'delta_w', 'delta_mlp_w2': 'delta_w', 'new_m_ada_w': 'new_m', 'new_m_ada_b': 'new_m', 'new_m_norm1_g': 'new_m', 'new_m_norm2_g': 'new_m', 'new_m_w_in': 'new_m', 'new_m_b_forget': 'new_m', 'new_m_q_norm_g': 'new_m', 'new_m_k_norm_g': 'new_m', 'new_m_sgu_norm_g': 'new_m', 'new_m_sgu_w': 'new_m', 'new_m_sgu_b': 'new_m', 'new_m_w_out': 'new_m', 'new_m_mlp_w1': 'new_m', 'new_m_mlp_w2': 'new_m', 'new_v_ada_w': 'new_v', 'new_v_ada_b': 'new_v', 'new_v_norm1_g': 'new_v', 'new_v_norm2_g': 'new_v', 'new_v_w_in': 'new_v', 'new_v_b_forget': 'new_v', 'new_v_q_norm_g': 'new_v', 'new_v_k_norm_g': 'new_v', 'new_v_sgu_norm_g': 'new_v', 'new_v_sgu_w': 'new_v', 'new_v_sgu_b': 'new_v', 'new_v_w_out': 'new_v', 'new_v_mlp_w1': 'new_v', 'new_v_mlp_w2': 'new_v'}


def _forward(args):
    return _fwd_reference(*[args[k] for k in FWD_PARAMS])


def _output_shape():
    def fwd():
        inp = _fwd_setup_inputs(0)
        return _fwd_reference(*[inp[k] for k in FWD_PARAMS])
    out = _jax.eval_shape(fwd)
    return out.shape, out.dtype

N_MICROBATCH = 1
ADAM_LR = 0.001
ADAM_B1 = 0.9
ADAM_B2 = 0.999
ADAM_EPS = 1e-08
ADAM_WD = 0.01
ADAM_STEP = 10
PER_EXAMPLE_BATCH_AXIS = {'x': 0, 'c': 0, 'loss_target': 0}
SHARED_INPUTS = []
_WEIGHT_DTYPES = {'ada_w': _jnp.float32, 'ada_b': _jnp.float32, 'norm1_g': _jnp.float32, 'norm2_g': _jnp.float32, 'w_in': _jnp.float32, 'b_forget': _jnp.float32, 'q_norm_g': _jnp.float32, 'k_norm_g': _jnp.float32, 'sgu_norm_g': _jnp.float32, 'sgu_w': _jnp.float32, 'sgu_b': _jnp.float32, 'w_out': _jnp.float32, 'mlp_w1': _jnp.float32, 'mlp_w2': _jnp.float32}
MOMENT_SCALE = {'ada_w': 5.757663e+00, 'ada_b': 1.389147e+01, 'norm1_g': 3.153071e+00, 'norm2_g': 2.402808e+01, 'w_in': 8.973729e-01, 'b_forget': 2.029083e+01, 'q_norm_g': 2.468306e+00, 'k_norm_g': 2.469502e+00, 'sgu_norm_g': 1.556207e+00, 'sgu_w': 4.719101e-01, 'sgu_b': 1.558490e+00, 'w_out': 1.594074e+00, 'mlp_w1': 9.787008e-01, 'mlp_w2': 3.676971e+00}


def _to_microbatches(a, axis):
    t = _jnp.moveaxis(a, axis, 0)
    t = t.reshape((N_MICROBATCH, t.shape[0] // N_MICROBATCH) + t.shape[1:])
    return _jnp.moveaxis(t, 1, axis + 1)


def setup_inputs(seed: int = 0) -> dict:
    inp = _fwd_setup_inputs(seed)
    key = _jax.random.fold_in(_jax.random.key(seed), 7919)
    shape, _ = _output_shape()
    out = dict(inp)
    out["loss_target"] = _jax.random.normal(_jax.random.fold_in(key, 0), shape, _jnp.float32)
    for i, name in enumerate(TWIN_WEIGHTS):
        w = inp[name].astype(_jnp.float32)
        if MOMENT_SCALE is None:
            s = _jnp.sqrt(_jnp.mean(_jnp.square(w)) + 1e-30)
        else:
            s = MOMENT_SCALE[name]
        km, kv = _jax.random.split(_jax.random.fold_in(key, i + 1))
        out[name] = w
        out["m_" + name] = s * _jax.random.normal(km, w.shape, _jnp.float32)
        out["v_" + name] = (s * s) * _jax.random.uniform(kv, w.shape, _jnp.float32, 0.5, 1.5)
    if N_MICROBATCH > 1:
        for name, axis in PER_EXAMPLE_BATCH_AXIS.items():
            out[name] = _to_microbatches(out[name], axis)
    return {'x': out['x'], 'c': out['c'], 'ada_w': out['ada_w'], 'ada_b': out['ada_b'], 'norm1_g': out['norm1_g'], 'norm2_g': out['norm2_g'], 'w_in': out['w_in'], 'b_forget': out['b_forget'], 'q_norm_g': out['q_norm_g'], 'k_norm_g': out['k_norm_g'], 'sgu_norm_g': out['sgu_norm_g'], 'sgu_w': out['sgu_w'], 'sgu_b': out['sgu_b'], 'w_out': out['w_out'], 'mlp_w1': out['mlp_w1'], 'mlp_w2': out['mlp_w2'], 'loss_target': out['loss_target'], 'm_ada_w': out['m_ada_w'], 'm_ada_b': out['m_ada_b'], 'm_norm1_g': out['m_norm1_g'], 'm_norm2_g': out['m_norm2_g'], 'm_w_in': out['m_w_in'], 'm_b_forget': out['m_b_forget'], 'm_q_norm_g': out['m_q_norm_g'], 'm_k_norm_g': out['m_k_norm_g'], 'm_sgu_norm_g': out['m_sgu_norm_g'], 'm_sgu_w': out['m_sgu_w'], 'm_sgu_b': out['m_sgu_b'], 'm_w_out': out['m_w_out'], 'm_mlp_w1': out['m_mlp_w1'], 'm_mlp_w2': out['m_mlp_w2'], 'v_ada_w': out['v_ada_w'], 'v_ada_b': out['v_ada_b'], 'v_norm1_g': out['v_norm1_g'], 'v_norm2_g': out['v_norm2_g'], 'v_w_in': out['v_w_in'], 'v_b_forget': out['v_b_forget'], 'v_q_norm_g': out['v_q_norm_g'], 'v_k_norm_g': out['v_k_norm_g'], 'v_sgu_norm_g': out['v_sgu_norm_g'], 'v_sgu_w': out['v_sgu_w'], 'v_sgu_b': out['v_sgu_b'], 'v_w_out': out['v_w_out'], 'v_mlp_w1': out['v_mlp_w1'], 'v_mlp_w2': out['v_mlp_w2']}


def _loss(weights, diff, rest, loss_target):
    with _jax.named_scope("forward"):
        args = {**rest, TWIN_DIFF_INPUT: diff, **{k: w.astype(_WEIGHT_DTYPES[k]) for k, w in weights.items()}}
        y = _forward(args)
    with _jax.named_scope("loss_head"):
        err = _jnp.square(y.astype(_jnp.float32) - loss_target)
        return 0.5 * _jnp.sum(_jnp.mean(err, axis=-1)) if err.ndim else 0.5 * err


def _adamw(w, g, m, v):
    m = ADAM_B1 * m + (1.0 - ADAM_B1) * g
    v = ADAM_B2 * v + (1.0 - ADAM_B2) * _jnp.square(g)
    m_hat = m / (1.0 - ADAM_B1 ** ADAM_STEP)
    v_hat = v / (1.0 - ADAM_B2 ** ADAM_STEP)
    delta = -ADAM_LR * (m_hat / (_jnp.sqrt(v_hat) + ADAM_EPS) + ADAM_WD * w)
    return delta, m, v


def reference(x, c, ada_w, ada_b, norm1_g, norm2_g, w_in, b_forget, q_norm_g, k_norm_g, sgu_norm_g, sgu_w, sgu_b, w_out, mlp_w1, mlp_w2, loss_target, m_ada_w, m_ada_b, m_norm1_g, m_norm2_g, m_w_in, m_b_forget, m_q_norm_g, m_k_norm_g, m_sgu_norm_g, m_sgu_w, m_sgu_b, m_w_out, m_mlp_w1, m_mlp_w2, v_ada_w, v_ada_b, v_norm1_g, v_norm2_g, v_w_in, v_b_forget, v_q_norm_g, v_k_norm_g, v_sgu_norm_g, v_sgu_w, v_sgu_b, v_w_out, v_mlp_w1, v_mlp_w2):
    given = dict(x=x, c=c, ada_w=ada_w, ada_b=ada_b, norm1_g=norm1_g, norm2_g=norm2_g, w_in=w_in, b_forget=b_forget, q_norm_g=q_norm_g, k_norm_g=k_norm_g, sgu_norm_g=sgu_norm_g, sgu_w=sgu_w, sgu_b=sgu_b, w_out=w_out, mlp_w1=mlp_w1, mlp_w2=mlp_w2, loss_target=loss_target, m_ada_w=m_ada_w, m_ada_b=m_ada_b, m_norm1_g=m_norm1_g, m_norm2_g=m_norm2_g, m_w_in=m_w_in, m_b_forget=m_b_forget, m_q_norm_g=m_q_norm_g, m_k_norm_g=m_k_norm_g, m_sgu_norm_g=m_sgu_norm_g, m_sgu_w=m_sgu_w, m_sgu_b=m_sgu_b, m_w_out=m_w_out, m_mlp_w1=m_mlp_w1, m_mlp_w2=m_mlp_w2, v_ada_w=v_ada_w, v_ada_b=v_ada_b, v_norm1_g=v_norm1_g, v_norm2_g=v_norm2_g, v_w_in=v_w_in, v_b_forget=v_b_forget, v_q_norm_g=v_q_norm_g, v_k_norm_g=v_k_norm_g, v_sgu_norm_g=v_sgu_norm_g, v_sgu_w=v_sgu_w, v_sgu_b=v_sgu_b, v_w_out=v_w_out, v_mlp_w1=v_mlp_w1, v_mlp_w2=v_mlp_w2)
    weights = {n: given[n] for n in TWIN_WEIGHTS}
    shared = {n: given[n] for n in SHARED_INPUTS}
    per_example = {n: given[n] for n in ['x', 'c']}
    grad_fn = _jax.value_and_grad(_loss, argnums=(0, 1))

    def one_microbatch(ex, loss_target):
        ex = dict(ex)
        diff = ex.pop(TWIN_DIFF_INPUT)
        return grad_fn(weights, diff, {**shared, **ex}, loss_target)

    if N_MICROBATCH == 1:
        loss, (grad_w, grad_x) = one_microbatch(per_example, given["loss_target"])
    else:
        def body(carry, xs):
            loss_sum, grad_sum = carry
            l_k, (gw_k, gx_k) = one_microbatch(xs[0], xs[1])
            with _jax.named_scope("update"):
                return (loss_sum + l_k, _jax.tree.map(_jnp.add, grad_sum, gw_k)), gx_k

        init = (_jnp.zeros((), _jnp.float32), _jax.tree.map(_jnp.zeros_like, weights))
        (loss, grad_w), grad_x = _jax.lax.scan(body, init, (per_example, given["loss_target"]))
    with _jax.named_scope("update"):
        delta_w, new_m, new_v = {}, {}, {}
        for n in TWIN_WEIGHTS:
            delta_w[n], new_m[n], new_v[n] = _adamw(weights[n], grad_w[n], given["m_" + n], given["v_" + n])
    return (loss, grad_x, *[grad_w[n] for n in TWIN_WEIGHTS], *[delta_w[n] for n in TWIN_WEIGHTS],
            *[new_m[n] for n in TWIN_WEIGHTS], *[new_v[n] for n in TWIN_WEIGHTS])
```

```python
import jax
import jax.numpy as jnp
from jax import lax
from jax.experimental import pallas as pl
from jax.experimental.pallas import tpu as pltpu

F32 = jnp.float32
_BF = jnp.bfloat16
_XBF = jnp.bfloat16

N_DEV = 8
D_MODEL = 1024
DEPTH = 4
HEAD_DIM = 64
SB_HEADS = 4
FOX_HEADS = 8
SGU_GROUPS = 4
SGU_CHUNK = 128
D_FF = 4096
EPS = 1e-6
IN_W = 2824
FL_SRC = 2304
UC_SRC = 2312

C_QB, C_KB, C_VB = 0, 512, 1024
C_QA, C_KA, C_VA = 1536, 1792, 2048
C_UC, C_VC = 2304, 2560
C_FL = 2816
IN_P = 3072
M_SB, M_FOX, M_SGU = 0, 256, 768

ADAM_LR = 0.001
ADAM_B1 = 0.9
ADAM_B2 = 0.999
ADAM_EPS = 1e-08
ADAM_WD = 0.01
ADAM_STEP = 10

ATT_T = 256
ROW_T = 512
MM_T = 1024
NEG = -1e30

MESH = pl.DeviceIdType.MESH
ARB = pltpu.ARBITRARY
PAR = pltpu.PARALLEL
HIGHEST = lax.Precision.HIGHEST


def _pcall(body, **kw):
    return pl.pallas_call(body, **kw)


def _cp(*sem):
    return pltpu.CompilerParams(dimension_semantics=tuple(sem))


def _dot(a, b, ca, cb):
    return lax.dot_general(a, b, (((ca,), (cb,)), ((), ())), preferred_element_type=F32)


def _dot_f32(a, b, ca, cb):
    return lax.dot_general(a, b, (((ca,), (cb,)), ((), ())), precision=HIGHEST, preferred_element_type=F32)


def _split(v, parts):
    out = []
    r = v
    for _ in range(parts - 1):
        p = r.astype(_BF)
        out.append(p)
        r = r - p.astype(F32)
    out.append(r.astype(_BF))
    return out


def _dot_exact(v, m, ca, cb, parts, v_left=True):
    out = None
    for p in _split(v, parts):
        term = _dot(p, m, ca, cb) if v_left else _dot(m, p, ca, cb)
        out = term if out is None else out + term
    return out


def _iota2(shape, dim):
    return lax.broadcasted_iota(jnp.int32, shape, dim)


def _my_pos():
    return lax.axis_index("x"), lax.axis_index("y"), lax.axis_index("c")


def _flip(pos, k):
    x, y, c = pos
    px = 1 - x if (k >> 2) & 1 else x
    py = 1 - y if (k >> 1) & 1 else y
    pc = 1 - c if k & 1 else c
    return px, py, pc


def _lin(pos):
    return 4 * pos[0] + 2 * pos[1] + pos[2]


def _exchange(arrays, scatter, name):
    n = len(arrays)
    if scatter:
        out_shapes = [jax.ShapeDtypeStruct(a.shape, a.dtype) for a in arrays]
    else:
        out_shapes = [jax.ShapeDtypeStruct((N_DEV,) + a.shape, a.dtype) for a in arrays]

    def body(*refs):
        ins, outs = refs[:n], refs[n:2 * n]
        send_sems, recv_sems, local_sems = refs[2 * n:]
        me = _my_pos()
        me_i = _lin(me)
        local = []
        for a in range(n):
            src = ins[a].at[me_i] if scatter else ins[a]
            cp = pltpu.make_async_copy(src, outs[a].at[me_i], local_sems.at[a])
            cp.start()
            local.append(cp)
        remote = []
        for a in range(n):
            for k in range(1, N_DEV):
                peer = _flip(me, k)
                src = ins[a].at[_lin(peer)] if scatter else ins[a]
                cp = pltpu.make_async_remote_copy(
                    src_ref=src, dst_ref=outs[a].at[me_i],
                    send_sem=send_sems.at[a, k - 1], recv_sem=recv_sems.at[a, k - 1],
                    device_id=peer, device_id_type=MESH)
                cp.start()
                remote.append((a, k, cp))
        for a, k, cp in remote:
            cp.wait_send()
            peer = _flip(me, k)
            src = ins[a].at[me_i] if scatter else ins[a]
            pltpu.make_async_remote_copy(
                src_ref=src, dst_ref=outs[a].at[_lin(peer)],
                send_sem=send_sems.at[a, k - 1], recv_sem=recv_sems.at[a, k - 1],
                device_id=peer, device_id_type=MESH).wait_recv()
        for cp in local:
            cp.wait()

    any_spec = pl.BlockSpec(memory_space=pl.ANY)
    outs = _pcall(
        body, name=name, out_shape=out_shapes,
        in_specs=[any_spec] * n, out_specs=[any_spec] * n,
        scratch_shapes=[pltpu.SemaphoreType.DMA((n, N_DEV - 1)), pltpu.SemaphoreType.DMA((n, N_DEV - 1)),
                        pltpu.SemaphoreType.DMA((n,))],
    )(*arrays)
    return list(outs)


def _row_spec(t, w, col=0):
    return pl.BlockSpec((t, w), lambda i: (i, col))


def _vec_spec(w, col=0):
    return pl.BlockSpec((1, w), lambda i: (0, col))


def _norm_mod(x, g, sc, sh, name):
    s, d = x.shape
    t = min(ROW_T, s)

    def body(x_ref, g_ref, sc_ref, sh_ref, h_ref):
        xf = x_ref[...]
        rstd = lax.rsqrt(jnp.mean(xf * xf, axis=-1, keepdims=True) + EPS)
        y = xf * rstd * g_ref[...]
        h_ref[...] = (y * (1.0 + sc_ref[...]) + sh_ref[...]).astype(h_ref.dtype)

    return _pcall(
        body, name=name, grid=(s // t,), out_shape=jax.ShapeDtypeStruct((s, d), _BF),
        in_specs=[_row_spec(t, d), _vec_spec(d), _vec_spec(d), _vec_spec(d)],
        out_specs=_row_spec(t, d), compiler_params=_cp(PAR),
    )(x, g, sc, sh)


def _norm_mod_bwd(x, dh, dres, g, sc, name):
    s, d = x.shape
    t = min(ROW_T, s)

    def body(x_ref, dh_ref, dres_ref, g_ref, sc_ref, dx_ref, dsh_ref, dsc_ref, dg_ref):
        i = pl.program_id(0)
        xf = x_ref[...]
        dh = dh_ref[...]
        gv = g_ref[...]
        rstd = lax.rsqrt(jnp.mean(xf * xf, axis=-1, keepdims=True) + EPS)
        xhat = xf * rstd
        dn = dh * (1.0 + sc_ref[...])
        dxh = dn * gv
        dx_ref[...] = dres_ref[...] + rstd * (dxh - xhat * jnp.mean(dxh * xhat, axis=-1, keepdims=True))

        @pl.when(i == 0)
        def _():
            dsh_ref[...] = jnp.zeros_like(dsh_ref)
            dsc_ref[...] = jnp.zeros_like(dsc_ref)
            dg_ref[...] = jnp.zeros_like(dg_ref)

        dsh_ref[...] += jnp.sum(dh, axis=0, keepdims=True)
        dsc_ref[...] += jnp.sum(dh * (xhat * gv), axis=0, keepdims=True)
        dg_ref[...] += jnp.sum(dn * xhat, axis=0, keepdims=True)

    vec = jax.ShapeDtypeStruct((1, d), F32)
    return _pcall(
        body, name=name, grid=(s // t,),
        out_shape=[jax.ShapeDtypeStruct((s, d), F32), vec, vec, vec],
        in_specs=[_row_spec(t, d), _row_spec(t, d), _row_spec(t, d), _vec_spec(d), _vec_spec(d)],
        out_specs=[_row_spec(t, d), _vec_spec(d), _vec_spec(d), _vec_spec(d)],
        compiler_params=_cp(ARB),
    )(x, dh, dres, g, sc)


def _res_bwd(dx, y, gate, name):
    s, d = dx.shape
    t = min(ROW_T, s)

    def body(dx_ref, y_ref, g_ref, dy_ref, dg_ref):
        i = pl.program_id(0)
        dxv = dx_ref[...]
        dy_ref[...] = (dxv * g_ref[...]).astype(dy_ref.dtype)

        @pl.when(i == 0)
        def _():
            dg_ref[...] = jnp.zeros_like(dg_ref)

        dg_ref[...] += jnp.sum(dxv * y_ref[...], axis=0, keepdims=True)

    return _pcall(
        body, name=name, grid=(s // t,),
        out_shape=[jax.ShapeDtypeStruct((s, d), _BF), jax.ShapeDtypeStruct((1, d), F32)],
        in_specs=[_row_spec(t, d), _row_spec(t, d), _vec_spec(d)],
        out_specs=[_row_spec(t, d), _vec_spec(d)], compiler_params=_cp(ARB),
    )(dx, y, gate)


def _loss_head(y, target, name):
    s, d = y.shape
    t = min(ROW_T, s)

    def body(y_ref, t_ref, loss_ref, dy_ref):
        i = pl.program_id(0)
        diff = y_ref[...] - t_ref[...]
        dy_ref[...] = diff * (1.0 / d)

        @pl.when(i == 0)
        def _():
            loss_ref[...] = jnp.zeros_like(loss_ref)

        rows = jnp.sum(diff * diff, axis=-1, keepdims=True)
        loss_ref[...] += (0.5 / d) * jnp.sum(rows, axis=0, keepdims=True)

    return _pcall(
        body, name=name, grid=(s // t,),
        out_shape=[jax.ShapeDtypeStruct((1, 1), F32), jax.ShapeDtypeStruct((s, d), F32)],
        in_specs=[_row_spec(t, d), _row_spec(t, d)],
        out_specs=[pl.BlockSpec((1, 1), lambda i: (0, 0)), _row_spec(t, d)], compiler_params=_cp(ARB),
    )(y, target)


def _mm_nn(a, b, epi, extras, name):
    m, kdim = a.shape
    n = b.shape[1]
    tm, tn, tk = min(MM_T, m), min(MM_T, n), min(MM_T, kdim)
    nk = kdim // tk
    n_extra = len(extras)
    n_out = 1 if epi == "plain" else 2

    def finish(y, extra_refs, out_refs):
        if epi == "plain":
            out_refs[0][...] = y.astype(out_refs[0].dtype)
        elif epi == "resid":
            x_ref, g_ref = extra_refs
            out_refs[0][...] = x_ref[...] + g_ref[...] * y
            out_refs[1][...] = y
        else:
            r = jnp.maximum(y, 0.0)
            out_refs[0][...] = (r * r).astype(out_refs[0].dtype)
            out_refs[1][...] = r.astype(out_refs[1].dtype)

    def body(a_ref, b_ref, *rest):
        extra_refs = rest[:n_extra]
        out_refs = rest[n_extra:n_extra + n_out]
        part = _dot(a_ref[...].astype(_BF), b_ref[...].astype(_BF), 1, 0)
        if nk == 1:
            finish(part, extra_refs, out_refs)
        else:
            acc_ref = rest[-1]
            k = pl.program_id(2)

            @pl.when(k == 0)
            def _():
                acc_ref[...] = part

            @pl.when(k > 0)
            def _():
                acc_ref[...] += part

            @pl.when(k == nk - 1)
            def _():
                finish(acc_ref[...], extra_refs, out_refs)

    tile = pl.BlockSpec((tm, tn), lambda i, j, k: (i, j))
    in_specs = [pl.BlockSpec((tm, tk), lambda i, j, k: (i, k)), pl.BlockSpec((tk, tn), lambda i, j, k: (k, j))]
    if epi == "plain":
        out_shape = [jax.ShapeDtypeStruct((m, n), F32)]
    elif epi == "resid":
        in_specs += [tile, pl.BlockSpec((1, tn), lambda i, j, k: (0, j))]
        out_shape = [jax.ShapeDtypeStruct((m, n), F32)] * 2
    else:
        out_shape = [jax.ShapeDtypeStruct((m, n), _BF), jax.ShapeDtypeStruct((m, n), F32)]
    outs = _pcall(
        body, name=name, grid=(m // tm, n // tn, nk), out_shape=out_shape,
        in_specs=in_specs, out_specs=[tile] * n_out,
        scratch_shapes=[pltpu.VMEM((tm, tn), F32)] if nk > 1 else [],
        compiler_params=_cp(PAR, PAR, ARB),
    )(a, b, *extras)
    return outs[0] if n_out == 1 else outs


def _mm_nt(a, b, epi, extras, out_dtype, name):
    m, kdim = a.shape
    n = b.shape[0]
    tm, tn, tk = min(MM_T, m), min(MM_T, n), min(MM_T, kdim)
    nk = kdim // tk
    n_extra = len(extras)

    def finish(y, extra_refs, o_ref):
        if epi == "mul2":
            y = y * (2.0 * extra_refs[0][...].astype(F32))
        o_ref[...] = y.astype(o_ref.dtype)

    def body(a_ref, b_ref, *rest):
        extra_refs = rest[:n_extra]
        o_ref = rest[n_extra]
        part = _dot(a_ref[...].astype(_BF), b_ref[...].astype(_BF), 1, 1)
        if nk == 1:
            finish(part, extra_refs, o_ref)
        else:
            acc_ref = rest[-1]
            k = pl.program_id(2)

            @pl.when(k == 0)
            def _():
                acc_ref[...] = part

            @pl.when(k > 0)
            def _():
                acc_ref[...] += part

            @pl.when(k == nk - 1)
            def _():
                finish(acc_ref[...], extra_refs, o_ref)

    tile = pl.BlockSpec((tm, tn), lambda i, j, k: (i, j))
    in_specs = [pl.BlockSpec((tm, tk), lambda i, j, k: (i, k)), pl.BlockSpec((tn, tk), lambda i, j, k: (j, k))]
    in_specs += [tile] * n_extra
    return _pcall(
        body, name=name, grid=(m // tm, n // tn, nk), out_shape=jax.ShapeDtypeStruct((m, n), out_dtype),
        in_specs=in_specs, out_specs=tile,
        scratch_shapes=[pltpu.VMEM((tm, tn), F32)] if nk > 1 else [],
        compiler_params=_cp(PAR, PAR, ARB),
    )(a, b, *extras)


def _mm_tn(a, b, split, name):
    s, m = a.shape
    n = b.shape[1]
    ts = min(MM_T, s)
    ns = s // ts
    if split == "rows":
        tm, tn = min(MM_T, m // N_DEV), min(MM_T, n)
        per = (m // N_DEV) // tm
        out_shape = (N_DEV, m // N_DEV, n)
        out_spec = pl.BlockSpec((None, tm, tn), lambda i, j, k: (i // per, i % per, j))
    elif split == "cols":
        tm, tn = min(MM_T, m), min(MM_T, n // N_DEV)
        per = (n // N_DEV) // tn
        out_shape = (N_DEV, m, n // N_DEV)
        out_spec = pl.BlockSpec((None, tm, tn), lambda i, j, k: (j // per, i, j % per))
    else:
        tm, tn = min(MM_T, m), min(MM_T, n)
        out_shape = (m, n)
        out_spec = pl.BlockSpec((tm, tn), lambda i, j, k: (i, j))

    def body(a_ref, b_ref, o_ref, acc_ref):
        k = pl.program_id(2)
        part = _dot(a_ref[...].astype(_BF), b_ref[...].astype(_BF), 0, 0)

        @pl.when(k == 0)
        def _():
            acc_ref[...] = part

        @pl.when(k > 0)
        def _():
            acc_ref[...] += part

        @pl.when(k == ns - 1)
        def _():
            o_ref[...] = acc_ref[...].astype(o_ref.dtype)

    return _pcall(
        body, name=name, grid=(m // tm, n // tn, ns), out_shape=jax.ShapeDtypeStruct(out_shape, _XBF),
        in_specs=[pl.BlockSpec((ts, tm), lambda i, j, k: (k, i)), pl.BlockSpec((ts, tn), lambda i, j, k: (k, j))],
        out_specs=out_spec, scratch_shapes=[pltpu.VMEM((tm, tn), F32)],
        compiler_params=_cp(PAR, PAR, ARB),
    )(a, b)


def _head_masks():
    lane = _iota2((1, 128), 1)
    return [(lane // HEAD_DIM) == hh for hh in range(2)]


def _softplus_parts(z):
    sp = jnp.maximum(z, 0.0) + jnp.log(1.0 + jnp.exp(-jnp.abs(z)))
    return -sp, z - sp


def _sb_fwd(proj, name):
    s = proj.shape[0]
    t = min(ATT_T, s)
    nblk = s // t
    scale = HEAD_DIM ** -0.5

    def body(q_ref, k_ref, v_ref, o_ref):
        i = pl.program_id(1)
        masks = _head_masks()
        row = _iota2((t, t), 0)
        col = _iota2((t, t), 1)
        u_strict = (row > col).astype(_BF)
        causal = col < row
        qs = q_ref[...] * scale
        out = jnp.zeros((t, 128), F32)
        for hh in range(2):
            qm = jnp.where(masks[hh], qs, 0.0).astype(_BF)

            def block(j, carry, diag):
                run, o = carry
                off = pl.multiple_of(j * t, t)
                kb = k_ref[pl.ds(off, t), :].astype(_BF)
                vb = v_ref[pl.ds(off, t), :].astype(_BF)
                z = _dot(qm, kb, 1, 1)
                l1, lb = _softplus_parts(z)
                if diag:
                    l1 = jnp.where(causal, l1, 0.0)
                between = run + _dot_exact(l1, u_strict, 1, 0, 2)
                a = jnp.exp(lb + between)
                if diag:
                    a = jnp.where(causal, a, 0.0)
                o = o + _dot_exact(a, vb, 1, 0, 2)
                run = run + jnp.sum(l1, axis=-1, keepdims=True)
                return run, o

            carry = block(i, (jnp.zeros((t, 1), F32), jnp.zeros((t, 128), F32)), True)
            carry = lax.fori_loop(1, i + 1, lambda jj, c: block(i - jj, c, False), carry)
            out = jnp.where(masks[hh], carry[1], out)
        o_ref[...] = out

    kv_spec = lambda c0: pl.BlockSpec((s, 128), lambda p, i: (0, c0 // 128 + p))
    return _pcall(
        body, name=name, grid=(SB_HEADS // 2, nblk), out_shape=jax.ShapeDtypeStruct((s, 256), F32),
        in_specs=[pl.BlockSpec((t, 128), lambda p, i: (i, C_QA // 128 + p)), kv_spec(C_KA), kv_spec(C_VA)],
        out_specs=pl.BlockSpec((t, 128), lambda p, i: (i, p)), compiler_params=_cp(PAR, ARB),
    )(proj, proj, proj)


def _sb_bwd(proj, o, dmix, name):
    s = proj.shape[0]
    t = min(ATT_T, s)
    nblk = s // t
    scale = HEAD_DIM ** -0.5

    def body(q_ref, k_ref, v_ref, o_ref, do_ref, dq_ref, dk_ref, dv_ref):
        i = pl.program_id(1)
        masks = _head_masks()
        row = _iota2((t, t), 0)
        col = _iota2((t, t), 1)
        u_strict = (row > col).astype(_BF)
        u_incl = (row >= col).astype(_BF)
        causal = col < row

        @pl.when(i == 0)
        def _():
            dk_ref[...] = jnp.zeros_like(dk_ref)
            dv_ref[...] = jnp.zeros_like(dv_ref)

        qs = q_ref[...] * scale
        dov = do_ref[...]
        ov = o_ref[...]
        dq_out = jnp.zeros((t, 128), F32)
        for hh in range(2):
            qm = jnp.where(masks[hh], qs, 0.0).astype(_BF)
            dom = jnp.where(masks[hh], dov, 0.0).astype(_BF)
            dsum = jnp.sum(dom.astype(F32) * ov, axis=-1, keepdims=True)

            def block(j, carry, diag):
                run, erun, dq = carry
                off = pl.multiple_of(j * t, t)
                kb = k_ref[pl.ds(off, t), :].astype(_BF)
                vb = v_ref[pl.ds(off, t), :].astype(_BF)
                z = _dot(qm, kb, 1, 1)
                l1, lb = _softplus_parts(z)
                if diag:
                    l1 = jnp.where(causal, l1, 0.0)
                between = run + _dot_exact(l1, u_strict, 1, 0, 2)
                a = jnp.exp(lb + between)
                if diag:
                    a = jnp.where(causal, a, 0.0)
                g = a * _dot(dom, vb, 1, 1)
                cum = dsum - (erun + _dot_exact(g, u_incl, 1, 0, 2))
                beta = jnp.exp(lb)
                dz = g * (1.0 - beta) - cum * beta
                if diag:
                    dz = jnp.where(causal, dz, 0.0)
                dzb = dz.astype(_BF)
                dq = dq + _dot(dzb, kb, 1, 0)
                dk_ref[pl.ds(off, t), :] += _dot(dzb, qm, 0, 0)
                dv_ref[pl.ds(off, t), :] += _dot(a.astype(_BF), dom, 0, 0)
                run = run + jnp.sum(l1, axis=-1, keepdims=True)
                erun = erun + jnp.sum(g, axis=-1, keepdims=True)
                return run, erun, dq

            zero = jnp.zeros((t, 1), F32)
            carry = block(i, (zero, zero, jnp.zeros((t, 128), F32)), True)
            carry = lax.fori_loop(1, i + 1, lambda jj, c: block(i - jj, c, False), carry)
            dq_out = jnp.where(masks[hh], carry[2] * scale, dq_out)
        dq_ref[...] = dq_out

    kv_spec = lambda c0: pl.BlockSpec((s, 128), lambda p, i: (0, c0 // 128 + p))
    blk = lambda c0: pl.BlockSpec((t, 128), lambda p, i: (i, c0 // 128 + p))
    acc = pl.BlockSpec((s, 128), lambda p, i: (0, p))
    shp = jax.ShapeDtypeStruct((s, 256), F32)
    return _pcall(
        body, name=name, grid=(SB_HEADS // 2, nblk), out_shape=[shp, shp, shp],
        in_specs=[blk(C_QA), kv_spec(C_KA), kv_spec(C_VA), blk(0), blk(M_SB)],
        out_specs=[blk(0), acc, acc], compiler_params=_cp(PAR, ARB),
    )(proj, proj, proj, o, dmix)


def _group_matrix(width):
    r = _iota2((width, width), 0) // HEAD_DIM
    c = _iota2((width, width), 1) // HEAD_DIM
    return (r == c).astype(_BF)


def _group_mean(v, gm):
    return _dot_exact(v, gm, 1, 0, 2) * (1.0 / HEAD_DIM)


def _fox_prep(proj, gq, gk, bf, name):
    s = proj.shape[0]
    t = min(ATT_T, s)
    nblk = s // t

    def body(q_ref, k_ref, fl_ref, gq_ref, gk_ref, bf_ref, qn_ref, kn_ref, cfc_ref, cfr_ref, carry_ref):
        i = pl.program_id(0)
        gm = _group_matrix(512)
        for src, gref, dst in ((q_ref, gq_ref, qn_ref), (k_ref, gk_ref, kn_ref)):
            v = src[...]
            rstd = lax.rsqrt(_group_mean(v * v, gm) + EPS)
            dst[...] = (v * rstd * gref[...]).astype(dst.dtype)

        @pl.when(i == 0)
        def _():
            carry_ref[...] = jnp.zeros_like(carry_ref)

        lane = _iota2((1, 128), 1)
        logf = jnp.where(lane < FOX_HEADS, _softplus_parts(-(fl_ref[...] + bf_ref[...]))[0], 0.0)
        lower = (_iota2((t, t), 0) >= _iota2((t, t), 1)).astype(_BF)
        cf = carry_ref[...] + _dot_exact(logf, lower, 1, 0, 3, v_left=False)
        cfc_ref[...] = cf
        cfr_ref[...] = cf.T[:FOX_HEADS, :]
        carry_ref[...] = cf[t - 1:t, :]

    col = lambda w, c0: pl.BlockSpec((t, w), lambda i: (i, c0 // w))
    v512 = pl.BlockSpec((1, 512), lambda i: (0, 0))
    return _pcall(
        body, name=name, grid=(nblk,),
        out_shape=[jax.ShapeDtypeStruct((s, 512), _BF), jax.ShapeDtypeStruct((s, 512), _BF),
                   jax.ShapeDtypeStruct((s, 128), F32), jax.ShapeDtypeStruct((FOX_HEADS, s), F32)],
        in_specs=[col(512, C_QB), col(512, C_KB), col(128, C_FL), v512, v512, pl.BlockSpec((1, 128), lambda i: (0, 0))],
        out_specs=[col(512, 0), col(512, 0), col(128, 0), pl.BlockSpec((FOX_HEADS, t), lambda i: (0, i))],
        scratch_shapes=[pltpu.VMEM((1, 128), F32)], compiler_params=_cp(ARB),
    )(proj, proj, proj, gq, gk, bf)


def _fox_bias(cfc_blk, cfr_ref, off, t, head):
    lane = _iota2((1, 128), 1)
    sub = _iota2((FOX_HEADS, 1), 0)
    cq = jnp.sum(jnp.where(lane == head, cfc_blk, 0.0), axis=-1, keepdims=True)
    ck = jnp.sum(jnp.where(sub == head, cfr_ref[:, pl.ds(off, t)], 0.0), axis=0, keepdims=True)
    return cq - ck


def _fox_fwd(proj, qn, kn, cfc, cfr, name):
    s = proj.shape[0]
    t = min(ATT_T, s)
    nblk = s // t
    scale = HEAD_DIM ** -0.5

    def body(q_ref, k_ref, v_ref, cfc_ref, cfr_ref, o_ref, lse_ref):
        p = pl.program_id(0)
        i = pl.program_id(1)
        masks = _head_masks()
        lane = _iota2((1, 128), 1)
        valid = _iota2((t, t), 1) <= _iota2((t, t), 0)
        qv = q_ref[...]
        cfc_blk = cfc_ref[...]
        out = jnp.zeros((t, 128), F32)
        lse_out = jnp.zeros((t, 128), F32)
        for hh in range(2):
            head = 2 * p + hh
            qm = (jnp.where(masks[hh], qv, 0) * scale).astype(_BF)

            def block(j, carry, diag):
                m, l, acc = carry
                off = pl.multiple_of(j * t, t)
                kb = k_ref[pl.ds(off, t), :]
                vb = v_ref[pl.ds(off, t), :].astype(_BF)
                z = _dot(qm, kb, 1, 1) + _fox_bias(cfc_blk, cfr_ref, off, t, head)
                if diag:
                    z = jnp.where(valid, z, NEG)
                m_new = jnp.maximum(m, jnp.max(z, axis=-1, keepdims=True))
                pe = jnp.exp(z - m_new)
                alpha = jnp.exp(m - m_new)
                l = alpha * l + jnp.sum(pe, axis=-1, keepdims=True)
                acc = alpha * acc + _dot_exact(pe, vb, 1, 0, 2)
                return m_new, l, acc

            init = (jnp.full((t, 1), NEG, F32), jnp.zeros((t, 1), F32), jnp.zeros((t, 128), F32))
            carry = block(i, init, True)
            m, l, acc = lax.fori_loop(1, i + 1, lambda jj, c: block(i - jj, c, False), carry)
            out = jnp.where(masks[hh], acc / l, out)
            lse_out = jnp.where(lane == hh, m + jnp.log(l), lse_out)
        o_ref[...] = out
        lse_ref[...] = lse_out

    return _pcall(
        body, name=name, grid=(FOX_HEADS // 2, nblk),
        out_shape=[jax.ShapeDtypeStruct((s, 512), F32), jax.ShapeDtypeStruct((FOX_HEADS // 2, s, 128), F32)],
        in_specs=[pl.BlockSpec((t, 128), lambda p, i: (i, p)), pl.BlockSpec((s, 128), lambda p, i: (0, p)),
                  pl.BlockSpec((s, 128), lambda p, i: (0, C_VB // 128 + p)),
                  pl.BlockSpec((t, 128), lambda p, i: (i, 0)), pl.BlockSpec((FOX_HEADS, s), lambda p, i: (0, 0))],
        out_specs=[pl.BlockSpec((t, 128), lambda p, i: (i, p)), pl.BlockSpec((None, t, 128), lambda p, i: (p, i, 0))],
        compiler_params=_cp(PAR, ARB),
    )(qn, kn, proj, cfc, cfr)


def _fox_bwd(proj, qn, kn, cfc, cfr, o, lse, dmix, name):
    s = proj.shape[0]
    t = min(ATT_T, s)
    nblk = s // t
    scale = HEAD_DIM ** -0.5

    def body(q_ref, k_ref, v_ref, cfc_ref, cfr_ref, o_ref, lse_ref, do_ref, dq_ref, dk_ref, dv_ref, dcr_ref):
        p = pl.program_id(0)
        i = pl.program_id(1)
        masks = _head_masks()
        lane = _iota2((1, 128), 1)
        valid = _iota2((t, t), 1) <= _iota2((t, t), 0)
        before = (_iota2((t, t), 0) < _iota2((t, t), 1)).astype(_BF)

        @pl.when(i == 0)
        def _():
            dk_ref[...] = jnp.zeros_like(dk_ref)
            dv_ref[...] = jnp.zeros_like(dv_ref)
            dcr_ref[...] = jnp.zeros_like(dcr_ref)

        qv = q_ref[...]
        cfc_blk = cfc_ref[...]
        dov = do_ref[...]
        ov = o_ref[...]
        lse_blk = lse_ref[...]
        dq_out = jnp.zeros((t, 128), F32)
        for hh in range(2):
            head = 2 * p + hh
            qm = (jnp.where(masks[hh], qv, 0) * scale).astype(_BF)
            dom = jnp.where(masks[hh], dov, 0.0).astype(_BF)
            dsum = jnp.sum(dom.astype(F32) * ov, axis=-1, keepdims=True)
            lse_h = jnp.sum(jnp.where(lane == hh, lse_blk, 0.0), axis=-1, keepdims=True)

            def block(j, carry, diag):
                dq, pre = carry
                off = pl.multiple_of(j * t, t)
                kb = k_ref[pl.ds(off, t), :]
                vb = v_ref[pl.ds(off, t), :].astype(_BF)
                z = _dot(qm, kb, 1, 1) + _fox_bias(cfc_blk, cfr_ref, off, t, head)
                pm = jnp.exp(z - lse_h)
                if diag:
                    pm = jnp.where(valid, pm, 0.0)
                ds = pm * (_dot(dom, vb, 1, 1) - dsum)
                dsb = ds.astype(_BF)
                dq = dq + _dot(dsb, kb, 1, 0)
                dk_ref[pl.ds(off, t), :] += _dot(dsb, qm, 0, 0)
                dv_ref[pl.ds(off, t), :] += _dot(pm.astype(_BF), dom, 0, 0)
                lower_keys = pre + _dot_exact(ds, before, 1, 0, 2)
                if diag:
                    lower_keys = jnp.where(valid, lower_keys, 0.0)
                dcr_ref[hh:hh + 1, pl.ds(off, t)] += jnp.sum(lower_keys, axis=0, keepdims=True)
                return dq, pre + jnp.sum(ds, axis=-1, keepdims=True)

            carry = (jnp.zeros((t, 128), F32), jnp.zeros((t, 1), F32))
            carry = lax.fori_loop(0, i, lambda j, c: block(j, c, False), carry)
            dq, _ = block(i, carry, True)
            dq_out = jnp.where(masks[hh], dq * scale, dq_out)
        dq_ref[...] = dq_out

    blk = lambda c0: pl.BlockSpec((t, 128), lambda p, i: (i, c0 // 128 + p))
    res = lambda c0: pl.BlockSpec((s, 128), lambda p, i: (0, c0 // 128 + p))
    shp = jax.ShapeDtypeStruct((s, 512), F32)
    return _pcall(
        body, name=name, grid=(FOX_HEADS // 2, nblk),
        out_shape=[shp, shp, shp, jax.ShapeDtypeStruct((FOX_HEADS // 2, 8, s), F32)],
        in_specs=[blk(0), res(0), res(C_VB), pl.BlockSpec((t, 128), lambda p, i: (i, 0)),
                  pl.BlockSpec((FOX_HEADS, s), lambda p, i: (0, 0)), blk(0),
                  pl.BlockSpec((None, t, 128), lambda p, i: (p, i, 0)), blk(M_FOX)],
        out_specs=[blk(0), res(0), res(0), pl.BlockSpec((None, 8, s), lambda p, i: (p, 0, 0))],
        compiler_params=_cp(PAR, ARB),
    )(qn, kn, proj, cfc, cfr, o, lse, dmix)


def _fox_prep_bwd(proj, gq, gk, bf, dqn, dkn, dlogf, name):
    s = proj.shape[0]
    t = min(ATT_T, s)
    nblk = s // t

    def body(q_ref, k_ref, fl_ref, gq_ref, gk_ref, bf_ref, dqn_ref, dkn_ref, dlogf_ref,
             dq_ref, dk_ref, dfl_ref, dgq_ref, dgk_ref, dbf_ref):
        i = pl.program_id(0)
        gm = _group_matrix(512)

        @pl.when(i == 0)
        def _():
            dgq_ref[...] = jnp.zeros_like(dgq_ref)
            dgk_ref[...] = jnp.zeros_like(dgk_ref)
            dbf_ref[...] = jnp.zeros_like(dbf_ref)

        for src, gref, dyref, dst, dgref in ((q_ref, gq_ref, dqn_ref, dq_ref, dgq_ref),
                                              (k_ref, gk_ref, dkn_ref, dk_ref, dgk_ref)):
            v = src[...]
            dy = dyref[...]
            rstd = lax.rsqrt(_group_mean(v * v, gm) + EPS)
            xhat = v * rstd
            dgref[...] += jnp.sum(dy * xhat, axis=0, keepdims=True)
            dxh = dy * gref[...]
            dst[...] = (rstd * (dxh - xhat * _group_mean(dxh * xhat, gm))).astype(dst.dtype)

        xv = fl_ref[...] + bf_ref[...]
        e = jnp.exp(-jnp.abs(xv))
        dfl = dlogf_ref[...] * (jnp.where(xv >= 0.0, e, 1.0) / (1.0 + e))
        dfl_ref[...] = dfl.astype(dfl_ref.dtype)
        dbf_ref[...] += jnp.sum(dfl, axis=0, keepdims=True)

    col = lambda w, c0: pl.BlockSpec((t, w), lambda i: (i, c0 // w))
    v512 = pl.BlockSpec((1, 512), lambda i: (0, 0))
    v128 = pl.BlockSpec((1, 128), lambda i: (0, 0))
    return _pcall(
        body, name=name, grid=(nblk,),
        out_shape=[jax.ShapeDtypeStruct((s, 512), _BF), jax.ShapeDtypeStruct((s, 512), _BF),
                   jax.ShapeDtypeStruct((s, 128), _BF), jax.ShapeDtypeStruct((1, 512), F32),
                   jax.ShapeDtypeStruct((1, 512), F32), jax.ShapeDtypeStruct((1, 128), F32)],
        in_specs=[col(512, C_QB), col(512, C_KB), col(128, C_FL), v512, v512, v128, col(512, 0), col(512, 0), col(128, 0)],
        out_specs=[col(512, 0), col(512, 0), col(128, 0), v512, v512, v128], compiler_params=_cp(ARB),
    )(proj, proj, proj, gq, gk, bf, dqn, dkn, dlogf)


GELU_C = 0.7978845608028654
GELU_A = 0.044715


def _gelu(x):
    return 0.5 * x * (1.0 + jnp.tanh(GELU_C * (x + GELU_A * x * x * x)))


def _gelu_grad(x):
    th = jnp.tanh(GELU_C * (x + GELU_A * x * x * x))
    return 0.5 * (1.0 + th) + 0.5 * x * (1.0 - th * th) * (GELU_C * (1.0 + 3.0 * GELU_A * x * x))


def _sgu_tril():
    return _iota2((SGU_CHUNK, SGU_CHUNK), 0) >= _iota2((SGU_CHUNK, SGU_CHUNK), 1)


def _sgu_group_masks():
    lane = _iota2((1, 256), 1)
    return [(lane // HEAD_DIM) == g for g in range(SGU_GROUPS)]


def _sgu_fwd(proj, w, gs, bexp, name):
    s = proj.shape[0]
    t = min(2 * SGU_CHUNK, s)

    def body(u_ref, v_ref, w_ref, gs_ref, b_ref, o_ref):
        gm = _group_matrix(256)
        gmask = _sgu_group_masks()
        tril = _sgu_tril()
        u = _gelu(u_ref[...])
        vg = _gelu(v_ref[...])
        vhat = (vg * lax.rsqrt(_group_mean(vg * vg, gm) + EPS) * gs_ref[...]).astype(_BF)
        for ch in range(t // SGU_CHUNK):
            rows = slice(ch * SGU_CHUNK, (ch + 1) * SGU_CHUNK)
            mixed = b_ref[...]
            for g in range(SGU_GROUPS):
                wg = jnp.where(tril, w_ref[g], 0.0).astype(_BF)
                mixed = jnp.where(gmask[g], mixed + _dot(wg, vhat[rows], 1, 0), mixed)
            o_ref[rows, :] = u[rows] * mixed

    return _pcall(
        body, name=name, grid=(s // t,), out_shape=jax.ShapeDtypeStruct((s, 256), F32),
        in_specs=[_row_spec(t, 256, C_UC // 256), _row_spec(t, 256, C_VC // 256),
                  pl.BlockSpec((SGU_GROUPS, SGU_CHUNK, SGU_CHUNK), lambda i: (0, 0, 0)), _vec_spec(256),
                  pl.BlockSpec((SGU_CHUNK, 256), lambda i: (0, 0))],
        out_specs=_row_spec(t, 256), compiler_params=_cp(PAR),
    )(proj, proj, w, gs, bexp)


def _sgu_bwd(proj, w, gs, bexp, dmix, name):
    s = proj.shape[0]
    t = min(2 * SGU_CHUNK, s)
    nstep = s // t

    def body(u_ref, v_ref, w_ref, gs_ref, b_ref, do_ref, du_ref, dv_ref, dw_ref, db_ref, dgs_ref):
        i = pl.program_id(0)
        gm = _group_matrix(256)
        gmask = _sgu_group_masks()
        tril = _sgu_tril()

        @pl.when(i == 0)
        def _():
            dw_ref[...] = jnp.zeros_like(dw_ref)
            db_ref[...] = jnp.zeros_like(db_ref)
            dgs_ref[...] = jnp.zeros_like(dgs_ref)

        uc = u_ref[...]
        vc = v_ref[...]
        u = _gelu(uc)
        vg = _gelu(vc)
        rstd = lax.rsqrt(_group_mean(vg * vg, gm) + EPS)
        xh = vg * rstd
        gsv = gs_ref[...]
        vhat = (xh * gsv).astype(_BF)
        dov = do_ref[...]
        dm = dov * u
        for ch in range(t // SGU_CHUNK):
            rows = slice(ch * SGU_CHUNK, (ch + 1) * SGU_CHUNK)
            mixed = b_ref[...]
            dvh = jnp.zeros((SGU_CHUNK, 256), F32)
            dmc = dm[rows]
            for g in range(SGU_GROUPS):
                wg = jnp.where(tril, w_ref[g], 0.0).astype(_BF)
                mixed = jnp.where(gmask[g], mixed + _dot(wg, vhat[rows], 1, 0), mixed)
                dvh = jnp.where(gmask[g], _dot(wg, dmc.astype(_BF), 0, 0), dvh)
                dw_ref[g] += _dot(jnp.where(gmask[g], dmc, 0.0).astype(_BF), vhat[rows], 1, 1)
            db_ref[...] += dmc
            du_ref[rows, :] = (dov[rows] * mixed * _gelu_grad(uc[rows])).astype(du_ref.dtype)
            xhc = xh[rows]
            dgs_ref[...] += jnp.sum(dvh * xhc, axis=0, keepdims=True)
            dxh = dvh * gsv
            dvg = rstd[rows] * (dxh - xhc * _group_mean(dxh * xhc, gm))
            dv_ref[rows, :] = (dvg * _gelu_grad(vc[rows])).astype(dv_ref.dtype)

        @pl.when(i == nstep - 1)
        def _():
            for g in range(SGU_GROUPS):
                dw_ref[g] = jnp.where(tril, dw_ref[g], 0.0)

    wspec = pl.BlockSpec((SGU_GROUPS, SGU_CHUNK, SGU_CHUNK), lambda i: (0, 0, 0))
    bspec = pl.BlockSpec((SGU_CHUNK, 256), lambda i: (0, 0))
    return _pcall(
        body, name=name, grid=(nstep,),
        out_shape=[jax.ShapeDtypeStruct((s, 256), _BF), jax.ShapeDtypeStruct((s, 256), _BF),
                   jax.ShapeDtypeStruct((SGU_GROUPS, SGU_CHUNK, SGU_CHUNK), F32),
                   jax.ShapeDtypeStruct((SGU_CHUNK, 256), F32), jax.ShapeDtypeStruct((1, 256), F32)],
        in_specs=[_row_spec(t, 256, C_UC // 256), _row_spec(t, 256, C_VC // 256), wspec, _vec_spec(256), bspec,
                  _row_spec(t, 256, M_SGU // 256)],
        out_specs=[_row_spec(t, 256), _row_spec(t, 256), wspec, bspec, _vec_spec(256)],
        compiler_params=_cp(ARB),
    )(proj, proj, w, gs, bexp, dmix)


def _ada_fwd(c_all, ada_w, name):
    depth, d, n = ada_w.shape

    def body(c_ref, w_ref, o_ref):
        cv = c_ref[...]
        cond = cv / (1.0 + jnp.exp(-cv))
        o_ref[...] = _dot_f32(cond, w_ref[...], 1, 0)

    return _pcall(
        body, name=name, grid=(depth,), out_shape=jax.ShapeDtypeStruct((depth, N_DEV, n), F32),
        in_specs=[pl.BlockSpec((N_DEV, d), lambda l: (0, 0)), pl.BlockSpec((None, d, n), lambda l: (l, 0, 0))],
        out_specs=pl.BlockSpec((None, N_DEV, n), lambda l: (l, 0, 0)), compiler_params=_cp(PAR),
    )(c_all, ada_w)


def _ada_bwd(c_all, dmod, name):
    depth, _, n = dmod.shape
    d = c_all.shape[1]

    def body(c_ref, dm_ref, o_ref):
        cv = c_ref[...]
        cond = cv / (1.0 + jnp.exp(-cv))
        o_ref[...] = _dot_f32(cond, dm_ref[...], 0, 0)

    return _pcall(
        body, name=name, grid=(depth,), out_shape=jax.ShapeDtypeStruct((depth, d, n), F32),
        in_specs=[pl.BlockSpec((N_DEV, d), lambda l: (0, 0)), pl.BlockSpec((None, N_DEV, n), lambda l: (l, 0, 0))],
        out_specs=pl.BlockSpec((None, d, n), lambda l: (l, 0, 0)), compiler_params=_cp(PAR),
    )(c_all, dmod)


def _adamw(slots, w, m, v, name):
    n, r, c = slots.shape
    tr = 256 if r % 256 == 0 else r
    bc1 = 1.0 - ADAM_B1 ** ADAM_STEP
    bc2 = 1.0 - ADAM_B2 ** ADAM_STEP

    def body(s_ref, w_ref, m_ref, v_ref, g_ref, d_ref, nm_ref, nv_ref):
        g = s_ref[0].astype(F32)
        for j in range(1, n):
            g = g + s_ref[j].astype(F32)
        m_new = ADAM_B1 * m_ref[...] + (1.0 - ADAM_B1) * g
        v_new = ADAM_B2 * v_ref[...] + (1.0 - ADAM_B2) * (g * g)
        g_ref[...] = g
        nm_ref[...] = m_new
        nv_ref[...] = v_new
        d_ref[...] = -ADAM_LR * ((m_new / bc1) / (jnp.sqrt(v_new / bc2) + ADAM_EPS) + ADAM_WD * w_ref[...])

    tile = pl.BlockSpec((tr, c), lambda i: (i, 0))
    shp = jax.ShapeDtypeStruct((r, c), F32)
    return _pcall(
        body, name=name, grid=(r // tr,), out_shape=[shp] * 4,
        in_specs=[pl.BlockSpec((n, tr, c), lambda i: (0, i, 0)), tile, tile, tile],
        out_specs=[tile] * 4, compiler_params=_cp(PAR),
    )(slots, w, m, v)


def _pad_cols(a, width):
    return jnp.pad(a, ((0, 0), (0, width - a.shape[1])))


def _w_in_layout(w):
    pad = jnp.zeros(w.shape[:-1] + (IN_P - IN_W,), w.dtype)
    return jnp.concatenate([w[..., 768:FL_SRC], w[..., :768], w[..., UC_SRC:], w[..., FL_SRC:UC_SRC], pad], axis=-1)


def _w_in_unlayout(g):
    return jnp.concatenate([g[..., C_QA:C_UC], g[..., :C_QA], g[..., C_FL:C_FL + FOX_HEADS], g[..., C_UC:C_FL]], axis=-1)


SMALL = [("ada_b", DEPTH * 6 * D_MODEL), ("norm1_g", DEPTH * D_MODEL), ("norm2_g", DEPTH * D_MODEL),
         ("sgu_w", DEPTH * SGU_GROUPS * SGU_CHUNK * SGU_CHUNK), ("sgu_b", DEPTH * SGU_GROUPS * SGU_CHUNK),
         ("sgu_norm_g", DEPTH * SGU_GROUPS * HEAD_DIM), ("q_norm_g", DEPTH * HEAD_DIM), ("k_norm_g", DEPTH * HEAD_DIM),
         ("b_forget", DEPTH * FOX_HEADS), ("loss", 1)]
SMALL_ROWS = 2560


def _pack_small(parts):
    flat = jnp.concatenate([parts[name].reshape(-1).astype(F32) for name, _ in SMALL])
    return jnp.pad(flat, (0, SMALL_ROWS * 128 - flat.shape[0])).reshape(SMALL_ROWS, 128)


def _unpack_small(packed, shapes):
    flat = packed.reshape(-1)
    out, off = {}, 0
    for name, size in SMALL:
        out[name] = flat[off:off + size].reshape(shapes[name])
        off += size
    return out


def kernel(x, c, ada_w, ada_b, norm1_g, norm2_g, w_in, b_forget, q_norm_g, k_norm_g, sgu_norm_g, sgu_w, sgu_b, w_out, mlp_w1, mlp_w2, loss_target, m_ada_w, m_ada_b, m_norm1_g, m_norm2_g, m_w_in, m_b_forget, m_q_norm_g, m_k_norm_g, m_sgu_norm_g, m_sgu_w, m_sgu_b, m_w_out, m_mlp_w1, m_mlp_w2, v_ada_w, v_ada_b, v_norm1_g, v_norm2_g, v_w_in, v_b_forget, v_q_norm_g, v_k_norm_g, v_sgu_norm_g, v_sgu_w, v_sgu_b, v_w_out, v_mlp_w1, v_mlp_w2):
    me = _lin(_my_pos())
    x0 = x[0]
    target = loss_target[0]
    n_ada = ada_w.shape[2]

    w_in_g, w_out_g, w1_g, w2_g, c_all = _exchange(
        [w_in.astype(_XBF), w_out.astype(_XBF), mlp_w1.astype(_XBF), mlp_w2.astype(_XBF), c], False, "gather_weights")
    c_all = c_all.reshape(N_DEV, D_MODEL)
    w_in_full = _w_in_layout(jnp.transpose(w_in_g, (1, 2, 0, 3)).reshape(DEPTH, D_MODEL, IN_W))
    w_out_full = jnp.transpose(w_out_g, (1, 0, 2, 3)).reshape(DEPTH, D_MODEL, D_MODEL)
    w1_full = jnp.transpose(w1_g, (1, 2, 0, 3)).reshape(DEPTH, D_MODEL, D_FF)
    w2_full = jnp.transpose(w2_g, (1, 0, 2, 3)).reshape(DEPTH, D_FF, D_MODEL)

    mod_part = _ada_fwd(c_all, ada_w, "ada_fwd")
    (mod_rows,) = _exchange([jnp.transpose(mod_part, (1, 0, 2))], True, "scatter_mod")
    mod = jnp.transpose(mod_rows, (1, 0, 2)).reshape(DEPTH, 6 * D_MODEL) + ada_b
    mods = mod.reshape(DEPTH, 6, 1, D_MODEL)

    saved = []
    xl = x0
    for l in range(DEPTH):
        sh1, sc1, g1, sh2, sc2, g2 = (mods[l, r] for r in range(6))
        n1 = norm1_g[l].reshape(1, D_MODEL)
        n2 = norm2_g[l].reshape(1, D_MODEL)
        gq = jnp.tile(q_norm_g[l], FOX_HEADS).reshape(1, 512)
        gk = jnp.tile(k_norm_g[l], FOX_HEADS).reshape(1, 512)
        bf = jnp.pad(b_forget[l], (0, 128 - FOX_HEADS)).reshape(1, 128)
        gs = sgu_norm_g[l].reshape(1, 256)
        bexp = jnp.repeat(sgu_b[l].T, HEAD_DIM, axis=1)

        h1 = _norm_mod(xl, n1, sc1, sh1, "norm_mod")
        proj = _mm_nn(h1, w_in_full[l], "plain", (), "mm_in")
        o_sb = _sb_fwd(proj, "sb_fwd")
        qn, kn, cfc, cfr = _fox_prep(proj, gq, gk, bf, "fox_prep")
        o_fox, lse = _fox_fwd(proj, qn, kn, cfc, cfr, "fox_fwd")
        o_sgu = _sgu_fwd(proj, sgu_w[l], gs, bexp, "sgu_fwd")
        mixed = jnp.concatenate([o_sb, o_fox, o_sgu], axis=-1).astype(_BF)
        x_mid, y1 = _mm_nn(mixed, w_out_full[l], "resid", (xl, g1), "mm_out")
        h2 = _norm_mod(x_mid, n2, sc2, sh2, "norm_mod")
        r2, r1 = _mm_nn(h2, w1_full[l], "relu2", (), "mm_w1")
        x_out, y2 = _mm_nn(r2, w2_full[l], "resid", (x_mid, g2), "mm_w2")
        saved.append(dict(x_in=xl, h1=h1, proj=proj, o_sb=o_sb, qn=qn, kn=kn, cfc=cfc, cfr=cfr, o_fox=o_fox, lse=lse,
                          mixed=mixed, x_mid=x_mid, y1=y1, h2=h2, r2=r2, r1=r1, y2=y2,
                          n1=n1, n2=n2, gq=gq, gk=gk, bf=bf, gs=gs, bexp=bexp))
        xl = x_out

    loss_part, dx = _loss_head(xl, target, "loss_head")

    g_in, g_out, g_w1, g_w2 = [None] * DEPTH, [None] * DEPTH, [None] * DEPTH, [None] * DEPTH
    small = {k: [None] * DEPTH for k in ("mod", "norm1_g", "norm2_g", "sgu_w", "sgu_b", "sgu_norm_g",
                                         "q_norm_g", "k_norm_g", "b_forget")}
    for l in reversed(range(DEPTH)):
        sv = saved[l]
        sh1, sc1, g1, sh2, sc2, g2 = (mods[l, r] for r in range(6))
        dy2, dg2 = _res_bwd(dx, sv["y2"], g2, "res_bwd")
        da = _mm_nt(dy2, w2_full[l], "mul2", (sv["r1"],), _BF, "mm_w2_bwd")
        g_w2[l] = _mm_tn(sv["r2"], dy2, "rows", "mm_w2_grad")
        dh2 = _mm_nt(da, w1_full[l], "plain", (), F32, "mm_w1_bwd")
        g_w1[l] = _mm_tn(sv["h2"], da, "cols", "mm_w1_grad")
        dx_mid, dsh2, dsc2, dn2 = _norm_mod_bwd(sv["x_mid"], dh2, dx, sv["n2"], sc2, "norm_mod_bwd")
        dy1, dg1 = _res_bwd(dx_mid, sv["y1"], g1, "res_bwd")
        dmix = _mm_nt(dy1, w_out_full[l], "plain", (), F32, "mm_out_bwd")
        g_out[l] = _mm_tn(sv["mixed"], dy1, "rows", "mm_out_grad")
        dq_sb, dk_sb, dv_sb = _sb_bwd(sv["proj"], sv["o_sb"], dmix, "sb_bwd")
        dqn, dkn, dv_fox, dcr = _fox_bwd(sv["proj"], sv["qn"], sv["kn"], sv["cfc"], sv["cfr"], sv["o_fox"], sv["lse"],
                                         dmix, "fox_bwd")
        dlogf = _pad_cols(dcr[:, :2, :].reshape(FOX_HEADS, -1).T, 128)
        dq_fox, dk_fox, dfl, dgq, dgk, dbf = _fox_prep_bwd(sv["proj"], sv["gq"], sv["gk"], sv["bf"], dqn, dkn, dlogf,
                                                           "fox_prep_bwd")
        duc, dvc, dsw, dsb, dsg = _sgu_bwd(sv["proj"], sgu_w[l], sv["gs"], sv["bexp"], dmix, "sgu_bwd")
        dproj = jnp.concatenate(
            [dq_fox, dk_fox, dv_fox.astype(_BF), dq_sb.astype(_BF), dk_sb.astype(_BF), dv_sb.astype(_BF), duc, dvc, dfl,
             jnp.zeros((dfl.shape[0], IN_P - C_FL - 128), _BF)], axis=-1)
        dh1 = _mm_nt(dproj, w_in_full[l], "plain", (), F32, "mm_in_bwd")
        g_in[l] = _mm_tn(sv["h1"], dproj, None, "mm_in_grad")
        dx, dsh1, dsc1, dn1 = _norm_mod_bwd(sv["x_in"], dh1, dx_mid, sv["n1"], sc1, "norm_mod_bwd")

        small["mod"][l] = jnp.concatenate([dsh1, dsc1, dg1, dsh2, dsc2, dg2], axis=-1).reshape(-1)
        small["norm1_g"][l] = dn1.reshape(-1)
        small["norm2_g"][l] = dn2.reshape(-1)
        small["sgu_w"][l] = dsw
        small["sgu_b"][l] = dsb.reshape(SGU_CHUNK, SGU_GROUPS, HEAD_DIM).sum(-1).T
        small["sgu_norm_g"][l] = dsg.reshape(SGU_GROUPS, HEAD_DIM)
        small["q_norm_g"][l] = dgq.reshape(FOX_HEADS, HEAD_DIM).sum(0)
        small["k_norm_g"][l] = dgk.reshape(FOX_HEADS, HEAD_DIM).sum(0)
        small["b_forget"][l] = dbf[0, :FOX_HEADS]

    parts = {k: jnp.stack(v) for k, v in small.items()}
    parts["ada_b"] = parts.pop("mod")
    parts["loss"] = loss_part
    (small_all,) = _exchange([_pack_small(parts)], False, "gather_small")
    zero1 = jnp.zeros((1,), F32)
    packs = [_pack_small(dict(ada_b=a, norm1_g=b, norm2_g=cc, sgu_w=d, sgu_b=e, sgu_norm_g=f, q_norm_g=g, k_norm_g=h,
                              b_forget=i, loss=zero1))
             for a, b, cc, d, e, f, g, h, i in (
                 (ada_b, norm1_g, norm2_g, sgu_w, sgu_b, sgu_norm_g, q_norm_g, k_norm_g, b_forget),
                 (m_ada_b, m_norm1_g, m_norm2_g, m_sgu_w, m_sgu_b, m_sgu_norm_g, m_q_norm_g, m_k_norm_g, m_b_forget),
                 (v_ada_b, v_norm1_g, v_norm2_g, v_sgu_w, v_sgu_b, v_sgu_norm_g, v_q_norm_g, v_k_norm_g, v_b_forget))]
    small_shapes = dict(ada_b=ada_b.shape, norm1_g=norm1_g.shape, norm2_g=norm2_g.shape, sgu_w=sgu_w.shape,
                        sgu_b=sgu_b.shape, sgu_norm_g=sgu_norm_g.shape, q_norm_g=q_norm_g.shape,
                        k_norm_g=k_norm_g.shape, b_forget=b_forget.shape, loss=())
    s_out = [_unpack_small(a, small_shapes) for a in _adamw(small_all, *packs, "adamw_small")]

    dmod_all = small_all[:, :DEPTH * 6 * D_MODEL // 128, :].reshape(N_DEV, DEPTH, 6 * D_MODEL)
    dmod_mine = lax.dynamic_slice_in_dim(dmod_all, me * n_ada, n_ada, axis=2)
    g_ada = _ada_bwd(c_all, jnp.transpose(dmod_mine, (1, 0, 2)), "ada_bwd")
    ada_out = _adamw(g_ada.reshape(1, DEPTH * D_MODEL, n_ada), ada_w.reshape(-1, n_ada), m_ada_w.reshape(-1, n_ada),
                     v_ada_w.reshape(-1, n_ada), "adamw_ada")
    ada_out = [a.reshape(ada_w.shape) for a in ada_out]

    gi = jnp.stack([_w_in_unlayout(g) for g in g_in])
    gi = jnp.transpose(gi.reshape(DEPTH, D_MODEL, N_DEV, IN_W // N_DEV), (2, 0, 1, 3))
    go = jnp.stack(g_out, axis=1)
    gw1 = jnp.stack(g_w1, axis=1)
    gw2 = jnp.stack(g_w2, axis=1)
    ri, ro, r1, r2 = _exchange([gi, go, gw1, gw2], True, "scatter_grads")

    def big(slots, w, m, v, name):
        cdim = w.shape[-1]
        outs = _adamw(slots.reshape(N_DEV, -1, cdim), w.reshape(-1, cdim), m.reshape(-1, cdim), v.reshape(-1, cdim), name)
        return [a.reshape(w.shape) for a in outs]

    in_out = big(ri, w_in, m_w_in, v_w_in, "adamw_in")
    out_out = big(ro, w_out, m_w_out, v_w_out, "adamw_out")
    w1_out = big(r1, mlp_w1, m_mlp_w1, v_mlp_w1, "adamw_w1")
    w2_out = big(r2, mlp_w2, m_mlp_w2, v_mlp_w2, "adamw_w2")

    def leaves(idx):
        sm = s_out[idx]
        return [ada_out[idx], sm["ada_b"], sm["norm1_g"], sm["norm2_g"], in_out[idx], sm["b_forget"], sm["q_norm_g"],
                sm["k_norm_g"], sm["sgu_norm_g"], sm["sgu_w"], sm["sgu_b"], out_out[idx], w1_out[idx], w2_out[idx]]

    loss = s_out[0]["loss"]
    grad_x = dx.reshape(x.shape)
    return (loss, grad_x, *leaves(0), *leaves(1), *leaves(2), *leaves(3))
```

```python
import jax
import jax.numpy as jnp
from jax import lax
from jax.experimental import pallas as pl
from jax.experimental.pallas import tpu as pltpu

F32 = jnp.float32
_BF = jnp.bfloat16
_XBF = jnp.bfloat16

N_DEV = 8
D_MODEL = 1024
DEPTH = 4
HEAD_DIM = 64
SB_HEADS = 4
FOX_HEADS = 8
SGU_GROUPS = 4
SGU_CHUNK = 128
D_FF = 4096
EPS = 1e-6
IN_W = 2824
FL_SRC = 2304
UC_SRC = 2312

C_QB, C_KB, C_VB = 0, 512, 1024
C_QA, C_KA, C_VA = 1536, 1792, 2048
C_UC, C_VC = 2304, 2560
C_FL = 2816
IN_P = 3072
M_SB, M_FOX, M_SGU = 0, 256, 768

ADAM_LR = 0.001
ADAM_B1 = 0.9
ADAM_B2 = 0.999
ADAM_EPS = 1e-08
ADAM_WD = 0.01
ADAM_STEP = 10

ATT_T = 256
ROW_T = 512
MM_T = 1024
NEG = -1e30
UNDERFLOW = -100.0

MESH = pl.DeviceIdType.MESH
ARB = pltpu.ARBITRARY
PAR = pltpu.PARALLEL
HIGHEST = lax.Precision.HIGHEST


def _pcall(body, **kw):
    return pl.pallas_call(body, **kw)


def _cp(*sem):
    return pltpu.CompilerParams(dimension_semantics=tuple(sem))


def _dot(a, b, ca, cb):
    return lax.dot_general(a, b, (((ca,), (cb,)), ((), ())), preferred_element_type=F32)


def _dot_f32(a, b, ca, cb):
    return lax.dot_general(a, b, (((ca,), (cb,)), ((), ())), precision=HIGHEST, preferred_element_type=F32)


def _split(v, parts):
    out = []
    r = v
    for _ in range(parts - 1):
        p = r.astype(_BF)
        out.append(p)
        r = r - p.astype(F32)
    out.append(r.astype(_BF))
    return out


def _dot_exact(v, m, ca, cb, parts, v_left=True):
    out = None
    for p in _split(v, parts):
        term = _dot(p, m, ca, cb) if v_left else _dot(m, p, ca, cb)
        out = term if out is None else out + term
    return out


def _iota2(shape, dim):
    return lax.broadcasted_iota(jnp.int32, shape, dim)


def _my_pos():
    return lax.axis_index("x"), lax.axis_index("y"), lax.axis_index("c")


def _flip(pos, k):
    x, y, c = pos
    px = 1 - x if (k >> 2) & 1 else x
    py = 1 - y if (k >> 1) & 1 else y
    pc = 1 - c if k & 1 else c
    return px, py, pc


def _lin(pos):
    return 4 * pos[0] + 2 * pos[1] + pos[2]


def _exchange(arrays, scatter, name):
    n = len(arrays)
    if scatter:
        out_shapes = [jax.ShapeDtypeStruct(a.shape, a.dtype) for a in arrays]
    else:
        out_shapes = [jax.ShapeDtypeStruct((N_DEV,) + a.shape, a.dtype) for a in arrays]

    def body(*refs):
        ins, outs = refs[:n], refs[n:2 * n]
        send_sems, recv_sems, local_sems = refs[2 * n:]
        me = _my_pos()
        me_i = _lin(me)
        local = []
        for a in range(n):
            src = ins[a].at[me_i] if scatter else ins[a]
            cp = pltpu.make_async_copy(src, outs[a].at[me_i], local_sems.at[a])
            cp.start()
            local.append(cp)
        remote = []
        for a in range(n):
            for k in range(1, N_DEV):
                peer = _flip(me, k)
                src = ins[a].at[_lin(peer)] if scatter else ins[a]
                cp = pltpu.make_async_remote_copy(
                    src_ref=src, dst_ref=outs[a].at[me_i],
                    send_sem=send_sems.at[a, k - 1], recv_sem=recv_sems.at[a, k - 1],
                    device_id=peer, device_id_type=MESH)
                cp.start()
                remote.append((a, k, cp))
        for a, k, cp in remote:
            cp.wait_send()
            peer = _flip(me, k)
            src = ins[a].at[me_i] if scatter else ins[a]
            pltpu.make_async_remote_copy(
                src_ref=src, dst_ref=outs[a].at[_lin(peer)],
                send_sem=send_sems.at[a, k - 1], recv_sem=recv_sems.at[a, k - 1],
                device_id=peer, device_id_type=MESH).wait_recv()
        for cp in local:
            cp.wait()

    any_spec = pl.BlockSpec(memory_space=pl.ANY)
    outs = _pcall(
        body, name=name, out_shape=out_shapes,
        in_specs=[any_spec] * n, out_specs=[any_spec] * n,
        scratch_shapes=[pltpu.SemaphoreType.DMA((n, N_DEV - 1)), pltpu.SemaphoreType.DMA((n, N_DEV - 1)),
                        pltpu.SemaphoreType.DMA((n,))],
    )(*arrays)
    return list(outs)


def _row_spec(t, w, col=0):
    return pl.BlockSpec((t, w), lambda i: (i, col))


def _vec_spec(w, col=0):
    return pl.BlockSpec((1, w), lambda i: (0, col))


def _norm_mod(x, g, sc, sh, name):
    s, d = x.shape
    t = min(ROW_T, s)

    def body(x_ref, g_ref, sc_ref, sh_ref, h_ref):
        xf = x_ref[...]
        rstd = lax.rsqrt(jnp.mean(xf * xf, axis=-1, keepdims=True) + EPS)
        y = xf * rstd * g_ref[...]
        h_ref[...] = (y * (1.0 + sc_ref[...]) + sh_ref[...]).astype(h_ref.dtype)

    return _pcall(
        body, name=name, grid=(s // t,), out_shape=jax.ShapeDtypeStruct((s, d), _BF),
        in_specs=[_row_spec(t, d), _vec_spec(d), _vec_spec(d), _vec_spec(d)],
        out_specs=_row_spec(t, d), compiler_params=_cp(PAR),
    )(x, g, sc, sh)


def _norm_mod_bwd(x, dh, dres, g, sc, name):
    s, d = x.shape
    t = min(ROW_T, s)

    def body(x_ref, dh_ref, dres_ref, g_ref, sc_ref, dx_ref, dsh_ref, dsc_ref, dg_ref):
        i = pl.program_id(0)
        xf = x_ref[...]
        dh = dh_ref[...]
        gv = g_ref[...]
        rstd = lax.rsqrt(jnp.mean(xf * xf, axis=-1, keepdims=True) + EPS)
        xhat = xf * rstd
        dn = dh * (1.0 + sc_ref[...])
        dxh = dn * gv
        dx_ref[...] = dres_ref[...] + rstd * (dxh - xhat * jnp.mean(dxh * xhat, axis=-1, keepdims=True))

        @pl.when(i == 0)
        def _():
            dsh_ref[...] = jnp.zeros_like(dsh_ref)
            dsc_ref[...] = jnp.zeros_like(dsc_ref)
            dg_ref[...] = jnp.zeros_like(dg_ref)

        dsh_ref[...] += jnp.sum(dh, axis=0, keepdims=True)
        dsc_ref[...] += jnp.sum(dh * (xhat * gv), axis=0, keepdims=True)
        dg_ref[...] += jnp.sum(dn * xhat, axis=0, keepdims=True)

    vec = jax.ShapeDtypeStruct((1, d), F32)
    return _pcall(
        body, name=name, grid=(s // t,),
        out_shape=[jax.ShapeDtypeStruct((s, d), F32), vec, vec, vec],
        in_specs=[_row_spec(t, d), _row_spec(t, d), _row_spec(t, d), _vec_spec(d), _vec_spec(d)],
        out_specs=[_row_spec(t, d), _vec_spec(d), _vec_spec(d), _vec_spec(d)],
        compiler_params=_cp(ARB),
    )(x, dh, dres, g, sc)


def _res_bwd(dx, y, gate, name):
    s, d = dx.shape
    t = min(ROW_T, s)

    def body(dx_ref, y_ref, g_ref, dy_ref, dg_ref):
        i = pl.program_id(0)
        dxv = dx_ref[...]
        dy_ref[...] = (dxv * g_ref[...]).astype(dy_ref.dtype)

        @pl.when(i == 0)
        def _():
            dg_ref[...] = jnp.zeros_like(dg_ref)

        dg_ref[...] += jnp.sum(dxv * y_ref[...], axis=0, keepdims=True)

    return _pcall(
        body, name=name, grid=(s // t,),
        out_shape=[jax.ShapeDtypeStruct((s, d), _BF), jax.ShapeDtypeStruct((1, d), F32)],
        in_specs=[_row_spec(t, d), _row_spec(t, d), _vec_spec(d)],
        out_specs=[_row_spec(t, d), _vec_spec(d)], compiler_params=_cp(ARB),
    )(dx, y, gate)


def _loss_head(y, target, name):
    s, d = y.shape
    t = min(ROW_T, s)

    def body(y_ref, t_ref, loss_ref, dy_ref):
        i = pl.program_id(0)
        diff = y_ref[...] - t_ref[...]
        dy_ref[...] = diff * (1.0 / d)

        @pl.when(i == 0)
        def _():
            loss_ref[...] = jnp.zeros_like(loss_ref)

        rows = jnp.sum(diff * diff, axis=-1, keepdims=True)
        loss_ref[...] += (0.5 / d) * jnp.sum(rows, axis=0, keepdims=True)

    return _pcall(
        body, name=name, grid=(s // t,),
        out_shape=[jax.ShapeDtypeStruct((1, 1), F32), jax.ShapeDtypeStruct((s, d), F32)],
        in_specs=[_row_spec(t, d), _row_spec(t, d)],
        out_specs=[pl.BlockSpec((1, 1), lambda i: (0, 0)), _row_spec(t, d)], compiler_params=_cp(ARB),
    )(y, target)


def _mm_nn(a, b, epi, extras, name):
    m, kdim = a.shape
    n = b.shape[1]
    tm, tn, tk = min(MM_T, m), min(MM_T, n), min(MM_T, kdim)
    nk = kdim // tk
    n_extra = len(extras)
    n_out = 1 if epi == "plain" else 2

    def finish(y, extra_refs, out_refs):
        if epi == "plain":
            out_refs[0][...] = y.astype(out_refs[0].dtype)
        elif epi == "resid":
            x_ref, g_ref = extra_refs
            out_refs[0][...] = x_ref[...] + g_ref[...] * y
            out_refs[1][...] = y
        else:
            r = jnp.maximum(y, 0.0)
            out_refs[0][...] = (r * r).astype(out_refs[0].dtype)
            out_refs[1][...] = r.astype(out_refs[1].dtype)

    def body(a_ref, b_ref, *rest):
        extra_refs = rest[:n_extra]
        out_refs = rest[n_extra:n_extra + n_out]
        part = _dot(a_ref[...].astype(_BF), b_ref[...].astype(_BF), 1, 0)
        if nk == 1:
            finish(part, extra_refs, out_refs)
        else:
            acc_ref = rest[-1]
            k = pl.program_id(2)

            @pl.when(k == 0)
            def _():
                acc_ref[...] = part

            @pl.when(k > 0)
            def _():
                acc_ref[...] += part

            @pl.when(k == nk - 1)
            def _():
                finish(acc_ref[...], extra_refs, out_refs)

    tile = pl.BlockSpec((tm, tn), lambda i, j, k: (i, j))
    in_specs = [pl.BlockSpec((tm, tk), lambda i, j, k: (i, k)), pl.BlockSpec((tk, tn), lambda i, j, k: (k, j))]
    if epi == "plain":
        out_shape = [jax.ShapeDtypeStruct((m, n), F32)]
    elif epi == "resid":
        in_specs += [tile, pl.BlockSpec((1, tn), lambda i, j, k: (0, j))]
        out_shape = [jax.ShapeDtypeStruct((m, n), F32)] * 2
    else:
        out_shape = [jax.ShapeDtypeStruct((m, n), _BF), jax.ShapeDtypeStruct((m, n), F32)]
    outs = _pcall(
        body, name=name, grid=(m // tm, n // tn, nk), out_shape=out_shape,
        in_specs=in_specs, out_specs=[tile] * n_out,
        scratch_shapes=[pltpu.VMEM((tm, tn), F32)] if nk > 1 else [],
        compiler_params=_cp(PAR, PAR, ARB),
    )(a, b, *extras)
    return outs[0] if n_out == 1 else outs


def _mm_nt(a, b, epi, extras, out_dtype, name):
    m, kdim = a.shape
    n = b.shape[0]
    tm, tn, tk = min(MM_T, m), min(MM_T, n), min(MM_T, kdim)
    nk = kdim // tk
    n_extra = len(extras)

    def finish(y, extra_refs, o_ref):
        if epi == "mul2":
            y = y * (2.0 * extra_refs[0][...].astype(F32))
        o_ref[...] = y.astype(o_ref.dtype)

    def body(a_ref, b_ref, *rest):
        extra_refs = rest[:n_extra]
        o_ref = rest[n_extra]
        part = _dot(a_ref[...].astype(_BF), b_ref[...].astype(_BF), 1, 1)
        if nk == 1:
            finish(part, extra_refs, o_ref)
        else:
            acc_ref = rest[-1]
            k = pl.program_id(2)

            @pl.when(k == 0)
            def _():
                acc_ref[...] = part

            @pl.when(k > 0)
            def _():
                acc_ref[...] += part

            @pl.when(k == nk - 1)
            def _():
                finish(acc_ref[...], extra_refs, o_ref)

    tile = pl.BlockSpec((tm, tn), lambda i, j, k: (i, j))
    in_specs = [pl.BlockSpec((tm, tk), lambda i, j, k: (i, k)), pl.BlockSpec((tn, tk), lambda i, j, k: (j, k))]
    in_specs += [tile] * n_extra
    return _pcall(
        body, name=name, grid=(m // tm, n // tn, nk), out_shape=jax.ShapeDtypeStruct((m, n), out_dtype),
        in_specs=in_specs, out_specs=tile,
        scratch_shapes=[pltpu.VMEM((tm, tn), F32)] if nk > 1 else [],
        compiler_params=_cp(PAR, PAR, ARB),
    )(a, b, *extras)


def _mm_tn(a, b, split, name):
    s, m = a.shape
    n = b.shape[1]
    ts = min(MM_T, s)
    ns = s // ts
    if split == "rows":
        tm, tn = min(MM_T, m // N_DEV), min(MM_T, n)
        per = (m // N_DEV) // tm
        out_shape = (N_DEV, m // N_DEV, n)
        out_spec = pl.BlockSpec((None, tm, tn), lambda i, j, k: (i // per, i % per, j))
    elif split == "cols":
        tm, tn = min(MM_T, m), min(MM_T, n // N_DEV)
        per = (n // N_DEV) // tn
        out_shape = (N_DEV, m, n // N_DEV)
        out_spec = pl.BlockSpec((None, tm, tn), lambda i, j, k: (j // per, i, j % per))
    else:
        tm, tn = min(MM_T, m), min(MM_T, n)
        out_shape = (m, n)
        out_spec = pl.BlockSpec((tm, tn), lambda i, j, k: (i, j))

    def body(a_ref, b_ref, o_ref, acc_ref):
        k = pl.program_id(2)
        part = _dot(a_ref[...].astype(_BF), b_ref[...].astype(_BF), 0, 0)

        @pl.when(k == 0)
        def _():
            acc_ref[...] = part

        @pl.when(k > 0)
        def _():
            acc_ref[...] += part

        @pl.when(k == ns - 1)
        def _():
            o_ref[...] = acc_ref[...].astype(o_ref.dtype)

    return _pcall(
        body, name=name, grid=(m // tm, n // tn, ns), out_shape=jax.ShapeDtypeStruct(out_shape, _XBF),
        in_specs=[pl.BlockSpec((ts, tm), lambda i, j, k: (k, i)), pl.BlockSpec((ts, tn), lambda i, j, k: (k, j))],
        out_specs=out_spec, scratch_shapes=[pltpu.VMEM((tm, tn), F32)],
        compiler_params=_cp(PAR, PAR, ARB),
    )(a, b)


def _head_masks():
    lane = _iota2((1, 128), 1)
    return [(lane // HEAD_DIM) == hh for hh in range(2)]


def _softplus_parts(z):
    sp = jnp.maximum(z, 0.0) + jnp.log(1.0 + jnp.exp(-jnp.abs(z)))
    return -sp, z - sp


def _sb_key_loop(i, carry, step):
    def cond(c):
        return jnp.logical_and(c[0] <= i, jnp.max(c[1][0]) > UNDERFLOW)

    def body(c):
        return c[0] + 1, step(i - c[0], c[1])

    return lax.while_loop(cond, body, (jnp.int32(1), carry))[1]


def _sb_fwd(proj, name):
    s = proj.shape[0]
    t = min(ATT_T, s)
    nblk = s // t
    scale = HEAD_DIM ** -0.5

    def body(q_ref, k_ref, v_ref, o_ref):
        i = pl.program_id(1)
        masks = _head_masks()
        row = _iota2((t, t), 0)
        col = _iota2((t, t), 1)
        u_strict = (row > col).astype(_BF)
        causal = col < row
        qs = q_ref[...] * scale
        out = jnp.zeros((t, 128), F32)
        for hh in range(2):
            qm = jnp.where(masks[hh], qs, 0.0).astype(_BF)

            def block(j, carry, diag):
                run, o = carry
                off = pl.multiple_of(j * t, t)
                kb = k_ref[pl.ds(off, t), :].astype(_BF)
                vb = v_ref[pl.ds(off, t), :].astype(_BF)
                z = _dot(qm, kb, 1, 1)
                l1, lb = _softplus_parts(z)
                if diag:
                    l1 = jnp.where(causal, l1, 0.0)
                between = run + _dot_exact(l1, u_strict, 1, 0, 2)
                a = jnp.exp(lb + between)
                if diag:
                    a = jnp.where(causal, a, 0.0)
                o = o + _dot_exact(a, vb, 1, 0, 2)
                run = run + jnp.sum(l1, axis=-1, keepdims=True)
                return run, o

            carry = block(i, (jnp.zeros((t, 1), F32), jnp.zeros((t, 128), F32)), True)
            carry = _sb_key_loop(i, carry, lambda j, c: block(j, c, False))
            out = jnp.where(masks[hh], carry[1], out)
        o_ref[...] = out

    kv_spec = lambda c0: pl.BlockSpec((s, 128), lambda p, i: (0, c0 // 128 + p))
    return _pcall(
        body, name=name, grid=(SB_HEADS // 2, nblk), out_shape=jax.ShapeDtypeStruct((s, 256), F32),
        in_specs=[pl.BlockSpec((t, 128), lambda p, i: (i, C_QA // 128 + p)), kv_spec(C_KA), kv_spec(C_VA)],
        out_specs=pl.BlockSpec((t, 128), lambda p, i: (i, p)), compiler_params=_cp(PAR, ARB),
    )(proj, proj, proj)


def _sb_bwd(proj, o, dmix, name):
    s = proj.shape[0]
    t = min(ATT_T, s)
    nblk = s // t
    scale = HEAD_DIM ** -0.5

    def body(q_ref, k_ref, v_ref, o_ref, do_ref, dq_ref, dk_ref, dv_ref):
        i = pl.program_id(1)
        masks = _head_masks()
        row = _iota2((t, t), 0)
        col = _iota2((t, t), 1)
        u_strict = (row > col).astype(_BF)
        u_incl = (row >= col).astype(_BF)
        causal = col < row

        @pl.when(i == 0)
        def _():
            dk_ref[...] = jnp.zeros_like(dk_ref)
            dv_ref[...] = jnp.zeros_like(dv_ref)

        qs = q_ref[...] * scale
        dov = do_ref[...]
        ov = o_ref[...]
        dq_out = jnp.zeros((t, 128), F32)
        for hh in range(2):
            qm = jnp.where(masks[hh], qs, 0.0).astype(_BF)
            dom = jnp.where(masks[hh], dov, 0.0).astype(_BF)
            dsum = jnp.sum(dom.astype(F32) * ov, axis=-1, keepdims=True)

            def block(j, carry, diag):
                run, erun, dq = carry
                off = pl.multiple_of(j * t, t)
                kb = k_ref[pl.ds(off, t), :].astype(_BF)
                vb = v_ref[pl.ds(off, t), :].astype(_BF)
                z = _dot(qm, kb, 1, 1)
                l1, lb = _softplus_parts(z)
                if diag:
                    l1 = jnp.where(causal, l1, 0.0)
                between = run + _dot_exact(l1, u_strict, 1, 0, 2)
                a = jnp.exp(lb + between)
                if diag:
                    a = jnp.where(causal, a, 0.0)
                g = a * _dot(dom, vb, 1, 1)
                cum = dsum - (erun + _dot_exact(g, u_incl, 1, 0, 2))
                beta = jnp.exp(lb)
                dz = g * (1.0 - beta) - cum * beta
                if diag:
                    dz = jnp.where(causal, dz, 0.0)
                dzb = dz.astype(_BF)
                dq = dq + _dot(dzb, kb, 1, 0)
                dk_ref[pl.ds(off, t), :] += _dot(dzb, qm, 0, 0)
                dv_ref[pl.ds(off, t), :] += _dot(a.astype(_BF), dom, 0, 0)
                run = run + jnp.sum(l1, axis=-1, keepdims=True)
                erun = erun + jnp.sum(g, axis=-1, keepdims=True)
                return run, erun, dq

            zero = jnp.zeros((t, 1), F32)
            carry = block(i, (zero, zero, jnp.zeros((t, 128), F32)), True)
            carry = _sb_key_loop(i, carry, lambda j, c: block(j, c, False))
            dq_out = jnp.where(masks[hh], carry[2] * scale, dq_out)
        dq_ref[...] = dq_out

    kv_spec = lambda c0: pl.BlockSpec((s, 128), lambda p, i: (0, c0 // 128 + p))
    blk = lambda c0: pl.BlockSpec((t, 128), lambda p, i: (i, c0 // 128 + p))
    acc = pl.BlockSpec((s, 128), lambda p, i: (0, p))
    shp = jax.ShapeDtypeStruct((s, 256), F32)
    return _pcall(
        body, name=name, grid=(SB_HEADS // 2, nblk), out_shape=[shp, shp, shp],
        in_specs=[blk(C_QA), kv_spec(C_KA), kv_spec(C_VA), blk(0), blk(M_SB)],
        out_specs=[blk(0), acc, acc], compiler_params=_cp(PAR, ARB),
    )(proj, proj, proj, o, dmix)


def _group_matrix(width):
    r = _iota2((width, width), 0) // HEAD_DIM
    c = _iota2((width, width), 1) // HEAD_DIM
    return (r == c).astype(_BF)


def _group_mean(v, gm):
    return _dot_exact(v, gm, 1, 0, 2) * (1.0 / HEAD_DIM)


def _fox_prep(proj, gq, gk, bf, name):
    s = proj.shape[0]
    t = min(ATT_T, s)
    nblk = s // t

    def body(q_ref, k_ref, fl_ref, gq_ref, gk_ref, bf_ref, qn_ref, kn_ref, cfc_ref, cfr_ref, carry_ref):
        i = pl.program_id(0)
        gm = _group_matrix(512)
        for src, gref, dst in ((q_ref, gq_ref, qn_ref), (k_ref, gk_ref, kn_ref)):
            v = src[...]
            rstd = lax.rsqrt(_group_mean(v * v, gm) + EPS)
            dst[...] = (v * rstd * gref[...]).astype(dst.dtype)

        @pl.when(i == 0)
        def _():
            carry_ref[...] = jnp.zeros_like(carry_ref)

        lane = _iota2((1, 128), 1)
        logf = jnp.where(lane < FOX_HEADS, _softplus_parts(-(fl_ref[...] + bf_ref[...]))[0], 0.0)
        lower = (_iota2((t, t), 0) >= _iota2((t, t), 1)).astype(_BF)
        cf = carry_ref[...] + _dot_exact(logf, lower, 1, 0, 3, v_left=False)
        cfc_ref[...] = cf
        cfr_ref[...] = cf.T[:FOX_HEADS, :]
        carry_ref[...] = cf[t - 1:t, :]

    col = lambda w, c0: pl.BlockSpec((t, w), lambda i: (i, c0 // w))
    v512 = pl.BlockSpec((1, 512), lambda i: (0, 0))
    return _pcall(
        body, name=name, grid=(nblk,),
        out_shape=[jax.ShapeDtypeStruct((s, 512), _BF), jax.ShapeDtypeStruct((s, 512), _BF),
                   jax.ShapeDtypeStruct((s, 128), F32), jax.ShapeDtypeStruct((FOX_HEADS, s), F32)],
        in_specs=[col(512, C_QB), col(512, C_KB), col(128, C_FL), v512, v512, pl.BlockSpec((1, 128), lambda i: (0, 0))],
        out_specs=[col(512, 0), col(512, 0), col(128, 0), pl.BlockSpec((FOX_HEADS, t), lambda i: (0, i))],
        scratch_shapes=[pltpu.VMEM((1, 128), F32)], compiler_params=_cp(ARB),
    )(proj, proj, proj, gq, gk, bf)


def _fox_cq(cfc_blk, head):
    lane = _iota2((1, 128), 1)
    return jnp.sum(jnp.where(lane == head, cfc_blk, 0.0), axis=-1, keepdims=True)


def _fox_ck(cfr_ref, off, t, head):
    sub = _iota2((FOX_HEADS, 1), 0)
    return jnp.sum(jnp.where(sub == head, cfr_ref[:, pl.ds(off, t)], 0.0), axis=0, keepdims=True)


def _fox_last_key(cfr_ref, j, t, head):
    return jnp.min(_fox_ck(cfr_ref, pl.multiple_of(j * t, t), t, head))


def _fox_fwd(proj, qn, kn, cfc, cfr, qk_bound, name):
    s = proj.shape[0]
    t = min(ATT_T, s)
    nblk = s // t
    scale = HEAD_DIM ** -0.5

    def body(q_ref, k_ref, v_ref, cfc_ref, cfr_ref, qkb_ref, o_ref, lse_ref):
        p = pl.program_id(0)
        i = pl.program_id(1)
        masks = _head_masks()
        lane = _iota2((1, 128), 1)
        valid = _iota2((t, t), 1) <= _iota2((t, t), 0)
        qv = q_ref[...]
        cfc_blk = cfc_ref[...]
        out = jnp.zeros((t, 128), F32)
        lse_out = jnp.zeros((t, 128), F32)
        for hh in range(2):
            head = 2 * p + hh
            qm = (jnp.where(masks[hh], qv, 0) * scale).astype(_BF)
            cq = _fox_cq(cfc_blk, head)

            def block(j, carry, diag):
                m, l, acc = carry
                off = pl.multiple_of(j * t, t)
                kb = k_ref[pl.ds(off, t), :]
                vb = v_ref[pl.ds(off, t), :].astype(_BF)
                z = _dot(qm, kb, 1, 1) + (cq - _fox_ck(cfr_ref, off, t, head))
                if diag:
                    z = jnp.where(valid, z, NEG)
                m_new = jnp.maximum(m, jnp.max(z, axis=-1, keepdims=True))
                pe = jnp.exp(z - m_new)
                alpha = jnp.exp(m - m_new)
                l = alpha * l + jnp.sum(pe, axis=-1, keepdims=True)
                acc = alpha * acc + _dot_exact(pe, vb, 1, 0, 2)
                return m_new, l, acc

            def cond(c):
                reach = jnp.max(qkb_ref[...] + cq - c[1][0]) - _fox_last_key(cfr_ref, jnp.maximum(i - c[0], 0), t, head)
                return jnp.logical_and(c[0] <= i, reach > UNDERFLOW)

            init = (jnp.full((t, 1), NEG, F32), jnp.zeros((t, 1), F32), jnp.zeros((t, 128), F32))
            carry = block(i, init, True)
            m, l, acc = lax.while_loop(cond, lambda c: (c[0] + 1, block(i - c[0], c[1], False)),
                                       (jnp.int32(1), carry))[1]
            out = jnp.where(masks[hh], acc / l, out)
            lse_out = jnp.where(lane == hh, m + jnp.log(l), lse_out)
        o_ref[...] = out
        lse_ref[...] = lse_out

    return _pcall(
        body, name=name, grid=(FOX_HEADS // 2, nblk),
        out_shape=[jax.ShapeDtypeStruct((s, 512), F32), jax.ShapeDtypeStruct((FOX_HEADS // 2, s, 128), F32)],
        in_specs=[pl.BlockSpec((t, 128), lambda p, i: (i, p)), pl.BlockSpec((s, 128), lambda p, i: (0, p)),
                  pl.BlockSpec((s, 128), lambda p, i: (0, C_VB // 128 + p)),
                  pl.BlockSpec((t, 128), lambda p, i: (i, 0)), pl.BlockSpec((FOX_HEADS, s), lambda p, i: (0, 0)),
                  pl.BlockSpec((1, 1), lambda p, i: (0, 0))],
        out_specs=[pl.BlockSpec((t, 128), lambda p, i: (i, p)), pl.BlockSpec((None, t, 128), lambda p, i: (p, i, 0))],
        compiler_params=_cp(PAR, ARB),
    )(qn, kn, proj, cfc, cfr, qk_bound)


def _fox_bwd(proj, qn, kn, cfc, cfr, qk_bound, o, lse, dmix, name):
    s = proj.shape[0]
    t = min(ATT_T, s)
    nblk = s // t
    scale = HEAD_DIM ** -0.5

    def body(q_ref, k_ref, v_ref, cfc_ref, cfr_ref, qkb_ref, o_ref, lse_ref, do_ref, dq_ref, dk_ref, dv_ref, dcr_ref):
        p = pl.program_id(0)
        i = pl.program_id(1)
        masks = _head_masks()
        lane = _iota2((1, 128), 1)
        valid = _iota2((t, t), 1) <= _iota2((t, t), 0)
        before = (_iota2((t, t), 0) < _iota2((t, t), 1)).astype(_BF)

        @pl.when(i == 0)
        def _():
            dk_ref[...] = jnp.zeros_like(dk_ref)
            dv_ref[...] = jnp.zeros_like(dv_ref)
            dcr_ref[...] = jnp.zeros_like(dcr_ref)

        qv = q_ref[...]
        cfc_blk = cfc_ref[...]
        dov = do_ref[...]
        ov = o_ref[...]
        lse_blk = lse_ref[...]
        dq_out = jnp.zeros((t, 128), F32)
        for hh in range(2):
            head = 2 * p + hh
            qm = (jnp.where(masks[hh], qv, 0) * scale).astype(_BF)
            dom = jnp.where(masks[hh], dov, 0.0).astype(_BF)
            dsum = jnp.sum(dom.astype(F32) * ov, axis=-1, keepdims=True)
            lse_h = jnp.sum(jnp.where(lane == hh, lse_blk, 0.0), axis=-1, keepdims=True)
            cq = _fox_cq(cfc_blk, head)
            reach = jnp.max(qkb_ref[...] + cq - lse_h)
            first = lax.while_loop(
                lambda j: jnp.logical_and(j > 0, reach - _fox_last_key(cfr_ref, jnp.maximum(j - 1, 0), t, head) > UNDERFLOW),
                lambda j: j - 1, i)

            def block(j, carry, diag):
                dq, pre = carry
                off = pl.multiple_of(j * t, t)
                kb = k_ref[pl.ds(off, t), :]
                vb = v_ref[pl.ds(off, t), :].astype(_BF)
                z = _dot(qm, kb, 1, 1) + (cq - _fox_ck(cfr_ref, off, t, head))
                pm = jnp.exp(z - lse_h)
                if diag:
                    pm = jnp.where(valid, pm, 0.0)
                ds = pm * (_dot(dom, vb, 1, 1) - dsum)
                dsb = ds.astype(_BF)
                dq = dq + _dot(dsb, kb, 1, 0)
                dk_ref[pl.ds(off, t), :] += _dot(dsb, qm, 0, 0)
                dv_ref[pl.ds(off, t), :] += _dot(pm.astype(_BF), dom, 0, 0)
                lower_keys = pre + _dot_exact(ds, before, 1, 0, 2)
                if diag:
                    lower_keys = jnp.where(valid, lower_keys, 0.0)
                dcr_ref[hh:hh + 1, pl.ds(off, t)] += jnp.sum(lower_keys, axis=0, keepdims=True)
                return dq, pre + jnp.sum(ds, axis=-1, keepdims=True)

            carry = (jnp.zeros((t, 128), F32), jnp.zeros((t, 1), F32))
            carry = lax.fori_loop(first, i, lambda j, c: block(j, c, False), carry)
            dq, _ = block(i, carry, True)
            dq_out = jnp.where(masks[hh], dq * scale, dq_out)
        dq_ref[...] = dq_out

    blk = lambda c0: pl.BlockSpec((t, 128), lambda p, i: (i, c0 // 128 + p))
    res = lambda c0: pl.BlockSpec((s, 128), lambda p, i: (0, c0 // 128 + p))
    shp = jax.ShapeDtypeStruct((s, 512), F32)
    return _pcall(
        body, name=name, grid=(FOX_HEADS // 2, nblk),
        out_shape=[shp, shp, shp, jax.ShapeDtypeStruct((FOX_HEADS // 2, 8, s), F32)],
        in_specs=[blk(0), res(0), res(C_VB), pl.BlockSpec((t, 128), lambda p, i: (i, 0)),
                  pl.BlockSpec((FOX_HEADS, s), lambda p, i: (0, 0)), pl.BlockSpec((1, 1), lambda p, i: (0, 0)), blk(0),
                  pl.BlockSpec((None, t, 128), lambda p, i: (p, i, 0)), blk(M_FOX)],
        out_specs=[blk(0), res(0), res(0), pl.BlockSpec((None, 8, s), lambda p, i: (p, 0, 0))],
        compiler_params=_cp(PAR, ARB),
    )(qn, kn, proj, cfc, cfr, qk_bound, o, lse, dmix)


def _fox_prep_bwd(proj, gq, gk, bf, dqn, dkn, dlogf, name):
    s = proj.shape[0]
    t = min(ATT_T, s)
    nblk = s // t

    def body(q_ref, k_ref, fl_ref, gq_ref, gk_ref, bf_ref, dqn_ref, dkn_ref, dlogf_ref,
             dq_ref, dk_ref, dfl_ref, dgq_ref, dgk_ref, dbf_ref):
        i = pl.program_id(0)
        gm = _group_matrix(512)

        @pl.when(i == 0)
        def _():
            dgq_ref[...] = jnp.zeros_like(dgq_ref)
            dgk_ref[...] = jnp.zeros_like(dgk_ref)
            dbf_ref[...] = jnp.zeros_like(dbf_ref)

        for src, gref, dyref, dst, dgref in ((q_ref, gq_ref, dqn_ref, dq_ref, dgq_ref),
                                              (k_ref, gk_ref, dkn_ref, dk_ref, dgk_ref)):
            v = src[...]
            dy = dyref[...]
            rstd = lax.rsqrt(_group_mean(v * v, gm) + EPS)
            xhat = v * rstd
            dgref[...] += jnp.sum(dy * xhat, axis=0, keepdims=True)
            dxh = dy * gref[...]
            dst[...] = (rstd * (dxh - xhat * _group_mean(dxh * xhat, gm))).astype(dst.dtype)

        xv = fl_ref[...] + bf_ref[...]
        e = jnp.exp(-jnp.abs(xv))
        dfl = dlogf_ref[...] * (jnp.where(xv >= 0.0, e, 1.0) / (1.0 + e))
        dfl_ref[...] = dfl.astype(dfl_ref.dtype)
        dbf_ref[...] += jnp.sum(dfl, axis=0, keepdims=True)

    col = lambda w, c0: pl.BlockSpec((t, w), lambda i: (i, c0 // w))
    v512 = pl.BlockSpec((1, 512), lambda i: (0, 0))
    v128 = pl.BlockSpec((1, 128), lambda i: (0, 0))
    return _pcall(
        body, name=name, grid=(nblk,),
        out_shape=[jax.ShapeDtypeStruct((s, 512), _BF), jax.ShapeDtypeStruct((s, 512), _BF),
                   jax.ShapeDtypeStruct((s, 128), _BF), jax.ShapeDtypeStruct((1, 512), F32),
                   jax.ShapeDtypeStruct((1, 512), F32), jax.ShapeDtypeStruct((1, 128), F32)],
        in_specs=[col(512, C_QB), col(512, C_KB), col(128, C_FL), v512, v512, v128, col(512, 0), col(512, 0), col(128, 0)],
        out_specs=[col(512, 0), col(512, 0), col(128, 0), v512, v512, v128], compiler_params=_cp(ARB),
    )(proj, proj, proj, gq, gk, bf, dqn, dkn, dlogf)


GELU_C = 0.7978845608028654
GELU_A = 0.044715


def _gelu(x):
    return 0.5 * x * (1.0 + jnp.tanh(GELU_C * (x + GELU_A * x * x * x)))


def _gelu_grad(x):
    th = jnp.tanh(GELU_C * (x + GELU_A * x * x * x))
    return 0.5 * (1.0 + th) + 0.5 * x * (1.0 - th * th) * (GELU_C * (1.0 + 3.0 * GELU_A * x * x))


def _sgu_tril():
    return _iota2((SGU_CHUNK, SGU_CHUNK), 0) >= _iota2((SGU_CHUNK, SGU_CHUNK), 1)


def _sgu_group_masks():
    lane = _iota2((1, 256), 1)
    return [(lane // HEAD_DIM) == g for g in range(SGU_GROUPS)]


def _sgu_fwd(proj, w, gs, bexp, name):
    s = proj.shape[0]
    t = min(2 * SGU_CHUNK, s)

    def body(u_ref, v_ref, w_ref, gs_ref, b_ref, o_ref):
        gm = _group_matrix(256)
        gmask = _sgu_group_masks()
        tril = _sgu_tril()
        u = _gelu(u_ref[...])
        vg = _gelu(v_ref[...])
        vhat = (vg * lax.rsqrt(_group_mean(vg * vg, gm) + EPS) * gs_ref[...]).astype(_BF)
        for ch in range(t // SGU_CHUNK):
            rows = slice(ch * SGU_CHUNK, (ch + 1) * SGU_CHUNK)
            mixed = b_ref[...]
            for g in range(SGU_GROUPS):
                wg = jnp.where(tril, w_ref[g], 0.0).astype(_BF)
                mixed = jnp.where(gmask[g], mixed + _dot(wg, vhat[rows], 1, 0), mixed)
            o_ref[rows, :] = u[rows] * mixed

    return _pcall(
        body, name=name, grid=(s // t,), out_shape=jax.ShapeDtypeStruct((s, 256), F32),
        in_specs=[_row_spec(t, 256, C_UC // 256), _row_spec(t, 256, C_VC // 256),
                  pl.BlockSpec((SGU_GROUPS, SGU_CHUNK, SGU_CHUNK), lambda i: (0, 0, 0)), _vec_spec(256),
                  pl.BlockSpec((SGU_CHUNK, 256), lambda i: (0, 0))],
        out_specs=_row_spec(t, 256), compiler_params=_cp(PAR),
    )(proj, proj, w, gs, bexp)


def _sgu_bwd(proj, w, gs, bexp, dmix, name):
    s = proj.shape[0]
    t = min(2 * SGU_CHUNK, s)
    nstep = s // t

    def body(u_ref, v_ref, w_ref, gs_ref, b_ref, do_ref, du_ref, dv_ref, dw_ref, db_ref, dgs_ref):
        i = pl.program_id(0)
        gm = _group_matrix(256)
        gmask = _sgu_group_masks()
        tril = _sgu_tril()

        @pl.when(i == 0)
        def _():
            dw_ref[...] = jnp.zeros_like(dw_ref)
            db_ref[...] = jnp.zeros_like(db_ref)
            dgs_ref[...] = jnp.zeros_like(dgs_ref)

        uc = u_ref[...]
        vc = v_ref[...]
        u = _gelu(uc)
        vg = _gelu(vc)
        rstd = lax.rsqrt(_group_mean(vg * vg, gm) + EPS)
        xh = vg * rstd
        gsv = gs_ref[...]
        vhat = (xh * gsv).astype(_BF)
        dov = do_ref[...]
        dm = dov * u
        for ch in range(t // SGU_CHUNK):
            rows = slice(ch * SGU_CHUNK, (ch + 1) * SGU_CHUNK)
            mixed = b_ref[...]
            dvh = jnp.zeros((SGU_CHUNK, 256), F32)
            dmc = dm[rows]
            for g in range(SGU_GROUPS):
                wg = jnp.where(tril, w_ref[g], 0.0).astype(_BF)
                mixed = jnp.where(gmask[g], mixed + _dot(wg, vhat[rows], 1, 0), mixed)
                dvh = jnp.where(gmask[g], _dot(wg, dmc.astype(_BF), 0, 0), dvh)
                dw_ref[g] += _dot(jnp.where(gmask[g], dmc, 0.0).astype(_BF), vhat[rows], 1, 1)
            db_ref[...] += dmc
            du_ref[rows, :] = (dov[rows] * mixed * _gelu_grad(uc[rows])).astype(du_ref.dtype)
            xhc = xh[rows]
            dgs_ref[...] += jnp.sum(dvh * xhc, axis=0, keepdims=True)
            dxh = dvh * gsv
            dvg = rstd[rows] * (dxh - xhc * _group_mean(dxh * xhc, gm))
            dv_ref[rows, :] = (dvg * _gelu_grad(vc[rows])).astype(dv_ref.dtype)

        @pl.when(i == nstep - 1)
        def _():
            for g in range(SGU_GROUPS):
                dw_ref[g] = jnp.where(tril, dw_ref[g], 0.0)

    wspec = pl.BlockSpec((SGU_GROUPS, SGU_CHUNK, SGU_CHUNK), lambda i: (0, 0, 0))
    bspec = pl.BlockSpec((SGU_CHUNK, 256), lambda i: (0, 0))
    return _pcall(
        body, name=name, grid=(nstep,),
        out_shape=[jax.ShapeDtypeStruct((s, 256), _BF), jax.ShapeDtypeStruct((s, 256), _BF),
                   jax.ShapeDtypeStruct((SGU_GROUPS, SGU_CHUNK, SGU_CHUNK), F32),
                   jax.ShapeDtypeStruct((SGU_CHUNK, 256), F32), jax.ShapeDtypeStruct((1, 256), F32)],
        in_specs=[_row_spec(t, 256, C_UC // 256), _row_spec(t, 256, C_VC // 256), wspec, _vec_spec(256), bspec,
                  _row_spec(t, 256, M_SGU // 256)],
        out_specs=[_row_spec(t, 256), _row_spec(t, 256), wspec, bspec, _vec_spec(256)],
        compiler_params=_cp(ARB),
    )(proj, proj, w, gs, bexp, dmix)


def _ada_fwd(c_all, ada_w, name):
    depth, d, n = ada_w.shape

    def body(c_ref, w_ref, o_ref):
        cv = c_ref[...]
        cond = cv / (1.0 + jnp.exp(-cv))
        o_ref[...] = _dot_f32(cond, w_ref[...], 1, 0)

    return _pcall(
        body, name=name, grid=(depth,), out_shape=jax.ShapeDtypeStruct((depth, N_DEV, n), F32),
        in_specs=[pl.BlockSpec((N_DEV, d), lambda l: (0, 0)), pl.BlockSpec((None, d, n), lambda l: (l, 0, 0))],
        out_specs=pl.BlockSpec((None, N_DEV, n), lambda l: (l, 0, 0)), compiler_params=_cp(PAR),
    )(c_all, ada_w)


def _ada_bwd(c_all, dmod, name):
    depth, _, n = dmod.shape
    d = c_all.shape[1]

    def body(c_ref, dm_ref, o_ref):
        cv = c_ref[...]
        cond = cv / (1.0 + jnp.exp(-cv))
        o_ref[...] = _dot_f32(cond, dm_ref[...], 0, 0)

    return _pcall(
        body, name=name, grid=(depth,), out_shape=jax.ShapeDtypeStruct((depth, d, n), F32),
        in_specs=[pl.BlockSpec((N_DEV, d), lambda l: (0, 0)), pl.BlockSpec((None, N_DEV, n), lambda l: (l, 0, 0))],
        out_specs=pl.BlockSpec((None, d, n), lambda l: (l, 0, 0)), compiler_params=_cp(PAR),
    )(c_all, dmod)


def _adamw(slots, w, m, v, name):
    n, r, c = slots.shape
    tr = 256 if r % 256 == 0 else r
    bc1 = 1.0 - ADAM_B1 ** ADAM_STEP
    bc2 = 1.0 - ADAM_B2 ** ADAM_STEP

    def body(s_ref, w_ref, m_ref, v_ref, g_ref, d_ref, nm_ref, nv_ref):
        g = s_ref[0].astype(F32)
        for j in range(1, n):
            g = g + s_ref[j].astype(F32)
        m_new = ADAM_B1 * m_ref[...] + (1.0 - ADAM_B1) * g
        v_new = ADAM_B2 * v_ref[...] + (1.0 - ADAM_B2) * (g * g)
        g_ref[...] = g
        nm_ref[...] = m_new
        nv_ref[...] = v_new
        d_ref[...] = -ADAM_LR * ((m_new / bc1) / (jnp.sqrt(v_new / bc2) + ADAM_EPS) + ADAM_WD * w_ref[...])

    tile = pl.BlockSpec((tr, c), lambda i: (i, 0))
    shp = jax.ShapeDtypeStruct((r, c), F32)
    return _pcall(
        body, name=name, grid=(r // tr,), out_shape=[shp] * 4,
        in_specs=[pl.BlockSpec((n, tr, c), lambda i: (0, i, 0)), tile, tile, tile],
        out_specs=[tile] * 4, compiler_params=_cp(PAR),
    )(slots, w, m, v)


def _pad_cols(a, width):
    return jnp.pad(a, ((0, 0), (0, width - a.shape[1])))


def _w_in_layout(w):
    pad = jnp.zeros(w.shape[:-1] + (IN_P - IN_W,), w.dtype)
    return jnp.concatenate([w[..., 768:FL_SRC], w[..., :768], w[..., UC_SRC:], w[..., FL_SRC:UC_SRC], pad], axis=-1)


def _w_in_unlayout(g):
    return jnp.concatenate([g[..., C_QA:C_UC], g[..., :C_QA], g[..., C_FL:C_FL + FOX_HEADS], g[..., C_UC:C_FL]], axis=-1)


SMALL = [("ada_b", DEPTH * 6 * D_MODEL), ("norm1_g", DEPTH * D_MODEL), ("norm2_g", DEPTH * D_MODEL),
         ("sgu_w", DEPTH * SGU_GROUPS * SGU_CHUNK * SGU_CHUNK), ("sgu_b", DEPTH * SGU_GROUPS * SGU_CHUNK),
         ("sgu_norm_g", DEPTH * SGU_GROUPS * HEAD_DIM), ("q_norm_g", DEPTH * HEAD_DIM), ("k_norm_g", DEPTH * HEAD_DIM),
         ("b_forget", DEPTH * FOX_HEADS), ("loss", 1)]
SMALL_ROWS = 2560


def _pack_small(parts):
    flat = jnp.concatenate([parts[name].reshape(-1).astype(F32) for name, _ in SMALL])
    return jnp.pad(flat, (0, SMALL_ROWS * 128 - flat.shape[0])).reshape(SMALL_ROWS, 128)


def _unpack_small(packed, shapes):
    flat = packed.reshape(-1)
    out, off = {}, 0
    for name, size in SMALL:
        out[name] = flat[off:off + size].reshape(shapes[name])
        off += size
    return out


def kernel(x, c, ada_w, ada_b, norm1_g, norm2_g, w_in, b_forget, q_norm_g, k_norm_g, sgu_norm_g, sgu_w, sgu_b, w_out, mlp_w1, mlp_w2, loss_target, m_ada_w, m_ada_b, m_norm1_g, m_norm2_g, m_w_in, m_b_forget, m_q_norm_g, m_k_norm_g, m_sgu_norm_g, m_sgu_w, m_sgu_b, m_w_out, m_mlp_w1, m_mlp_w2, v_ada_w, v_ada_b, v_norm1_g, v_norm2_g, v_w_in, v_b_forget, v_q_norm_g, v_k_norm_g, v_sgu_norm_g, v_sgu_w, v_sgu_b, v_w_out, v_mlp_w1, v_mlp_w2):
    me = _lin(_my_pos())
    x0 = x[0]
    target = loss_target[0]
    n_ada = ada_w.shape[2]

    w_in_g, w_out_g, w1_g, w2_g, c_all = _exchange(
        [w_in.astype(_XBF), w_out.astype(_XBF), mlp_w1.astype(_XBF), mlp_w2.astype(_XBF), c], False, "gather_weights")
    c_all = c_all.reshape(N_DEV, D_MODEL)
    w_in_full = _w_in_layout(jnp.transpose(w_in_g, (1, 2, 0, 3)).reshape(DEPTH, D_MODEL, IN_W))
    w_out_full = jnp.transpose(w_out_g, (1, 0, 2, 3)).reshape(DEPTH, D_MODEL, D_MODEL)
    w1_full = jnp.transpose(w1_g, (1, 2, 0, 3)).reshape(DEPTH, D_MODEL, D_FF)
    w2_full = jnp.transpose(w2_g, (1, 0, 2, 3)).reshape(DEPTH, D_FF, D_MODEL)

    mod_part = _ada_fwd(c_all, ada_w, "ada_fwd")
    (mod_rows,) = _exchange([jnp.transpose(mod_part, (1, 0, 2))], True, "scatter_mod")
    mod = jnp.transpose(mod_rows, (1, 0, 2)).reshape(DEPTH, 6 * D_MODEL) + ada_b
    mods = mod.reshape(DEPTH, 6, 1, D_MODEL)

    saved = []
    xl = x0
    for l in range(DEPTH):
        sh1, sc1, g1, sh2, sc2, g2 = (mods[l, r] for r in range(6))
        n1 = norm1_g[l].reshape(1, D_MODEL)
        n2 = norm2_g[l].reshape(1, D_MODEL)
        gq = jnp.tile(q_norm_g[l], FOX_HEADS).reshape(1, 512)
        gk = jnp.tile(k_norm_g[l], FOX_HEADS).reshape(1, 512)
        bf = jnp.pad(b_forget[l], (0, 128 - FOX_HEADS)).reshape(1, 128)
        gs = sgu_norm_g[l].reshape(1, 256)
        bexp = jnp.repeat(sgu_b[l].T, HEAD_DIM, axis=1)

        h1 = _norm_mod(xl, n1, sc1, sh1, "norm_mod")
        proj = _mm_nn(h1, w_in_full[l], "plain", (), "mm_in")
        o_sb = _sb_fwd(proj, "sb_fwd")
        qn, kn, cfc, cfr = _fox_prep(proj, gq, gk, bf, "fox_prep")
        qkb = (1.02 * HEAD_DIM ** 0.5 * jnp.max(jnp.abs(q_norm_g[l])) * jnp.max(jnp.abs(k_norm_g[l]))).reshape(1, 1)
        o_fox, lse = _fox_fwd(proj, qn, kn, cfc, cfr, qkb, "fox_fwd")
        o_sgu = _sgu_fwd(proj, sgu_w[l], gs, bexp, "sgu_fwd")
        mixed = jnp.concatenate([o_sb, o_fox, o_sgu], axis=-1).astype(_BF)
        x_mid, y1 = _mm_nn(mixed, w_out_full[l], "resid", (xl, g1), "mm_out")
        h2 = _norm_mod(x_mid, n2, sc2, sh2, "norm_mod")
        r2, r1 = _mm_nn(h2, w1_full[l], "relu2", (), "mm_w1")
        x_out, y2 = _mm_nn(r2, w2_full[l], "resid", (x_mid, g2), "mm_w2")
        saved.append(dict(x_in=xl, h1=h1, proj=proj, o_sb=o_sb, qn=qn, kn=kn, cfc=cfc, cfr=cfr, o_fox=o_fox, lse=lse,
                          mixed=mixed, x_mid=x_mid, y1=y1, h2=h2, r2=r2, r1=r1, y2=y2,
                          n1=n1, n2=n2, gq=gq, gk=gk, bf=bf, gs=gs, bexp=bexp, qkb=qkb))
        xl = x_out

    loss_part, dx = _loss_head(xl, target, "loss_head")

    g_in, g_out, g_w1, g_w2 = [None] * DEPTH, [None] * DEPTH, [None] * DEPTH, [None] * DEPTH
    small = {k: [None] * DEPTH for k in ("mod", "norm1_g", "norm2_g", "sgu_w", "sgu_b", "sgu_norm_g",
                                         "q_norm_g", "k_norm_g", "b_forget")}
    for l in reversed(range(DEPTH)):
        sv = saved[l]
        sh1, sc1, g1, sh2, sc2, g2 = (mods[l, r] for r in range(6))
        dy2, dg2 = _res_bwd(dx, sv["y2"], g2, "res_bwd")
        da = _mm_nt(dy2, w2_full[l], "mul2", (sv["r1"],), _BF, "mm_w2_bwd")
        g_w2[l] = _mm_tn(sv["r2"], dy2, "rows", "mm_w2_grad")
        dh2 = _mm_nt(da, w1_full[l], "plain", (), F32, "mm_w1_bwd")
        g_w1[l] = _mm_tn(sv["h2"], da, "cols", "mm_w1_grad")
        dx_mid, dsh2, dsc2, dn2 = _norm_mod_bwd(sv["x_mid"], dh2, dx, sv["n2"], sc2, "norm_mod_bwd")
        dy1, dg1 = _res_bwd(dx_mid, sv["y1"], g1, "res_bwd")
        dmix = _mm_nt(dy1, w_out_full[l], "plain", (), F32, "mm_out_bwd")
        g_out[l] = _mm_tn(sv["mixed"], dy1, "rows", "mm_out_grad")
        dq_sb, dk_sb, dv_sb = _sb_bwd(sv["proj"], sv["o_sb"], dmix, "sb_bwd")
        dqn, dkn, dv_fox, dcr = _fox_bwd(sv["proj"], sv["qn"], sv["kn"], sv["cfc"], sv["cfr"], sv["qkb"], sv["o_fox"],
                                         sv["lse"], dmix, "fox_bwd")
        dlogf = _pad_cols(dcr[:, :2, :].reshape(FOX_HEADS, -1).T, 128)
        dq_fox, dk_fox, dfl, dgq, dgk, dbf = _fox_prep_bwd(sv["proj"], sv["gq"], sv["gk"], sv["bf"], dqn, dkn, dlogf,
                                                           "fox_prep_bwd")
        duc, dvc, dsw, dsb, dsg = _sgu_bwd(sv["proj"], sgu_w[l], sv["gs"], sv["bexp"], dmix, "sgu_bwd")
        dproj = jnp.concatenate(
            [dq_fox, dk_fox, dv_fox.astype(_BF), dq_sb.astype(_BF), dk_sb.astype(_BF), dv_sb.astype(_BF), duc, dvc, dfl,
             jnp.zeros((dfl.shape[0], IN_P - C_FL - 128), _BF)], axis=-1)
        dh1 = _mm_nt(dproj, w_in_full[l], "plain", (), F32, "mm_in_bwd")
        g_in[l] = _mm_tn(sv["h1"], dproj, None, "mm_in_grad")
        dx, dsh1, dsc1, dn1 = _norm_mod_bwd(sv["x_in"], dh1, dx_mid, sv["n1"], sc1, "norm_mod_bwd")

        small["mod"][l] = jnp.concatenate([dsh1, dsc1, dg1, dsh2, dsc2, dg2], axis=-1).reshape(-1)
        small["norm1_g"][l] = dn1.reshape(-1)
        small["norm2_g"][l] = dn2.reshape(-1)
        small["sgu_w"][l] = dsw
        small["sgu_b"][l] = dsb.reshape(SGU_CHUNK, SGU_GROUPS, HEAD_DIM).sum(-1).T
        small["sgu_norm_g"][l] = dsg.reshape(SGU_GROUPS, HEAD_DIM)
        small["q_norm_g"][l] = dgq.reshape(FOX_HEADS, HEAD_DIM).sum(0)
        small["k_norm_g"][l] = dgk.reshape(FOX_HEADS, HEAD_DIM).sum(0)
        small["b_forget"][l] = dbf[0, :FOX_HEADS]

    parts = {k: jnp.stack(v) for k, v in small.items()}
    parts["ada_b"] = parts.pop("mod")
    parts["loss"] = loss_part
    (small_all,) = _exchange([_pack_small(parts)], False, "gather_small")
    zero1 = jnp.zeros((1,), F32)
    packs = [_pack_small(dict(ada_b=a, norm1_g=b, norm2_g=cc, sgu_w=d, sgu_b=e, sgu_norm_g=f, q_norm_g=g, k_norm_g=h,
                              b_forget=i, loss=zero1))
             for a, b, cc, d, e, f, g, h, i in (
                 (ada_b, norm1_g, norm2_g, sgu_w, sgu_b, sgu_norm_g, q_norm_g, k_norm_g, b_forget),
                 (m_ada_b, m_norm1_g, m_norm2_g, m_sgu_w, m_sgu_b, m_sgu_norm_g, m_q_norm_g, m_k_norm_g, m_b_forget),
                 (v_ada_b, v_norm1_g, v_norm2_g, v_sgu_w, v_sgu_b, v_sgu_norm_g, v_q_norm_g, v_k_norm_g, v_b_forget))]
    small_shapes = dict(ada_b=ada_b.shape, norm1_g=norm1_g.shape, norm2_g=norm2_g.shape, sgu_w=sgu_w.shape,
                        sgu_b=sgu_b.shape, sgu_norm_g=sgu_norm_g.shape, q_norm_g=q_norm_g.shape,
                        k_norm_g=k_norm_g.shape, b_forget=b_forget.shape, loss=())
    s_out = [_unpack_small(a, small_shapes) for a in _adamw(small_all, *packs, "adamw_small")]

    dmod_all = small_all[:, :DEPTH * 6 * D_MODEL // 128, :].reshape(N_DEV, DEPTH, 6 * D_MODEL)
    dmod_mine = lax.dynamic_slice_in_dim(dmod_all, me * n_ada, n_ada, axis=2)
    g_ada = _ada_bwd(c_all, jnp.transpose(dmod_mine, (1, 0, 2)), "ada_bwd")
    ada_out = _adamw(g_ada.reshape(1, DEPTH * D_MODEL, n_ada), ada_w.reshape(-1, n_ada), m_ada_w.reshape(-1, n_ada),
                     v_ada_w.reshape(-1, n_ada), "adamw_ada")
    ada_out = [a.reshape(ada_w.shape) for a in ada_out]

    gi = jnp.stack([_w_in_unlayout(g) for g in g_in])
    gi = jnp.transpose(gi.reshape(DEPTH, D_MODEL, N_DEV, IN_W // N_DEV), (2, 0, 1, 3))
    go = jnp.stack(g_out, axis=1)
    gw1 = jnp.stack(g_w1, axis=1)
    gw2 = jnp.stack(g_w2, axis=1)
    ri, ro, r1, r2 = _exchange([gi, go, gw1, gw2], True, "scatter_grads")

    def big(slots, w, m, v, name):
        cdim = w.shape[-1]
        outs = _adamw(slots.reshape(N_DEV, -1, cdim), w.reshape(-1, cdim), m.reshape(-1, cdim), v.reshape(-1, cdim), name)
        return [a.reshape(w.shape) for a in outs]

    in_out = big(ri, w_in, m_w_in, v_w_in, "adamw_in")
    out_out = big(ro, w_out, m_w_out, v_w_out, "adamw_out")
    w1_out = big(r1, mlp_w1, m_mlp_w1, v_mlp_w1, "adamw_w1")
    w2_out = big(r2, mlp_w2, m_mlp_w2, v_mlp_w2, "adamw_w2")

    def leaves(idx):
        sm = s_out[idx]
        return [ada_out[idx], sm["ada_b"], sm["norm1_g"], sm["norm2_g"], in_out[idx], sm["b_forget"], sm["q_norm_g"],
                sm["k_norm_g"], sm["sgu_norm_g"], sm["sgu_w"], sm["sgu_b"], out_out[idx], w1_out[idx], w2_out[idx]]

    loss = s_out[0]["loss"]
    grad_x = dx.reshape(x.shape)
    return (loss, grad_x, *leaves(0), *leaves(1), *leaves(2), *leaves(3))
```

```python
import jax
import jax.numpy as jnp
from jax import lax
from jax.experimental import pallas as pl
from jax.experimental.pallas import tpu as pltpu

F32 = jnp.float32
_BF = jnp.bfloat16
_XBF = jnp.bfloat16

N_DEV = 8
D_MODEL = 1024
DEPTH = 4
HEAD_DIM = 64
SB_HEADS = 4
FOX_HEADS = 8
SGU_GROUPS = 4
SGU_CHUNK = 128
D_FF = 4096
EPS = 1e-6
IN_W = 2824
FL_SRC = 2304
UC_SRC = 2312

C_QB, C_KB, C_VB = 0, 512, 1024
C_QA, C_KA, C_VA = 1536, 1792, 2048
C_UC, C_VC = 2304, 2560
C_FL = 2816
IN_P = 3072
M_SB, M_FOX, M_SGU = 0, 256, 768

ADAM_LR = 0.001
ADAM_B1 = 0.9
ADAM_B2 = 0.999
ADAM_EPS = 1e-08
ADAM_WD = 0.01
ADAM_STEP = 10

ATT_T = 256
ROW_T = 512
MM_T = 1024
NEG = -1e30
UNDERFLOW = -100.0

MESH = pl.DeviceIdType.MESH
ARB = pltpu.ARBITRARY
PAR = pltpu.PARALLEL
HIGHEST = lax.Precision.HIGHEST


def _pcall(body, **kw):
    return pl.pallas_call(body, **kw)


def _cp(*sem):
    return pltpu.CompilerParams(dimension_semantics=tuple(sem))


def _dot(a, b, ca, cb):
    return lax.dot_general(a, b, (((ca,), (cb,)), ((), ())), preferred_element_type=F32)


def _dot_f32(a, b, ca, cb):
    return lax.dot_general(a, b, (((ca,), (cb,)), ((), ())), precision=HIGHEST, preferred_element_type=F32)


def _split(v, parts):
    out = []
    r = v
    for _ in range(parts - 1):
        p = r.astype(_BF)
        out.append(p)
        r = r - p.astype(F32)
    out.append(r.astype(_BF))
    return out


def _dot_exact(v, m, ca, cb, parts, v_left=True):
    out = None
    for p in _split(v, parts):
        term = _dot(p, m, ca, cb) if v_left else _dot(m, p, ca, cb)
        out = term if out is None else out + term
    return out


def _iota2(shape, dim):
    return lax.broadcasted_iota(jnp.int32, shape, dim)


def _my_pos():
    return lax.axis_index("x"), lax.axis_index("y"), lax.axis_index("c")


def _flip(pos, k):
    x, y, c = pos
    px = 1 - x if (k >> 2) & 1 else x
    py = 1 - y if (k >> 1) & 1 else y
    pc = 1 - c if k & 1 else c
    return px, py, pc


def _lin(pos):
    return 4 * pos[0] + 2 * pos[1] + pos[2]


def _exchange(arrays, scatter, name):
    n = len(arrays)
    if scatter:
        out_shapes = [jax.ShapeDtypeStruct(a.shape, a.dtype) for a in arrays]
    else:
        out_shapes = [jax.ShapeDtypeStruct((N_DEV,) + a.shape, a.dtype) for a in arrays]

    def body(*refs):
        ins, outs = refs[:n], refs[n:2 * n]
        send_sems, recv_sems, local_sems = refs[2 * n:]
        me = _my_pos()
        me_i = _lin(me)
        local = []
        for a in range(n):
            src = ins[a].at[me_i] if scatter else ins[a]
            cp = pltpu.make_async_copy(src, outs[a].at[me_i], local_sems.at[a])
            cp.start()
            local.append(cp)
        remote = []
        for a in range(n):
            for k in range(1, N_DEV):
                peer = _flip(me, k)
                src = ins[a].at[_lin(peer)] if scatter else ins[a]
                cp = pltpu.make_async_remote_copy(
                    src_ref=src, dst_ref=outs[a].at[me_i],
                    send_sem=send_sems.at[a, k - 1], recv_sem=recv_sems.at[a, k - 1],
                    device_id=peer, device_id_type=MESH)
                cp.start()
                remote.append((a, k, cp))
        for a, k, cp in remote:
            cp.wait_send()
            peer = _flip(me, k)
            src = ins[a].at[me_i] if scatter else ins[a]
            pltpu.make_async_remote_copy(
                src_ref=src, dst_ref=outs[a].at[_lin(peer)],
                send_sem=send_sems.at[a, k - 1], recv_sem=recv_sems.at[a, k - 1],
                device_id=peer, device_id_type=MESH).wait_recv()
        for cp in local:
            cp.wait()

    any_spec = pl.BlockSpec(memory_space=pl.ANY)
    outs = _pcall(
        body, name=name, out_shape=out_shapes,
        in_specs=[any_spec] * n, out_specs=[any_spec] * n,
        scratch_shapes=[pltpu.SemaphoreType.DMA((n, N_DEV - 1)), pltpu.SemaphoreType.DMA((n, N_DEV - 1)),
                        pltpu.SemaphoreType.DMA((n,))],
    )(*arrays)
    return list(outs)


def _row_spec(t, w, col=0):
    return pl.BlockSpec((t, w), lambda i: (i, col))


def _vec_spec(w, col=0):
    return pl.BlockSpec((1, w), lambda i: (0, col))


def _norm_mod(x, g, sc, sh, name):
    s, d = x.shape
    t = min(ROW_T, s)

    def body(x_ref, g_ref, sc_ref, sh_ref, h_ref):
        xf = x_ref[...]
        rstd = lax.rsqrt(jnp.mean(xf * xf, axis=-1, keepdims=True) + EPS)
        y = xf * rstd * g_ref[...]
        h_ref[...] = (y * (1.0 + sc_ref[...]) + sh_ref[...]).astype(h_ref.dtype)

    return _pcall(
        body, name=name, grid=(s // t,), out_shape=jax.ShapeDtypeStruct((s, d), _BF),
        in_specs=[_row_spec(t, d), _vec_spec(d), _vec_spec(d), _vec_spec(d)],
        out_specs=_row_spec(t, d), compiler_params=_cp(PAR),
    )(x, g, sc, sh)


def _norm_mod_bwd(x, dh, dres, g, sc, name):
    s, d = x.shape
    t = min(ROW_T, s)

    def body(x_ref, dh_ref, dres_ref, g_ref, sc_ref, dx_ref, dsh_ref, dsc_ref, dg_ref):
        i = pl.program_id(0)
        xf = x_ref[...]
        dh = dh_ref[...]
        gv = g_ref[...]
        rstd = lax.rsqrt(jnp.mean(xf * xf, axis=-1, keepdims=True) + EPS)
        xhat = xf * rstd
        dn = dh * (1.0 + sc_ref[...])
        dxh = dn * gv
        dx_ref[...] = dres_ref[...] + rstd * (dxh - xhat * jnp.mean(dxh * xhat, axis=-1, keepdims=True))

        @pl.when(i == 0)
        def _():
            dsh_ref[...] = jnp.zeros_like(dsh_ref)
            dsc_ref[...] = jnp.zeros_like(dsc_ref)
            dg_ref[...] = jnp.zeros_like(dg_ref)

        dsh_ref[...] += jnp.sum(dh, axis=0, keepdims=True)
        dsc_ref[...] += jnp.sum(dh * (xhat * gv), axis=0, keepdims=True)
        dg_ref[...] += jnp.sum(dn * xhat, axis=0, keepdims=True)

    vec = jax.ShapeDtypeStruct((1, d), F32)
    return _pcall(
        body, name=name, grid=(s // t,),
        out_shape=[jax.ShapeDtypeStruct((s, d), F32), vec, vec, vec],
        in_specs=[_row_spec(t, d), _row_spec(t, d), _row_spec(t, d), _vec_spec(d), _vec_spec(d)],
        out_specs=[_row_spec(t, d), _vec_spec(d), _vec_spec(d), _vec_spec(d)],
        compiler_params=_cp(ARB),
    )(x, dh, dres, g, sc)


def _res_bwd(dx, y, gate, name):
    s, d = dx.shape
    t = min(ROW_T, s)

    def body(dx_ref, y_ref, g_ref, dy_ref, dg_ref):
        i = pl.program_id(0)
        dxv = dx_ref[...]
        dy_ref[...] = (dxv * g_ref[...]).astype(dy_ref.dtype)

        @pl.when(i == 0)
        def _():
            dg_ref[...] = jnp.zeros_like(dg_ref)

        dg_ref[...] += jnp.sum(dxv * y_ref[...], axis=0, keepdims=True)

    return _pcall(
        body, name=name, grid=(s // t,),
        out_shape=[jax.ShapeDtypeStruct((s, d), _BF), jax.ShapeDtypeStruct((1, d), F32)],
        in_specs=[_row_spec(t, d), _row_spec(t, d), _vec_spec(d)],
        out_specs=[_row_spec(t, d), _vec_spec(d)], compiler_params=_cp(ARB),
    )(dx, y, gate)


def _loss_head(y, target, name):
    s, d = y.shape
    t = min(ROW_T, s)

    def body(y_ref, t_ref, loss_ref, dy_ref):
        i = pl.program_id(0)
        diff = y_ref[...] - t_ref[...]
        dy_ref[...] = diff * (1.0 / d)

        @pl.when(i == 0)
        def _():
            loss_ref[...] = jnp.zeros_like(loss_ref)

        rows = jnp.sum(diff * diff, axis=-1, keepdims=True)
        loss_ref[...] += (0.5 / d) * jnp.sum(rows, axis=0, keepdims=True)

    return _pcall(
        body, name=name, grid=(s // t,),
        out_shape=[jax.ShapeDtypeStruct((1, 1), F32), jax.ShapeDtypeStruct((s, d), F32)],
        in_specs=[_row_spec(t, d), _row_spec(t, d)],
        out_specs=[pl.BlockSpec((1, 1), lambda i: (0, 0)), _row_spec(t, d)], compiler_params=_cp(ARB),
    )(y, target)


def _mm_nn(a, b, epi, extras, name):
    m, kdim = a.shape
    n = b.shape[1]
    tm, tn, tk = min(MM_T, m), min(MM_T, n), min(MM_T, kdim)
    nk = kdim // tk
    n_extra = len(extras)
    n_out = 1 if epi == "plain" else 2

    def finish(y, extra_refs, out_refs):
        if epi == "plain":
            out_refs[0][...] = y.astype(out_refs[0].dtype)
        elif epi == "resid":
            x_ref, g_ref = extra_refs
            out_refs[0][...] = x_ref[...] + g_ref[...] * y
            out_refs[1][...] = y
        else:
            r = jnp.maximum(y, 0.0)
            out_refs[0][...] = (r * r).astype(out_refs[0].dtype)
            out_refs[1][...] = r.astype(out_refs[1].dtype)

    def body(a_ref, b_ref, *rest):
        extra_refs = rest[:n_extra]
        out_refs = rest[n_extra:n_extra + n_out]
        part = _dot(a_ref[...].astype(_BF), b_ref[...].astype(_BF), 1, 0)
        if nk == 1:
            finish(part, extra_refs, out_refs)
        else:
            acc_ref = rest[-1]
            k = pl.program_id(2)

            @pl.when(k == 0)
            def _():
                acc_ref[...] = part

            @pl.when(k > 0)
            def _():
                acc_ref[...] += part

            @pl.when(k == nk - 1)
            def _():
                finish(acc_ref[...], extra_refs, out_refs)

    tile = pl.BlockSpec((tm, tn), lambda i, j, k: (i, j))
    in_specs = [pl.BlockSpec((tm, tk), lambda i, j, k: (i, k)), pl.BlockSpec((tk, tn), lambda i, j, k: (k, j))]
    if epi == "plain":
        out_shape = [jax.ShapeDtypeStruct((m, n), F32)]
    elif epi == "resid":
        in_specs += [tile, pl.BlockSpec((1, tn), lambda i, j, k: (0, j))]
        out_shape = [jax.ShapeDtypeStruct((m, n), F32)] * 2
    else:
        out_shape = [jax.ShapeDtypeStruct((m, n), _BF), jax.ShapeDtypeStruct((m, n), F32)]
    outs = _pcall(
        body, name=name, grid=(m // tm, n // tn, nk), out_shape=out_shape,
        in_specs=in_specs, out_specs=[tile] * n_out,
        scratch_shapes=[pltpu.VMEM((tm, tn), F32)] if nk > 1 else [],
        compiler_params=_cp(PAR, PAR, ARB),
    )(a, b, *extras)
    return outs[0] if n_out == 1 else outs


def _mm_nt(a, b, epi, extras, out_dtype, name):
    m, kdim = a.shape
    n = b.shape[0]
    tm, tn, tk = min(MM_T, m), min(MM_T, n), min(MM_T, kdim)
    nk = kdim // tk
    n_extra = len(extras)

    def finish(y, extra_refs, o_ref):
        if epi == "mul2":
            y = y * (2.0 * extra_refs[0][...].astype(F32))
        o_ref[...] = y.astype(o_ref.dtype)

    def body(a_ref, b_ref, *rest):
        extra_refs = rest[:n_extra]
        o_ref = rest[n_extra]
        part = _dot(a_ref[...].astype(_BF), b_ref[...].astype(_BF), 1, 1)
        if nk == 1:
            finish(part, extra_refs, o_ref)
        else:
            acc_ref = rest[-1]
            k = pl.program_id(2)

            @pl.when(k == 0)
            def _():
                acc_ref[...] = part

            @pl.when(k > 0)
            def _():
                acc_ref[...] += part

            @pl.when(k == nk - 1)
            def _():
                finish(acc_ref[...], extra_refs, o_ref)

    tile = pl.BlockSpec((tm, tn), lambda i, j, k: (i, j))
    in_specs = [pl.BlockSpec((tm, tk), lambda i, j, k: (i, k)), pl.BlockSpec((tn, tk), lambda i, j, k: (j, k))]
    in_specs += [tile] * n_extra
    return _pcall(
        body, name=name, grid=(m // tm, n // tn, nk), out_shape=jax.ShapeDtypeStruct((m, n), out_dtype),
        in_specs=in_specs, out_specs=tile,
        scratch_shapes=[pltpu.VMEM((tm, tn), F32)] if nk > 1 else [],
        compiler_params=_cp(PAR, PAR, ARB),
    )(a, b, *extras)


def _mm_tn(a, b, split, name):
    s, m = a.shape
    n = b.shape[1]
    ts = min(MM_T, s)
    ns = s // ts
    if split == "rows":
        tm, tn = min(MM_T, m // N_DEV), min(MM_T, n)
        per = (m // N_DEV) // tm
        out_shape = (N_DEV, m // N_DEV, n)
        out_spec = pl.BlockSpec((None, tm, tn), lambda i, j, k: (i // per, i % per, j))
    elif split == "cols":
        tm, tn = min(MM_T, m), min(MM_T, n // N_DEV)
        per = (n // N_DEV) // tn
        out_shape = (N_DEV, m, n // N_DEV)
        out_spec = pl.BlockSpec((None, tm, tn), lambda i, j, k: (j // per, i, j % per))
    else:
        tm, tn = min(MM_T, m), min(MM_T, n)
        out_shape = (m, n)
        out_spec = pl.BlockSpec((tm, tn), lambda i, j, k: (i, j))

    def body(a_ref, b_ref, o_ref, acc_ref):
        k = pl.program_id(2)
        part = _dot(a_ref[...].astype(_BF), b_ref[...].astype(_BF), 0, 0)

        @pl.when(k == 0)
        def _():
            acc_ref[...] = part

        @pl.when(k > 0)
        def _():
            acc_ref[...] += part

        @pl.when(k == ns - 1)
        def _():
            o_ref[...] = acc_ref[...].astype(o_ref.dtype)

    return _pcall(
        body, name=name, grid=(m // tm, n // tn, ns), out_shape=jax.ShapeDtypeStruct(out_shape, _XBF),
        in_specs=[pl.BlockSpec((ts, tm), lambda i, j, k: (k, i)), pl.BlockSpec((ts, tn), lambda i, j, k: (k, j))],
        out_specs=out_spec, scratch_shapes=[pltpu.VMEM((tm, tn), F32)],
        compiler_params=_cp(PAR, PAR, ARB),
    )(a, b)


def _head_masks():
    lane = _iota2((1, 128), 1)
    return [(lane // HEAD_DIM) == hh for hh in range(2)]


def _softplus_parts(z):
    sp = jnp.maximum(z, 0.0) + jnp.log(1.0 + jnp.exp(-jnp.abs(z)))
    return -sp, z - sp


def _sb_key_loop(i, carry, step):
    def cond(c):
        return jnp.logical_and(c[0] <= i, jnp.max(jnp.maximum(c[1][0][0], c[1][0][1])) > UNDERFLOW)

    def body(c):
        return c[0] + 1, step(i - c[0], c[1])

    return lax.while_loop(cond, body, (jnp.int32(1), carry))[1]


def _sb_fwd(proj, name):
    s = proj.shape[0]
    t = min(ATT_T, s)
    nblk = s // t
    scale = HEAD_DIM ** -0.5

    def body(q_ref, k_ref, v_ref, o_ref):
        i = pl.program_id(1)
        masks = _head_masks()
        row = _iota2((t, t), 0)
        col = _iota2((t, t), 1)
        u_strict = (row > col).astype(_BF)
        causal = col < row
        qs = q_ref[...] * scale
        qm = [jnp.where(masks[hh], qs, 0.0).astype(_BF) for hh in range(2)]

        def block(j, carry, diag):
            off = pl.multiple_of(j * t, t)
            kb = k_ref[pl.ds(off, t), :].astype(_BF)
            vb = v_ref[pl.ds(off, t), :].astype(_BF)
            runs, outs = [], []
            for hh in range(2):
                run, o = carry[0][hh], carry[1][hh]
                z = _dot(qm[hh], kb, 1, 1)
                l1, lb = _softplus_parts(z)
                if diag:
                    l1 = jnp.where(causal, l1, 0.0)
                between = run + _dot_exact(l1, u_strict, 1, 0, 2)
                a = jnp.exp(lb + between)
                if diag:
                    a = jnp.where(causal, a, 0.0)
                outs.append(o + _dot_exact(a, vb, 1, 0, 2))
                runs.append(run + jnp.sum(l1, axis=-1, keepdims=True))
            return tuple(runs), tuple(outs)

        zero, zero_o = jnp.zeros((t, 1), F32), jnp.zeros((t, 128), F32)
        carry = block(i, ((zero, zero), (zero_o, zero_o)), True)
        carry = _sb_key_loop(i, carry, lambda j, c: block(j, c, False))
        o_ref[...] = jnp.where(masks[0], carry[1][0], carry[1][1])

    kv_spec = lambda c0: pl.BlockSpec((s, 128), lambda p, i: (0, c0 // 128 + p))
    return _pcall(
        body, name=name, grid=(SB_HEADS // 2, nblk), out_shape=jax.ShapeDtypeStruct((s, 256), F32),
        in_specs=[pl.BlockSpec((t, 128), lambda p, i: (i, C_QA // 128 + p)), kv_spec(C_KA), kv_spec(C_VA)],
        out_specs=pl.BlockSpec((t, 128), lambda p, i: (i, p)), compiler_params=_cp(PAR, ARB),
    )(proj, proj, proj)


def _sb_bwd(proj, o, dmix, name):
    s = proj.shape[0]
    t = min(ATT_T, s)
    nblk = s // t
    scale = HEAD_DIM ** -0.5

    def body(q_ref, k_ref, v_ref, o_ref, do_ref, dq_ref, dk_ref, dv_ref):
        i = pl.program_id(1)
        masks = _head_masks()
        row = _iota2((t, t), 0)
        col = _iota2((t, t), 1)
        u_strict = (row > col).astype(_BF)
        u_incl = (row >= col).astype(_BF)
        causal = col < row

        @pl.when(i == 0)
        def _():
            dk_ref[...] = jnp.zeros_like(dk_ref)
            dv_ref[...] = jnp.zeros_like(dv_ref)

        qs = q_ref[...] * scale
        dov = do_ref[...]
        ov = o_ref[...]
        qm = [jnp.where(masks[hh], qs, 0.0).astype(_BF) for hh in range(2)]
        dom = [jnp.where(masks[hh], dov, 0.0).astype(_BF) for hh in range(2)]
        dsum = [jnp.sum(dom[hh].astype(F32) * ov, axis=-1, keepdims=True) for hh in range(2)]

        def block(j, carry, diag):
            off = pl.multiple_of(j * t, t)
            kb = k_ref[pl.ds(off, t), :].astype(_BF)
            vb = v_ref[pl.ds(off, t), :].astype(_BF)
            runs, eruns, dqs = [], [], []
            dk = dv = None
            for hh in range(2):
                run, erun, dq = carry[0][hh], carry[1][hh], carry[2][hh]
                z = _dot(qm[hh], kb, 1, 1)
                l1, lb = _softplus_parts(z)
                if diag:
                    l1 = jnp.where(causal, l1, 0.0)
                between = run + _dot_exact(l1, u_strict, 1, 0, 2)
                a = jnp.exp(lb + between)
                if diag:
                    a = jnp.where(causal, a, 0.0)
                g = a * _dot(dom[hh], vb, 1, 1)
                cum = dsum[hh] - (erun + _dot_exact(g, u_incl, 1, 0, 2))
                beta = jnp.exp(lb)
                dz = g * (1.0 - beta) - cum * beta
                if diag:
                    dz = jnp.where(causal, dz, 0.0)
                dzb = dz.astype(_BF)
                dqs.append(dq + _dot(dzb, kb, 1, 0))
                dk_h = _dot(dzb, qm[hh], 0, 0)
                dv_h = _dot(a.astype(_BF), dom[hh], 0, 0)
                dk = dk_h if dk is None else dk + dk_h
                dv = dv_h if dv is None else dv + dv_h
                runs.append(run + jnp.sum(l1, axis=-1, keepdims=True))
                eruns.append(erun + jnp.sum(g, axis=-1, keepdims=True))
            dk_ref[pl.ds(off, t), :] += dk
            dv_ref[pl.ds(off, t), :] += dv
            return tuple(runs), tuple(eruns), tuple(dqs)

        zero, zero_q = jnp.zeros((t, 1), F32), jnp.zeros((t, 128), F32)
        carry = block(i, ((zero, zero), (zero, zero), (zero_q, zero_q)), True)
        carry = _sb_key_loop(i, carry, lambda j, c: block(j, c, False))
        dq_ref[...] = jnp.where(masks[0], carry[2][0], carry[2][1]) * scale

    kv_spec = lambda c0: pl.BlockSpec((s, 128), lambda p, i: (0, c0 // 128 + p))
    blk = lambda c0: pl.BlockSpec((t, 128), lambda p, i: (i, c0 // 128 + p))
    acc = pl.BlockSpec((s, 128), lambda p, i: (0, p))
    shp = jax.ShapeDtypeStruct((s, 256), F32)
    return _pcall(
        body, name=name, grid=(SB_HEADS // 2, nblk), out_shape=[shp, shp, shp],
        in_specs=[blk(C_QA), kv_spec(C_KA), kv_spec(C_VA), blk(0), blk(M_SB)],
        out_specs=[blk(0), acc, acc], compiler_params=_cp(PAR, ARB),
    )(proj, proj, proj, o, dmix)


def _group_matrix(width):
    r = _iota2((width, width), 0) // HEAD_DIM
    c = _iota2((width, width), 1) // HEAD_DIM
    return (r == c).astype(_BF)


def _group_mean(v, gm):
    return _dot_exact(v, gm, 1, 0, 2) * (1.0 / HEAD_DIM)


def _fox_prep(proj, gq, gk, bf, name):
    s = proj.shape[0]
    t = min(ATT_T, s)
    nblk = s // t

    def body(q_ref, k_ref, fl_ref, gq_ref, gk_ref, bf_ref, qn_ref, kn_ref, cfc_ref, cfr_ref, carry_ref):
        i = pl.program_id(0)
        gm = _group_matrix(512)
        for src, gref, dst in ((q_ref, gq_ref, qn_ref), (k_ref, gk_ref, kn_ref)):
            v = src[...]
            rstd = lax.rsqrt(_group_mean(v * v, gm) + EPS)
            dst[...] = (v * rstd * gref[...]).astype(dst.dtype)

        @pl.when(i == 0)
        def _():
            carry_ref[...] = jnp.zeros_like(carry_ref)

        lane = _iota2((1, 128), 1)
        logf = jnp.where(lane < FOX_HEADS, _softplus_parts(-(fl_ref[...] + bf_ref[...]))[0], 0.0)
        lower = (_iota2((t, t), 0) >= _iota2((t, t), 1)).astype(_BF)
        cf = carry_ref[...] + _dot_exact(logf, lower, 1, 0, 3, v_left=False)
        cfc_ref[...] = cf
        cfr_ref[...] = cf.T[:FOX_HEADS, :]
        carry_ref[...] = cf[t - 1:t, :]

    col = lambda w, c0: pl.BlockSpec((t, w), lambda i: (i, c0 // w))
    v512 = pl.BlockSpec((1, 512), lambda i: (0, 0))
    return _pcall(
        body, name=name, grid=(nblk,),
        out_shape=[jax.ShapeDtypeStruct((s, 512), _BF), jax.ShapeDtypeStruct((s, 512), _BF),
                   jax.ShapeDtypeStruct((s, 128), F32), jax.ShapeDtypeStruct((FOX_HEADS, s), F32)],
        in_specs=[col(512, C_QB), col(512, C_KB), col(128, C_FL), v512, v512, pl.BlockSpec((1, 128), lambda i: (0, 0))],
        out_specs=[col(512, 0), col(512, 0), col(128, 0), pl.BlockSpec((FOX_HEADS, t), lambda i: (0, i))],
        scratch_shapes=[pltpu.VMEM((1, 128), F32)], compiler_params=_cp(ARB),
    )(proj, proj, proj, gq, gk, bf)


def _fox_cq(cfc_blk, head):
    lane = _iota2((1, 128), 1)
    return jnp.sum(jnp.where(lane == head, cfc_blk, 0.0), axis=-1, keepdims=True)


def _fox_ck(cfr_ref, off, t, head):
    sub = _iota2((FOX_HEADS, 1), 0)
    return jnp.sum(jnp.where(sub == head, cfr_ref[:, pl.ds(off, t)], 0.0), axis=0, keepdims=True)


def _fox_last_key(cfr_ref, j, t, head):
    return jnp.min(_fox_ck(cfr_ref, pl.multiple_of(j * t, t), t, head))


def _fox_fwd(proj, qn, kn, cfc, cfr, qk_bound, name):
    s = proj.shape[0]
    t = min(ATT_T, s)
    nblk = s // t
    scale = HEAD_DIM ** -0.5

    def body(q_ref, k_ref, v_ref, cfc_ref, cfr_ref, qkb_ref, o_ref, lse_ref):
        p = pl.program_id(0)
        i = pl.program_id(1)
        masks = _head_masks()
        lane = _iota2((1, 128), 1)
        valid = _iota2((t, t), 1) <= _iota2((t, t), 0)
        qv = q_ref[...]
        cfc_blk = cfc_ref[...]
        heads = [2 * p + hh for hh in range(2)]
        qm = [(jnp.where(masks[hh], qv, 0) * scale).astype(_BF) for hh in range(2)]
        cq = [_fox_cq(cfc_blk, heads[hh]) for hh in range(2)]

        def block(j, carry, diag):
            off = pl.multiple_of(j * t, t)
            kb = k_ref[pl.ds(off, t), :]
            vb = v_ref[pl.ds(off, t), :].astype(_BF)
            ms, ls, accs = [], [], []
            for hh in range(2):
                m, l, acc = carry[0][hh], carry[1][hh], carry[2][hh]
                z = _dot(qm[hh], kb, 1, 1) + (cq[hh] - _fox_ck(cfr_ref, off, t, heads[hh]))
                if diag:
                    z = jnp.where(valid, z, NEG)
                m_new = jnp.maximum(m, jnp.max(z, axis=-1, keepdims=True))
                pe = jnp.exp(z - m_new)
                alpha = jnp.exp(m - m_new)
                ms.append(m_new)
                ls.append(alpha * l + jnp.sum(pe, axis=-1, keepdims=True))
                accs.append(alpha * acc + _dot_exact(pe, vb, 1, 0, 2))
            return tuple(ms), tuple(ls), tuple(accs)

        def cond(c):
            j = jnp.maximum(i - c[0], 0)
            reach = [jnp.max(qkb_ref[...] + cq[hh] - c[1][0][hh]) - _fox_last_key(cfr_ref, j, t, heads[hh])
                     for hh in range(2)]
            return jnp.logical_and(c[0] <= i, jnp.maximum(reach[0], reach[1]) > UNDERFLOW)

        neg, zero, zero_o = jnp.full((t, 1), NEG, F32), jnp.zeros((t, 1), F32), jnp.zeros((t, 128), F32)
        carry = block(i, ((neg, neg), (zero, zero), (zero_o, zero_o)), True)
        m, l, acc = lax.while_loop(cond, lambda c: (c[0] + 1, block(i - c[0], c[1], False)), (jnp.int32(1), carry))[1]
        o_ref[...] = jnp.where(masks[0], acc[0] / l[0], acc[1] / l[1])
        lse_ref[...] = jnp.where(lane == 0, m[0] + jnp.log(l[0]), jnp.where(lane == 1, m[1] + jnp.log(l[1]), 0.0))

    return _pcall(
        body, name=name, grid=(FOX_HEADS // 2, nblk),
        out_shape=[jax.ShapeDtypeStruct((s, 512), F32), jax.ShapeDtypeStruct((FOX_HEADS // 2, s, 128), F32)],
        in_specs=[pl.BlockSpec((t, 128), lambda p, i: (i, p)), pl.BlockSpec((s, 128), lambda p, i: (0, p)),
                  pl.BlockSpec((s, 128), lambda p, i: (0, C_VB // 128 + p)),
                  pl.BlockSpec((t, 128), lambda p, i: (i, 0)), pl.BlockSpec((FOX_HEADS, s), lambda p, i: (0, 0)),
                  pl.BlockSpec((1, 1), lambda p, i: (0, 0))],
        out_specs=[pl.BlockSpec((t, 128), lambda p, i: (i, p)), pl.BlockSpec((None, t, 128), lambda p, i: (p, i, 0))],
        compiler_params=_cp(PAR, ARB),
    )(qn, kn, proj, cfc, cfr, qk_bound)


def _fox_bwd(proj, qn, kn, cfc, cfr, qk_bound, o, lse, dmix, name):
    s = proj.shape[0]
    t = min(ATT_T, s)
    nblk = s // t
    scale = HEAD_DIM ** -0.5

    def body(q_ref, k_ref, v_ref, cfc_ref, cfr_ref, qkb_ref, o_ref, lse_ref, do_ref, dq_ref, dk_ref, dv_ref, dcr_ref):
        p = pl.program_id(0)
        i = pl.program_id(1)
        masks = _head_masks()
        lane = _iota2((1, 128), 1)
        valid = _iota2((t, t), 1) <= _iota2((t, t), 0)
        before = (_iota2((t, t), 0) < _iota2((t, t), 1)).astype(_BF)

        @pl.when(i == 0)
        def _():
            dk_ref[...] = jnp.zeros_like(dk_ref)
            dv_ref[...] = jnp.zeros_like(dv_ref)
            dcr_ref[...] = jnp.zeros_like(dcr_ref)

        qv = q_ref[...]
        cfc_blk = cfc_ref[...]
        dov = do_ref[...]
        ov = o_ref[...]
        lse_blk = lse_ref[...]
        heads = [2 * p + hh for hh in range(2)]
        qm = [(jnp.where(masks[hh], qv, 0) * scale).astype(_BF) for hh in range(2)]
        dom = [jnp.where(masks[hh], dov, 0.0).astype(_BF) for hh in range(2)]
        dsum = [jnp.sum(dom[hh].astype(F32) * ov, axis=-1, keepdims=True) for hh in range(2)]
        lse_h = [jnp.sum(jnp.where(lane == hh, lse_blk, 0.0), axis=-1, keepdims=True) for hh in range(2)]
        cq = [_fox_cq(cfc_blk, heads[hh]) for hh in range(2)]
        reach = [jnp.max(qkb_ref[...] + cq[hh] - lse_h[hh]) for hh in range(2)]

        def matters(j):
            jm = jnp.maximum(j - 1, 0)
            far = [reach[hh] - _fox_last_key(cfr_ref, jm, t, heads[hh]) for hh in range(2)]
            return jnp.logical_and(j > 0, jnp.maximum(far[0], far[1]) > UNDERFLOW)

        first = lax.while_loop(matters, lambda j: j - 1, i)

        def block(j, carry, diag):
            off = pl.multiple_of(j * t, t)
            kb = k_ref[pl.ds(off, t), :]
            vb = v_ref[pl.ds(off, t), :].astype(_BF)
            dqs, pres = [], []
            dk = dv = None
            for hh in range(2):
                dq, pre = carry[0][hh], carry[1][hh]
                z = _dot(qm[hh], kb, 1, 1) + (cq[hh] - _fox_ck(cfr_ref, off, t, heads[hh]))
                pm = jnp.exp(z - lse_h[hh])
                if diag:
                    pm = jnp.where(valid, pm, 0.0)
                ds = pm * (_dot(dom[hh], vb, 1, 1) - dsum[hh])
                dsb = ds.astype(_BF)
                dqs.append(dq + _dot(dsb, kb, 1, 0))
                dk_h = _dot(dsb, qm[hh], 0, 0)
                dv_h = _dot(pm.astype(_BF), dom[hh], 0, 0)
                dk = dk_h if dk is None else dk + dk_h
                dv = dv_h if dv is None else dv + dv_h
                lower_keys = pre + _dot_exact(ds, before, 1, 0, 2)
                if diag:
                    lower_keys = jnp.where(valid, lower_keys, 0.0)
                dcr_ref[hh:hh + 1, pl.ds(off, t)] += jnp.sum(lower_keys, axis=0, keepdims=True)
                pres.append(pre + jnp.sum(ds, axis=-1, keepdims=True))
            dk_ref[pl.ds(off, t), :] += dk
            dv_ref[pl.ds(off, t), :] += dv
            return tuple(dqs), tuple(pres)

        zero, zero_q = jnp.zeros((t, 1), F32), jnp.zeros((t, 128), F32)
        carry = lax.fori_loop(first, i, lambda j, c: block(j, c, False), ((zero_q, zero_q), (zero, zero)))
        dqs, _ = block(i, carry, True)
        dq_ref[...] = jnp.where(masks[0], dqs[0], dqs[1]) * scale

    blk = lambda c0: pl.BlockSpec((t, 128), lambda p, i: (i, c0 // 128 + p))
    res = lambda c0: pl.BlockSpec((s, 128), lambda p, i: (0, c0 // 128 + p))
    shp = jax.ShapeDtypeStruct((s, 512), F32)
    return _pcall(
        body, name=name, grid=(FOX_HEADS // 2, nblk),
        out_shape=[shp, shp, shp, jax.ShapeDtypeStruct((FOX_HEADS // 2, 8, s), F32)],
        in_specs=[blk(0), res(0), res(C_VB), pl.BlockSpec((t, 128), lambda p, i: (i, 0)),
                  pl.BlockSpec((FOX_HEADS, s), lambda p, i: (0, 0)), pl.BlockSpec((1, 1), lambda p, i: (0, 0)), blk(0),
                  pl.BlockSpec((None, t, 128), lambda p, i: (p, i, 0)), blk(M_FOX)],
        out_specs=[blk(0), res(0), res(0), pl.BlockSpec((None, 8, s), lambda p, i: (p, 0, 0))],
        compiler_params=_cp(PAR, ARB),
    )(qn, kn, proj, cfc, cfr, qk_bound, o, lse, dmix)


def _fox_prep_bwd(proj, gq, gk, bf, dqn, dkn, dlogf, name):
    s = proj.shape[0]
    t = min(ATT_T, s)
    nblk = s // t

    def body(q_ref, k_ref, fl_ref, gq_ref, gk_ref, bf_ref, dqn_ref, dkn_ref, dlogf_ref,
             dq_ref, dk_ref, dfl_ref, dgq_ref, dgk_ref, dbf_ref):
        i = pl.program_id(0)
        gm = _group_matrix(512)

        @pl.when(i == 0)
        def _():
            dgq_ref[...] = jnp.zeros_like(dgq_ref)
            dgk_ref[...] = jnp.zeros_like(dgk_ref)
            dbf_ref[...] = jnp.zeros_like(dbf_ref)

        for src, gref, dyref, dst, dgref in ((q_ref, gq_ref, dqn_ref, dq_ref, dgq_ref),
                                              (k_ref, gk_ref, dkn_ref, dk_ref, dgk_ref)):
            v = src[...]
            dy = dyref[...]
            rstd = lax.rsqrt(_group_mean(v * v, gm) + EPS)
            xhat = v * rstd
            dgref[...] += jnp.sum(dy * xhat, axis=0, keepdims=True)
            dxh = dy * gref[...]
            dst[...] = (rstd * (dxh - xhat * _group_mean(dxh * xhat, gm))).astype(dst.dtype)

        xv = fl_ref[...] + bf_ref[...]
        e = jnp.exp(-jnp.abs(xv))
        dfl = dlogf_ref[...] * (jnp.where(xv >= 0.0, e, 1.0) / (1.0 + e))
        dfl_ref[...] = dfl.astype(dfl_ref.dtype)
        dbf_ref[...] += jnp.sum(dfl, axis=0, keepdims=True)

    col = lambda w, c0: pl.BlockSpec((t, w), lambda i: (i, c0 // w))
    v512 = pl.BlockSpec((1, 512), lambda i: (0, 0))
    v128 = pl.BlockSpec((1, 128), lambda i: (0, 0))
    return _pcall(
        body, name=name, grid=(nblk,),
        out_shape=[jax.ShapeDtypeStruct((s, 512), _BF), jax.ShapeDtypeStruct((s, 512), _BF),
                   jax.ShapeDtypeStruct((s, 128), _BF), jax.ShapeDtypeStruct((1, 512), F32),
                   jax.ShapeDtypeStruct((1, 512), F32), jax.ShapeDtypeStruct((1, 128), F32)],
        in_specs=[col(512, C_QB), col(512, C_KB), col(128, C_FL), v512, v512, v128, col(512, 0), col(512, 0), col(128, 0)],
        out_specs=[col(512, 0), col(512, 0), col(128, 0), v512, v512, v128], compiler_params=_cp(ARB),
    )(proj, proj, proj, gq, gk, bf, dqn, dkn, dlogf)


GELU_C = 0.7978845608028654
GELU_A = 0.044715


def _gelu(x):
    return 0.5 * x * (1.0 + jnp.tanh(GELU_C * (x + GELU_A * x * x * x)))


def _gelu_grad(x):
    th = jnp.tanh(GELU_C * (x + GELU_A * x * x * x))
    return 0.5 * (1.0 + th) + 0.5 * x * (1.0 - th * th) * (GELU_C * (1.0 + 3.0 * GELU_A * x * x))


def _sgu_tril():
    return _iota2((SGU_CHUNK, SGU_CHUNK), 0) >= _iota2((SGU_CHUNK, SGU_CHUNK), 1)


def _sgu_group_masks():
    lane = _iota2((1, 256), 1)
    return [(lane // HEAD_DIM) == g for g in range(SGU_GROUPS)]


def _sgu_fwd(proj, w, gs, bexp, name):
    s = proj.shape[0]
    t = min(2 * SGU_CHUNK, s)

    def body(u_ref, v_ref, w_ref, gs_ref, b_ref, o_ref):
        gm = _group_matrix(256)
        gmask = _sgu_group_masks()
        tril = _sgu_tril()
        u = _gelu(u_ref[...])
        vg = _gelu(v_ref[...])
        vhat = (vg * lax.rsqrt(_group_mean(vg * vg, gm) + EPS) * gs_ref[...]).astype(_BF)
        for ch in range(t // SGU_CHUNK):
            rows = slice(ch * SGU_CHUNK, (ch + 1) * SGU_CHUNK)
            mixed = b_ref[...]
            for g in range(SGU_GROUPS):
                wg = jnp.where(tril, w_ref[g], 0.0).astype(_BF)
                mixed = jnp.where(gmask[g], mixed + _dot(wg, vhat[rows], 1, 0), mixed)
            o_ref[rows, :] = u[rows] * mixed

    return _pcall(
        body, name=name, grid=(s // t,), out_shape=jax.ShapeDtypeStruct((s, 256), F32),
        in_specs=[_row_spec(t, 256, C_UC // 256), _row_spec(t, 256, C_VC // 256),
                  pl.BlockSpec((SGU_GROUPS, SGU_CHUNK, SGU_CHUNK), lambda i: (0, 0, 0)), _vec_spec(256),
                  pl.BlockSpec((SGU_CHUNK, 256), lambda i: (0, 0))],
        out_specs=_row_spec(t, 256), compiler_params=_cp(PAR),
    )(proj, proj, w, gs, bexp)


def _sgu_bwd(proj, w, gs, bexp, dmix, name):
    s = proj.shape[0]
    t = min(2 * SGU_CHUNK, s)
    nstep = s // t

    def body(u_ref, v_ref, w_ref, gs_ref, b_ref, do_ref, du_ref, dv_ref, dw_ref, db_ref, dgs_ref):
        i = pl.program_id(0)
        gm = _group_matrix(256)
        gmask = _sgu_group_masks()
        tril = _sgu_tril()

        @pl.when(i == 0)
        def _():
            dw_ref[...] = jnp.zeros_like(dw_ref)
            db_ref[...] = jnp.zeros_like(db_ref)
            dgs_ref[...] = jnp.zeros_like(dgs_ref)

        uc = u_ref[...]
        vc = v_ref[...]
        u = _gelu(uc)
        vg = _gelu(vc)
        rstd = lax.rsqrt(_group_mean(vg * vg, gm) + EPS)
        xh = vg * rstd
        gsv = gs_ref[...]
        vhat = (xh * gsv).astype(_BF)
        dov = do_ref[...]
        dm = dov * u
        for ch in range(t // SGU_CHUNK):
            rows = slice(ch * SGU_CHUNK, (ch + 1) * SGU_CHUNK)
            mixed = b_ref[...]
            dvh = jnp.zeros((SGU_CHUNK, 256), F32)
            dmc = dm[rows]
            for g in range(SGU_GROUPS):
                wg = jnp.where(tril, w_ref[g], 0.0).astype(_BF)
                mixed = jnp.where(gmask[g], mixed + _dot(wg, vhat[rows], 1, 0), mixed)
                dvh = jnp.where(gmask[g], _dot(wg, dmc.astype(_BF), 0, 0), dvh)
                dw_ref[g] += _dot(jnp.where(gmask[g], dmc, 0.0).astype(_BF), vhat[rows], 1, 1)
            db_ref[...] += dmc
            du_ref[rows, :] = (dov[rows] * mixed * _gelu_grad(uc[rows])).astype(du_ref.dtype)
            xhc = xh[rows]
            dgs_ref[...] += jnp.sum(dvh * xhc, axis=0, keepdims=True)
            dxh = dvh * gsv
            dvg = rstd[rows] * (dxh - xhc * _group_mean(dxh * xhc, gm))
            dv_ref[rows, :] = (dvg * _gelu_grad(vc[rows])).astype(dv_ref.dtype)

        @pl.when(i == nstep - 1)
        def _():
            for g in range(SGU_GROUPS):
                dw_ref[g] = jnp.where(tril, dw_ref[g], 0.0)

    wspec = pl.BlockSpec((SGU_GROUPS, SGU_CHUNK, SGU_CHUNK), lambda i: (0, 0, 0))
    bspec = pl.BlockSpec((SGU_CHUNK, 256), lambda i: (0, 0))
    return _pcall(
        body, name=name, grid=(nstep,),
        out_shape=[jax.ShapeDtypeStruct((s, 256), _BF), jax.ShapeDtypeStruct((s, 256), _BF),
                   jax.ShapeDtypeStruct((SGU_GROUPS, SGU_CHUNK, SGU_CHUNK), F32),
                   jax.ShapeDtypeStruct((SGU_CHUNK, 256), F32), jax.ShapeDtypeStruct((1, 256), F32)],
        in_specs=[_row_spec(t, 256, C_UC // 256), _row_spec(t, 256, C_VC // 256), wspec, _vec_spec(256), bspec,
                  _row_spec(t, 256, M_SGU // 256)],
        out_specs=[_row_spec(t, 256), _row_spec(t, 256), wspec, bspec, _vec_spec(256)],
        compiler_params=_cp(ARB),
    )(proj, proj, w, gs, bexp, dmix)


def _ada_fwd(c_all, ada_w, name):
    depth, d, n = ada_w.shape

    def body(c_ref, w_ref, o_ref):
        cv = c_ref[...]
        cond = cv / (1.0 + jnp.exp(-cv))
        o_ref[...] = _dot_f32(cond, w_ref[...], 1, 0)

    return _pcall(
        body, name=name, grid=(depth,), out_shape=jax.ShapeDtypeStruct((depth, N_DEV, n), F32),
        in_specs=[pl.BlockSpec((N_DEV, d), lambda l: (0, 0)), pl.BlockSpec((None, d, n), lambda l: (l, 0, 0))],
        out_specs=pl.BlockSpec((None, N_DEV, n), lambda l: (l, 0, 0)), compiler_params=_cp(PAR),
    )(c_all, ada_w)


def _ada_bwd(c_all, dmod, name):
    depth, _, n = dmod.shape
    d = c_all.shape[1]

    def body(c_ref, dm_ref, o_ref):
        cv = c_ref[...]
        cond = cv / (1.0 + jnp.exp(-cv))
        o_ref[...] = _dot_f32(cond, dm_ref[...], 0, 0)

    return _pcall(
        body, name=name, grid=(depth,), out_shape=jax.ShapeDtypeStruct((depth, d, n), F32),
        in_specs=[pl.BlockSpec((N_DEV, d), lambda l: (0, 0)), pl.BlockSpec((None, N_DEV, n), lambda l: (l, 0, 0))],
        out_specs=pl.BlockSpec((None, d, n), lambda l: (l, 0, 0)), compiler_params=_cp(PAR),
    )(c_all, dmod)


def _adamw(slots, w, m, v, name):
    n, r, c = slots.shape
    tr = 256 if r % 256 == 0 else r
    bc1 = 1.0 - ADAM_B1 ** ADAM_STEP
    bc2 = 1.0 - ADAM_B2 ** ADAM_STEP

    def body(s_ref, w_ref, m_ref, v_ref, g_ref, d_ref, nm_ref, nv_ref):
        g = s_ref[0].astype(F32)
        for j in range(1, n):
            g = g + s_ref[j].astype(F32)
        m_new = ADAM_B1 * m_ref[...] + (1.0 - ADAM_B1) * g
        v_new = ADAM_B2 * v_ref[...] + (1.0 - ADAM_B2) * (g * g)
        g_ref[...] = g
        nm_ref[...] = m_new
        nv_ref[...] = v_new
        d_ref[...] = -ADAM_LR * ((m_new / bc1) / (jnp.sqrt(v_new / bc2) + ADAM_EPS) + ADAM_WD * w_ref[...])

    tile = pl.BlockSpec((tr, c), lambda i: (i, 0))
    shp = jax.ShapeDtypeStruct((r, c), F32)
    return _pcall(
        body, name=name, grid=(r // tr,), out_shape=[shp] * 4,
        in_specs=[pl.BlockSpec((n, tr, c), lambda i: (0, i, 0)), tile, tile, tile],
        out_specs=[tile] * 4, compiler_params=_cp(PAR),
    )(slots, w, m, v)


def _pad_cols(a, width):
    return jnp.pad(a, ((0, 0), (0, width - a.shape[1])))


def _w_in_layout(w):
    pad = jnp.zeros(w.shape[:-1] + (IN_P - IN_W,), w.dtype)
    return jnp.concatenate([w[..., 768:FL_SRC], w[..., :768], w[..., UC_SRC:], w[..., FL_SRC:UC_SRC], pad], axis=-1)


def _w_in_unlayout(g):
    return jnp.concatenate([g[..., C_QA:C_UC], g[..., :C_QA], g[..., C_FL:C_FL + FOX_HEADS], g[..., C_UC:C_FL]], axis=-1)


SMALL = [("ada_b", DEPTH * 6 * D_MODEL), ("norm1_g", DEPTH * D_MODEL), ("norm2_g", DEPTH * D_MODEL),
         ("sgu_w", DEPTH * SGU_GROUPS * SGU_CHUNK * SGU_CHUNK), ("sgu_b", DEPTH * SGU_GROUPS * SGU_CHUNK),
         ("sgu_norm_g", DEPTH * SGU_GROUPS * HEAD_DIM), ("q_norm_g", DEPTH * HEAD_DIM), ("k_norm_g", DEPTH * HEAD_DIM),
         ("b_forget", DEPTH * FOX_HEADS), ("loss", 1)]
SMALL_ROWS = 2560


def _pack_small(parts):
    flat = jnp.concatenate([parts[name].reshape(-1).astype(F32) for name, _ in SMALL])
    return jnp.pad(flat, (0, SMALL_ROWS * 128 - flat.shape[0])).reshape(SMALL_ROWS, 128)


def _unpack_small(packed, shapes):
    flat = packed.reshape(-1)
    out, off = {}, 0
    for name, size in SMALL:
        out[name] = flat[off:off + size].reshape(shapes[name])
        off += size
    return out


def kernel(x, c, ada_w, ada_b, norm1_g, norm2_g, w_in, b_forget, q_norm_g, k_norm_g, sgu_norm_g, sgu_w, sgu_b, w_out, mlp_w1, mlp_w2, loss_target, m_ada_w, m_ada_b, m_norm1_g, m_norm2_g, m_w_in, m_b_forget, m_q_norm_g, m_k_norm_g, m_sgu_norm_g, m_sgu_w, m_sgu_b, m_w_out, m_mlp_w1, m_mlp_w2, v_ada_w, v_ada_b, v_norm1_g, v_norm2_g, v_w_in, v_b_forget, v_q_norm_g, v_k_norm_g, v_sgu_norm_g, v_sgu_w, v_sgu_b, v_w_out, v_mlp_w1, v_mlp_w2):
    me = _lin(_my_pos())
    x0 = x[0]
    target = loss_target[0]
    n_ada = ada_w.shape[2]

    w_in_g, w_out_g, w1_g, w2_g, c_all = _exchange(
        [w_in.astype(_XBF), w_out.astype(_XBF), mlp_w1.astype(_XBF), mlp_w2.astype(_XBF), c], False, "gather_weights")
    c_all = c_all.reshape(N_DEV, D_MODEL)
    w_in_full = _w_in_layout(jnp.transpose(w_in_g, (1, 2, 0, 3)).reshape(DEPTH, D_MODEL, IN_W))
    w_out_full = jnp.transpose(w_out_g, (1, 0, 2, 3)).reshape(DEPTH, D_MODEL, D_MODEL)
    w1_full = jnp.transpose(w1_g, (1, 2, 0, 3)).reshape(DEPTH, D_MODEL, D_FF)
    w2_full = jnp.transpose(w2_g, (1, 0, 2, 3)).reshape(DEPTH, D_FF, D_MODEL)

    mod_part = _ada_fwd(c_all, ada_w, "ada_fwd")
    (mod_rows,) = _exchange([jnp.transpose(mod_part, (1, 0, 2))], True, "scatter_mod")
    mod = jnp.transpose(mod_rows, (1, 0, 2)).reshape(DEPTH, 6 * D_MODEL) + ada_b
    mods = mod.reshape(DEPTH, 6, 1, D_MODEL)

    saved = []
    xl = x0
    for l in range(DEPTH):
        sh1, sc1, g1, sh2, sc2, g2 = (mods[l, r] for r in range(6))
        n1 = norm1_g[l].reshape(1, D_MODEL)
        n2 = norm2_g[l].reshape(1, D_MODEL)
        gq = jnp.tile(q_norm_g[l], FOX_HEADS).reshape(1, 512)
        gk = jnp.tile(k_norm_g[l], FOX_HEADS).reshape(1, 512)
        bf = jnp.pad(b_forget[l], (0, 128 - FOX_HEADS)).reshape(1, 128)
        gs = sgu_norm_g[l].reshape(1, 256)
        bexp = jnp.repeat(sgu_b[l].T, HEAD_DIM, axis=1)

        h1 = _norm_mod(xl, n1, sc1, sh1, "norm_mod")
        proj = _mm_nn(h1, w_in_full[l], "plain", (), "mm_in")
        o_sb = _sb_fwd(proj, "sb_fwd")
        qn, kn, cfc, cfr = _fox_prep(proj, gq, gk, bf, "fox_prep")
        qkb = (1.02 * HEAD_DIM ** 0.5 * jnp.max(jnp.abs(q_norm_g[l])) * jnp.max(jnp.abs(k_norm_g[l]))).reshape(1, 1)
        o_fox, lse = _fox_fwd(proj, qn, kn, cfc, cfr, qkb, "fox_fwd")
        o_sgu = _sgu_fwd(proj, sgu_w[l], gs, bexp, "sgu_fwd")
        mixed = jnp.concatenate([o_sb, o_fox, o_sgu], axis=-1).astype(_BF)
        x_mid, y1 = _mm_nn(mixed, w_out_full[l], "resid", (xl, g1), "mm_out")
        h2 = _norm_mod(x_mid, n2, sc2, sh2, "norm_mod")
        r2, r1 = _mm_nn(h2, w1_full[l], "relu2", (), "mm_w1")
        x_out, y2 = _mm_nn(r2, w2_full[l], "resid", (x_mid, g2), "mm_w2")
        saved.append(dict(x_in=xl, h1=h1, proj=proj, o_sb=o_sb, qn=qn, kn=kn, cfc=cfc, cfr=cfr, o_fox=o_fox, lse=lse,
                          mixed=mixed, x_mid=x_mid, y1=y1, h2=h2, r2=r2, r1=r1, y2=y2,
                          n1=n1, n2=n2, gq=gq, gk=gk, bf=bf, gs=gs, bexp=bexp, qkb=qkb))
        xl = x_out

    loss_part, dx = _loss_head(xl, target, "loss_head")

    g_in, g_out, g_w1, g_w2 = [None] * DEPTH, [None] * DEPTH, [None] * DEPTH, [None] * DEPTH
    small = {k: [None] * DEPTH for k in ("mod", "norm1_g", "norm2_g", "sgu_w", "sgu_b", "sgu_norm_g",
                                         "q_norm_g", "k_norm_g", "b_forget")}
    for l in reversed(range(DEPTH)):
        sv = saved[l]
        sh1, sc1, g1, sh2, sc2, g2 = (mods[l, r] for r in range(6))
        dy2, dg2 = _res_bwd(dx, sv["y2"], g2, "res_bwd")
        da = _mm_nt(dy2, w2_full[l], "mul2", (sv["r1"],), _BF, "mm_w2_bwd")
        g_w2[l] = _mm_tn(sv["r2"], dy2, "rows", "mm_w2_grad")
        dh2 = _mm_nt(da, w1_full[l], "plain", (), F32, "mm_w1_bwd")
        g_w1[l] = _mm_tn(sv["h2"], da, "cols", "mm_w1_grad")
        dx_mid, dsh2, dsc2, dn2 = _norm_mod_bwd(sv["x_mid"], dh2, dx, sv["n2"], sc2, "norm_mod_bwd")
        dy1, dg1 = _res_bwd(dx_mid, sv["y1"], g1, "res_bwd")
        dmix = _mm_nt(dy1, w_out_full[l], "plain", (), F32, "mm_out_bwd")
        g_out[l] = _mm_tn(sv["mixed"], dy1, "rows", "mm_out_grad")
        dq_sb, dk_sb, dv_sb = _sb_bwd(sv["proj"], sv["o_sb"], dmix, "sb_bwd")
        dqn, dkn, dv_fox, dcr = _fox_bwd(sv["proj"], sv["qn"], sv["kn"], sv["cfc"], sv["cfr"], sv["qkb"], sv["o_fox"],
                                         sv["lse"], dmix, "fox_bwd")
        dlogf = _pad_cols(dcr[:, :2, :].reshape(FOX_HEADS, -1).T, 128)
        dq_fox, dk_fox, dfl, dgq, dgk, dbf = _fox_prep_bwd(sv["proj"], sv["gq"], sv["gk"], sv["bf"], dqn, dkn, dlogf,
                                                           "fox_prep_bwd")
        duc, dvc, dsw, dsb, dsg = _sgu_bwd(sv["proj"], sgu_w[l], sv["gs"], sv["bexp"], dmix, "sgu_bwd")
        dproj = jnp.concatenate(
            [dq_fox, dk_fox, dv_fox.astype(_BF), dq_sb.astype(_BF), dk_sb.astype(_BF), dv_sb.astype(_BF), duc, dvc, dfl,
             jnp.zeros((dfl.shape[0], IN_P - C_FL - 128), _BF)], axis=-1)
        dh1 = _mm_nt(dproj, w_in_full[l], "plain", (), F32, "mm_in_bwd")
        g_in[l] = _mm_tn(sv["h1"], dproj, None, "mm_in_grad")
        dx, dsh1, dsc1, dn1 = _norm_mod_bwd(sv["x_in"], dh1, dx_mid, sv["n1"], sc1, "norm_mod_bwd")

        small["mod"][l] = jnp.concatenate([dsh1, dsc1, dg1, dsh2, dsc2, dg2], axis=-1).reshape(-1)
        small["norm1_g"][l] = dn1.reshape(-1)
        small["norm2_g"][l] = dn2.reshape(-1)
        small["sgu_w"][l] = dsw
        small["sgu_b"][l] = dsb.reshape(SGU_CHUNK, SGU_GROUPS, HEAD_DIM).sum(-1).T
        small["sgu_norm_g"][l] = dsg.reshape(SGU_GROUPS, HEAD_DIM)
        small["q_norm_g"][l] = dgq.reshape(FOX_HEADS, HEAD_DIM).sum(0)
        small["k_norm_g"][l] = dgk.reshape(FOX_HEADS, HEAD_DIM).sum(0)
        small["b_forget"][l] = dbf[0, :FOX_HEADS]

    parts = {k: jnp.stack(v) for k, v in small.items()}
    parts["ada_b"] = parts.pop("mod")
    parts["loss"] = loss_part
    (small_all,) = _exchange([_pack_small(parts)], False, "gather_small")
    zero1 = jnp.zeros((1,), F32)
    packs = [_pack_small(dict(ada_b=a, norm1_g=b, norm2_g=cc, sgu_w=d, sgu_b=e, sgu_norm_g=f, q_norm_g=g, k_norm_g=h,
                              b_forget=i, loss=zero1))
             for a, b, cc, d, e, f, g, h, i in (
                 (ada_b, norm1_g, norm2_g, sgu_w, sgu_b, sgu_norm_g, q_norm_g, k_norm_g, b_forget),
                 (m_ada_b, m_norm1_g, m_norm2_g, m_sgu_w, m_sgu_b, m_sgu_norm_g, m_q_norm_g, m_k_norm_g, m_b_forget),
                 (v_ada_b, v_norm1_g, v_norm2_g, v_sgu_w, v_sgu_b, v_sgu_norm_g, v_q_norm_g, v_k_norm_g, v_b_forget))]
    small_shapes = dict(ada_b=ada_b.shape, norm1_g=norm1_g.shape, norm2_g=norm2_g.shape, sgu_w=sgu_w.shape,
                        sgu_b=sgu_b.shape, sgu_norm_g=sgu_norm_g.shape, q_norm_g=q_norm_g.shape,
                        k_norm_g=k_norm_g.shape, b_forget=b_forget.shape, loss=())
    s_out = [_unpack_small(a, small_shapes) for a in _adamw(small_all, *packs, "adamw_small")]

    dmod_all = small_all[:, :DEPTH * 6 * D_MODEL // 128, :].reshape(N_DEV, DEPTH, 6 * D_MODEL)
    dmod_mine = lax.dynamic_slice_in_dim(dmod_all, me * n_ada, n_ada, axis=2)
    g_ada = _ada_bwd(c_all, jnp.transpose(dmod_mine, (1, 0, 2)), "ada_bwd")
    ada_out = _adamw(g_ada.reshape(1, DEPTH * D_MODEL, n_ada), ada_w.reshape(-1, n_ada), m_ada_w.reshape(-1, n_ada),
                     v_ada_w.reshape(-1, n_ada), "adamw_ada")
    ada_out = [a.reshape(ada_w.shape) for a in ada_out]

    gi = jnp.stack([_w_in_unlayout(g) for g in g_in])
    gi = jnp.transpose(gi.reshape(DEPTH, D_MODEL, N_DEV, IN_W // N_DEV), (2, 0, 1, 3))
    go = jnp.stack(g_out, axis=1)
    gw1 = jnp.stack(g_w1, axis=1)
    gw2 = jnp.stack(g_w2, axis=1)
    ri, ro, r1, r2 = _exchange([gi, go, gw1, gw2], True, "scatter_grads")

    def big(slots, w, m, v, name):
        cdim = w.shape[-1]
        outs = _adamw(slots.reshape(N_DEV, -1, cdim), w.reshape(-1, cdim), m.reshape(-1, cdim), v.reshape(-1, cdim), name)
        return [a.reshape(w.shape) for a in outs]

    in_out = big(ri, w_in, m_w_in, v_w_in, "adamw_in")
    out_out = big(ro, w_out, m_w_out, v_w_out, "adamw_out")
    w1_out = big(r1, mlp_w1, m_mlp_w1, v_mlp_w1, "adamw_w1")
    w2_out = big(r2, mlp_w2, m_mlp_w2, v_mlp_w2, "adamw_w2")

    def leaves(idx):
        sm = s_out[idx]
        return [ada_out[idx], sm["ada_b"], sm["norm1_g"], sm["norm2_g"], in_out[idx], sm["b_forget"], sm["q_norm_g"],
                sm["k_norm_g"], sm["sgu_norm_g"], sm["sgu_w"], sm["sgu_b"], out_out[idx], w1_out[idx], w2_out[idx]]

    loss = s_out[0]["loss"]
    grad_x = dx.reshape(x.shape)
    return (loss, grad_x, *leaves(0), *leaves(1), *leaves(2), *leaves(3))
```

```python
import jax
import jax.numpy as jnp
from jax import lax
from jax.experimental import pallas as pl
from jax.experimental.pallas import tpu as pltpu

F32 = jnp.float32
_BF = jnp.bfloat16
_XBF = jnp.bfloat16

N_DEV = 8
D_MODEL = 1024
DEPTH = 4
HEAD_DIM = 64
SB_HEADS = 4
FOX_HEADS = 8
SGU_GROUPS = 4
SGU_CHUNK = 128
D_FF = 4096
EPS = 1e-6
IN_W = 2824
FL_SRC = 2304
UC_SRC = 2312

C_QB, C_KB, C_VB = 0, 512, 1024
C_QA, C_KA, C_VA = 1536, 1792, 2048
C_UC, C_VC = 2304, 2560
C_FL = 2816
IN_P = 3072
M_SB, M_FOX, M_SGU = 0, 256, 768

ADAM_LR = 0.001
ADAM_B1 = 0.9
ADAM_B2 = 0.999
ADAM_EPS = 1e-08
ADAM_WD = 0.01
ADAM_STEP = 10

ATT_T = 256
ROW_T = 512
MM_T = 1024
NEG = -1e30
UNDERFLOW = -100.0

MESH = pl.DeviceIdType.MESH
ARB = pltpu.ARBITRARY
PAR = pltpu.PARALLEL
HIGHEST = lax.Precision.HIGHEST


def _pcall(body, **kw):
    return pl.pallas_call(body, **kw)


def _cp(*sem):
    return pltpu.CompilerParams(dimension_semantics=tuple(sem))


def _dot(a, b, ca, cb):
    return lax.dot_general(a, b, (((ca,), (cb,)), ((), ())), preferred_element_type=F32)


def _dot_f32(a, b, ca, cb):
    return lax.dot_general(a, b, (((ca,), (cb,)), ((), ())), precision=HIGHEST, preferred_element_type=F32)


def _split(v, parts):
    out = []
    r = v
    for _ in range(parts - 1):
        p = r.astype(_BF)
        out.append(p)
        r = r - p.astype(F32)
    out.append(r.astype(_BF))
    return out


def _dot_exact(v, m, ca, cb, parts, v_left=True):
    out = None
    for p in _split(v, parts):
        term = _dot(p, m, ca, cb) if v_left else _dot(m, p, ca, cb)
        out = term if out is None else out + term
    return out


def _iota2(shape, dim):
    return lax.broadcasted_iota(jnp.int32, shape, dim)


def _my_pos():
    return lax.axis_index("x"), lax.axis_index("y"), lax.axis_index("c")


def _flip(pos, k):
    x, y, c = pos
    px = 1 - x if (k >> 2) & 1 else x
    py = 1 - y if (k >> 1) & 1 else y
    pc = 1 - c if k & 1 else c
    return px, py, pc


def _lin(pos):
    return 4 * pos[0] + 2 * pos[1] + pos[2]


def _exchange_ops(ins, outs, sems, scatter, slot):
    send_sems, recv_sems, local_sems = sems
    n = len(ins)

    def copies(with_recvs):
        me = _my_pos()
        me_i = _lin(me)
        dst = lambda a, j: outs[a].at[j] if slot is None else outs[a].at[j, slot]
        mine = lambda a: ins[a].at[me_i] if scatter else ins[a]
        local = [pltpu.make_async_copy(mine(a), dst(a, me_i), local_sems.at[a]) for a in range(n)]
        sends, recvs = [], []
        for a in range(n):
            for k in range(1, N_DEV):
                peer = _flip(me, k)
                pair = dict(send_sem=send_sems.at[a, k - 1], recv_sem=recv_sems.at[a, k - 1],
                            device_id=peer, device_id_type=MESH)
                src = ins[a].at[_lin(peer)] if scatter else ins[a]
                sends.append(pltpu.make_async_remote_copy(src_ref=src, dst_ref=dst(a, me_i), **pair))
                if with_recvs:
                    recvs.append(pltpu.make_async_remote_copy(src_ref=mine(a), dst_ref=dst(a, _lin(peer)), **pair))
        return local, sends, recvs

    def start():
        local, sends, _ = copies(False)
        for cp in local + sends:
            cp.start()

    def wait():
        local, sends, recvs = copies(True)
        for snd, rcv in zip(sends, recvs):
            snd.wait_send()
            rcv.wait_recv()
        for cp in local:
            cp.wait()

    return start, wait


def _comm_operands(comm):
    arrays, recv = comm["arrays"], comm["recv"]
    n = len(arrays)
    any_spec = pl.BlockSpec(memory_space=pl.ANY)
    if recv is not None:
        out_shapes = [jax.ShapeDtypeStruct(r.shape, r.dtype) for r in recv]
    elif comm["scatter"]:
        out_shapes = [jax.ShapeDtypeStruct(a.shape, a.dtype) for a in arrays]
    else:
        out_shapes = [jax.ShapeDtypeStruct((N_DEV,) + a.shape, a.dtype) for a in arrays]
    operands = list(arrays) + (list(recv) if recv is not None else [])
    sems = [pltpu.SemaphoreType.DMA((n, N_DEV - 1)), pltpu.SemaphoreType.DMA((n, N_DEV - 1)),
            pltpu.SemaphoreType.DMA((n,))]
    return operands, [any_spec] * len(operands), out_shapes, [any_spec] * n, sems


def _with_comm(body, comm, n_in, n_out, first_step, last_step):
    if comm is None:
        return body
    n = len(comm["arrays"])
    n_cin = n if comm["recv"] is None else 2 * n

    def wrapped(*refs):
        ins, cins = refs[:n_in], refs[n_in:n_in + n]
        o0 = n_in + n_cin
        outs, couts, sems = refs[o0:o0 + n_out], refs[o0 + n_out:o0 + n_out + n], refs[o0 + n_out + n:]
        start, wait = _exchange_ops(cins, couts, sems, comm["scatter"], comm["slot"])
        pl.when(first_step())(start)
        body(*ins, *outs)
        pl.when(last_step())(wait)

    return wrapped


def _comm_aliases(comm, n_in, n_out):
    if comm is None or comm["recv"] is None:
        return {}
    n = len(comm["arrays"])
    return {n_in + n + a: n_out + a for a in range(n)}


def _exchange(arrays, scatter, name, recv=None, slot=None):
    comm = dict(arrays=arrays, scatter=scatter, recv=recv, slot=slot)
    operands, in_specs, out_shapes, out_specs, sems = _comm_operands(comm)
    n = len(arrays)

    def body(*refs):
        n_cin = len(operands)
        start, wait = _exchange_ops(refs[:n], refs[n_cin:n_cin + n], refs[n_cin + n:], scatter, slot)
        start()
        wait()

    outs = _pcall(body, name=name, out_shape=out_shapes, in_specs=in_specs, out_specs=out_specs, scratch_shapes=sems,
                  input_output_aliases=_comm_aliases(comm, 0, 0))(*operands)
    return list(outs)


def _row_spec(t, w, col=0):
    return pl.BlockSpec((t, w), lambda i: (i, col))


def _vec_spec(w, col=0):
    return pl.BlockSpec((1, w), lambda i: (0, col))


def _norm_mod(x, g, sc, sh, name):
    s, d = x.shape
    t = min(ROW_T, s)

    def body(x_ref, g_ref, sc_ref, sh_ref, h_ref):
        xf = x_ref[...]
        rstd = lax.rsqrt(jnp.mean(xf * xf, axis=-1, keepdims=True) + EPS)
        y = xf * rstd * g_ref[...]
        h_ref[...] = (y * (1.0 + sc_ref[...]) + sh_ref[...]).astype(h_ref.dtype)

    return _pcall(
        body, name=name, grid=(s // t,), out_shape=jax.ShapeDtypeStruct((s, d), _BF),
        in_specs=[_row_spec(t, d), _vec_spec(d), _vec_spec(d), _vec_spec(d)],
        out_specs=_row_spec(t, d), compiler_params=_cp(PAR),
    )(x, g, sc, sh)


def _norm_mod_bwd(x, dh, dres, g, sc, name):
    s, d = x.shape
    t = min(ROW_T, s)

    def body(x_ref, dh_ref, dres_ref, g_ref, sc_ref, dx_ref, dsh_ref, dsc_ref, dg_ref):
        i = pl.program_id(0)
        xf = x_ref[...]
        dh = dh_ref[...]
        gv = g_ref[...]
        rstd = lax.rsqrt(jnp.mean(xf * xf, axis=-1, keepdims=True) + EPS)
        xhat = xf * rstd
        dn = dh * (1.0 + sc_ref[...])
        dxh = dn * gv
        dx_ref[...] = dres_ref[...] + rstd * (dxh - xhat * jnp.mean(dxh * xhat, axis=-1, keepdims=True))

        @pl.when(i == 0)
        def _():
            dsh_ref[...] = jnp.zeros_like(dsh_ref)
            dsc_ref[...] = jnp.zeros_like(dsc_ref)
            dg_ref[...] = jnp.zeros_like(dg_ref)

        dsh_ref[...] += jnp.sum(dh, axis=0, keepdims=True)
        dsc_ref[...] += jnp.sum(dh * (xhat * gv), axis=0, keepdims=True)
        dg_ref[...] += jnp.sum(dn * xhat, axis=0, keepdims=True)

    vec = jax.ShapeDtypeStruct((1, d), F32)
    return _pcall(
        body, name=name, grid=(s // t,),
        out_shape=[jax.ShapeDtypeStruct((s, d), F32), vec, vec, vec],
        in_specs=[_row_spec(t, d), _row_spec(t, d), _row_spec(t, d), _vec_spec(d), _vec_spec(d)],
        out_specs=[_row_spec(t, d), _vec_spec(d), _vec_spec(d), _vec_spec(d)],
        compiler_params=_cp(ARB),
    )(x, dh, dres, g, sc)


def _res_bwd(dx, y, gate, name):
    s, d = dx.shape
    t = min(ROW_T, s)

    def body(dx_ref, y_ref, g_ref, dy_ref, dg_ref):
        i = pl.program_id(0)
        dxv = dx_ref[...]
        dy_ref[...] = (dxv * g_ref[...]).astype(dy_ref.dtype)

        @pl.when(i == 0)
        def _():
            dg_ref[...] = jnp.zeros_like(dg_ref)

        dg_ref[...] += jnp.sum(dxv * y_ref[...], axis=0, keepdims=True)

    return _pcall(
        body, name=name, grid=(s // t,),
        out_shape=[jax.ShapeDtypeStruct((s, d), _BF), jax.ShapeDtypeStruct((1, d), F32)],
        in_specs=[_row_spec(t, d), _row_spec(t, d), _vec_spec(d)],
        out_specs=[_row_spec(t, d), _vec_spec(d)], compiler_params=_cp(ARB),
    )(dx, y, gate)


def _loss_head(y, target, name):
    s, d = y.shape
    t = min(ROW_T, s)

    def body(y_ref, t_ref, loss_ref, dy_ref):
        i = pl.program_id(0)
        diff = y_ref[...] - t_ref[...]
        dy_ref[...] = diff * (1.0 / d)

        @pl.when(i == 0)
        def _():
            loss_ref[...] = jnp.zeros_like(loss_ref)

        rows = jnp.sum(diff * diff, axis=-1, keepdims=True)
        loss_ref[...] += (0.5 / d) * jnp.sum(rows, axis=0, keepdims=True)

    return _pcall(
        body, name=name, grid=(s // t,),
        out_shape=[jax.ShapeDtypeStruct((1, 1), F32), jax.ShapeDtypeStruct((s, d), F32)],
        in_specs=[_row_spec(t, d), _row_spec(t, d)],
        out_specs=[pl.BlockSpec((1, 1), lambda i: (0, 0)), _row_spec(t, d)], compiler_params=_cp(ARB),
    )(y, target)


def _mm_nn(a, b, epi, extras, name):
    m, kdim = a.shape
    n = b.shape[1]
    tm, tn, tk = min(MM_T, m), min(MM_T, n), min(MM_T, kdim)
    nk = kdim // tk
    n_extra = len(extras)
    n_out = 1 if epi == "plain" else 2

    def finish(y, extra_refs, out_refs):
        if epi == "plain":
            out_refs[0][...] = y.astype(out_refs[0].dtype)
        elif epi == "resid":
            x_ref, g_ref = extra_refs
            out_refs[0][...] = x_ref[...] + g_ref[...] * y
            out_refs[1][...] = y
        else:
            r = jnp.maximum(y, 0.0)
            out_refs[0][...] = (r * r).astype(out_refs[0].dtype)
            out_refs[1][...] = r.astype(out_refs[1].dtype)

    def body(a_ref, b_ref, *rest):
        extra_refs = rest[:n_extra]
        out_refs = rest[n_extra:n_extra + n_out]
        part = _dot(a_ref[...].astype(_BF), b_ref[...].astype(_BF), 1, 0)
        if nk == 1:
            finish(part, extra_refs, out_refs)
        else:
            acc_ref = rest[-1]
            k = pl.program_id(2)

            @pl.when(k == 0)
            def _():
                acc_ref[...] = part

            @pl.when(k > 0)
            def _():
                acc_ref[...] += part

            @pl.when(k == nk - 1)
            def _():
                finish(acc_ref[...], extra_refs, out_refs)

    tile = pl.BlockSpec((tm, tn), lambda i, j, k: (i, j))
    in_specs = [pl.BlockSpec((tm, tk), lambda i, j, k: (i, k)), pl.BlockSpec((tk, tn), lambda i, j, k: (k, j))]
    if epi == "plain":
        out_shape = [jax.ShapeDtypeStruct((m, n), F32)]
    elif epi == "resid":
        in_specs += [tile, pl.BlockSpec((1, tn), lambda i, j, k: (0, j))]
        out_shape = [jax.ShapeDtypeStruct((m, n), F32)] * 2
    else:
        out_shape = [jax.ShapeDtypeStruct((m, n), _BF), jax.ShapeDtypeStruct((m, n), F32)]
    outs = _pcall(
        body, name=name, grid=(m // tm, n // tn, nk), out_shape=out_shape,
        in_specs=in_specs, out_specs=[tile] * n_out,
        scratch_shapes=[pltpu.VMEM((tm, tn), F32)] if nk > 1 else [],
        compiler_params=_cp(PAR, PAR, ARB),
    )(a, b, *extras)
    return outs[0] if n_out == 1 else outs


def _mm_nt(a, b, epi, extras, out_dtype, name):
    m, kdim = a.shape
    n = b.shape[0]
    tm, tn, tk = min(MM_T, m), min(MM_T, n), min(MM_T, kdim)
    nk = kdim // tk
    n_extra = len(extras)

    def finish(y, extra_refs, o_ref):
        if epi == "mul2":
            y = y * (2.0 * extra_refs[0][...].astype(F32))
        o_ref[...] = y.astype(o_ref.dtype)

    def body(a_ref, b_ref, *rest):
        extra_refs = rest[:n_extra]
        o_ref = rest[n_extra]
        part = _dot(a_ref[...].astype(_BF), b_ref[...].astype(_BF), 1, 1)
        if nk == 1:
            finish(part, extra_refs, o_ref)
        else:
            acc_ref = rest[-1]
            k = pl.program_id(2)

            @pl.when(k == 0)
            def _():
                acc_ref[...] = part

            @pl.when(k > 0)
            def _():
                acc_ref[...] += part

            @pl.when(k == nk - 1)
            def _():
                finish(acc_ref[...], extra_refs, o_ref)

    tile = pl.BlockSpec((tm, tn), lambda i, j, k: (i, j))
    in_specs = [pl.BlockSpec((tm, tk), lambda i, j, k: (i, k)), pl.BlockSpec((tn, tk), lambda i, j, k: (j, k))]
    in_specs += [tile] * n_extra
    return _pcall(
        body, name=name, grid=(m // tm, n // tn, nk), out_shape=jax.ShapeDtypeStruct((m, n), out_dtype),
        in_specs=in_specs, out_specs=tile,
        scratch_shapes=[pltpu.VMEM((tm, tn), F32)] if nk > 1 else [],
        compiler_params=_cp(PAR, PAR, ARB),
    )(a, b, *extras)


def _mm_tn(a, b, split, name):
    s, m = a.shape
    n = b.shape[1]
    ts = min(MM_T, s)
    ns = s // ts
    if split == "rows":
        tm, tn = min(MM_T, m // N_DEV), min(MM_T, n)
        per = (m // N_DEV) // tm
        out_shape = (N_DEV, m // N_DEV, n)
        out_spec = pl.BlockSpec((None, tm, tn), lambda i, j, k: (i // per, i % per, j))
    elif split == "cols":
        tm, tn = min(MM_T, m), min(MM_T, n // N_DEV)
        per = (n // N_DEV) // tn
        out_shape = (N_DEV, m, n // N_DEV)
        out_spec = pl.BlockSpec((None, tm, tn), lambda i, j, k: (j // per, i, j % per))
    else:
        tm, tn = min(MM_T, m), min(MM_T, n)
        out_shape = (m, n)
        out_spec = pl.BlockSpec((tm, tn), lambda i, j, k: (i, j))

    def body(a_ref, b_ref, o_ref, acc_ref):
        k = pl.program_id(2)
        part = _dot(a_ref[...].astype(_BF), b_ref[...].astype(_BF), 0, 0)

        @pl.when(k == 0)
        def _():
            acc_ref[...] = part

        @pl.when(k > 0)
        def _():
            acc_ref[...] += part

        @pl.when(k == ns - 1)
        def _():
            o_ref[...] = acc_ref[...].astype(o_ref.dtype)

    return _pcall(
        body, name=name, grid=(m // tm, n // tn, ns), out_shape=jax.ShapeDtypeStruct(out_shape, _XBF),
        in_specs=[pl.BlockSpec((ts, tm), lambda i, j, k: (k, i)), pl.BlockSpec((ts, tn), lambda i, j, k: (k, j))],
        out_specs=out_spec, scratch_shapes=[pltpu.VMEM((tm, tn), F32)],
        compiler_params=_cp(PAR, PAR, ARB),
    )(a, b)


def _head_masks():
    lane = _iota2((1, 128), 1)
    return [(lane // HEAD_DIM) == hh for hh in range(2)]


def _softplus_parts(z):
    sp = jnp.maximum(z, 0.0) + jnp.log(1.0 + jnp.exp(-jnp.abs(z)))
    return -sp, z - sp


def _sb_key_loop(i, carry, step):
    def cond(c):
        return jnp.logical_and(c[0] <= i, jnp.max(jnp.maximum(c[1][0][0], c[1][0][1])) > UNDERFLOW)

    def body(c):
        return c[0] + 1, step(i - c[0], c[1])

    return lax.while_loop(cond, body, (jnp.int32(1), carry))[1]


def _sb_fwd(proj, name):
    s = proj.shape[0]
    t = min(ATT_T, s)
    nblk = s // t
    scale = HEAD_DIM ** -0.5

    def body(q_ref, k_ref, v_ref, o_ref):
        i = pl.program_id(1)
        masks = _head_masks()
        row = _iota2((t, t), 0)
        col = _iota2((t, t), 1)
        u_strict = (row > col).astype(_BF)
        causal = col < row
        qs = q_ref[...] * scale
        qm = [jnp.where(masks[hh], qs, 0.0).astype(_BF) for hh in range(2)]

        def block(j, carry, diag):
            off = pl.multiple_of(j * t, t)
            kb = k_ref[pl.ds(off, t), :].astype(_BF)
            vb = v_ref[pl.ds(off, t), :].astype(_BF)
            runs, outs = [], []
            for hh in range(2):
                run, o = carry[0][hh], carry[1][hh]
                z = _dot(qm[hh], kb, 1, 1)
                l1, lb = _softplus_parts(z)
                if diag:
                    l1 = jnp.where(causal, l1, 0.0)
                between = run + _dot_exact(l1, u_strict, 1, 0, 2)
                a = jnp.exp(lb + between)
                if diag:
                    a = jnp.where(causal, a, 0.0)
                outs.append(o + _dot_exact(a, vb, 1, 0, 2))
                runs.append(run + jnp.sum(l1, axis=-1, keepdims=True))
            return tuple(runs), tuple(outs)

        zero, zero_o = jnp.zeros((t, 1), F32), jnp.zeros((t, 128), F32)
        carry = block(i, ((zero, zero), (zero_o, zero_o)), True)
        carry = _sb_key_loop(i, carry, lambda j, c: block(j, c, False))
        o_ref[...] = jnp.where(masks[0], carry[1][0], carry[1][1])

    kv_spec = lambda c0: pl.BlockSpec((s, 128), lambda p, i: (0, c0 // 128 + p))
    return _pcall(
        body, name=name, grid=(SB_HEADS // 2, nblk), out_shape=jax.ShapeDtypeStruct((s, 256), F32),
        in_specs=[pl.BlockSpec((t, 128), lambda p, i: (i, C_QA // 128 + p)), kv_spec(C_KA), kv_spec(C_VA)],
        out_specs=pl.BlockSpec((t, 128), lambda p, i: (i, p)), compiler_params=_cp(PAR, ARB),
    )(proj, proj, proj)


def _sb_bwd(proj, o, dmix, name):
    s = proj.shape[0]
    t = min(ATT_T, s)
    nblk = s // t
    scale = HEAD_DIM ** -0.5

    def body(q_ref, k_ref, v_ref, o_ref, do_ref, dq_ref, dk_ref, dv_ref):
        i = pl.program_id(1)
        masks = _head_masks()
        row = _iota2((t, t), 0)
        col = _iota2((t, t), 1)
        u_strict = (row > col).astype(_BF)
        u_incl = (row >= col).astype(_BF)
        causal = col < row

        @pl.when(i == 0)
        def _():
            dk_ref[...] = jnp.zeros_like(dk_ref)
            dv_ref[...] = jnp.zeros_like(dv_ref)

        qs = q_ref[...] * scale
        dov = do_ref[...]
        ov = o_ref[...]
        qm = [jnp.where(masks[hh], qs, 0.0).astype(_BF) for hh in range(2)]
        dom = [jnp.where(masks[hh], dov, 0.0).astype(_BF) for hh in range(2)]
        dsum = [jnp.sum(dom[hh].astype(F32) * ov, axis=-1, keepdims=True) for hh in range(2)]

        def block(j, carry, diag):
            off = pl.multiple_of(j * t, t)
            kb = k_ref[pl.ds(off, t), :].astype(_BF)
            vb = v_ref[pl.ds(off, t), :].astype(_BF)
            runs, eruns, dqs = [], [], []
            dk = dv = None
            for hh in range(2):
                run, erun, dq = carry[0][hh], carry[1][hh], carry[2][hh]
                z = _dot(qm[hh], kb, 1, 1)
                l1, lb = _softplus_parts(z)
                if diag:
                    l1 = jnp.where(causal, l1, 0.0)
                between = run + _dot_exact(l1, u_strict, 1, 0, 2)
                a = jnp.exp(lb + between)
                if diag:
                    a = jnp.where(causal, a, 0.0)
                g = a * _dot(dom[hh], vb, 1, 1)
                cum = dsum[hh] - (erun + _dot_exact(g, u_incl, 1, 0, 2))
                beta = jnp.exp(lb)
                dz = g * (1.0 - beta) - cum * beta
                if diag:
                    dz = jnp.where(causal, dz, 0.0)
                dzb = dz.astype(_BF)
                dqs.append(dq + _dot(dzb, kb, 1, 0))
                dk_h = _dot(dzb, qm[hh], 0, 0)
                dv_h = _dot(a.astype(_BF), dom[hh], 0, 0)
                dk = dk_h if dk is None else dk + dk_h
                dv = dv_h if dv is None else dv + dv_h
                runs.append(run + jnp.sum(l1, axis=-1, keepdims=True))
                eruns.append(erun + jnp.sum(g, axis=-1, keepdims=True))
            dk_ref[pl.ds(off, t), :] += dk
            dv_ref[pl.ds(off, t), :] += dv
            return tuple(runs), tuple(eruns), tuple(dqs)

        zero, zero_q = jnp.zeros((t, 1), F32), jnp.zeros((t, 128), F32)
        carry = block(i, ((zero, zero), (zero, zero), (zero_q, zero_q)), True)
        carry = _sb_key_loop(i, carry, lambda j, c: block(j, c, False))
        dq_ref[...] = jnp.where(masks[0], carry[2][0], carry[2][1]) * scale

    kv_spec = lambda c0: pl.BlockSpec((s, 128), lambda p, i: (0, c0 // 128 + p))
    blk = lambda c0: pl.BlockSpec((t, 128), lambda p, i: (i, c0 // 128 + p))
    acc = pl.BlockSpec((s, 128), lambda p, i: (0, p))
    shp = jax.ShapeDtypeStruct((s, 256), F32)
    return _pcall(
        body, name=name, grid=(SB_HEADS // 2, nblk), out_shape=[shp, shp, shp],
        in_specs=[blk(C_QA), kv_spec(C_KA), kv_spec(C_VA), blk(0), blk(M_SB)],
        out_specs=[blk(0), acc, acc], compiler_params=_cp(PAR, ARB),
    )(proj, proj, proj, o, dmix)


def _group_matrix(width):
    r = _iota2((width, width), 0) // HEAD_DIM
    c = _iota2((width, width), 1) // HEAD_DIM
    return (r == c).astype(_BF)


def _group_mean(v, gm):
    return _dot_exact(v, gm, 1, 0, 2) * (1.0 / HEAD_DIM)


def _fox_prep(proj, gq, gk, bf, name):
    s = proj.shape[0]
    t = min(ATT_T, s)
    nblk = s // t

    def body(q_ref, k_ref, fl_ref, gq_ref, gk_ref, bf_ref, qn_ref, kn_ref, cfc_ref, cfr_ref, carry_ref):
        i = pl.program_id(0)
        gm = _group_matrix(512)
        for src, gref, dst in ((q_ref, gq_ref, qn_ref), (k_ref, gk_ref, kn_ref)):
            v = src[...]
            rstd = lax.rsqrt(_group_mean(v * v, gm) + EPS)
            dst[...] = (v * rstd * gref[...]).astype(dst.dtype)

        @pl.when(i == 0)
        def _():
            carry_ref[...] = jnp.zeros_like(carry_ref)

        lane = _iota2((1, 128), 1)
        logf = jnp.where(lane < FOX_HEADS, _softplus_parts(-(fl_ref[...] + bf_ref[...]))[0], 0.0)
        lower = (_iota2((t, t), 0) >= _iota2((t, t), 1)).astype(_BF)
        cf = carry_ref[...] + _dot_exact(logf, lower, 1, 0, 3, v_left=False)
        cfc_ref[...] = cf
        cfr_ref[...] = cf.T[:FOX_HEADS, :]
        carry_ref[...] = cf[t - 1:t, :]

    col = lambda w, c0: pl.BlockSpec((t, w), lambda i: (i, c0 // w))
    v512 = pl.BlockSpec((1, 512), lambda i: (0, 0))
    return _pcall(
        body, name=name, grid=(nblk,),
        out_shape=[jax.ShapeDtypeStruct((s, 512), _BF), jax.ShapeDtypeStruct((s, 512), _BF),
                   jax.ShapeDtypeStruct((s, 128), F32), jax.ShapeDtypeStruct((FOX_HEADS, s), F32)],
        in_specs=[col(512, C_QB), col(512, C_KB), col(128, C_FL), v512, v512, pl.BlockSpec((1, 128), lambda i: (0, 0))],
        out_specs=[col(512, 0), col(512, 0), col(128, 0), pl.BlockSpec((FOX_HEADS, t), lambda i: (0, i))],
        scratch_shapes=[pltpu.VMEM((1, 128), F32)], compiler_params=_cp(ARB),
    )(proj, proj, proj, gq, gk, bf)


def _fox_cq(cfc_blk, head):
    lane = _iota2((1, 128), 1)
    return jnp.sum(jnp.where(lane == head, cfc_blk, 0.0), axis=-1, keepdims=True)


def _fox_ck(cfr_ref, off, t, head):
    sub = _iota2((FOX_HEADS, 1), 0)
    return jnp.sum(jnp.where(sub == head, cfr_ref[:, pl.ds(off, t)], 0.0), axis=0, keepdims=True)


def _fox_last_key(cfr_ref, j, t, head):
    return jnp.min(_fox_ck(cfr_ref, pl.multiple_of(j * t, t), t, head))


def _grid_ends(n_outer, n_inner):
    first = lambda: jnp.logical_and(pl.program_id(0) == 0, pl.program_id(1) == 0)
    last = lambda: jnp.logical_and(pl.program_id(0) == n_outer - 1, pl.program_id(1) == n_inner - 1)
    return first, last


def _fox_fwd(proj, qn, kn, cfc, cfr, qk_bound, name, comm=None):
    s = proj.shape[0]
    t = min(ATT_T, s)
    nblk = s // t
    scale = HEAD_DIM ** -0.5

    def body(q_ref, k_ref, v_ref, cfc_ref, cfr_ref, qkb_ref, o_ref, lse_ref):
        p = pl.program_id(0)
        i = pl.program_id(1)
        masks = _head_masks()
        lane = _iota2((1, 128), 1)
        valid = _iota2((t, t), 1) <= _iota2((t, t), 0)
        qv = q_ref[...]
        cfc_blk = cfc_ref[...]
        heads = [2 * p + hh for hh in range(2)]
        qm = [(jnp.where(masks[hh], qv, 0) * scale).astype(_BF) for hh in range(2)]
        cq = [_fox_cq(cfc_blk, heads[hh]) for hh in range(2)]

        def block(j, carry, diag):
            off = pl.multiple_of(j * t, t)
            kb = k_ref[pl.ds(off, t), :]
            vb = v_ref[pl.ds(off, t), :].astype(_BF)
            ms, ls, accs = [], [], []
            for hh in range(2):
                m, l, acc = carry[0][hh], carry[1][hh], carry[2][hh]
                z = _dot(qm[hh], kb, 1, 1) + (cq[hh] - _fox_ck(cfr_ref, off, t, heads[hh]))
                if diag:
                    z = jnp.where(valid, z, NEG)
                m_new = jnp.maximum(m, jnp.max(z, axis=-1, keepdims=True))
                pe = jnp.exp(z - m_new)
                alpha = jnp.exp(m - m_new)
                ms.append(m_new)
                ls.append(alpha * l + jnp.sum(pe, axis=-1, keepdims=True))
                accs.append(alpha * acc + _dot_exact(pe, vb, 1, 0, 2))
            return tuple(ms), tuple(ls), tuple(accs)

        def cond(c):
            j = jnp.maximum(i - c[0], 0)
            reach = [jnp.max(qkb_ref[...] + cq[hh] - c[1][0][hh]) - _fox_last_key(cfr_ref, j, t, heads[hh])
                     for hh in range(2)]
            return jnp.logical_and(c[0] <= i, jnp.maximum(reach[0], reach[1]) > UNDERFLOW)

        neg, zero, zero_o = jnp.full((t, 1), NEG, F32), jnp.zeros((t, 1), F32), jnp.zeros((t, 128), F32)
        carry = block(i, ((neg, neg), (zero, zero), (zero_o, zero_o)), True)
        m, l, acc = lax.while_loop(cond, lambda c: (c[0] + 1, block(i - c[0], c[1], False)), (jnp.int32(1), carry))[1]
        o_ref[...] = jnp.where(masks[0], acc[0] / l[0], acc[1] / l[1])
        lse_ref[...] = jnp.where(lane == 0, m[0] + jnp.log(l[0]), jnp.where(lane == 1, m[1] + jnp.log(l[1]), 0.0))

    c_operands, c_in_specs, c_out_shapes, c_out_specs, c_sems = _comm_operands(comm) if comm else ([], [], [], [], [])
    outs = _pcall(
        _with_comm(body, comm, 6, 2, *_grid_ends(FOX_HEADS // 2, nblk)), name=name, grid=(FOX_HEADS // 2, nblk),
        out_shape=[jax.ShapeDtypeStruct((s, 512), F32), jax.ShapeDtypeStruct((FOX_HEADS // 2, s, 128), F32)] + c_out_shapes,
        in_specs=[pl.BlockSpec((t, 128), lambda p, i: (i, p)), pl.BlockSpec((s, 128), lambda p, i: (0, p)),
                  pl.BlockSpec((s, 128), lambda p, i: (0, C_VB // 128 + p)),
                  pl.BlockSpec((t, 128), lambda p, i: (i, 0)), pl.BlockSpec((FOX_HEADS, s), lambda p, i: (0, 0)),
                  pl.BlockSpec((1, 1), lambda p, i: (0, 0))] + c_in_specs,
        out_specs=[pl.BlockSpec((t, 128), lambda p, i: (i, p)),
                   pl.BlockSpec((None, t, 128), lambda p, i: (p, i, 0))] + c_out_specs,
        scratch_shapes=c_sems, input_output_aliases=_comm_aliases(comm, 6, 2),
        compiler_params=_cp(ARB, ARB),
    )(qn, kn, proj, cfc, cfr, qk_bound, *c_operands)
    return outs[0], outs[1], list(outs[2:])


def _fox_bwd(proj, qn, kn, cfc, cfr, qk_bound, o, lse, dmix, name, comm=None):
    s = proj.shape[0]
    t = min(ATT_T, s)
    nblk = s // t
    scale = HEAD_DIM ** -0.5

    def body(q_ref, k_ref, v_ref, cfc_ref, cfr_ref, qkb_ref, o_ref, lse_ref, do_ref, dq_ref, dk_ref, dv_ref, dcr_ref):
        p = pl.program_id(0)
        i = pl.program_id(1)
        masks = _head_masks()
        lane = _iota2((1, 128), 1)
        valid = _iota2((t, t), 1) <= _iota2((t, t), 0)
        before = (_iota2((t, t), 0) < _iota2((t, t), 1)).astype(_BF)

        @pl.when(i == 0)
        def _():
            dk_ref[...] = jnp.zeros_like(dk_ref)
            dv_ref[...] = jnp.zeros_like(dv_ref)
            dcr_ref[...] = jnp.zeros_like(dcr_ref)

        qv = q_ref[...]
        cfc_blk = cfc_ref[...]
        dov = do_ref[...]
        ov = o_ref[...]
        lse_blk = lse_ref[...]
        heads = [2 * p + hh for hh in range(2)]
        qm = [(jnp.where(masks[hh], qv, 0) * scale).astype(_BF) for hh in range(2)]
        dom = [jnp.where(masks[hh], dov, 0.0).astype(_BF) for hh in range(2)]
        dsum = [jnp.sum(dom[hh].astype(F32) * ov, axis=-1, keepdims=True) for hh in range(2)]
        lse_h = [jnp.sum(jnp.where(lane == hh, lse_blk, 0.0), axis=-1, keepdims=True) for hh in range(2)]
        cq = [_fox_cq(cfc_blk, heads[hh]) for hh in range(2)]
        reach = [jnp.max(qkb_ref[...] + cq[hh] - lse_h[hh]) for hh in range(2)]

        def matters(j):
            jm = jnp.maximum(j - 1, 0)
            far = [reach[hh] - _fox_last_key(cfr_ref, jm, t, heads[hh]) for hh in range(2)]
            return jnp.logical_and(j > 0, jnp.maximum(far[0], far[1]) > UNDERFLOW)

        first = lax.while_loop(matters, lambda j: j - 1, i)

        def block(j, carry, diag):
            off = pl.multiple_of(j * t, t)
            kb = k_ref[pl.ds(off, t), :]
            vb = v_ref[pl.ds(off, t), :].astype(_BF)
            dqs, pres = [], []
            dk = dv = None
            for hh in range(2):
                dq, pre = carry[0][hh], carry[1][hh]
                z = _dot(qm[hh], kb, 1, 1) + (cq[hh] - _fox_ck(cfr_ref, off, t, heads[hh]))
                pm = jnp.exp(z - lse_h[hh])
                if diag:
                    pm = jnp.where(valid, pm, 0.0)
                ds = pm * (_dot(dom[hh], vb, 1, 1) - dsum[hh])
                dsb = ds.astype(_BF)
                dqs.append(dq + _dot(dsb, kb, 1, 0))
                dk_h = _dot(dsb, qm[hh], 0, 0)
                dv_h = _dot(pm.astype(_BF), dom[hh], 0, 0)
                dk = dk_h if dk is None else dk + dk_h
                dv = dv_h if dv is None else dv + dv_h
                lower_keys = pre + _dot_exact(ds, before, 1, 0, 2)
                if diag:
                    lower_keys = jnp.where(valid, lower_keys, 0.0)
                dcr_ref[hh:hh + 1, pl.ds(off, t)] += jnp.sum(lower_keys, axis=0, keepdims=True)
                pres.append(pre + jnp.sum(ds, axis=-1, keepdims=True))
            dk_ref[pl.ds(off, t), :] += dk
            dv_ref[pl.ds(off, t), :] += dv
            return tuple(dqs), tuple(pres)

        zero, zero_q = jnp.zeros((t, 1), F32), jnp.zeros((t, 128), F32)
        carry = lax.fori_loop(first, i, lambda j, c: block(j, c, False), ((zero_q, zero_q), (zero, zero)))
        dqs, _ = block(i, carry, True)
        dq_ref[...] = jnp.where(masks[0], dqs[0], dqs[1]) * scale

    blk = lambda c0: pl.BlockSpec((t, 128), lambda p, i: (i, c0 // 128 + p))
    res = lambda c0: pl.BlockSpec((s, 128), lambda p, i: (0, c0 // 128 + p))
    shp = jax.ShapeDtypeStruct((s, 512), F32)
    c_operands, c_in_specs, c_out_shapes, c_out_specs, c_sems = _comm_operands(comm) if comm else ([], [], [], [], [])
    outs = _pcall(
        _with_comm(body, comm, 9, 4, *_grid_ends(FOX_HEADS // 2, nblk)), name=name, grid=(FOX_HEADS // 2, nblk),
        out_shape=[shp, shp, shp, jax.ShapeDtypeStruct((FOX_HEADS // 2, 8, s), F32)] + c_out_shapes,
        in_specs=[blk(0), res(0), res(C_VB), pl.BlockSpec((t, 128), lambda p, i: (i, 0)),
                  pl.BlockSpec((FOX_HEADS, s), lambda p, i: (0, 0)), pl.BlockSpec((1, 1), lambda p, i: (0, 0)), blk(0),
                  pl.BlockSpec((None, t, 128), lambda p, i: (p, i, 0)), blk(M_FOX)] + c_in_specs,
        out_specs=[blk(0), res(0), res(0), pl.BlockSpec((None, 8, s), lambda p, i: (p, 0, 0))] + c_out_specs,
        scratch_shapes=c_sems, input_output_aliases=_comm_aliases(comm, 9, 4),
        compiler_params=_cp(ARB, ARB),
    )(qn, kn, proj, cfc, cfr, qk_bound, o, lse, dmix, *c_operands)
    return outs[0], outs[1], outs[2], outs[3], list(outs[4:])


def _fox_prep_bwd(proj, gq, gk, bf, dqn, dkn, dlogf, name):
    s = proj.shape[0]
    t = min(ATT_T, s)
    nblk = s // t

    def body(q_ref, k_ref, fl_ref, gq_ref, gk_ref, bf_ref, dqn_ref, dkn_ref, dlogf_ref,
             dq_ref, dk_ref, dfl_ref, dgq_ref, dgk_ref, dbf_ref):
        i = pl.program_id(0)
        gm = _group_matrix(512)

        @pl.when(i == 0)
        def _():
            dgq_ref[...] = jnp.zeros_like(dgq_ref)
            dgk_ref[...] = jnp.zeros_like(dgk_ref)
            dbf_ref[...] = jnp.zeros_like(dbf_ref)

        for src, gref, dyref, dst, dgref in ((q_ref, gq_ref, dqn_ref, dq_ref, dgq_ref),
                                              (k_ref, gk_ref, dkn_ref, dk_ref, dgk_ref)):
            v = src[...]
            dy = dyref[...]
            rstd = lax.rsqrt(_group_mean(v * v, gm) + EPS)
            xhat = v * rstd
            dgref[...] += jnp.sum(dy * xhat, axis=0, keepdims=True)
            dxh = dy * gref[...]
            dst[...] = (rstd * (dxh - xhat * _group_mean(dxh * xhat, gm))).astype(dst.dtype)

        xv = fl_ref[...] + bf_ref[...]
        e = jnp.exp(-jnp.abs(xv))
        dfl = dlogf_ref[...] * (jnp.where(xv >= 0.0, e, 1.0) / (1.0 + e))
        dfl_ref[...] = dfl.astype(dfl_ref.dtype)
        dbf_ref[...] += jnp.sum(dfl, axis=0, keepdims=True)

    col = lambda w, c0: pl.BlockSpec((t, w), lambda i: (i, c0 // w))
    v512 = pl.BlockSpec((1, 512), lambda i: (0, 0))
    v128 = pl.BlockSpec((1, 128), lambda i: (0, 0))
    return _pcall(
        body, name=name, grid=(nblk,),
        out_shape=[jax.ShapeDtypeStruct((s, 512), _BF), jax.ShapeDtypeStruct((s, 512), _BF),
                   jax.ShapeDtypeStruct((s, 128), _BF), jax.ShapeDtypeStruct((1, 512), F32),
                   jax.ShapeDtypeStruct((1, 512), F32), jax.ShapeDtypeStruct((1, 128), F32)],
        in_specs=[col(512, C_QB), col(512, C_KB), col(128, C_FL), v512, v512, v128, col(512, 0), col(512, 0), col(128, 0)],
        out_specs=[col(512, 0), col(512, 0), col(128, 0), v512, v512, v128], compiler_params=_cp(ARB),
    )(proj, proj, proj, gq, gk, bf, dqn, dkn, dlogf)


GELU_C = 0.7978845608028654
GELU_A = 0.044715


def _gelu(x):
    return 0.5 * x * (1.0 + jnp.tanh(GELU_C * (x + GELU_A * x * x * x)))


def _gelu_grad(x):
    th = jnp.tanh(GELU_C * (x + GELU_A * x * x * x))
    return 0.5 * (1.0 + th) + 0.5 * x * (1.0 - th * th) * (GELU_C * (1.0 + 3.0 * GELU_A * x * x))


def _sgu_tril():
    return _iota2((SGU_CHUNK, SGU_CHUNK), 0) >= _iota2((SGU_CHUNK, SGU_CHUNK), 1)


def _sgu_group_masks():
    lane = _iota2((1, 256), 1)
    return [(lane // HEAD_DIM) == g for g in range(SGU_GROUPS)]


def _sgu_fwd(proj, w, gs, bexp, name):
    s = proj.shape[0]
    t = min(2 * SGU_CHUNK, s)

    def body(u_ref, v_ref, w_ref, gs_ref, b_ref, o_ref):
        gm = _group_matrix(256)
        gmask = _sgu_group_masks()
        tril = _sgu_tril()
        u = _gelu(u_ref[...])
        vg = _gelu(v_ref[...])
        vhat = (vg * lax.rsqrt(_group_mean(vg * vg, gm) + EPS) * gs_ref[...]).astype(_BF)
        for ch in range(t // SGU_CHUNK):
            rows = slice(ch * SGU_CHUNK, (ch + 1) * SGU_CHUNK)
            mixed = b_ref[...]
            for g in range(SGU_GROUPS):
                wg = jnp.where(tril, w_ref[g], 0.0).astype(_BF)
                mixed = jnp.where(gmask[g], mixed + _dot(wg, vhat[rows], 1, 0), mixed)
            o_ref[rows, :] = u[rows] * mixed

    return _pcall(
        body, name=name, grid=(s // t,), out_shape=jax.ShapeDtypeStruct((s, 256), F32),
        in_specs=[_row_spec(t, 256, C_UC // 256), _row_spec(t, 256, C_VC // 256),
                  pl.BlockSpec((SGU_GROUPS, SGU_CHUNK, SGU_CHUNK), lambda i: (0, 0, 0)), _vec_spec(256),
                  pl.BlockSpec((SGU_CHUNK, 256), lambda i: (0, 0))],
        out_specs=_row_spec(t, 256), compiler_params=_cp(PAR),
    )(proj, proj, w, gs, bexp)


def _sgu_bwd(proj, w, gs, bexp, dmix, name):
    s = proj.shape[0]
    t = min(2 * SGU_CHUNK, s)
    nstep = s // t

    def body(u_ref, v_ref, w_ref, gs_ref, b_ref, do_ref, du_ref, dv_ref, dw_ref, db_ref, dgs_ref):
        i = pl.program_id(0)
        gm = _group_matrix(256)
        gmask = _sgu_group_masks()
        tril = _sgu_tril()

        @pl.when(i == 0)
        def _():
            dw_ref[...] = jnp.zeros_like(dw_ref)
            db_ref[...] = jnp.zeros_like(db_ref)
            dgs_ref[...] = jnp.zeros_like(dgs_ref)

        uc = u_ref[...]
        vc = v_ref[...]
        u = _gelu(uc)
        vg = _gelu(vc)
        rstd = lax.rsqrt(_group_mean(vg * vg, gm) + EPS)
        xh = vg * rstd
        gsv = gs_ref[...]
        vhat = (xh * gsv).astype(_BF)
        dov = do_ref[...]
        dm = dov * u
        for ch in range(t // SGU_CHUNK):
            rows = slice(ch * SGU_CHUNK, (ch + 1) * SGU_CHUNK)
            mixed = b_ref[...]
            dvh = jnp.zeros((SGU_CHUNK, 256), F32)
            dmc = dm[rows]
            for g in range(SGU_GROUPS):
                wg = jnp.where(tril, w_ref[g], 0.0).astype(_BF)
                mixed = jnp.where(gmask[g], mixed + _dot(wg, vhat[rows], 1, 0), mixed)
                dvh = jnp.where(gmask[g], _dot(wg, dmc.astype(_BF), 0, 0), dvh)
                dw_ref[g] += _dot(jnp.where(gmask[g], dmc, 0.0).astype(_BF), vhat[rows], 1, 1)
            db_ref[...] += dmc
            du_ref[rows, :] = (dov[rows] * mixed * _gelu_grad(uc[rows])).astype(du_ref.dtype)
            xhc = xh[rows]
            dgs_ref[...] += jnp.sum(dvh * xhc, axis=0, keepdims=True)
            dxh = dvh * gsv
            dvg = rstd[rows] * (dxh - xhc * _group_mean(dxh * xhc, gm))
            dv_ref[rows, :] = (dvg * _gelu_grad(vc[rows])).astype(dv_ref.dtype)

        @pl.when(i == nstep - 1)
        def _():
            for g in range(SGU_GROUPS):
                dw_ref[g] = jnp.where(tril, dw_ref[g], 0.0)

    wspec = pl.BlockSpec((SGU_GROUPS, SGU_CHUNK, SGU_CHUNK), lambda i: (0, 0, 0))
    bspec = pl.BlockSpec((SGU_CHUNK, 256), lambda i: (0, 0))
    return _pcall(
        body, name=name, grid=(nstep,),
        out_shape=[jax.ShapeDtypeStruct((s, 256), _BF), jax.ShapeDtypeStruct((s, 256), _BF),
                   jax.ShapeDtypeStruct((SGU_GROUPS, SGU_CHUNK, SGU_CHUNK), F32),
                   jax.ShapeDtypeStruct((SGU_CHUNK, 256), F32), jax.ShapeDtypeStruct((1, 256), F32)],
        in_specs=[_row_spec(t, 256, C_UC // 256), _row_spec(t, 256, C_VC // 256), wspec, _vec_spec(256), bspec,
                  _row_spec(t, 256, M_SGU // 256)],
        out_specs=[_row_spec(t, 256), _row_spec(t, 256), wspec, bspec, _vec_spec(256)],
        compiler_params=_cp(ARB),
    )(proj, proj, w, gs, bexp, dmix)


def _ada_fwd(c_all, ada_w, name):
    depth, d, n = ada_w.shape

    def body(c_ref, w_ref, o_ref):
        cv = c_ref[...]
        cond = cv / (1.0 + jnp.exp(-cv))
        o_ref[...] = _dot_f32(cond, w_ref[...], 1, 0)

    return _pcall(
        body, name=name, grid=(depth,), out_shape=jax.ShapeDtypeStruct((depth, N_DEV, n), F32),
        in_specs=[pl.BlockSpec((N_DEV, d), lambda l: (0, 0)), pl.BlockSpec((None, d, n), lambda l: (l, 0, 0))],
        out_specs=pl.BlockSpec((None, N_DEV, n), lambda l: (l, 0, 0)), compiler_params=_cp(PAR),
    )(c_all, ada_w)


def _ada_bwd(c_all, dmod, name):
    depth, _, n = dmod.shape
    d = c_all.shape[1]

    def body(c_ref, dm_ref, o_ref):
        cv = c_ref[...]
        cond = cv / (1.0 + jnp.exp(-cv))
        o_ref[...] = _dot_f32(cond, dm_ref[...], 0, 0)

    return _pcall(
        body, name=name, grid=(depth,), out_shape=jax.ShapeDtypeStruct((depth, d, n), F32),
        in_specs=[pl.BlockSpec((N_DEV, d), lambda l: (0, 0)), pl.BlockSpec((None, N_DEV, n), lambda l: (l, 0, 0))],
        out_specs=pl.BlockSpec((None, d, n), lambda l: (l, 0, 0)), compiler_params=_cp(PAR),
    )(c_all, dmod)


def _adamw(slots, w, m, v, name):
    n, r, c = slots.shape
    tr = 256 if r % 256 == 0 else r
    bc1 = 1.0 - ADAM_B1 ** ADAM_STEP
    bc2 = 1.0 - ADAM_B2 ** ADAM_STEP

    def body(s_ref, w_ref, m_ref, v_ref, g_ref, d_ref, nm_ref, nv_ref):
        g = s_ref[0].astype(F32)
        for j in range(1, n):
            g = g + s_ref[j].astype(F32)
        m_new = ADAM_B1 * m_ref[...] + (1.0 - ADAM_B1) * g
        v_new = ADAM_B2 * v_ref[...] + (1.0 - ADAM_B2) * (g * g)
        g_ref[...] = g
        nm_ref[...] = m_new
        nv_ref[...] = v_new
        d_ref[...] = -ADAM_LR * ((m_new / bc1) / (jnp.sqrt(v_new / bc2) + ADAM_EPS) + ADAM_WD * w_ref[...])

    tile = pl.BlockSpec((tr, c), lambda i: (i, 0))
    shp = jax.ShapeDtypeStruct((r, c), F32)
    return _pcall(
        body, name=name, grid=(r // tr,), out_shape=[shp] * 4,
        in_specs=[pl.BlockSpec((n, tr, c), lambda i: (0, i, 0)), tile, tile, tile],
        out_specs=[tile] * 4, compiler_params=_cp(PAR),
    )(slots, w, m, v)


def _pad_cols(a, width):
    return jnp.pad(a, ((0, 0), (0, width - a.shape[1])))


def _w_in_layout(w):
    pad = jnp.zeros(w.shape[:-1] + (IN_P - IN_W,), w.dtype)
    return jnp.concatenate([w[..., 768:FL_SRC], w[..., :768], w[..., UC_SRC:], w[..., FL_SRC:UC_SRC], pad], axis=-1)


def _w_in_unlayout(g):
    return jnp.concatenate([g[..., C_QA:C_UC], g[..., :C_QA], g[..., C_FL:C_FL + FOX_HEADS], g[..., C_UC:C_FL]], axis=-1)


SMALL = [("ada_b", DEPTH * 6 * D_MODEL), ("norm1_g", DEPTH * D_MODEL), ("norm2_g", DEPTH * D_MODEL),
         ("sgu_w", DEPTH * SGU_GROUPS * SGU_CHUNK * SGU_CHUNK), ("sgu_b", DEPTH * SGU_GROUPS * SGU_CHUNK),
         ("sgu_norm_g", DEPTH * SGU_GROUPS * HEAD_DIM), ("q_norm_g", DEPTH * HEAD_DIM), ("k_norm_g", DEPTH * HEAD_DIM),
         ("b_forget", DEPTH * FOX_HEADS), ("loss", 1)]
SMALL_ROWS = 2560


def _pack_small(parts):
    flat = jnp.concatenate([parts[name].reshape(-1).astype(F32) for name, _ in SMALL])
    return jnp.pad(flat, (0, SMALL_ROWS * 128 - flat.shape[0])).reshape(SMALL_ROWS, 128)


def _unpack_small(packed, shapes):
    flat = packed.reshape(-1)
    out, off = {}, 0
    for name, size in SMALL:
        out[name] = flat[off:off + size].reshape(shapes[name])
        off += size
    return out


def kernel(x, c, ada_w, ada_b, norm1_g, norm2_g, w_in, b_forget, q_norm_g, k_norm_g, sgu_norm_g, sgu_w, sgu_b, w_out, mlp_w1, mlp_w2, loss_target, m_ada_w, m_ada_b, m_norm1_g, m_norm2_g, m_w_in, m_b_forget, m_q_norm_g, m_k_norm_g, m_sgu_norm_g, m_sgu_w, m_sgu_b, m_w_out, m_mlp_w1, m_mlp_w2, v_ada_w, v_ada_b, v_norm1_g, v_norm2_g, v_w_in, v_b_forget, v_q_norm_g, v_k_norm_g, v_sgu_norm_g, v_sgu_w, v_sgu_b, v_w_out, v_mlp_w1, v_mlp_w2):
    me = _lin(_my_pos())
    x0 = x[0]
    target = loss_target[0]
    n_ada = ada_w.shape[2]

    shards = [w.astype(_XBF) for w in (w_in, w_out, mlp_w1, mlp_w2)]

    def whole(w_in_g, w_out_g, w1_g, w2_g):
        return (_w_in_layout(jnp.transpose(w_in_g, (1, 0, 2)).reshape(D_MODEL, IN_W)),
                w_out_g.reshape(D_MODEL, D_MODEL),
                jnp.transpose(w1_g, (1, 0, 2)).reshape(D_MODEL, D_FF),
                w2_g.reshape(D_FF, D_MODEL))

    *first, c_all = _exchange([w[0] for w in shards] + [c], False, "gather_weights")
    c_all = c_all.reshape(N_DEV, D_MODEL)
    weights = [whole(*first)]

    mod_part = _ada_fwd(c_all, ada_w, "ada_fwd")
    (mod_rows,) = _exchange([jnp.transpose(mod_part, (1, 0, 2))], True, "scatter_mod")
    mod = jnp.transpose(mod_rows, (1, 0, 2)).reshape(DEPTH, 6 * D_MODEL) + ada_b
    mods = mod.reshape(DEPTH, 6, 1, D_MODEL)

    saved = []
    xl = x0
    for l in range(DEPTH):
        sh1, sc1, g1, sh2, sc2, g2 = (mods[l, r] for r in range(6))
        n1 = norm1_g[l].reshape(1, D_MODEL)
        n2 = norm2_g[l].reshape(1, D_MODEL)
        gq = jnp.tile(q_norm_g[l], FOX_HEADS).reshape(1, 512)
        gk = jnp.tile(k_norm_g[l], FOX_HEADS).reshape(1, 512)
        bf = jnp.pad(b_forget[l], (0, 128 - FOX_HEADS)).reshape(1, 128)
        gs = sgu_norm_g[l].reshape(1, 256)
        bexp = jnp.repeat(sgu_b[l].T, HEAD_DIM, axis=1)

        w_in_l, w_out_l, w1_l, w2_l = weights[l]
        h1 = _norm_mod(xl, n1, sc1, sh1, "norm_mod")
        proj = _mm_nn(h1, w_in_l, "plain", (), "mm_in")
        o_sb = _sb_fwd(proj, "sb_fwd")
        qn, kn, cfc, cfr = _fox_prep(proj, gq, gk, bf, "fox_prep")
        qkb = (1.02 * HEAD_DIM ** 0.5 * jnp.max(jnp.abs(q_norm_g[l])) * jnp.max(jnp.abs(k_norm_g[l]))).reshape(1, 1)
        gather_next = None
        if l + 1 < DEPTH:
            gather_next = dict(arrays=[w[l + 1] for w in shards], scatter=False, recv=None, slot=None)
        o_fox, lse, gathered = _fox_fwd(proj, qn, kn, cfc, cfr, qkb, "fox_fwd", gather_next)
        if gathered:
            weights.append(whole(*gathered))
        o_sgu = _sgu_fwd(proj, sgu_w[l], gs, bexp, "sgu_fwd")
        mixed = jnp.concatenate([o_sb, o_fox, o_sgu], axis=-1).astype(_BF)
        x_mid, y1 = _mm_nn(mixed, w_out_l, "resid", (xl, g1), "mm_out")
        h2 = _norm_mod(x_mid, n2, sc2, sh2, "norm_mod")
        r2, r1 = _mm_nn(h2, w1_l, "relu2", (), "mm_w1")
        x_out, y2 = _mm_nn(r2, w2_l, "resid", (x_mid, g2), "mm_w2")
        saved.append(dict(x_in=xl, h1=h1, proj=proj, o_sb=o_sb, qn=qn, kn=kn, cfc=cfc, cfr=cfr, o_fox=o_fox, lse=lse,
                          mixed=mixed, x_mid=x_mid, y1=y1, h2=h2, r2=r2, r1=r1, y2=y2,
                          n1=n1, n2=n2, gq=gq, gk=gk, bf=bf, gs=gs, bexp=bexp, qkb=qkb))
        xl = x_out

    loss_part, dx = _loss_head(xl, target, "loss_head")

    grads_ready = None
    received = [jnp.zeros((N_DEV,) + w.shape, _XBF) for w in shards]
    small = {k: [None] * DEPTH for k in ("mod", "norm1_g", "norm2_g", "sgu_w", "sgu_b", "sgu_norm_g",
                                         "q_norm_g", "k_norm_g", "b_forget")}
    for l in reversed(range(DEPTH)):
        sv = saved[l]
        sh1, sc1, g1, sh2, sc2, g2 = (mods[l, r] for r in range(6))
        w_in_l, w_out_l, w1_l, w2_l = weights[l]
        dy2, dg2 = _res_bwd(dx, sv["y2"], g2, "res_bwd")
        da = _mm_nt(dy2, w2_l, "mul2", (sv["r1"],), _BF, "mm_w2_bwd")
        g_w2 = _mm_tn(sv["r2"], dy2, "rows", "mm_w2_grad")
        dh2 = _mm_nt(da, w1_l, "plain", (), F32, "mm_w1_bwd")
        g_w1 = _mm_tn(sv["h2"], da, "cols", "mm_w1_grad")
        dx_mid, dsh2, dsc2, dn2 = _norm_mod_bwd(sv["x_mid"], dh2, dx, sv["n2"], sc2, "norm_mod_bwd")
        dy1, dg1 = _res_bwd(dx_mid, sv["y1"], g1, "res_bwd")
        dmix = _mm_nt(dy1, w_out_l, "plain", (), F32, "mm_out_bwd")
        g_out = _mm_tn(sv["mixed"], dy1, "rows", "mm_out_grad")
        dq_sb, dk_sb, dv_sb = _sb_bwd(sv["proj"], sv["o_sb"], dmix, "sb_bwd")
        send_prev = None
        if grads_ready is not None:
            send_prev = dict(arrays=grads_ready, scatter=True, recv=received, slot=l + 1)
        dqn, dkn, dv_fox, dcr, filled = _fox_bwd(sv["proj"], sv["qn"], sv["kn"], sv["cfc"], sv["cfr"], sv["qkb"],
                                                 sv["o_fox"], sv["lse"], dmix, "fox_bwd", send_prev)
        if filled:
            received = filled
        dlogf = _pad_cols(dcr[:, :2, :].reshape(FOX_HEADS, -1).T, 128)
        dq_fox, dk_fox, dfl, dgq, dgk, dbf = _fox_prep_bwd(sv["proj"], sv["gq"], sv["gk"], sv["bf"], dqn, dkn, dlogf,
                                                           "fox_prep_bwd")
        duc, dvc, dsw, dsb, dsg = _sgu_bwd(sv["proj"], sgu_w[l], sv["gs"], sv["bexp"], dmix, "sgu_bwd")
        dproj = jnp.concatenate(
            [dq_fox, dk_fox, dv_fox.astype(_BF), dq_sb.astype(_BF), dk_sb.astype(_BF), dv_sb.astype(_BF), duc, dvc, dfl,
             jnp.zeros((dfl.shape[0], IN_P - C_FL - 128), _BF)], axis=-1)
        dh1 = _mm_nt(dproj, w_in_l, "plain", (), F32, "mm_in_bwd")
        g_in = _w_in_unlayout(_mm_tn(sv["h1"], dproj, None, "mm_in_grad"))
        g_in = jnp.transpose(g_in.reshape(D_MODEL, N_DEV, IN_W // N_DEV), (1, 0, 2))
        grads_ready = [g_in, g_out, g_w1, g_w2]
        dx, dsh1, dsc1, dn1 = _norm_mod_bwd(sv["x_in"], dh1, dx_mid, sv["n1"], sc1, "norm_mod_bwd")

        small["mod"][l] = jnp.concatenate([dsh1, dsc1, dg1, dsh2, dsc2, dg2], axis=-1).reshape(-1)
        small["norm1_g"][l] = dn1.reshape(-1)
        small["norm2_g"][l] = dn2.reshape(-1)
        small["sgu_w"][l] = dsw
        small["sgu_b"][l] = dsb.reshape(SGU_CHUNK, SGU_GROUPS, HEAD_DIM).sum(-1).T
        small["sgu_norm_g"][l] = dsg.reshape(SGU_GROUPS, HEAD_DIM)
        small["q_norm_g"][l] = dgq.reshape(FOX_HEADS, HEAD_DIM).sum(0)
        small["k_norm_g"][l] = dgk.reshape(FOX_HEADS, HEAD_DIM).sum(0)
        small["b_forget"][l] = dbf[0, :FOX_HEADS]

    parts = {k: jnp.stack(v) for k, v in small.items()}
    parts["ada_b"] = parts.pop("mod")
    parts["loss"] = loss_part
    (small_all,) = _exchange([_pack_small(parts)], False, "gather_small")
    zero1 = jnp.zeros((1,), F32)
    packs = [_pack_small(dict(ada_b=a, norm1_g=b, norm2_g=cc, sgu_w=d, sgu_b=e, sgu_norm_g=f, q_norm_g=g, k_norm_g=h,
                              b_forget=i, loss=zero1))
             for a, b, cc, d, e, f, g, h, i in (
                 (ada_b, norm1_g, norm2_g, sgu_w, sgu_b, sgu_norm_g, q_norm_g, k_norm_g, b_forget),
                 (m_ada_b, m_norm1_g, m_norm2_g, m_sgu_w, m_sgu_b, m_sgu_norm_g, m_q_norm_g, m_k_norm_g, m_b_forget),
                 (v_ada_b, v_norm1_g, v_norm2_g, v_sgu_w, v_sgu_b, v_sgu_norm_g, v_q_norm_g, v_k_norm_g, v_b_forget))]
    small_shapes = dict(ada_b=ada_b.shape, norm1_g=norm1_g.shape, norm2_g=norm2_g.shape, sgu_w=sgu_w.shape,
                        sgu_b=sgu_b.shape, sgu_norm_g=sgu_norm_g.shape, q_norm_g=q_norm_g.shape,
                        k_norm_g=k_norm_g.shape, b_forget=b_forget.shape, loss=())
    s_out = [_unpack_small(a, small_shapes) for a in _adamw(small_all, *packs, "adamw_small")]

    dmod_all = small_all[:, :DEPTH * 6 * D_MODEL // 128, :].reshape(N_DEV, DEPTH, 6 * D_MODEL)
    dmod_mine = lax.dynamic_slice_in_dim(dmod_all, me * n_ada, n_ada, axis=2)
    g_ada = _ada_bwd(c_all, jnp.transpose(dmod_mine, (1, 0, 2)), "ada_bwd")
    ada_out = _adamw(g_ada.reshape(1, DEPTH * D_MODEL, n_ada), ada_w.reshape(-1, n_ada), m_ada_w.reshape(-1, n_ada),
                     v_ada_w.reshape(-1, n_ada), "adamw_ada")
    ada_out = [a.reshape(ada_w.shape) for a in ada_out]

    ri, ro, r1, r2 = _exchange(grads_ready, True, "scatter_grads", recv=received, slot=0)

    def big(slots, w, m, v, name):
        cdim = w.shape[-1]
        outs = _adamw(slots.reshape(N_DEV, -1, cdim), w.reshape(-1, cdim), m.reshape(-1, cdim), v.reshape(-1, cdim), name)
        return [a.reshape(w.shape) for a in outs]

    in_out = big(ri, w_in, m_w_in, v_w_in, "adamw_in")
    out_out = big(ro, w_out, m_w_out, v_w_out, "adamw_out")
    w1_out = big(r1, mlp_w1, m_mlp_w1, v_mlp_w1, "adamw_w1")
    w2_out = big(r2, mlp_w2, m_mlp_w2, v_mlp_w2, "adamw_w2")

    def leaves(idx):
        sm = s_out[idx]
        return [ada_out[idx], sm["ada_b"], sm["norm1_g"], sm["norm2_g"], in_out[idx], sm["b_forget"], sm["q_norm_g"],
                sm["k_norm_g"], sm["sgu_norm_g"], sm["sgu_w"], sm["sgu_b"], out_out[idx], w1_out[idx], w2_out[idx]]

    loss = s_out[0]["loss"]
    grad_x = dx.reshape(x.shape)
    return (loss, grad_x, *leaves(0), *leaves(1), *leaves(2), *leaves(3))
```

```python
import jax
import jax.numpy as jnp
from jax import lax
from jax.experimental import pallas as pl
from jax.experimental.pallas import tpu as pltpu

F32 = jnp.float32
_BF = jnp.bfloat16
_XBF = jnp.bfloat16

N_DEV = 8
D_MODEL = 1024
DEPTH = 4
HEAD_DIM = 64
SB_HEADS = 4
FOX_HEADS = 8
SGU_GROUPS = 4
SGU_CHUNK = 128
D_FF = 4096
EPS = 1e-6
IN_W = 2824
FL_SRC = 2304
UC_SRC = 2312

C_QB, C_KB, C_VB = 0, 512, 1024
C_QA, C_KA, C_VA = 1536, 1792, 2048
C_UC, C_VC = 2304, 2560
C_FL = 2816
IN_P = 3072
M_SB, M_FOX, M_SGU = 0, 256, 768

ADAM_LR = 0.001
ADAM_B1 = 0.9
ADAM_B2 = 0.999
ADAM_EPS = 1e-08
ADAM_WD = 0.01
ADAM_STEP = 10

ATT_T = 256
ROW_T = 512
MM_T = 1024
NEG = -1e30
UNDERFLOW = -100.0

MESH = pl.DeviceIdType.MESH
ARB = pltpu.ARBITRARY
PAR = pltpu.PARALLEL
HIGHEST = lax.Precision.HIGHEST


def _pcall(body, **kw):
    return pl.pallas_call(body, **kw)


def _cp(*sem):
    return pltpu.CompilerParams(dimension_semantics=tuple(sem))


def _dot(a, b, ca, cb):
    return lax.dot_general(a, b, (((ca,), (cb,)), ((), ())), preferred_element_type=F32)


def _dot_f32(a, b, ca, cb):
    return lax.dot_general(a, b, (((ca,), (cb,)), ((), ())), precision=HIGHEST, preferred_element_type=F32)


def _split(v, parts):
    out = []
    r = v
    for _ in range(parts - 1):
        p = r.astype(_BF)
        out.append(p)
        r = r - p.astype(F32)
    out.append(r.astype(_BF))
    return out


def _dot_exact(v, m, ca, cb, parts, v_left=True):
    out = None
    for p in _split(v, parts):
        term = _dot(p, m, ca, cb) if v_left else _dot(m, p, ca, cb)
        out = term if out is None else out + term
    return out


def _iota2(shape, dim):
    return lax.broadcasted_iota(jnp.int32, shape, dim)


def _my_pos():
    return lax.axis_index("x"), lax.axis_index("y"), lax.axis_index("c")


def _flip(pos, k):
    x, y, c = pos
    px = 1 - x if (k >> 2) & 1 else x
    py = 1 - y if (k >> 1) & 1 else y
    pc = 1 - c if k & 1 else c
    return px, py, pc


def _lin(pos):
    return 4 * pos[0] + 2 * pos[1] + pos[2]


def _exchange_ops(ins, outs, sems, scatter, slot):
    send_sems, recv_sems, local_sems = sems
    n = len(ins)

    def copies(with_recvs):
        me = _my_pos()
        me_i = _lin(me)
        dst = lambda a, j: outs[a].at[j] if slot is None else outs[a].at[j, slot]
        mine = lambda a: ins[a].at[me_i] if scatter else ins[a]
        local = [pltpu.make_async_copy(mine(a), dst(a, me_i), local_sems.at[a]) for a in range(n)]
        sends, recvs = [], []
        for a in range(n):
            for k in range(1, N_DEV):
                peer = _flip(me, k)
                pair = dict(send_sem=send_sems.at[a, k - 1], recv_sem=recv_sems.at[a, k - 1],
                            device_id=peer, device_id_type=MESH)
                src = ins[a].at[_lin(peer)] if scatter else ins[a]
                sends.append(pltpu.make_async_remote_copy(src_ref=src, dst_ref=dst(a, me_i), **pair))
                if with_recvs:
                    recvs.append(pltpu.make_async_remote_copy(src_ref=mine(a), dst_ref=dst(a, _lin(peer)), **pair))
        return local, sends, recvs

    def start():
        local, sends, _ = copies(False)
        for cp in local + sends:
            cp.start()

    def wait():
        local, sends, recvs = copies(True)
        for snd, rcv in zip(sends, recvs):
            snd.wait_send()
            rcv.wait_recv()
        for cp in local:
            cp.wait()

    return start, wait


def _comm_operands(comm):
    arrays, recv = comm["arrays"], comm["recv"]
    n = len(arrays)
    any_spec = pl.BlockSpec(memory_space=pl.ANY)
    if recv is not None:
        out_shapes = [jax.ShapeDtypeStruct(r.shape, r.dtype) for r in recv]
    elif comm["scatter"]:
        out_shapes = [jax.ShapeDtypeStruct(a.shape, a.dtype) for a in arrays]
    else:
        out_shapes = [jax.ShapeDtypeStruct((N_DEV,) + a.shape, a.dtype) for a in arrays]
    operands = list(arrays) + (list(recv) if recv is not None else [])
    sems = [pltpu.SemaphoreType.DMA((n, N_DEV - 1)), pltpu.SemaphoreType.DMA((n, N_DEV - 1)),
            pltpu.SemaphoreType.DMA((n,))]
    return operands, [any_spec] * len(operands), out_shapes, [any_spec] * n, sems


def _with_comm(body, comm, n_in, n_out, first_step, last_step):
    if comm is None:
        return body
    n = len(comm["arrays"])
    n_cin = n if comm["recv"] is None else 2 * n

    def wrapped(*refs):
        ins, cins = refs[:n_in], refs[n_in:n_in + n]
        o0 = n_in + n_cin
        outs, couts, sems = refs[o0:o0 + n_out], refs[o0 + n_out:o0 + n_out + n], refs[o0 + n_out + n:]
        start, wait = _exchange_ops(cins, couts, sems, comm["scatter"], comm["slot"])
        pl.when(first_step())(start)
        body(*ins, *outs)
        pl.when(last_step())(wait)

    return wrapped


def _comm_aliases(comm, n_in, n_out):
    if comm is None or comm["recv"] is None:
        return {}
    n = len(comm["arrays"])
    return {n_in + n + a: n_out + a for a in range(n)}


def _exchange(arrays, scatter, name, recv=None, slot=None):
    comm = dict(arrays=arrays, scatter=scatter, recv=recv, slot=slot)
    operands, in_specs, out_shapes, out_specs, sems = _comm_operands(comm)
    n = len(arrays)

    def body(*refs):
        n_cin = len(operands)
        start, wait = _exchange_ops(refs[:n], refs[n_cin:n_cin + n], refs[n_cin + n:], scatter, slot)
        start()
        wait()

    outs = _pcall(body, name=name, out_shape=out_shapes, in_specs=in_specs, out_specs=out_specs, scratch_shapes=sems,
                  input_output_aliases=_comm_aliases(comm, 0, 0))(*operands)
    return list(outs)


def _row_spec(t, w, col=0):
    return pl.BlockSpec((t, w), lambda i: (i, col))


def _vec_spec(w, col=0):
    return pl.BlockSpec((1, w), lambda i: (0, col))


def _norm_mod(x, g, sc, sh, name):
    s, d = x.shape
    t = min(ROW_T, s)

    def body(x_ref, g_ref, sc_ref, sh_ref, h_ref):
        xf = x_ref[...]
        rstd = lax.rsqrt(jnp.mean(xf * xf, axis=-1, keepdims=True) + EPS)
        y = xf * rstd * g_ref[...]
        h_ref[...] = (y * (1.0 + sc_ref[...]) + sh_ref[...]).astype(h_ref.dtype)

    return _pcall(
        body, name=name, grid=(s // t,), out_shape=jax.ShapeDtypeStruct((s, d), _BF),
        in_specs=[_row_spec(t, d), _vec_spec(d), _vec_spec(d), _vec_spec(d)],
        out_specs=_row_spec(t, d), compiler_params=_cp(PAR),
    )(x, g, sc, sh)


def _norm_mod_bwd(x, dh, dres, g, sc, name):
    s, d = x.shape
    t = min(ROW_T, s)

    def body(x_ref, dh_ref, dres_ref, g_ref, sc_ref, dx_ref, dsh_ref, dsc_ref, dg_ref):
        i = pl.program_id(0)
        xf = x_ref[...]
        dh = dh_ref[...]
        gv = g_ref[...]
        rstd = lax.rsqrt(jnp.mean(xf * xf, axis=-1, keepdims=True) + EPS)
        xhat = xf * rstd
        dn = dh * (1.0 + sc_ref[...])
        dxh = dn * gv
        dx_ref[...] = dres_ref[...] + rstd * (dxh - xhat * jnp.mean(dxh * xhat, axis=-1, keepdims=True))

        @pl.when(i == 0)
        def _():
            dsh_ref[...] = jnp.zeros_like(dsh_ref)
            dsc_ref[...] = jnp.zeros_like(dsc_ref)
            dg_ref[...] = jnp.zeros_like(dg_ref)

        dsh_ref[...] += jnp.sum(dh, axis=0, keepdims=True)
        dsc_ref[...] += jnp.sum(dh * (xhat * gv), axis=0, keepdims=True)
        dg_ref[...] += jnp.sum(dn * xhat, axis=0, keepdims=True)

    vec = jax.ShapeDtypeStruct((1, d), F32)
    return _pcall(
        body, name=name, grid=(s // t,),
        out_shape=[jax.ShapeDtypeStruct((s, d), F32), vec, vec, vec],
        in_specs=[_row_spec(t, d), _row_spec(t, d), _row_spec(t, d), _vec_spec(d), _vec_spec(d)],
        out_specs=[_row_spec(t, d), _vec_spec(d), _vec_spec(d), _vec_spec(d)],
        compiler_params=_cp(ARB),
    )(x, dh, dres, g, sc)


def _res_bwd(dx, y, gate, name):
    s, d = dx.shape
    t = min(ROW_T, s)

    def body(dx_ref, y_ref, g_ref, dy_ref, dg_ref):
        i = pl.program_id(0)
        dxv = dx_ref[...]
        dy_ref[...] = (dxv * g_ref[...]).astype(dy_ref.dtype)

        @pl.when(i == 0)
        def _():
            dg_ref[...] = jnp.zeros_like(dg_ref)

        dg_ref[...] += jnp.sum(dxv * y_ref[...], axis=0, keepdims=True)

    return _pcall(
        body, name=name, grid=(s // t,),
        out_shape=[jax.ShapeDtypeStruct((s, d), _BF), jax.ShapeDtypeStruct((1, d), F32)],
        in_specs=[_row_spec(t, d), _row_spec(t, d), _vec_spec(d)],
        out_specs=[_row_spec(t, d), _vec_spec(d)], compiler_params=_cp(ARB),
    )(dx, y, gate)


def _loss_head(y, target, name):
    s, d = y.shape
    t = min(ROW_T, s)

    def body(y_ref, t_ref, loss_ref, dy_ref):
        i = pl.program_id(0)
        diff = y_ref[...] - t_ref[...]
        dy_ref[...] = diff * (1.0 / d)

        @pl.when(i == 0)
        def _():
            loss_ref[...] = jnp.zeros_like(loss_ref)

        rows = jnp.sum(diff * diff, axis=-1, keepdims=True)
        loss_ref[...] += (0.5 / d) * jnp.sum(rows, axis=0, keepdims=True)

    return _pcall(
        body, name=name, grid=(s // t,),
        out_shape=[jax.ShapeDtypeStruct((1, 1), F32), jax.ShapeDtypeStruct((s, d), F32)],
        in_specs=[_row_spec(t, d), _row_spec(t, d)],
        out_specs=[pl.BlockSpec((1, 1), lambda i: (0, 0)), _row_spec(t, d)], compiler_params=_cp(ARB),
    )(y, target)


def _mm_nn(a, b, epi, extras, name):
    m, kdim = a.shape
    n = b.shape[1]
    tm, tn, tk = min(MM_T, m), min(MM_T, n), min(MM_T, kdim)
    nk = kdim // tk
    n_extra = len(extras)
    n_out = 1 if epi == "plain" else 2

    def finish(y, extra_refs, out_refs):
        if epi == "plain":
            out_refs[0][...] = y.astype(out_refs[0].dtype)
        elif epi == "resid":
            x_ref, g_ref = extra_refs
            out_refs[0][...] = x_ref[...] + g_ref[...] * y
            out_refs[1][...] = y
        else:
            r = jnp.maximum(y, 0.0)
            out_refs[0][...] = (r * r).astype(out_refs[0].dtype)
            out_refs[1][...] = r.astype(out_refs[1].dtype)

    def body(a_ref, b_ref, *rest):
        extra_refs = rest[:n_extra]
        out_refs = rest[n_extra:n_extra + n_out]
        part = _dot(a_ref[...].astype(_BF), b_ref[...].astype(_BF), 1, 0)
        if nk == 1:
            finish(part, extra_refs, out_refs)
        else:
            acc_ref = rest[-1]
            k = pl.program_id(2)

            @pl.when(k == 0)
            def _():
                acc_ref[...] = part

            @pl.when(k > 0)
            def _():
                acc_ref[...] += part

            @pl.when(k == nk - 1)
            def _():
                finish(acc_ref[...], extra_refs, out_refs)

    tile = pl.BlockSpec((tm, tn), lambda i, j, k: (i, j))
    in_specs = [pl.BlockSpec((tm, tk), lambda i, j, k: (i, k)), pl.BlockSpec((tk, tn), lambda i, j, k: (k, j))]
    if epi == "plain":
        out_shape = [jax.ShapeDtypeStruct((m, n), F32)]
    elif epi == "resid":
        in_specs += [tile, pl.BlockSpec((1, tn), lambda i, j, k: (0, j))]
        out_shape = [jax.ShapeDtypeStruct((m, n), F32)] * 2
    else:
        out_shape = [jax.ShapeDtypeStruct((m, n), _BF), jax.ShapeDtypeStruct((m, n), F32)]
    outs = _pcall(
        body, name=name, grid=(m // tm, n // tn, nk), out_shape=out_shape,
        in_specs=in_specs, out_specs=[tile] * n_out,
        scratch_shapes=[pltpu.VMEM((tm, tn), F32)] if nk > 1 else [],
        compiler_params=_cp(PAR, PAR, ARB),
    )(a, b, *extras)
    return outs[0] if n_out == 1 else outs


def _mm_nt(a, b, epi, extras, out_dtype, name):
    m, kdim = a.shape
    n = b.shape[0]
    tm, tn, tk = min(MM_T, m), min(MM_T, n), min(MM_T, kdim)
    nk = kdim // tk
    n_extra = len(extras)

    def finish(y, extra_refs, o_ref):
        if epi == "mul2":
            y = y * (2.0 * extra_refs[0][...].astype(F32))
        o_ref[...] = y.astype(o_ref.dtype)

    def body(a_ref, b_ref, *rest):
        extra_refs = rest[:n_extra]
        o_ref = rest[n_extra]
        part = _dot(a_ref[...].astype(_BF), b_ref[...].astype(_BF), 1, 1)
        if nk == 1:
            finish(part, extra_refs, o_ref)
        else:
            acc_ref = rest[-1]
            k = pl.program_id(2)

            @pl.when(k == 0)
            def _():
                acc_ref[...] = part

            @pl.when(k > 0)
            def _():
                acc_ref[...] += part

            @pl.when(k == nk - 1)
            def _():
                finish(acc_ref[...], extra_refs, o_ref)

    tile = pl.BlockSpec((tm, tn), lambda i, j, k: (i, j))
    in_specs = [pl.BlockSpec((tm, tk), lambda i, j, k: (i, k)), pl.BlockSpec((tn, tk), lambda i, j, k: (j, k))]
    in_specs += [tile] * n_extra
    return _pcall(
        body, name=name, grid=(m // tm, n // tn, nk), out_shape=jax.ShapeDtypeStruct((m, n), out_dtype),
        in_specs=in_specs, out_specs=tile,
        scratch_shapes=[pltpu.VMEM((tm, tn), F32)] if nk > 1 else [],
        compiler_params=_cp(PAR, PAR, ARB),
    )(a, b, *extras)


def _mm_tn(a, b, split, name):
    s, m = a.shape
    n = b.shape[1]
    ts = min(MM_T, s)
    ns = s // ts
    if split == "rows":
        tm, tn = min(MM_T, m // N_DEV), min(MM_T, n)
        per = (m // N_DEV) // tm
        out_shape = (N_DEV, m // N_DEV, n)
        out_spec = pl.BlockSpec((None, tm, tn), lambda i, j, k: (i // per, i % per, j))
    elif split == "cols":
        tm, tn = min(MM_T, m), min(MM_T, n // N_DEV)
        per = (n // N_DEV) // tn
        out_shape = (N_DEV, m, n // N_DEV)
        out_spec = pl.BlockSpec((None, tm, tn), lambda i, j, k: (j // per, i, j % per))
    else:
        tm, tn = min(MM_T, m), min(MM_T, n)
        out_shape = (m, n)
        out_spec = pl.BlockSpec((tm, tn), lambda i, j, k: (i, j))

    def body(a_ref, b_ref, o_ref, acc_ref):
        k = pl.program_id(2)
        part = _dot(a_ref[...].astype(_BF), b_ref[...].astype(_BF), 0, 0)

        @pl.when(k == 0)
        def _():
            acc_ref[...] = part

        @pl.when(k > 0)
        def _():
            acc_ref[...] += part

        @pl.when(k == ns - 1)
        def _():
            o_ref[...] = acc_ref[...].astype(o_ref.dtype)

    return _pcall(
        body, name=name, grid=(m // tm, n // tn, ns), out_shape=jax.ShapeDtypeStruct(out_shape, _XBF),
        in_specs=[pl.BlockSpec((ts, tm), lambda i, j, k: (k, i)), pl.BlockSpec((ts, tn), lambda i, j, k: (k, j))],
        out_specs=out_spec, scratch_shapes=[pltpu.VMEM((tm, tn), F32)],
        compiler_params=_cp(PAR, PAR, ARB),
    )(a, b)


def _head_masks():
    lane = _iota2((1, 128), 1)
    return [(lane // HEAD_DIM) == hh for hh in range(2)]


def _softplus_parts(z):
    sp = jnp.maximum(z, 0.0) + jnp.log(1.0 + jnp.exp(-jnp.abs(z)))
    return -sp, z - sp


def _sb_key_loop(i, carry, step):
    def cond(c):
        return jnp.logical_and(c[0] <= i, jnp.max(jnp.maximum(c[1][0][0], c[1][0][1])) > UNDERFLOW)

    def body(c):
        return c[0] + 1, step(i - c[0], c[1])

    return lax.while_loop(cond, body, (jnp.int32(1), carry))[1]


def _sb_fwd(proj, name):
    s = proj.shape[0]
    t = min(ATT_T, s)
    nblk = s // t
    scale = HEAD_DIM ** -0.5

    def body(q_ref, k_ref, v_ref, o_ref):
        i = pl.program_id(1)
        masks = _head_masks()
        row = _iota2((t, t), 0)
        col = _iota2((t, t), 1)
        u_strict = (row > col).astype(_BF)
        causal = col < row
        qs = q_ref[...] * scale
        qm = [jnp.where(masks[hh], qs, 0.0).astype(_BF) for hh in range(2)]

        def block(j, carry, diag):
            off = pl.multiple_of(j * t, t)
            kb = k_ref[pl.ds(off, t), :].astype(_BF)
            vb = v_ref[pl.ds(off, t), :].astype(_BF)
            runs, outs = [], []
            for hh in range(2):
                run, o = carry[0][hh], carry[1][hh]
                z = _dot(qm[hh], kb, 1, 1)
                l1, lb = _softplus_parts(z)
                if diag:
                    l1 = jnp.where(causal, l1, 0.0)
                between = run + _dot_exact(l1, u_strict, 1, 0, 2)
                a = jnp.exp(lb + between)
                if diag:
                    a = jnp.where(causal, a, 0.0)
                outs.append(o + _dot_exact(a, vb, 1, 0, 2))
                runs.append(run + jnp.sum(l1, axis=-1, keepdims=True))
            return tuple(runs), tuple(outs)

        zero, zero_o = jnp.zeros((t, 1), F32), jnp.zeros((t, 128), F32)
        carry = block(i, ((zero, zero), (zero_o, zero_o)), True)
        carry = _sb_key_loop(i, carry, lambda j, c: block(j, c, False))
        o_ref[...] = jnp.where(masks[0], carry[1][0], carry[1][1])

    kv_spec = lambda c0: pl.BlockSpec((s, 128), lambda p, i: (0, c0 // 128 + p))
    return _pcall(
        body, name=name, grid=(SB_HEADS // 2, nblk), out_shape=jax.ShapeDtypeStruct((s, 256), F32),
        in_specs=[pl.BlockSpec((t, 128), lambda p, i: (i, C_QA // 128 + p)), kv_spec(C_KA), kv_spec(C_VA)],
        out_specs=pl.BlockSpec((t, 128), lambda p, i: (i, p)), compiler_params=_cp(PAR, ARB),
    )(proj, proj, proj)


def _sb_bwd(proj, o, dmix, name):
    s = proj.shape[0]
    t = min(ATT_T, s)
    nblk = s // t
    scale = HEAD_DIM ** -0.5

    def body(q_ref, k_ref, v_ref, o_ref, do_ref, dq_ref, dk_ref, dv_ref):
        i = pl.program_id(1)
        masks = _head_masks()
        row = _iota2((t, t), 0)
        col = _iota2((t, t), 1)
        u_strict = (row > col).astype(_BF)
        u_incl = (row >= col).astype(_BF)
        causal = col < row

        @pl.when(i == 0)
        def _():
            dk_ref[...] = jnp.zeros_like(dk_ref)
            dv_ref[...] = jnp.zeros_like(dv_ref)

        qs = q_ref[...] * scale
        dov = do_ref[...]
        ov = o_ref[...]
        qm = [jnp.where(masks[hh], qs, 0.0).astype(_BF) for hh in range(2)]
        dom = [jnp.where(masks[hh], dov, 0.0).astype(_BF) for hh in range(2)]
        dsum = [jnp.sum(dom[hh].astype(F32) * ov, axis=-1, keepdims=True) for hh in range(2)]

        def block(j, carry, diag):
            off = pl.multiple_of(j * t, t)
            kb = k_ref[pl.ds(off, t), :].astype(_BF)
            vb = v_ref[pl.ds(off, t), :].astype(_BF)
            runs, eruns, dqs = [], [], []
            dk = dv = None
            for hh in range(2):
                run, erun, dq = carry[0][hh], carry[1][hh], carry[2][hh]
                z = _dot(qm[hh], kb, 1, 1)
                l1, lb = _softplus_parts(z)
                if diag:
                    l1 = jnp.where(causal, l1, 0.0)
                between = run + _dot_exact(l1, u_strict, 1, 0, 2)
                a = jnp.exp(lb + between)
                if diag:
                    a = jnp.where(causal, a, 0.0)
                g = a * _dot(dom[hh], vb, 1, 1)
                cum = dsum[hh] - (erun + _dot_exact(g, u_incl, 1, 0, 2))
                beta = jnp.exp(lb)
                dz = g * (1.0 - beta) - cum * beta
                if diag:
                    dz = jnp.where(causal, dz, 0.0)
                dzb = dz.astype(_BF)
                dqs.append(dq + _dot(dzb, kb, 1, 0))
                dk_h = _dot(dzb, qm[hh], 0, 0)
                dv_h = _dot(a.astype(_BF), dom[hh], 0, 0)
                dk = dk_h if dk is None else dk + dk_h
                dv = dv_h if dv is None else dv + dv_h
                runs.append(run + jnp.sum(l1, axis=-1, keepdims=True))
                eruns.append(erun + jnp.sum(g, axis=-1, keepdims=True))
            dk_ref[pl.ds(off, t), :] += dk
            dv_ref[pl.ds(off, t), :] += dv
            return tuple(runs), tuple(eruns), tuple(dqs)

        zero, zero_q = jnp.zeros((t, 1), F32), jnp.zeros((t, 128), F32)
        carry = block(i, ((zero, zero), (zero, zero), (zero_q, zero_q)), True)
        carry = _sb_key_loop(i, carry, lambda j, c: block(j, c, False))
        dq_ref[...] = jnp.where(masks[0], carry[2][0], carry[2][1]) * scale

    kv_spec = lambda c0: pl.BlockSpec((s, 128), lambda p, i: (0, c0 // 128 + p))
    blk = lambda c0: pl.BlockSpec((t, 128), lambda p, i: (i, c0 // 128 + p))
    acc = pl.BlockSpec((s, 128), lambda p, i: (0, p))
    shp = jax.ShapeDtypeStruct((s, 256), F32)
    return _pcall(
        body, name=name, grid=(SB_HEADS // 2, nblk), out_shape=[shp, shp, shp],
        in_specs=[blk(C_QA), kv_spec(C_KA), kv_spec(C_VA), blk(0), blk(M_SB)],
        out_specs=[blk(0), acc, acc], compiler_params=_cp(PAR, ARB),
    )(proj, proj, proj, o, dmix)


def _group_matrix(width):
    r = _iota2((width, width), 0) // HEAD_DIM
    c = _iota2((width, width), 1) // HEAD_DIM
    return (r == c).astype(_BF)


def _group_mean(v, gm):
    return _dot_exact(v, gm, 1, 0, 2) * (1.0 / HEAD_DIM)


def _fox_prep(proj, gq, gk, bf, name):
    s = proj.shape[0]
    t = min(ATT_T, s)
    nblk = s // t

    def body(q_ref, k_ref, fl_ref, gq_ref, gk_ref, bf_ref, qn_ref, kn_ref, cfc_ref, cfr_ref, carry_ref):
        i = pl.program_id(0)
        gm = _group_matrix(512)
        for src, gref, dst in ((q_ref, gq_ref, qn_ref), (k_ref, gk_ref, kn_ref)):
            v = src[...]
            rstd = lax.rsqrt(_group_mean(v * v, gm) + EPS)
            dst[...] = (v * rstd * gref[...]).astype(dst.dtype)

        @pl.when(i == 0)
        def _():
            carry_ref[...] = jnp.zeros_like(carry_ref)

        lane = _iota2((1, 128), 1)
        logf = jnp.where(lane < FOX_HEADS, _softplus_parts(-(fl_ref[...] + bf_ref[...]))[0], 0.0)
        lower = (_iota2((t, t), 0) >= _iota2((t, t), 1)).astype(_BF)
        cf = carry_ref[...] + _dot_exact(logf, lower, 1, 0, 3, v_left=False)
        cfc_ref[...] = cf
        cfr_ref[...] = cf.T[:FOX_HEADS, :]
        carry_ref[...] = cf[t - 1:t, :]

    col = lambda w, c0: pl.BlockSpec((t, w), lambda i: (i, c0 // w))
    v512 = pl.BlockSpec((1, 512), lambda i: (0, 0))
    return _pcall(
        body, name=name, grid=(nblk,),
        out_shape=[jax.ShapeDtypeStruct((s, 512), _BF), jax.ShapeDtypeStruct((s, 512), _BF),
                   jax.ShapeDtypeStruct((s, 128), F32), jax.ShapeDtypeStruct((FOX_HEADS, s), F32)],
        in_specs=[col(512, C_QB), col(512, C_KB), col(128, C_FL), v512, v512, pl.BlockSpec((1, 128), lambda i: (0, 0))],
        out_specs=[col(512, 0), col(512, 0), col(128, 0), pl.BlockSpec((FOX_HEADS, t), lambda i: (0, i))],
        scratch_shapes=[pltpu.VMEM((1, 128), F32)], compiler_params=_cp(ARB),
    )(proj, proj, proj, gq, gk, bf)


def _fox_cq(cfc_blk, head):
    lane = _iota2((1, 128), 1)
    return jnp.sum(jnp.where(lane == head, cfc_blk, 0.0), axis=-1, keepdims=True)


def _fox_ck(cfr_ref, off, t, head):
    sub = _iota2((FOX_HEADS, 1), 0)
    return jnp.sum(jnp.where(sub == head, cfr_ref[:, pl.ds(off, t)], 0.0), axis=0, keepdims=True)


def _fox_last_key(cfr_ref, j, t, head):
    return jnp.min(_fox_ck(cfr_ref, pl.multiple_of(j * t, t), t, head))


def _grid_ends(n_outer, n_inner):
    first = lambda: jnp.logical_and(pl.program_id(0) == 0, pl.program_id(1) == 0)
    last = lambda: jnp.logical_and(pl.program_id(0) == n_outer - 1, pl.program_id(1) == n_inner - 1)
    return first, last


def _fox_fwd(proj, qn, kn, cfc, cfr, qk_bound, name, comm=None):
    s = proj.shape[0]
    t = min(ATT_T, s)
    nblk = s // t
    scale = HEAD_DIM ** -0.5

    def body(q_ref, k_ref, v_ref, cfc_ref, cfr_ref, qkb_ref, o_ref, lse_ref):
        p = pl.program_id(0)
        i = pl.program_id(1)
        masks = _head_masks()
        lane = _iota2((1, 128), 1)
        valid = _iota2((t, t), 1) <= _iota2((t, t), 0)
        qv = q_ref[...]
        cfc_blk = cfc_ref[...]
        heads = [2 * p + hh for hh in range(2)]
        qm = [(jnp.where(masks[hh], qv, 0) * scale).astype(_BF) for hh in range(2)]
        cq = [_fox_cq(cfc_blk, heads[hh]) for hh in range(2)]

        def block(j, carry, diag):
            off = pl.multiple_of(j * t, t)
            kb = k_ref[pl.ds(off, t), :]
            vb = v_ref[pl.ds(off, t), :].astype(_BF)
            ms, ls, accs = [], [], []
            for hh in range(2):
                m, l, acc = carry[0][hh], carry[1][hh], carry[2][hh]
                z = _dot(qm[hh], kb, 1, 1) + (cq[hh] - _fox_ck(cfr_ref, off, t, heads[hh]))
                if diag:
                    z = jnp.where(valid, z, NEG)
                m_new = jnp.maximum(m, jnp.max(z, axis=-1, keepdims=True))
                pe = jnp.exp(z - m_new)
                alpha = jnp.exp(m - m_new)
                ms.append(m_new)
                ls.append(alpha * l + jnp.sum(pe, axis=-1, keepdims=True))
                accs.append(alpha * acc + _dot(pe.astype(_BF), vb, 1, 0))
            return tuple(ms), tuple(ls), tuple(accs)

        def cond(c):
            j = jnp.maximum(i - c[0], 0)
            reach = [jnp.max(qkb_ref[...] + cq[hh] - c[1][0][hh]) - _fox_last_key(cfr_ref, j, t, heads[hh])
                     for hh in range(2)]
            return jnp.logical_and(c[0] <= i, jnp.maximum(reach[0], reach[1]) > UNDERFLOW)

        neg, zero, zero_o = jnp.full((t, 1), NEG, F32), jnp.zeros((t, 1), F32), jnp.zeros((t, 128), F32)
        carry = block(i, ((neg, neg), (zero, zero), (zero_o, zero_o)), True)
        m, l, acc = lax.while_loop(cond, lambda c: (c[0] + 1, block(i - c[0], c[1], False)), (jnp.int32(1), carry))[1]
        o_ref[...] = jnp.where(masks[0], acc[0] / l[0], acc[1] / l[1])
        lse_ref[...] = jnp.where(lane == 0, m[0] + jnp.log(l[0]), jnp.where(lane == 1, m[1] + jnp.log(l[1]), 0.0))

    c_operands, c_in_specs, c_out_shapes, c_out_specs, c_sems = _comm_operands(comm) if comm else ([], [], [], [], [])
    outs = _pcall(
        _with_comm(body, comm, 6, 2, *_grid_ends(FOX_HEADS // 2, nblk)), name=name, grid=(FOX_HEADS // 2, nblk),
        out_shape=[jax.ShapeDtypeStruct((s, 512), F32), jax.ShapeDtypeStruct((FOX_HEADS // 2, s, 128), F32)] + c_out_shapes,
        in_specs=[pl.BlockSpec((t, 128), lambda p, i: (i, p)), pl.BlockSpec((s, 128), lambda p, i: (0, p)),
                  pl.BlockSpec((s, 128), lambda p, i: (0, C_VB // 128 + p)),
                  pl.BlockSpec((t, 128), lambda p, i: (i, 0)), pl.BlockSpec((FOX_HEADS, s), lambda p, i: (0, 0)),
                  pl.BlockSpec((1, 1), lambda p, i: (0, 0))] + c_in_specs,
        out_specs=[pl.BlockSpec((t, 128), lambda p, i: (i, p)),
                   pl.BlockSpec((None, t, 128), lambda p, i: (p, i, 0))] + c_out_specs,
        scratch_shapes=c_sems, input_output_aliases=_comm_aliases(comm, 6, 2),
        compiler_params=_cp(ARB, ARB),
    )(qn, kn, proj, cfc, cfr, qk_bound, *c_operands)
    return outs[0], outs[1], list(outs[2:])


def _fox_bwd(proj, qn, kn, cfc, cfr, qk_bound, o, lse, dmix, name, comm=None):
    s = proj.shape[0]
    t = min(ATT_T, s)
    nblk = s // t
    scale = HEAD_DIM ** -0.5

    def body(q_ref, k_ref, v_ref, cfc_ref, cfr_ref, qkb_ref, o_ref, lse_ref, do_ref, dq_ref, dk_ref, dv_ref, dcr_ref):
        p = pl.program_id(0)
        i = pl.program_id(1)
        masks = _head_masks()
        lane = _iota2((1, 128), 1)
        valid = _iota2((t, t), 1) <= _iota2((t, t), 0)
        before = (_iota2((t, t), 0) < _iota2((t, t), 1)).astype(_BF)

        @pl.when(i == 0)
        def _():
            dk_ref[...] = jnp.zeros_like(dk_ref)
            dv_ref[...] = jnp.zeros_like(dv_ref)
            dcr_ref[...] = jnp.zeros_like(dcr_ref)

        qv = q_ref[...]
        cfc_blk = cfc_ref[...]
        dov = do_ref[...]
        ov = o_ref[...]
        lse_blk = lse_ref[...]
        heads = [2 * p + hh for hh in range(2)]
        qm = [(jnp.where(masks[hh], qv, 0) * scale).astype(_BF) for hh in range(2)]
        dom = [jnp.where(masks[hh], dov, 0.0).astype(_BF) for hh in range(2)]
        dsum = [jnp.sum(dom[hh].astype(F32) * ov, axis=-1, keepdims=True) for hh in range(2)]
        lse_h = [jnp.sum(jnp.where(lane == hh, lse_blk, 0.0), axis=-1, keepdims=True) for hh in range(2)]
        cq = [_fox_cq(cfc_blk, heads[hh]) for hh in range(2)]
        reach = [jnp.max(qkb_ref[...] + cq[hh] - lse_h[hh]) for hh in range(2)]

        def matters(j):
            jm = jnp.maximum(j - 1, 0)
            far = [reach[hh] - _fox_last_key(cfr_ref, jm, t, heads[hh]) for hh in range(2)]
            return jnp.logical_and(j > 0, jnp.maximum(far[0], far[1]) > UNDERFLOW)

        first = lax.while_loop(matters, lambda j: j - 1, i)

        def block(j, carry, diag):
            off = pl.multiple_of(j * t, t)
            kb = k_ref[pl.ds(off, t), :]
            vb = v_ref[pl.ds(off, t), :].astype(_BF)
            dqs, pres = [], []
            dk = dv = None
            for hh in range(2):
                dq, pre = carry[0][hh], carry[1][hh]
                z = _dot(qm[hh], kb, 1, 1) + (cq[hh] - _fox_ck(cfr_ref, off, t, heads[hh]))
                pm = jnp.exp(z - lse_h[hh])
                if diag:
                    pm = jnp.where(valid, pm, 0.0)
                ds = pm * (_dot(dom[hh], vb, 1, 1) - dsum[hh])
                dsb = ds.astype(_BF)
                dqs.append(dq + _dot(dsb, kb, 1, 0))
                dk_h = _dot(dsb, qm[hh], 0, 0)
                dv_h = _dot(pm.astype(_BF), dom[hh], 0, 0)
                dk = dk_h if dk is None else dk + dk_h
                dv = dv_h if dv is None else dv + dv_h
                if diag:
                    lower_keys = jnp.where(valid, pre + _dot_exact(ds, before, 1, 0, 2), 0.0)
                    dlogf = jnp.sum(lower_keys, axis=0, keepdims=True)
                else:
                    dlogf = jnp.sum(pre) + _dot_exact(jnp.sum(ds, axis=0, keepdims=True), before, 1, 0, 3)
                dcr_ref[hh:hh + 1, pl.ds(off, t)] += dlogf
                pres.append(pre + jnp.sum(ds, axis=-1, keepdims=True))
            dk_ref[pl.ds(off, t), :] += dk
            dv_ref[pl.ds(off, t), :] += dv
            return tuple(dqs), tuple(pres)

        zero, zero_q = jnp.zeros((t, 1), F32), jnp.zeros((t, 128), F32)
        carry = lax.fori_loop(first, i, lambda j, c: block(j, c, False), ((zero_q, zero_q), (zero, zero)))
        dqs, _ = block(i, carry, True)
        dq_ref[...] = jnp.where(masks[0], dqs[0], dqs[1]) * scale

    blk = lambda c0: pl.BlockSpec((t, 128), lambda p, i: (i, c0 // 128 + p))
    res = lambda c0: pl.BlockSpec((s, 128), lambda p, i: (0, c0 // 128 + p))
    shp = jax.ShapeDtypeStruct((s, 512), F32)
    c_operands, c_in_specs, c_out_shapes, c_out_specs, c_sems = _comm_operands(comm) if comm else ([], [], [], [], [])
    outs = _pcall(
        _with_comm(body, comm, 9, 4, *_grid_ends(FOX_HEADS // 2, nblk)), name=name, grid=(FOX_HEADS // 2, nblk),
        out_shape=[shp, shp, shp, jax.ShapeDtypeStruct((FOX_HEADS // 2, 8, s), F32)] + c_out_shapes,
        in_specs=[blk(0), res(0), res(C_VB), pl.BlockSpec((t, 128), lambda p, i: (i, 0)),
                  pl.BlockSpec((FOX_HEADS, s), lambda p, i: (0, 0)), pl.BlockSpec((1, 1), lambda p, i: (0, 0)), blk(0),
                  pl.BlockSpec((None, t, 128), lambda p, i: (p, i, 0)), blk(M_FOX)] + c_in_specs,
        out_specs=[blk(0), res(0), res(0), pl.BlockSpec((None, 8, s), lambda p, i: (p, 0, 0))] + c_out_specs,
        scratch_shapes=c_sems, input_output_aliases=_comm_aliases(comm, 9, 4),
        compiler_params=_cp(ARB, ARB),
    )(qn, kn, proj, cfc, cfr, qk_bound, o, lse, dmix, *c_operands)
    return outs[0], outs[1], outs[2], outs[3], list(outs[4:])


def _fox_prep_bwd(proj, gq, gk, bf, dqn, dkn, dlogf, name):
    s = proj.shape[0]
    t = min(ATT_T, s)
    nblk = s // t

    def body(q_ref, k_ref, fl_ref, gq_ref, gk_ref, bf_ref, dqn_ref, dkn_ref, dlogf_ref,
             dq_ref, dk_ref, dfl_ref, dgq_ref, dgk_ref, dbf_ref):
        i = pl.program_id(0)
        gm = _group_matrix(512)

        @pl.when(i == 0)
        def _():
            dgq_ref[...] = jnp.zeros_like(dgq_ref)
            dgk_ref[...] = jnp.zeros_like(dgk_ref)
            dbf_ref[...] = jnp.zeros_like(dbf_ref)

        for src, gref, dyref, dst, dgref in ((q_ref, gq_ref, dqn_ref, dq_ref, dgq_ref),
                                              (k_ref, gk_ref, dkn_ref, dk_ref, dgk_ref)):
            v = src[...]
            dy = dyref[...]
            rstd = lax.rsqrt(_group_mean(v * v, gm) + EPS)
            xhat = v * rstd
            dgref[...] += jnp.sum(dy * xhat, axis=0, keepdims=True)
            dxh = dy * gref[...]
            dst[...] = (rstd * (dxh - xhat * _group_mean(dxh * xhat, gm))).astype(dst.dtype)

        xv = fl_ref[...] + bf_ref[...]
        e = jnp.exp(-jnp.abs(xv))
        dfl = dlogf_ref[...] * (jnp.where(xv >= 0.0, e, 1.0) / (1.0 + e))
        dfl_ref[...] = dfl.astype(dfl_ref.dtype)
        dbf_ref[...] += jnp.sum(dfl, axis=0, keepdims=True)

    col = lambda w, c0: pl.BlockSpec((t, w), lambda i: (i, c0 // w))
    v512 = pl.BlockSpec((1, 512), lambda i: (0, 0))
    v128 = pl.BlockSpec((1, 128), lambda i: (0, 0))
    return _pcall(
        body, name=name, grid=(nblk,),
        out_shape=[jax.ShapeDtypeStruct((s, 512), _BF), jax.ShapeDtypeStruct((s, 512), _BF),
                   jax.ShapeDtypeStruct((s, 128), _BF), jax.ShapeDtypeStruct((1, 512), F32),
                   jax.ShapeDtypeStruct((1, 512), F32), jax.ShapeDtypeStruct((1, 128), F32)],
        in_specs=[col(512, C_QB), col(512, C_KB), col(128, C_FL), v512, v512, v128, col(512, 0), col(512, 0), col(128, 0)],
        out_specs=[col(512, 0), col(512, 0), col(128, 0), v512, v512, v128], compiler_params=_cp(ARB),
    )(proj, proj, proj, gq, gk, bf, dqn, dkn, dlogf)


GELU_C = 0.7978845608028654
GELU_A = 0.044715


def _gelu(x):
    return 0.5 * x * (1.0 + jnp.tanh(GELU_C * (x + GELU_A * x * x * x)))


def _gelu_grad(x):
    th = jnp.tanh(GELU_C * (x + GELU_A * x * x * x))
    return 0.5 * (1.0 + th) + 0.5 * x * (1.0 - th * th) * (GELU_C * (1.0 + 3.0 * GELU_A * x * x))


def _sgu_tril():
    return _iota2((SGU_CHUNK, SGU_CHUNK), 0) >= _iota2((SGU_CHUNK, SGU_CHUNK), 1)


def _sgu_group_masks():
    lane = _iota2((1, 256), 1)
    return [(lane // HEAD_DIM) == g for g in range(SGU_GROUPS)]


def _sgu_fwd(proj, w, gs, bexp, name):
    s = proj.shape[0]
    t = min(2 * SGU_CHUNK, s)

    def body(u_ref, v_ref, w_ref, gs_ref, b_ref, o_ref):
        gm = _group_matrix(256)
        gmask = _sgu_group_masks()
        tril = _sgu_tril()
        u = _gelu(u_ref[...])
        vg = _gelu(v_ref[...])
        vhat = (vg * lax.rsqrt(_group_mean(vg * vg, gm) + EPS) * gs_ref[...]).astype(_BF)
        for ch in range(t // SGU_CHUNK):
            rows = slice(ch * SGU_CHUNK, (ch + 1) * SGU_CHUNK)
            mixed = b_ref[...]
            for g in range(SGU_GROUPS):
                wg = jnp.where(tril, w_ref[g], 0.0).astype(_BF)
                mixed = jnp.where(gmask[g], mixed + _dot(wg, vhat[rows], 1, 0), mixed)
            o_ref[rows, :] = u[rows] * mixed

    return _pcall(
        body, name=name, grid=(s // t,), out_shape=jax.ShapeDtypeStruct((s, 256), F32),
        in_specs=[_row_spec(t, 256, C_UC // 256), _row_spec(t, 256, C_VC // 256),
                  pl.BlockSpec((SGU_GROUPS, SGU_CHUNK, SGU_CHUNK), lambda i: (0, 0, 0)), _vec_spec(256),
                  pl.BlockSpec((SGU_CHUNK, 256), lambda i: (0, 0))],
        out_specs=_row_spec(t, 256), compiler_params=_cp(PAR),
    )(proj, proj, w, gs, bexp)


def _sgu_bwd(proj, w, gs, bexp, dmix, name):
    s = proj.shape[0]
    t = min(2 * SGU_CHUNK, s)
    nstep = s // t

    def body(u_ref, v_ref, w_ref, gs_ref, b_ref, do_ref, du_ref, dv_ref, dw_ref, db_ref, dgs_ref):
        i = pl.program_id(0)
        gm = _group_matrix(256)
        gmask = _sgu_group_masks()
        tril = _sgu_tril()

        @pl.when(i == 0)
        def _():
            dw_ref[...] = jnp.zeros_like(dw_ref)
            db_ref[...] = jnp.zeros_like(db_ref)
            dgs_ref[...] = jnp.zeros_like(dgs_ref)

        uc = u_ref[...]
        vc = v_ref[...]
        u = _gelu(uc)
        vg = _gelu(vc)
        rstd = lax.rsqrt(_group_mean(vg * vg, gm) + EPS)
        xh = vg * rstd
        gsv = gs_ref[...]
        vhat = (xh * gsv).astype(_BF)
        dov = do_ref[...]
        dm = dov * u
        for ch in range(t // SGU_CHUNK):
            rows = slice(ch * SGU_CHUNK, (ch + 1) * SGU_CHUNK)
            mixed = b_ref[...]
            dvh = jnp.zeros((SGU_CHUNK, 256), F32)
            dmc = dm[rows]
            for g in range(SGU_GROUPS):
                wg = jnp.where(tril, w_ref[g], 0.0).astype(_BF)
                mixed = jnp.where(gmask[g], mixed + _dot(wg, vhat[rows], 1, 0), mixed)
                dvh = jnp.where(gmask[g], _dot(wg, dmc.astype(_BF), 0, 0), dvh)
                dw_ref[g] += _dot(jnp.where(gmask[g], dmc, 0.0).astype(_BF), vhat[rows], 1, 1)
            db_ref[...] += dmc
            du_ref[rows, :] = (dov[rows] * mixed * _gelu_grad(uc[rows])).astype(du_ref.dtype)
            xhc = xh[rows]
            dgs_ref[...] += jnp.sum(dvh * xhc, axis=0, keepdims=True)
            dxh = dvh * gsv
            dvg = rstd[rows] * (dxh - xhc * _group_mean(dxh * xhc, gm))
            dv_ref[rows, :] = (dvg * _gelu_grad(vc[rows])).astype(dv_ref.dtype)

        @pl.when(i == nstep - 1)
        def _():
            for g in range(SGU_GROUPS):
                dw_ref[g] = jnp.where(tril, dw_ref[g], 0.0)

    wspec = pl.BlockSpec((SGU_GROUPS, SGU_CHUNK, SGU_CHUNK), lambda i: (0, 0, 0))
    bspec = pl.BlockSpec((SGU_CHUNK, 256), lambda i: (0, 0))
    return _pcall(
        body, name=name, grid=(nstep,),
        out_shape=[jax.ShapeDtypeStruct((s, 256), _BF), jax.ShapeDtypeStruct((s, 256), _BF),
                   jax.ShapeDtypeStruct((SGU_GROUPS, SGU_CHUNK, SGU_CHUNK), F32),
                   jax.ShapeDtypeStruct((SGU_CHUNK, 256), F32), jax.ShapeDtypeStruct((1, 256), F32)],
        in_specs=[_row_spec(t, 256, C_UC // 256), _row_spec(t, 256, C_VC // 256), wspec, _vec_spec(256), bspec,
                  _row_spec(t, 256, M_SGU // 256)],
        out_specs=[_row_spec(t, 256), _row_spec(t, 256), wspec, bspec, _vec_spec(256)],
        compiler_params=_cp(ARB),
    )(proj, proj, w, gs, bexp, dmix)


def _ada_fwd(c_all, ada_w, name):
    depth, d, n = ada_w.shape

    def body(c_ref, w_ref, o_ref):
        cv = c_ref[...]
        cond = cv / (1.0 + jnp.exp(-cv))
        o_ref[...] = _dot_f32(cond, w_ref[...], 1, 0)

    return _pcall(
        body, name=name, grid=(depth,), out_shape=jax.ShapeDtypeStruct((depth, N_DEV, n), F32),
        in_specs=[pl.BlockSpec((N_DEV, d), lambda l: (0, 0)), pl.BlockSpec((None, d, n), lambda l: (l, 0, 0))],
        out_specs=pl.BlockSpec((None, N_DEV, n), lambda l: (l, 0, 0)), compiler_params=_cp(PAR),
    )(c_all, ada_w)


def _ada_bwd(c_all, dmod, name):
    depth, _, n = dmod.shape
    d = c_all.shape[1]

    def body(c_ref, dm_ref, o_ref):
        cv = c_ref[...]
        cond = cv / (1.0 + jnp.exp(-cv))
        o_ref[...] = _dot_f32(cond, dm_ref[...], 0, 0)

    return _pcall(
        body, name=name, grid=(depth,), out_shape=jax.ShapeDtypeStruct((depth, d, n), F32),
        in_specs=[pl.BlockSpec((N_DEV, d), lambda l: (0, 0)), pl.BlockSpec((None, N_DEV, n), lambda l: (l, 0, 0))],
        out_specs=pl.BlockSpec((None, d, n), lambda l: (l, 0, 0)), compiler_params=_cp(PAR),
    )(c_all, dmod)


def _adamw(slots, w, m, v, name):
    n, r, c = slots.shape
    tr = 256 if r % 256 == 0 else r
    bc1 = 1.0 - ADAM_B1 ** ADAM_STEP
    bc2 = 1.0 - ADAM_B2 ** ADAM_STEP

    def body(s_ref, w_ref, m_ref, v_ref, g_ref, d_ref, nm_ref, nv_ref):
        g = s_ref[0].astype(F32)
        for j in range(1, n):
            g = g + s_ref[j].astype(F32)
        m_new = ADAM_B1 * m_ref[...] + (1.0 - ADAM_B1) * g
        v_new = ADAM_B2 * v_ref[...] + (1.0 - ADAM_B2) * (g * g)
        g_ref[...] = g
        nm_ref[...] = m_new
        nv_ref[...] = v_new
        d_ref[...] = -ADAM_LR * ((m_new / bc1) / (jnp.sqrt(v_new / bc2) + ADAM_EPS) + ADAM_WD * w_ref[...])

    tile = pl.BlockSpec((tr, c), lambda i: (i, 0))
    shp = jax.ShapeDtypeStruct((r, c), F32)
    return _pcall(
        body, name=name, grid=(r // tr,), out_shape=[shp] * 4,
        in_specs=[pl.BlockSpec((n, tr, c), lambda i: (0, i, 0)), tile, tile, tile],
        out_specs=[tile] * 4, compiler_params=_cp(PAR),
    )(slots, w, m, v)


def _pad_cols(a, width):
    return jnp.pad(a, ((0, 0), (0, width - a.shape[1])))


def _w_in_layout(w):
    pad = jnp.zeros(w.shape[:-1] + (IN_P - IN_W,), w.dtype)
    return jnp.concatenate([w[..., 768:FL_SRC], w[..., :768], w[..., UC_SRC:], w[..., FL_SRC:UC_SRC], pad], axis=-1)


def _w_in_unlayout(g):
    return jnp.concatenate([g[..., C_QA:C_UC], g[..., :C_QA], g[..., C_FL:C_FL + FOX_HEADS], g[..., C_UC:C_FL]], axis=-1)


SMALL = [("ada_b", DEPTH * 6 * D_MODEL), ("norm1_g", DEPTH * D_MODEL), ("norm2_g", DEPTH * D_MODEL),
         ("sgu_w", DEPTH * SGU_GROUPS * SGU_CHUNK * SGU_CHUNK), ("sgu_b", DEPTH * SGU_GROUPS * SGU_CHUNK),
         ("sgu_norm_g", DEPTH * SGU_GROUPS * HEAD_DIM), ("q_norm_g", DEPTH * HEAD_DIM), ("k_norm_g", DEPTH * HEAD_DIM),
         ("b_forget", DEPTH * FOX_HEADS), ("loss", 1)]
SMALL_ROWS = 2560


def _pack_small(parts):
    flat = jnp.concatenate([parts[name].reshape(-1).astype(F32) for name, _ in SMALL])
    return jnp.pad(flat, (0, SMALL_ROWS * 128 - flat.shape[0])).reshape(SMALL_ROWS, 128)


def _unpack_small(packed, shapes):
    flat = packed.reshape(-1)
    out, off = {}, 0
    for name, size in SMALL:
        out[name] = flat[off:off + size].reshape(shapes[name])
        off += size
    return out


def kernel(x, c, ada_w, ada_b, norm1_g, norm2_g, w_in, b_forget, q_norm_g, k_norm_g, sgu_norm_g, sgu_w, sgu_b, w_out, mlp_w1, mlp_w2, loss_target, m_ada_w, m_ada_b, m_norm1_g, m_norm2_g, m_w_in, m_b_forget, m_q_norm_g, m_k_norm_g, m_sgu_norm_g, m_sgu_w, m_sgu_b, m_w_out, m_mlp_w1, m_mlp_w2, v_ada_w, v_ada_b, v_norm1_g, v_norm2_g, v_w_in, v_b_forget, v_q_norm_g, v_k_norm_g, v_sgu_norm_g, v_sgu_w, v_sgu_b, v_w_out, v_mlp_w1, v_mlp_w2):
    me = _lin(_my_pos())
    x0 = x[0]
    target = loss_target[0]
    n_ada = ada_w.shape[2]

    shards = [w.astype(_XBF) for w in (w_in, w_out, mlp_w1, mlp_w2)]

    def whole(w_in_g, w_out_g, w1_g, w2_g):
        return (_w_in_layout(jnp.transpose(w_in_g, (1, 0, 2)).reshape(D_MODEL, IN_W)),
                w_out_g.reshape(D_MODEL, D_MODEL),
                jnp.transpose(w1_g, (1, 0, 2)).reshape(D_MODEL, D_FF),
                w2_g.reshape(D_FF, D_MODEL))

    *first, c_all = _exchange([w[0] for w in shards] + [c], False, "gather_weights")
    c_all = c_all.reshape(N_DEV, D_MODEL)
    weights = [whole(*first)]

    mod_part = _ada_fwd(c_all, ada_w, "ada_fwd")
    (mod_rows,) = _exchange([jnp.transpose(mod_part, (1, 0, 2))], True, "scatter_mod")
    mod = jnp.transpose(mod_rows, (1, 0, 2)).reshape(DEPTH, 6 * D_MODEL) + ada_b
    mods = mod.reshape(DEPTH, 6, 1, D_MODEL)

    saved = []
    xl = x0
    for l in range(DEPTH):
        sh1, sc1, g1, sh2, sc2, g2 = (mods[l, r] for r in range(6))
        n1 = norm1_g[l].reshape(1, D_MODEL)
        n2 = norm2_g[l].reshape(1, D_MODEL)
        gq = jnp.tile(q_norm_g[l], FOX_HEADS).reshape(1, 512)
        gk = jnp.tile(k_norm_g[l], FOX_HEADS).reshape(1, 512)
        bf = jnp.pad(b_forget[l], (0, 128 - FOX_HEADS)).reshape(1, 128)
        gs = sgu_norm_g[l].reshape(1, 256)
        bexp = jnp.repeat(sgu_b[l].T, HEAD_DIM, axis=1)

        w_in_l, w_out_l, w1_l, w2_l = weights[l]
        h1 = _norm_mod(xl, n1, sc1, sh1, "norm_mod")
        proj = _mm_nn(h1, w_in_l, "plain", (), "mm_in")
        o_sb = _sb_fwd(proj, "sb_fwd")
        qn, kn, cfc, cfr = _fox_prep(proj, gq, gk, bf, "fox_prep")
        qkb = (1.02 * HEAD_DIM ** 0.5 * jnp.max(jnp.abs(q_norm_g[l])) * jnp.max(jnp.abs(k_norm_g[l]))).reshape(1, 1)
        gather_next = None
        if l + 1 < DEPTH:
            gather_next = dict(arrays=[w[l + 1] for w in shards], scatter=False, recv=None, slot=None)
        o_fox, lse, gathered = _fox_fwd(proj, qn, kn, cfc, cfr, qkb, "fox_fwd", gather_next)
        if gathered:
            weights.append(whole(*gathered))
        o_sgu = _sgu_fwd(proj, sgu_w[l], gs, bexp, "sgu_fwd")
        mixed = jnp.concatenate([o_sb, o_fox, o_sgu], axis=-1).astype(_BF)
        x_mid, y1 = _mm_nn(mixed, w_out_l, "resid", (xl, g1), "mm_out")
        h2 = _norm_mod(x_mid, n2, sc2, sh2, "norm_mod")
        r2, r1 = _mm_nn(h2, w1_l, "relu2", (), "mm_w1")
        x_out, y2 = _mm_nn(r2, w2_l, "resid", (x_mid, g2), "mm_w2")
        saved.append(dict(x_in=xl, h1=h1, proj=proj, o_sb=o_sb, qn=qn, kn=kn, cfc=cfc, cfr=cfr, o_fox=o_fox, lse=lse,
                          mixed=mixed, x_mid=x_mid, y1=y1, h2=h2, r2=r2, r1=r1, y2=y2,
                          n1=n1, n2=n2, gq=gq, gk=gk, bf=bf, gs=gs, bexp=bexp, qkb=qkb))
        xl = x_out

    loss_part, dx = _loss_head(xl, target, "loss_head")

    grads_ready = None
    received = [jnp.zeros((N_DEV,) + w.shape, _XBF) for w in shards]
    small = {k: [None] * DEPTH for k in ("mod", "norm1_g", "norm2_g", "sgu_w", "sgu_b", "sgu_norm_g",
                                         "q_norm_g", "k_norm_g", "b_forget")}
    for l in reversed(range(DEPTH)):
        sv = saved[l]
        sh1, sc1, g1, sh2, sc2, g2 = (mods[l, r] for r in range(6))
        w_in_l, w_out_l, w1_l, w2_l = weights[l]
        dy2, dg2 = _res_bwd(dx, sv["y2"], g2, "res_bwd")
        da = _mm_nt(dy2, w2_l, "mul2", (sv["r1"],), _BF, "mm_w2_bwd")
        g_w2 = _mm_tn(sv["r2"], dy2, "rows", "mm_w2_grad")
        dh2 = _mm_nt(da, w1_l, "plain", (), F32, "mm_w1_bwd")
        g_w1 = _mm_tn(sv["h2"], da, "cols", "mm_w1_grad")
        dx_mid, dsh2, dsc2, dn2 = _norm_mod_bwd(sv["x_mid"], dh2, dx, sv["n2"], sc2, "norm_mod_bwd")
        dy1, dg1 = _res_bwd(dx_mid, sv["y1"], g1, "res_bwd")
        dmix = _mm_nt(dy1, w_out_l, "plain", (), F32, "mm_out_bwd")
        g_out = _mm_tn(sv["mixed"], dy1, "rows", "mm_out_grad")
        dq_sb, dk_sb, dv_sb = _sb_bwd(sv["proj"], sv["o_sb"], dmix, "sb_bwd")
        send_prev = None
        if grads_ready is not None:
            send_prev = dict(arrays=grads_ready, scatter=True, recv=received, slot=l + 1)
        dqn, dkn, dv_fox, dcr, filled = _fox_bwd(sv["proj"], sv["qn"], sv["kn"], sv["cfc"], sv["cfr"], sv["qkb"],
                                                 sv["o_fox"], sv["lse"], dmix, "fox_bwd", send_prev)
        if filled:
            received = filled
        dlogf = _pad_cols(dcr[:, :2, :].reshape(FOX_HEADS, -1).T, 128)
        dq_fox, dk_fox, dfl, dgq, dgk, dbf = _fox_prep_bwd(sv["proj"], sv["gq"], sv["gk"], sv["bf"], dqn, dkn, dlogf,
                                                           "fox_prep_bwd")
        duc, dvc, dsw, dsb, dsg = _sgu_bwd(sv["proj"], sgu_w[l], sv["gs"], sv["bexp"], dmix, "sgu_bwd")
        dproj = jnp.concatenate(
            [dq_fox, dk_fox, dv_fox.astype(_BF), dq_sb.astype(_BF), dk_sb.astype(_BF), dv_sb.astype(_BF), duc, dvc, dfl,
             jnp.zeros((dfl.shape[0], IN_P - C_FL - 128), _BF)], axis=-1)
        dh1 = _mm_nt(dproj, w_in_l, "plain", (), F32, "mm_in_bwd")
        g_in = _w_in_unlayout(_mm_tn(sv["h1"], dproj, None, "mm_in_grad"))
        g_in = jnp.transpose(g_in.reshape(D_MODEL, N_DEV, IN_W // N_DEV), (1, 0, 2))
        grads_ready = [g_in, g_out, g_w1, g_w2]
        dx, dsh1, dsc1, dn1 = _norm_mod_bwd(sv["x_in"], dh1, dx_mid, sv["n1"], sc1, "norm_mod_bwd")

        small["mod"][l] = jnp.concatenate([dsh1, dsc1, dg1, dsh2, dsc2, dg2], axis=-1).reshape(-1)
        small["norm1_g"][l] = dn1.reshape(-1)
        small["norm2_g"][l] = dn2.reshape(-1)
        small["sgu_w"][l] = dsw
        small["sgu_b"][l] = dsb.reshape(SGU_CHUNK, SGU_GROUPS, HEAD_DIM).sum(-1).T
        small["sgu_norm_g"][l] = dsg.reshape(SGU_GROUPS, HEAD_DIM)
        small["q_norm_g"][l] = dgq.reshape(FOX_HEADS, HEAD_DIM).sum(0)
        small["k_norm_g"][l] = dgk.reshape(FOX_HEADS, HEAD_DIM).sum(0)
        small["b_forget"][l] = dbf[0, :FOX_HEADS]

    parts = {k: jnp.stack(v) for k, v in small.items()}
    parts["ada_b"] = parts.pop("mod")
    parts["loss"] = loss_part
    (small_all,) = _exchange([_pack_small(parts)], False, "gather_small")
    zero1 = jnp.zeros((1,), F32)
    packs = [_pack_small(dict(ada_b=a, norm1_g=b, norm2_g=cc, sgu_w=d, sgu_b=e, sgu_norm_g=f, q_norm_g=g, k_norm_g=h,
                              b_forget=i, loss=zero1))
             for a, b, cc, d, e, f, g, h, i in (
                 (ada_b, norm1_g, norm2_g, sgu_w, sgu_b, sgu_norm_g, q_norm_g, k_norm_g, b_forget),
                 (m_ada_b, m_norm1_g, m_norm2_g, m_sgu_w, m_sgu_b, m_sgu_norm_g, m_q_norm_g, m_k_norm_g, m_b_forget),
                 (v_ada_b, v_norm1_g, v_norm2_g, v_sgu_w, v_sgu_b, v_sgu_norm_g, v_q_norm_g, v_k_norm_g, v_b_forget))]
    small_shapes = dict(ada_b=ada_b.shape, norm1_g=norm1_g.shape, norm2_g=norm2_g.shape, sgu_w=sgu_w.shape,
                        sgu_b=sgu_b.shape, sgu_norm_g=sgu_norm_g.shape, q_norm_g=q_norm_g.shape,
                        k_norm_g=k_norm_g.shape, b_forget=b_forget.shape, loss=())
    s_out = [_unpack_small(a, small_shapes) for a in _adamw(small_all, *packs, "adamw_small")]

    dmod_all = small_all[:, :DEPTH * 6 * D_MODEL // 128, :].reshape(N_DEV, DEPTH, 6 * D_MODEL)
    dmod_mine = lax.dynamic_slice_in_dim(dmod_all, me * n_ada, n_ada, axis=2)
    g_ada = _ada_bwd(c_all, jnp.transpose(dmod_mine, (1, 0, 2)), "ada_bwd")
    ada_out = _adamw(g_ada.reshape(1, DEPTH * D_MODEL, n_ada), ada_w.reshape(-1, n_ada), m_ada_w.reshape(-1, n_ada),
                     v_ada_w.reshape(-1, n_ada), "adamw_ada")
    ada_out = [a.reshape(ada_w.shape) for a in ada_out]

    ri, ro, r1, r2 = _exchange(grads_ready, True, "scatter_grads", recv=received, slot=0)

    def big(slots, w, m, v, name):
        cdim = w.shape[-1]
        outs = _adamw(slots.reshape(N_DEV, -1, cdim), w.reshape(-1, cdim), m.reshape(-1, cdim), v.reshape(-1, cdim), name)
        return [a.reshape(w.shape) for a in outs]

    in_out = big(ri, w_in, m_w_in, v_w_in, "adamw_in")
    out_out = big(ro, w_out, m_w_out, v_w_out, "adamw_out")
    w1_out = big(r1, mlp_w1, m_mlp_w1, v_mlp_w1, "adamw_w1")
    w2_out = big(r2, mlp_w2, m_mlp_w2, v_mlp_w2, "adamw_w2")

    def leaves(idx):
        sm = s_out[idx]
        return [ada_out[idx], sm["ada_b"], sm["norm1_g"], sm["norm2_g"], in_out[idx], sm["b_forget"], sm["q_norm_g"],
                sm["k_norm_g"], sm["sgu_norm_g"], sm["sgu_w"], sm["sgu_b"], out_out[idx], w1_out[idx], w2_out[idx]]

    loss = s_out[0]["loss"]
    grad_x = dx.reshape(x.shape)
    return (loss, grad_x, *leaves(0), *leaves(1), *leaves(2), *leaves(3))
```

```python
import jax
import jax.numpy as jnp
from jax import lax
from jax.experimental import pallas as pl
from jax.experimental.pallas import tpu as pltpu

F32 = jnp.float32
_BF = jnp.bfloat16
_XBF = jnp.bfloat16

N_DEV = 8
D_MODEL = 1024
DEPTH = 4
HEAD_DIM = 64
SB_HEADS = 4
FOX_HEADS = 8
SGU_GROUPS = 4
SGU_CHUNK = 128
D_FF = 4096
EPS = 1e-6
IN_W = 2824
FL_SRC = 2304
UC_SRC = 2312

C_QB, C_KB, C_VB = 0, 512, 1024
C_QA, C_KA, C_VA = 1536, 1792, 2048
C_UC, C_VC = 2304, 2560
C_FL = 2816
IN_P = 3072
M_SB, M_FOX, M_SGU = 0, 256, 768

ADAM_LR = 0.001
ADAM_B1 = 0.9
ADAM_B2 = 0.999
ADAM_EPS = 1e-08
ADAM_WD = 0.01
ADAM_STEP = 10

ATT_T = 256
ROW_T = 512
MM_T = 1024
NEG = -1e30
UNDERFLOW = -100.0

MESH = pl.DeviceIdType.MESH
ARB = pltpu.ARBITRARY
PAR = pltpu.PARALLEL
HIGHEST = lax.Precision.HIGHEST


def _pcall(body, **kw):
    return pl.pallas_call(body, **kw)


def _cp(*sem):
    return pltpu.CompilerParams(dimension_semantics=tuple(sem))


def _dot(a, b, ca, cb):
    return lax.dot_general(a, b, (((ca,), (cb,)), ((), ())), preferred_element_type=F32)


def _dot_f32(a, b, ca, cb):
    return lax.dot_general(a, b, (((ca,), (cb,)), ((), ())), precision=HIGHEST, preferred_element_type=F32)


def _split(v, parts):
    out = []
    r = v
    for _ in range(parts - 1):
        p = r.astype(_BF)
        out.append(p)
        r = r - p.astype(F32)
    out.append(r.astype(_BF))
    return out


def _dot_exact(v, m, ca, cb, parts, v_left=True):
    out = None
    for p in _split(v, parts):
        term = _dot(p, m, ca, cb) if v_left else _dot(m, p, ca, cb)
        out = term if out is None else out + term
    return out


def _iota2(shape, dim):
    return lax.broadcasted_iota(jnp.int32, shape, dim)


def _my_pos():
    return lax.axis_index("x"), lax.axis_index("y"), lax.axis_index("c")


def _flip(pos, k):
    x, y, c = pos
    px = 1 - x if (k >> 2) & 1 else x
    py = 1 - y if (k >> 1) & 1 else y
    pc = 1 - c if k & 1 else c
    return px, py, pc


def _lin(pos):
    return 4 * pos[0] + 2 * pos[1] + pos[2]


def _exchange_ops(ins, outs, sems, scatter, slot):
    send_sems, recv_sems, local_sems = sems
    n = len(ins)

    def copies(with_recvs):
        me = _my_pos()
        me_i = _lin(me)
        dst = lambda a, j: outs[a].at[j] if slot is None else outs[a].at[j, slot]
        mine = lambda a: ins[a].at[me_i] if scatter else ins[a]
        local = [pltpu.make_async_copy(mine(a), dst(a, me_i), local_sems.at[a]) for a in range(n)]
        sends, recvs = [], []
        for a in range(n):
            for k in range(1, N_DEV):
                peer = _flip(me, k)
                pair = dict(send_sem=send_sems.at[a, k - 1], recv_sem=recv_sems.at[a, k - 1],
                            device_id=peer, device_id_type=MESH)
                src = ins[a].at[_lin(peer)] if scatter else ins[a]
                sends.append(pltpu.make_async_remote_copy(src_ref=src, dst_ref=dst(a, me_i), **pair))
                if with_recvs:
                    recvs.append(pltpu.make_async_remote_copy(src_ref=mine(a), dst_ref=dst(a, _lin(peer)), **pair))
        return local, sends, recvs

    def start():
        local, sends, _ = copies(False)
        for cp in local + sends:
            cp.start()

    def wait():
        local, sends, recvs = copies(True)
        for snd, rcv in zip(sends, recvs):
            snd.wait_send()
            rcv.wait_recv()
        for cp in local:
            cp.wait()

    return start, wait


def _comm_operands(comm):
    arrays, recv = comm["arrays"], comm["recv"]
    n = len(arrays)
    any_spec = pl.BlockSpec(memory_space=pl.ANY)
    if recv is not None:
        out_shapes = [jax.ShapeDtypeStruct(r.shape, r.dtype) for r in recv]
    elif comm["scatter"]:
        out_shapes = [jax.ShapeDtypeStruct(a.shape, a.dtype) for a in arrays]
    else:
        out_shapes = [jax.ShapeDtypeStruct((N_DEV,) + a.shape, a.dtype) for a in arrays]
    operands = list(arrays) + (list(recv) if recv is not None else [])
    sems = [pltpu.SemaphoreType.DMA((n, N_DEV - 1)), pltpu.SemaphoreType.DMA((n, N_DEV - 1)),
            pltpu.SemaphoreType.DMA((n,))]
    return operands, [any_spec] * len(operands), out_shapes, [any_spec] * n, sems


def _with_comm(body, comm, n_in, n_out, first_step, last_step):
    if comm is None:
        return body
    n = len(comm["arrays"])
    n_cin = n if comm["recv"] is None else 2 * n

    def wrapped(*refs):
        ins, cins = refs[:n_in], refs[n_in:n_in + n]
        o0 = n_in + n_cin
        outs, couts, sems = refs[o0:o0 + n_out], refs[o0 + n_out:o0 + n_out + n], refs[o0 + n_out + n:]
        start, wait = _exchange_ops(cins, couts, sems, comm["scatter"], comm["slot"])
        pl.when(first_step())(start)
        body(*ins, *outs)
        pl.when(last_step())(wait)

    return wrapped


def _comm_aliases(comm, n_in, n_out):
    if comm is None or comm["recv"] is None:
        return {}
    n = len(comm["arrays"])
    return {n_in + n + a: n_out + a for a in range(n)}


def _exchange(arrays, scatter, name, recv=None, slot=None):
    comm = dict(arrays=arrays, scatter=scatter, recv=recv, slot=slot)
    operands, in_specs, out_shapes, out_specs, sems = _comm_operands(comm)
    n = len(arrays)

    def body(*refs):
        n_cin = len(operands)
        start, wait = _exchange_ops(refs[:n], refs[n_cin:n_cin + n], refs[n_cin + n:], scatter, slot)
        start()
        wait()

    outs = _pcall(body, name=name, out_shape=out_shapes, in_specs=in_specs, out_specs=out_specs, scratch_shapes=sems,
                  input_output_aliases=_comm_aliases(comm, 0, 0))(*operands)
    return list(outs)


def _row_spec(t, w, col=0):
    return pl.BlockSpec((t, w), lambda i: (i, col))


def _vec_spec(w, col=0):
    return pl.BlockSpec((1, w), lambda i: (0, col))


def _norm_mod(x, g, sc, sh, name):
    s, d = x.shape
    t = min(ROW_T, s)

    def body(x_ref, g_ref, sc_ref, sh_ref, h_ref):
        xf = x_ref[...]
        rstd = lax.rsqrt(jnp.mean(xf * xf, axis=-1, keepdims=True) + EPS)
        y = xf * rstd * g_ref[...]
        h_ref[...] = (y * (1.0 + sc_ref[...]) + sh_ref[...]).astype(h_ref.dtype)

    return _pcall(
        body, name=name, grid=(s // t,), out_shape=jax.ShapeDtypeStruct((s, d), _BF),
        in_specs=[_row_spec(t, d), _vec_spec(d), _vec_spec(d), _vec_spec(d)],
        out_specs=_row_spec(t, d), compiler_params=_cp(PAR),
    )(x, g, sc, sh)


def _norm_mod_bwd(x, dh, dres, g, sc, name):
    s, d = x.shape
    t = min(ROW_T, s)

    def body(x_ref, dh_ref, dres_ref, g_ref, sc_ref, dx_ref, dsh_ref, dsc_ref, dg_ref):
        i = pl.program_id(0)
        xf = x_ref[...]
        dh = dh_ref[...]
        gv = g_ref[...]
        rstd = lax.rsqrt(jnp.mean(xf * xf, axis=-1, keepdims=True) + EPS)
        xhat = xf * rstd
        dn = dh * (1.0 + sc_ref[...])
        dxh = dn * gv
        dx_ref[...] = dres_ref[...] + rstd * (dxh - xhat * jnp.mean(dxh * xhat, axis=-1, keepdims=True))

        @pl.when(i == 0)
        def _():
            dsh_ref[...] = jnp.zeros_like(dsh_ref)
            dsc_ref[...] = jnp.zeros_like(dsc_ref)
            dg_ref[...] = jnp.zeros_like(dg_ref)

        dsh_ref[...] += jnp.sum(dh, axis=0, keepdims=True)
        dsc_ref[...] += jnp.sum(dh * (xhat * gv), axis=0, keepdims=True)
        dg_ref[...] += jnp.sum(dn * xhat, axis=0, keepdims=True)

    vec = jax.ShapeDtypeStruct((1, d), F32)
    return _pcall(
        body, name=name, grid=(s // t,),
        out_shape=[jax.ShapeDtypeStruct((s, d), F32), vec, vec, vec],
        in_specs=[_row_spec(t, d), _row_spec(t, d), _row_spec(t, d), _vec_spec(d), _vec_spec(d)],
        out_specs=[_row_spec(t, d), _vec_spec(d), _vec_spec(d), _vec_spec(d)],
        compiler_params=_cp(ARB),
    )(x, dh, dres, g, sc)


def _res_bwd(dx, y, gate, name):
    s, d = dx.shape
    t = min(ROW_T, s)

    def body(dx_ref, y_ref, g_ref, dy_ref, dg_ref):
        i = pl.program_id(0)
        dxv = dx_ref[...]
        dy_ref[...] = (dxv * g_ref[...]).astype(dy_ref.dtype)

        @pl.when(i == 0)
        def _():
            dg_ref[...] = jnp.zeros_like(dg_ref)

        dg_ref[...] += jnp.sum(dxv * y_ref[...], axis=0, keepdims=True)

    return _pcall(
        body, name=name, grid=(s // t,),
        out_shape=[jax.ShapeDtypeStruct((s, d), _BF), jax.ShapeDtypeStruct((1, d), F32)],
        in_specs=[_row_spec(t, d), _row_spec(t, d), _vec_spec(d)],
        out_specs=[_row_spec(t, d), _vec_spec(d)], compiler_params=_cp(ARB),
    )(dx, y, gate)


def _loss_head(y, target, name):
    s, d = y.shape
    t = min(ROW_T, s)

    def body(y_ref, t_ref, loss_ref, dy_ref):
        i = pl.program_id(0)
        diff = y_ref[...] - t_ref[...]
        dy_ref[...] = diff * (1.0 / d)

        @pl.when(i == 0)
        def _():
            loss_ref[...] = jnp.zeros_like(loss_ref)

        rows = jnp.sum(diff * diff, axis=-1, keepdims=True)
        loss_ref[...] += (0.5 / d) * jnp.sum(rows, axis=0, keepdims=True)

    return _pcall(
        body, name=name, grid=(s // t,),
        out_shape=[jax.ShapeDtypeStruct((1, 1), F32), jax.ShapeDtypeStruct((s, d), F32)],
        in_specs=[_row_spec(t, d), _row_spec(t, d)],
        out_specs=[pl.BlockSpec((1, 1), lambda i: (0, 0)), _row_spec(t, d)], compiler_params=_cp(ARB),
    )(y, target)


def _mm_nn(a, b, epi, extras, name):
    m, kdim = a.shape
    n = b.shape[1]
    tm, tn, tk = min(MM_T, m), min(MM_T, n), min(MM_T, kdim)
    nk = kdim // tk
    n_extra = len(extras)
    n_out = 1 if epi == "plain" else 2

    def finish(y, extra_refs, out_refs):
        if epi == "plain":
            out_refs[0][...] = y.astype(out_refs[0].dtype)
        elif epi == "resid":
            x_ref, g_ref = extra_refs
            out_refs[0][...] = x_ref[...] + g_ref[...] * y
            out_refs[1][...] = y
        else:
            r = jnp.maximum(y, 0.0)
            out_refs[0][...] = (r * r).astype(out_refs[0].dtype)
            out_refs[1][...] = r.astype(out_refs[1].dtype)

    def body(a_ref, b_ref, *rest):
        extra_refs = rest[:n_extra]
        out_refs = rest[n_extra:n_extra + n_out]
        part = _dot(a_ref[...].astype(_BF), b_ref[...].astype(_BF), 1, 0)
        if nk == 1:
            finish(part, extra_refs, out_refs)
        else:
            acc_ref = rest[-1]
            k = pl.program_id(2)

            @pl.when(k == 0)
            def _():
                acc_ref[...] = part

            @pl.when(k > 0)
            def _():
                acc_ref[...] += part

            @pl.when(k == nk - 1)
            def _():
                finish(acc_ref[...], extra_refs, out_refs)

    tile = pl.BlockSpec((tm, tn), lambda i, j, k: (i, j))
    in_specs = [pl.BlockSpec((tm, tk), lambda i, j, k: (i, k)), pl.BlockSpec((tk, tn), lambda i, j, k: (k, j))]
    if epi == "plain":
        out_shape = [jax.ShapeDtypeStruct((m, n), F32)]
    elif epi == "resid":
        in_specs += [tile, pl.BlockSpec((1, tn), lambda i, j, k: (0, j))]
        out_shape = [jax.ShapeDtypeStruct((m, n), F32)] * 2
    else:
        out_shape = [jax.ShapeDtypeStruct((m, n), _BF), jax.ShapeDtypeStruct((m, n), F32)]
    outs = _pcall(
        body, name=name, grid=(m // tm, n // tn, nk), out_shape=out_shape,
        in_specs=in_specs, out_specs=[tile] * n_out,
        scratch_shapes=[pltpu.VMEM((tm, tn), F32)] if nk > 1 else [],
        compiler_params=_cp(PAR, PAR, ARB),
    )(a, b, *extras)
    return outs[0] if n_out == 1 else outs


def _mm_nt(a, b, epi, extras, out_dtype, name):
    m, kdim = a.shape
    n = b.shape[0]
    tm, tn, tk = min(MM_T, m), min(MM_T, n), min(MM_T, kdim)
    nk = kdim // tk
    n_extra = len(extras)

    def finish(y, extra_refs, o_ref):
        if epi == "mul2":
            y = y * (2.0 * extra_refs[0][...].astype(F32))
        o_ref[...] = y.astype(o_ref.dtype)

    def body(a_ref, b_ref, *rest):
        extra_refs = rest[:n_extra]
        o_ref = rest[n_extra]
        part = _dot(a_ref[...].astype(_BF), b_ref[...].astype(_BF), 1, 1)
        if nk == 1:
            finish(part, extra_refs, o_ref)
        else:
            acc_ref = rest[-1]
            k = pl.program_id(2)

            @pl.when(k == 0)
            def _():
                acc_ref[...] = part

            @pl.when(k > 0)
            def _():
                acc_ref[...] += part

            @pl.when(k == nk - 1)
            def _():
                finish(acc_ref[...], extra_refs, o_ref)

    tile = pl.BlockSpec((tm, tn), lambda i, j, k: (i, j))
    in_specs = [pl.BlockSpec((tm, tk), lambda i, j, k: (i, k)), pl.BlockSpec((tn, tk), lambda i, j, k: (j, k))]
    in_specs += [tile] * n_extra
    return _pcall(
        body, name=name, grid=(m // tm, n // tn, nk), out_shape=jax.ShapeDtypeStruct((m, n), out_dtype),
        in_specs=in_specs, out_specs=tile,
        scratch_shapes=[pltpu.VMEM((tm, tn), F32)] if nk > 1 else [],
        compiler_params=_cp(PAR, PAR, ARB),
    )(a, b, *extras)


def _mm_tn(a, b, split, name):
    s, m = a.shape
    n = b.shape[1]
    ts = min(MM_T, s)
    ns = s // ts
    if split == "rows":
        tm, tn = min(MM_T, m // N_DEV), min(MM_T, n)
        per = (m // N_DEV) // tm
        out_shape = (N_DEV, m // N_DEV, n)
        out_spec = pl.BlockSpec((None, tm, tn), lambda i, j, k: (i // per, i % per, j))
    elif split == "cols":
        tm, tn = min(MM_T, m), min(MM_T, n // N_DEV)
        per = (n // N_DEV) // tn
        out_shape = (N_DEV, m, n // N_DEV)
        out_spec = pl.BlockSpec((None, tm, tn), lambda i, j, k: (j // per, i, j % per))
    else:
        tm, tn = min(MM_T, m), min(MM_T, n)
        out_shape = (m, n)
        out_spec = pl.BlockSpec((tm, tn), lambda i, j, k: (i, j))

    def body(a_ref, b_ref, o_ref, acc_ref):
        k = pl.program_id(2)
        part = _dot(a_ref[...].astype(_BF), b_ref[...].astype(_BF), 0, 0)

        @pl.when(k == 0)
        def _():
            acc_ref[...] = part

        @pl.when(k > 0)
        def _():
            acc_ref[...] += part

        @pl.when(k == ns - 1)
        def _():
            o_ref[...] = acc_ref[...].astype(o_ref.dtype)

    return _pcall(
        body, name=name, grid=(m // tm, n // tn, ns), out_shape=jax.ShapeDtypeStruct(out_shape, _XBF),
        in_specs=[pl.BlockSpec((ts, tm), lambda i, j, k: (k, i)), pl.BlockSpec((ts, tn), lambda i, j, k: (k, j))],
        out_specs=out_spec, scratch_shapes=[pltpu.VMEM((tm, tn), F32)],
        compiler_params=_cp(PAR, PAR, ARB),
    )(a, b)


def _head_masks():
    lane = _iota2((1, 128), 1)
    return [(lane // HEAD_DIM) == hh for hh in range(2)]


def _softplus_parts(z):
    sp = jnp.maximum(z, 0.0) + jnp.log(1.0 + jnp.exp(-jnp.abs(z)))
    return -sp, z - sp


def _sb_key_loop(i, carry, step, lookahead, first_ahead):
    def cond(c):
        return jnp.logical_and(c[0] <= i, jnp.max(jnp.maximum(c[1][0][0], c[1][0][1])) > UNDERFLOW)

    def body(c):
        ahead = lookahead(i - c[0] - 1)
        return c[0] + 1, step(i - c[0], c[1], c[2]), ahead

    return lax.while_loop(cond, body, (jnp.int32(1), carry, first_ahead))[1]


def _sb_fwd(proj, name):
    s = proj.shape[0]
    t = min(ATT_T, s)
    nblk = s // t
    scale = HEAD_DIM ** -0.5

    def body(q_ref, k_ref, v_ref, o_ref):
        i = pl.program_id(1)
        masks = _head_masks()
        row = _iota2((t, t), 0)
        col = _iota2((t, t), 1)
        u_strict = (row > col).astype(_BF)
        causal = col < row
        qs = q_ref[...] * scale
        qm = [jnp.where(masks[hh], qs, 0.0).astype(_BF) for hh in range(2)]

        def scores(j):
            kb = k_ref[pl.ds(pl.multiple_of(jnp.maximum(j, 0) * t, t), t), :].astype(_BF)
            return tuple(_dot(qm[hh], kb, 1, 1) for hh in range(2))

        def block(j, carry, qk, diag):
            off = pl.multiple_of(j * t, t)
            vb = v_ref[pl.ds(off, t), :].astype(_BF)
            runs, outs = [], []
            for hh in range(2):
                run, o = carry[0][hh], carry[1][hh]
                z = qk[hh]
                l1, lb = _softplus_parts(z)
                if diag:
                    l1 = jnp.where(causal, l1, 0.0)
                between = run + _dot_exact(l1, u_strict, 1, 0, 2)
                a = jnp.exp(lb + between)
                if diag:
                    a = jnp.where(causal, a, 0.0)
                outs.append(o + _dot_exact(a, vb, 1, 0, 2))
                runs.append(run + jnp.sum(l1, axis=-1, keepdims=True))
            return tuple(runs), tuple(outs)

        zero, zero_o = jnp.zeros((t, 1), F32), jnp.zeros((t, 128), F32)
        ahead = scores(i - 1)
        carry = block(i, ((zero, zero), (zero_o, zero_o)), scores(i), True)
        carry = _sb_key_loop(i, carry, lambda j, c, qk: block(j, c, qk, False), scores, ahead)
        o_ref[...] = jnp.where(masks[0], carry[1][0], carry[1][1])

    kv_spec = lambda c0: pl.BlockSpec((s, 128), lambda p, i: (0, c0 // 128 + p))
    return _pcall(
        body, name=name, grid=(SB_HEADS // 2, nblk), out_shape=jax.ShapeDtypeStruct((s, 256), F32),
        in_specs=[pl.BlockSpec((t, 128), lambda p, i: (i, C_QA // 128 + p)), kv_spec(C_KA), kv_spec(C_VA)],
        out_specs=pl.BlockSpec((t, 128), lambda p, i: (i, p)), compiler_params=_cp(PAR, ARB),
    )(proj, proj, proj)


def _sb_bwd(proj, o, dmix, name):
    s = proj.shape[0]
    t = min(ATT_T, s)
    nblk = s // t
    scale = HEAD_DIM ** -0.5

    def body(q_ref, k_ref, v_ref, o_ref, do_ref, dq_ref, dk_ref, dv_ref):
        i = pl.program_id(1)
        masks = _head_masks()
        row = _iota2((t, t), 0)
        col = _iota2((t, t), 1)
        u_strict = (row > col).astype(_BF)
        u_incl = (row >= col).astype(_BF)
        causal = col < row

        @pl.when(i == 0)
        def _():
            dk_ref[...] = jnp.zeros_like(dk_ref)
            dv_ref[...] = jnp.zeros_like(dv_ref)

        qs = q_ref[...] * scale
        dov = do_ref[...]
        ov = o_ref[...]
        qm = [jnp.where(masks[hh], qs, 0.0).astype(_BF) for hh in range(2)]
        dom = [jnp.where(masks[hh], dov, 0.0).astype(_BF) for hh in range(2)]
        dsum = [jnp.sum(dom[hh].astype(F32) * ov, axis=-1, keepdims=True) for hh in range(2)]

        def scores(j):
            off = pl.multiple_of(jnp.maximum(j, 0) * t, t)
            kb = k_ref[pl.ds(off, t), :].astype(_BF)
            vb = v_ref[pl.ds(off, t), :].astype(_BF)
            return tuple((_dot(qm[hh], kb, 1, 1), _dot(dom[hh], vb, 1, 1)) for hh in range(2))

        def block(j, carry, ahead, diag):
            off = pl.multiple_of(j * t, t)
            kb = k_ref[pl.ds(off, t), :].astype(_BF)
            runs, eruns, dqs = [], [], []
            dk = dv = None
            for hh in range(2):
                run, erun, dq = carry[0][hh], carry[1][hh], carry[2][hh]
                z, da = ahead[hh]
                l1, lb = _softplus_parts(z)
                if diag:
                    l1 = jnp.where(causal, l1, 0.0)
                between = run + _dot_exact(l1, u_strict, 1, 0, 2)
                a = jnp.exp(lb + between)
                if diag:
                    a = jnp.where(causal, a, 0.0)
                g = a * da
                cum = dsum[hh] - (erun + _dot_exact(g, u_incl, 1, 0, 2))
                beta = jnp.exp(lb)
                dz = g * (1.0 - beta) - cum * beta
                if diag:
                    dz = jnp.where(causal, dz, 0.0)
                dzb = dz.astype(_BF)
                dqs.append(dq + _dot(dzb, kb, 1, 0))
                dk_h = _dot(dzb, qm[hh], 0, 0)
                dv_h = _dot(a.astype(_BF), dom[hh], 0, 0)
                dk = dk_h if dk is None else dk + dk_h
                dv = dv_h if dv is None else dv + dv_h
                runs.append(run + jnp.sum(l1, axis=-1, keepdims=True))
                eruns.append(erun + jnp.sum(g, axis=-1, keepdims=True))
            dk_ref[pl.ds(off, t), :] += dk
            dv_ref[pl.ds(off, t), :] += dv
            return tuple(runs), tuple(eruns), tuple(dqs)

        zero, zero_q = jnp.zeros((t, 1), F32), jnp.zeros((t, 128), F32)
        ahead = scores(i - 1)
        carry = block(i, ((zero, zero), (zero, zero), (zero_q, zero_q)), scores(i), True)
        carry = _sb_key_loop(i, carry, lambda j, c, a: block(j, c, a, False), scores, ahead)
        dq_ref[...] = jnp.where(masks[0], carry[2][0], carry[2][1]) * scale

    kv_spec = lambda c0: pl.BlockSpec((s, 128), lambda p, i: (0, c0 // 128 + p))
    blk = lambda c0: pl.BlockSpec((t, 128), lambda p, i: (i, c0 // 128 + p))
    acc = pl.BlockSpec((s, 128), lambda p, i: (0, p))
    shp = jax.ShapeDtypeStruct((s, 256), F32)
    return _pcall(
        body, name=name, grid=(SB_HEADS // 2, nblk), out_shape=[shp, shp, shp],
        in_specs=[blk(C_QA), kv_spec(C_KA), kv_spec(C_VA), blk(0), blk(M_SB)],
        out_specs=[blk(0), acc, acc], compiler_params=_cp(PAR, ARB),
    )(proj, proj, proj, o, dmix)


def _group_matrix(width):
    r = _iota2((width, width), 0) // HEAD_DIM
    c = _iota2((width, width), 1) // HEAD_DIM
    return (r == c).astype(_BF)


def _group_mean(v, gm):
    return _dot_exact(v, gm, 1, 0, 2) * (1.0 / HEAD_DIM)


def _fox_prep(proj, gq, gk, bf, name):
    s = proj.shape[0]
    t = min(ATT_T, s)
    nblk = s // t

    def body(q_ref, k_ref, fl_ref, gq_ref, gk_ref, bf_ref, qn_ref, kn_ref, cfc_ref, cfr_ref, carry_ref):
        i = pl.program_id(0)
        gm = _group_matrix(512)
        for src, gref, dst in ((q_ref, gq_ref, qn_ref), (k_ref, gk_ref, kn_ref)):
            v = src[...]
            rstd = lax.rsqrt(_group_mean(v * v, gm) + EPS)
            dst[...] = (v * rstd * gref[...]).astype(dst.dtype)

        @pl.when(i == 0)
        def _():
            carry_ref[...] = jnp.zeros_like(carry_ref)

        lane = _iota2((1, 128), 1)
        logf = jnp.where(lane < FOX_HEADS, _softplus_parts(-(fl_ref[...] + bf_ref[...]))[0], 0.0)
        lower = (_iota2((t, t), 0) >= _iota2((t, t), 1)).astype(_BF)
        cf = carry_ref[...] + _dot_exact(logf, lower, 1, 0, 3, v_left=False)
        cfc_ref[...] = cf
        cfr_ref[...] = cf.T[:FOX_HEADS, :]
        carry_ref[...] = cf[t - 1:t, :]

    col = lambda w, c0: pl.BlockSpec((t, w), lambda i: (i, c0 // w))
    v512 = pl.BlockSpec((1, 512), lambda i: (0, 0))
    return _pcall(
        body, name=name, grid=(nblk,),
        out_shape=[jax.ShapeDtypeStruct((s, 512), _BF), jax.ShapeDtypeStruct((s, 512), _BF),
                   jax.ShapeDtypeStruct((s, 128), F32), jax.ShapeDtypeStruct((FOX_HEADS, s), F32)],
        in_specs=[col(512, C_QB), col(512, C_KB), col(128, C_FL), v512, v512, pl.BlockSpec((1, 128), lambda i: (0, 0))],
        out_specs=[col(512, 0), col(512, 0), col(128, 0), pl.BlockSpec((FOX_HEADS, t), lambda i: (0, i))],
        scratch_shapes=[pltpu.VMEM((1, 128), F32)], compiler_params=_cp(ARB),
    )(proj, proj, proj, gq, gk, bf)


def _fox_cq(cfc_blk, head):
    lane = _iota2((1, 128), 1)
    return jnp.sum(jnp.where(lane == head, cfc_blk, 0.0), axis=-1, keepdims=True)


def _fox_ck(cfr_ref, off, t, head):
    sub = _iota2((FOX_HEADS, 1), 0)
    return jnp.sum(jnp.where(sub == head, cfr_ref[:, pl.ds(off, t)], 0.0), axis=0, keepdims=True)


def _fox_last_key(cfr_ref, j, t, head):
    return jnp.min(_fox_ck(cfr_ref, pl.multiple_of(j * t, t), t, head))


def _grid_ends(n_outer, n_inner):
    first = lambda: jnp.logical_and(pl.program_id(0) == 0, pl.program_id(1) == 0)
    last = lambda: jnp.logical_and(pl.program_id(0) == n_outer - 1, pl.program_id(1) == n_inner - 1)
    return first, last


def _fox_fwd(proj, qn, kn, cfc, cfr, qk_bound, name, comm=None):
    s = proj.shape[0]
    t = min(ATT_T, s)
    nblk = s // t
    scale = HEAD_DIM ** -0.5

    def body(q_ref, k_ref, v_ref, cfc_ref, cfr_ref, qkb_ref, o_ref, lse_ref):
        p = pl.program_id(0)
        i = pl.program_id(1)
        masks = _head_masks()
        lane = _iota2((1, 128), 1)
        valid = _iota2((t, t), 1) <= _iota2((t, t), 0)
        qv = q_ref[...]
        cfc_blk = cfc_ref[...]
        heads = [2 * p + hh for hh in range(2)]
        qm = [(jnp.where(masks[hh], qv, 0) * scale).astype(_BF) for hh in range(2)]
        cq = [_fox_cq(cfc_blk, heads[hh]) for hh in range(2)]

        def scores(j):
            kb = k_ref[pl.ds(pl.multiple_of(jnp.maximum(j, 0) * t, t), t), :]
            return tuple(_dot(qm[hh], kb, 1, 1) for hh in range(2))

        def block(j, carry, qk, diag):
            off = pl.multiple_of(j * t, t)
            vb = v_ref[pl.ds(off, t), :].astype(_BF)
            ms, ls, accs = [], [], []
            for hh in range(2):
                m, l, acc = carry[0][hh], carry[1][hh], carry[2][hh]
                z = qk[hh] + (cq[hh] - _fox_ck(cfr_ref, off, t, heads[hh]))
                if diag:
                    z = jnp.where(valid, z, NEG)
                m_new = jnp.maximum(m, jnp.max(z, axis=-1, keepdims=True))
                pe = jnp.exp(z - m_new)
                alpha = jnp.exp(m - m_new)
                ms.append(m_new)
                ls.append(alpha * l + jnp.sum(pe, axis=-1, keepdims=True))
                accs.append(alpha * acc + _dot_exact(pe, vb, 1, 0, 2))
            return tuple(ms), tuple(ls), tuple(accs)

        def cond(c):
            j = jnp.maximum(i - c[0], 0)
            reach = [jnp.max(qkb_ref[...] + cq[hh] - c[1][0][hh]) - _fox_last_key(cfr_ref, j, t, heads[hh])
                     for hh in range(2)]
            return jnp.logical_and(c[0] <= i, jnp.maximum(reach[0], reach[1]) > UNDERFLOW)

        def step(c):
            ahead = scores(i - c[0] - 1)
            return c[0] + 1, block(i - c[0], c[1], c[2], False), ahead

        neg, zero, zero_o = jnp.full((t, 1), NEG, F32), jnp.zeros((t, 1), F32), jnp.zeros((t, 128), F32)
        ahead = scores(i - 1)
        carry = block(i, ((neg, neg), (zero, zero), (zero_o, zero_o)), scores(i), True)
        m, l, acc = lax.while_loop(cond, step, (jnp.int32(1), carry, ahead))[1]
        o_ref[...] = jnp.where(masks[0], acc[0] / l[0], acc[1] / l[1])
        lse_ref[...] = jnp.where(lane == 0, m[0] + jnp.log(l[0]), jnp.where(lane == 1, m[1] + jnp.log(l[1]), 0.0))

    c_operands, c_in_specs, c_out_shapes, c_out_specs, c_sems = _comm_operands(comm) if comm else ([], [], [], [], [])
    outs = _pcall(
        _with_comm(body, comm, 6, 2, *_grid_ends(FOX_HEADS // 2, nblk)), name=name, grid=(FOX_HEADS // 2, nblk),
        out_shape=[jax.ShapeDtypeStruct((s, 512), F32), jax.ShapeDtypeStruct((FOX_HEADS // 2, s, 128), F32)] + c_out_shapes,
        in_specs=[pl.BlockSpec((t, 128), lambda p, i: (i, p)), pl.BlockSpec((s, 128), lambda p, i: (0, p)),
                  pl.BlockSpec((s, 128), lambda p, i: (0, C_VB // 128 + p)),
                  pl.BlockSpec((t, 128), lambda p, i: (i, 0)), pl.BlockSpec((FOX_HEADS, s), lambda p, i: (0, 0)),
                  pl.BlockSpec((1, 1), lambda p, i: (0, 0))] + c_in_specs,
        out_specs=[pl.BlockSpec((t, 128), lambda p, i: (i, p)),
                   pl.BlockSpec((None, t, 128), lambda p, i: (p, i, 0))] + c_out_specs,
        scratch_shapes=c_sems, input_output_aliases=_comm_aliases(comm, 6, 2),
        compiler_params=_cp(ARB, ARB),
    )(qn, kn, proj, cfc, cfr, qk_bound, *c_operands)
    return outs[0], outs[1], list(outs[2:])


def _fox_bwd(proj, qn, kn, cfc, cfr, qk_bound, o, lse, dmix, name, comm=None):
    s = proj.shape[0]
    t = min(ATT_T, s)
    nblk = s // t
    scale = HEAD_DIM ** -0.5

    def body(q_ref, k_ref, v_ref, cfc_ref, cfr_ref, qkb_ref, o_ref, lse_ref, do_ref, dq_ref, dk_ref, dv_ref, dcr_ref):
        p = pl.program_id(0)
        i = pl.program_id(1)
        masks = _head_masks()
        lane = _iota2((1, 128), 1)
        valid = _iota2((t, t), 1) <= _iota2((t, t), 0)
        before = (_iota2((t, t), 0) < _iota2((t, t), 1)).astype(_BF)

        @pl.when(i == 0)
        def _():
            dk_ref[...] = jnp.zeros_like(dk_ref)
            dv_ref[...] = jnp.zeros_like(dv_ref)
            dcr_ref[...] = jnp.zeros_like(dcr_ref)

        qv = q_ref[...]
        cfc_blk = cfc_ref[...]
        dov = do_ref[...]
        ov = o_ref[...]
        lse_blk = lse_ref[...]
        heads = [2 * p + hh for hh in range(2)]
        qm = [(jnp.where(masks[hh], qv, 0) * scale).astype(_BF) for hh in range(2)]
        dom = [jnp.where(masks[hh], dov, 0.0).astype(_BF) for hh in range(2)]
        dsum = [jnp.sum(dom[hh].astype(F32) * ov, axis=-1, keepdims=True) for hh in range(2)]
        lse_h = [jnp.sum(jnp.where(lane == hh, lse_blk, 0.0), axis=-1, keepdims=True) for hh in range(2)]
        cq = [_fox_cq(cfc_blk, heads[hh]) for hh in range(2)]
        reach = [jnp.max(qkb_ref[...] + cq[hh] - lse_h[hh]) for hh in range(2)]

        def matters(j):
            jm = jnp.maximum(j - 1, 0)
            far = [reach[hh] - _fox_last_key(cfr_ref, jm, t, heads[hh]) for hh in range(2)]
            return jnp.logical_and(j > 0, jnp.maximum(far[0], far[1]) > UNDERFLOW)

        first = lax.while_loop(matters, lambda j: j - 1, i)

        def scores(j):
            off = pl.multiple_of(j * t, t)
            kb = k_ref[pl.ds(off, t), :]
            vb = v_ref[pl.ds(off, t), :].astype(_BF)
            return tuple((_dot(qm[hh], kb, 1, 1), _dot(dom[hh], vb, 1, 1)) for hh in range(2))

        def block(j, carry, ahead, diag):
            off = pl.multiple_of(j * t, t)
            kb = k_ref[pl.ds(off, t), :]
            dqs, pres = [], []
            dk = dv = None
            for hh in range(2):
                dq, pre = carry[0][hh], carry[1][hh]
                z = ahead[hh][0] + (cq[hh] - _fox_ck(cfr_ref, off, t, heads[hh]))
                pm = jnp.exp(z - lse_h[hh])
                if diag:
                    pm = jnp.where(valid, pm, 0.0)
                ds = pm * (ahead[hh][1] - dsum[hh])
                dsb = ds.astype(_BF)
                dqs.append(dq + _dot(dsb, kb, 1, 0))
                dk_h = _dot(dsb, qm[hh], 0, 0)
                dv_h = _dot(pm.astype(_BF), dom[hh], 0, 0)
                dk = dk_h if dk is None else dk + dk_h
                dv = dv_h if dv is None else dv + dv_h
                if diag:
                    lower_keys = jnp.where(valid, pre + _dot_exact(ds, before, 1, 0, 2), 0.0)
                    dlogf = jnp.sum(lower_keys, axis=0, keepdims=True)
                else:
                    dlogf = jnp.sum(pre) + _dot_exact(jnp.sum(ds, axis=0, keepdims=True), before, 1, 0, 3)
                dcr_ref[hh:hh + 1, pl.ds(off, t)] += dlogf
                pres.append(pre + jnp.sum(ds, axis=-1, keepdims=True))
            dk_ref[pl.ds(off, t), :] += dk
            dv_ref[pl.ds(off, t), :] += dv
            return tuple(dqs), tuple(pres)

        zero, zero_q = jnp.zeros((t, 1), F32), jnp.zeros((t, 128), F32)


        def step(j, c):
            ahead = scores(j + 1)
            return block(j, c[0], c[1], False), ahead

        carry, ahead = lax.fori_loop(first, i, step, (((zero_q, zero_q), (zero, zero)), scores(first)))
        dqs, _ = block(i, carry, ahead, True)
        dq_ref[...] = jnp.where(masks[0], dqs[0], dqs[1]) * scale

    blk = lambda c0: pl.BlockSpec((t, 128), lambda p, i: (i, c0 // 128 + p))
    res = lambda c0: pl.BlockSpec((s, 128), lambda p, i: (0, c0 // 128 + p))
    shp = jax.ShapeDtypeStruct((s, 512), F32)
    c_operands, c_in_specs, c_out_shapes, c_out_specs, c_sems = _comm_operands(comm) if comm else ([], [], [], [], [])
    outs = _pcall(
        _with_comm(body, comm, 9, 4, *_grid_ends(FOX_HEADS // 2, nblk)), name=name, grid=(FOX_HEADS // 2, nblk),
        out_shape=[shp, shp, shp, jax.ShapeDtypeStruct((FOX_HEADS // 2, 8, s), F32)] + c_out_shapes,
        in_specs=[blk(0), res(0), res(C_VB), pl.BlockSpec((t, 128), lambda p, i: (i, 0)),
                  pl.BlockSpec((FOX_HEADS, s), lambda p, i: (0, 0)), pl.BlockSpec((1, 1), lambda p, i: (0, 0)), blk(0),
                  pl.BlockSpec((None, t, 128), lambda p, i: (p, i, 0)), blk(M_FOX)] + c_in_specs,
        out_specs=[blk(0), res(0), res(0), pl.BlockSpec((None, 8, s), lambda p, i: (p, 0, 0))] + c_out_specs,
        scratch_shapes=c_sems, input_output_aliases=_comm_aliases(comm, 9, 4),
        compiler_params=_cp(ARB, ARB),
    )(qn, kn, proj, cfc, cfr, qk_bound, o, lse, dmix, *c_operands)
    return outs[0], outs[1], outs[2], outs[3], list(outs[4:])


def _fox_prep_bwd(proj, gq, gk, bf, dqn, dkn, dlogf, name):
    s = proj.shape[0]
    t = min(ATT_T, s)
    nblk = s // t

    def body(q_ref, k_ref, fl_ref, gq_ref, gk_ref, bf_ref, dqn_ref, dkn_ref, dlogf_ref,
             dq_ref, dk_ref, dfl_ref, dgq_ref, dgk_ref, dbf_ref):
        i = pl.program_id(0)
        gm = _group_matrix(512)

        @pl.when(i == 0)
        def _():
            dgq_ref[...] = jnp.zeros_like(dgq_ref)
            dgk_ref[...] = jnp.zeros_like(dgk_ref)
            dbf_ref[...] = jnp.zeros_like(dbf_ref)

        for src, gref, dyref, dst, dgref in ((q_ref, gq_ref, dqn_ref, dq_ref, dgq_ref),
                                              (k_ref, gk_ref, dkn_ref, dk_ref, dgk_ref)):
            v = src[...]
            dy = dyref[...]
            rstd = lax.rsqrt(_group_mean(v * v, gm) + EPS)
            xhat = v * rstd
            dgref[...] += jnp.sum(dy * xhat, axis=0, keepdims=True)
            dxh = dy * gref[...]
            dst[...] = (rstd * (dxh - xhat * _group_mean(dxh * xhat, gm))).astype(dst.dtype)

        xv = fl_ref[...] + bf_ref[...]
        e = jnp.exp(-jnp.abs(xv))
        dfl = dlogf_ref[...] * (jnp.where(xv >= 0.0, e, 1.0) / (1.0 + e))
        dfl_ref[...] = dfl.astype(dfl_ref.dtype)
        dbf_ref[...] += jnp.sum(dfl, axis=0, keepdims=True)

    col = lambda w, c0: pl.BlockSpec((t, w), lambda i: (i, c0 // w))
    v512 = pl.BlockSpec((1, 512), lambda i: (0, 0))
    v128 = pl.BlockSpec((1, 128), lambda i: (0, 0))
    return _pcall(
        body, name=name, grid=(nblk,),
        out_shape=[jax.ShapeDtypeStruct((s, 512), _BF), jax.ShapeDtypeStruct((s, 512), _BF),
                   jax.ShapeDtypeStruct((s, 128), _BF), jax.ShapeDtypeStruct((1, 512), F32),
                   jax.ShapeDtypeStruct((1, 512), F32), jax.ShapeDtypeStruct((1, 128), F32)],
        in_specs=[col(512, C_QB), col(512, C_KB), col(128, C_FL), v512, v512, v128, col(512, 0), col(512, 0), col(128, 0)],
        out_specs=[col(512, 0), col(512, 0), col(128, 0), v512, v512, v128], compiler_params=_cp(ARB),
    )(proj, proj, proj, gq, gk, bf, dqn, dkn, dlogf)


GELU_C = 0.7978845608028654
GELU_A = 0.044715


def _gelu(x):
    return 0.5 * x * (1.0 + jnp.tanh(GELU_C * (x + GELU_A * x * x * x)))


def _gelu_grad(x):
    th = jnp.tanh(GELU_C * (x + GELU_A * x * x * x))
    return 0.5 * (1.0 + th) + 0.5 * x * (1.0 - th * th) * (GELU_C * (1.0 + 3.0 * GELU_A * x * x))


def _sgu_tril():
    return _iota2((SGU_CHUNK, SGU_CHUNK), 0) >= _iota2((SGU_CHUNK, SGU_CHUNK), 1)


def _sgu_group_masks():
    lane = _iota2((1, 256), 1)
    return [(lane // HEAD_DIM) == g for g in range(SGU_GROUPS)]


def _sgu_fwd(proj, w, gs, bexp, name):
    s = proj.shape[0]
    t = min(2 * SGU_CHUNK, s)

    def body(u_ref, v_ref, w_ref, gs_ref, b_ref, o_ref):
        gm = _group_matrix(256)
        gmask = _sgu_group_masks()
        tril = _sgu_tril()
        u = _gelu(u_ref[...])
        vg = _gelu(v_ref[...])
        vhat = (vg * lax.rsqrt(_group_mean(vg * vg, gm) + EPS) * gs_ref[...]).astype(_BF)
        for ch in range(t // SGU_CHUNK):
            rows = slice(ch * SGU_CHUNK, (ch + 1) * SGU_CHUNK)
            mixed = b_ref[...]
            for g in range(SGU_GROUPS):
                wg = jnp.where(tril, w_ref[g], 0.0).astype(_BF)
                mixed = jnp.where(gmask[g], mixed + _dot(wg, vhat[rows], 1, 0), mixed)
            o_ref[rows, :] = u[rows] * mixed

    return _pcall(
        body, name=name, grid=(s // t,), out_shape=jax.ShapeDtypeStruct((s, 256), F32),
        in_specs=[_row_spec(t, 256, C_UC // 256), _row_spec(t, 256, C_VC // 256),
                  pl.BlockSpec((SGU_GROUPS, SGU_CHUNK, SGU_CHUNK), lambda i: (0, 0, 0)), _vec_spec(256),
                  pl.BlockSpec((SGU_CHUNK, 256), lambda i: (0, 0))],
        out_specs=_row_spec(t, 256), compiler_params=_cp(PAR),
    )(proj, proj, w, gs, bexp)


def _sgu_bwd(proj, w, gs, bexp, dmix, name):
    s = proj.shape[0]
    t = min(2 * SGU_CHUNK, s)
    nstep = s // t

    def body(u_ref, v_ref, w_ref, gs_ref, b_ref, do_ref, du_ref, dv_ref, dw_ref, db_ref, dgs_ref):
        i = pl.program_id(0)
        gm = _group_matrix(256)
        gmask = _sgu_group_masks()
        tril = _sgu_tril()

        @pl.when(i == 0)
        def _():
            dw_ref[...] = jnp.zeros_like(dw_ref)
            db_ref[...] = jnp.zeros_like(db_ref)
            dgs_ref[...] = jnp.zeros_like(dgs_ref)

        uc = u_ref[...]
        vc = v_ref[...]
        u = _gelu(uc)
        vg = _gelu(vc)
        rstd = lax.rsqrt(_group_mean(vg * vg, gm) + EPS)
        xh = vg * rstd
        gsv = gs_ref[...]
        vhat = (xh * gsv).astype(_BF)
        dov = do_ref[...]
        dm = dov * u
        for ch in range(t // SGU_CHUNK):
            rows = slice(ch * SGU_CHUNK, (ch + 1) * SGU_CHUNK)
            mixed = b_ref[...]
            dvh = jnp.zeros((SGU_CHUNK, 256), F32)
            dmc = dm[rows]
            for g in range(SGU_GROUPS):
                wg = jnp.where(tril, w_ref[g], 0.0).astype(_BF)
                mixed = jnp.where(gmask[g], mixed + _dot(wg, vhat[rows], 1, 0), mixed)
                dvh = jnp.where(gmask[g], _dot(wg, dmc.astype(_BF), 0, 0), dvh)
                dw_ref[g] += _dot(jnp.where(gmask[g], dmc, 0.0).astype(_BF), vhat[rows], 1, 1)
            db_ref[...] += dmc
            du_ref[rows, :] = (dov[rows] * mixed * _gelu_grad(uc[rows])).astype(du_ref.dtype)
            xhc = xh[rows]
            dgs_ref[...] += jnp.sum(dvh * xhc, axis=0, keepdims=True)
            dxh = dvh * gsv
            dvg = rstd[rows] * (dxh - xhc * _group_mean(dxh * xhc, gm))
            dv_ref[rows, :] = (dvg * _gelu_grad(vc[rows])).astype(dv_ref.dtype)

        @pl.when(i == nstep - 1)
        def _():
            for g in range(SGU_GROUPS):
                dw_ref[g] = jnp.where(tril, dw_ref[g], 0.0)

    wspec = pl.BlockSpec((SGU_GROUPS, SGU_CHUNK, SGU_CHUNK), lambda i: (0, 0, 0))
    bspec = pl.BlockSpec((SGU_CHUNK, 256), lambda i: (0, 0))
    return _pcall(
        body, name=name, grid=(nstep,),
        out_shape=[jax.ShapeDtypeStruct((s, 256), _BF), jax.ShapeDtypeStruct((s, 256), _BF),
                   jax.ShapeDtypeStruct((SGU_GROUPS, SGU_CHUNK, SGU_CHUNK), F32),
                   jax.ShapeDtypeStruct((SGU_CHUNK, 256), F32), jax.ShapeDtypeStruct((1, 256), F32)],
        in_specs=[_row_spec(t, 256, C_UC // 256), _row_spec(t, 256, C_VC // 256), wspec, _vec_spec(256), bspec,
                  _row_spec(t, 256, M_SGU // 256)],
        out_specs=[_row_spec(t, 256), _row_spec(t, 256), wspec, bspec, _vec_spec(256)],
        compiler_params=_cp(ARB),
    )(proj, proj, w, gs, bexp, dmix)


def _ada_fwd(c_all, ada_w, name):
    depth, d, n = ada_w.shape

    def body(c_ref, w_ref, o_ref):
        cv = c_ref[...]
        cond = cv / (1.0 + jnp.exp(-cv))
        o_ref[...] = _dot_f32(cond, w_ref[...], 1, 0)

    return _pcall(
        body, name=name, grid=(depth,), out_shape=jax.ShapeDtypeStruct((depth, N_DEV, n), F32),
        in_specs=[pl.BlockSpec((N_DEV, d), lambda l: (0, 0)), pl.BlockSpec((None, d, n), lambda l: (l, 0, 0))],
        out_specs=pl.BlockSpec((None, N_DEV, n), lambda l: (l, 0, 0)), compiler_params=_cp(PAR),
    )(c_all, ada_w)


def _ada_bwd(c_all, dmod, name):
    depth, _, n = dmod.shape
    d = c_all.shape[1]

    def body(c_ref, dm_ref, o_ref):
        cv = c_ref[...]
        cond = cv / (1.0 + jnp.exp(-cv))
        o_ref[...] = _dot_f32(cond, dm_ref[...], 0, 0)

    return _pcall(
        body, name=name, grid=(depth,), out_shape=jax.ShapeDtypeStruct((depth, d, n), F32),
        in_specs=[pl.BlockSpec((N_DEV, d), lambda l: (0, 0)), pl.BlockSpec((None, N_DEV, n), lambda l: (l, 0, 0))],
        out_specs=pl.BlockSpec((None, d, n), lambda l: (l, 0, 0)), compiler_params=_cp(PAR),
    )(c_all, dmod)


def _adamw(slots, w, m, v, name):
    n, r, c = slots.shape
    tr = 256 if r % 256 == 0 else r
    bc1 = 1.0 - ADAM_B1 ** ADAM_STEP
    bc2 = 1.0 - ADAM_B2 ** ADAM_STEP

    def body(s_ref, w_ref, m_ref, v_ref, g_ref, d_ref, nm_ref, nv_ref):
        g = s_ref[0].astype(F32)
        for j in range(1, n):
            g = g + s_ref[j].astype(F32)
        m_new = ADAM_B1 * m_ref[...] + (1.0 - ADAM_B1) * g
        v_new = ADAM_B2 * v_ref[...] + (1.0 - ADAM_B2) * (g * g)
        g_ref[...] = g
        nm_ref[...] = m_new
        nv_ref[...] = v_new
        d_ref[...] = -ADAM_LR * ((m_new / bc1) / (jnp.sqrt(v_new / bc2) + ADAM_EPS) + ADAM_WD * w_ref[...])

    tile = pl.BlockSpec((tr, c), lambda i: (i, 0))
    shp = jax.ShapeDtypeStruct((r, c), F32)
    return _pcall(
        body, name=name, grid=(r // tr,), out_shape=[shp] * 4,
        in_specs=[pl.BlockSpec((n, tr, c), lambda i: (0, i, 0)), tile, tile, tile],
        out_specs=[tile] * 4, compiler_params=_cp(PAR),
    )(slots, w, m, v)


def _pad_cols(a, width):
    return jnp.pad(a, ((0, 0), (0, width - a.shape[1])))


def _w_in_layout(w):
    pad = jnp.zeros(w.shape[:-1] + (IN_P - IN_W,), w.dtype)
    return jnp.concatenate([w[..., 768:FL_SRC], w[..., :768], w[..., UC_SRC:], w[..., FL_SRC:UC_SRC], pad], axis=-1)


def _w_in_unlayout(g):
    return jnp.concatenate([g[..., C_QA:C_UC], g[..., :C_QA], g[..., C_FL:C_FL + FOX_HEADS], g[..., C_UC:C_FL]], axis=-1)


SMALL = [("ada_b", DEPTH * 6 * D_MODEL), ("norm1_g", DEPTH * D_MODEL), ("norm2_g", DEPTH * D_MODEL),
         ("sgu_w", DEPTH * SGU_GROUPS * SGU_CHUNK * SGU_CHUNK), ("sgu_b", DEPTH * SGU_GROUPS * SGU_CHUNK),
         ("sgu_norm_g", DEPTH * SGU_GROUPS * HEAD_DIM), ("q_norm_g", DEPTH * HEAD_DIM), ("k_norm_g", DEPTH * HEAD_DIM),
         ("b_forget", DEPTH * FOX_HEADS), ("loss", 1)]
SMALL_ROWS = 2560


def _pack_small(parts):
    flat = jnp.concatenate([parts[name].reshape(-1).astype(F32) for name, _ in SMALL])
    return jnp.pad(flat, (0, SMALL_ROWS * 128 - flat.shape[0])).reshape(SMALL_ROWS, 128)


def _unpack_small(packed, shapes):
    flat = packed.reshape(-1)
    out, off = {}, 0
    for name, size in SMALL:
        out[name] = flat[off:off + size].reshape(shapes[name])
        off += size
    return out


def kernel(x, c, ada_w, ada_b, norm1_g, norm2_g, w_in, b_forget, q_norm_g, k_norm_g, sgu_norm_g, sgu_w, sgu_b, w_out, mlp_w1, mlp_w2, loss_target, m_ada_w, m_ada_b, m_norm1_g, m_norm2_g, m_w_in, m_b_forget, m_q_norm_g, m_k_norm_g, m_sgu_norm_g, m_sgu_w, m_sgu_b, m_w_out, m_mlp_w1, m_mlp_w2, v_ada_w, v_ada_b, v_norm1_g, v_norm2_g, v_w_in, v_b_forget, v_q_norm_g, v_k_norm_g, v_sgu_norm_g, v_sgu_w, v_sgu_b, v_w_out, v_mlp_w1, v_mlp_w2):
    me = _lin(_my_pos())
    x0 = x[0]
    target = loss_target[0]
    n_ada = ada_w.shape[2]

    shards = [w.astype(_XBF) for w in (w_in, w_out, mlp_w1, mlp_w2)]

    def whole(w_in_g, w_out_g, w1_g, w2_g):
        return (_w_in_layout(jnp.transpose(w_in_g, (1, 0, 2)).reshape(D_MODEL, IN_W)),
                w_out_g.reshape(D_MODEL, D_MODEL),
                jnp.transpose(w1_g, (1, 0, 2)).reshape(D_MODEL, D_FF),
                w2_g.reshape(D_FF, D_MODEL))

    *first, c_all = _exchange([w[0] for w in shards] + [c], False, "gather_weights")
    c_all = c_all.reshape(N_DEV, D_MODEL)
    weights = [whole(*first)]

    mod_part = _ada_fwd(c_all, ada_w, "ada_fwd")
    (mod_rows,) = _exchange([jnp.transpose(mod_part, (1, 0, 2))], True, "scatter_mod")
    mod = jnp.transpose(mod_rows, (1, 0, 2)).reshape(DEPTH, 6 * D_MODEL) + ada_b
    mods = mod.reshape(DEPTH, 6, 1, D_MODEL)

    saved = []
    xl = x0
    for l in range(DEPTH):
        sh1, sc1, g1, sh2, sc2, g2 = (mods[l, r] for r in range(6))
        n1 = norm1_g[l].reshape(1, D_MODEL)
        n2 = norm2_g[l].reshape(1, D_MODEL)
        gq = jnp.tile(q_norm_g[l], FOX_HEADS).reshape(1, 512)
        gk = jnp.tile(k_norm_g[l], FOX_HEADS).reshape(1, 512)
        bf = jnp.pad(b_forget[l], (0, 128 - FOX_HEADS)).reshape(1, 128)
        gs = sgu_norm_g[l].reshape(1, 256)
        bexp = jnp.repeat(sgu_b[l].T, HEAD_DIM, axis=1)

        w_in_l, w_out_l, w1_l, w2_l = weights[l]
        h1 = _norm_mod(xl, n1, sc1, sh1, "norm_mod")
        proj = _mm_nn(h1, w_in_l, "plain", (), "mm_in")
        o_sb = _sb_fwd(proj, "sb_fwd")
        qn, kn, cfc, cfr = _fox_prep(proj, gq, gk, bf, "fox_prep")
        qkb = (1.02 * HEAD_DIM ** 0.5 * jnp.max(jnp.abs(q_norm_g[l])) * jnp.max(jnp.abs(k_norm_g[l]))).reshape(1, 1)
        gather_next = None
        if l + 1 < DEPTH:
            gather_next = dict(arrays=[w[l + 1] for w in shards], scatter=False, recv=None, slot=None)
        o_fox, lse, gathered = _fox_fwd(proj, qn, kn, cfc, cfr, qkb, "fox_fwd", gather_next)
        if gathered:
            weights.append(whole(*gathered))
        o_sgu = _sgu_fwd(proj, sgu_w[l], gs, bexp, "sgu_fwd")
        mixed = jnp.concatenate([o_sb, o_fox, o_sgu], axis=-1).astype(_BF)
        x_mid, y1 = _mm_nn(mixed, w_out_l, "resid", (xl, g1), "mm_out")
        h2 = _norm_mod(x_mid, n2, sc2, sh2, "norm_mod")
        r2, r1 = _mm_nn(h2, w1_l, "relu2", (), "mm_w1")
        x_out, y2 = _mm_nn(r2, w2_l, "resid", (x_mid, g2), "mm_w2")
        saved.append(dict(x_in=xl, h1=h1, proj=proj, o_sb=o_sb, qn=qn, kn=kn, cfc=cfc, cfr=cfr, o_fox=o_fox, lse=lse,
                          mixed=mixed, x_mid=x_mid, y1=y1, h2=h2, r2=r2, r1=r1, y2=y2,
                          n1=n1, n2=n2, gq=gq, gk=gk, bf=bf, gs=gs, bexp=bexp, qkb=qkb))
        xl = x_out

    loss_part, dx = _loss_head(xl, target, "loss_head")

    grads_ready = None
    received = [jnp.zeros((N_DEV,) + w.shape, _XBF) for w in shards]
    small = {k: [None] * DEPTH for k in ("mod", "norm1_g", "norm2_g", "sgu_w", "sgu_b", "sgu_norm_g",
                                         "q_norm_g", "k_norm_g", "b_forget")}
    for l in reversed(range(DEPTH)):
        sv = saved[l]
        sh1, sc1, g1, sh2, sc2, g2 = (mods[l, r] for r in range(6))
        w_in_l, w_out_l, w1_l, w2_l = weights[l]
        dy2, dg2 = _res_bwd(dx, sv["y2"], g2, "res_bwd")
        da = _mm_nt(dy2, w2_l, "mul2", (sv["r1"],), _BF, "mm_w2_bwd")
        g_w2 = _mm_tn(sv["r2"], dy2, "rows", "mm_w2_grad")
        dh2 = _mm_nt(da, w1_l, "plain", (), F32, "mm_w1_bwd")
        g_w1 = _mm_tn(sv["h2"], da, "cols", "mm_w1_grad")
        dx_mid, dsh2, dsc2, dn2 = _norm_mod_bwd(sv["x_mid"], dh2, dx, sv["n2"], sc2, "norm_mod_bwd")
        dy1, dg1 = _res_bwd(dx_mid, sv["y1"], g1, "res_bwd")
        dmix = _mm_nt(dy1, w_out_l, "plain", (), F32, "mm_out_bwd")
        g_out = _mm_tn(sv["mixed"], dy1, "rows", "mm_out_grad")
        dq_sb, dk_sb, dv_sb = _sb_bwd(sv["proj"], sv["o_sb"], dmix, "sb_bwd")
        send_prev = None
        if grads_ready is not None:
            send_prev = dict(arrays=grads_ready, scatter=True, recv=received, slot=l + 1)
        dqn, dkn, dv_fox, dcr, filled = _fox_bwd(sv["proj"], sv["qn"], sv["kn"], sv["cfc"], sv["cfr"], sv["qkb"],
                                                 sv["o_fox"], sv["lse"], dmix, "fox_bwd", send_prev)
        if filled:
            received = filled
        dlogf = _pad_cols(dcr[:, :2, :].reshape(FOX_HEADS, -1).T, 128)
        dq_fox, dk_fox, dfl, dgq, dgk, dbf = _fox_prep_bwd(sv["proj"], sv["gq"], sv["gk"], sv["bf"], dqn, dkn, dlogf,
                                                           "fox_prep_bwd")
        duc, dvc, dsw, dsb, dsg = _sgu_bwd(sv["proj"], sgu_w[l], sv["gs"], sv["bexp"], dmix, "sgu_bwd")
        dproj = jnp.concatenate(
            [dq_fox, dk_fox, dv_fox.astype(_BF), dq_sb.astype(_BF), dk_sb.astype(_BF), dv_sb.astype(_BF), duc, dvc, dfl,
             jnp.zeros((dfl.shape[0], IN_P - C_FL - 128), _BF)], axis=-1)
        dh1 = _mm_nt(dproj, w_in_l, "plain", (), F32, "mm_in_bwd")
        g_in = _w_in_unlayout(_mm_tn(sv["h1"], dproj, None, "mm_in_grad"))
        g_in = jnp.transpose(g_in.reshape(D_MODEL, N_DEV, IN_W // N_DEV), (1, 0, 2))
        grads_ready = [g_in, g_out, g_w1, g_w2]
        dx, dsh1, dsc1, dn1 = _norm_mod_bwd(sv["x_in"], dh1, dx_mid, sv["n1"], sc1, "norm_mod_bwd")

        small["mod"][l] = jnp.concatenate([dsh1, dsc1, dg1, dsh2, dsc2, dg2], axis=-1).reshape(-1)
        small["norm1_g"][l] = dn1.reshape(-1)
        small["norm2_g"][l] = dn2.reshape(-1)
        small["sgu_w"][l] = dsw
        small["sgu_b"][l] = dsb.reshape(SGU_CHUNK, SGU_GROUPS, HEAD_DIM).sum(-1).T
        small["sgu_norm_g"][l] = dsg.reshape(SGU_GROUPS, HEAD_DIM)
        small["q_norm_g"][l] = dgq.reshape(FOX_HEADS, HEAD_DIM).sum(0)
        small["k_norm_g"][l] = dgk.reshape(FOX_HEADS, HEAD_DIM).sum(0)
        small["b_forget"][l] = dbf[0, :FOX_HEADS]

    parts = {k: jnp.stack(v) for k, v in small.items()}
    parts["ada_b"] = parts.pop("mod")
    parts["loss"] = loss_part
    (small_all,) = _exchange([_pack_small(parts)], False, "gather_small")
    zero1 = jnp.zeros((1,), F32)
    packs = [_pack_small(dict(ada_b=a, norm1_g=b, norm2_g=cc, sgu_w=d, sgu_b=e, sgu_norm_g=f, q_norm_g=g, k_norm_g=h,
                              b_forget=i, loss=zero1))
             for a, b, cc, d, e, f, g, h, i in (
                 (ada_b, norm1_g, norm2_g, sgu_w, sgu_b, sgu_norm_g, q_norm_g, k_norm_g, b_forget),
                 (m_ada_b, m_norm1_g, m_norm2_g, m_sgu_w, m_sgu_b, m_sgu_norm_g, m_q_norm_g, m_k_norm_g, m_b_forget),
                 (v_ada_b, v_norm1_g, v_norm2_g, v_sgu_w, v_sgu_b, v_sgu_norm_g, v_q_norm_g, v_k_norm_g, v_b_forget))]
    small_shapes = dict(ada_b=ada_b.shape, norm1_g=norm1_g.shape, norm2_g=norm2_g.shape, sgu_w=sgu_w.shape,
                        sgu_b=sgu_b.shape, sgu_norm_g=sgu_norm_g.shape, q_norm_g=q_norm_g.shape,
                        k_norm_g=k_norm_g.shape, b_forget=b_forget.shape, loss=())
    s_out = [_unpack_small(a, small_shapes) for a in _adamw(small_all, *packs, "adamw_small")]

    dmod_all = small_all[:, :DEPTH * 6 * D_MODEL // 128, :].reshape(N_DEV, DEPTH, 6 * D_MODEL)
    dmod_mine = lax.dynamic_slice_in_dim(dmod_all, me * n_ada, n_ada, axis=2)
    g_ada = _ada_bwd(c_all, jnp.transpose(dmod_mine, (1, 0, 2)), "ada_bwd")
    ada_out = _adamw(g_ada.reshape(1, DEPTH * D_MODEL, n_ada), ada_w.reshape(-1, n_ada), m_ada_w.reshape(-1, n_ada),
                     v_ada_w.reshape(-1, n_ada), "adamw_ada")
    ada_out = [a.reshape(ada_w.shape) for a in ada_out]

    ri, ro, r1, r2 = _exchange(grads_ready, True, "scatter_grads", recv=received, slot=0)

    def big(slots, w, m, v, name):
        cdim = w.shape[-1]
        outs = _adamw(slots.reshape(N_DEV, -1, cdim), w.reshape(-1, cdim), m.reshape(-1, cdim), v.reshape(-1, cdim), name)
        return [a.reshape(w.shape) for a in outs]

    in_out = big(ri, w_in, m_w_in, v_w_in, "adamw_in")
    out_out = big(ro, w_out, m_w_out, v_w_out, "adamw_out")
    w1_out = big(r1, mlp_w1, m_mlp_w1, v_mlp_w1, "adamw_w1")
    w2_out = big(r2, mlp_w2, m_mlp_w2, v_mlp_w2, "adamw_w2")

    def leaves(idx):
        sm = s_out[idx]
        return [ada_out[idx], sm["ada_b"], sm["norm1_g"], sm["norm2_g"], in_out[idx], sm["b_forget"], sm["q_norm_g"],
                sm["k_norm_g"], sm["sgu_norm_g"], sm["sgu_w"], sm["sgu_b"], out_out[idx], w1_out[idx], w2_out[idx]]

    loss = s_out[0]["loss"]
    grad_x = dx.reshape(x.shape)
    return (loss, grad_x, *leaves(0), *leaves(1), *leaves(2), *leaves(3))
```

```python
import jax
import jax.numpy as jnp
from jax import lax
from jax.experimental import pallas as pl
from jax.experimental.pallas import tpu as pltpu

F32 = jnp.float32
_BF = jnp.bfloat16
_XBF = jnp.bfloat16

N_DEV = 8
D_MODEL = 1024
DEPTH = 4
HEAD_DIM = 64
SB_HEADS = 4
FOX_HEADS = 8
SGU_GROUPS = 4
SGU_CHUNK = 128
D_FF = 4096
EPS = 1e-6
IN_W = 2824
FL_SRC = 2304
UC_SRC = 2312

C_QB, C_KB, C_VB = 0, 512, 1024
C_QA, C_KA, C_VA = 1536, 1792, 2048
C_UC, C_VC = 2304, 2560
C_FL = 2816
IN_P = 3072
M_SB, M_FOX, M_SGU = 0, 256, 768

ADAM_LR = 0.001
ADAM_B1 = 0.9
ADAM_B2 = 0.999
ADAM_EPS = 1e-08
ADAM_WD = 0.01
ADAM_STEP = 10

ATT_T = 256
ROW_T = 512
MM_T = 1024
NEG = -1e30
UNDERFLOW = -100.0
NO_BLOCK = 1e30

MESH = pl.DeviceIdType.MESH
ARB = pltpu.ARBITRARY
PAR = pltpu.PARALLEL
HIGHEST = lax.Precision.HIGHEST


def _pcall(body, **kw):
    return pl.pallas_call(body, **kw)


def _cp(*sem):
    return pltpu.CompilerParams(dimension_semantics=tuple(sem))


def _dot(a, b, ca, cb):
    return lax.dot_general(a, b, (((ca,), (cb,)), ((), ())), preferred_element_type=F32)


def _dot_f32(a, b, ca, cb):
    return lax.dot_general(a, b, (((ca,), (cb,)), ((), ())), precision=HIGHEST, preferred_element_type=F32)


def _split(v, parts):
    out = []
    r = v
    for _ in range(parts - 1):
        p = r.astype(_BF)
        out.append(p)
        r = r - p.astype(F32)
    out.append(r.astype(_BF))
    return out


def _dot_exact(v, m, ca, cb, parts, v_left=True):
    out = None
    for p in _split(v, parts):
        term = _dot(p, m, ca, cb) if v_left else _dot(m, p, ca, cb)
        out = term if out is None else out + term
    return out


def _iota2(shape, dim):
    return lax.broadcasted_iota(jnp.int32, shape, dim)


def _my_pos():
    return lax.axis_index("x"), lax.axis_index("y"), lax.axis_index("c")


def _flip(pos, k):
    x, y, c = pos
    px = 1 - x if (k >> 2) & 1 else x
    py = 1 - y if (k >> 1) & 1 else y
    pc = 1 - c if k & 1 else c
    return px, py, pc


def _lin(pos):
    return 4 * pos[0] + 2 * pos[1] + pos[2]


def _exchange_ops(ins, outs, sems, scatter, slot):
    send_sems, recv_sems, local_sems = sems
    n = len(ins)

    def copies(with_recvs):
        me = _my_pos()
        me_i = _lin(me)
        dst = lambda a, j: outs[a].at[j] if slot is None else outs[a].at[j, slot]
        mine = lambda a: ins[a].at[me_i] if scatter else ins[a]
        local = [pltpu.make_async_copy(mine(a), dst(a, me_i), local_sems.at[a]) for a in range(n)]
        sends, recvs = [], []
        for a in range(n):
            for k in range(1, N_DEV):
                peer = _flip(me, k)
                pair = dict(send_sem=send_sems.at[a, k - 1], recv_sem=recv_sems.at[a, k - 1],
                            device_id=peer, device_id_type=MESH)
                src = ins[a].at[_lin(peer)] if scatter else ins[a]
                sends.append(pltpu.make_async_remote_copy(src_ref=src, dst_ref=dst(a, me_i), **pair))
                if with_recvs:
                    recvs.append(pltpu.make_async_remote_copy(src_ref=mine(a), dst_ref=dst(a, _lin(peer)), **pair))
        return local, sends, recvs

    def start():
        local, sends, _ = copies(False)
        for cp in local + sends:
            cp.start()

    def wait():
        local, sends, recvs = copies(True)
        for snd, rcv in zip(sends, recvs):
            snd.wait_send()
            rcv.wait_recv()
        for cp in local:
            cp.wait()

    return start, wait


def _comm_operands(comm):
    arrays, recv = comm["arrays"], comm["recv"]
    n = len(arrays)
    any_spec = pl.BlockSpec(memory_space=pl.ANY)
    if recv is not None:
        out_shapes = [jax.ShapeDtypeStruct(r.shape, r.dtype) for r in recv]
    elif comm["scatter"]:
        out_shapes = [jax.ShapeDtypeStruct(a.shape, a.dtype) for a in arrays]
    else:
        out_shapes = [jax.ShapeDtypeStruct((N_DEV,) + a.shape, a.dtype) for a in arrays]
    operands = list(arrays) + (list(recv) if recv is not None else [])
    sems = [pltpu.SemaphoreType.DMA((n, N_DEV - 1)), pltpu.SemaphoreType.DMA((n, N_DEV - 1)),
            pltpu.SemaphoreType.DMA((n,))]
    return operands, [any_spec] * len(operands), out_shapes, [any_spec] * n, sems


def _with_comm(body, comm, n_in, n_out, first_step, last_step):
    if comm is None:
        return body
    n = len(comm["arrays"])
    n_cin = n if comm["recv"] is None else 2 * n

    def wrapped(*refs):
        ins, cins = refs[:n_in], refs[n_in:n_in + n]
        o0 = n_in + n_cin
        outs, couts, sems = refs[o0:o0 + n_out], refs[o0 + n_out:o0 + n_out + n], refs[o0 + n_out + n:]
        start, wait = _exchange_ops(cins, couts, sems, comm["scatter"], comm["slot"])
        pl.when(first_step())(start)
        body(*ins, *outs)
        pl.when(last_step())(wait)

    return wrapped


def _comm_aliases(comm, n_in, n_out):
    if comm is None or comm["recv"] is None:
        return {}
    n = len(comm["arrays"])
    return {n_in + n + a: n_out + a for a in range(n)}


def _exchange(arrays, scatter, name, recv=None, slot=None):
    comm = dict(arrays=arrays, scatter=scatter, recv=recv, slot=slot)
    operands, in_specs, out_shapes, out_specs, sems = _comm_operands(comm)
    n = len(arrays)

    def body(*refs):
        n_cin = len(operands)
        start, wait = _exchange_ops(refs[:n], refs[n_cin:n_cin + n], refs[n_cin + n:], scatter, slot)
        start()
        wait()

    outs = _pcall(body, name=name, out_shape=out_shapes, in_specs=in_specs, out_specs=out_specs, scratch_shapes=sems,
                  input_output_aliases=_comm_aliases(comm, 0, 0))(*operands)
    return list(outs)


def _row_spec(t, w, col=0):
    return pl.BlockSpec((t, w), lambda i: (i, col))


def _vec_spec(w, col=0):
    return pl.BlockSpec((1, w), lambda i: (0, col))


def _norm_mod(x, g, sc, sh, name):
    s, d = x.shape
    t = min(ROW_T, s)

    def body(x_ref, g_ref, sc_ref, sh_ref, h_ref):
        xf = x_ref[...]
        rstd = lax.rsqrt(jnp.mean(xf * xf, axis=-1, keepdims=True) + EPS)
        y = xf * rstd * g_ref[...]
        h_ref[...] = (y * (1.0 + sc_ref[...]) + sh_ref[...]).astype(h_ref.dtype)

    return _pcall(
        body, name=name, grid=(s // t,), out_shape=jax.ShapeDtypeStruct((s, d), _BF),
        in_specs=[_row_spec(t, d), _vec_spec(d), _vec_spec(d), _vec_spec(d)],
        out_specs=_row_spec(t, d), compiler_params=_cp(PAR),
    )(x, g, sc, sh)


def _norm_mod_bwd(x, dh, dres, g, sc, name):
    s, d = x.shape
    t = min(ROW_T, s)

    def body(x_ref, dh_ref, dres_ref, g_ref, sc_ref, dx_ref, dsh_ref, dsc_ref, dg_ref):
        i = pl.program_id(0)
        xf = x_ref[...]
        dh = dh_ref[...]
        gv = g_ref[...]
        rstd = lax.rsqrt(jnp.mean(xf * xf, axis=-1, keepdims=True) + EPS)
        xhat = xf * rstd
        dn = dh * (1.0 + sc_ref[...])
        dxh = dn * gv
        dx_ref[...] = dres_ref[...] + rstd * (dxh - xhat * jnp.mean(dxh * xhat, axis=-1, keepdims=True))

        @pl.when(i == 0)
        def _():
            dsh_ref[...] = jnp.zeros_like(dsh_ref)
            dsc_ref[...] = jnp.zeros_like(dsc_ref)
            dg_ref[...] = jnp.zeros_like(dg_ref)

        dsh_ref[...] += jnp.sum(dh, axis=0, keepdims=True)
        dsc_ref[...] += jnp.sum(dh * (xhat * gv), axis=0, keepdims=True)
        dg_ref[...] += jnp.sum(dn * xhat, axis=0, keepdims=True)

    vec = jax.ShapeDtypeStruct((1, d), F32)
    return _pcall(
        body, name=name, grid=(s // t,),
        out_shape=[jax.ShapeDtypeStruct((s, d), F32), vec, vec, vec],
        in_specs=[_row_spec(t, d), _row_spec(t, d), _row_spec(t, d), _vec_spec(d), _vec_spec(d)],
        out_specs=[_row_spec(t, d), _vec_spec(d), _vec_spec(d), _vec_spec(d)],
        compiler_params=_cp(ARB),
    )(x, dh, dres, g, sc)


def _res_bwd(dx, y, gate, name):
    s, d = dx.shape
    t = min(ROW_T, s)

    def body(dx_ref, y_ref, g_ref, dy_ref, dg_ref):
        i = pl.program_id(0)
        dxv = dx_ref[...]
        dy_ref[...] = (dxv * g_ref[...]).astype(dy_ref.dtype)

        @pl.when(i == 0)
        def _():
            dg_ref[...] = jnp.zeros_like(dg_ref)

        dg_ref[...] += jnp.sum(dxv * y_ref[...], axis=0, keepdims=True)

    return _pcall(
        body, name=name, grid=(s // t,),
        out_shape=[jax.ShapeDtypeStruct((s, d), _BF), jax.ShapeDtypeStruct((1, d), F32)],
        in_specs=[_row_spec(t, d), _row_spec(t, d), _vec_spec(d)],
        out_specs=[_row_spec(t, d), _vec_spec(d)], compiler_params=_cp(ARB),
    )(dx, y, gate)


def _loss_head(y, target, name):
    s, d = y.shape
    t = min(ROW_T, s)

    def body(y_ref, t_ref, loss_ref, dy_ref):
        i = pl.program_id(0)
        diff = y_ref[...] - t_ref[...]
        dy_ref[...] = diff * (1.0 / d)

        @pl.when(i == 0)
        def _():
            loss_ref[...] = jnp.zeros_like(loss_ref)

        rows = jnp.sum(diff * diff, axis=-1, keepdims=True)
        loss_ref[...] += (0.5 / d) * jnp.sum(rows, axis=0, keepdims=True)

    return _pcall(
        body, name=name, grid=(s // t,),
        out_shape=[jax.ShapeDtypeStruct((1, 1), F32), jax.ShapeDtypeStruct((s, d), F32)],
        in_specs=[_row_spec(t, d), _row_spec(t, d)],
        out_specs=[pl.BlockSpec((1, 1), lambda i: (0, 0)), _row_spec(t, d)], compiler_params=_cp(ARB),
    )(y, target)


def _mm_nn(a, b, epi, extras, name):
    m, kdim = a.shape
    n = b.shape[1]
    tm, tn, tk = min(MM_T, m), min(MM_T, n), min(MM_T, kdim)
    nk = kdim // tk
    n_extra = len(extras)
    n_out = 1 if epi == "plain" else 2

    def finish(y, extra_refs, out_refs):
        if epi == "plain":
            out_refs[0][...] = y.astype(out_refs[0].dtype)
        elif epi == "resid":
            x_ref, g_ref = extra_refs
            out_refs[0][...] = x_ref[...] + g_ref[...] * y
            out_refs[1][...] = y
        else:
            r = jnp.maximum(y, 0.0)
            out_refs[0][...] = (r * r).astype(out_refs[0].dtype)
            out_refs[1][...] = r.astype(out_refs[1].dtype)

    def body(a_ref, b_ref, *rest):
        extra_refs = rest[:n_extra]
        out_refs = rest[n_extra:n_extra + n_out]
        part = _dot(a_ref[...].astype(_BF), b_ref[...].astype(_BF), 1, 0)
        if nk == 1:
            finish(part, extra_refs, out_refs)
        else:
            acc_ref = rest[-1]
            k = pl.program_id(2)

            @pl.when(k == 0)
            def _():
                acc_ref[...] = part

            @pl.when(k > 0)
            def _():
                acc_ref[...] += part

            @pl.when(k == nk - 1)
            def _():
                finish(acc_ref[...], extra_refs, out_refs)

    tile = pl.BlockSpec((tm, tn), lambda i, j, k: (i, j))
    in_specs = [pl.BlockSpec((tm, tk), lambda i, j, k: (i, k)), pl.BlockSpec((tk, tn), lambda i, j, k: (k, j))]
    if epi == "plain":
        out_shape = [jax.ShapeDtypeStruct((m, n), F32)]
    elif epi == "resid":
        in_specs += [tile, pl.BlockSpec((1, tn), lambda i, j, k: (0, j))]
        out_shape = [jax.ShapeDtypeStruct((m, n), F32)] * 2
    else:
        out_shape = [jax.ShapeDtypeStruct((m, n), _BF), jax.ShapeDtypeStruct((m, n), F32)]
    outs = _pcall(
        body, name=name, grid=(m // tm, n // tn, nk), out_shape=out_shape,
        in_specs=in_specs, out_specs=[tile] * n_out,
        scratch_shapes=[pltpu.VMEM((tm, tn), F32)] if nk > 1 else [],
        compiler_params=_cp(PAR, PAR, ARB),
    )(a, b, *extras)
    return outs[0] if n_out == 1 else outs


def _mm_nt(a, b, epi, extras, out_dtype, name):
    m, kdim = a.shape
    n = b.shape[0]
    tm, tn, tk = min(MM_T, m), min(MM_T, n), min(MM_T, kdim)
    nk = kdim // tk
    n_extra = len(extras)

    def finish(y, extra_refs, o_ref):
        if epi == "mul2":
            y = y * (2.0 * extra_refs[0][...].astype(F32))
        o_ref[...] = y.astype(o_ref.dtype)

    def body(a_ref, b_ref, *rest):
        extra_refs = rest[:n_extra]
        o_ref = rest[n_extra]
        part = _dot(a_ref[...].astype(_BF), b_ref[...].astype(_BF), 1, 1)
        if nk == 1:
            finish(part, extra_refs, o_ref)
        else:
            acc_ref = rest[-1]
            k = pl.program_id(2)

            @pl.when(k == 0)
            def _():
                acc_ref[...] = part

            @pl.when(k > 0)
            def _():
                acc_ref[...] += part

            @pl.when(k == nk - 1)
            def _():
                finish(acc_ref[...], extra_refs, o_ref)

    tile = pl.BlockSpec((tm, tn), lambda i, j, k: (i, j))
    in_specs = [pl.BlockSpec((tm, tk), lambda i, j, k: (i, k)), pl.BlockSpec((tn, tk), lambda i, j, k: (j, k))]
    in_specs += [tile] * n_extra
    return _pcall(
        body, name=name, grid=(m // tm, n // tn, nk), out_shape=jax.ShapeDtypeStruct((m, n), out_dtype),
        in_specs=in_specs, out_specs=tile,
        scratch_shapes=[pltpu.VMEM((tm, tn), F32)] if nk > 1 else [],
        compiler_params=_cp(PAR, PAR, ARB),
    )(a, b, *extras)


def _mm_tn(a, b, split, name):
    s, m = a.shape
    n = b.shape[1]
    ts = min(MM_T, s)
    ns = s // ts
    if split == "rows":
        tm, tn = min(MM_T, m // N_DEV), min(MM_T, n)
        per = (m // N_DEV) // tm
        out_shape = (N_DEV, m // N_DEV, n)
        out_spec = pl.BlockSpec((None, tm, tn), lambda i, j, k: (i // per, i % per, j))
    elif split == "cols":
        tm, tn = min(MM_T, m), min(MM_T, n // N_DEV)
        per = (n // N_DEV) // tn
        out_shape = (N_DEV, m, n // N_DEV)
        out_spec = pl.BlockSpec((None, tm, tn), lambda i, j, k: (j // per, i, j % per))
    else:
        tm, tn = min(MM_T, m), min(MM_T, n)
        out_shape = (m, n)
        out_spec = pl.BlockSpec((tm, tn), lambda i, j, k: (i, j))

    def body(a_ref, b_ref, o_ref, acc_ref):
        k = pl.program_id(2)
        part = _dot(a_ref[...].astype(_BF), b_ref[...].astype(_BF), 0, 0)

        @pl.when(k == 0)
        def _():
            acc_ref[...] = part

        @pl.when(k > 0)
        def _():
            acc_ref[...] += part

        @pl.when(k == ns - 1)
        def _():
            o_ref[...] = acc_ref[...].astype(o_ref.dtype)

    return _pcall(
        body, name=name, grid=(m // tm, n // tn, ns), out_shape=jax.ShapeDtypeStruct(out_shape, _XBF),
        in_specs=[pl.BlockSpec((ts, tm), lambda i, j, k: (k, i)), pl.BlockSpec((ts, tn), lambda i, j, k: (k, j))],
        out_specs=out_spec, scratch_shapes=[pltpu.VMEM((tm, tn), F32)],
        compiler_params=_cp(PAR, PAR, ARB),
    )(a, b)


def _head_masks():
    lane = _iota2((1, 128), 1)
    return [(lane // HEAD_DIM) == hh for hh in range(2)]


def _softplus_parts(z):
    sp = jnp.maximum(z, 0.0) + jnp.log(1.0 + jnp.exp(-jnp.abs(z)))
    return -sp, z - sp


def _sb_key_loop(i, carry, step, lookahead, first_ahead):
    def cond(c):
        return jnp.logical_and(c[0] <= i, jnp.max(jnp.maximum(c[1][0][0], c[1][0][1])) > UNDERFLOW)

    def body(c):
        ahead = lookahead(i - c[0] - 1)
        return c[0] + 1, step(i - c[0], c[1], c[2]), ahead

    return lax.while_loop(cond, body, (jnp.int32(1), carry, first_ahead))[1]


def _sb_fwd(proj, name):
    s = proj.shape[0]
    t = min(ATT_T, s)
    nblk = s // t
    scale = HEAD_DIM ** -0.5

    def body(q_ref, k_ref, v_ref, o_ref):
        i = pl.program_id(1)
        masks = _head_masks()
        row = _iota2((t, t), 0)
        col = _iota2((t, t), 1)
        u_strict = (row > col).astype(_BF)
        causal = col < row
        qs = q_ref[...] * scale
        qm = [jnp.where(masks[hh], qs, 0.0).astype(_BF) for hh in range(2)]

        def scores(j):
            kb = k_ref[pl.ds(pl.multiple_of(jnp.maximum(j, 0) * t, t), t), :].astype(_BF)
            return tuple(_dot(qm[hh], kb, 1, 1) for hh in range(2))

        def block(j, carry, qk, diag):
            off = pl.multiple_of(j * t, t)
            vb = v_ref[pl.ds(off, t), :].astype(_BF)
            runs, outs = [], []
            for hh in range(2):
                run, o = carry[0][hh], carry[1][hh]
                z = qk[hh]
                l1, lb = _softplus_parts(z)
                if diag:
                    l1 = jnp.where(causal, l1, 0.0)
                between = run + _dot_exact(l1, u_strict, 1, 0, 2)
                a = jnp.exp(lb + between)
                if diag:
                    a = jnp.where(causal, a, 0.0)
                outs.append(o + _dot_exact(a, vb, 1, 0, 2))
                runs.append(run + jnp.sum(l1, axis=-1, keepdims=True))
            return tuple(runs), tuple(outs)

        zero, zero_o = jnp.zeros((t, 1), F32), jnp.zeros((t, 128), F32)
        ahead = scores(i - 1)
        carry = block(i, ((zero, zero), (zero_o, zero_o)), scores(i), True)
        carry = _sb_key_loop(i, carry, lambda j, c, qk: block(j, c, qk, False), scores, ahead)
        o_ref[...] = jnp.where(masks[0], carry[1][0], carry[1][1])

    kv_spec = lambda c0: pl.BlockSpec((s, 128), lambda p, i: (0, c0 // 128 + p))
    return _pcall(
        body, name=name, grid=(SB_HEADS // 2, nblk), out_shape=jax.ShapeDtypeStruct((s, 256), F32),
        in_specs=[pl.BlockSpec((t, 128), lambda p, i: (i, C_QA // 128 + p)), kv_spec(C_KA), kv_spec(C_VA)],
        out_specs=pl.BlockSpec((t, 128), lambda p, i: (i, p)), compiler_params=_cp(PAR, ARB),
    )(proj, proj, proj)


def _sb_bwd(proj, o, dmix, name):
    s = proj.shape[0]
    t = min(ATT_T, s)
    nblk = s // t
    scale = HEAD_DIM ** -0.5

    def body(q_ref, k_ref, v_ref, o_ref, do_ref, dq_ref, dk_ref, dv_ref):
        i = pl.program_id(1)
        masks = _head_masks()
        row = _iota2((t, t), 0)
        col = _iota2((t, t), 1)
        u_strict = (row > col).astype(_BF)
        u_incl = (row >= col).astype(_BF)
        causal = col < row

        @pl.when(i == 0)
        def _():
            dk_ref[...] = jnp.zeros_like(dk_ref)
            dv_ref[...] = jnp.zeros_like(dv_ref)

        qs = q_ref[...] * scale
        dov = do_ref[...]
        ov = o_ref[...]
        qm = [jnp.where(masks[hh], qs, 0.0).astype(_BF) for hh in range(2)]
        dom = [jnp.where(masks[hh], dov, 0.0).astype(_BF) for hh in range(2)]
        dsum = [jnp.sum(dom[hh].astype(F32) * ov, axis=-1, keepdims=True) for hh in range(2)]

        def scores(j):
            off = pl.multiple_of(jnp.maximum(j, 0) * t, t)
            kb = k_ref[pl.ds(off, t), :].astype(_BF)
            vb = v_ref[pl.ds(off, t), :].astype(_BF)
            return tuple((_dot(qm[hh], kb, 1, 1), _dot(dom[hh], vb, 1, 1)) for hh in range(2))

        def block(j, carry, ahead, diag):
            off = pl.multiple_of(j * t, t)
            kb = k_ref[pl.ds(off, t), :].astype(_BF)
            runs, eruns, dqs = [], [], []
            dk = dv = None
            for hh in range(2):
                run, erun, dq = carry[0][hh], carry[1][hh], carry[2][hh]
                z, da = ahead[hh]
                l1, lb = _softplus_parts(z)
                if diag:
                    l1 = jnp.where(causal, l1, 0.0)
                between = run + _dot_exact(l1, u_strict, 1, 0, 2)
                a = jnp.exp(lb + between)
                if diag:
                    a = jnp.where(causal, a, 0.0)
                g = a * da
                cum = dsum[hh] - (erun + _dot_exact(g, u_incl, 1, 0, 2))
                beta = jnp.exp(lb)
                dz = g * (1.0 - beta) - cum * beta
                if diag:
                    dz = jnp.where(causal, dz, 0.0)
                dzb = dz.astype(_BF)
                dqs.append(dq + _dot(dzb, kb, 1, 0))
                dk_h = _dot(dzb, qm[hh], 0, 0)
                dv_h = _dot(a.astype(_BF), dom[hh], 0, 0)
                dk = dk_h if dk is None else dk + dk_h
                dv = dv_h if dv is None else dv + dv_h
                runs.append(run + jnp.sum(l1, axis=-1, keepdims=True))
                eruns.append(erun + jnp.sum(g, axis=-1, keepdims=True))
            dk_ref[pl.ds(off, t), :] += dk
            dv_ref[pl.ds(off, t), :] += dv
            return tuple(runs), tuple(eruns), tuple(dqs)

        zero, zero_q = jnp.zeros((t, 1), F32), jnp.zeros((t, 128), F32)
        ahead = scores(i - 1)
        carry = block(i, ((zero, zero), (zero, zero), (zero_q, zero_q)), scores(i), True)
        carry = _sb_key_loop(i, carry, lambda j, c, a: block(j, c, a, False), scores, ahead)
        dq_ref[...] = jnp.where(masks[0], carry[2][0], carry[2][1]) * scale

    kv_spec = lambda c0: pl.BlockSpec((s, 128), lambda p, i: (0, c0 // 128 + p))
    blk = lambda c0: pl.BlockSpec((t, 128), lambda p, i: (i, c0 // 128 + p))
    acc = pl.BlockSpec((s, 128), lambda p, i: (0, p))
    shp = jax.ShapeDtypeStruct((s, 256), F32)
    return _pcall(
        body, name=name, grid=(SB_HEADS // 2, nblk), out_shape=[shp, shp, shp],
        in_specs=[blk(C_QA), kv_spec(C_KA), kv_spec(C_VA), blk(0), blk(M_SB)],
        out_specs=[blk(0), acc, acc], compiler_params=_cp(PAR, ARB),
    )(proj, proj, proj, o, dmix)


def _group_matrix(width):
    r = _iota2((width, width), 0) // HEAD_DIM
    c = _iota2((width, width), 1) // HEAD_DIM
    return (r == c).astype(_BF)


def _group_mean(v, gm):
    return _dot_exact(v, gm, 1, 0, 2) * (1.0 / HEAD_DIM)


def _fox_prep(proj, gq, gk, bf, name):
    s = proj.shape[0]
    t = min(ATT_T, s)
    nblk = s // t
    assert nblk <= 128

    def body(q_ref, k_ref, fl_ref, gq_ref, gk_ref, bf_ref, qn_ref, kn_ref, cfc_ref, cfr_ref, cfe_ref, carry_ref):
        i = pl.program_id(0)
        gm = _group_matrix(512)
        for src, gref, dst in ((q_ref, gq_ref, qn_ref), (k_ref, gk_ref, kn_ref)):
            v = src[...]
            rstd = lax.rsqrt(_group_mean(v * v, gm) + EPS)
            dst[...] = (v * rstd * gref[...]).astype(dst.dtype)

        @pl.when(i == 0)
        def _():
            carry_ref[...] = jnp.zeros_like(carry_ref)
            cfe_ref[...] = jnp.full(cfe_ref.shape, NO_BLOCK, F32)

        lane = _iota2((1, 128), 1)
        logf = jnp.where(lane < FOX_HEADS, _softplus_parts(-(fl_ref[...] + bf_ref[...]))[0], 0.0)
        lower = (_iota2((t, t), 0) >= _iota2((t, t), 1)).astype(_BF)
        cf = carry_ref[...] + _dot_exact(logf, lower, 1, 0, 3, v_left=False)
        cfc_ref[...] = cf
        rows = cf.T[:FOX_HEADS, :]
        cfr_ref[...] = rows
        cfe_ref[...] = jnp.where(lane == i, rows[:, t - 1:t], cfe_ref[...])
        carry_ref[...] = cf[t - 1:t, :]

    col = lambda w, c0: pl.BlockSpec((t, w), lambda i: (i, c0 // w))
    v512 = pl.BlockSpec((1, 512), lambda i: (0, 0))
    return _pcall(
        body, name=name, grid=(nblk,),
        out_shape=[jax.ShapeDtypeStruct((s, 512), _BF), jax.ShapeDtypeStruct((s, 512), _BF),
                   jax.ShapeDtypeStruct((s, 128), F32), jax.ShapeDtypeStruct((FOX_HEADS, s), F32),
                   jax.ShapeDtypeStruct((FOX_HEADS, 128), F32)],
        in_specs=[col(512, C_QB), col(512, C_KB), col(128, C_FL), v512, v512, pl.BlockSpec((1, 128), lambda i: (0, 0))],
        out_specs=[col(512, 0), col(512, 0), col(128, 0), pl.BlockSpec((FOX_HEADS, t), lambda i: (0, i)),
                   pl.BlockSpec((FOX_HEADS, 128), lambda i: (0, 0))],
        scratch_shapes=[pltpu.VMEM((1, 128), F32)], compiler_params=_cp(ARB),
    )(proj, proj, proj, gq, gk, bf)


def _fox_cq(cfc_blk, head):
    lane = _iota2((1, 128), 1)
    return jnp.sum(jnp.where(lane == head, cfc_blk, 0.0), axis=-1, keepdims=True)


def _fox_ck(cfr_ref, off, t, head):
    sub = _iota2((FOX_HEADS, 1), 0)
    return jnp.sum(jnp.where(sub == head, cfr_ref[:, pl.ds(off, t)], 0.0), axis=0, keepdims=True)


def _fox_blocks_needed(top, cfe_ref, head, i):
    sub = _iota2((FOX_HEADS, 1), 0)
    lane = _iota2((1, 128), 1)
    last_key = jnp.sum(jnp.where(sub == head, cfe_ref[...], 0.0), axis=0, keepdims=True)
    need = jnp.logical_and(lane < i, top - last_key > UNDERFLOW)
    return jnp.sum(need.astype(jnp.int32))


def _grid_ends(n_outer, n_inner):
    first = lambda: jnp.logical_and(pl.program_id(0) == 0, pl.program_id(1) == 0)
    last = lambda: jnp.logical_and(pl.program_id(0) == n_outer - 1, pl.program_id(1) == n_inner - 1)
    return first, last


def _fox_fwd(proj, qn, kn, cfc, cfr, cfe, qk_bound, name, comm=None):
    s = proj.shape[0]
    t = min(ATT_T, s)
    nblk = s // t
    scale = HEAD_DIM ** -0.5

    def body(q_ref, k_ref, v_ref, cfc_ref, cfr_ref, cfe_ref, qkb_ref, o_ref, lse_ref):
        p = pl.program_id(0)
        i = pl.program_id(1)
        masks = _head_masks()
        lane = _iota2((1, 128), 1)
        valid = _iota2((t, t), 1) <= _iota2((t, t), 0)
        qv = q_ref[...]
        cfc_blk = cfc_ref[...]
        heads = [2 * p + hh for hh in range(2)]
        qm = [(jnp.where(masks[hh], qv, 0) * scale).astype(_BF) for hh in range(2)]
        cq = [_fox_cq(cfc_blk, heads[hh]) for hh in range(2)]

        def scores(j):
            kb = k_ref[pl.ds(pl.multiple_of(jnp.maximum(j, 0) * t, t), t), :]
            return tuple(_dot(qm[hh], kb, 1, 1) for hh in range(2))

        def block(j, carry, qk, diag):
            off = pl.multiple_of(j * t, t)
            vb = v_ref[pl.ds(off, t), :].astype(_BF)
            ms, ls, accs = [], [], []
            for hh in range(2):
                m, l, acc = carry[0][hh], carry[1][hh], carry[2][hh]
                z = qk[hh] + (cq[hh] - _fox_ck(cfr_ref, off, t, heads[hh]))
                if diag:
                    z = jnp.where(valid, z, NEG)
                m_new = jnp.maximum(m, jnp.max(z, axis=-1, keepdims=True))
                pe = jnp.exp(z - m_new)
                alpha = jnp.exp(m - m_new)
                ms.append(m_new)
                ls.append(alpha * l + jnp.sum(pe, axis=-1, keepdims=True))
                accs.append(alpha * acc + _dot_exact(pe, vb, 1, 0, 2))
            return tuple(ms), tuple(ls), tuple(accs)

        def step(jj, c):
            ahead = scores(i - jj - 2)
            return block(i - jj - 1, c[0], c[1], False), ahead

        neg, zero, zero_o = jnp.full((t, 1), NEG, F32), jnp.zeros((t, 1), F32), jnp.zeros((t, 128), F32)
        ahead = scores(i - 1)
        carry = block(i, ((neg, neg), (zero, zero), (zero_o, zero_o)), scores(i), True)
        needed = [_fox_blocks_needed(jnp.max(qkb_ref[...] + cq[hh] - carry[0][hh]), cfe_ref, heads[hh], i)
                  for hh in range(2)]
        m, l, acc = lax.fori_loop(0, jnp.maximum(needed[0], needed[1]), step, (carry, ahead))[0]
        o_ref[...] = jnp.where(masks[0], acc[0] / l[0], acc[1] / l[1])
        lse_ref[...] = jnp.where(lane == 0, m[0] + jnp.log(l[0]), jnp.where(lane == 1, m[1] + jnp.log(l[1]), 0.0))

    c_operands, c_in_specs, c_out_shapes, c_out_specs, c_sems = _comm_operands(comm) if comm else ([], [], [], [], [])
    outs = _pcall(
        _with_comm(body, comm, 7, 2, *_grid_ends(FOX_HEADS // 2, nblk)), name=name, grid=(FOX_HEADS // 2, nblk),
        out_shape=[jax.ShapeDtypeStruct((s, 512), F32), jax.ShapeDtypeStruct((FOX_HEADS // 2, s, 128), F32)] + c_out_shapes,
        in_specs=[pl.BlockSpec((t, 128), lambda p, i: (i, p)), pl.BlockSpec((s, 128), lambda p, i: (0, p)),
                  pl.BlockSpec((s, 128), lambda p, i: (0, C_VB // 128 + p)),
                  pl.BlockSpec((t, 128), lambda p, i: (i, 0)), pl.BlockSpec((FOX_HEADS, s), lambda p, i: (0, 0)),
                  pl.BlockSpec((FOX_HEADS, 128), lambda p, i: (0, 0)), pl.BlockSpec((1, 1), lambda p, i: (0, 0))] + c_in_specs,
        out_specs=[pl.BlockSpec((t, 128), lambda p, i: (i, p)),
                   pl.BlockSpec((None, t, 128), lambda p, i: (p, i, 0))] + c_out_specs,
        scratch_shapes=c_sems, input_output_aliases=_comm_aliases(comm, 7, 2),
        compiler_params=_cp(ARB, ARB),
    )(qn, kn, proj, cfc, cfr, cfe, qk_bound, *c_operands)
    return outs[0], outs[1], list(outs[2:])


def _fox_bwd(proj, qn, kn, cfc, cfr, cfe, qk_bound, o, lse, dmix, name, comm=None):
    s = proj.shape[0]
    t = min(ATT_T, s)
    nblk = s // t
    scale = HEAD_DIM ** -0.5

    def body(q_ref, k_ref, v_ref, cfc_ref, cfr_ref, cfe_ref, qkb_ref, o_ref, lse_ref, do_ref,
             dq_ref, dk_ref, dv_ref, dcr_ref):
        p = pl.program_id(0)
        i = pl.program_id(1)
        masks = _head_masks()
        lane = _iota2((1, 128), 1)
        valid = _iota2((t, t), 1) <= _iota2((t, t), 0)
        before = (_iota2((t, t), 0) < _iota2((t, t), 1)).astype(_BF)

        @pl.when(i == 0)
        def _():
            dk_ref[...] = jnp.zeros_like(dk_ref)
            dv_ref[...] = jnp.zeros_like(dv_ref)
            dcr_ref[...] = jnp.zeros_like(dcr_ref)

        qv = q_ref[...]
        cfc_blk = cfc_ref[...]
        dov = do_ref[...]
        ov = o_ref[...]
        lse_blk = lse_ref[...]
        heads = [2 * p + hh for hh in range(2)]
        qm = [(jnp.where(masks[hh], qv, 0) * scale).astype(_BF) for hh in range(2)]
        dom = [jnp.where(masks[hh], dov, 0.0).astype(_BF) for hh in range(2)]
        dsum = [jnp.sum(dom[hh].astype(F32) * ov, axis=-1, keepdims=True) for hh in range(2)]
        lse_h = [jnp.sum(jnp.where(lane == hh, lse_blk, 0.0), axis=-1, keepdims=True) for hh in range(2)]
        cq = [_fox_cq(cfc_blk, heads[hh]) for hh in range(2)]
        needed = [_fox_blocks_needed(jnp.max(qkb_ref[...] + cq[hh] - lse_h[hh]), cfe_ref, heads[hh], i)
                  for hh in range(2)]
        first = i - jnp.maximum(needed[0], needed[1])

        def scores(j):
            off = pl.multiple_of(j * t, t)
            kb = k_ref[pl.ds(off, t), :]
            vb = v_ref[pl.ds(off, t), :].astype(_BF)
            return tuple((_dot(qm[hh], kb, 1, 1), _dot(dom[hh], vb, 1, 1)) for hh in range(2))

        def block(j, carry, ahead, diag):
            off = pl.multiple_of(j * t, t)
            kb = k_ref[pl.ds(off, t), :]
            dqs, pres = [], []
            dk = dv = None
            for hh in range(2):
                dq, pre = carry[0][hh], carry[1][hh]
                z = ahead[hh][0] + (cq[hh] - _fox_ck(cfr_ref, off, t, heads[hh]))
                pm = jnp.exp(z - lse_h[hh])
                if diag:
                    pm = jnp.where(valid, pm, 0.0)
                ds = pm * (ahead[hh][1] - dsum[hh])
                dsb = ds.astype(_BF)
                dqs.append(dq + _dot(dsb, kb, 1, 0))
                dk_h = _dot(dsb, qm[hh], 0, 0)
                dv_h = _dot(pm.astype(_BF), dom[hh], 0, 0)
                dk = dk_h if dk is None else dk + dk_h
                dv = dv_h if dv is None else dv + dv_h
                if diag:
                    lower_keys = jnp.where(valid, pre + _dot_exact(ds, before, 1, 0, 2), 0.0)
                    dlogf = jnp.sum(lower_keys, axis=0, keepdims=True)
                else:
                    dlogf = jnp.sum(pre) + _dot_exact(jnp.sum(ds, axis=0, keepdims=True), before, 1, 0, 3)
                dcr_ref[hh:hh + 1, pl.ds(off, t)] += dlogf
                pres.append(pre + jnp.sum(ds, axis=-1, keepdims=True))
            dk_ref[pl.ds(off, t), :] += dk
            dv_ref[pl.ds(off, t), :] += dv
            return tuple(dqs), tuple(pres)

        zero, zero_q = jnp.zeros((t, 1), F32), jnp.zeros((t, 128), F32)


        def step(j, c):
            ahead = scores(j + 1)
            return block(j, c[0], c[1], False), ahead

        carry, ahead = lax.fori_loop(first, i, step, (((zero_q, zero_q), (zero, zero)), scores(first)))
        dqs, _ = block(i, carry, ahead, True)
        dq_ref[...] = jnp.where(masks[0], dqs[0], dqs[1]) * scale

    blk = lambda c0: pl.BlockSpec((t, 128), lambda p, i: (i, c0 // 128 + p))
    res = lambda c0: pl.BlockSpec((s, 128), lambda p, i: (0, c0 // 128 + p))
    shp = jax.ShapeDtypeStruct((s, 512), F32)
    c_operands, c_in_specs, c_out_shapes, c_out_specs, c_sems = _comm_operands(comm) if comm else ([], [], [], [], [])
    outs = _pcall(
        _with_comm(body, comm, 10, 4, *_grid_ends(FOX_HEADS // 2, nblk)), name=name, grid=(FOX_HEADS // 2, nblk),
        out_shape=[shp, shp, shp, jax.ShapeDtypeStruct((FOX_HEADS // 2, 8, s), F32)] + c_out_shapes,
        in_specs=[blk(0), res(0), res(C_VB), pl.BlockSpec((t, 128), lambda p, i: (i, 0)),
                  pl.BlockSpec((FOX_HEADS, s), lambda p, i: (0, 0)), pl.BlockSpec((FOX_HEADS, 128), lambda p, i: (0, 0)),
                  pl.BlockSpec((1, 1), lambda p, i: (0, 0)), blk(0),
                  pl.BlockSpec((None, t, 128), lambda p, i: (p, i, 0)), blk(M_FOX)] + c_in_specs,
        out_specs=[blk(0), res(0), res(0), pl.BlockSpec((None, 8, s), lambda p, i: (p, 0, 0))] + c_out_specs,
        scratch_shapes=c_sems, input_output_aliases=_comm_aliases(comm, 10, 4),
        compiler_params=_cp(ARB, ARB),
    )(qn, kn, proj, cfc, cfr, cfe, qk_bound, o, lse, dmix, *c_operands)
    return outs[0], outs[1], outs[2], outs[3], list(outs[4:])


def _fox_prep_bwd(proj, gq, gk, bf, dqn, dkn, dlogf, name):
    s = proj.shape[0]
    t = min(ATT_T, s)
    nblk = s // t

    def body(q_ref, k_ref, fl_ref, gq_ref, gk_ref, bf_ref, dqn_ref, dkn_ref, dlogf_ref,
             dq_ref, dk_ref, dfl_ref, dgq_ref, dgk_ref, dbf_ref):
        i = pl.program_id(0)
        gm = _group_matrix(512)

        @pl.when(i == 0)
        def _():
            dgq_ref[...] = jnp.zeros_like(dgq_ref)
            dgk_ref[...] = jnp.zeros_like(dgk_ref)
            dbf_ref[...] = jnp.zeros_like(dbf_ref)

        for src, gref, dyref, dst, dgref in ((q_ref, gq_ref, dqn_ref, dq_ref, dgq_ref),
                                              (k_ref, gk_ref, dkn_ref, dk_ref, dgk_ref)):
            v = src[...]
            dy = dyref[...]
            rstd = lax.rsqrt(_group_mean(v * v, gm) + EPS)
            xhat = v * rstd
            dgref[...] += jnp.sum(dy * xhat, axis=0, keepdims=True)
            dxh = dy * gref[...]
            dst[...] = (rstd * (dxh - xhat * _group_mean(dxh * xhat, gm))).astype(dst.dtype)

        xv = fl_ref[...] + bf_ref[...]
        e = jnp.exp(-jnp.abs(xv))
        dfl = dlogf_ref[...] * (jnp.where(xv >= 0.0, e, 1.0) / (1.0 + e))
        dfl_ref[...] = dfl.astype(dfl_ref.dtype)
        dbf_ref[...] += jnp.sum(dfl, axis=0, keepdims=True)

    col = lambda w, c0: pl.BlockSpec((t, w), lambda i: (i, c0 // w))
    v512 = pl.BlockSpec((1, 512), lambda i: (0, 0))
    v128 = pl.BlockSpec((1, 128), lambda i: (0, 0))
    return _pcall(
        body, name=name, grid=(nblk,),
        out_shape=[jax.ShapeDtypeStruct((s, 512), _BF), jax.ShapeDtypeStruct((s, 512), _BF),
                   jax.ShapeDtypeStruct((s, 128), _BF), jax.ShapeDtypeStruct((1, 512), F32),
                   jax.ShapeDtypeStruct((1, 512), F32), jax.ShapeDtypeStruct((1, 128), F32)],
        in_specs=[col(512, C_QB), col(512, C_KB), col(128, C_FL), v512, v512, v128, col(512, 0), col(512, 0), col(128, 0)],
        out_specs=[col(512, 0), col(512, 0), col(128, 0), v512, v512, v128], compiler_params=_cp(ARB),
    )(proj, proj, proj, gq, gk, bf, dqn, dkn, dlogf)


GELU_C = 0.7978845608028654
GELU_A = 0.044715


def _gelu(x):
    return 0.5 * x * (1.0 + jnp.tanh(GELU_C * (x + GELU_A * x * x * x)))


def _gelu_grad(x):
    th = jnp.tanh(GELU_C * (x + GELU_A * x * x * x))
    return 0.5 * (1.0 + th) + 0.5 * x * (1.0 - th * th) * (GELU_C * (1.0 + 3.0 * GELU_A * x * x))


def _sgu_tril():
    return _iota2((SGU_CHUNK, SGU_CHUNK), 0) >= _iota2((SGU_CHUNK, SGU_CHUNK), 1)


def _sgu_group_masks():
    lane = _iota2((1, 256), 1)
    return [(lane // HEAD_DIM) == g for g in range(SGU_GROUPS)]


def _sgu_fwd(proj, w, gs, bexp, name):
    s = proj.shape[0]
    t = min(2 * SGU_CHUNK, s)

    def body(u_ref, v_ref, w_ref, gs_ref, b_ref, o_ref):
        gm = _group_matrix(256)
        gmask = _sgu_group_masks()
        tril = _sgu_tril()
        u = _gelu(u_ref[...])
        vg = _gelu(v_ref[...])
        vhat = (vg * lax.rsqrt(_group_mean(vg * vg, gm) + EPS) * gs_ref[...]).astype(_BF)
        for ch in range(t // SGU_CHUNK):
            rows = slice(ch * SGU_CHUNK, (ch + 1) * SGU_CHUNK)
            mixed = b_ref[...]
            for g in range(SGU_GROUPS):
                wg = jnp.where(tril, w_ref[g], 0.0).astype(_BF)
                mixed = jnp.where(gmask[g], mixed + _dot(wg, vhat[rows], 1, 0), mixed)
            o_ref[rows, :] = u[rows] * mixed

    return _pcall(
        body, name=name, grid=(s // t,), out_shape=jax.ShapeDtypeStruct((s, 256), F32),
        in_specs=[_row_spec(t, 256, C_UC // 256), _row_spec(t, 256, C_VC // 256),
                  pl.BlockSpec((SGU_GROUPS, SGU_CHUNK, SGU_CHUNK), lambda i: (0, 0, 0)), _vec_spec(256),
                  pl.BlockSpec((SGU_CHUNK, 256), lambda i: (0, 0))],
        out_specs=_row_spec(t, 256), compiler_params=_cp(PAR),
    )(proj, proj, w, gs, bexp)


def _sgu_bwd(proj, w, gs, bexp, dmix, name):
    s = proj.shape[0]
    t = min(2 * SGU_CHUNK, s)
    nstep = s // t

    def body(u_ref, v_ref, w_ref, gs_ref, b_ref, do_ref, du_ref, dv_ref, dw_ref, db_ref, dgs_ref):
        i = pl.program_id(0)
        gm = _group_matrix(256)
        gmask = _sgu_group_masks()
        tril = _sgu_tril()

        @pl.when(i == 0)
        def _():
            dw_ref[...] = jnp.zeros_like(dw_ref)
            db_ref[...] = jnp.zeros_like(db_ref)
            dgs_ref[...] = jnp.zeros_like(dgs_ref)

        uc = u_ref[...]
        vc = v_ref[...]
        u = _gelu(uc)
        vg = _gelu(vc)
        rstd = lax.rsqrt(_group_mean(vg * vg, gm) + EPS)
        xh = vg * rstd
        gsv = gs_ref[...]
        vhat = (xh * gsv).astype(_BF)
        dov = do_ref[...]
        dm = dov * u
        for ch in range(t // SGU_CHUNK):
            rows = slice(ch * SGU_CHUNK, (ch + 1) * SGU_CHUNK)
            mixed = b_ref[...]
            dvh = jnp.zeros((SGU_CHUNK, 256), F32)
            dmc = dm[rows]
            for g in range(SGU_GROUPS):
                wg = jnp.where(tril, w_ref[g], 0.0).astype(_BF)
                mixed = jnp.where(gmask[g], mixed + _dot(wg, vhat[rows], 1, 0), mixed)
                dvh = jnp.where(gmask[g], _dot(wg, dmc.astype(_BF), 0, 0), dvh)
                dw_ref[g] += _dot(jnp.where(gmask[g], dmc, 0.0).astype(_BF), vhat[rows], 1, 1)
            db_ref[...] += dmc
            du_ref[rows, :] = (dov[rows] * mixed * _gelu_grad(uc[rows])).astype(du_ref.dtype)
            xhc = xh[rows]
            dgs_ref[...] += jnp.sum(dvh * xhc, axis=0, keepdims=True)
            dxh = dvh * gsv
            dvg = rstd[rows] * (dxh - xhc * _group_mean(dxh * xhc, gm))
            dv_ref[rows, :] = (dvg * _gelu_grad(vc[rows])).astype(dv_ref.dtype)

        @pl.when(i == nstep - 1)
        def _():
            for g in range(SGU_GROUPS):
                dw_ref[g] = jnp.where(tril, dw_ref[g], 0.0)

    wspec = pl.BlockSpec((SGU_GROUPS, SGU_CHUNK, SGU_CHUNK), lambda i: (0, 0, 0))
    bspec = pl.BlockSpec((SGU_CHUNK, 256), lambda i: (0, 0))
    return _pcall(
        body, name=name, grid=(nstep,),
        out_shape=[jax.ShapeDtypeStruct((s, 256), _BF), jax.ShapeDtypeStruct((s, 256), _BF),
                   jax.ShapeDtypeStruct((SGU_GROUPS, SGU_CHUNK, SGU_CHUNK), F32),
                   jax.ShapeDtypeStruct((SGU_CHUNK, 256), F32), jax.ShapeDtypeStruct((1, 256), F32)],
        in_specs=[_row_spec(t, 256, C_UC // 256), _row_spec(t, 256, C_VC // 256), wspec, _vec_spec(256), bspec,
                  _row_spec(t, 256, M_SGU // 256)],
        out_specs=[_row_spec(t, 256), _row_spec(t, 256), wspec, bspec, _vec_spec(256)],
        compiler_params=_cp(ARB),
    )(proj, proj, w, gs, bexp, dmix)


def _ada_fwd(c_all, ada_w, name):
    depth, d, n = ada_w.shape

    def body(c_ref, w_ref, o_ref):
        cv = c_ref[...]
        cond = cv / (1.0 + jnp.exp(-cv))
        o_ref[...] = _dot_f32(cond, w_ref[...], 1, 0)

    return _pcall(
        body, name=name, grid=(depth,), out_shape=jax.ShapeDtypeStruct((depth, N_DEV, n), F32),
        in_specs=[pl.BlockSpec((N_DEV, d), lambda l: (0, 0)), pl.BlockSpec((None, d, n), lambda l: (l, 0, 0))],
        out_specs=pl.BlockSpec((None, N_DEV, n), lambda l: (l, 0, 0)), compiler_params=_cp(PAR),
    )(c_all, ada_w)


def _ada_bwd(c_all, dmod, name):
    depth, _, n = dmod.shape
    d = c_all.shape[1]

    def body(c_ref, dm_ref, o_ref):
        cv = c_ref[...]
        cond = cv / (1.0 + jnp.exp(-cv))
        o_ref[...] = _dot_f32(cond, dm_ref[...], 0, 0)

    return _pcall(
        body, name=name, grid=(depth,), out_shape=jax.ShapeDtypeStruct((depth, d, n), F32),
        in_specs=[pl.BlockSpec((N_DEV, d), lambda l: (0, 0)), pl.BlockSpec((None, N_DEV, n), lambda l: (l, 0, 0))],
        out_specs=pl.BlockSpec((None, d, n), lambda l: (l, 0, 0)), compiler_params=_cp(PAR),
    )(c_all, dmod)


def _adamw(slots, w, m, v, name):
    n, r, c = slots.shape
    tr = 256 if r % 256 == 0 else r
    bc1 = 1.0 - ADAM_B1 ** ADAM_STEP
    bc2 = 1.0 - ADAM_B2 ** ADAM_STEP

    def body(s_ref, w_ref, m_ref, v_ref, g_ref, d_ref, nm_ref, nv_ref):
        g = s_ref[0].astype(F32)
        for j in range(1, n):
            g = g + s_ref[j].astype(F32)
        m_new = ADAM_B1 * m_ref[...] + (1.0 - ADAM_B1) * g
        v_new = ADAM_B2 * v_ref[...] + (1.0 - ADAM_B2) * (g * g)
        g_ref[...] = g
        nm_ref[...] = m_new
        nv_ref[...] = v_new
        d_ref[...] = -ADAM_LR * ((m_new / bc1) / (jnp.sqrt(v_new / bc2) + ADAM_EPS) + ADAM_WD * w_ref[...])

    tile = pl.BlockSpec((tr, c), lambda i: (i, 0))
    shp = jax.ShapeDtypeStruct((r, c), F32)
    return _pcall(
        body, name=name, grid=(r // tr,), out_shape=[shp] * 4,
        in_specs=[pl.BlockSpec((n, tr, c), lambda i: (0, i, 0)), tile, tile, tile],
        out_specs=[tile] * 4, compiler_params=_cp(PAR),
    )(slots, w, m, v)


def _pad_cols(a, width):
    return jnp.pad(a, ((0, 0), (0, width - a.shape[1])))


def _w_in_layout(w):
    pad = jnp.zeros(w.shape[:-1] + (IN_P - IN_W,), w.dtype)
    return jnp.concatenate([w[..., 768:FL_SRC], w[..., :768], w[..., UC_SRC:], w[..., FL_SRC:UC_SRC], pad], axis=-1)


def _w_in_unlayout(g):
    return jnp.concatenate([g[..., C_QA:C_UC], g[..., :C_QA], g[..., C_FL:C_FL + FOX_HEADS], g[..., C_UC:C_FL]], axis=-1)


SMALL = [("ada_b", DEPTH * 6 * D_MODEL), ("norm1_g", DEPTH * D_MODEL), ("norm2_g", DEPTH * D_MODEL),
         ("sgu_w", DEPTH * SGU_GROUPS * SGU_CHUNK * SGU_CHUNK), ("sgu_b", DEPTH * SGU_GROUPS * SGU_CHUNK),
         ("sgu_norm_g", DEPTH * SGU_GROUPS * HEAD_DIM), ("q_norm_g", DEPTH * HEAD_DIM), ("k_norm_g", DEPTH * HEAD_DIM),
         ("b_forget", DEPTH * FOX_HEADS), ("loss", 1)]
SMALL_ROWS = 2560


def _pack_small(parts):
    flat = jnp.concatenate([parts[name].reshape(-1).astype(F32) for name, _ in SMALL])
    return jnp.pad(flat, (0, SMALL_ROWS * 128 - flat.shape[0])).reshape(SMALL_ROWS, 128)


def _unpack_small(packed, shapes):
    flat = packed.reshape(-1)
    out, off = {}, 0
    for name, size in SMALL:
        out[name] = flat[off:off + size].reshape(shapes[name])
        off += size
    return out


def kernel(x, c, ada_w, ada_b, norm1_g, norm2_g, w_in, b_forget, q_norm_g, k_norm_g, sgu_norm_g, sgu_w, sgu_b, w_out, mlp_w1, mlp_w2, loss_target, m_ada_w, m_ada_b, m_norm1_g, m_norm2_g, m_w_in, m_b_forget, m_q_norm_g, m_k_norm_g, m_sgu_norm_g, m_sgu_w, m_sgu_b, m_w_out, m_mlp_w1, m_mlp_w2, v_ada_w, v_ada_b, v_norm1_g, v_norm2_g, v_w_in, v_b_forget, v_q_norm_g, v_k_norm_g, v_sgu_norm_g, v_sgu_w, v_sgu_b, v_w_out, v_mlp_w1, v_mlp_w2):
    me = _lin(_my_pos())
    x0 = x[0]
    target = loss_target[0]
    n_ada = ada_w.shape[2]

    shards = [w.astype(_XBF) for w in (w_in, w_out, mlp_w1, mlp_w2)]

    def whole(w_in_g, w_out_g, w1_g, w2_g):
        return (_w_in_layout(jnp.transpose(w_in_g, (1, 0, 2)).reshape(D_MODEL, IN_W)),
                w_out_g.reshape(D_MODEL, D_MODEL),
                jnp.transpose(w1_g, (1, 0, 2)).reshape(D_MODEL, D_FF),
                w2_g.reshape(D_FF, D_MODEL))

    *first, c_all = _exchange([w[0] for w in shards] + [c], False, "gather_weights")
    c_all = c_all.reshape(N_DEV, D_MODEL)
    weights = [whole(*first)]

    mod_part = _ada_fwd(c_all, ada_w, "ada_fwd")
    (mod_rows,) = _exchange([jnp.transpose(mod_part, (1, 0, 2))], True, "scatter_mod")
    mod = jnp.transpose(mod_rows, (1, 0, 2)).reshape(DEPTH, 6 * D_MODEL) + ada_b
    mods = mod.reshape(DEPTH, 6, 1, D_MODEL)

    saved = []
    xl = x0
    for l in range(DEPTH):
        sh1, sc1, g1, sh2, sc2, g2 = (mods[l, r] for r in range(6))
        n1 = norm1_g[l].reshape(1, D_MODEL)
        n2 = norm2_g[l].reshape(1, D_MODEL)
        gq = jnp.tile(q_norm_g[l], FOX_HEADS).reshape(1, 512)
        gk = jnp.tile(k_norm_g[l], FOX_HEADS).reshape(1, 512)
        bf = jnp.pad(b_forget[l], (0, 128 - FOX_HEADS)).reshape(1, 128)
        gs = sgu_norm_g[l].reshape(1, 256)
        bexp = jnp.repeat(sgu_b[l].T, HEAD_DIM, axis=1)

        w_in_l, w_out_l, w1_l, w2_l = weights[l]
        h1 = _norm_mod(xl, n1, sc1, sh1, "norm_mod")
        proj = _mm_nn(h1, w_in_l, "plain", (), "mm_in")
        o_sb = _sb_fwd(proj, "sb_fwd")
        qn, kn, cfc, cfr, cfe = _fox_prep(proj, gq, gk, bf, "fox_prep")
        qkb = (1.02 * HEAD_DIM ** 0.5 * jnp.max(jnp.abs(q_norm_g[l])) * jnp.max(jnp.abs(k_norm_g[l]))).reshape(1, 1)
        gather_next = None
        if l + 1 < DEPTH:
            gather_next = dict(arrays=[w[l + 1] for w in shards], scatter=False, recv=None, slot=None)
        o_fox, lse, gathered = _fox_fwd(proj, qn, kn, cfc, cfr, cfe, qkb, "fox_fwd", gather_next)
        if gathered:
            weights.append(whole(*gathered))
        o_sgu = _sgu_fwd(proj, sgu_w[l], gs, bexp, "sgu_fwd")
        mixed = jnp.concatenate([o_sb, o_fox, o_sgu], axis=-1).astype(_BF)
        x_mid, y1 = _mm_nn(mixed, w_out_l, "resid", (xl, g1), "mm_out")
        h2 = _norm_mod(x_mid, n2, sc2, sh2, "norm_mod")
        r2, r1 = _mm_nn(h2, w1_l, "relu2", (), "mm_w1")
        x_out, y2 = _mm_nn(r2, w2_l, "resid", (x_mid, g2), "mm_w2")
        saved.append(dict(x_in=xl, h1=h1, proj=proj, o_sb=o_sb, qn=qn, kn=kn, cfc=cfc, cfr=cfr, cfe=cfe, o_fox=o_fox, lse=lse,
                          mixed=mixed, x_mid=x_mid, y1=y1, h2=h2, r2=r2, r1=r1, y2=y2,
                          n1=n1, n2=n2, gq=gq, gk=gk, bf=bf, gs=gs, bexp=bexp, qkb=qkb))
        xl = x_out

    loss_part, dx = _loss_head(xl, target, "loss_head")

    grads_ready = None
    received = [jnp.zeros((N_DEV,) + w.shape, _XBF) for w in shards]
    small = {k: [None] * DEPTH for k in ("mod", "norm1_g", "norm2_g", "sgu_w", "sgu_b", "sgu_norm_g",
                                         "q_norm_g", "k_norm_g", "b_forget")}
    for l in reversed(range(DEPTH)):
        sv = saved[l]
        sh1, sc1, g1, sh2, sc2, g2 = (mods[l, r] for r in range(6))
        w_in_l, w_out_l, w1_l, w2_l = weights[l]
        dy2, dg2 = _res_bwd(dx, sv["y2"], g2, "res_bwd")
        da = _mm_nt(dy2, w2_l, "mul2", (sv["r1"],), _BF, "mm_w2_bwd")
        g_w2 = _mm_tn(sv["r2"], dy2, "rows", "mm_w2_grad")
        dh2 = _mm_nt(da, w1_l, "plain", (), F32, "mm_w1_bwd")
        g_w1 = _mm_tn(sv["h2"], da, "cols", "mm_w1_grad")
        dx_mid, dsh2, dsc2, dn2 = _norm_mod_bwd(sv["x_mid"], dh2, dx, sv["n2"], sc2, "norm_mod_bwd")
        dy1, dg1 = _res_bwd(dx_mid, sv["y1"], g1, "res_bwd")
        dmix = _mm_nt(dy1, w_out_l, "plain", (), F32, "mm_out_bwd")
        g_out = _mm_tn(sv["mixed"], dy1, "rows", "mm_out_grad")
        dq_sb, dk_sb, dv_sb = _sb_bwd(sv["proj"], sv["o_sb"], dmix, "sb_bwd")
        send_prev = None
        if grads_ready is not None:
            send_prev = dict(arrays=grads_ready, scatter=True, recv=received, slot=l + 1)
        dqn, dkn, dv_fox, dcr, filled = _fox_bwd(sv["proj"], sv["qn"], sv["kn"], sv["cfc"], sv["cfr"], sv["cfe"], sv["qkb"],
                                                 sv["o_fox"], sv["lse"], dmix, "fox_bwd", send_prev)
        if filled:
            received = filled
        dlogf = _pad_cols(dcr[:, :2, :].reshape(FOX_HEADS, -1).T, 128)
        dq_fox, dk_fox, dfl, dgq, dgk, dbf = _fox_prep_bwd(sv["proj"], sv["gq"], sv["gk"], sv["bf"], dqn, dkn, dlogf,
                                                           "fox_prep_bwd")
        duc, dvc, dsw, dsb, dsg = _sgu_bwd(sv["proj"], sgu_w[l], sv["gs"], sv["bexp"], dmix, "sgu_bwd")
        dproj = jnp.concatenate(
            [dq_fox, dk_fox, dv_fox.astype(_BF), dq_sb.astype(_BF), dk_sb.astype(_BF), dv_sb.astype(_BF), duc, dvc, dfl,
             jnp.zeros((dfl.shape[0], IN_P - C_FL - 128), _BF)], axis=-1)
        dh1 = _mm_nt(dproj, w_in_l, "plain", (), F32, "mm_in_bwd")
        g_in = _w_in_unlayout(_mm_tn(sv["h1"], dproj, None, "mm_in_grad"))
        g_in = jnp.transpose(g_in.reshape(D_MODEL, N_DEV, IN_W // N_DEV), (1, 0, 2))
        grads_ready = [g_in, g_out, g_w1, g_w2]
        dx, dsh1, dsc1, dn1 = _norm_mod_bwd(sv["x_in"], dh1, dx_mid, sv["n1"], sc1, "norm_mod_bwd")

        small["mod"][l] = jnp.concatenate([dsh1, dsc1, dg1, dsh2, dsc2, dg2], axis=-1).reshape(-1)
        small["norm1_g"][l] = dn1.reshape(-1)
        small["norm2_g"][l] = dn2.reshape(-1)
        small["sgu_w"][l] = dsw
        small["sgu_b"][l] = dsb.reshape(SGU_CHUNK, SGU_GROUPS, HEAD_DIM).sum(-1).T
        small["sgu_norm_g"][l] = dsg.reshape(SGU_GROUPS, HEAD_DIM)
        small["q_norm_g"][l] = dgq.reshape(FOX_HEADS, HEAD_DIM).sum(0)
        small["k_norm_g"][l] = dgk.reshape(FOX_HEADS, HEAD_DIM).sum(0)
        small["b_forget"][l] = dbf[0, :FOX_HEADS]

    parts = {k: jnp.stack(v) for k, v in small.items()}
    parts["ada_b"] = parts.pop("mod")
    parts["loss"] = loss_part
    (small_all,) = _exchange([_pack_small(parts)], False, "gather_small")
    zero1 = jnp.zeros((1,), F32)
    packs = [_pack_small(dict(ada_b=a, norm1_g=b, norm2_g=cc, sgu_w=d, sgu_b=e, sgu_norm_g=f, q_norm_g=g, k_norm_g=h,
                              b_forget=i, loss=zero1))
             for a, b, cc, d, e, f, g, h, i in (
                 (ada_b, norm1_g, norm2_g, sgu_w, sgu_b, sgu_norm_g, q_norm_g, k_norm_g, b_forget),
                 (m_ada_b, m_norm1_g, m_norm2_g, m_sgu_w, m_sgu_b, m_sgu_norm_g, m_q_norm_g, m_k_norm_g, m_b_forget),
                 (v_ada_b, v_norm1_g, v_norm2_g, v_sgu_w, v_sgu_b, v_sgu_norm_g, v_q_norm_g, v_k_norm_g, v_b_forget))]
    small_shapes = dict(ada_b=ada_b.shape, norm1_g=norm1_g.shape, norm2_g=norm2_g.shape, sgu_w=sgu_w.shape,
                        sgu_b=sgu_b.shape, sgu_norm_g=sgu_norm_g.shape, q_norm_g=q_norm_g.shape,
                        k_norm_g=k_norm_g.shape, b_forget=b_forget.shape, loss=())
    s_out = [_unpack_small(a, small_shapes) for a in _adamw(small_all, *packs, "adamw_small")]

    dmod_all = small_all[:, :DEPTH * 6 * D_MODEL // 128, :].reshape(N_DEV, DEPTH, 6 * D_MODEL)
    dmod_mine = lax.dynamic_slice_in_dim(dmod_all, me * n_ada, n_ada, axis=2)
    g_ada = _ada_bwd(c_all, jnp.transpose(dmod_mine, (1, 0, 2)), "ada_bwd")
    ada_out = _adamw(g_ada.reshape(1, DEPTH * D_MODEL, n_ada), ada_w.reshape(-1, n_ada), m_ada_w.reshape(-1, n_ada),
                     v_ada_w.reshape(-1, n_ada), "adamw_ada")
    ada_out = [a.reshape(ada_w.shape) for a in ada_out]

    ri, ro, r1, r2 = _exchange(grads_ready, True, "scatter_grads", recv=received, slot=0)

    def big(slots, w, m, v, name):
        cdim = w.shape[-1]
        outs = _adamw(slots.reshape(N_DEV, -1, cdim), w.reshape(-1, cdim), m.reshape(-1, cdim), v.reshape(-1, cdim), name)
        return [a.reshape(w.shape) for a in outs]

    in_out = big(ri, w_in, m_w_in, v_w_in, "adamw_in")
    out_out = big(ro, w_out, m_w_out, v_w_out, "adamw_out")
    w1_out = big(r1, mlp_w1, m_mlp_w1, v_mlp_w1, "adamw_w1")
    w2_out = big(r2, mlp_w2, m_mlp_w2, v_mlp_w2, "adamw_w2")

    def leaves(idx):
        sm = s_out[idx]
        return [ada_out[idx], sm["ada_b"], sm["norm1_g"], sm["norm2_g"], in_out[idx], sm["b_forget"], sm["q_norm_g"],
                sm["k_norm_g"], sm["sgu_norm_g"], sm["sgu_w"], sm["sgu_b"], out_out[idx], w1_out[idx], w2_out[idx]]

    loss = s_out[0]["loss"]
    grad_x = dx.reshape(x.shape)
    return (loss, grad_x, *leaves(0), *leaves(1), *leaves(2), *leaves(3))
```

```python
import jax
import jax.numpy as jnp
from jax import lax
from jax.experimental import pallas as pl
from jax.experimental.pallas import tpu as pltpu

F32 = jnp.float32
_BF = jnp.bfloat16
_XBF = jnp.bfloat16

N_DEV = 8
D_MODEL = 1024
DEPTH = 4
HEAD_DIM = 64
SB_HEADS = 4
FOX_HEADS = 8
SGU_GROUPS = 4
SGU_CHUNK = 128
D_FF = 4096
EPS = 1e-6
IN_W = 2824
FL_SRC = 2304
UC_SRC = 2312

C_QB, C_KB, C_VB = 0, 512, 1024
C_QA, C_KA, C_VA = 1536, 1792, 2048
C_UC, C_VC = 2304, 2560
C_FL = 2816
IN_P = 3072
M_SB, M_FOX, M_SGU = 0, 256, 768

ADAM_LR = 0.001
ADAM_B1 = 0.9
ADAM_B2 = 0.999
ADAM_EPS = 1e-08
ADAM_WD = 0.01
ADAM_STEP = 10

ATT_T = 256
ROW_T = 512
MM_T = 1024
GRAD_TS = 2048
ROW_CHUNK = 64
NEG = -1e30
UNDERFLOW = -100.0
NO_BLOCK = 1e30

MESH = pl.DeviceIdType.MESH
ARB = pltpu.ARBITRARY
PAR = pltpu.PARALLEL
HIGHEST = lax.Precision.HIGHEST


def _pcall(body, **kw):
    return pl.pallas_call(body, **kw)


def _cp(*sem):
    return pltpu.CompilerParams(dimension_semantics=tuple(sem))


def _dot(a, b, ca, cb):
    return lax.dot_general(a, b, (((ca,), (cb,)), ((), ())), preferred_element_type=F32)


def _dot_f32(a, b, ca, cb):
    return lax.dot_general(a, b, (((ca,), (cb,)), ((), ())), precision=HIGHEST, preferred_element_type=F32)


def _split(v, parts):
    out = []
    r = v
    for _ in range(parts - 1):
        p = r.astype(_BF)
        out.append(p)
        r = r - p.astype(F32)
    out.append(r.astype(_BF))
    return out


def _dot_exact(v, m, ca, cb, parts, v_left=True):
    out = None
    for p in _split(v, parts):
        term = _dot(p, m, ca, cb) if v_left else _dot(m, p, ca, cb)
        out = term if out is None else out + term
    return out


def _iota2(shape, dim):
    return lax.broadcasted_iota(jnp.int32, shape, dim)


def _my_pos():
    return lax.axis_index("x"), lax.axis_index("y"), lax.axis_index("c")


def _flip(pos, k):
    x, y, c = pos
    px = 1 - x if (k >> 2) & 1 else x
    py = 1 - y if (k >> 1) & 1 else y
    pc = 1 - c if k & 1 else c
    return px, py, pc


def _lin(pos):
    return 4 * pos[0] + 2 * pos[1] + pos[2]


def _exchange_ops(ins, outs, sems, scatter, slot):
    send_sems, recv_sems, local_sems = sems
    n = len(ins)

    def copies(with_recvs):
        me = _my_pos()
        me_i = _lin(me)
        dst = lambda a, j: outs[a].at[j] if slot is None else outs[a].at[j, slot]
        mine = lambda a: ins[a].at[me_i] if scatter else ins[a]
        local = [pltpu.make_async_copy(mine(a), dst(a, me_i), local_sems.at[a]) for a in range(n)]
        sends, recvs = [], []
        for a in range(n):
            for k in range(1, N_DEV):
                peer = _flip(me, k)
                pair = dict(send_sem=send_sems.at[a, k - 1], recv_sem=recv_sems.at[a, k - 1],
                            device_id=peer, device_id_type=MESH)
                src = ins[a].at[_lin(peer)] if scatter else ins[a]
                sends.append(pltpu.make_async_remote_copy(src_ref=src, dst_ref=dst(a, me_i), **pair))
                if with_recvs:
                    recvs.append(pltpu.make_async_remote_copy(src_ref=mine(a), dst_ref=dst(a, _lin(peer)), **pair))
        return local, sends, recvs

    def start():
        local, sends, _ = copies(False)
        for cp in local + sends:
            cp.start()

    def wait():
        local, sends, recvs = copies(True)
        for snd, rcv in zip(sends, recvs):
            snd.wait_send()
            rcv.wait_recv()
        for cp in local:
            cp.wait()

    return start, wait


def _comm_operands(comm):
    arrays, recv = comm["arrays"], comm["recv"]
    n = len(arrays)
    any_spec = pl.BlockSpec(memory_space=pl.ANY)
    if recv is not None:
        out_shapes = [jax.ShapeDtypeStruct(r.shape, r.dtype) for r in recv]
    elif comm["scatter"]:
        out_shapes = [jax.ShapeDtypeStruct(a.shape, a.dtype) for a in arrays]
    else:
        out_shapes = [jax.ShapeDtypeStruct((N_DEV,) + a.shape, a.dtype) for a in arrays]
    operands = list(arrays) + (list(recv) if recv is not None else [])
    sems = [pltpu.SemaphoreType.DMA((n, N_DEV - 1)), pltpu.SemaphoreType.DMA((n, N_DEV - 1)),
            pltpu.SemaphoreType.DMA((n,))]
    return operands, [any_spec] * len(operands), out_shapes, [any_spec] * n, sems


def _with_comm(body, comm, n_in, n_out, first_step, last_step):
    if comm is None:
        return body
    n = len(comm["arrays"])
    n_cin = n if comm["recv"] is None else 2 * n

    def wrapped(*refs):
        ins, cins = refs[:n_in], refs[n_in:n_in + n]
        o0 = n_in + n_cin
        outs, couts, sems = refs[o0:o0 + n_out], refs[o0 + n_out:o0 + n_out + n], refs[o0 + n_out + n:]
        start, wait = _exchange_ops(cins, couts, sems, comm["scatter"], comm["slot"])
        pl.when(first_step())(start)
        body(*ins, *outs)
        pl.when(last_step())(wait)

    return wrapped


def _comm_aliases(comm, n_in, n_out):
    if comm is None or comm["recv"] is None:
        return {}
    n = len(comm["arrays"])
    return {n_in + n + a: n_out + a for a in range(n)}


def _exchange(arrays, scatter, name, recv=None, slot=None):
    comm = dict(arrays=arrays, scatter=scatter, recv=recv, slot=slot)
    operands, in_specs, out_shapes, out_specs, sems = _comm_operands(comm)
    n = len(arrays)

    def body(*refs):
        n_cin = len(operands)
        start, wait = _exchange_ops(refs[:n], refs[n_cin:n_cin + n], refs[n_cin + n:], scatter, slot)
        start()
        wait()

    outs = _pcall(body, name=name, out_shape=out_shapes, in_specs=in_specs, out_specs=out_specs, scratch_shapes=sems,
                  input_output_aliases=_comm_aliases(comm, 0, 0))(*operands)
    return list(outs)


def _row_spec(t, w, col=0):
    return pl.BlockSpec((t, w), lambda i: (i, col))


def _vec_spec(w, col=0):
    return pl.BlockSpec((1, w), lambda i: (0, col))


def _norm_mod(x, g, sc, sh, name):
    s, d = x.shape
    t = min(ROW_T, s)

    def body(x_ref, g_ref, sc_ref, sh_ref, h_ref, ht_ref):
        xf = x_ref[...]
        rstd = lax.rsqrt(jnp.mean(xf * xf, axis=-1, keepdims=True) + EPS)
        y = xf * rstd * g_ref[...]
        h = y * (1.0 + sc_ref[...]) + sh_ref[...]
        h_ref[...] = h.astype(h_ref.dtype)
        ht_ref[...] = h.T.astype(ht_ref.dtype)

    return _pcall(
        body, name=name, grid=(s // t,),
        out_shape=[jax.ShapeDtypeStruct((s, d), _BF), jax.ShapeDtypeStruct((d, s), _BF)],
        in_specs=[_row_spec(t, d), _vec_spec(d), _vec_spec(d), _vec_spec(d)],
        out_specs=[_row_spec(t, d), pl.BlockSpec((d, t), lambda i: (0, i))], compiler_params=_cp(PAR),
    )(x, g, sc, sh)


def _res_bwd_step(i, dx, y_ref, gate_ref, dy_ref, dgate_ref):
    dy_ref[...] = (dx * gate_ref[...]).astype(dy_ref.dtype)

    @pl.when(i == 0)
    def _():
        dgate_ref[...] = jnp.zeros_like(dgate_ref)

    dgate_ref[...] += jnp.sum(dx * y_ref[...], axis=0, keepdims=True)


def _norm_mod_bwd(x, dh, dres, g, sc, branch, name):
    s, d = x.shape
    t = min(ROW_T, s)
    n_branch = 0 if branch is None else 2

    ch = min(ROW_CHUNK, t)

    def body(x_ref, dh_ref, dres_ref, g_ref, sc_ref, *rest):
        dx_ref, dsh_ref, dsc_ref, dg_ref = rest[n_branch:n_branch + 4]
        i = pl.program_id(0)
        gv = g_ref[...]
        sc1 = 1.0 + sc_ref[...]
        n_sum = 3 if branch is None else 4

        def rows(r, sums):
            sl = pl.ds(pl.multiple_of(r * ch, ch), ch)
            xf = x_ref[sl, :]
            dh = dh_ref[sl, :]
            rstd = lax.rsqrt(jnp.mean(xf * xf, axis=-1, keepdims=True) + EPS)
            xhat = xf * rstd
            dn = dh * sc1
            dxh = dn * gv
            dx = dres_ref[sl, :] + rstd * (dxh - xhat * jnp.mean(dxh * xhat, axis=-1, keepdims=True))
            dx_ref[sl, :] = dx
            new = [sums[0] + jnp.sum(dh, axis=0, keepdims=True),
                   sums[1] + jnp.sum(dh * (xhat * gv), axis=0, keepdims=True),
                   sums[2] + jnp.sum(dn * xhat, axis=0, keepdims=True)]
            if branch is not None:
                rest[n_branch + 4][sl, :] = (dx * rest[1][...]).astype(_BF)
                new.append(sums[3] + jnp.sum(dx * rest[0][sl, :], axis=0, keepdims=True))
            return tuple(new)

        sums = lax.fori_loop(0, t // ch, rows, tuple(jnp.zeros((1, d), F32) for _ in range(n_sum)))
        totals = [dsh_ref, dsc_ref, dg_ref] + ([rest[n_branch + 5]] if branch is not None else [])

        @pl.when(i == 0)
        def _():
            for ref in totals:
                ref[...] = jnp.zeros_like(ref)

        for ref, part in zip(totals, sums):
            ref[...] += part

    vec = jax.ShapeDtypeStruct((1, d), F32)
    out_shape = [jax.ShapeDtypeStruct((s, d), F32), vec, vec, vec]
    in_specs = [_row_spec(t, d), _row_spec(t, d), _row_spec(t, d), _vec_spec(d), _vec_spec(d)]
    out_specs = [_row_spec(t, d), _vec_spec(d), _vec_spec(d), _vec_spec(d)]
    if branch is not None:
        in_specs += [_row_spec(t, d), _vec_spec(d)]
        out_shape += [jax.ShapeDtypeStruct((s, d), _BF), vec]
        out_specs += [_row_spec(t, d), _vec_spec(d)]
    return _pcall(
        body, name=name, grid=(s // t,), out_shape=out_shape, in_specs=in_specs, out_specs=out_specs,
        compiler_params=_cp(ARB),
    )(x, dh, dres, g, sc, *(branch or ()))


def _loss_head(y, target, branch, name):
    s, d = y.shape
    t = min(ROW_T, s)

    def body(y_ref, t_ref, yb_ref, gate_ref, loss_ref, dy_ref, dyb_ref, dgate_ref):
        i = pl.program_id(0)
        diff = y_ref[...] - t_ref[...]
        dy = diff * (1.0 / d)
        dy_ref[...] = dy

        @pl.when(i == 0)
        def _():
            loss_ref[...] = jnp.zeros_like(loss_ref)

        rows = jnp.sum(diff * diff, axis=-1, keepdims=True)
        loss_ref[...] += (0.5 / d) * jnp.sum(rows, axis=0, keepdims=True)
        _res_bwd_step(i, dy, yb_ref, gate_ref, dyb_ref, dgate_ref)

    return _pcall(
        body, name=name, grid=(s // t,),
        out_shape=[jax.ShapeDtypeStruct((1, 1), F32), jax.ShapeDtypeStruct((s, d), F32),
                   jax.ShapeDtypeStruct((s, d), _BF), jax.ShapeDtypeStruct((1, d), F32)],
        in_specs=[_row_spec(t, d), _row_spec(t, d), _row_spec(t, d), _vec_spec(d)],
        out_specs=[pl.BlockSpec((1, 1), lambda i: (0, 0)), _row_spec(t, d), _row_spec(t, d), _vec_spec(d)],
        compiler_params=_cp(ARB),
    )(y, target, *branch)


def _mm_nn(a, b, epi, extras, name):
    m, kdim = a.shape
    n = b.shape[1]
    tm, tn, tk = min(MM_T, m), min(MM_T, n), min(MM_T, kdim)
    nk = kdim // tk
    n_extra = len(extras)
    n_out = {"plain": 1, "resid": 2, "relu2": 3}[epi]

    def finish(y, extra_refs, out_refs):
        if epi == "plain":
            out_refs[0][...] = y.astype(out_refs[0].dtype)
        elif epi == "resid":
            x_ref, g_ref = extra_refs
            out_refs[0][...] = x_ref[...] + g_ref[...] * y
            out_refs[1][...] = y
        else:
            r = jnp.maximum(y, 0.0)
            out_refs[0][...] = (r * r).astype(out_refs[0].dtype)
            out_refs[1][...] = r.astype(out_refs[1].dtype)
            out_refs[2][...] = (r * r).T.astype(out_refs[2].dtype)

    def body(a_ref, b_ref, *rest):
        extra_refs = rest[:n_extra]
        out_refs = rest[n_extra:n_extra + n_out]
        part = _dot(a_ref[...].astype(_BF), b_ref[...].astype(_BF), 1, 0)
        if nk == 1:
            finish(part, extra_refs, out_refs)
        else:
            acc_ref = rest[-1]
            k = pl.program_id(2)

            @pl.when(k == 0)
            def _():
                acc_ref[...] = part

            @pl.when(k > 0)
            def _():
                acc_ref[...] += part

            @pl.when(k == nk - 1)
            def _():
                finish(acc_ref[...], extra_refs, out_refs)

    tile = pl.BlockSpec((tm, tn), lambda i, j, k: (i, j))
    in_specs = [pl.BlockSpec((tm, tk), lambda i, j, k: (i, k)), pl.BlockSpec((tk, tn), lambda i, j, k: (k, j))]
    if epi == "plain":
        out_shape = [jax.ShapeDtypeStruct((m, n), F32)]
    elif epi == "resid":
        in_specs += [tile, pl.BlockSpec((1, tn), lambda i, j, k: (0, j))]
        out_shape = [jax.ShapeDtypeStruct((m, n), F32)] * 2
    out_specs = [tile] * n_out
    if epi == "relu2":
        out_shape = [jax.ShapeDtypeStruct((m, n), _BF), jax.ShapeDtypeStruct((m, n), F32),
                     jax.ShapeDtypeStruct((n, m), _BF)]
        out_specs[2] = pl.BlockSpec((tn, tm), lambda i, j, k: (j, i))
    outs = _pcall(
        body, name=name, grid=(m // tm, n // tn, nk), out_shape=out_shape,
        in_specs=in_specs, out_specs=out_specs,
        scratch_shapes=[pltpu.VMEM((tm, tn), F32)] if nk > 1 else [],
        compiler_params=_cp(PAR, PAR, ARB),
    )(a, b, *extras)
    return outs[0] if n_out == 1 else outs


def _mm_nt(a, b, epi, extras, out_dtype, name):
    m, kdim = a.shape
    n = b.shape[0]
    tm, tn, tk = min(MM_T, m), min(MM_T, n), min(MM_T, kdim)
    nk = kdim // tk
    n_extra = len(extras)

    def finish(y, extra_refs, o_ref):
        if epi == "mul2":
            y = y * (2.0 * extra_refs[0][...].astype(F32))
        o_ref[...] = y.astype(o_ref.dtype)

    def body(a_ref, b_ref, *rest):
        extra_refs = rest[:n_extra]
        o_ref = rest[n_extra]
        part = _dot(a_ref[...].astype(_BF), b_ref[...].astype(_BF), 1, 1)
        if nk == 1:
            finish(part, extra_refs, o_ref)
        else:
            acc_ref = rest[-1]
            k = pl.program_id(2)

            @pl.when(k == 0)
            def _():
                acc_ref[...] = part

            @pl.when(k > 0)
            def _():
                acc_ref[...] += part

            @pl.when(k == nk - 1)
            def _():
                finish(acc_ref[...], extra_refs, o_ref)

    tile = pl.BlockSpec((tm, tn), lambda i, j, k: (i, j))
    in_specs = [pl.BlockSpec((tm, tk), lambda i, j, k: (i, k)), pl.BlockSpec((tn, tk), lambda i, j, k: (j, k))]
    in_specs += [tile] * n_extra
    return _pcall(
        body, name=name, grid=(m // tm, n // tn, nk), out_shape=jax.ShapeDtypeStruct((m, n), out_dtype),
        in_specs=in_specs, out_specs=tile,
        scratch_shapes=[pltpu.VMEM((tm, tn), F32)] if nk > 1 else [],
        compiler_params=_cp(PAR, PAR, ARB),
    )(a, b, *extras)


def _mm_grad(at, b, split, name):
    m, s = at.shape
    n = b.shape[1]
    ts = min(GRAD_TS, s)
    ns = s // ts
    if split == "rows":
        tm, tn = min(MM_T, m // N_DEV), min(MM_T, n)
        per = (m // N_DEV) // tm
        out_shape = (N_DEV, m // N_DEV, n)
        out_spec = pl.BlockSpec((None, tm, tn), lambda i, j, k: (i // per, i % per, j))
    elif split == "cols":
        tm, tn = min(MM_T, m), min(MM_T, n // N_DEV)
        per = (n // N_DEV) // tn
        out_shape = (N_DEV, m, n // N_DEV)
        out_spec = pl.BlockSpec((None, tm, tn), lambda i, j, k: (j // per, i, j % per))
    else:
        tm, tn = min(MM_T, m), min(MM_T, n)
        out_shape = (m, n)
        out_spec = pl.BlockSpec((tm, tn), lambda i, j, k: (i, j))

    def body(a_ref, b_ref, o_ref, acc_ref):
        k = pl.program_id(2)
        part = _dot(a_ref[...].astype(_BF), b_ref[...].astype(_BF), 1, 0)

        @pl.when(k == 0)
        def _():
            acc_ref[...] = part

        @pl.when(k > 0)
        def _():
            acc_ref[...] += part

        @pl.when(k == ns - 1)
        def _():
            o_ref[...] = acc_ref[...].astype(o_ref.dtype)

    return _pcall(
        body, name=name, grid=(m // tm, n // tn, ns), out_shape=jax.ShapeDtypeStruct(out_shape, _XBF),
        in_specs=[pl.BlockSpec((tm, ts), lambda i, j, k: (i, k)), pl.BlockSpec((ts, tn), lambda i, j, k: (k, j))],
        out_specs=out_spec, scratch_shapes=[pltpu.VMEM((tm, tn), F32)],
        compiler_params=_cp(PAR, PAR, ARB),
    )(at, b)


def _head_masks():
    lane = _iota2((1, 128), 1)
    return [(lane // HEAD_DIM) == hh for hh in range(2)]


def _softplus_parts(z):
    sp = jnp.maximum(z, 0.0) + jnp.log(1.0 + jnp.exp(-jnp.abs(z)))
    return -sp, z - sp


def _sb_key_loop(i, carry, step, lookahead, first_ahead):
    def cond(c):
        return jnp.logical_and(c[0] <= i, jnp.max(jnp.maximum(c[1][0][0], c[1][0][1])) > UNDERFLOW)

    def body(c):
        ahead = lookahead(i - c[0] - 1)
        return c[0] + 1, step(i - c[0], c[1], c[2]), ahead

    return lax.while_loop(cond, body, (jnp.int32(1), carry, first_ahead))[1]


def _sb_fwd(proj, name):
    s = proj.shape[0]
    t = min(ATT_T, s)
    nblk = s // t
    scale = HEAD_DIM ** -0.5

    def body(q_ref, k_ref, v_ref, o_ref):
        i = pl.program_id(1)
        masks = _head_masks()
        row = _iota2((t, t), 0)
        col = _iota2((t, t), 1)
        u_strict = (row > col).astype(_BF)
        causal = col < row
        qs = q_ref[...] * scale
        qm = [jnp.where(masks[hh], qs, 0.0).astype(_BF) for hh in range(2)]

        def scores(j):
            kb = k_ref[pl.ds(pl.multiple_of(jnp.maximum(j, 0) * t, t), t), :].astype(_BF)
            return tuple(_dot(qm[hh], kb, 1, 1) for hh in range(2))

        def block(j, carry, qk, diag):
            off = pl.multiple_of(j * t, t)
            vb = v_ref[pl.ds(off, t), :].astype(_BF)
            runs, outs = [], []
            for hh in range(2):
                run, o = carry[0][hh], carry[1][hh]
                z = qk[hh]
                l1, lb = _softplus_parts(z)
                if diag:
                    l1 = jnp.where(causal, l1, 0.0)
                between = run + _dot_exact(l1, u_strict, 1, 0, 2)
                a = jnp.exp(lb + between)
                if diag:
                    a = jnp.where(causal, a, 0.0)
                outs.append(o + _dot_exact(a, vb, 1, 0, 2))
                runs.append(run + jnp.sum(l1, axis=-1, keepdims=True))
            return tuple(runs), tuple(outs)

        zero, zero_o = jnp.zeros((t, 1), F32), jnp.zeros((t, 128), F32)
        ahead = scores(i - 1)
        carry = block(i, ((zero, zero), (zero_o, zero_o)), scores(i), True)
        carry = _sb_key_loop(i, carry, lambda j, c, qk: block(j, c, qk, False), scores, ahead)
        o_ref[...] = jnp.where(masks[0], carry[1][0], carry[1][1])

    kv_spec = lambda c0: pl.BlockSpec((s, 128), lambda p, i: (0, c0 // 128 + p))
    return _pcall(
        body, name=name, grid=(SB_HEADS // 2, nblk), out_shape=jax.ShapeDtypeStruct((s, 256), F32),
        in_specs=[pl.BlockSpec((t, 128), lambda p, i: (i, C_QA // 128 + p)), kv_spec(C_KA), kv_spec(C_VA)],
        out_specs=pl.BlockSpec((t, 128), lambda p, i: (i, p)), compiler_params=_cp(PAR, ARB),
    )(proj, proj, proj)


def _sb_bwd(proj, o, dmix, name):
    s = proj.shape[0]
    t = min(ATT_T, s)
    nblk = s // t
    scale = HEAD_DIM ** -0.5

    def body(q_ref, k_ref, v_ref, o_ref, do_ref, dq_ref, dk_ref, dv_ref):
        i = pl.program_id(1)
        masks = _head_masks()
        row = _iota2((t, t), 0)
        col = _iota2((t, t), 1)
        u_strict = (row > col).astype(_BF)
        u_incl = (row >= col).astype(_BF)
        causal = col < row

        @pl.when(i == 0)
        def _():
            dk_ref[...] = jnp.zeros_like(dk_ref)
            dv_ref[...] = jnp.zeros_like(dv_ref)

        qs = q_ref[...] * scale
        dov = do_ref[...]
        ov = o_ref[...]
        qm = [jnp.where(masks[hh], qs, 0.0).astype(_BF) for hh in range(2)]
        dom = [jnp.where(masks[hh], dov, 0.0).astype(_BF) for hh in range(2)]
        dsum = [jnp.sum(dom[hh].astype(F32) * ov, axis=-1, keepdims=True) for hh in range(2)]

        def scores(j):
            off = pl.multiple_of(jnp.maximum(j, 0) * t, t)
            kb = k_ref[pl.ds(off, t), :].astype(_BF)
            vb = v_ref[pl.ds(off, t), :].astype(_BF)
            return tuple((_dot(qm[hh], kb, 1, 1), _dot(dom[hh], vb, 1, 1)) for hh in range(2))

        def block(j, carry, ahead, diag):
            off = pl.multiple_of(j * t, t)
            kb = k_ref[pl.ds(off, t), :].astype(_BF)
            runs, eruns, dqs = [], [], []
            dk = dv = None
            for hh in range(2):
                run, erun, dq = carry[0][hh], carry[1][hh], carry[2][hh]
                z, da = ahead[hh]
                l1, lb = _softplus_parts(z)
                if diag:
                    l1 = jnp.where(causal, l1, 0.0)
                between = run + _dot_exact(l1, u_strict, 1, 0, 2)
                a = jnp.exp(lb + between)
                if diag:
                    a = jnp.where(causal, a, 0.0)
                g = a * da
                cum = dsum[hh] - (erun + _dot_exact(g, u_incl, 1, 0, 2))
                beta = jnp.exp(lb)
                dz = g * (1.0 - beta) - cum * beta
                if diag:
                    dz = jnp.where(causal, dz, 0.0)
                dzb = dz.astype(_BF)
                dqs.append(dq + _dot(dzb, kb, 1, 0))
                dk_h = _dot(dzb, qm[hh], 0, 0)
                dv_h = _dot(a.astype(_BF), dom[hh], 0, 0)
                dk = dk_h if dk is None else dk + dk_h
                dv = dv_h if dv is None else dv + dv_h
                runs.append(run + jnp.sum(l1, axis=-1, keepdims=True))
                eruns.append(erun + jnp.sum(g, axis=-1, keepdims=True))
            dk_ref[pl.ds(off, t), :] += dk
            dv_ref[pl.ds(off, t), :] += dv
            return tuple(runs), tuple(eruns), tuple(dqs)

        zero, zero_q = jnp.zeros((t, 1), F32), jnp.zeros((t, 128), F32)
        ahead = scores(i - 1)
        carry = block(i, ((zero, zero), (zero, zero), (zero_q, zero_q)), scores(i), True)
        carry = _sb_key_loop(i, carry, lambda j, c, a: block(j, c, a, False), scores, ahead)
        dq_ref[...] = jnp.where(masks[0], carry[2][0], carry[2][1]) * scale

    kv_spec = lambda c0: pl.BlockSpec((s, 128), lambda p, i: (0, c0 // 128 + p))
    blk = lambda c0: pl.BlockSpec((t, 128), lambda p, i: (i, c0 // 128 + p))
    acc = pl.BlockSpec((s, 128), lambda p, i: (0, p))
    shp = jax.ShapeDtypeStruct((s, 256), F32)
    return _pcall(
        body, name=name, grid=(SB_HEADS // 2, nblk), out_shape=[shp, shp, shp],
        in_specs=[blk(C_QA), kv_spec(C_KA), kv_spec(C_VA), blk(0), blk(M_SB)],
        out_specs=[blk(0), acc, acc], compiler_params=_cp(PAR, ARB),
    )(proj, proj, proj, o, dmix)


def _group_matrix(width):
    r = _iota2((width, width), 0) // HEAD_DIM
    c = _iota2((width, width), 1) // HEAD_DIM
    return (r == c).astype(_BF)


def _group_mean(v, gm):
    return _dot_exact(v, gm, 1, 0, 2) * (1.0 / HEAD_DIM)


def _fox_prep(proj, gq, gk, bf, name):
    s = proj.shape[0]
    t = min(ATT_T, s)
    nblk = s // t
    assert nblk <= 128

    def body(q_ref, k_ref, fl_ref, gq_ref, gk_ref, bf_ref, qn_ref, kn_ref, cfc_ref, cfr_ref, cfe_ref, carry_ref):
        i = pl.program_id(0)
        gm = _group_matrix(512)
        for src, gref, dst in ((q_ref, gq_ref, qn_ref), (k_ref, gk_ref, kn_ref)):
            v = src[...]
            rstd = lax.rsqrt(_group_mean(v * v, gm) + EPS)
            dst[...] = (v * rstd * gref[...]).astype(dst.dtype)

        @pl.when(i == 0)
        def _():
            carry_ref[...] = jnp.zeros_like(carry_ref)
            cfe_ref[...] = jnp.full(cfe_ref.shape, NO_BLOCK, F32)

        lane = _iota2((1, 128), 1)
        logf = jnp.where(lane < FOX_HEADS, _softplus_parts(-(fl_ref[...] + bf_ref[...]))[0], 0.0)
        lower = (_iota2((t, t), 0) >= _iota2((t, t), 1)).astype(_BF)
        cf = carry_ref[...] + _dot_exact(logf, lower, 1, 0, 3, v_left=False)
        cfc_ref[...] = cf
        rows = cf.T[:FOX_HEADS, :]
        cfr_ref[...] = rows
        cfe_ref[...] = jnp.where(lane == i, rows[:, t - 1:t], cfe_ref[...])
        carry_ref[...] = cf[t - 1:t, :]

    col = lambda w, c0: pl.BlockSpec((t, w), lambda i: (i, c0 // w))
    v512 = pl.BlockSpec((1, 512), lambda i: (0, 0))
    return _pcall(
        body, name=name, grid=(nblk,),
        out_shape=[jax.ShapeDtypeStruct((s, 512), _BF), jax.ShapeDtypeStruct((s, 512), _BF),
                   jax.ShapeDtypeStruct((s, 128), F32), jax.ShapeDtypeStruct((FOX_HEADS, s), F32),
                   jax.ShapeDtypeStruct((FOX_HEADS, 128), F32)],
        in_specs=[col(512, C_QB), col(512, C_KB), col(128, C_FL), v512, v512, pl.BlockSpec((1, 128), lambda i: (0, 0))],
        out_specs=[col(512, 0), col(512, 0), col(128, 0), pl.BlockSpec((FOX_HEADS, t), lambda i: (0, i)),
                   pl.BlockSpec((FOX_HEADS, 128), lambda i: (0, 0))],
        scratch_shapes=[pltpu.VMEM((1, 128), F32)], compiler_params=_cp(ARB),
    )(proj, proj, proj, gq, gk, bf)


def _fox_cq(cfc_blk, head):
    lane = _iota2((1, 128), 1)
    return jnp.sum(jnp.where(lane == head, cfc_blk, 0.0), axis=-1, keepdims=True)


def _fox_ck(cfr_ref, off, t, head):
    sub = _iota2((FOX_HEADS, 1), 0)
    return jnp.sum(jnp.where(sub == head, cfr_ref[:, pl.ds(off, t)], 0.0), axis=0, keepdims=True)


def _fox_blocks_needed(top, cfe_ref, head, i):
    sub = _iota2((FOX_HEADS, 1), 0)
    lane = _iota2((1, 128), 1)
    last_key = jnp.sum(jnp.where(sub == head, cfe_ref[...], 0.0), axis=0, keepdims=True)
    need = jnp.logical_and(lane < i, top - last_key > UNDERFLOW)
    return jnp.sum(need.astype(jnp.int32))


def _grid_ends(n_outer, n_inner):
    first = lambda: jnp.logical_and(pl.program_id(0) == 0, pl.program_id(1) == 0)
    last = lambda: jnp.logical_and(pl.program_id(0) == n_outer - 1, pl.program_id(1) == n_inner - 1)
    return first, last


def _fox_fwd(proj, qn, kn, cfc, cfr, cfe, qk_bound, name, comm=None):
    s = proj.shape[0]
    t = min(ATT_T, s)
    nblk = s // t
    scale = HEAD_DIM ** -0.5

    def body(q_ref, k_ref, v_ref, cfc_ref, cfr_ref, cfe_ref, qkb_ref, o_ref, lse_ref):
        p = pl.program_id(0)
        i = pl.program_id(1)
        masks = _head_masks()
        lane = _iota2((1, 128), 1)
        valid = _iota2((t, t), 1) <= _iota2((t, t), 0)
        qv = q_ref[...]
        cfc_blk = cfc_ref[...]
        heads = [2 * p + hh for hh in range(2)]
        qm = [(jnp.where(masks[hh], qv, 0) * scale).astype(_BF) for hh in range(2)]
        cq = [_fox_cq(cfc_blk, heads[hh]) for hh in range(2)]

        def scores(j):
            kb = k_ref[pl.ds(pl.multiple_of(jnp.maximum(j, 0) * t, t), t), :]
            return tuple(_dot(qm[hh], kb, 1, 1) for hh in range(2))

        def block(j, carry, qk, diag):
            off = pl.multiple_of(j * t, t)
            vb = v_ref[pl.ds(off, t), :].astype(_BF)
            ms, ls, accs = [], [], []
            for hh in range(2):
                m, l, acc = carry[0][hh], carry[1][hh], carry[2][hh]
                z = qk[hh] + (cq[hh] - _fox_ck(cfr_ref, off, t, heads[hh]))
                if diag:
                    z = jnp.where(valid, z, NEG)
                m_new = jnp.maximum(m, jnp.max(z, axis=-1, keepdims=True))
                pe = jnp.exp(z - m_new)
                alpha = jnp.exp(m - m_new)
                ms.append(m_new)
                ls.append(alpha * l + jnp.sum(pe, axis=-1, keepdims=True))
                accs.append(alpha * acc + _dot_exact(pe, vb, 1, 0, 2))
            return tuple(ms), tuple(ls), tuple(accs)

        def step(jj, c):
            ahead = scores(i - jj - 2)
            return block(i - jj - 1, c[0], c[1], False), ahead

        neg, zero, zero_o = jnp.full((t, 1), NEG, F32), jnp.zeros((t, 1), F32), jnp.zeros((t, 128), F32)
        ahead = scores(i - 1)
        carry = block(i, ((neg, neg), (zero, zero), (zero_o, zero_o)), scores(i), True)
        needed = [_fox_blocks_needed(jnp.max(qkb_ref[...] + cq[hh] - carry[0][hh]), cfe_ref, heads[hh], i)
                  for hh in range(2)]
        m, l, acc = lax.fori_loop(0, jnp.maximum(needed[0], needed[1]), step, (carry, ahead))[0]
        o_ref[...] = jnp.where(masks[0], acc[0] / l[0], acc[1] / l[1])
        lse_ref[...] = jnp.where(lane == 0, m[0] + jnp.log(l[0]), jnp.where(lane == 1, m[1] + jnp.log(l[1]), 0.0))

    c_operands, c_in_specs, c_out_shapes, c_out_specs, c_sems = _comm_operands(comm) if comm else ([], [], [], [], [])
    outs = _pcall(
        _with_comm(body, comm, 7, 2, *_grid_ends(FOX_HEADS // 2, nblk)), name=name, grid=(FOX_HEADS // 2, nblk),
        out_shape=[jax.ShapeDtypeStruct((s, 512), F32), jax.ShapeDtypeStruct((FOX_HEADS // 2, s, 128), F32)] + c_out_shapes,
        in_specs=[pl.BlockSpec((t, 128), lambda p, i: (i, p)), pl.BlockSpec((s, 128), lambda p, i: (0, p)),
                  pl.BlockSpec((s, 128), lambda p, i: (0, C_VB // 128 + p)),
                  pl.BlockSpec((t, 128), lambda p, i: (i, 0)), pl.BlockSpec((FOX_HEADS, s), lambda p, i: (0, 0)),
                  pl.BlockSpec((FOX_HEADS, 128), lambda p, i: (0, 0)), pl.BlockSpec((1, 1), lambda p, i: (0, 0))] + c_in_specs,
        out_specs=[pl.BlockSpec((t, 128), lambda p, i: (i, p)),
                   pl.BlockSpec((None, t, 128), lambda p, i: (p, i, 0))] + c_out_specs,
        scratch_shapes=c_sems, input_output_aliases=_comm_aliases(comm, 7, 2),
        compiler_params=_cp(ARB, ARB),
    )(qn, kn, proj, cfc, cfr, cfe, qk_bound, *c_operands)
    return outs[0], outs[1], list(outs[2:])


def _fox_bwd(proj, qn, kn, cfc, cfr, cfe, qk_bound, o, lse, dmix, name, comm=None):
    s = proj.shape[0]
    t = min(ATT_T, s)
    nblk = s // t
    scale = HEAD_DIM ** -0.5

    def body(q_ref, k_ref, v_ref, cfc_ref, cfr_ref, cfe_ref, qkb_ref, o_ref, lse_ref, do_ref,
             dq_ref, dk_ref, dv_ref, dcr_ref):
        p = pl.program_id(0)
        i = pl.program_id(1)
        masks = _head_masks()
        lane = _iota2((1, 128), 1)
        valid = _iota2((t, t), 1) <= _iota2((t, t), 0)
        before = (_iota2((t, t), 0) < _iota2((t, t), 1)).astype(_BF)

        @pl.when(i == 0)
        def _():
            dk_ref[...] = jnp.zeros_like(dk_ref)
            dv_ref[...] = jnp.zeros_like(dv_ref)
            dcr_ref[...] = jnp.zeros_like(dcr_ref)

        qv = q_ref[...]
        cfc_blk = cfc_ref[...]
        dov = do_ref[...]
        ov = o_ref[...]
        lse_blk = lse_ref[...]
        heads = [2 * p + hh for hh in range(2)]
        qm = [(jnp.where(masks[hh], qv, 0) * scale).astype(_BF) for hh in range(2)]
        dom = [jnp.where(masks[hh], dov, 0.0).astype(_BF) for hh in range(2)]
        dsum = [jnp.sum(dom[hh].astype(F32) * ov, axis=-1, keepdims=True) for hh in range(2)]
        lse_h = [jnp.sum(jnp.where(lane == hh, lse_blk, 0.0), axis=-1, keepdims=True) for hh in range(2)]
        cq = [_fox_cq(cfc_blk, heads[hh]) for hh in range(2)]
        needed = [_fox_blocks_needed(jnp.max(qkb_ref[...] + cq[hh] - lse_h[hh]), cfe_ref, heads[hh], i)
                  for hh in range(2)]
        first = i - jnp.maximum(needed[0], needed[1])

        def scores(j):
            off = pl.multiple_of(j * t, t)
            kb = k_ref[pl.ds(off, t), :]
            vb = v_ref[pl.ds(off, t), :].astype(_BF)
            return tuple((_dot(qm[hh], kb, 1, 1), _dot(dom[hh], vb, 1, 1)) for hh in range(2))

        def block(j, carry, ahead, diag):
            off = pl.multiple_of(j * t, t)
            kb = k_ref[pl.ds(off, t), :]
            dqs, pres = [], []
            dk = dv = None
            for hh in range(2):
                dq, pre = carry[0][hh], carry[1][hh]
                z = ahead[hh][0] + (cq[hh] - _fox_ck(cfr_ref, off, t, heads[hh]))
                pm = jnp.exp(z - lse_h[hh])
                if diag:
                    pm = jnp.where(valid, pm, 0.0)
                ds = pm * (ahead[hh][1] - dsum[hh])
                dsb = ds.astype(_BF)
                dqs.append(dq + _dot(dsb, kb, 1, 0))
                dk_h = _dot(dsb, qm[hh], 0, 0)
                dv_h = _dot(pm.astype(_BF), dom[hh], 0, 0)
                dk = dk_h if dk is None else dk + dk_h
                dv = dv_h if dv is None else dv + dv_h
                if diag:
                    lower_keys = jnp.where(valid, pre + _dot_exact(ds, before, 1, 0, 2), 0.0)
                    dlogf = jnp.sum(lower_keys, axis=0, keepdims=True)
                else:
                    dlogf = jnp.sum(pre) + _dot_exact(jnp.sum(ds, axis=0, keepdims=True), before, 1, 0, 3)
                dcr_ref[hh:hh + 1, pl.ds(off, t)] += dlogf
                pres.append(pre + jnp.sum(ds, axis=-1, keepdims=True))
            dk_ref[pl.ds(off, t), :] += dk
            dv_ref[pl.ds(off, t), :] += dv
            return tuple(dqs), tuple(pres)

        zero, zero_q = jnp.zeros((t, 1), F32), jnp.zeros((t, 128), F32)


        def step(j, c):
            ahead = scores(j + 1)
            return block(j, c[0], c[1], False), ahead

        carry, ahead = lax.fori_loop(first, i, step, (((zero_q, zero_q), (zero, zero)), scores(first)))
        dqs, _ = block(i, carry, ahead, True)
        dq_ref[...] = jnp.where(masks[0], dqs[0], dqs[1]) * scale

    blk = lambda c0: pl.BlockSpec((t, 128), lambda p, i: (i, c0 // 128 + p))
    res = lambda c0: pl.BlockSpec((s, 128), lambda p, i: (0, c0 // 128 + p))
    shp = jax.ShapeDtypeStruct((s, 512), F32)
    c_operands, c_in_specs, c_out_shapes, c_out_specs, c_sems = _comm_operands(comm) if comm else ([], [], [], [], [])
    outs = _pcall(
        _with_comm(body, comm, 10, 4, *_grid_ends(FOX_HEADS // 2, nblk)), name=name, grid=(FOX_HEADS // 2, nblk),
        out_shape=[shp, shp, shp, jax.ShapeDtypeStruct((FOX_HEADS // 2, 8, s), F32)] + c_out_shapes,
        in_specs=[blk(0), res(0), res(C_VB), pl.BlockSpec((t, 128), lambda p, i: (i, 0)),
                  pl.BlockSpec((FOX_HEADS, s), lambda p, i: (0, 0)), pl.BlockSpec((FOX_HEADS, 128), lambda p, i: (0, 0)),
                  pl.BlockSpec((1, 1), lambda p, i: (0, 0)), blk(0),
                  pl.BlockSpec((None, t, 128), lambda p, i: (p, i, 0)), blk(M_FOX)] + c_in_specs,
        out_specs=[blk(0), res(0), res(0), pl.BlockSpec((None, 8, s), lambda p, i: (p, 0, 0))] + c_out_specs,
        scratch_shapes=c_sems, input_output_aliases=_comm_aliases(comm, 10, 4),
        compiler_params=_cp(ARB, ARB),
    )(qn, kn, proj, cfc, cfr, cfe, qk_bound, o, lse, dmix, *c_operands)
    return outs[0], outs[1], outs[2], outs[3], list(outs[4:])


def _fox_prep_bwd(proj, gq, gk, bf, dqn, dkn, dlogf, name):
    s = proj.shape[0]
    t = min(ATT_T, s)
    nblk = s // t

    def body(q_ref, k_ref, fl_ref, gq_ref, gk_ref, bf_ref, dqn_ref, dkn_ref, dlogf_ref,
             dq_ref, dk_ref, dfl_ref, dgq_ref, dgk_ref, dbf_ref):
        i = pl.program_id(0)
        gm = _group_matrix(512)

        @pl.when(i == 0)
        def _():
            dgq_ref[...] = jnp.zeros_like(dgq_ref)
            dgk_ref[...] = jnp.zeros_like(dgk_ref)
            dbf_ref[...] = jnp.zeros_like(dbf_ref)

        for src, gref, dyref, dst, dgref in ((q_ref, gq_ref, dqn_ref, dq_ref, dgq_ref),
                                              (k_ref, gk_ref, dkn_ref, dk_ref, dgk_ref)):
            v = src[...]
            dy = dyref[...]
            rstd = lax.rsqrt(_group_mean(v * v, gm) + EPS)
            xhat = v * rstd
            dgref[...] += jnp.sum(dy * xhat, axis=0, keepdims=True)
            dxh = dy * gref[...]
            dst[...] = (rstd * (dxh - xhat * _group_mean(dxh * xhat, gm))).astype(dst.dtype)

        xv = fl_ref[...] + bf_ref[...]
        e = jnp.exp(-jnp.abs(xv))
        dfl = dlogf_ref[...] * (jnp.where(xv >= 0.0, e, 1.0) / (1.0 + e))
        dfl_ref[...] = dfl.astype(dfl_ref.dtype)
        dbf_ref[...] += jnp.sum(dfl, axis=0, keepdims=True)

    col = lambda w, c0: pl.BlockSpec((t, w), lambda i: (i, c0 // w))
    v512 = pl.BlockSpec((1, 512), lambda i: (0, 0))
    v128 = pl.BlockSpec((1, 128), lambda i: (0, 0))
    return _pcall(
        body, name=name, grid=(nblk,),
        out_shape=[jax.ShapeDtypeStruct((s, 512), _BF), jax.ShapeDtypeStruct((s, 512), _BF),
                   jax.ShapeDtypeStruct((s, 128), _BF), jax.ShapeDtypeStruct((1, 512), F32),
                   jax.ShapeDtypeStruct((1, 512), F32), jax.ShapeDtypeStruct((1, 128), F32)],
        in_specs=[col(512, C_QB), col(512, C_KB), col(128, C_FL), v512, v512, v128, col(512, 0), col(512, 0), col(128, 0)],
        out_specs=[col(512, 0), col(512, 0), col(128, 0), v512, v512, v128], compiler_params=_cp(ARB),
    )(proj, proj, proj, gq, gk, bf, dqn, dkn, dlogf)


GELU_C = 0.7978845608028654
GELU_A = 0.044715


def _gelu(x):
    return 0.5 * x * (1.0 + jnp.tanh(GELU_C * (x + GELU_A * x * x * x)))


def _gelu_grad(x):
    th = jnp.tanh(GELU_C * (x + GELU_A * x * x * x))
    return 0.5 * (1.0 + th) + 0.5 * x * (1.0 - th * th) * (GELU_C * (1.0 + 3.0 * GELU_A * x * x))


def _sgu_tril():
    return _iota2((SGU_CHUNK, SGU_CHUNK), 0) >= _iota2((SGU_CHUNK, SGU_CHUNK), 1)


def _sgu_group_masks():
    lane = _iota2((1, 256), 1)
    return [(lane // HEAD_DIM) == g for g in range(SGU_GROUPS)]


def _sgu_fwd(proj, w, gs, bexp, name):
    s = proj.shape[0]
    t = min(2 * SGU_CHUNK, s)

    def body(u_ref, v_ref, w_ref, gs_ref, b_ref, o_ref):
        gm = _group_matrix(256)
        gmask = _sgu_group_masks()
        tril = _sgu_tril()
        u = _gelu(u_ref[...])
        vg = _gelu(v_ref[...])
        vhat = (vg * lax.rsqrt(_group_mean(vg * vg, gm) + EPS) * gs_ref[...]).astype(_BF)
        for ch in range(t // SGU_CHUNK):
            rows = slice(ch * SGU_CHUNK, (ch + 1) * SGU_CHUNK)
            mixed = b_ref[...]
            for g in range(SGU_GROUPS):
                wg = jnp.where(tril, w_ref[g], 0.0).astype(_BF)
                mixed = jnp.where(gmask[g], mixed + _dot(wg, vhat[rows], 1, 0), mixed)
            o_ref[rows, :] = u[rows] * mixed

    return _pcall(
        body, name=name, grid=(s // t,), out_shape=jax.ShapeDtypeStruct((s, 256), F32),
        in_specs=[_row_spec(t, 256, C_UC // 256), _row_spec(t, 256, C_VC // 256),
                  pl.BlockSpec((SGU_GROUPS, SGU_CHUNK, SGU_CHUNK), lambda i: (0, 0, 0)), _vec_spec(256),
                  pl.BlockSpec((SGU_CHUNK, 256), lambda i: (0, 0))],
        out_specs=_row_spec(t, 256), compiler_params=_cp(PAR),
    )(proj, proj, w, gs, bexp)


def _sgu_bwd(proj, w, gs, bexp, dmix, name):
    s = proj.shape[0]
    t = min(2 * SGU_CHUNK, s)
    nstep = s // t

    def body(u_ref, v_ref, w_ref, gs_ref, b_ref, do_ref, du_ref, dv_ref, dw_ref, db_ref, dgs_ref):
        i = pl.program_id(0)
        gm = _group_matrix(256)
        gmask = _sgu_group_masks()
        tril = _sgu_tril()

        @pl.when(i == 0)
        def _():
            dw_ref[...] = jnp.zeros_like(dw_ref)
            db_ref[...] = jnp.zeros_like(db_ref)
            dgs_ref[...] = jnp.zeros_like(dgs_ref)

        uc = u_ref[...]
        vc = v_ref[...]
        u = _gelu(uc)
        vg = _gelu(vc)
        rstd = lax.rsqrt(_group_mean(vg * vg, gm) + EPS)
        xh = vg * rstd
        gsv = gs_ref[...]
        vhat = (xh * gsv).astype(_BF)
        dov = do_ref[...]
        dm = dov * u
        for ch in range(t // SGU_CHUNK):
            rows = slice(ch * SGU_CHUNK, (ch + 1) * SGU_CHUNK)
            mixed = b_ref[...]
            dvh = jnp.zeros((SGU_CHUNK, 256), F32)
            dmc = dm[rows]
            for g in range(SGU_GROUPS):
                wg = jnp.where(tril, w_ref[g], 0.0).astype(_BF)
                mixed = jnp.where(gmask[g], mixed + _dot(wg, vhat[rows], 1, 0), mixed)
                dvh = jnp.where(gmask[g], _dot(wg, dmc.astype(_BF), 0, 0), dvh)
                dw_ref[g] += _dot(jnp.where(gmask[g], dmc, 0.0).astype(_BF), vhat[rows], 1, 1)
            db_ref[...] += dmc
            du_ref[rows, :] = (dov[rows] * mixed * _gelu_grad(uc[rows])).astype(du_ref.dtype)
            xhc = xh[rows]
            dgs_ref[...] += jnp.sum(dvh * xhc, axis=0, keepdims=True)
            dxh = dvh * gsv
            dvg = rstd[rows] * (dxh - xhc * _group_mean(dxh * xhc, gm))
            dv_ref[rows, :] = (dvg * _gelu_grad(vc[rows])).astype(dv_ref.dtype)

        @pl.when(i == nstep - 1)
        def _():
            for g in range(SGU_GROUPS):
                dw_ref[g] = jnp.where(tril, dw_ref[g], 0.0)

    wspec = pl.BlockSpec((SGU_GROUPS, SGU_CHUNK, SGU_CHUNK), lambda i: (0, 0, 0))
    bspec = pl.BlockSpec((SGU_CHUNK, 256), lambda i: (0, 0))
    return _pcall(
        body, name=name, grid=(nstep,),
        out_shape=[jax.ShapeDtypeStruct((s, 256), _BF), jax.ShapeDtypeStruct((s, 256), _BF),
                   jax.ShapeDtypeStruct((SGU_GROUPS, SGU_CHUNK, SGU_CHUNK), F32),
                   jax.ShapeDtypeStruct((SGU_CHUNK, 256), F32), jax.ShapeDtypeStruct((1, 256), F32)],
        in_specs=[_row_spec(t, 256, C_UC // 256), _row_spec(t, 256, C_VC // 256), wspec, _vec_spec(256), bspec,
                  _row_spec(t, 256, M_SGU // 256)],
        out_specs=[_row_spec(t, 256), _row_spec(t, 256), wspec, bspec, _vec_spec(256)],
        compiler_params=_cp(ARB),
    )(proj, proj, w, gs, bexp, dmix)


def _ada_fwd(c_all, ada_w, name):
    depth, d, n = ada_w.shape

    def body(c_ref, w_ref, o_ref):
        cv = c_ref[...]
        cond = cv / (1.0 + jnp.exp(-cv))
        o_ref[...] = _dot_f32(cond, w_ref[...], 1, 0)

    return _pcall(
        body, name=name, grid=(depth,), out_shape=jax.ShapeDtypeStruct((depth, N_DEV, n), F32),
        in_specs=[pl.BlockSpec((N_DEV, d), lambda l: (0, 0)), pl.BlockSpec((None, d, n), lambda l: (l, 0, 0))],
        out_specs=pl.BlockSpec((None, N_DEV, n), lambda l: (l, 0, 0)), compiler_params=_cp(PAR),
    )(c_all, ada_w)


def _ada_bwd(c_all, dmod, name):
    depth, _, n = dmod.shape
    d = c_all.shape[1]

    def body(c_ref, dm_ref, o_ref):
        cv = c_ref[...]
        cond = cv / (1.0 + jnp.exp(-cv))
        o_ref[...] = _dot_f32(cond, dm_ref[...], 0, 0)

    return _pcall(
        body, name=name, grid=(depth,), out_shape=jax.ShapeDtypeStruct((depth, d, n), F32),
        in_specs=[pl.BlockSpec((N_DEV, d), lambda l: (0, 0)), pl.BlockSpec((None, N_DEV, n), lambda l: (l, 0, 0))],
        out_specs=pl.BlockSpec((None, d, n), lambda l: (l, 0, 0)), compiler_params=_cp(PAR),
    )(c_all, dmod)


def _adamw(slots, w, m, v, name):
    n, r, c = slots.shape
    tr = 256 if r % 256 == 0 else r
    bc1 = 1.0 - ADAM_B1 ** ADAM_STEP
    bc2 = 1.0 - ADAM_B2 ** ADAM_STEP

    def body(s_ref, w_ref, m_ref, v_ref, g_ref, d_ref, nm_ref, nv_ref):
        g = s_ref[0].astype(F32)
        for j in range(1, n):
            g = g + s_ref[j].astype(F32)
        m_new = ADAM_B1 * m_ref[...] + (1.0 - ADAM_B1) * g
        v_new = ADAM_B2 * v_ref[...] + (1.0 - ADAM_B2) * (g * g)
        g_ref[...] = g
        nm_ref[...] = m_new
        nv_ref[...] = v_new
        d_ref[...] = -ADAM_LR * ((m_new / bc1) / (jnp.sqrt(v_new / bc2) + ADAM_EPS) + ADAM_WD * w_ref[...])

    tile = pl.BlockSpec((tr, c), lambda i: (i, 0))
    shp = jax.ShapeDtypeStruct((r, c), F32)
    return _pcall(
        body, name=name, grid=(r // tr,), out_shape=[shp] * 4,
        in_specs=[pl.BlockSpec((n, tr, c), lambda i: (0, i, 0)), tile, tile, tile],
        out_specs=[tile] * 4, compiler_params=_cp(PAR),
    )(slots, w, m, v)


def _pad_cols(a, width):
    return jnp.pad(a, ((0, 0), (0, width - a.shape[1])))


def _w_in_layout(w):
    pad = jnp.zeros(w.shape[:-1] + (IN_P - IN_W,), w.dtype)
    return jnp.concatenate([w[..., 768:FL_SRC], w[..., :768], w[..., UC_SRC:], w[..., FL_SRC:UC_SRC], pad], axis=-1)


def _w_in_unlayout(g):
    return jnp.concatenate([g[..., C_QA:C_UC], g[..., :C_QA], g[..., C_FL:C_FL + FOX_HEADS], g[..., C_UC:C_FL]], axis=-1)


SMALL = [("ada_b", DEPTH * 6 * D_MODEL), ("norm1_g", DEPTH * D_MODEL), ("norm2_g", DEPTH * D_MODEL),
         ("sgu_w", DEPTH * SGU_GROUPS * SGU_CHUNK * SGU_CHUNK), ("sgu_b", DEPTH * SGU_GROUPS * SGU_CHUNK),
         ("sgu_norm_g", DEPTH * SGU_GROUPS * HEAD_DIM), ("q_norm_g", DEPTH * HEAD_DIM), ("k_norm_g", DEPTH * HEAD_DIM),
         ("b_forget", DEPTH * FOX_HEADS), ("loss", 1)]
SMALL_ROWS = 2560


def _pack_small(parts):
    flat = jnp.concatenate([parts[name].reshape(-1).astype(F32) for name, _ in SMALL])
    return jnp.pad(flat, (0, SMALL_ROWS * 128 - flat.shape[0])).reshape(SMALL_ROWS, 128)


def _unpack_small(packed, shapes):
    flat = packed.reshape(-1)
    out, off = {}, 0
    for name, size in SMALL:
        out[name] = flat[off:off + size].reshape(shapes[name])
        off += size
    return out


def kernel(x, c, ada_w, ada_b, norm1_g, norm2_g, w_in, b_forget, q_norm_g, k_norm_g, sgu_norm_g, sgu_w, sgu_b, w_out, mlp_w1, mlp_w2, loss_target, m_ada_w, m_ada_b, m_norm1_g, m_norm2_g, m_w_in, m_b_forget, m_q_norm_g, m_k_norm_g, m_sgu_norm_g, m_sgu_w, m_sgu_b, m_w_out, m_mlp_w1, m_mlp_w2, v_ada_w, v_ada_b, v_norm1_g, v_norm2_g, v_w_in, v_b_forget, v_q_norm_g, v_k_norm_g, v_sgu_norm_g, v_sgu_w, v_sgu_b, v_w_out, v_mlp_w1, v_mlp_w2):
    me = _lin(_my_pos())
    x0 = x[0]
    target = loss_target[0]
    n_ada = ada_w.shape[2]

    shards = [w.astype(_XBF) for w in (w_in, w_out, mlp_w1, mlp_w2)]

    def whole(w_in_g, w_out_g, w1_g, w2_g):
        return (_w_in_layout(jnp.transpose(w_in_g, (1, 0, 2)).reshape(D_MODEL, IN_W)),
                w_out_g.reshape(D_MODEL, D_MODEL),
                jnp.transpose(w1_g, (1, 0, 2)).reshape(D_MODEL, D_FF),
                w2_g.reshape(D_FF, D_MODEL))

    *first, c_all = _exchange([w[0] for w in shards] + [c], False, "gather_weights")
    c_all = c_all.reshape(N_DEV, D_MODEL)
    weights = [whole(*first)]

    mod_part = _ada_fwd(c_all, ada_w, "ada_fwd")
    (mod_rows,) = _exchange([jnp.transpose(mod_part, (1, 0, 2))], True, "scatter_mod")
    mod = jnp.transpose(mod_rows, (1, 0, 2)).reshape(DEPTH, 6 * D_MODEL) + ada_b
    mods = mod.reshape(DEPTH, 6, 1, D_MODEL)

    saved = []
    xl = x0
    for l in range(DEPTH):
        sh1, sc1, g1, sh2, sc2, g2 = (mods[l, r] for r in range(6))
        n1 = norm1_g[l].reshape(1, D_MODEL)
        n2 = norm2_g[l].reshape(1, D_MODEL)
        gq = jnp.tile(q_norm_g[l], FOX_HEADS).reshape(1, 512)
        gk = jnp.tile(k_norm_g[l], FOX_HEADS).reshape(1, 512)
        bf = jnp.pad(b_forget[l], (0, 128 - FOX_HEADS)).reshape(1, 128)
        gs = sgu_norm_g[l].reshape(1, 256)
        bexp = jnp.repeat(sgu_b[l].T, HEAD_DIM, axis=1)

        w_in_l, w_out_l, w1_l, w2_l = weights[l]
        h1, h1_t = _norm_mod(xl, n1, sc1, sh1, "norm_mod")
        proj = _mm_nn(h1, w_in_l, "plain", (), "mm_in")
        o_sb = _sb_fwd(proj, "sb_fwd")
        qn, kn, cfc, cfr, cfe = _fox_prep(proj, gq, gk, bf, "fox_prep")
        qkb = (1.02 * HEAD_DIM ** 0.5 * jnp.max(jnp.abs(q_norm_g[l])) * jnp.max(jnp.abs(k_norm_g[l]))).reshape(1, 1)
        gather_next = None
        if l + 1 < DEPTH:
            gather_next = dict(arrays=[w[l + 1] for w in shards], scatter=False, recv=None, slot=None)
        o_fox, lse, gathered = _fox_fwd(proj, qn, kn, cfc, cfr, cfe, qkb, "fox_fwd", gather_next)
        if gathered:
            weights.append(whole(*gathered))
        o_sgu = _sgu_fwd(proj, sgu_w[l], gs, bexp, "sgu_fwd")
        mixed = jnp.concatenate([o_sb, o_fox, o_sgu], axis=-1).astype(_BF)
        x_mid, y1 = _mm_nn(mixed, w_out_l, "resid", (xl, g1), "mm_out")
        h2, h2_t = _norm_mod(x_mid, n2, sc2, sh2, "norm_mod")
        r2, r1, r2_t = _mm_nn(h2, w1_l, "relu2", (), "mm_w1")
        x_out, y2 = _mm_nn(r2, w2_l, "resid", (x_mid, g2), "mm_w2")
        saved.append(dict(x_in=xl, h1_t=h1_t, proj=proj, o_sb=o_sb, qn=qn, kn=kn, cfc=cfc, cfr=cfr, cfe=cfe, o_fox=o_fox,
                          lse=lse, mixed_t=mixed.T, x_mid=x_mid, y1=y1, h2_t=h2_t, r2_t=r2_t, r1=r1, y2=y2,
                          n1=n1, n2=n2, gq=gq, gk=gk, bf=bf, gs=gs, bexp=bexp, qkb=qkb))
        xl = x_out

    loss_part, dx, dy2, dg2 = _loss_head(xl, target, (saved[-1]["y2"], mods[DEPTH - 1, 5]), "loss_head")

    grads_ready = None
    received = [jnp.zeros((N_DEV,) + w.shape, _XBF) for w in shards]
    small = {k: [None] * DEPTH for k in ("mod", "norm1_g", "norm2_g", "sgu_w", "sgu_b", "sgu_norm_g",
                                         "q_norm_g", "k_norm_g", "b_forget")}
    for l in reversed(range(DEPTH)):
        sv = saved[l]
        sh1, sc1, g1, sh2, sc2, g2 = (mods[l, r] for r in range(6))
        w_in_l, w_out_l, w1_l, w2_l = weights[l]
        da = _mm_nt(dy2, w2_l, "mul2", (sv["r1"],), _BF, "mm_w2_bwd")
        g_w2 = _mm_grad(sv["r2_t"], dy2, "rows", "mm_w2_grad")
        dh2 = _mm_nt(da, w1_l, "plain", (), F32, "mm_w1_bwd")
        g_w1 = _mm_grad(sv["h2_t"], da, "cols", "mm_w1_grad")
        dx_mid, dsh2, dsc2, dn2, dy1, dg1 = _norm_mod_bwd(sv["x_mid"], dh2, dx, sv["n2"], sc2, (sv["y1"], g1),
                                                          "norm_mod_bwd")
        dmix = _mm_nt(dy1, w_out_l, "plain", (), F32, "mm_out_bwd")
        g_out = _mm_grad(sv["mixed_t"], dy1, "rows", "mm_out_grad")
        dq_sb, dk_sb, dv_sb = _sb_bwd(sv["proj"], sv["o_sb"], dmix, "sb_bwd")
        send_prev = None
        if grads_ready is not None:
            send_prev = dict(arrays=grads_ready, scatter=True, recv=received, slot=l + 1)
        dqn, dkn, dv_fox, dcr, filled = _fox_bwd(sv["proj"], sv["qn"], sv["kn"], sv["cfc"], sv["cfr"], sv["cfe"], sv["qkb"],
                                                 sv["o_fox"], sv["lse"], dmix, "fox_bwd", send_prev)
        if filled:
            received = filled
        dlogf = _pad_cols(dcr[:, :2, :].reshape(FOX_HEADS, -1).T, 128)
        dq_fox, dk_fox, dfl, dgq, dgk, dbf = _fox_prep_bwd(sv["proj"], sv["gq"], sv["gk"], sv["bf"], dqn, dkn, dlogf,
                                                           "fox_prep_bwd")
        duc, dvc, dsw, dsb, dsg = _sgu_bwd(sv["proj"], sgu_w[l], sv["gs"], sv["bexp"], dmix, "sgu_bwd")
        dproj = jnp.concatenate(
            [dq_fox, dk_fox, dv_fox.astype(_BF), dq_sb.astype(_BF), dk_sb.astype(_BF), dv_sb.astype(_BF), duc, dvc, dfl,
             jnp.zeros((dfl.shape[0], IN_P - C_FL - 128), _BF)], axis=-1)
        dh1 = _mm_nt(dproj, w_in_l, "plain", (), F32, "mm_in_bwd")
        g_in = _w_in_unlayout(_mm_grad(sv["h1_t"], dproj, None, "mm_in_grad"))
        g_in = jnp.transpose(g_in.reshape(D_MODEL, N_DEV, IN_W // N_DEV), (1, 0, 2))
        grads_ready = [g_in, g_out, g_w1, g_w2]
        small["mod"][l] = [dg1, dsh2, dsc2, dg2]
        below = (saved[l - 1]["y2"], mods[l - 1, 5]) if l > 0 else None
        dx, dsh1, dsc1, dn1, *below_grads = _norm_mod_bwd(sv["x_in"], dh1, dx_mid, sv["n1"], sc1, below, "norm_mod_bwd")
        if below_grads:
            dy2, dg2 = below_grads
        small["mod"][l] = jnp.concatenate([dsh1, dsc1] + small["mod"][l], axis=-1).reshape(-1)
        small["norm1_g"][l] = dn1.reshape(-1)
        small["norm2_g"][l] = dn2.reshape(-1)
        small["sgu_w"][l] = dsw
        small["sgu_b"][l] = dsb.reshape(SGU_CHUNK, SGU_GROUPS, HEAD_DIM).sum(-1).T
        small["sgu_norm_g"][l] = dsg.reshape(SGU_GROUPS, HEAD_DIM)
        small["q_norm_g"][l] = dgq.reshape(FOX_HEADS, HEAD_DIM).sum(0)
        small["k_norm_g"][l] = dgk.reshape(FOX_HEADS, HEAD_DIM).sum(0)
        small["b_forget"][l] = dbf[0, :FOX_HEADS]

    parts = {k: jnp.stack(v) for k, v in small.items()}
    parts["ada_b"] = parts.pop("mod")
    parts["loss"] = loss_part
    (small_all,) = _exchange([_pack_small(parts)], False, "gather_small")
    zero1 = jnp.zeros((1,), F32)
    packs = [_pack_small(dict(ada_b=a, norm1_g=b, norm2_g=cc, sgu_w=d, sgu_b=e, sgu_norm_g=f, q_norm_g=g, k_norm_g=h,
                              b_forget=i, loss=zero1))
             for a, b, cc, d, e, f, g, h, i in (
                 (ada_b, norm1_g, norm2_g, sgu_w, sgu_b, sgu_norm_g, q_norm_g, k_norm_g, b_forget),
                 (m_ada_b, m_norm1_g, m_norm2_g, m_sgu_w, m_sgu_b, m_sgu_norm_g, m_q_norm_g, m_k_norm_g, m_b_forget),
                 (v_ada_b, v_norm1_g, v_norm2_g, v_sgu_w, v_sgu_b, v_sgu_norm_g, v_q_norm_g, v_k_norm_g, v_b_forget))]
    small_shapes = dict(ada_b=ada_b.shape, norm1_g=norm1_g.shape, norm2_g=norm2_g.shape, sgu_w=sgu_w.shape,
                        sgu_b=sgu_b.shape, sgu_norm_g=sgu_norm_g.shape, q_norm_g=q_norm_g.shape,
                        k_norm_g=k_norm_g.shape, b_forget=b_forget.shape, loss=())
    s_out = [_unpack_small(a, small_shapes) for a in _adamw(small_all, *packs, "adamw_small")]

    dmod_all = small_all[:, :DEPTH * 6 * D_MODEL // 128, :].reshape(N_DEV, DEPTH, 6 * D_MODEL)
    dmod_mine = lax.dynamic_slice_in_dim(dmod_all, me * n_ada, n_ada, axis=2)
    g_ada = _ada_bwd(c_all, jnp.transpose(dmod_mine, (1, 0, 2)), "ada_bwd")
    ada_out = _adamw(g_ada.reshape(1, DEPTH * D_MODEL, n_ada), ada_w.reshape(-1, n_ada), m_ada_w.reshape(-1, n_ada),
                     v_ada_w.reshape(-1, n_ada), "adamw_ada")
    ada_out = [a.reshape(ada_w.shape) for a in ada_out]

    ri, ro, r1, r2 = _exchange(grads_ready, True, "scatter_grads", recv=received, slot=0)

    def big(slots, w, m, v, name):
        cdim = w.shape[-1]
        outs = _adamw(slots.reshape(N_DEV, -1, cdim), w.reshape(-1, cdim), m.reshape(-1, cdim), v.reshape(-1, cdim), name)
        return [a.reshape(w.shape) for a in outs]

    in_out = big(ri, w_in, m_w_in, v_w_in, "adamw_in")
    out_out = big(ro, w_out, m_w_out, v_w_out, "adamw_out")
    w1_out = big(r1, mlp_w1, m_mlp_w1, v_mlp_w1, "adamw_w1")
    w2_out = big(r2, mlp_w2, m_mlp_w2, v_mlp_w2, "adamw_w2")

    def leaves(idx):
        sm = s_out[idx]
        return [ada_out[idx], sm["ada_b"], sm["norm1_g"], sm["norm2_g"], in_out[idx], sm["b_forget"], sm["q_norm_g"],
                sm["k_norm_g"], sm["sgu_norm_g"], sm["sgu_w"], sm["sgu_b"], out_out[idx], w1_out[idx], w2_out[idx]]

    loss = s_out[0]["loss"]
    grad_x = dx.reshape(x.shape)
    return (loss, grad_x, *leaves(0), *leaves(1), *leaves(2), *leaves(3))
```

```python
import jax
import jax.numpy as jnp
from jax import lax
from jax.experimental import pallas as pl
from jax.experimental.pallas import tpu as pltpu

F32 = jnp.float32
_BF = jnp.bfloat16
_XBF = jnp.bfloat16

N_DEV = 8
D_MODEL = 1024
DEPTH = 4
HEAD_DIM = 64
SB_HEADS = 4
FOX_HEADS = 8
SGU_GROUPS = 4
SGU_CHUNK = 128
D_FF = 4096
EPS = 1e-6
IN_W = 2824
FL_SRC = 2304
UC_SRC = 2312

C_QB, C_KB, C_VB = 0, 512, 1024
C_QA, C_KA, C_VA = 1536, 1792, 2048
C_UC, C_VC = 2304, 2560
C_FL = 2816
IN_P = 3072
M_SB, M_FOX, M_SGU = 0, 256, 768

ADAM_LR = 0.001
ADAM_B1 = 0.9
ADAM_B2 = 0.999
ADAM_EPS = 1e-08
ADAM_WD = 0.01
ADAM_STEP = 10

ATT_T = 256
ROW_T = 512
MM_T = 1024
GRAD_TS = 2048
ROW_CHUNK = 64
NEG = -1e30
UNDERFLOW = -100.0
NO_BLOCK = 1e30

MESH = pl.DeviceIdType.MESH
ARB = pltpu.ARBITRARY
PAR = pltpu.PARALLEL
HIGHEST = lax.Precision.HIGHEST


def _pcall(body, **kw):
    return pl.pallas_call(body, **kw)


def _cp(*sem):
    return pltpu.CompilerParams(dimension_semantics=tuple(sem))


def _dot(a, b, ca, cb):
    return lax.dot_general(a, b, (((ca,), (cb,)), ((), ())), preferred_element_type=F32)


def _dot_f32(a, b, ca, cb):
    return lax.dot_general(a, b, (((ca,), (cb,)), ((), ())), precision=HIGHEST, preferred_element_type=F32)


def _split(v, parts):
    out = []
    r = v
    for _ in range(parts - 1):
        p = r.astype(_BF)
        out.append(p)
        r = r - p.astype(F32)
    out.append(r.astype(_BF))
    return out


def _dot_exact(v, m, ca, cb, parts, v_left=True):
    out = None
    for p in _split(v, parts):
        term = _dot(p, m, ca, cb) if v_left else _dot(m, p, ca, cb)
        out = term if out is None else out + term
    return out


def _iota2(shape, dim):
    return lax.broadcasted_iota(jnp.int32, shape, dim)


def _my_pos():
    return lax.axis_index("x"), lax.axis_index("y"), lax.axis_index("c")


def _flip(pos, k):
    x, y, c = pos
    px = 1 - x if (k >> 2) & 1 else x
    py = 1 - y if (k >> 1) & 1 else y
    pc = 1 - c if k & 1 else c
    return px, py, pc


def _lin(pos):
    return 4 * pos[0] + 2 * pos[1] + pos[2]


def _exchange_ops(ins, outs, sems, scatter, slot):
    send_sems, recv_sems, local_sems = sems
    n = len(ins)

    def copies(with_recvs):
        me = _my_pos()
        me_i = _lin(me)
        dst = lambda a, j: outs[a].at[j] if slot is None else outs[a].at[j, slot]
        mine = lambda a: ins[a].at[me_i] if scatter else ins[a]
        local = [pltpu.make_async_copy(mine(a), dst(a, me_i), local_sems.at[a]) for a in range(n)]
        sends, recvs = [], []
        for a in range(n):
            for k in range(1, N_DEV):
                peer = _flip(me, k)
                pair = dict(send_sem=send_sems.at[a, k - 1], recv_sem=recv_sems.at[a, k - 1],
                            device_id=peer, device_id_type=MESH)
                src = ins[a].at[_lin(peer)] if scatter else ins[a]
                sends.append(pltpu.make_async_remote_copy(src_ref=src, dst_ref=dst(a, me_i), **pair))
                if with_recvs:
                    recvs.append(pltpu.make_async_remote_copy(src_ref=mine(a), dst_ref=dst(a, _lin(peer)), **pair))
        return local, sends, recvs

    def start():
        local, sends, _ = copies(False)
        for cp in local + sends:
            cp.start()

    def wait():
        local, sends, recvs = copies(True)
        for snd, rcv in zip(sends, recvs):
            snd.wait_send()
            rcv.wait_recv()
        for cp in local:
            cp.wait()

    return start, wait


def _comm_operands(comm):
    arrays, recv = comm["arrays"], comm["recv"]
    n = len(arrays)
    any_spec = pl.BlockSpec(memory_space=pl.ANY)
    if recv is not None:
        out_shapes = [jax.ShapeDtypeStruct(r.shape, r.dtype) for r in recv]
    elif comm["scatter"]:
        out_shapes = [jax.ShapeDtypeStruct(a.shape, a.dtype) for a in arrays]
    else:
        out_shapes = [jax.ShapeDtypeStruct((N_DEV,) + a.shape, a.dtype) for a in arrays]
    operands = list(arrays) + (list(recv) if recv is not None else [])
    sems = [pltpu.SemaphoreType.DMA((n, N_DEV - 1)), pltpu.SemaphoreType.DMA((n, N_DEV - 1)),
            pltpu.SemaphoreType.DMA((n,))]
    return operands, [any_spec] * len(operands), out_shapes, [any_spec] * n, sems


def _with_comm(body, comm, n_in, n_out, first_step, last_step):
    if comm is None:
        return body
    n = len(comm["arrays"])
    n_cin = n if comm["recv"] is None else 2 * n

    def wrapped(*refs):
        ins, cins = refs[:n_in], refs[n_in:n_in + n]
        o0 = n_in + n_cin
        outs, couts, sems = refs[o0:o0 + n_out], refs[o0 + n_out:o0 + n_out + n], refs[o0 + n_out + n:]
        start, wait = _exchange_ops(cins, couts, sems, comm["scatter"], comm["slot"])
        pl.when(first_step())(start)
        body(*ins, *outs)
        pl.when(last_step())(wait)

    return wrapped


def _comm_aliases(comm, n_in, n_out):
    if comm is None or comm["recv"] is None:
        return {}
    n = len(comm["arrays"])
    return {n_in + n + a: n_out + a for a in range(n)}


def _exchange(arrays, scatter, name, recv=None, slot=None):
    comm = dict(arrays=arrays, scatter=scatter, recv=recv, slot=slot)
    operands, in_specs, out_shapes, out_specs, sems = _comm_operands(comm)
    n = len(arrays)

    def body(*refs):
        n_cin = len(operands)
        start, wait = _exchange_ops(refs[:n], refs[n_cin:n_cin + n], refs[n_cin + n:], scatter, slot)
        start()
        wait()

    outs = _pcall(body, name=name, out_shape=out_shapes, in_specs=in_specs, out_specs=out_specs, scratch_shapes=sems,
                  input_output_aliases=_comm_aliases(comm, 0, 0))(*operands)
    return list(outs)


def _row_spec(t, w, col=0):
    return pl.BlockSpec((t, w), lambda i: (i, col))


def _vec_spec(w, col=0):
    return pl.BlockSpec((1, w), lambda i: (0, col))


def _norm_mod(x, g, sc, sh, name):
    s, d = x.shape
    t = min(ROW_T, s)

    def body(x_ref, g_ref, sc_ref, sh_ref, h_ref, ht_ref):
        xf = x_ref[...]
        rstd = lax.rsqrt(jnp.mean(xf * xf, axis=-1, keepdims=True) + EPS)
        y = xf * rstd * g_ref[...]
        h = y * (1.0 + sc_ref[...]) + sh_ref[...]
        h_ref[...] = h.astype(h_ref.dtype)
        ht_ref[...] = h.T.astype(ht_ref.dtype)

    return _pcall(
        body, name=name, grid=(s // t,),
        out_shape=[jax.ShapeDtypeStruct((s, d), _BF), jax.ShapeDtypeStruct((d, s), _BF)],
        in_specs=[_row_spec(t, d), _vec_spec(d), _vec_spec(d), _vec_spec(d)],
        out_specs=[_row_spec(t, d), pl.BlockSpec((d, t), lambda i: (0, i))], compiler_params=_cp(PAR),
    )(x, g, sc, sh)


def _res_bwd_step(i, dx, y_ref, gate_ref, dy_ref, dgate_ref):
    dy_ref[...] = (dx * gate_ref[...]).astype(dy_ref.dtype)

    @pl.when(i == 0)
    def _():
        dgate_ref[...] = jnp.zeros_like(dgate_ref)

    dgate_ref[...] += jnp.sum(dx * y_ref[...], axis=0, keepdims=True)


def _norm_mod_bwd(x, dh, dres, g, sc, branch, name):
    s, d = x.shape
    t = min(ROW_T, s)
    n_branch = 0 if branch is None else 2

    ch = min(ROW_CHUNK, t)

    def body(x_ref, dh_ref, dres_ref, g_ref, sc_ref, *rest):
        dx_ref, dsh_ref, dsc_ref, dg_ref = rest[n_branch:n_branch + 4]
        i = pl.program_id(0)
        gv = g_ref[...]
        sc1 = 1.0 + sc_ref[...]
        n_sum = 3 if branch is None else 4

        def rows(r, sums):
            sl = pl.ds(pl.multiple_of(r * ch, ch), ch)
            xf = x_ref[sl, :]
            dh = dh_ref[sl, :]
            rstd = lax.rsqrt(jnp.mean(xf * xf, axis=-1, keepdims=True) + EPS)
            xhat = xf * rstd
            dn = dh * sc1
            dxh = dn * gv
            dx = dres_ref[sl, :] + rstd * (dxh - xhat * jnp.mean(dxh * xhat, axis=-1, keepdims=True))
            dx_ref[sl, :] = dx
            new = [sums[0] + jnp.sum(dh, axis=0, keepdims=True),
                   sums[1] + jnp.sum(dh * (xhat * gv), axis=0, keepdims=True),
                   sums[2] + jnp.sum(dn * xhat, axis=0, keepdims=True)]
            if branch is not None:
                rest[n_branch + 4][sl, :] = (dx * rest[1][...]).astype(_BF)
                new.append(sums[3] + jnp.sum(dx * rest[0][sl, :], axis=0, keepdims=True))
            return tuple(new)

        sums = lax.fori_loop(0, t // ch, rows, tuple(jnp.zeros((1, d), F32) for _ in range(n_sum)))
        totals = [dsh_ref, dsc_ref, dg_ref] + ([rest[n_branch + 5]] if branch is not None else [])

        @pl.when(i == 0)
        def _():
            for ref in totals:
                ref[...] = jnp.zeros_like(ref)

        for ref, part in zip(totals, sums):
            ref[...] += part

    vec = jax.ShapeDtypeStruct((1, d), F32)
    out_shape = [jax.ShapeDtypeStruct((s, d), F32), vec, vec, vec]
    in_specs = [_row_spec(t, d), _row_spec(t, d), _row_spec(t, d), _vec_spec(d), _vec_spec(d)]
    out_specs = [_row_spec(t, d), _vec_spec(d), _vec_spec(d), _vec_spec(d)]
    if branch is not None:
        in_specs += [_row_spec(t, d), _vec_spec(d)]
        out_shape += [jax.ShapeDtypeStruct((s, d), _BF), vec]
        out_specs += [_row_spec(t, d), _vec_spec(d)]
    return _pcall(
        body, name=name, grid=(s // t,), out_shape=out_shape, in_specs=in_specs, out_specs=out_specs,
        compiler_params=_cp(ARB),
    )(x, dh, dres, g, sc, *(branch or ()))


def _loss_head(y, target, branch, name):
    s, d = y.shape
    t = min(ROW_T, s)

    def body(y_ref, t_ref, yb_ref, gate_ref, loss_ref, dy_ref, dyb_ref, dgate_ref):
        i = pl.program_id(0)
        diff = y_ref[...] - t_ref[...]
        dy = diff * (1.0 / d)
        dy_ref[...] = dy

        @pl.when(i == 0)
        def _():
            loss_ref[...] = jnp.zeros_like(loss_ref)

        rows = jnp.sum(diff * diff, axis=-1, keepdims=True)
        loss_ref[...] += (0.5 / d) * jnp.sum(rows, axis=0, keepdims=True)
        _res_bwd_step(i, dy, yb_ref, gate_ref, dyb_ref, dgate_ref)

    return _pcall(
        body, name=name, grid=(s // t,),
        out_shape=[jax.ShapeDtypeStruct((1, 1), F32), jax.ShapeDtypeStruct((s, d), F32),
                   jax.ShapeDtypeStruct((s, d), _BF), jax.ShapeDtypeStruct((1, d), F32)],
        in_specs=[_row_spec(t, d), _row_spec(t, d), _row_spec(t, d), _vec_spec(d)],
        out_specs=[pl.BlockSpec((1, 1), lambda i: (0, 0)), _row_spec(t, d), _row_spec(t, d), _vec_spec(d)],
        compiler_params=_cp(ARB),
    )(y, target, *branch)


def _mm_nn(a, b, epi, extras, name):
    m, kdim = a.shape
    n = b.shape[1]
    tm, tn, tk = min(MM_T, m), min(MM_T, n), min(MM_T, kdim)
    nk = kdim // tk
    n_extra = len(extras)
    n_out = {"plain": 1, "resid": 2, "relu2": 3}[epi]

    def finish(y, extra_refs, out_refs):
        if epi == "plain":
            out_refs[0][...] = y.astype(out_refs[0].dtype)
        elif epi == "resid":
            x_ref, g_ref = extra_refs
            out_refs[0][...] = x_ref[...] + g_ref[...] * y
            out_refs[1][...] = y
        else:
            r = jnp.maximum(y, 0.0)
            out_refs[0][...] = (r * r).astype(out_refs[0].dtype)
            out_refs[1][...] = r.astype(out_refs[1].dtype)
            out_refs[2][...] = (r * r).T.astype(out_refs[2].dtype)

    def body(a_ref, b_ref, *rest):
        extra_refs = rest[:n_extra]
        out_refs = rest[n_extra:n_extra + n_out]
        part = _dot(a_ref[...].astype(_BF), b_ref[...].astype(_BF), 1, 0)
        if nk == 1:
            finish(part, extra_refs, out_refs)
        else:
            acc_ref = rest[-1]
            k = pl.program_id(2)

            @pl.when(k == 0)
            def _():
                acc_ref[...] = part

            @pl.when(k > 0)
            def _():
                acc_ref[...] += part

            @pl.when(k == nk - 1)
            def _():
                finish(acc_ref[...], extra_refs, out_refs)

    tile = pl.BlockSpec((tm, tn), lambda i, j, k: (i, j))
    in_specs = [pl.BlockSpec((tm, tk), lambda i, j, k: (i, k)), pl.BlockSpec((tk, tn), lambda i, j, k: (k, j))]
    if epi == "plain":
        out_shape = [jax.ShapeDtypeStruct((m, n), F32)]
    elif epi == "resid":
        in_specs += [tile, pl.BlockSpec((1, tn), lambda i, j, k: (0, j))]
        out_shape = [jax.ShapeDtypeStruct((m, n), F32)] * 2
    out_specs = [tile] * n_out
    if epi == "relu2":
        out_shape = [jax.ShapeDtypeStruct((m, n), _BF), jax.ShapeDtypeStruct((m, n), F32),
                     jax.ShapeDtypeStruct((n, m), _BF)]
        out_specs[2] = pl.BlockSpec((tn, tm), lambda i, j, k: (j, i))
    outs = _pcall(
        body, name=name, grid=(m // tm, n // tn, nk), out_shape=out_shape,
        in_specs=in_specs, out_specs=out_specs,
        scratch_shapes=[pltpu.VMEM((tm, tn), F32)] if nk > 1 else [],
        compiler_params=_cp(PAR, PAR, ARB),
    )(a, b, *extras)
    return outs[0] if n_out == 1 else outs


def _mm_out(parts, w, x, gate, name):
    s, n = x.shape
    widths = [p.shape[1] for p in parts]
    kdim = sum(widths)
    tm = min(ROW_T, s)

    def body(a0_ref, a1_ref, a2_ref, w_ref, x_ref, g_ref, xo_ref, y_ref, mt_ref):
        mixed = jnp.concatenate([a0_ref[...], a1_ref[...], a2_ref[...]], axis=-1)
        y = _dot(mixed.astype(_BF), w_ref[...], 1, 0)
        xo_ref[...] = x_ref[...] + g_ref[...] * y
        y_ref[...] = y
        mt_ref[...] = mixed.T.astype(mt_ref.dtype)

    row = lambda width: pl.BlockSpec((tm, width), lambda i: (i, 0))
    return _pcall(
        body, name=name, grid=(s // tm,),
        out_shape=[jax.ShapeDtypeStruct((s, n), F32), jax.ShapeDtypeStruct((s, n), F32),
                   jax.ShapeDtypeStruct((kdim, s), _BF)],
        in_specs=[row(widths[0]), row(widths[1]), row(widths[2]), pl.BlockSpec((kdim, n), lambda i: (0, 0)),
                  row(n), pl.BlockSpec((1, n), lambda i: (0, 0))],
        out_specs=[row(n), row(n), pl.BlockSpec((kdim, tm), lambda i: (0, i))], compiler_params=_cp(PAR),
    )(*parts, w, x, gate)


def _mm_nt(a, b, epi, extras, out_dtype, name):
    m, kdim = a.shape
    n = b.shape[0]
    tm, tn, tk = min(MM_T, m), min(MM_T, n), min(MM_T, kdim)
    nk = kdim // tk
    n_extra = len(extras)

    def finish(y, extra_refs, o_ref):
        if epi == "mul2":
            y = y * (2.0 * extra_refs[0][...].astype(F32))
        o_ref[...] = y.astype(o_ref.dtype)

    def body(a_ref, b_ref, *rest):
        extra_refs = rest[:n_extra]
        o_ref = rest[n_extra]
        part = _dot(a_ref[...].astype(_BF), b_ref[...].astype(_BF), 1, 1)
        if nk == 1:
            finish(part, extra_refs, o_ref)
        else:
            acc_ref = rest[-1]
            k = pl.program_id(2)

            @pl.when(k == 0)
            def _():
                acc_ref[...] = part

            @pl.when(k > 0)
            def _():
                acc_ref[...] += part

            @pl.when(k == nk - 1)
            def _():
                finish(acc_ref[...], extra_refs, o_ref)

    tile = pl.BlockSpec((tm, tn), lambda i, j, k: (i, j))
    in_specs = [pl.BlockSpec((tm, tk), lambda i, j, k: (i, k)), pl.BlockSpec((tn, tk), lambda i, j, k: (j, k))]
    in_specs += [tile] * n_extra
    return _pcall(
        body, name=name, grid=(m // tm, n // tn, nk), out_shape=jax.ShapeDtypeStruct((m, n), out_dtype),
        in_specs=in_specs, out_specs=tile,
        scratch_shapes=[pltpu.VMEM((tm, tn), F32)] if nk > 1 else [],
        compiler_params=_cp(PAR, PAR, ARB),
    )(a, b, *extras)


def _mm_grad(at, b, split, name):
    m, s = at.shape
    n = b.shape[1]
    ts = min(GRAD_TS, s)
    ns = s // ts
    if split == "rows":
        tm, tn = min(MM_T, m // N_DEV), min(MM_T, n)
        per = (m // N_DEV) // tm
        out_shape = (N_DEV, m // N_DEV, n)
        out_spec = pl.BlockSpec((None, tm, tn), lambda i, j, k: (i // per, i % per, j))
    elif split == "cols":
        tm, tn = min(MM_T, m), min(MM_T, n // N_DEV)
        per = (n // N_DEV) // tn
        out_shape = (N_DEV, m, n // N_DEV)
        out_spec = pl.BlockSpec((None, tm, tn), lambda i, j, k: (j // per, i, j % per))
    else:
        tm, tn = min(MM_T, m), min(MM_T, n)
        out_shape = (m, n)
        out_spec = pl.BlockSpec((tm, tn), lambda i, j, k: (i, j))

    def body(a_ref, b_ref, o_ref, acc_ref):
        k = pl.program_id(2)
        part = _dot(a_ref[...].astype(_BF), b_ref[...].astype(_BF), 1, 0)

        @pl.when(k == 0)
        def _():
            acc_ref[...] = part

        @pl.when(k > 0)
        def _():
            acc_ref[...] += part

        @pl.when(k == ns - 1)
        def _():
            o_ref[...] = acc_ref[...].astype(o_ref.dtype)

    return _pcall(
        body, name=name, grid=(m // tm, n // tn, ns), out_shape=jax.ShapeDtypeStruct(out_shape, _XBF),
        in_specs=[pl.BlockSpec((tm, ts), lambda i, j, k: (i, k)), pl.BlockSpec((ts, tn), lambda i, j, k: (k, j))],
        out_specs=out_spec, scratch_shapes=[pltpu.VMEM((tm, tn), F32)],
        compiler_params=_cp(PAR, PAR, ARB),
    )(at, b)


def _head_masks():
    lane = _iota2((1, 128), 1)
    return [(lane // HEAD_DIM) == hh for hh in range(2)]


def _softplus_parts(z):
    sp = jnp.maximum(z, 0.0) + jnp.log(1.0 + jnp.exp(-jnp.abs(z)))
    return -sp, z - sp


def _sb_key_loop(i, carry, step, lookahead, first_ahead):
    def cond(c):
        return jnp.logical_and(c[0] <= i, jnp.max(jnp.maximum(c[1][0][0], c[1][0][1])) > UNDERFLOW)

    def body(c):
        ahead = lookahead(i - c[0] - 1)
        return c[0] + 1, step(i - c[0], c[1], c[2]), ahead

    return lax.while_loop(cond, body, (jnp.int32(1), carry, first_ahead))[1]


def _sb_fwd(proj, name):
    s = proj.shape[0]
    t = min(ATT_T, s)
    nblk = s // t
    scale = HEAD_DIM ** -0.5

    def body(q_ref, k_ref, v_ref, o_ref):
        i = pl.program_id(1)
        masks = _head_masks()
        row = _iota2((t, t), 0)
        col = _iota2((t, t), 1)
        u_strict = (row > col).astype(_BF)
        causal = col < row
        qs = q_ref[...] * scale
        qm = [jnp.where(masks[hh], qs, 0.0).astype(_BF) for hh in range(2)]

        def scores(j):
            kb = k_ref[pl.ds(pl.multiple_of(jnp.maximum(j, 0) * t, t), t), :].astype(_BF)
            return tuple(_dot(qm[hh], kb, 1, 1) for hh in range(2))

        def block(j, carry, qk, diag):
            off = pl.multiple_of(j * t, t)
            vb = v_ref[pl.ds(off, t), :].astype(_BF)
            runs, outs = [], []
            for hh in range(2):
                run, o = carry[0][hh], carry[1][hh]
                z = qk[hh]
                l1, lb = _softplus_parts(z)
                if diag:
                    l1 = jnp.where(causal, l1, 0.0)
                between = run + _dot_exact(l1, u_strict, 1, 0, 2)
                a = jnp.exp(lb + between)
                if diag:
                    a = jnp.where(causal, a, 0.0)
                outs.append(o + _dot_exact(a, vb, 1, 0, 2))
                runs.append(run + jnp.sum(l1, axis=-1, keepdims=True))
            return tuple(runs), tuple(outs)

        zero, zero_o = jnp.zeros((t, 1), F32), jnp.zeros((t, 128), F32)
        ahead = scores(i - 1)
        carry = block(i, ((zero, zero), (zero_o, zero_o)), scores(i), True)
        carry = _sb_key_loop(i, carry, lambda j, c, qk: block(j, c, qk, False), scores, ahead)
        o_ref[...] = jnp.where(masks[0], carry[1][0], carry[1][1])

    kv_spec = lambda c0: pl.BlockSpec((s, 128), lambda p, i: (0, c0 // 128 + p))
    return _pcall(
        body, name=name, grid=(SB_HEADS // 2, nblk), out_shape=jax.ShapeDtypeStruct((s, 256), F32),
        in_specs=[pl.BlockSpec((t, 128), lambda p, i: (i, C_QA // 128 + p)), kv_spec(C_KA), kv_spec(C_VA)],
        out_specs=pl.BlockSpec((t, 128), lambda p, i: (i, p)), compiler_params=_cp(PAR, ARB),
    )(proj, proj, proj)


def _sb_bwd(proj, o, dmix, name):
    s = proj.shape[0]
    t = min(ATT_T, s)
    nblk = s // t
    scale = HEAD_DIM ** -0.5

    def body(q_ref, k_ref, v_ref, o_ref, do_ref, dq_ref, dk_ref, dv_ref):
        i = pl.program_id(1)
        masks = _head_masks()
        row = _iota2((t, t), 0)
        col = _iota2((t, t), 1)
        u_strict = (row > col).astype(_BF)
        u_incl = (row >= col).astype(_BF)
        causal = col < row

        @pl.when(i == 0)
        def _():
            dk_ref[...] = jnp.zeros_like(dk_ref)
            dv_ref[...] = jnp.zeros_like(dv_ref)

        qs = q_ref[...] * scale
        dov = do_ref[...]
        ov = o_ref[...]
        qm = [jnp.where(masks[hh], qs, 0.0).astype(_BF) for hh in range(2)]
        dom = [jnp.where(masks[hh], dov, 0.0).astype(_BF) for hh in range(2)]
        dsum = [jnp.sum(dom[hh].astype(F32) * ov, axis=-1, keepdims=True) for hh in range(2)]

        def scores(j):
            off = pl.multiple_of(jnp.maximum(j, 0) * t, t)
            kb = k_ref[pl.ds(off, t), :].astype(_BF)
            vb = v_ref[pl.ds(off, t), :].astype(_BF)
            return tuple((_dot(qm[hh], kb, 1, 1), _dot(dom[hh], vb, 1, 1)) for hh in range(2))

        def block(j, carry, ahead, diag):
            off = pl.multiple_of(j * t, t)
            kb = k_ref[pl.ds(off, t), :].astype(_BF)
            runs, eruns, dqs = [], [], []
            dk = dv = None
            for hh in range(2):
                run, erun, dq = carry[0][hh], carry[1][hh], carry[2][hh]
                z, da = ahead[hh]
                l1, lb = _softplus_parts(z)
                if diag:
                    l1 = jnp.where(causal, l1, 0.0)
                between = run + _dot_exact(l1, u_strict, 1, 0, 2)
                a = jnp.exp(lb + between)
                if diag:
                    a = jnp.where(causal, a, 0.0)
                g = a * da
                cum = dsum[hh] - (erun + _dot_exact(g, u_incl, 1, 0, 2))
                beta = jnp.exp(lb)
                dz = g * (1.0 - beta) - cum * beta
                if diag:
                    dz = jnp.where(causal, dz, 0.0)
                dzb = dz.astype(_BF)
                dqs.append(dq + _dot(dzb, kb, 1, 0))
                dk_h = _dot(dzb, qm[hh], 0, 0)
                dv_h = _dot(a.astype(_BF), dom[hh], 0, 0)
                dk = dk_h if dk is None else dk + dk_h
                dv = dv_h if dv is None else dv + dv_h
                runs.append(run + jnp.sum(l1, axis=-1, keepdims=True))
                eruns.append(erun + jnp.sum(g, axis=-1, keepdims=True))
            dk_ref[pl.ds(off, t), :] += dk
            dv_ref[pl.ds(off, t), :] += dv
            return tuple(runs), tuple(eruns), tuple(dqs)

        zero, zero_q = jnp.zeros((t, 1), F32), jnp.zeros((t, 128), F32)
        ahead = scores(i - 1)
        carry = block(i, ((zero, zero), (zero, zero), (zero_q, zero_q)), scores(i), True)
        carry = _sb_key_loop(i, carry, lambda j, c, a: block(j, c, a, False), scores, ahead)
        dq_ref[...] = jnp.where(masks[0], carry[2][0], carry[2][1]) * scale

    kv_spec = lambda c0: pl.BlockSpec((s, 128), lambda p, i: (0, c0 // 128 + p))
    blk = lambda c0: pl.BlockSpec((t, 128), lambda p, i: (i, c0 // 128 + p))
    acc = pl.BlockSpec((s, 128), lambda p, i: (0, p))
    shp = jax.ShapeDtypeStruct((s, 256), F32)
    return _pcall(
        body, name=name, grid=(SB_HEADS // 2, nblk), out_shape=[shp, shp, shp],
        in_specs=[blk(C_QA), kv_spec(C_KA), kv_spec(C_VA), blk(0), blk(M_SB)],
        out_specs=[blk(0), acc, acc], compiler_params=_cp(PAR, ARB),
    )(proj, proj, proj, o, dmix)


def _group_matrix(width):
    r = _iota2((width, width), 0) // HEAD_DIM
    c = _iota2((width, width), 1) // HEAD_DIM
    return (r == c).astype(_BF)


def _group_mean(v, gm):
    return _dot_exact(v, gm, 1, 0, 2) * (1.0 / HEAD_DIM)


def _fox_prep(proj, gq, gk, bf, name):
    s = proj.shape[0]
    t = min(ATT_T, s)
    nblk = s // t
    assert nblk <= 128

    def body(q_ref, k_ref, fl_ref, gq_ref, gk_ref, bf_ref, qn_ref, kn_ref, cfc_ref, cfr_ref, cfe_ref, carry_ref):
        i = pl.program_id(0)
        gm = _group_matrix(512)
        for src, gref, dst in ((q_ref, gq_ref, qn_ref), (k_ref, gk_ref, kn_ref)):
            v = src[...]
            rstd = lax.rsqrt(_group_mean(v * v, gm) + EPS)
            dst[...] = (v * rstd * gref[...]).astype(dst.dtype)

        @pl.when(i == 0)
        def _():
            carry_ref[...] = jnp.zeros_like(carry_ref)
            cfe_ref[...] = jnp.full(cfe_ref.shape, NO_BLOCK, F32)

        lane = _iota2((1, 128), 1)
        logf = jnp.where(lane < FOX_HEADS, _softplus_parts(-(fl_ref[...] + bf_ref[...]))[0], 0.0)
        lower = (_iota2((t, t), 0) >= _iota2((t, t), 1)).astype(_BF)
        cf = carry_ref[...] + _dot_exact(logf, lower, 1, 0, 3, v_left=False)
        cfc_ref[...] = cf
        rows = cf.T[:FOX_HEADS, :]
        cfr_ref[...] = rows
        cfe_ref[...] = jnp.where(lane == i, rows[:, t - 1:t], cfe_ref[...])
        carry_ref[...] = cf[t - 1:t, :]

    col = lambda w, c0: pl.BlockSpec((t, w), lambda i: (i, c0 // w))
    v512 = pl.BlockSpec((1, 512), lambda i: (0, 0))
    return _pcall(
        body, name=name, grid=(nblk,),
        out_shape=[jax.ShapeDtypeStruct((s, 512), _BF), jax.ShapeDtypeStruct((s, 512), _BF),
                   jax.ShapeDtypeStruct((s, 128), F32), jax.ShapeDtypeStruct((FOX_HEADS, s), F32),
                   jax.ShapeDtypeStruct((FOX_HEADS, 128), F32)],
        in_specs=[col(512, C_QB), col(512, C_KB), col(128, C_FL), v512, v512, pl.BlockSpec((1, 128), lambda i: (0, 0))],
        out_specs=[col(512, 0), col(512, 0), col(128, 0), pl.BlockSpec((FOX_HEADS, t), lambda i: (0, i)),
                   pl.BlockSpec((FOX_HEADS, 128), lambda i: (0, 0))],
        scratch_shapes=[pltpu.VMEM((1, 128), F32)], compiler_params=_cp(ARB),
    )(proj, proj, proj, gq, gk, bf)


def _fox_cq(cfc_blk, head):
    lane = _iota2((1, 128), 1)
    return jnp.sum(jnp.where(lane == head, cfc_blk, 0.0), axis=-1, keepdims=True)


def _fox_ck(cfr_ref, off, t, head):
    sub = _iota2((FOX_HEADS, 1), 0)
    return jnp.sum(jnp.where(sub == head, cfr_ref[:, pl.ds(off, t)], 0.0), axis=0, keepdims=True)


def _fox_blocks_needed(top, cfe_ref, head, i):
    sub = _iota2((FOX_HEADS, 1), 0)
    lane = _iota2((1, 128), 1)
    last_key = jnp.sum(jnp.where(sub == head, cfe_ref[...], 0.0), axis=0, keepdims=True)
    need = jnp.logical_and(lane < i, top - last_key > UNDERFLOW)
    return jnp.sum(need.astype(jnp.int32))


def _grid_ends(n_outer, n_inner):
    first = lambda: jnp.logical_and(pl.program_id(0) == 0, pl.program_id(1) == 0)
    last = lambda: jnp.logical_and(pl.program_id(0) == n_outer - 1, pl.program_id(1) == n_inner - 1)
    return first, last


def _fox_fwd(proj, qn, kn, cfc, cfr, cfe, qk_bound, name, comm=None):
    s = proj.shape[0]
    t = min(ATT_T, s)
    nblk = s // t
    scale = HEAD_DIM ** -0.5

    def body(q_ref, k_ref, v_ref, cfc_ref, cfr_ref, cfe_ref, qkb_ref, o_ref, lse_ref):
        p = pl.program_id(0)
        i = pl.program_id(1)
        masks = _head_masks()
        lane = _iota2((1, 128), 1)
        valid = _iota2((t, t), 1) <= _iota2((t, t), 0)
        qv = q_ref[...]
        cfc_blk = cfc_ref[...]
        heads = [2 * p + hh for hh in range(2)]
        qm = [(jnp.where(masks[hh], qv, 0) * scale).astype(_BF) for hh in range(2)]
        cq = [_fox_cq(cfc_blk, heads[hh]) for hh in range(2)]

        def scores(j):
            kb = k_ref[pl.ds(pl.multiple_of(jnp.maximum(j, 0) * t, t), t), :]
            return tuple(_dot(qm[hh], kb, 1, 1) for hh in range(2))

        def block(j, carry, qk, diag):
            off = pl.multiple_of(j * t, t)
            vb = v_ref[pl.ds(off, t), :].astype(_BF)
            ms, ls, accs = [], [], []
            for hh in range(2):
                m, l, acc = carry[0][hh], carry[1][hh], carry[2][hh]
                z = qk[hh] + (cq[hh] - _fox_ck(cfr_ref, off, t, heads[hh]))
                if diag:
                    z = jnp.where(valid, z, NEG)
                m_new = jnp.maximum(m, jnp.max(z, axis=-1, keepdims=True))
                pe = jnp.exp(z - m_new)
                alpha = jnp.exp(m - m_new)
                ms.append(m_new)
                ls.append(alpha * l + jnp.sum(pe, axis=-1, keepdims=True))
                accs.append(alpha * acc + _dot_exact(pe, vb, 1, 0, 2))
            return tuple(ms), tuple(ls), tuple(accs)

        def step(jj, c):
            ahead = scores(i - jj - 2)
            return block(i - jj - 1, c[0], c[1], False), ahead

        neg, zero, zero_o = jnp.full((t, 1), NEG, F32), jnp.zeros((t, 1), F32), jnp.zeros((t, 128), F32)
        ahead = scores(i - 1)
        carry = block(i, ((neg, neg), (zero, zero), (zero_o, zero_o)), scores(i), True)
        needed = [_fox_blocks_needed(jnp.max(qkb_ref[...] + cq[hh] - carry[0][hh]), cfe_ref, heads[hh], i)
                  for hh in range(2)]
        m, l, acc = lax.fori_loop(0, jnp.maximum(needed[0], needed[1]), step, (carry, ahead))[0]
        o_ref[...] = jnp.where(masks[0], acc[0] / l[0], acc[1] / l[1])
        lse_ref[...] = jnp.where(lane == 0, m[0] + jnp.log(l[0]), jnp.where(lane == 1, m[1] + jnp.log(l[1]), 0.0))

    c_operands, c_in_specs, c_out_shapes, c_out_specs, c_sems = _comm_operands(comm) if comm else ([], [], [], [], [])
    outs = _pcall(
        _with_comm(body, comm, 7, 2, *_grid_ends(FOX_HEADS // 2, nblk)), name=name, grid=(FOX_HEADS // 2, nblk),
        out_shape=[jax.ShapeDtypeStruct((s, 512), F32), jax.ShapeDtypeStruct((FOX_HEADS // 2, s, 128), F32)] + c_out_shapes,
        in_specs=[pl.BlockSpec((t, 128), lambda p, i: (i, p)), pl.BlockSpec((s, 128), lambda p, i: (0, p)),
                  pl.BlockSpec((s, 128), lambda p, i: (0, C_VB // 128 + p)),
                  pl.BlockSpec((t, 128), lambda p, i: (i, 0)), pl.BlockSpec((FOX_HEADS, s), lambda p, i: (0, 0)),
                  pl.BlockSpec((FOX_HEADS, 128), lambda p, i: (0, 0)), pl.BlockSpec((1, 1), lambda p, i: (0, 0))] + c_in_specs,
        out_specs=[pl.BlockSpec((t, 128), lambda p, i: (i, p)),
                   pl.BlockSpec((None, t, 128), lambda p, i: (p, i, 0))] + c_out_specs,
        scratch_shapes=c_sems, input_output_aliases=_comm_aliases(comm, 7, 2),
        compiler_params=_cp(ARB, ARB),
    )(qn, kn, proj, cfc, cfr, cfe, qk_bound, *c_operands)
    return outs[0], outs[1], list(outs[2:])


def _fox_bwd(proj, qn, kn, cfc, cfr, cfe, qk_bound, o, lse, dmix, name, comm=None):
    s = proj.shape[0]
    t = min(ATT_T, s)
    nblk = s // t
    scale = HEAD_DIM ** -0.5

    def body(q_ref, k_ref, v_ref, cfc_ref, cfr_ref, cfe_ref, qkb_ref, o_ref, lse_ref, do_ref,
             dq_ref, dk_ref, dv_ref, dcr_ref):
        p = pl.program_id(0)
        i = pl.program_id(1)
        masks = _head_masks()
        lane = _iota2((1, 128), 1)
        valid = _iota2((t, t), 1) <= _iota2((t, t), 0)
        before = (_iota2((t, t), 0) < _iota2((t, t), 1)).astype(_BF)

        @pl.when(i == 0)
        def _():
            dk_ref[...] = jnp.zeros_like(dk_ref)
            dv_ref[...] = jnp.zeros_like(dv_ref)

        @pl.when(jnp.logical_and(p == 0, i == 0))
        def _():
            dcr_ref[...] = jnp.zeros_like(dcr_ref)

        head_row = _iota2((FOX_HEADS, 1), 0)
        qv = q_ref[...]
        cfc_blk = cfc_ref[...]
        dov = do_ref[...]
        ov = o_ref[...]
        lse_blk = lse_ref[...]
        heads = [2 * p + hh for hh in range(2)]
        qm = [(jnp.where(masks[hh], qv, 0) * scale).astype(_BF) for hh in range(2)]
        dom = [jnp.where(masks[hh], dov, 0.0).astype(_BF) for hh in range(2)]
        dsum = [jnp.sum(dom[hh].astype(F32) * ov, axis=-1, keepdims=True) for hh in range(2)]
        lse_h = [jnp.sum(jnp.where(lane == hh, lse_blk, 0.0), axis=-1, keepdims=True) for hh in range(2)]
        cq = [_fox_cq(cfc_blk, heads[hh]) for hh in range(2)]
        needed = [_fox_blocks_needed(jnp.max(qkb_ref[...] + cq[hh] - lse_h[hh]), cfe_ref, heads[hh], i)
                  for hh in range(2)]
        first = i - jnp.maximum(needed[0], needed[1])

        def scores(j):
            off = pl.multiple_of(j * t, t)
            kb = k_ref[pl.ds(off, t), :]
            vb = v_ref[pl.ds(off, t), :].astype(_BF)
            return tuple((_dot(qm[hh], kb, 1, 1), _dot(dom[hh], vb, 1, 1)) for hh in range(2))

        def block(j, carry, ahead, diag):
            off = pl.multiple_of(j * t, t)
            kb = k_ref[pl.ds(off, t), :]
            dqs, pres = [], []
            dk = dv = None
            for hh in range(2):
                dq, pre = carry[0][hh], carry[1][hh]
                z = ahead[hh][0] + (cq[hh] - _fox_ck(cfr_ref, off, t, heads[hh]))
                pm = jnp.exp(z - lse_h[hh])
                if diag:
                    pm = jnp.where(valid, pm, 0.0)
                ds = pm * (ahead[hh][1] - dsum[hh])
                dsb = ds.astype(_BF)
                dqs.append(dq + _dot(dsb, kb, 1, 0))
                dk_h = _dot(dsb, qm[hh], 0, 0)
                dv_h = _dot(pm.astype(_BF), dom[hh], 0, 0)
                dk = dk_h if dk is None else dk + dk_h
                dv = dv_h if dv is None else dv + dv_h
                if diag:
                    lower_keys = jnp.where(valid, pre + _dot_exact(ds, before, 1, 0, 2), 0.0)
                    dlogf = jnp.sum(lower_keys, axis=0, keepdims=True)
                else:
                    dlogf = jnp.sum(pre) + _dot_exact(jnp.sum(ds, axis=0, keepdims=True), before, 1, 0, 3)
                dcr_ref[:, pl.ds(off, t)] += jnp.where(head_row == heads[hh], dlogf, 0.0)
                pres.append(pre + jnp.sum(ds, axis=-1, keepdims=True))
            dk_ref[pl.ds(off, t), :] += dk
            dv_ref[pl.ds(off, t), :] += dv
            return tuple(dqs), tuple(pres)

        zero, zero_q = jnp.zeros((t, 1), F32), jnp.zeros((t, 128), F32)


        def step(j, c):
            ahead = scores(j + 1)
            return block(j, c[0], c[1], False), ahead

        carry, ahead = lax.fori_loop(first, i, step, (((zero_q, zero_q), (zero, zero)), scores(first)))
        dqs, _ = block(i, carry, ahead, True)
        dq_ref[...] = jnp.where(masks[0], dqs[0], dqs[1]) * scale

    blk = lambda c0: pl.BlockSpec((t, 128), lambda p, i: (i, c0 // 128 + p))
    res = lambda c0: pl.BlockSpec((s, 128), lambda p, i: (0, c0 // 128 + p))
    shp = jax.ShapeDtypeStruct((s, 512), F32)
    c_operands, c_in_specs, c_out_shapes, c_out_specs, c_sems = _comm_operands(comm) if comm else ([], [], [], [], [])
    outs = _pcall(
        _with_comm(body, comm, 10, 4, *_grid_ends(FOX_HEADS // 2, nblk)), name=name, grid=(FOX_HEADS // 2, nblk),
        out_shape=[shp, shp, shp, jax.ShapeDtypeStruct((FOX_HEADS, s), F32)] + c_out_shapes,
        in_specs=[blk(0), res(0), res(C_VB), pl.BlockSpec((t, 128), lambda p, i: (i, 0)),
                  pl.BlockSpec((FOX_HEADS, s), lambda p, i: (0, 0)), pl.BlockSpec((FOX_HEADS, 128), lambda p, i: (0, 0)),
                  pl.BlockSpec((1, 1), lambda p, i: (0, 0)), blk(0),
                  pl.BlockSpec((None, t, 128), lambda p, i: (p, i, 0)), blk(M_FOX)] + c_in_specs,
        out_specs=[blk(0), res(0), res(0), pl.BlockSpec((FOX_HEADS, s), lambda p, i: (0, 0))] + c_out_specs,
        scratch_shapes=c_sems, input_output_aliases=_comm_aliases(comm, 10, 4),
        compiler_params=_cp(ARB, ARB),
    )(qn, kn, proj, cfc, cfr, cfe, qk_bound, o, lse, dmix, *c_operands)
    return outs[0], outs[1], outs[2], outs[3], list(outs[4:])


def _fox_prep_bwd(proj, gq, gk, bf, dqn, dkn, dlogf, name):
    s = proj.shape[0]
    t = min(ATT_T, s)
    nblk = s // t

    def body(q_ref, k_ref, fl_ref, gq_ref, gk_ref, bf_ref, dqn_ref, dkn_ref, dlogf_ref,
             dq_ref, dk_ref, dfl_ref, dgq_ref, dgk_ref, dbf_ref):
        i = pl.program_id(0)
        gm = _group_matrix(512)

        @pl.when(i == 0)
        def _():
            dgq_ref[...] = jnp.zeros_like(dgq_ref)
            dgk_ref[...] = jnp.zeros_like(dgk_ref)
            dbf_ref[...] = jnp.zeros_like(dbf_ref)

        for src, gref, dyref, dst, dgref in ((q_ref, gq_ref, dqn_ref, dq_ref, dgq_ref),
                                              (k_ref, gk_ref, dkn_ref, dk_ref, dgk_ref)):
            v = src[...]
            dy = dyref[...]
            rstd = lax.rsqrt(_group_mean(v * v, gm) + EPS)
            xhat = v * rstd
            dgref[...] += jnp.sum(dy * xhat, axis=0, keepdims=True)
            dxh = dy * gref[...]
            dst[...] = (rstd * (dxh - xhat * _group_mean(dxh * xhat, gm))).astype(dst.dtype)

        to_lanes = (_iota2((FOX_HEADS, 128), 0) == _iota2((FOX_HEADS, 128), 1)).astype(_BF)
        dlogf = _dot_exact(dlogf_ref[...], to_lanes, 0, 0, 3)
        xv = fl_ref[...] + bf_ref[...]
        e = jnp.exp(-jnp.abs(xv))
        dfl = dlogf * (jnp.where(xv >= 0.0, e, 1.0) / (1.0 + e))
        dfl_ref[...] = dfl.astype(dfl_ref.dtype)
        dbf_ref[...] += jnp.sum(dfl, axis=0, keepdims=True)

    col = lambda w, c0: pl.BlockSpec((t, w), lambda i: (i, c0 // w))
    v512 = pl.BlockSpec((1, 512), lambda i: (0, 0))
    v128 = pl.BlockSpec((1, 128), lambda i: (0, 0))
    return _pcall(
        body, name=name, grid=(nblk,),
        out_shape=[jax.ShapeDtypeStruct((s, 512), _BF), jax.ShapeDtypeStruct((s, 512), _BF),
                   jax.ShapeDtypeStruct((s, 128), _BF), jax.ShapeDtypeStruct((1, 512), F32),
                   jax.ShapeDtypeStruct((1, 512), F32), jax.ShapeDtypeStruct((1, 128), F32)],
        in_specs=[col(512, C_QB), col(512, C_KB), col(128, C_FL), v512, v512, v128, col(512, 0), col(512, 0),
                  pl.BlockSpec((FOX_HEADS, t), lambda i: (0, i))],
        out_specs=[col(512, 0), col(512, 0), col(128, 0), v512, v512, v128], compiler_params=_cp(ARB),
    )(proj, proj, proj, gq, gk, bf, dqn, dkn, dlogf)


GELU_C = 0.7978845608028654
GELU_A = 0.044715


def _gelu(x):
    return 0.5 * x * (1.0 + jnp.tanh(GELU_C * (x + GELU_A * x * x * x)))


def _gelu_grad(x):
    th = jnp.tanh(GELU_C * (x + GELU_A * x * x * x))
    return 0.5 * (1.0 + th) + 0.5 * x * (1.0 - th * th) * (GELU_C * (1.0 + 3.0 * GELU_A * x * x))


def _sgu_tril():
    return _iota2((SGU_CHUNK, SGU_CHUNK), 0) >= _iota2((SGU_CHUNK, SGU_CHUNK), 1)


def _sgu_group_masks():
    lane = _iota2((1, 256), 1)
    return [(lane // HEAD_DIM) == g for g in range(SGU_GROUPS)]


def _sgu_fwd(proj, w, gs, bexp, name):
    s = proj.shape[0]
    t = min(2 * SGU_CHUNK, s)

    def body(u_ref, v_ref, w_ref, gs_ref, b_ref, o_ref):
        gm = _group_matrix(256)
        gmask = _sgu_group_masks()
        tril = _sgu_tril()
        u = _gelu(u_ref[...])
        vg = _gelu(v_ref[...])
        vhat = (vg * lax.rsqrt(_group_mean(vg * vg, gm) + EPS) * gs_ref[...]).astype(_BF)
        for ch in range(t // SGU_CHUNK):
            rows = slice(ch * SGU_CHUNK, (ch + 1) * SGU_CHUNK)
            mixed = b_ref[...]
            for g in range(SGU_GROUPS):
                wg = jnp.where(tril, w_ref[g], 0.0).astype(_BF)
                mixed = jnp.where(gmask[g], mixed + _dot(wg, vhat[rows], 1, 0), mixed)
            o_ref[rows, :] = u[rows] * mixed

    return _pcall(
        body, name=name, grid=(s // t,), out_shape=jax.ShapeDtypeStruct((s, 256), F32),
        in_specs=[_row_spec(t, 256, C_UC // 256), _row_spec(t, 256, C_VC // 256),
                  pl.BlockSpec((SGU_GROUPS, SGU_CHUNK, SGU_CHUNK), lambda i: (0, 0, 0)), _vec_spec(256),
                  pl.BlockSpec((SGU_CHUNK, 256), lambda i: (0, 0))],
        out_specs=_row_spec(t, 256), compiler_params=_cp(PAR),
    )(proj, proj, w, gs, bexp)


def _sgu_bwd(proj, w, gs, bexp, dmix, name):
    s = proj.shape[0]
    t = min(2 * SGU_CHUNK, s)
    nstep = s // t

    def body(u_ref, v_ref, w_ref, gs_ref, b_ref, do_ref, du_ref, dv_ref, dw_ref, db_ref, dgs_ref):
        i = pl.program_id(0)
        gm = _group_matrix(256)
        gmask = _sgu_group_masks()
        tril = _sgu_tril()

        @pl.when(i == 0)
        def _():
            dw_ref[...] = jnp.zeros_like(dw_ref)
            db_ref[...] = jnp.zeros_like(db_ref)
            dgs_ref[...] = jnp.zeros_like(dgs_ref)

        uc = u_ref[...]
        vc = v_ref[...]
        u = _gelu(uc)
        vg = _gelu(vc)
        rstd = lax.rsqrt(_group_mean(vg * vg, gm) + EPS)
        xh = vg * rstd
        gsv = gs_ref[...]
        vhat = (xh * gsv).astype(_BF)
        dov = do_ref[...]
        dm = dov * u
        for ch in range(t // SGU_CHUNK):
            rows = slice(ch * SGU_CHUNK, (ch + 1) * SGU_CHUNK)
            mixed = b_ref[...]
            dvh = jnp.zeros((SGU_CHUNK, 256), F32)
            dmc = dm[rows]
            for g in range(SGU_GROUPS):
                wg = jnp.where(tril, w_ref[g], 0.0).astype(_BF)
                mixed = jnp.where(gmask[g], mixed + _dot(wg, vhat[rows], 1, 0), mixed)
                dvh = jnp.where(gmask[g], _dot(wg, dmc.astype(_BF), 0, 0), dvh)
                dw_ref[g] += _dot(jnp.where(gmask[g], dmc, 0.0).astype(_BF), vhat[rows], 1, 1)
            db_ref[...] += dmc
            du_ref[rows, :] = (dov[rows] * mixed * _gelu_grad(uc[rows])).astype(du_ref.dtype)
            xhc = xh[rows]
            dgs_ref[...] += jnp.sum(dvh * xhc, axis=0, keepdims=True)
            dxh = dvh * gsv
            dvg = rstd[rows] * (dxh - xhc * _group_mean(dxh * xhc, gm))
            dv_ref[rows, :] = (dvg * _gelu_grad(vc[rows])).astype(dv_ref.dtype)

        @pl.when(i == nstep - 1)
        def _():
            for g in range(SGU_GROUPS):
                dw_ref[g] = jnp.where(tril, dw_ref[g], 0.0)

    wspec = pl.BlockSpec((SGU_GROUPS, SGU_CHUNK, SGU_CHUNK), lambda i: (0, 0, 0))
    bspec = pl.BlockSpec((SGU_CHUNK, 256), lambda i: (0, 0))
    return _pcall(
        body, name=name, grid=(nstep,),
        out_shape=[jax.ShapeDtypeStruct((s, 256), _BF), jax.ShapeDtypeStruct((s, 256), _BF),
                   jax.ShapeDtypeStruct((SGU_GROUPS, SGU_CHUNK, SGU_CHUNK), F32),
                   jax.ShapeDtypeStruct((SGU_CHUNK, 256), F32), jax.ShapeDtypeStruct((1, 256), F32)],
        in_specs=[_row_spec(t, 256, C_UC // 256), _row_spec(t, 256, C_VC // 256), wspec, _vec_spec(256), bspec,
                  _row_spec(t, 256, M_SGU // 256)],
        out_specs=[_row_spec(t, 256), _row_spec(t, 256), wspec, bspec, _vec_spec(256)],
        compiler_params=_cp(ARB),
    )(proj, proj, w, gs, bexp, dmix)


def _ada_fwd(c_all, ada_w, name):
    depth, d, n = ada_w.shape

    def body(c_ref, w_ref, o_ref):
        cv = c_ref[...]
        cond = cv / (1.0 + jnp.exp(-cv))
        o_ref[...] = _dot_f32(cond, w_ref[...], 1, 0)

    return _pcall(
        body, name=name, grid=(depth,), out_shape=jax.ShapeDtypeStruct((depth, N_DEV, n), F32),
        in_specs=[pl.BlockSpec((N_DEV, d), lambda l: (0, 0)), pl.BlockSpec((None, d, n), lambda l: (l, 0, 0))],
        out_specs=pl.BlockSpec((None, N_DEV, n), lambda l: (l, 0, 0)), compiler_params=_cp(PAR),
    )(c_all, ada_w)


def _ada_bwd(c_all, dmod, name):
    depth, _, n = dmod.shape
    d = c_all.shape[1]

    def body(c_ref, dm_ref, o_ref):
        cv = c_ref[...]
        cond = cv / (1.0 + jnp.exp(-cv))
        o_ref[...] = _dot_f32(cond, dm_ref[...], 0, 0)

    return _pcall(
        body, name=name, grid=(depth,), out_shape=jax.ShapeDtypeStruct((depth, d, n), F32),
        in_specs=[pl.BlockSpec((N_DEV, d), lambda l: (0, 0)), pl.BlockSpec((None, N_DEV, n), lambda l: (l, 0, 0))],
        out_specs=pl.BlockSpec((None, d, n), lambda l: (l, 0, 0)), compiler_params=_cp(PAR),
    )(c_all, dmod)


def _adamw(slots, w, m, v, name):
    n, r, c = slots.shape
    tr = 256 if r % 256 == 0 else r
    bc1 = 1.0 - ADAM_B1 ** ADAM_STEP
    bc2 = 1.0 - ADAM_B2 ** ADAM_STEP

    def body(s_ref, w_ref, m_ref, v_ref, g_ref, d_ref, nm_ref, nv_ref):
        g = s_ref[0].astype(F32)
        for j in range(1, n):
            g = g + s_ref[j].astype(F32)
        m_new = ADAM_B1 * m_ref[...] + (1.0 - ADAM_B1) * g
        v_new = ADAM_B2 * v_ref[...] + (1.0 - ADAM_B2) * (g * g)
        g_ref[...] = g
        nm_ref[...] = m_new
        nv_ref[...] = v_new
        d_ref[...] = -ADAM_LR * ((m_new / bc1) / (jnp.sqrt(v_new / bc2) + ADAM_EPS) + ADAM_WD * w_ref[...])

    tile = pl.BlockSpec((tr, c), lambda i: (i, 0))
    shp = jax.ShapeDtypeStruct((r, c), F32)
    return _pcall(
        body, name=name, grid=(r // tr,), out_shape=[shp] * 4,
        in_specs=[pl.BlockSpec((n, tr, c), lambda i: (0, i, 0)), tile, tile, tile],
        out_specs=[tile] * 4, compiler_params=_cp(PAR),
    )(slots, w, m, v)


def _w_in_layout(w):
    pad = jnp.zeros(w.shape[:-1] + (IN_P - IN_W,), w.dtype)
    return jnp.concatenate([w[..., 768:FL_SRC], w[..., :768], w[..., UC_SRC:], w[..., FL_SRC:UC_SRC], pad], axis=-1)


def _w_in_unlayout(g):
    return jnp.concatenate([g[..., C_QA:C_UC], g[..., :C_QA], g[..., C_FL:C_FL + FOX_HEADS], g[..., C_UC:C_FL]], axis=-1)


SMALL = [("ada_b", DEPTH * 6 * D_MODEL), ("norm1_g", DEPTH * D_MODEL), ("norm2_g", DEPTH * D_MODEL),
         ("sgu_w", DEPTH * SGU_GROUPS * SGU_CHUNK * SGU_CHUNK), ("sgu_b", DEPTH * SGU_GROUPS * SGU_CHUNK),
         ("sgu_norm_g", DEPTH * SGU_GROUPS * HEAD_DIM), ("q_norm_g", DEPTH * HEAD_DIM), ("k_norm_g", DEPTH * HEAD_DIM),
         ("b_forget", DEPTH * FOX_HEADS), ("loss", 1)]
SMALL_ROWS = 2560


def _pack_small(parts):
    flat = jnp.concatenate([parts[name].reshape(-1).astype(F32) for name, _ in SMALL])
    return jnp.pad(flat, (0, SMALL_ROWS * 128 - flat.shape[0])).reshape(SMALL_ROWS, 128)


def _unpack_small(packed, shapes):
    flat = packed.reshape(-1)
    out, off = {}, 0
    for name, size in SMALL:
        out[name] = flat[off:off + size].reshape(shapes[name])
        off += size
    return out


def kernel(x, c, ada_w, ada_b, norm1_g, norm2_g, w_in, b_forget, q_norm_g, k_norm_g, sgu_norm_g, sgu_w, sgu_b, w_out, mlp_w1, mlp_w2, loss_target, m_ada_w, m_ada_b, m_norm1_g, m_norm2_g, m_w_in, m_b_forget, m_q_norm_g, m_k_norm_g, m_sgu_norm_g, m_sgu_w, m_sgu_b, m_w_out, m_mlp_w1, m_mlp_w2, v_ada_w, v_ada_b, v_norm1_g, v_norm2_g, v_w_in, v_b_forget, v_q_norm_g, v_k_norm_g, v_sgu_norm_g, v_sgu_w, v_sgu_b, v_w_out, v_mlp_w1, v_mlp_w2):
    me = _lin(_my_pos())
    x0 = x[0]
    target = loss_target[0]
    n_ada = ada_w.shape[2]

    shards = [w.astype(_XBF) for w in (w_in, w_out, mlp_w1, mlp_w2)]

    def whole(w_in_g, w_out_g, w1_g, w2_g):
        return (_w_in_layout(jnp.transpose(w_in_g, (1, 0, 2)).reshape(D_MODEL, IN_W)),
                w_out_g.reshape(D_MODEL, D_MODEL),
                jnp.transpose(w1_g, (1, 0, 2)).reshape(D_MODEL, D_FF),
                w2_g.reshape(D_FF, D_MODEL))

    *first, c_all = _exchange([w[0] for w in shards] + [c], False, "gather_weights")
    c_all = c_all.reshape(N_DEV, D_MODEL)
    weights = [whole(*first)]

    mod_part = _ada_fwd(c_all, ada_w, "ada_fwd")
    (mod_rows,) = _exchange([jnp.transpose(mod_part, (1, 0, 2))], True, "scatter_mod")
    mod = jnp.transpose(mod_rows, (1, 0, 2)).reshape(DEPTH, 6 * D_MODEL) + ada_b
    mods = mod.reshape(DEPTH, 6, 1, D_MODEL)

    saved = []
    xl = x0
    for l in range(DEPTH):
        sh1, sc1, g1, sh2, sc2, g2 = (mods[l, r] for r in range(6))
        n1 = norm1_g[l].reshape(1, D_MODEL)
        n2 = norm2_g[l].reshape(1, D_MODEL)
        gq = jnp.tile(q_norm_g[l], FOX_HEADS).reshape(1, 512)
        gk = jnp.tile(k_norm_g[l], FOX_HEADS).reshape(1, 512)
        bf = jnp.pad(b_forget[l], (0, 128 - FOX_HEADS)).reshape(1, 128)
        gs = sgu_norm_g[l].reshape(1, 256)
        bexp = jnp.repeat(sgu_b[l].T, HEAD_DIM, axis=1)

        w_in_l, w_out_l, w1_l, w2_l = weights[l]
        h1, h1_t = _norm_mod(xl, n1, sc1, sh1, "norm_mod")
        proj = _mm_nn(h1, w_in_l, "plain", (), "mm_in")
        o_sb = _sb_fwd(proj, "sb_fwd")
        qn, kn, cfc, cfr, cfe = _fox_prep(proj, gq, gk, bf, "fox_prep")
        qkb = (1.02 * HEAD_DIM ** 0.5 * jnp.max(jnp.abs(q_norm_g[l])) * jnp.max(jnp.abs(k_norm_g[l]))).reshape(1, 1)
        gather_next = None
        if l + 1 < DEPTH:
            gather_next = dict(arrays=[w[l + 1] for w in shards], scatter=False, recv=None, slot=None)
        o_fox, lse, gathered = _fox_fwd(proj, qn, kn, cfc, cfr, cfe, qkb, "fox_fwd", gather_next)
        if gathered:
            weights.append(whole(*gathered))
        o_sgu = _sgu_fwd(proj, sgu_w[l], gs, bexp, "sgu_fwd")
        x_mid, y1, mixed_t = _mm_out((o_sb, o_fox, o_sgu), w_out_l, xl, g1, "mm_out")
        h2, h2_t = _norm_mod(x_mid, n2, sc2, sh2, "norm_mod")
        r2, r1, r2_t = _mm_nn(h2, w1_l, "relu2", (), "mm_w1")
        x_out, y2 = _mm_nn(r2, w2_l, "resid", (x_mid, g2), "mm_w2")
        saved.append(dict(x_in=xl, h1_t=h1_t, proj=proj, o_sb=o_sb, qn=qn, kn=kn, cfc=cfc, cfr=cfr, cfe=cfe, o_fox=o_fox,
                          lse=lse, mixed_t=mixed_t, x_mid=x_mid, y1=y1, h2_t=h2_t, r2_t=r2_t, r1=r1, y2=y2,
                          n1=n1, n2=n2, gq=gq, gk=gk, bf=bf, gs=gs, bexp=bexp, qkb=qkb))
        xl = x_out

    loss_part, dx, dy2, dg2 = _loss_head(xl, target, (saved[-1]["y2"], mods[DEPTH - 1, 5]), "loss_head")

    grads_ready = None
    received = [jnp.zeros((N_DEV,) + w.shape, _XBF) for w in shards]
    small = {k: [None] * DEPTH for k in ("mod", "norm1_g", "norm2_g", "sgu_w", "sgu_b", "sgu_norm_g",
                                         "q_norm_g", "k_norm_g", "b_forget")}
    for l in reversed(range(DEPTH)):
        sv = saved[l]
        sh1, sc1, g1, sh2, sc2, g2 = (mods[l, r] for r in range(6))
        w_in_l, w_out_l, w1_l, w2_l = weights[l]
        da = _mm_nt(dy2, w2_l, "mul2", (sv["r1"],), _BF, "mm_w2_bwd")
        g_w2 = _mm_grad(sv["r2_t"], dy2, "rows", "mm_w2_grad")
        dh2 = _mm_nt(da, w1_l, "plain", (), F32, "mm_w1_bwd")
        g_w1 = _mm_grad(sv["h2_t"], da, "cols", "mm_w1_grad")
        dx_mid, dsh2, dsc2, dn2, dy1, dg1 = _norm_mod_bwd(sv["x_mid"], dh2, dx, sv["n2"], sc2, (sv["y1"], g1),
                                                          "norm_mod_bwd")
        dmix = _mm_nt(dy1, w_out_l, "plain", (), F32, "mm_out_bwd")
        g_out = _mm_grad(sv["mixed_t"], dy1, "rows", "mm_out_grad")
        dq_sb, dk_sb, dv_sb = _sb_bwd(sv["proj"], sv["o_sb"], dmix, "sb_bwd")
        send_prev = None
        if grads_ready is not None:
            send_prev = dict(arrays=grads_ready, scatter=True, recv=received, slot=l + 1)
        dqn, dkn, dv_fox, dlogf, filled = _fox_bwd(sv["proj"], sv["qn"], sv["kn"], sv["cfc"], sv["cfr"], sv["cfe"], sv["qkb"],
                                                 sv["o_fox"], sv["lse"], dmix, "fox_bwd", send_prev)
        if filled:
            received = filled
        dq_fox, dk_fox, dfl, dgq, dgk, dbf = _fox_prep_bwd(sv["proj"], sv["gq"], sv["gk"], sv["bf"], dqn, dkn, dlogf,
                                                           "fox_prep_bwd")
        duc, dvc, dsw, dsb, dsg = _sgu_bwd(sv["proj"], sgu_w[l], sv["gs"], sv["bexp"], dmix, "sgu_bwd")
        dproj = jnp.concatenate(
            [dq_fox, dk_fox, dv_fox.astype(_BF), dq_sb.astype(_BF), dk_sb.astype(_BF), dv_sb.astype(_BF), duc, dvc, dfl,
             jnp.zeros((dfl.shape[0], IN_P - C_FL - 128), _BF)], axis=-1)
        dh1 = _mm_nt(dproj, w_in_l, "plain", (), F32, "mm_in_bwd")
        g_in = _w_in_unlayout(_mm_grad(sv["h1_t"], dproj, None, "mm_in_grad"))
        g_in = jnp.transpose(g_in.reshape(D_MODEL, N_DEV, IN_W // N_DEV), (1, 0, 2))
        grads_ready = [g_in, g_out, g_w1, g_w2]
        small["mod"][l] = [dg1, dsh2, dsc2, dg2]
        below = (saved[l - 1]["y2"], mods[l - 1, 5]) if l > 0 else None
        dx, dsh1, dsc1, dn1, *below_grads = _norm_mod_bwd(sv["x_in"], dh1, dx_mid, sv["n1"], sc1, below, "norm_mod_bwd")
        if below_grads:
            dy2, dg2 = below_grads
        small["mod"][l] = jnp.concatenate([dsh1, dsc1] + small["mod"][l], axis=-1).reshape(-1)
        small["norm1_g"][l] = dn1.reshape(-1)
        small["norm2_g"][l] = dn2.reshape(-1)
        small["sgu_w"][l] = dsw
        small["sgu_b"][l] = dsb.reshape(SGU_CHUNK, SGU_GROUPS, HEAD_DIM).sum(-1).T
        small["sgu_norm_g"][l] = dsg.reshape(SGU_GROUPS, HEAD_DIM)
        small["q_norm_g"][l] = dgq.reshape(FOX_HEADS, HEAD_DIM).sum(0)
        small["k_norm_g"][l] = dgk.reshape(FOX_HEADS, HEAD_DIM).sum(0)
        small["b_forget"][l] = dbf[0, :FOX_HEADS]

    parts = {k: jnp.stack(v) for k, v in small.items()}
    parts["ada_b"] = parts.pop("mod")
    parts["loss"] = loss_part
    (small_all,) = _exchange([_pack_small(parts)], False, "gather_small")
    zero1 = jnp.zeros((1,), F32)
    packs = [_pack_small(dict(ada_b=a, norm1_g=b, norm2_g=cc, sgu_w=d, sgu_b=e, sgu_norm_g=f, q_norm_g=g, k_norm_g=h,
                              b_forget=i, loss=zero1))
             for a, b, cc, d, e, f, g, h, i in (
                 (ada_b, norm1_g, norm2_g, sgu_w, sgu_b, sgu_norm_g, q_norm_g, k_norm_g, b_forget),
                 (m_ada_b, m_norm1_g, m_norm2_g, m_sgu_w, m_sgu_b, m_sgu_norm_g, m_q_norm_g, m_k_norm_g, m_b_forget),
                 (v_ada_b, v_norm1_g, v_norm2_g, v_sgu_w, v_sgu_b, v_sgu_norm_g, v_q_norm_g, v_k_norm_g, v_b_forget))]
    small_shapes = dict(ada_b=ada_b.shape, norm1_g=norm1_g.shape, norm2_g=norm2_g.shape, sgu_w=sgu_w.shape,
                        sgu_b=sgu_b.shape, sgu_norm_g=sgu_norm_g.shape, q_norm_g=q_norm_g.shape,
                        k_norm_g=k_norm_g.shape, b_forget=b_forget.shape, loss=())
    s_out = [_unpack_small(a, small_shapes) for a in _adamw(small_all, *packs, "adamw_small")]

    dmod_all = small_all[:, :DEPTH * 6 * D_MODEL // 128, :].reshape(N_DEV, DEPTH, 6 * D_MODEL)
    dmod_mine = lax.dynamic_slice_in_dim(dmod_all, me * n_ada, n_ada, axis=2)
    g_ada = _ada_bwd(c_all, jnp.transpose(dmod_mine, (1, 0, 2)), "ada_bwd")
    ada_out = _adamw(g_ada.reshape(1, DEPTH * D_MODEL, n_ada), ada_w.reshape(-1, n_ada), m_ada_w.reshape(-1, n_ada),
                     v_ada_w.reshape(-1, n_ada), "adamw_ada")
    ada_out = [a.reshape(ada_w.shape) for a in ada_out]

    ri, ro, r1, r2 = _exchange(grads_ready, True, "scatter_grads", recv=received, slot=0)

    def big(slots, w, m, v, name):
        cdim = w.shape[-1]
        outs = _adamw(slots.reshape(N_DEV, -1, cdim), w.reshape(-1, cdim), m.reshape(-1, cdim), v.reshape(-1, cdim), name)
        return [a.reshape(w.shape) for a in outs]

    in_out = big(ri, w_in, m_w_in, v_w_in, "adamw_in")
    out_out = big(ro, w_out, m_w_out, v_w_out, "adamw_out")
    w1_out = big(r1, mlp_w1, m_mlp_w1, v_mlp_w1, "adamw_w1")
    w2_out = big(r2, mlp_w2, m_mlp_w2, v_mlp_w2, "adamw_w2")

    def leaves(idx):
        sm = s_out[idx]
        return [ada_out[idx], sm["ada_b"], sm["norm1_g"], sm["norm2_g"], in_out[idx], sm["b_forget"], sm["q_norm_g"],
                sm["k_norm_g"], sm["sgu_norm_g"], sm["sgu_w"], sm["sgu_b"], out_out[idx], w1_out[idx], w2_out[idx]]

    loss = s_out[0]["loss"]
    grad_x = dx.reshape(x.shape)
    return (loss, grad_x, *leaves(0), *leaves(1), *leaves(2), *leaves(3))
```

```python
import jax
import jax.numpy as jnp
from jax import lax
from jax.experimental import pallas as pl
from jax.experimental.pallas import tpu as pltpu

F32 = jnp.float32
_BF = jnp.bfloat16
_XBF = jnp.bfloat16

N_DEV = 8
D_MODEL = 1024
DEPTH = 4
HEAD_DIM = 64
SB_HEADS = 4
FOX_HEADS = 8
SGU_GROUPS = 4
SGU_CHUNK = 128
D_FF = 4096
EPS = 1e-6
IN_W = 2824
FL_SRC = 2304
UC_SRC = 2312

C_QB, C_KB, C_VB = 0, 512, 1024
C_QA, C_KA, C_VA = 1536, 1792, 2048
C_UC, C_VC = 2304, 2560
C_FL = 2816
IN_P = 3072
M_SB, M_FOX, M_SGU = 0, 256, 768

ADAM_LR = 0.001
ADAM_B1 = 0.9
ADAM_B2 = 0.999
ADAM_EPS = 1e-08
ADAM_WD = 0.01
ADAM_STEP = 10

ATT_T = 256
ROW_T = 512
MM_T = 1024
GRAD_TS = 2048
ROW_CHUNK = 64
NEG = -1e30
UNDERFLOW = -100.0
NO_BLOCK = 1e30

MESH = pl.DeviceIdType.MESH
ARB = pltpu.ARBITRARY
PAR = pltpu.PARALLEL
HIGHEST = lax.Precision.HIGHEST


def _pcall(body, **kw):
    return pl.pallas_call(body, **kw)


def _cp(*sem):
    return pltpu.CompilerParams(dimension_semantics=tuple(sem))


def _dot(a, b, ca, cb):
    return lax.dot_general(a, b, (((ca,), (cb,)), ((), ())), preferred_element_type=F32)


def _dot_f32(a, b, ca, cb):
    return lax.dot_general(a, b, (((ca,), (cb,)), ((), ())), precision=HIGHEST, preferred_element_type=F32)


def _split(v, parts):
    out = []
    r = v
    for _ in range(parts - 1):
        p = r.astype(_BF)
        out.append(p)
        r = r - p.astype(F32)
    out.append(r.astype(_BF))
    return out


def _dot_exact(v, m, ca, cb, parts, v_left=True):
    out = None
    for p in _split(v, parts):
        term = _dot(p, m, ca, cb) if v_left else _dot(m, p, ca, cb)
        out = term if out is None else out + term
    return out


def _iota2(shape, dim):
    return lax.broadcasted_iota(jnp.int32, shape, dim)


def _my_pos():
    return lax.axis_index("x"), lax.axis_index("y"), lax.axis_index("c")


def _flip(pos, k):
    x, y, c = pos
    px = 1 - x if (k >> 2) & 1 else x
    py = 1 - y if (k >> 1) & 1 else y
    pc = 1 - c if k & 1 else c
    return px, py, pc


def _lin(pos):
    return 4 * pos[0] + 2 * pos[1] + pos[2]


def _exchange_ops(ins, outs, sems, scatter, slot):
    send_sems, recv_sems, local_sems = sems
    n = len(ins)

    def copies(with_recvs):
        me = _my_pos()
        me_i = _lin(me)
        dst = lambda a, j: outs[a].at[j] if slot is None else outs[a].at[j, slot]
        mine = lambda a: ins[a].at[me_i] if scatter else ins[a]
        local = [pltpu.make_async_copy(mine(a), dst(a, me_i), local_sems.at[a]) for a in range(n)]
        sends, recvs = [], []
        for a in range(n):
            for k in range(1, N_DEV):
                peer = _flip(me, k)
                pair = dict(send_sem=send_sems.at[a, k - 1], recv_sem=recv_sems.at[a, k - 1],
                            device_id=peer, device_id_type=MESH)
                src = ins[a].at[_lin(peer)] if scatter else ins[a]
                sends.append(pltpu.make_async_remote_copy(src_ref=src, dst_ref=dst(a, me_i), **pair))
                if with_recvs:
                    recvs.append(pltpu.make_async_remote_copy(src_ref=mine(a), dst_ref=dst(a, _lin(peer)), **pair))
        return local, sends, recvs

    def start():
        local, sends, _ = copies(False)
        for cp in local + sends:
            cp.start()

    def wait():
        local, sends, recvs = copies(True)
        for snd, rcv in zip(sends, recvs):
            snd.wait_send()
            rcv.wait_recv()
        for cp in local:
            cp.wait()

    return start, wait


def _comm_operands(comm):
    arrays, recv = comm["arrays"], comm["recv"]
    n = len(arrays)
    any_spec = pl.BlockSpec(memory_space=pl.ANY)
    if recv is not None:
        out_shapes = [jax.ShapeDtypeStruct(r.shape, r.dtype) for r in recv]
    elif comm["scatter"]:
        out_shapes = [jax.ShapeDtypeStruct(a.shape, a.dtype) for a in arrays]
    else:
        out_shapes = [jax.ShapeDtypeStruct((N_DEV,) + a.shape, a.dtype) for a in arrays]
    operands = list(arrays) + (list(recv) if recv is not None else [])
    sems = [pltpu.SemaphoreType.DMA((n, N_DEV - 1)), pltpu.SemaphoreType.DMA((n, N_DEV - 1)),
            pltpu.SemaphoreType.DMA((n,))]
    return operands, [any_spec] * len(operands), out_shapes, [any_spec] * n, sems


def _with_comm(body, comm, n_in, n_out, first_step, last_step):
    if comm is None:
        return body
    n = len(comm["arrays"])
    n_cin = n if comm["recv"] is None else 2 * n

    def wrapped(*refs):
        ins, cins = refs[:n_in], refs[n_in:n_in + n]
        o0 = n_in + n_cin
        outs, couts, sems = refs[o0:o0 + n_out], refs[o0 + n_out:o0 + n_out + n], refs[o0 + n_out + n:]
        start, wait = _exchange_ops(cins, couts, sems, comm["scatter"], comm["slot"])
        pl.when(first_step())(start)
        body(*ins, *outs)
        pl.when(last_step())(wait)

    return wrapped


def _comm_aliases(comm, n_in, n_out):
    if comm is None or comm["recv"] is None:
        return {}
    n = len(comm["arrays"])
    return {n_in + n + a: n_out + a for a in range(n)}


def _exchange(arrays, scatter, name, recv=None, slot=None):
    comm = dict(arrays=arrays, scatter=scatter, recv=recv, slot=slot)
    operands, in_specs, out_shapes, out_specs, sems = _comm_operands(comm)
    n = len(arrays)

    def body(*refs):
        n_cin = len(operands)
        start, wait = _exchange_ops(refs[:n], refs[n_cin:n_cin + n], refs[n_cin + n:], scatter, slot)
        start()
        wait()

    outs = _pcall(body, name=name, out_shape=out_shapes, in_specs=in_specs, out_specs=out_specs, scratch_shapes=sems,
                  input_output_aliases=_comm_aliases(comm, 0, 0))(*operands)
    return list(outs)


def _row_spec(t, w, col=0):
    return pl.BlockSpec((t, w), lambda i: (i, col))


def _vec_spec(w, col=0):
    return pl.BlockSpec((1, w), lambda i: (0, col))


def _norm_mod(x, g, sc, sh, name):
    s, d = x.shape
    t = min(ROW_T, s)

    def body(x_ref, g_ref, sc_ref, sh_ref, h_ref, ht_ref):
        xf = x_ref[...]
        rstd = lax.rsqrt(jnp.mean(xf * xf, axis=-1, keepdims=True) + EPS)
        y = xf * rstd * g_ref[...]
        h = y * (1.0 + sc_ref[...]) + sh_ref[...]
        h_ref[...] = h.astype(h_ref.dtype)
        ht_ref[...] = h.T.astype(ht_ref.dtype)

    return _pcall(
        body, name=name, grid=(s // t,),
        out_shape=[jax.ShapeDtypeStruct((s, d), _BF), jax.ShapeDtypeStruct((d, s), _BF)],
        in_specs=[_row_spec(t, d), _vec_spec(d), _vec_spec(d), _vec_spec(d)],
        out_specs=[_row_spec(t, d), pl.BlockSpec((d, t), lambda i: (0, i))], compiler_params=_cp(PAR),
    )(x, g, sc, sh)


def _res_bwd_step(i, dx, y_ref, gate_ref, dy_ref, dgate_ref):
    dy_ref[...] = (dx * gate_ref[...]).astype(dy_ref.dtype)

    @pl.when(i == 0)
    def _():
        dgate_ref[...] = jnp.zeros_like(dgate_ref)

    dgate_ref[...] += jnp.sum(dx * y_ref[...], axis=0, keepdims=True)


def _norm_mod_bwd(x, dh, dres, g, sc, branch, name):
    s, d = x.shape
    t = min(ROW_T, s)
    n_branch = 0 if branch is None else 2

    ch = min(ROW_CHUNK, t)

    def body(x_ref, dh_ref, dres_ref, g_ref, sc_ref, *rest):
        dx_ref, dsh_ref, dsc_ref, dg_ref = rest[n_branch:n_branch + 4]
        i = pl.program_id(0)
        gv = g_ref[...]
        sc1 = 1.0 + sc_ref[...]
        n_sum = 3 if branch is None else 4

        def rows(r, sums):
            sl = pl.ds(pl.multiple_of(r * ch, ch), ch)
            xf = x_ref[sl, :]
            dh = dh_ref[sl, :]
            rstd = lax.rsqrt(jnp.mean(xf * xf, axis=-1, keepdims=True) + EPS)
            xhat = xf * rstd
            dn = dh * sc1
            dxh = dn * gv
            dx = dres_ref[sl, :] + rstd * (dxh - xhat * jnp.mean(dxh * xhat, axis=-1, keepdims=True))
            dx_ref[sl, :] = dx
            new = [sums[0] + jnp.sum(dh, axis=0, keepdims=True),
                   sums[1] + jnp.sum(dh * (xhat * gv), axis=0, keepdims=True),
                   sums[2] + jnp.sum(dn * xhat, axis=0, keepdims=True)]
            if branch is not None:
                rest[n_branch + 4][sl, :] = (dx * rest[1][...]).astype(_BF)
                new.append(sums[3] + jnp.sum(dx * rest[0][sl, :], axis=0, keepdims=True))
            return tuple(new)

        sums = lax.fori_loop(0, t // ch, rows, tuple(jnp.zeros((1, d), F32) for _ in range(n_sum)))
        totals = [dsh_ref, dsc_ref, dg_ref] + ([rest[n_branch + 5]] if branch is not None else [])

        @pl.when(i == 0)
        def _():
            for ref in totals:
                ref[...] = jnp.zeros_like(ref)

        for ref, part in zip(totals, sums):
            ref[...] += part

    vec = jax.ShapeDtypeStruct((1, d), F32)
    out_shape = [jax.ShapeDtypeStruct((s, d), F32), vec, vec, vec]
    in_specs = [_row_spec(t, d), _row_spec(t, d), _row_spec(t, d), _vec_spec(d), _vec_spec(d)]
    out_specs = [_row_spec(t, d), _vec_spec(d), _vec_spec(d), _vec_spec(d)]
    if branch is not None:
        in_specs += [_row_spec(t, d), _vec_spec(d)]
        out_shape += [jax.ShapeDtypeStruct((s, d), _BF), vec]
        out_specs += [_row_spec(t, d), _vec_spec(d)]
    return _pcall(
        body, name=name, grid=(s // t,), out_shape=out_shape, in_specs=in_specs, out_specs=out_specs,
        compiler_params=_cp(ARB),
    )(x, dh, dres, g, sc, *(branch or ()))


def _loss_head(y, target, branch, name):
    s, d = y.shape
    t = min(ROW_T, s)

    def body(y_ref, t_ref, yb_ref, gate_ref, loss_ref, dy_ref, dyb_ref, dgate_ref):
        i = pl.program_id(0)
        diff = y_ref[...] - t_ref[...]
        dy = diff * (1.0 / d)
        dy_ref[...] = dy

        @pl.when(i == 0)
        def _():
            loss_ref[...] = jnp.zeros_like(loss_ref)

        rows = jnp.sum(diff * diff, axis=-1, keepdims=True)
        loss_ref[...] += (0.5 / d) * jnp.sum(rows, axis=0, keepdims=True)
        _res_bwd_step(i, dy, yb_ref, gate_ref, dyb_ref, dgate_ref)

    return _pcall(
        body, name=name, grid=(s // t,),
        out_shape=[jax.ShapeDtypeStruct((1, 1), F32), jax.ShapeDtypeStruct((s, d), F32),
                   jax.ShapeDtypeStruct((s, d), _BF), jax.ShapeDtypeStruct((1, d), F32)],
        in_specs=[_row_spec(t, d), _row_spec(t, d), _row_spec(t, d), _vec_spec(d)],
        out_specs=[pl.BlockSpec((1, 1), lambda i: (0, 0)), _row_spec(t, d), _row_spec(t, d), _vec_spec(d)],
        compiler_params=_cp(ARB),
    )(y, target, *branch)


def _mm_nn(a, b, epi, extras, name):
    m, kdim = a.shape
    n = b.shape[1]
    tm, tn, tk = min(MM_T, m), min(MM_T, n), min(MM_T, kdim)
    nk = kdim // tk
    n_extra = len(extras)
    n_out = {"plain": 1, "resid": 2, "relu2": 3}[epi]

    def finish(y, extra_refs, out_refs):
        if epi == "plain":
            out_refs[0][...] = y.astype(out_refs[0].dtype)
        elif epi == "resid":
            x_ref, g_ref = extra_refs
            out_refs[0][...] = x_ref[...] + g_ref[...] * y
            out_refs[1][...] = y
        else:
            r = jnp.maximum(y, 0.0)
            out_refs[0][...] = (r * r).astype(out_refs[0].dtype)
            out_refs[1][...] = r.astype(out_refs[1].dtype)
            out_refs[2][...] = (r * r).T.astype(out_refs[2].dtype)

    def body(a_ref, b_ref, *rest):
        extra_refs = rest[:n_extra]
        out_refs = rest[n_extra:n_extra + n_out]
        part = _dot(a_ref[...].astype(_BF), b_ref[...].astype(_BF), 1, 0)
        if nk == 1:
            finish(part, extra_refs, out_refs)
        else:
            acc_ref = rest[-1]
            k = pl.program_id(2)

            @pl.when(k == 0)
            def _():
                acc_ref[...] = part

            @pl.when(k > 0)
            def _():
                acc_ref[...] += part

            @pl.when(k == nk - 1)
            def _():
                finish(acc_ref[...], extra_refs, out_refs)

    tile = pl.BlockSpec((tm, tn), lambda i, j, k: (i, j))
    in_specs = [pl.BlockSpec((tm, tk), lambda i, j, k: (i, k)), pl.BlockSpec((tk, tn), lambda i, j, k: (k, j))]
    if epi == "plain":
        out_shape = [jax.ShapeDtypeStruct((m, n), F32)]
    elif epi == "resid":
        in_specs += [tile, pl.BlockSpec((1, tn), lambda i, j, k: (0, j))]
        out_shape = [jax.ShapeDtypeStruct((m, n), F32)] * 2
    out_specs = [tile] * n_out
    if epi == "relu2":
        out_shape = [jax.ShapeDtypeStruct((m, n), _BF), jax.ShapeDtypeStruct((m, n), F32),
                     jax.ShapeDtypeStruct((n, m), _BF)]
        out_specs[2] = pl.BlockSpec((tn, tm), lambda i, j, k: (j, i))
    outs = _pcall(
        body, name=name, grid=(m // tm, n // tn, nk), out_shape=out_shape,
        in_specs=in_specs, out_specs=out_specs,
        scratch_shapes=[pltpu.VMEM((tm, tn), F32)] if nk > 1 else [],
        compiler_params=_cp(PAR, PAR, ARB),
    )(a, b, *extras)
    return outs[0] if n_out == 1 else outs


def _mm_out(parts, w, x, gate, name):
    s, n = x.shape
    widths = [p.shape[1] for p in parts]
    kdim = sum(widths)
    tm = min(ROW_T, s)

    def body(a0_ref, a1_ref, a2_ref, w_ref, x_ref, g_ref, xo_ref, y_ref, mt_ref):
        mixed = jnp.concatenate([a0_ref[...], a1_ref[...], a2_ref[...]], axis=-1)
        y = _dot(mixed.astype(_BF), w_ref[...], 1, 0)
        xo_ref[...] = x_ref[...] + g_ref[...] * y
        y_ref[...] = y
        mt_ref[...] = mixed.T.astype(mt_ref.dtype)

    row = lambda width: pl.BlockSpec((tm, width), lambda i: (i, 0))
    return _pcall(
        body, name=name, grid=(s // tm,),
        out_shape=[jax.ShapeDtypeStruct((s, n), F32), jax.ShapeDtypeStruct((s, n), F32),
                   jax.ShapeDtypeStruct((kdim, s), _BF)],
        in_specs=[row(widths[0]), row(widths[1]), row(widths[2]), pl.BlockSpec((kdim, n), lambda i: (0, 0)),
                  row(n), pl.BlockSpec((1, n), lambda i: (0, 0))],
        out_specs=[row(n), row(n), pl.BlockSpec((kdim, tm), lambda i: (0, i))], compiler_params=_cp(PAR),
    )(*parts, w, x, gate)


def _mm_nt(a, b, epi, extras, out_dtype, name):
    m, kdim = a.shape
    n = b.shape[0]
    tm, tn, tk = min(MM_T, m), min(MM_T, n), min(MM_T, kdim)
    nk = kdim // tk
    n_extra = len(extras)

    def finish(y, extra_refs, o_ref):
        if epi == "mul2":
            y = y * (2.0 * extra_refs[0][...].astype(F32))
        o_ref[...] = y.astype(o_ref.dtype)

    def body(a_ref, b_ref, *rest):
        extra_refs = rest[:n_extra]
        o_ref = rest[n_extra]
        part = _dot(a_ref[...].astype(_BF), b_ref[...].astype(_BF), 1, 1)
        if nk == 1:
            finish(part, extra_refs, o_ref)
        else:
            acc_ref = rest[-1]
            k = pl.program_id(2)

            @pl.when(k == 0)
            def _():
                acc_ref[...] = part

            @pl.when(k > 0)
            def _():
                acc_ref[...] += part

            @pl.when(k == nk - 1)
            def _():
                finish(acc_ref[...], extra_refs, o_ref)

    tile = pl.BlockSpec((tm, tn), lambda i, j, k: (i, j))
    in_specs = [pl.BlockSpec((tm, tk), lambda i, j, k: (i, k)), pl.BlockSpec((tn, tk), lambda i, j, k: (j, k))]
    in_specs += [tile] * n_extra
    return _pcall(
        body, name=name, grid=(m // tm, n // tn, nk), out_shape=jax.ShapeDtypeStruct((m, n), out_dtype),
        in_specs=in_specs, out_specs=tile,
        scratch_shapes=[pltpu.VMEM((tm, tn), F32)] if nk > 1 else [],
        compiler_params=_cp(PAR, PAR, ARB),
    )(a, b, *extras)


def _mm_grad(at, b, split, name):
    m, s = at.shape
    n = b.shape[1]
    ts = min(GRAD_TS, s)
    ns = s // ts
    if split == "rows":
        tm, tn = min(MM_T, m // N_DEV), min(MM_T, n)
        per = (m // N_DEV) // tm
        out_shape = (N_DEV, m // N_DEV, n)
        out_spec = pl.BlockSpec((None, tm, tn), lambda i, j, k: (i // per, i % per, j))
    elif split == "cols":
        tm, tn = min(MM_T, m), min(MM_T, n // N_DEV)
        per = (n // N_DEV) // tn
        out_shape = (N_DEV, m, n // N_DEV)
        out_spec = pl.BlockSpec((None, tm, tn), lambda i, j, k: (j // per, i, j % per))
    else:
        tm, tn = min(MM_T, m), min(MM_T, n)
        out_shape = (m, n)
        out_spec = pl.BlockSpec((tm, tn), lambda i, j, k: (i, j))

    def body(a_ref, b_ref, o_ref, acc_ref):
        k = pl.program_id(2)
        part = _dot(a_ref[...].astype(_BF), b_ref[...].astype(_BF), 1, 0)

        @pl.when(k == 0)
        def _():
            acc_ref[...] = part

        @pl.when(k > 0)
        def _():
            acc_ref[...] += part

        @pl.when(k == ns - 1)
        def _():
            o_ref[...] = acc_ref[...].astype(o_ref.dtype)

    return _pcall(
        body, name=name, grid=(m // tm, n // tn, ns), out_shape=jax.ShapeDtypeStruct(out_shape, _XBF),
        in_specs=[pl.BlockSpec((tm, ts), lambda i, j, k: (i, k)), pl.BlockSpec((ts, tn), lambda i, j, k: (k, j))],
        out_specs=out_spec, scratch_shapes=[pltpu.VMEM((tm, tn), F32)],
        compiler_params=_cp(PAR, PAR, ARB),
    )(at, b)


def _head_masks():
    lane = _iota2((1, 128), 1)
    return [(lane // HEAD_DIM) == hh for hh in range(2)]


def _softplus_parts(z):
    sp = jnp.maximum(z, 0.0) + jnp.log(1.0 + jnp.exp(-jnp.abs(z)))
    return -sp, z - sp


def _sb_key_loop(i, carry, step, lookahead, first_ahead):
    def cond(c):
        return jnp.logical_and(c[0] <= i, jnp.max(jnp.maximum(c[1][0][0], c[1][0][1])) > UNDERFLOW)

    def body(c):
        ahead = lookahead(i - c[0] - 1)
        return c[0] + 1, step(i - c[0], c[1], c[2]), ahead

    return lax.while_loop(cond, body, (jnp.int32(1), carry, first_ahead))[1]


def _sb_fwd(proj, name, comm=None):
    s = proj.shape[0]
    t = min(ATT_T, s)
    nblk = s // t
    scale = HEAD_DIM ** -0.5

    def body(q_ref, k_ref, v_ref, o_ref):
        i = pl.program_id(1)
        masks = _head_masks()
        row = _iota2((t, t), 0)
        col = _iota2((t, t), 1)
        u_strict = (row > col).astype(_BF)
        causal = col < row
        qs = q_ref[...] * scale
        qm = [jnp.where(masks[hh], qs, 0.0).astype(_BF) for hh in range(2)]

        def scores(j):
            kb = k_ref[pl.ds(pl.multiple_of(jnp.maximum(j, 0) * t, t), t), :].astype(_BF)
            return tuple(_dot(qm[hh], kb, 1, 1) for hh in range(2))

        def block(j, carry, qk, diag):
            off = pl.multiple_of(j * t, t)
            vb = v_ref[pl.ds(off, t), :].astype(_BF)
            runs, outs = [], []
            for hh in range(2):
                run, o = carry[0][hh], carry[1][hh]
                z = qk[hh]
                l1, lb = _softplus_parts(z)
                if diag:
                    l1 = jnp.where(causal, l1, 0.0)
                between = run + _dot_exact(l1, u_strict, 1, 0, 2)
                a = jnp.exp(lb + between)
                if diag:
                    a = jnp.where(causal, a, 0.0)
                outs.append(o + _dot_exact(a, vb, 1, 0, 2))
                runs.append(run + jnp.sum(l1, axis=-1, keepdims=True))
            return tuple(runs), tuple(outs)

        zero, zero_o = jnp.zeros((t, 1), F32), jnp.zeros((t, 128), F32)
        ahead = scores(i - 1)
        carry = block(i, ((zero, zero), (zero_o, zero_o)), scores(i), True)
        carry = _sb_key_loop(i, carry, lambda j, c, qk: block(j, c, qk, False), scores, ahead)
        o_ref[...] = jnp.where(masks[0], carry[1][0], carry[1][1])

    kv_spec = lambda c0: pl.BlockSpec((s, 128), lambda p, i: (0, c0 // 128 + p))
    c_operands, c_in_specs, c_out_shapes, c_out_specs, c_sems = _comm_operands(comm) if comm else ([], [], [], [], [])
    outs = _pcall(
        _with_comm(body, comm, 3, 1, *_grid_ends(SB_HEADS // 2, nblk)), name=name, grid=(SB_HEADS // 2, nblk),
        out_shape=[jax.ShapeDtypeStruct((s, 256), F32)] + c_out_shapes,
        in_specs=[pl.BlockSpec((t, 128), lambda p, i: (i, C_QA // 128 + p)), kv_spec(C_KA), kv_spec(C_VA)] + c_in_specs,
        out_specs=[pl.BlockSpec((t, 128), lambda p, i: (i, p))] + c_out_specs,
        scratch_shapes=c_sems, input_output_aliases=_comm_aliases(comm, 3, 1), compiler_params=_cp(ARB, ARB),
    )(proj, proj, proj, *c_operands)
    return outs[0], list(outs[1:])


def _sb_bwd(proj, o, dmix, name, comm=None):
    s = proj.shape[0]
    t = min(ATT_T, s)
    nblk = s // t
    scale = HEAD_DIM ** -0.5

    def body(q_ref, k_ref, v_ref, o_ref, do_ref, dq_ref, dk_ref, dv_ref):
        i = pl.program_id(1)
        masks = _head_masks()
        row = _iota2((t, t), 0)
        col = _iota2((t, t), 1)
        u_strict = (row > col).astype(_BF)
        u_incl = (row >= col).astype(_BF)
        causal = col < row

        @pl.when(i == 0)
        def _():
            dk_ref[...] = jnp.zeros_like(dk_ref)
            dv_ref[...] = jnp.zeros_like(dv_ref)

        qs = q_ref[...] * scale
        dov = do_ref[...]
        ov = o_ref[...]
        qm = [jnp.where(masks[hh], qs, 0.0).astype(_BF) for hh in range(2)]
        dom = [jnp.where(masks[hh], dov, 0.0).astype(_BF) for hh in range(2)]
        dsum = [jnp.sum(dom[hh].astype(F32) * ov, axis=-1, keepdims=True) for hh in range(2)]

        def scores(j):
            off = pl.multiple_of(jnp.maximum(j, 0) * t, t)
            kb = k_ref[pl.ds(off, t), :].astype(_BF)
            vb = v_ref[pl.ds(off, t), :].astype(_BF)
            return tuple((_dot(qm[hh], kb, 1, 1), _dot(dom[hh], vb, 1, 1)) for hh in range(2))

        def block(j, carry, ahead, diag):
            off = pl.multiple_of(j * t, t)
            kb = k_ref[pl.ds(off, t), :].astype(_BF)
            runs, eruns, dqs = [], [], []
            dk = dv = None
            for hh in range(2):
                run, erun, dq = carry[0][hh], carry[1][hh], carry[2][hh]
                z, da = ahead[hh]
                l1, lb = _softplus_parts(z)
                if diag:
                    l1 = jnp.where(causal, l1, 0.0)
                between = run + _dot_exact(l1, u_strict, 1, 0, 2)
                a = jnp.exp(lb + between)
                if diag:
                    a = jnp.where(causal, a, 0.0)
                g = a * da
                cum = dsum[hh] - (erun + _dot_exact(g, u_incl, 1, 0, 2))
                beta = jnp.exp(lb)
                dz = g * (1.0 - beta) - cum * beta
                if diag:
                    dz = jnp.where(causal, dz, 0.0)
                dzb = dz.astype(_BF)
                dqs.append(dq + _dot(dzb, kb, 1, 0))
                dk_h = _dot(dzb, qm[hh], 0, 0)
                dv_h = _dot(a.astype(_BF), dom[hh], 0, 0)
                dk = dk_h if dk is None else dk + dk_h
                dv = dv_h if dv is None else dv + dv_h
                runs.append(run + jnp.sum(l1, axis=-1, keepdims=True))
                eruns.append(erun + jnp.sum(g, axis=-1, keepdims=True))
            dk_ref[pl.ds(off, t), :] += dk
            dv_ref[pl.ds(off, t), :] += dv
            return tuple(runs), tuple(eruns), tuple(dqs)

        zero, zero_q = jnp.zeros((t, 1), F32), jnp.zeros((t, 128), F32)
        ahead = scores(i - 1)
        carry = block(i, ((zero, zero), (zero, zero), (zero_q, zero_q)), scores(i), True)
        carry = _sb_key_loop(i, carry, lambda j, c, a: block(j, c, a, False), scores, ahead)
        dq_ref[...] = jnp.where(masks[0], carry[2][0], carry[2][1]) * scale

    kv_spec = lambda c0: pl.BlockSpec((s, 128), lambda p, i: (0, c0 // 128 + p))
    blk = lambda c0: pl.BlockSpec((t, 128), lambda p, i: (i, c0 // 128 + p))
    acc = pl.BlockSpec((s, 128), lambda p, i: (0, p))
    shp = jax.ShapeDtypeStruct((s, 256), F32)
    c_operands, c_in_specs, c_out_shapes, c_out_specs, c_sems = _comm_operands(comm) if comm else ([], [], [], [], [])
    outs = _pcall(
        _with_comm(body, comm, 5, 3, *_grid_ends(SB_HEADS // 2, nblk)), name=name, grid=(SB_HEADS // 2, nblk),
        out_shape=[shp, shp, shp] + c_out_shapes,
        in_specs=[blk(C_QA), kv_spec(C_KA), kv_spec(C_VA), blk(0), blk(M_SB)] + c_in_specs,
        out_specs=[blk(0), acc, acc] + c_out_specs,
        scratch_shapes=c_sems, input_output_aliases=_comm_aliases(comm, 5, 3), compiler_params=_cp(ARB, ARB),
    )(proj, proj, proj, o, dmix, *c_operands)
    return outs[0], outs[1], outs[2], list(outs[3:])


def _group_matrix(width):
    r = _iota2((width, width), 0) // HEAD_DIM
    c = _iota2((width, width), 1) // HEAD_DIM
    return (r == c).astype(_BF)


def _group_mean(v, gm):
    return _dot_exact(v, gm, 1, 0, 2) * (1.0 / HEAD_DIM)


def _fox_prep(proj, gq, gk, bf, name):
    s = proj.shape[0]
    t = min(ATT_T, s)
    nblk = s // t
    assert nblk <= 128

    def body(q_ref, k_ref, fl_ref, gq_ref, gk_ref, bf_ref, qn_ref, kn_ref, cfc_ref, cfr_ref, cfe_ref, carry_ref):
        i = pl.program_id(0)
        gm = _group_matrix(512)
        for src, gref, dst in ((q_ref, gq_ref, qn_ref), (k_ref, gk_ref, kn_ref)):
            v = src[...]
            rstd = lax.rsqrt(_group_mean(v * v, gm) + EPS)
            dst[...] = (v * rstd * gref[...]).astype(dst.dtype)

        @pl.when(i == 0)
        def _():
            carry_ref[...] = jnp.zeros_like(carry_ref)
            cfe_ref[...] = jnp.full(cfe_ref.shape, NO_BLOCK, F32)

        lane = _iota2((1, 128), 1)
        logf = jnp.where(lane < FOX_HEADS, _softplus_parts(-(fl_ref[...] + bf_ref[...]))[0], 0.0)
        lower = (_iota2((t, t), 0) >= _iota2((t, t), 1)).astype(_BF)
        cf = carry_ref[...] + _dot_exact(logf, lower, 1, 0, 3, v_left=False)
        cfc_ref[...] = cf
        rows = cf.T[:FOX_HEADS, :]
        cfr_ref[...] = rows
        cfe_ref[...] = jnp.where(lane == i, rows[:, t - 1:t], cfe_ref[...])
        carry_ref[...] = cf[t - 1:t, :]

    col = lambda w, c0: pl.BlockSpec((t, w), lambda i: (i, c0 // w))
    v512 = pl.BlockSpec((1, 512), lambda i: (0, 0))
    return _pcall(
        body, name=name, grid=(nblk,),
        out_shape=[jax.ShapeDtypeStruct((s, 512), _BF), jax.ShapeDtypeStruct((s, 512), _BF),
                   jax.ShapeDtypeStruct((s, 128), F32), jax.ShapeDtypeStruct((FOX_HEADS, s), F32),
                   jax.ShapeDtypeStruct((FOX_HEADS, 128), F32)],
        in_specs=[col(512, C_QB), col(512, C_KB), col(128, C_FL), v512, v512, pl.BlockSpec((1, 128), lambda i: (0, 0))],
        out_specs=[col(512, 0), col(512, 0), col(128, 0), pl.BlockSpec((FOX_HEADS, t), lambda i: (0, i)),
                   pl.BlockSpec((FOX_HEADS, 128), lambda i: (0, 0))],
        scratch_shapes=[pltpu.VMEM((1, 128), F32)], compiler_params=_cp(ARB),
    )(proj, proj, proj, gq, gk, bf)


def _fox_cq(cfc_blk, head):
    lane = _iota2((1, 128), 1)
    return jnp.sum(jnp.where(lane == head, cfc_blk, 0.0), axis=-1, keepdims=True)


def _fox_ck(cfr_ref, off, t, head):
    sub = _iota2((FOX_HEADS, 1), 0)
    return jnp.sum(jnp.where(sub == head, cfr_ref[:, pl.ds(off, t)], 0.0), axis=0, keepdims=True)


def _fox_blocks_needed(top, cfe_ref, head, i):
    sub = _iota2((FOX_HEADS, 1), 0)
    lane = _iota2((1, 128), 1)
    last_key = jnp.sum(jnp.where(sub == head, cfe_ref[...], 0.0), axis=0, keepdims=True)
    need = jnp.logical_and(lane < i, top - last_key > UNDERFLOW)
    return jnp.sum(need.astype(jnp.int32))


def _grid_ends(n_outer, n_inner):
    first = lambda: jnp.logical_and(pl.program_id(0) == 0, pl.program_id(1) == 0)
    last = lambda: jnp.logical_and(pl.program_id(0) == n_outer - 1, pl.program_id(1) == n_inner - 1)
    return first, last


def _fox_fwd(proj, qn, kn, cfc, cfr, cfe, qk_bound, name, comm=None):
    s = proj.shape[0]
    t = min(ATT_T, s)
    nblk = s // t
    scale = HEAD_DIM ** -0.5

    def body(q_ref, k_ref, v_ref, cfc_ref, cfr_ref, cfe_ref, qkb_ref, o_ref, lse_ref):
        p = pl.program_id(0)
        i = pl.program_id(1)
        masks = _head_masks()
        lane = _iota2((1, 128), 1)
        valid = _iota2((t, t), 1) <= _iota2((t, t), 0)
        qv = q_ref[...]
        cfc_blk = cfc_ref[...]
        heads = [2 * p + hh for hh in range(2)]
        qm = [(jnp.where(masks[hh], qv, 0) * scale).astype(_BF) for hh in range(2)]
        cq = [_fox_cq(cfc_blk, heads[hh]) for hh in range(2)]

        def scores(j):
            kb = k_ref[pl.ds(pl.multiple_of(jnp.maximum(j, 0) * t, t), t), :]
            return tuple(_dot(qm[hh], kb, 1, 1) for hh in range(2))

        def block(j, carry, qk, diag):
            off = pl.multiple_of(j * t, t)
            vb = v_ref[pl.ds(off, t), :].astype(_BF)
            ms, ls, accs = [], [], []
            for hh in range(2):
                m, l, acc = carry[0][hh], carry[1][hh], carry[2][hh]
                z = qk[hh] + (cq[hh] - _fox_ck(cfr_ref, off, t, heads[hh]))
                if diag:
                    z = jnp.where(valid, z, NEG)
                m_new = jnp.maximum(m, jnp.max(z, axis=-1, keepdims=True))
                pe = jnp.exp(z - m_new)
                alpha = jnp.exp(m - m_new)
                ms.append(m_new)
                ls.append(alpha * l + jnp.sum(pe, axis=-1, keepdims=True))
                accs.append(alpha * acc + _dot_exact(pe, vb, 1, 0, 2))
            return tuple(ms), tuple(ls), tuple(accs)

        def step(jj, c):
            ahead = scores(i - jj - 2)
            return block(i - jj - 1, c[0], c[1], False), ahead

        neg, zero, zero_o = jnp.full((t, 1), NEG, F32), jnp.zeros((t, 1), F32), jnp.zeros((t, 128), F32)
        ahead = scores(i - 1)
        carry = block(i, ((neg, neg), (zero, zero), (zero_o, zero_o)), scores(i), True)
        needed = [_fox_blocks_needed(jnp.max(qkb_ref[...] + cq[hh] - carry[0][hh]), cfe_ref, heads[hh], i)
                  for hh in range(2)]
        m, l, acc = lax.fori_loop(0, jnp.maximum(needed[0], needed[1]), step, (carry, ahead))[0]
        o_ref[...] = jnp.where(masks[0], acc[0] / l[0], acc[1] / l[1])
        lse_ref[...] = jnp.where(lane == 0, m[0] + jnp.log(l[0]), jnp.where(lane == 1, m[1] + jnp.log(l[1]), 0.0))

    c_operands, c_in_specs, c_out_shapes, c_out_specs, c_sems = _comm_operands(comm) if comm else ([], [], [], [], [])
    outs = _pcall(
        _with_comm(body, comm, 7, 2, *_grid_ends(FOX_HEADS // 2, nblk)), name=name, grid=(FOX_HEADS // 2, nblk),
        out_shape=[jax.ShapeDtypeStruct((s, 512), F32), jax.ShapeDtypeStruct((FOX_HEADS // 2, s, 128), F32)] + c_out_shapes,
        in_specs=[pl.BlockSpec((t, 128), lambda p, i: (i, p)), pl.BlockSpec((s, 128), lambda p, i: (0, p)),
                  pl.BlockSpec((s, 128), lambda p, i: (0, C_VB // 128 + p)),
                  pl.BlockSpec((t, 128), lambda p, i: (i, 0)), pl.BlockSpec((FOX_HEADS, s), lambda p, i: (0, 0)),
                  pl.BlockSpec((FOX_HEADS, 128), lambda p, i: (0, 0)), pl.BlockSpec((1, 1), lambda p, i: (0, 0))] + c_in_specs,
        out_specs=[pl.BlockSpec((t, 128), lambda p, i: (i, p)),
                   pl.BlockSpec((None, t, 128), lambda p, i: (p, i, 0))] + c_out_specs,
        scratch_shapes=c_sems, input_output_aliases=_comm_aliases(comm, 7, 2),
        compiler_params=_cp(ARB, ARB),
    )(qn, kn, proj, cfc, cfr, cfe, qk_bound, *c_operands)
    return outs[0], outs[1], list(outs[2:])


def _fox_bwd(proj, qn, kn, cfc, cfr, cfe, qk_bound, o, lse, dmix, name, comm=None):
    s = proj.shape[0]
    t = min(ATT_T, s)
    nblk = s // t
    scale = HEAD_DIM ** -0.5

    def body(q_ref, k_ref, v_ref, cfc_ref, cfr_ref, cfe_ref, qkb_ref, o_ref, lse_ref, do_ref,
             dq_ref, dk_ref, dv_ref, dcr_ref):
        p = pl.program_id(0)
        i = pl.program_id(1)
        masks = _head_masks()
        lane = _iota2((1, 128), 1)
        valid = _iota2((t, t), 1) <= _iota2((t, t), 0)
        before = (_iota2((t, t), 0) < _iota2((t, t), 1)).astype(_BF)

        @pl.when(i == 0)
        def _():
            dk_ref[...] = jnp.zeros_like(dk_ref)
            dv_ref[...] = jnp.zeros_like(dv_ref)

        @pl.when(jnp.logical_and(p == 0, i == 0))
        def _():
            dcr_ref[...] = jnp.zeros_like(dcr_ref)

        head_row = _iota2((FOX_HEADS, 1), 0)
        qv = q_ref[...]
        cfc_blk = cfc_ref[...]
        dov = do_ref[...]
        ov = o_ref[...]
        lse_blk = lse_ref[...]
        heads = [2 * p + hh for hh in range(2)]
        qm = [(jnp.where(masks[hh], qv, 0) * scale).astype(_BF) for hh in range(2)]
        dom = [jnp.where(masks[hh], dov, 0.0).astype(_BF) for hh in range(2)]
        dsum = [jnp.sum(dom[hh].astype(F32) * ov, axis=-1, keepdims=True) for hh in range(2)]
        lse_h = [jnp.sum(jnp.where(lane == hh, lse_blk, 0.0), axis=-1, keepdims=True) for hh in range(2)]
        cq = [_fox_cq(cfc_blk, heads[hh]) for hh in range(2)]
        needed = [_fox_blocks_needed(jnp.max(qkb_ref[...] + cq[hh] - lse_h[hh]), cfe_ref, heads[hh], i)
                  for hh in range(2)]
        first = i - jnp.maximum(needed[0], needed[1])

        def scores(j):
            off = pl.multiple_of(j * t, t)
            kb = k_ref[pl.ds(off, t), :]
            vb = v_ref[pl.ds(off, t), :].astype(_BF)
            return tuple((_dot(qm[hh], kb, 1, 1), _dot(dom[hh], vb, 1, 1)) for hh in range(2))

        def block(j, carry, ahead, diag):
            off = pl.multiple_of(j * t, t)
            kb = k_ref[pl.ds(off, t), :]
            dqs, pres = [], []
            dk = dv = None
            for hh in range(2):
                dq, pre = carry[0][hh], carry[1][hh]
                z = ahead[hh][0] + (cq[hh] - _fox_ck(cfr_ref, off, t, heads[hh]))
                pm = jnp.exp(z - lse_h[hh])
                if diag:
                    pm = jnp.where(valid, pm, 0.0)
                ds = pm * (ahead[hh][1] - dsum[hh])
                dsb = ds.astype(_BF)
                dqs.append(dq + _dot(dsb, kb, 1, 0))
                dk_h = _dot(dsb, qm[hh], 0, 0)
                dv_h = _dot(pm.astype(_BF), dom[hh], 0, 0)
                dk = dk_h if dk is None else dk + dk_h
                dv = dv_h if dv is None else dv + dv_h
                if diag:
                    lower_keys = jnp.where(valid, pre + _dot_exact(ds, before, 1, 0, 2), 0.0)
                    dlogf = jnp.sum(lower_keys, axis=0, keepdims=True)
                else:
                    dlogf = jnp.sum(pre) + _dot_exact(jnp.sum(ds, axis=0, keepdims=True), before, 1, 0, 3)
                dcr_ref[:, pl.ds(off, t)] += jnp.where(head_row == heads[hh], dlogf, 0.0)
                pres.append(pre + jnp.sum(ds, axis=-1, keepdims=True))
            dk_ref[pl.ds(off, t), :] += dk
            dv_ref[pl.ds(off, t), :] += dv
            return tuple(dqs), tuple(pres)

        zero, zero_q = jnp.zeros((t, 1), F32), jnp.zeros((t, 128), F32)


        def step(j, c):
            ahead = scores(j + 1)
            return block(j, c[0], c[1], False), ahead

        carry, ahead = lax.fori_loop(first, i, step, (((zero_q, zero_q), (zero, zero)), scores(first)))
        dqs, _ = block(i, carry, ahead, True)
        dq_ref[...] = jnp.where(masks[0], dqs[0], dqs[1]) * scale

    blk = lambda c0: pl.BlockSpec((t, 128), lambda p, i: (i, c0 // 128 + p))
    res = lambda c0: pl.BlockSpec((s, 128), lambda p, i: (0, c0 // 128 + p))
    shp = jax.ShapeDtypeStruct((s, 512), F32)
    c_operands, c_in_specs, c_out_shapes, c_out_specs, c_sems = _comm_operands(comm) if comm else ([], [], [], [], [])
    outs = _pcall(
        _with_comm(body, comm, 10, 4, *_grid_ends(FOX_HEADS // 2, nblk)), name=name, grid=(FOX_HEADS // 2, nblk),
        out_shape=[shp, shp, shp, jax.ShapeDtypeStruct((FOX_HEADS, s), F32)] + c_out_shapes,
        in_specs=[blk(0), res(0), res(C_VB), pl.BlockSpec((t, 128), lambda p, i: (i, 0)),
                  pl.BlockSpec((FOX_HEADS, s), lambda p, i: (0, 0)), pl.BlockSpec((FOX_HEADS, 128), lambda p, i: (0, 0)),
                  pl.BlockSpec((1, 1), lambda p, i: (0, 0)), blk(0),
                  pl.BlockSpec((None, t, 128), lambda p, i: (p, i, 0)), blk(M_FOX)] + c_in_specs,
        out_specs=[blk(0), res(0), res(0), pl.BlockSpec((FOX_HEADS, s), lambda p, i: (0, 0))] + c_out_specs,
        scratch_shapes=c_sems, input_output_aliases=_comm_aliases(comm, 10, 4),
        compiler_params=_cp(ARB, ARB),
    )(qn, kn, proj, cfc, cfr, cfe, qk_bound, o, lse, dmix, *c_operands)
    return outs[0], outs[1], outs[2], outs[3], list(outs[4:])


def _fox_prep_bwd(proj, gq, gk, bf, dqn, dkn, dlogf, name):
    s = proj.shape[0]
    t = min(ATT_T, s)
    nblk = s // t

    def body(q_ref, k_ref, fl_ref, gq_ref, gk_ref, bf_ref, dqn_ref, dkn_ref, dlogf_ref,
             dq_ref, dk_ref, dfl_ref, dgq_ref, dgk_ref, dbf_ref):
        i = pl.program_id(0)
        gm = _group_matrix(512)

        @pl.when(i == 0)
        def _():
            dgq_ref[...] = jnp.zeros_like(dgq_ref)
            dgk_ref[...] = jnp.zeros_like(dgk_ref)
            dbf_ref[...] = jnp.zeros_like(dbf_ref)

        for src, gref, dyref, dst, dgref in ((q_ref, gq_ref, dqn_ref, dq_ref, dgq_ref),
                                              (k_ref, gk_ref, dkn_ref, dk_ref, dgk_ref)):
            v = src[...]
            dy = dyref[...]
            rstd = lax.rsqrt(_group_mean(v * v, gm) + EPS)
            xhat = v * rstd
            dgref[...] += jnp.sum(dy * xhat, axis=0, keepdims=True)
            dxh = dy * gref[...]
            dst[...] = (rstd * (dxh - xhat * _group_mean(dxh * xhat, gm))).astype(dst.dtype)

        to_lanes = (_iota2((FOX_HEADS, 128), 0) == _iota2((FOX_HEADS, 128), 1)).astype(_BF)
        dlogf = _dot_exact(dlogf_ref[...], to_lanes, 0, 0, 3)
        xv = fl_ref[...] + bf_ref[...]
        e = jnp.exp(-jnp.abs(xv))
        dfl = dlogf * (jnp.where(xv >= 0.0, e, 1.0) / (1.0 + e))
        dfl_ref[...] = dfl.astype(dfl_ref.dtype)
        dbf_ref[...] += jnp.sum(dfl, axis=0, keepdims=True)

    col = lambda w, c0: pl.BlockSpec((t, w), lambda i: (i, c0 // w))
    v512 = pl.BlockSpec((1, 512), lambda i: (0, 0))
    v128 = pl.BlockSpec((1, 128), lambda i: (0, 0))
    return _pcall(
        body, name=name, grid=(nblk,),
        out_shape=[jax.ShapeDtypeStruct((s, 512), _BF), jax.ShapeDtypeStruct((s, 512), _BF),
                   jax.ShapeDtypeStruct((s, 128), _BF), jax.ShapeDtypeStruct((1, 512), F32),
                   jax.ShapeDtypeStruct((1, 512), F32), jax.ShapeDtypeStruct((1, 128), F32)],
        in_specs=[col(512, C_QB), col(512, C_KB), col(128, C_FL), v512, v512, v128, col(512, 0), col(512, 0),
                  pl.BlockSpec((FOX_HEADS, t), lambda i: (0, i))],
        out_specs=[col(512, 0), col(512, 0), col(128, 0), v512, v512, v128], compiler_params=_cp(ARB),
    )(proj, proj, proj, gq, gk, bf, dqn, dkn, dlogf)


GELU_C = 0.7978845608028654
GELU_A = 0.044715


def _gelu(x):
    return 0.5 * x * (1.0 + jnp.tanh(GELU_C * (x + GELU_A * x * x * x)))


def _gelu_grad(x):
    th = jnp.tanh(GELU_C * (x + GELU_A * x * x * x))
    return 0.5 * (1.0 + th) + 0.5 * x * (1.0 - th * th) * (GELU_C * (1.0 + 3.0 * GELU_A * x * x))


def _sgu_tril():
    return _iota2((SGU_CHUNK, SGU_CHUNK), 0) >= _iota2((SGU_CHUNK, SGU_CHUNK), 1)


def _sgu_group_masks():
    lane = _iota2((1, 256), 1)
    return [(lane // HEAD_DIM) == g for g in range(SGU_GROUPS)]


def _sgu_fwd(proj, w, gs, bexp, name):
    s = proj.shape[0]
    t = min(2 * SGU_CHUNK, s)

    def body(u_ref, v_ref, w_ref, gs_ref, b_ref, o_ref):
        gm = _group_matrix(256)
        gmask = _sgu_group_masks()
        tril = _sgu_tril()
        u = _gelu(u_ref[...])
        vg = _gelu(v_ref[...])
        vhat = (vg * lax.rsqrt(_group_mean(vg * vg, gm) + EPS) * gs_ref[...]).astype(_BF)
        for ch in range(t // SGU_CHUNK):
            rows = slice(ch * SGU_CHUNK, (ch + 1) * SGU_CHUNK)
            mixed = b_ref[...]
            for g in range(SGU_GROUPS):
                wg = jnp.where(tril, w_ref[g], 0.0).astype(_BF)
                mixed = jnp.where(gmask[g], mixed + _dot(wg, vhat[rows], 1, 0), mixed)
            o_ref[rows, :] = u[rows] * mixed

    return _pcall(
        body, name=name, grid=(s // t,), out_shape=jax.ShapeDtypeStruct((s, 256), F32),
        in_specs=[_row_spec(t, 256, C_UC // 256), _row_spec(t, 256, C_VC // 256),
                  pl.BlockSpec((SGU_GROUPS, SGU_CHUNK, SGU_CHUNK), lambda i: (0, 0, 0)), _vec_spec(256),
                  pl.BlockSpec((SGU_CHUNK, 256), lambda i: (0, 0))],
        out_specs=_row_spec(t, 256), compiler_params=_cp(PAR),
    )(proj, proj, w, gs, bexp)


def _sgu_bwd(proj, w, gs, bexp, dmix, name):
    s = proj.shape[0]
    t = min(2 * SGU_CHUNK, s)
    nstep = s // t

    def body(u_ref, v_ref, w_ref, gs_ref, b_ref, do_ref, du_ref, dv_ref, dw_ref, db_ref, dgs_ref):
        i = pl.program_id(0)
        gm = _group_matrix(256)
        gmask = _sgu_group_masks()
        tril = _sgu_tril()

        @pl.when(i == 0)
        def _():
            dw_ref[...] = jnp.zeros_like(dw_ref)
            db_ref[...] = jnp.zeros_like(db_ref)
            dgs_ref[...] = jnp.zeros_like(dgs_ref)

        uc = u_ref[...]
        vc = v_ref[...]
        u = _gelu(uc)
        vg = _gelu(vc)
        rstd = lax.rsqrt(_group_mean(vg * vg, gm) + EPS)
        xh = vg * rstd
        gsv = gs_ref[...]
        vhat = (xh * gsv).astype(_BF)
        dov = do_ref[...]
        dm = dov * u
        for ch in range(t // SGU_CHUNK):
            rows = slice(ch * SGU_CHUNK, (ch + 1) * SGU_CHUNK)
            mixed = b_ref[...]
            dvh = jnp.zeros((SGU_CHUNK, 256), F32)
            dmc = dm[rows]
            for g in range(SGU_GROUPS):
                wg = jnp.where(tril, w_ref[g], 0.0).astype(_BF)
                mixed = jnp.where(gmask[g], mixed + _dot(wg, vhat[rows], 1, 0), mixed)
                dvh = jnp.where(gmask[g], _dot(wg, dmc.astype(_BF), 0, 0), dvh)
                dw_ref[g] += _dot(jnp.where(gmask[g], dmc, 0.0).astype(_BF), vhat[rows], 1, 1)
            db_ref[...] += dmc
            du_ref[rows, :] = (dov[rows] * mixed * _gelu_grad(uc[rows])).astype(du_ref.dtype)
            xhc = xh[rows]
            dgs_ref[...] += jnp.sum(dvh * xhc, axis=0, keepdims=True)
            dxh = dvh * gsv
            dvg = rstd[rows] * (dxh - xhc * _group_mean(dxh * xhc, gm))
            dv_ref[rows, :] = (dvg * _gelu_grad(vc[rows])).astype(dv_ref.dtype)

        @pl.when(i == nstep - 1)
        def _():
            for g in range(SGU_GROUPS):
                dw_ref[g] = jnp.where(tril, dw_ref[g], 0.0)

    wspec = pl.BlockSpec((SGU_GROUPS, SGU_CHUNK, SGU_CHUNK), lambda i: (0, 0, 0))
    bspec = pl.BlockSpec((SGU_CHUNK, 256), lambda i: (0, 0))
    return _pcall(
        body, name=name, grid=(nstep,),
        out_shape=[jax.ShapeDtypeStruct((s, 256), _BF), jax.ShapeDtypeStruct((s, 256), _BF),
                   jax.ShapeDtypeStruct((SGU_GROUPS, SGU_CHUNK, SGU_CHUNK), F32),
                   jax.ShapeDtypeStruct((SGU_CHUNK, 256), F32), jax.ShapeDtypeStruct((1, 256), F32)],
        in_specs=[_row_spec(t, 256, C_UC // 256), _row_spec(t, 256, C_VC // 256), wspec, _vec_spec(256), bspec,
                  _row_spec(t, 256, M_SGU // 256)],
        out_specs=[_row_spec(t, 256), _row_spec(t, 256), wspec, bspec, _vec_spec(256)],
        compiler_params=_cp(ARB),
    )(proj, proj, w, gs, bexp, dmix)


def _ada_fwd(c_all, ada_w, name):
    depth, d, n = ada_w.shape

    def body(c_ref, w_ref, o_ref):
        cv = c_ref[...]
        cond = cv / (1.0 + jnp.exp(-cv))
        o_ref[...] = _dot_f32(cond, w_ref[...], 1, 0)

    return _pcall(
        body, name=name, grid=(depth,), out_shape=jax.ShapeDtypeStruct((depth, N_DEV, n), F32),
        in_specs=[pl.BlockSpec((N_DEV, d), lambda l: (0, 0)), pl.BlockSpec((None, d, n), lambda l: (l, 0, 0))],
        out_specs=pl.BlockSpec((None, N_DEV, n), lambda l: (l, 0, 0)), compiler_params=_cp(PAR),
    )(c_all, ada_w)


def _ada_bwd(c_all, dmod, name):
    depth, _, n = dmod.shape
    d = c_all.shape[1]

    def body(c_ref, dm_ref, o_ref):
        cv = c_ref[...]
        cond = cv / (1.0 + jnp.exp(-cv))
        o_ref[...] = _dot_f32(cond, dm_ref[...], 0, 0)

    return _pcall(
        body, name=name, grid=(depth,), out_shape=jax.ShapeDtypeStruct((depth, d, n), F32),
        in_specs=[pl.BlockSpec((N_DEV, d), lambda l: (0, 0)), pl.BlockSpec((None, N_DEV, n), lambda l: (l, 0, 0))],
        out_specs=pl.BlockSpec((None, d, n), lambda l: (l, 0, 0)), compiler_params=_cp(PAR),
    )(c_all, dmod)


def _adamw(slots, w, m, v, name):
    n, r, c = slots.shape
    tr = 256 if r % 256 == 0 else r
    bc1 = 1.0 - ADAM_B1 ** ADAM_STEP
    bc2 = 1.0 - ADAM_B2 ** ADAM_STEP

    def body(s_ref, w_ref, m_ref, v_ref, g_ref, d_ref, nm_ref, nv_ref):
        g = s_ref[0].astype(F32)
        for j in range(1, n):
            g = g + s_ref[j].astype(F32)
        m_new = ADAM_B1 * m_ref[...] + (1.0 - ADAM_B1) * g
        v_new = ADAM_B2 * v_ref[...] + (1.0 - ADAM_B2) * (g * g)
        g_ref[...] = g
        nm_ref[...] = m_new
        nv_ref[...] = v_new
        d_ref[...] = -ADAM_LR * ((m_new / bc1) / (jnp.sqrt(v_new / bc2) + ADAM_EPS) + ADAM_WD * w_ref[...])

    tile = pl.BlockSpec((tr, c), lambda i: (i, 0))
    shp = jax.ShapeDtypeStruct((r, c), F32)
    return _pcall(
        body, name=name, grid=(r // tr,), out_shape=[shp] * 4,
        in_specs=[pl.BlockSpec((n, tr, c), lambda i: (0, i, 0)), tile, tile, tile],
        out_specs=[tile] * 4, compiler_params=_cp(PAR),
    )(slots, w, m, v)


def _w_in_layout(w):
    pad = jnp.zeros(w.shape[:-1] + (IN_P - IN_W,), w.dtype)
    return jnp.concatenate([w[..., 768:FL_SRC], w[..., :768], w[..., UC_SRC:], w[..., FL_SRC:UC_SRC], pad], axis=-1)


def _w_in_unlayout(g):
    return jnp.concatenate([g[..., C_QA:C_UC], g[..., :C_QA], g[..., C_FL:C_FL + FOX_HEADS], g[..., C_UC:C_FL]], axis=-1)


SMALL = [("ada_b", DEPTH * 6 * D_MODEL), ("norm1_g", DEPTH * D_MODEL), ("norm2_g", DEPTH * D_MODEL),
         ("sgu_w", DEPTH * SGU_GROUPS * SGU_CHUNK * SGU_CHUNK), ("sgu_b", DEPTH * SGU_GROUPS * SGU_CHUNK),
         ("sgu_norm_g", DEPTH * SGU_GROUPS * HEAD_DIM), ("q_norm_g", DEPTH * HEAD_DIM), ("k_norm_g", DEPTH * HEAD_DIM),
         ("b_forget", DEPTH * FOX_HEADS), ("loss", 1)]
SMALL_ROWS = 2560


def _pack_small(parts):
    flat = jnp.concatenate([parts[name].reshape(-1).astype(F32) for name, _ in SMALL])
    return jnp.pad(flat, (0, SMALL_ROWS * 128 - flat.shape[0])).reshape(SMALL_ROWS, 128)


def _unpack_small(packed, shapes):
    flat = packed.reshape(-1)
    out, off = {}, 0
    for name, size in SMALL:
        out[name] = flat[off:off + size].reshape(shapes[name])
        off += size
    return out


def kernel(x, c, ada_w, ada_b, norm1_g, norm2_g, w_in, b_forget, q_norm_g, k_norm_g, sgu_norm_g, sgu_w, sgu_b, w_out, mlp_w1, mlp_w2, loss_target, m_ada_w, m_ada_b, m_norm1_g, m_norm2_g, m_w_in, m_b_forget, m_q_norm_g, m_k_norm_g, m_sgu_norm_g, m_sgu_w, m_sgu_b, m_w_out, m_mlp_w1, m_mlp_w2, v_ada_w, v_ada_b, v_norm1_g, v_norm2_g, v_w_in, v_b_forget, v_q_norm_g, v_k_norm_g, v_sgu_norm_g, v_sgu_w, v_sgu_b, v_w_out, v_mlp_w1, v_mlp_w2):
    me = _lin(_my_pos())
    x0 = x[0]
    target = loss_target[0]
    n_ada = ada_w.shape[2]

    shards = [w.astype(_XBF) for w in (w_in, w_out, mlp_w1, mlp_w2)]

    def whole_in(w_in_g):
        return _w_in_layout(jnp.transpose(w_in_g, (1, 0, 2)).reshape(D_MODEL, IN_W))

    def whole_rest(w_out_g, w1_g, w2_g):
        return (w_out_g.reshape(D_MODEL, D_MODEL), jnp.transpose(w1_g, (1, 0, 2)).reshape(D_MODEL, D_FF),
                w2_g.reshape(D_FF, D_MODEL))

    w_in_first, c_all = _exchange([shards[0][0], c], False, "gather_weights")
    c_all = c_all.reshape(N_DEV, D_MODEL)
    weights = [None] * DEPTH

    mod_part = _ada_fwd(c_all, ada_w, "ada_fwd")
    (mod_rows,) = _exchange([jnp.transpose(mod_part, (1, 0, 2))], True, "scatter_mod")
    mod = jnp.transpose(mod_rows, (1, 0, 2)).reshape(DEPTH, 6 * D_MODEL) + ada_b
    mods = mod.reshape(DEPTH, 6, 1, D_MODEL)

    saved = []
    xl = x0
    for l in range(DEPTH):
        sh1, sc1, g1, sh2, sc2, g2 = (mods[l, r] for r in range(6))
        n1 = norm1_g[l].reshape(1, D_MODEL)
        n2 = norm2_g[l].reshape(1, D_MODEL)
        gq = jnp.tile(q_norm_g[l], FOX_HEADS).reshape(1, 512)
        gk = jnp.tile(k_norm_g[l], FOX_HEADS).reshape(1, 512)
        bf = jnp.pad(b_forget[l], (0, 128 - FOX_HEADS)).reshape(1, 128)
        gs = sgu_norm_g[l].reshape(1, 256)
        bexp = jnp.repeat(sgu_b[l].T, HEAD_DIM, axis=1)

        if l == 0:
            w_in_l = whole_in(w_in_first)
            gather_rest = dict(arrays=[w[0] for w in shards[1:]], scatter=False, recv=None, slot=None)
        else:
            w_in_l, gather_rest = weights[l][0], None
        h1, h1_t = _norm_mod(xl, n1, sc1, sh1, "norm_mod")
        proj = _mm_nn(h1, w_in_l, "plain", (), "mm_in")
        o_sb, gathered = _sb_fwd(proj, "sb_fwd", gather_rest)
        if gathered:
            weights[0] = (w_in_l,) + whole_rest(*gathered)
        qn, kn, cfc, cfr, cfe = _fox_prep(proj, gq, gk, bf, "fox_prep")
        qkb = (1.02 * HEAD_DIM ** 0.5 * jnp.max(jnp.abs(q_norm_g[l])) * jnp.max(jnp.abs(k_norm_g[l]))).reshape(1, 1)
        gather_next = None
        if l + 1 < DEPTH:
            gather_next = dict(arrays=[w[l + 1] for w in shards], scatter=False, recv=None, slot=None)
        o_fox, lse, gathered = _fox_fwd(proj, qn, kn, cfc, cfr, cfe, qkb, "fox_fwd", gather_next)
        if gathered:
            weights[l + 1] = (whole_in(gathered[0]),) + whole_rest(*gathered[1:])
        _, w_out_l, w1_l, w2_l = weights[l]
        o_sgu = _sgu_fwd(proj, sgu_w[l], gs, bexp, "sgu_fwd")
        x_mid, y1, mixed_t = _mm_out((o_sb, o_fox, o_sgu), w_out_l, xl, g1, "mm_out")
        h2, h2_t = _norm_mod(x_mid, n2, sc2, sh2, "norm_mod")
        r2, r1, r2_t = _mm_nn(h2, w1_l, "relu2", (), "mm_w1")
        x_out, y2 = _mm_nn(r2, w2_l, "resid", (x_mid, g2), "mm_w2")
        saved.append(dict(x_in=xl, h1_t=h1_t, proj=proj, o_sb=o_sb, qn=qn, kn=kn, cfc=cfc, cfr=cfr, cfe=cfe, o_fox=o_fox,
                          lse=lse, mixed_t=mixed_t, x_mid=x_mid, y1=y1, h2_t=h2_t, r2_t=r2_t, r1=r1, y2=y2,
                          n1=n1, n2=n2, gq=gq, gk=gk, bf=bf, gs=gs, bexp=bexp, qkb=qkb))
        xl = x_out

    loss_part, dx, dy2, dg2 = _loss_head(xl, target, (saved[-1]["y2"], mods[DEPTH - 1, 5]), "loss_head")

    grads_ready = None
    received = [jnp.zeros((N_DEV,) + w.shape, _XBF) for w in shards]
    small = {k: [None] * DEPTH for k in ("mod", "norm1_g", "norm2_g", "sgu_w", "sgu_b", "sgu_norm_g",
                                         "q_norm_g", "k_norm_g", "b_forget")}
    for l in reversed(range(DEPTH)):
        sv = saved[l]
        sh1, sc1, g1, sh2, sc2, g2 = (mods[l, r] for r in range(6))
        w_in_l, w_out_l, w1_l, w2_l = weights[l]
        da = _mm_nt(dy2, w2_l, "mul2", (sv["r1"],), _BF, "mm_w2_bwd")
        g_w2 = _mm_grad(sv["r2_t"], dy2, "rows", "mm_w2_grad")
        dh2 = _mm_nt(da, w1_l, "plain", (), F32, "mm_w1_bwd")
        g_w1 = _mm_grad(sv["h2_t"], da, "cols", "mm_w1_grad")
        dx_mid, dsh2, dsc2, dn2, dy1, dg1 = _norm_mod_bwd(sv["x_mid"], dh2, dx, sv["n2"], sc2, (sv["y1"], g1),
                                                          "norm_mod_bwd")
        dmix = _mm_nt(dy1, w_out_l, "plain", (), F32, "mm_out_bwd")
        g_out = _mm_grad(sv["mixed_t"], dy1, "rows", "mm_out_grad")
        send_own = None
        if l == 0:
            send_own = dict(arrays=[g_out, g_w1, g_w2], scatter=True, recv=received[1:], slot=0)
        dq_sb, dk_sb, dv_sb, filled = _sb_bwd(sv["proj"], sv["o_sb"], dmix, "sb_bwd", send_own)
        if filled:
            received = received[:1] + filled
        send_prev = None
        if grads_ready is not None:
            send_prev = dict(arrays=grads_ready, scatter=True, recv=received, slot=l + 1)
        dqn, dkn, dv_fox, dlogf, filled = _fox_bwd(sv["proj"], sv["qn"], sv["kn"], sv["cfc"], sv["cfr"], sv["cfe"], sv["qkb"],
                                                 sv["o_fox"], sv["lse"], dmix, "fox_bwd", send_prev)
        if filled:
            received = filled
        dq_fox, dk_fox, dfl, dgq, dgk, dbf = _fox_prep_bwd(sv["proj"], sv["gq"], sv["gk"], sv["bf"], dqn, dkn, dlogf,
                                                           "fox_prep_bwd")
        duc, dvc, dsw, dsb, dsg = _sgu_bwd(sv["proj"], sgu_w[l], sv["gs"], sv["bexp"], dmix, "sgu_bwd")
        dproj = jnp.concatenate(
            [dq_fox, dk_fox, dv_fox.astype(_BF), dq_sb.astype(_BF), dk_sb.astype(_BF), dv_sb.astype(_BF), duc, dvc, dfl,
             jnp.zeros((dfl.shape[0], IN_P - C_FL - 128), _BF)], axis=-1)
        dh1 = _mm_nt(dproj, w_in_l, "plain", (), F32, "mm_in_bwd")
        g_in = _w_in_unlayout(_mm_grad(sv["h1_t"], dproj, None, "mm_in_grad"))
        g_in = jnp.transpose(g_in.reshape(D_MODEL, N_DEV, IN_W // N_DEV), (1, 0, 2))
        grads_ready = [g_in, g_out, g_w1, g_w2]
        small["mod"][l] = [dg1, dsh2, dsc2, dg2]
        below = (saved[l - 1]["y2"], mods[l - 1, 5]) if l > 0 else None
        dx, dsh1, dsc1, dn1, *below_grads = _norm_mod_bwd(sv["x_in"], dh1, dx_mid, sv["n1"], sc1, below, "norm_mod_bwd")
        if below_grads:
            dy2, dg2 = below_grads
        small["mod"][l] = jnp.concatenate([dsh1, dsc1] + small["mod"][l], axis=-1).reshape(-1)
        small["norm1_g"][l] = dn1.reshape(-1)
        small["norm2_g"][l] = dn2.reshape(-1)
        small["sgu_w"][l] = dsw
        small["sgu_b"][l] = dsb.reshape(SGU_CHUNK, SGU_GROUPS, HEAD_DIM).sum(-1).T
        small["sgu_norm_g"][l] = dsg.reshape(SGU_GROUPS, HEAD_DIM)
        small["q_norm_g"][l] = dgq.reshape(FOX_HEADS, HEAD_DIM).sum(0)
        small["k_norm_g"][l] = dgk.reshape(FOX_HEADS, HEAD_DIM).sum(0)
        small["b_forget"][l] = dbf[0, :FOX_HEADS]

    parts = {k: jnp.stack(v) for k, v in small.items()}
    parts["ada_b"] = parts.pop("mod")
    parts["loss"] = loss_part
    (small_all,) = _exchange([_pack_small(parts)], False, "gather_small")
    zero1 = jnp.zeros((1,), F32)
    packs = [_pack_small(dict(ada_b=a, norm1_g=b, norm2_g=cc, sgu_w=d, sgu_b=e, sgu_norm_g=f, q_norm_g=g, k_norm_g=h,
                              b_forget=i, loss=zero1))
             for a, b, cc, d, e, f, g, h, i in (
                 (ada_b, norm1_g, norm2_g, sgu_w, sgu_b, sgu_norm_g, q_norm_g, k_norm_g, b_forget),
                 (m_ada_b, m_norm1_g, m_norm2_g, m_sgu_w, m_sgu_b, m_sgu_norm_g, m_q_norm_g, m_k_norm_g, m_b_forget),
                 (v_ada_b, v_norm1_g, v_norm2_g, v_sgu_w, v_sgu_b, v_sgu_norm_g, v_q_norm_g, v_k_norm_g, v_b_forget))]
    small_shapes = dict(ada_b=ada_b.shape, norm1_g=norm1_g.shape, norm2_g=norm2_g.shape, sgu_w=sgu_w.shape,
                        sgu_b=sgu_b.shape, sgu_norm_g=sgu_norm_g.shape, q_norm_g=q_norm_g.shape,
                        k_norm_g=k_norm_g.shape, b_forget=b_forget.shape, loss=())
    s_out = [_unpack_small(a, small_shapes) for a in _adamw(small_all, *packs, "adamw_small")]

    dmod_all = small_all[:, :DEPTH * 6 * D_MODEL // 128, :].reshape(N_DEV, DEPTH, 6 * D_MODEL)
    dmod_mine = lax.dynamic_slice_in_dim(dmod_all, me * n_ada, n_ada, axis=2)
    g_ada = _ada_bwd(c_all, jnp.transpose(dmod_mine, (1, 0, 2)), "ada_bwd")
    ada_out = _adamw(g_ada.reshape(1, DEPTH * D_MODEL, n_ada), ada_w.reshape(-1, n_ada), m_ada_w.reshape(-1, n_ada),
                     v_ada_w.reshape(-1, n_ada), "adamw_ada")
    ada_out = [a.reshape(ada_w.shape) for a in ada_out]

    (ri,) = _exchange(grads_ready[:1], True, "scatter_grads", recv=received[:1], slot=0)
    _, ro, r1, r2 = received

    def big(slots, w, m, v, name):
        cdim = w.shape[-1]
        outs = _adamw(slots.reshape(N_DEV, -1, cdim), w.reshape(-1, cdim), m.reshape(-1, cdim), v.reshape(-1, cdim), name)
        return [a.reshape(w.shape) for a in outs]

    in_out = big(ri, w_in, m_w_in, v_w_in, "adamw_in")
    out_out = big(ro, w_out, m_w_out, v_w_out, "adamw_out")
    w1_out = big(r1, mlp_w1, m_mlp_w1, v_mlp_w1, "adamw_w1")
    w2_out = big(r2, mlp_w2, m_mlp_w2, v_mlp_w2, "adamw_w2")

    def leaves(idx):
        sm = s_out[idx]
        return [ada_out[idx], sm["ada_b"], sm["norm1_g"], sm["norm2_g"], in_out[idx], sm["b_forget"], sm["q_norm_g"],
                sm["k_norm_g"], sm["sgu_norm_g"], sm["sgu_w"], sm["sgu_b"], out_out[idx], w1_out[idx], w2_out[idx]]

    loss = s_out[0]["loss"]
    grad_x = dx.reshape(x.shape)
    return (loss, grad_x, *leaves(0), *leaves(1), *leaves(2), *leaves(3))
```

```python
import jax
import jax.numpy as jnp
from jax import lax
from jax.experimental import pallas as pl
from jax.experimental.pallas import tpu as pltpu

F32 = jnp.float32
_BF = jnp.bfloat16
_XBF = jnp.bfloat16

N_DEV = 8
D_MODEL = 1024
DEPTH = 4
HEAD_DIM = 64
SB_HEADS = 4
FOX_HEADS = 8
SGU_GROUPS = 4
SGU_CHUNK = 128
D_FF = 4096
EPS = 1e-6
IN_W = 2824
FL_SRC = 2304
UC_SRC = 2312

C_QB, C_KB, C_VB = 0, 512, 1024
C_QA, C_KA, C_VA = 1536, 1792, 2048
C_UC, C_VC = 2304, 2560
C_FL = 2816
IN_P = 3072
M_SB, M_FOX, M_SGU = 0, 256, 768

ADAM_LR = 0.001
ADAM_B1 = 0.9
ADAM_B2 = 0.999
ADAM_EPS = 1e-08
ADAM_WD = 0.01
ADAM_STEP = 10

ATT_T = 256
ROW_T = 512
MM_T = 1024
GRAD_TS = 2048
ROW_CHUNK = 64
NEG = -1e30
UNDERFLOW = -90.0
NO_BLOCK = 1e30

MESH = pl.DeviceIdType.MESH
ARB = pltpu.ARBITRARY
PAR = pltpu.PARALLEL
HIGHEST = lax.Precision.HIGHEST


def _pcall(body, **kw):
    return pl.pallas_call(body, **kw)


def _cp(*sem):
    return pltpu.CompilerParams(dimension_semantics=tuple(sem))


def _dot(a, b, ca, cb):
    return lax.dot_general(a, b, (((ca,), (cb,)), ((), ())), preferred_element_type=F32)


def _dot_f32(a, b, ca, cb):
    return lax.dot_general(a, b, (((ca,), (cb,)), ((), ())), precision=HIGHEST, preferred_element_type=F32)


def _split(v, parts):
    out = []
    r = v
    for _ in range(parts - 1):
        p = r.astype(_BF)
        out.append(p)
        r = r - p.astype(F32)
    out.append(r.astype(_BF))
    return out


def _dot_exact(v, m, ca, cb, parts, v_left=True):
    out = None
    for p in _split(v, parts):
        term = _dot(p, m, ca, cb) if v_left else _dot(m, p, ca, cb)
        out = term if out is None else out + term
    return out


def _iota2(shape, dim):
    return lax.broadcasted_iota(jnp.int32, shape, dim)


def _my_pos():
    return lax.axis_index("x"), lax.axis_index("y"), lax.axis_index("c")


def _flip(pos, k):
    x, y, c = pos
    px = 1 - x if (k >> 2) & 1 else x
    py = 1 - y if (k >> 1) & 1 else y
    pc = 1 - c if k & 1 else c
    return px, py, pc


def _lin(pos):
    return 4 * pos[0] + 2 * pos[1] + pos[2]


def _exchange_ops(ins, outs, sems, scatter, slot):
    send_sems, recv_sems, local_sems = sems
    n = len(ins)

    def copies(with_recvs):
        me = _my_pos()
        me_i = _lin(me)
        dst = lambda a, j: outs[a].at[j] if slot is None else outs[a].at[j, slot]
        mine = lambda a: ins[a].at[me_i] if scatter else ins[a]
        local = [pltpu.make_async_copy(mine(a), dst(a, me_i), local_sems.at[a]) for a in range(n)]
        sends, recvs = [], []
        for a in range(n):
            for k in range(1, N_DEV):
                peer = _flip(me, k)
                pair = dict(send_sem=send_sems.at[a, k - 1], recv_sem=recv_sems.at[a, k - 1],
                            device_id=peer, device_id_type=MESH)
                src = ins[a].at[_lin(peer)] if scatter else ins[a]
                sends.append(pltpu.make_async_remote_copy(src_ref=src, dst_ref=dst(a, me_i), **pair))
                if with_recvs:
                    recvs.append(pltpu.make_async_remote_copy(src_ref=mine(a), dst_ref=dst(a, _lin(peer)), **pair))
        return local, sends, recvs

    def start():
        local, sends, _ = copies(False)
        for cp in local + sends:
            cp.start()

    def wait():
        local, sends, recvs = copies(True)
        for snd, rcv in zip(sends, recvs):
            snd.wait_send()
            rcv.wait_recv()
        for cp in local:
            cp.wait()

    return start, wait


def _comm_operands(comm):
    arrays, recv = comm["arrays"], comm["recv"]
    n = len(arrays)
    any_spec = pl.BlockSpec(memory_space=pl.ANY)
    if recv is not None:
        out_shapes = [jax.ShapeDtypeStruct(r.shape, r.dtype) for r in recv]
    elif comm["scatter"]:
        out_shapes = [jax.ShapeDtypeStruct(a.shape, a.dtype) for a in arrays]
    else:
        out_shapes = [jax.ShapeDtypeStruct((N_DEV,) + a.shape, a.dtype) for a in arrays]
    operands = list(arrays) + (list(recv) if recv is not None else [])
    sems = [pltpu.SemaphoreType.DMA((n, N_DEV - 1)), pltpu.SemaphoreType.DMA((n, N_DEV - 1)),
            pltpu.SemaphoreType.DMA((n,))]
    return operands, [any_spec] * len(operands), out_shapes, [any_spec] * n, sems


def _with_comm(body, comm, n_in, n_out, first_step, last_step):
    if comm is None:
        return body
    n = len(comm["arrays"])
    n_cin = n if comm["recv"] is None else 2 * n

    def wrapped(*refs):
        ins, cins = refs[:n_in], refs[n_in:n_in + n]
        o0 = n_in + n_cin
        outs, couts, sems = refs[o0:o0 + n_out], refs[o0 + n_out:o0 + n_out + n], refs[o0 + n_out + n:]
        start, wait = _exchange_ops(cins, couts, sems, comm["scatter"], comm["slot"])
        pl.when(first_step())(start)
        body(*ins, *outs)
        pl.when(last_step())(wait)

    return wrapped


def _comm_aliases(comm, n_in, n_out):
    if comm is None or comm["recv"] is None:
        return {}
    n = len(comm["arrays"])
    return {n_in + n + a: n_out + a for a in range(n)}


def _exchange(arrays, scatter, name, recv=None, slot=None):
    comm = dict(arrays=arrays, scatter=scatter, recv=recv, slot=slot)
    operands, in_specs, out_shapes, out_specs, sems = _comm_operands(comm)
    n = len(arrays)

    def body(*refs):
        n_cin = len(operands)
        start, wait = _exchange_ops(refs[:n], refs[n_cin:n_cin + n], refs[n_cin + n:], scatter, slot)
        start()
        wait()

    outs = _pcall(body, name=name, out_shape=out_shapes, in_specs=in_specs, out_specs=out_specs, scratch_shapes=sems,
                  input_output_aliases=_comm_aliases(comm, 0, 0))(*operands)
    return list(outs)


def _row_spec(t, w, col=0):
    return pl.BlockSpec((t, w), lambda i: (i, col))


def _vec_spec(w, col=0):
    return pl.BlockSpec((1, w), lambda i: (0, col))


def _norm_mod(x, g, sc, sh, name):
    s, d = x.shape
    t = min(ROW_T, s)

    def body(x_ref, g_ref, sc_ref, sh_ref, h_ref, ht_ref):
        xf = x_ref[...]
        rstd = lax.rsqrt(jnp.mean(xf * xf, axis=-1, keepdims=True) + EPS)
        y = xf * rstd * g_ref[...]
        h = y * (1.0 + sc_ref[...]) + sh_ref[...]
        h_ref[...] = h.astype(h_ref.dtype)
        ht_ref[...] = h.T.astype(ht_ref.dtype)

    return _pcall(
        body, name=name, grid=(s // t,),
        out_shape=[jax.ShapeDtypeStruct((s, d), _BF), jax.ShapeDtypeStruct((d, s), _BF)],
        in_specs=[_row_spec(t, d), _vec_spec(d), _vec_spec(d), _vec_spec(d)],
        out_specs=[_row_spec(t, d), pl.BlockSpec((d, t), lambda i: (0, i))], compiler_params=_cp(PAR),
    )(x, g, sc, sh)


def _res_bwd_step(i, dx, y_ref, gate_ref, dy_ref, dgate_ref):
    dy_ref[...] = (dx * gate_ref[...]).astype(dy_ref.dtype)

    @pl.when(i == 0)
    def _():
        dgate_ref[...] = jnp.zeros_like(dgate_ref)

    dgate_ref[...] += jnp.sum(dx * y_ref[...], axis=0, keepdims=True)


def _norm_mod_bwd(x, dh, dres, g, sc, branch, name):
    s, d = x.shape
    t = min(ROW_T, s)
    n_branch = 0 if branch is None else 2

    ch = min(ROW_CHUNK, t)

    def body(x_ref, dh_ref, dres_ref, g_ref, sc_ref, *rest):
        dx_ref, dsh_ref, dsc_ref, dg_ref = rest[n_branch:n_branch + 4]
        i = pl.program_id(0)
        gv = g_ref[...]
        sc1 = 1.0 + sc_ref[...]
        gain = gv * sc1
        n_sum = 2 if branch is None else 3

        def rows(r, sums):
            sl = pl.ds(pl.multiple_of(r * ch, ch), ch)
            xf = x_ref[sl, :]
            dh = dh_ref[sl, :]
            rstd = lax.rsqrt(jnp.mean(xf * xf, axis=-1, keepdims=True) + EPS)
            xhat = xf * rstd
            dh_xhat = dh * xhat
            dxh = dh * gain
            dx = dres_ref[sl, :] + rstd * (dxh - xhat * jnp.mean(dh_xhat * gain, axis=-1, keepdims=True))
            dx_ref[sl, :] = dx
            new = [sums[0] + jnp.sum(dh, axis=0, keepdims=True), sums[1] + jnp.sum(dh_xhat, axis=0, keepdims=True)]
            if branch is not None:
                rest[n_branch + 4][sl, :] = (dx * rest[1][...]).astype(_BF)
                new.append(sums[2] + jnp.sum(dx * rest[0][sl, :], axis=0, keepdims=True))
            return tuple(new)

        sums = lax.fori_loop(0, t // ch, rows, tuple(jnp.zeros((1, d), F32) for _ in range(n_sum)))
        parts = [sums[0], sums[1] * gv, sums[1] * sc1] + list(sums[2:])
        totals = [dsh_ref, dsc_ref, dg_ref] + ([rest[n_branch + 5]] if branch is not None else [])

        @pl.when(i == 0)
        def _():
            for ref in totals:
                ref[...] = jnp.zeros_like(ref)

        for ref, part in zip(totals, parts):
            ref[...] += part

    vec = jax.ShapeDtypeStruct((1, d), F32)
    out_shape = [jax.ShapeDtypeStruct((s, d), F32), vec, vec, vec]
    in_specs = [_row_spec(t, d), _row_spec(t, d), _row_spec(t, d), _vec_spec(d), _vec_spec(d)]
    out_specs = [_row_spec(t, d), _vec_spec(d), _vec_spec(d), _vec_spec(d)]
    if branch is not None:
        in_specs += [_row_spec(t, d), _vec_spec(d)]
        out_shape += [jax.ShapeDtypeStruct((s, d), _BF), vec]
        out_specs += [_row_spec(t, d), _vec_spec(d)]
    return _pcall(
        body, name=name, grid=(s // t,), out_shape=out_shape, in_specs=in_specs, out_specs=out_specs,
        compiler_params=_cp(ARB),
    )(x, dh, dres, g, sc, *(branch or ()))


def _loss_head(y, target, branch, name):
    s, d = y.shape
    t = min(ROW_T, s)

    def body(y_ref, t_ref, yb_ref, gate_ref, loss_ref, dy_ref, dyb_ref, dgate_ref):
        i = pl.program_id(0)
        diff = y_ref[...] - t_ref[...]
        dy = diff * (1.0 / d)
        dy_ref[...] = dy

        @pl.when(i == 0)
        def _():
            loss_ref[...] = jnp.zeros_like(loss_ref)

        rows = jnp.sum(diff * diff, axis=-1, keepdims=True)
        loss_ref[...] += (0.5 / d) * jnp.sum(rows, axis=0, keepdims=True)
        _res_bwd_step(i, dy, yb_ref, gate_ref, dyb_ref, dgate_ref)

    return _pcall(
        body, name=name, grid=(s // t,),
        out_shape=[jax.ShapeDtypeStruct((1, 1), F32), jax.ShapeDtypeStruct((s, d), F32),
                   jax.ShapeDtypeStruct((s, d), _BF), jax.ShapeDtypeStruct((1, d), F32)],
        in_specs=[_row_spec(t, d), _row_spec(t, d), _row_spec(t, d), _vec_spec(d)],
        out_specs=[pl.BlockSpec((1, 1), lambda i: (0, 0)), _row_spec(t, d), _row_spec(t, d), _vec_spec(d)],
        compiler_params=_cp(ARB),
    )(y, target, *branch)


def _mm_nn(a, b, epi, extras, name):
    m, kdim = a.shape
    n = b.shape[1]
    tm, tn, tk = min(MM_T, m), min(MM_T, n), min(MM_T, kdim)
    nk = kdim // tk
    n_extra = len(extras)
    n_out = {"plain": 1, "resid": 2, "relu2": 3}[epi]

    def finish(y, extra_refs, out_refs):
        if epi == "plain":
            out_refs[0][...] = y.astype(out_refs[0].dtype)
        elif epi == "resid":
            x_ref, g_ref = extra_refs
            out_refs[0][...] = x_ref[...] + g_ref[...] * y
            out_refs[1][...] = y
        else:
            r = jnp.maximum(y, 0.0)
            out_refs[0][...] = (r * r).astype(out_refs[0].dtype)
            out_refs[1][...] = r.astype(out_refs[1].dtype)
            out_refs[2][...] = (r * r).T.astype(out_refs[2].dtype)

    def body(a_ref, b_ref, *rest):
        extra_refs = rest[:n_extra]
        out_refs = rest[n_extra:n_extra + n_out]
        part = _dot(a_ref[...].astype(_BF), b_ref[...].astype(_BF), 1, 0)
        if nk == 1:
            finish(part, extra_refs, out_refs)
        else:
            acc_ref = rest[-1]
            k = pl.program_id(2)

            @pl.when(k == 0)
            def _():
                acc_ref[...] = part

            @pl.when(k > 0)
            def _():
                acc_ref[...] += part

            @pl.when(k == nk - 1)
            def _():
                finish(acc_ref[...], extra_refs, out_refs)

    tile = pl.BlockSpec((tm, tn), lambda i, j, k: (i, j))
    in_specs = [pl.BlockSpec((tm, tk), lambda i, j, k: (i, k)), pl.BlockSpec((tk, tn), lambda i, j, k: (k, j))]
    if epi == "plain":
        out_shape = [jax.ShapeDtypeStruct((m, n), F32)]
    elif epi == "resid":
        in_specs += [tile, pl.BlockSpec((1, tn), lambda i, j, k: (0, j))]
        out_shape = [jax.ShapeDtypeStruct((m, n), F32)] * 2
    out_specs = [tile] * n_out
    if epi == "relu2":
        out_shape = [jax.ShapeDtypeStruct((m, n), _BF), jax.ShapeDtypeStruct((m, n), F32),
                     jax.ShapeDtypeStruct((n, m), _BF)]
        out_specs[2] = pl.BlockSpec((tn, tm), lambda i, j, k: (j, i))
    outs = _pcall(
        body, name=name, grid=(m // tm, n // tn, nk), out_shape=out_shape,
        in_specs=in_specs, out_specs=out_specs,
        scratch_shapes=[pltpu.VMEM((tm, tn), F32)] if nk > 1 else [],
        compiler_params=_cp(PAR, PAR, ARB),
    )(a, b, *extras)
    return outs[0] if n_out == 1 else outs


def _mm_out(parts, w, x, gate, name):
    s, n = x.shape
    widths = [p.shape[1] for p in parts]
    kdim = sum(widths)
    tm = min(ROW_T, s)

    def body(a0_ref, a1_ref, a2_ref, w_ref, x_ref, g_ref, xo_ref, y_ref, mt_ref):
        mixed = jnp.concatenate([a0_ref[...], a1_ref[...], a2_ref[...]], axis=-1)
        y = _dot(mixed.astype(_BF), w_ref[...], 1, 0)
        xo_ref[...] = x_ref[...] + g_ref[...] * y
        y_ref[...] = y
        mt_ref[...] = mixed.T.astype(mt_ref.dtype)

    row = lambda width: pl.BlockSpec((tm, width), lambda i: (i, 0))
    return _pcall(
        body, name=name, grid=(s // tm,),
        out_shape=[jax.ShapeDtypeStruct((s, n), F32), jax.ShapeDtypeStruct((s, n), F32),
                   jax.ShapeDtypeStruct((kdim, s), _BF)],
        in_specs=[row(widths[0]), row(widths[1]), row(widths[2]), pl.BlockSpec((kdim, n), lambda i: (0, 0)),
                  row(n), pl.BlockSpec((1, n), lambda i: (0, 0))],
        out_specs=[row(n), row(n), pl.BlockSpec((kdim, tm), lambda i: (0, i))], compiler_params=_cp(PAR),
    )(*parts, w, x, gate)


def _mm_nt(a, b, epi, extras, out_dtype, name):
    m, kdim = a.shape
    n = b.shape[0]
    tm, tn, tk = min(MM_T, m), min(MM_T, n), min(MM_T, kdim)
    nk = kdim // tk
    n_extra = len(extras)

    def finish(y, extra_refs, o_ref):
        if epi == "mul2":
            y = y * (2.0 * extra_refs[0][...].astype(F32))
        o_ref[...] = y.astype(o_ref.dtype)

    def body(a_ref, b_ref, *rest):
        extra_refs = rest[:n_extra]
        o_ref = rest[n_extra]
        part = _dot(a_ref[...].astype(_BF), b_ref[...].astype(_BF), 1, 1)
        if nk == 1:
            finish(part, extra_refs, o_ref)
        else:
            acc_ref = rest[-1]
            k = pl.program_id(2)

            @pl.when(k == 0)
            def _():
                acc_ref[...] = part

            @pl.when(k > 0)
            def _():
                acc_ref[...] += part

            @pl.when(k == nk - 1)
            def _():
                finish(acc_ref[...], extra_refs, o_ref)

    tile = pl.BlockSpec((tm, tn), lambda i, j, k: (i, j))
    in_specs = [pl.BlockSpec((tm, tk), lambda i, j, k: (i, k)), pl.BlockSpec((tn, tk), lambda i, j, k: (j, k))]
    in_specs += [tile] * n_extra
    return _pcall(
        body, name=name, grid=(m // tm, n // tn, nk), out_shape=jax.ShapeDtypeStruct((m, n), out_dtype),
        in_specs=in_specs, out_specs=tile,
        scratch_shapes=[pltpu.VMEM((tm, tn), F32)] if nk > 1 else [],
        compiler_params=_cp(PAR, PAR, ARB),
    )(a, b, *extras)


def _mm_grad(at, b, split, name):
    m, s = at.shape
    n = b.shape[1]
    ts = min(GRAD_TS, s)
    ns = s // ts
    if split == "rows":
        tm, tn = min(MM_T, m // N_DEV), min(MM_T, n)
        per = (m // N_DEV) // tm
        out_shape = (N_DEV, m // N_DEV, n)
        out_spec = pl.BlockSpec((None, tm, tn), lambda i, j, k: (i // per, i % per, j))
    elif split == "cols":
        tm, tn = min(MM_T, m), min(MM_T, n // N_DEV)
        per = (n // N_DEV) // tn
        out_shape = (N_DEV, m, n // N_DEV)
        out_spec = pl.BlockSpec((None, tm, tn), lambda i, j, k: (j // per, i, j % per))
    else:
        tm, tn = min(MM_T, m), min(MM_T, n)
        out_shape = (m, n)
        out_spec = pl.BlockSpec((tm, tn), lambda i, j, k: (i, j))

    def body(a_ref, b_ref, o_ref, acc_ref):
        k = pl.program_id(2)
        part = _dot(a_ref[...].astype(_BF), b_ref[...].astype(_BF), 1, 0)

        @pl.when(k == 0)
        def _():
            acc_ref[...] = part

        @pl.when(k > 0)
        def _():
            acc_ref[...] += part

        @pl.when(k == ns - 1)
        def _():
            o_ref[...] = acc_ref[...].astype(o_ref.dtype)

    return _pcall(
        body, name=name, grid=(m // tm, n // tn, ns), out_shape=jax.ShapeDtypeStruct(out_shape, _XBF),
        in_specs=[pl.BlockSpec((tm, ts), lambda i, j, k: (i, k)), pl.BlockSpec((ts, tn), lambda i, j, k: (k, j))],
        out_specs=out_spec, scratch_shapes=[pltpu.VMEM((tm, tn), F32)],
        compiler_params=_cp(PAR, PAR, ARB),
    )(at, b)


def _head_masks():
    lane = _iota2((1, 128), 1)
    return [(lane // HEAD_DIM) == hh for hh in range(2)]


def _softplus_parts(z):
    sp = jnp.maximum(z, 0.0) + jnp.log(1.0 + jnp.exp(-jnp.abs(z)))
    return -sp, z - sp


def _sb_key_loop(i, carry, step, lookahead, first_ahead):
    def cond(c):
        return jnp.logical_and(c[0] <= i, jnp.max(jnp.maximum(c[1][0][0], c[1][0][1])) > UNDERFLOW)

    def body(c):
        ahead = lookahead(i - c[0] - 1)
        return c[0] + 1, step(i - c[0], c[1], c[2]), ahead

    return lax.while_loop(cond, body, (jnp.int32(1), carry, first_ahead))[1]


def _sb_fwd(proj, name, comm=None):
    s = proj.shape[0]
    t = min(ATT_T, s)
    nblk = s // t
    scale = HEAD_DIM ** -0.5

    def body(q_ref, k_ref, v_ref, o_ref):
        i = pl.program_id(1)
        masks = _head_masks()
        row = _iota2((t, t), 0)
        col = _iota2((t, t), 1)
        u_strict = (row > col).astype(_BF)
        causal = col < row
        qs = q_ref[...] * scale
        qm = [jnp.where(masks[hh], qs, 0.0).astype(_BF) for hh in range(2)]

        def scores(j):
            kb = k_ref[pl.ds(pl.multiple_of(jnp.maximum(j, 0) * t, t), t), :].astype(_BF)
            return tuple(_dot(qm[hh], kb, 1, 1) for hh in range(2))

        def block(j, carry, qk, diag):
            off = pl.multiple_of(j * t, t)
            vb = v_ref[pl.ds(off, t), :].astype(_BF)
            runs, outs = [], []
            for hh in range(2):
                run, o = carry[0][hh], carry[1][hh]
                z = qk[hh]
                l1, lb = _softplus_parts(z)
                if diag:
                    l1 = jnp.where(causal, l1, 0.0)
                between = run + _dot_exact(l1, u_strict, 1, 0, 2)
                a = jnp.exp(lb + between)
                if diag:
                    a = jnp.where(causal, a, 0.0)
                outs.append(o + _dot_exact(a, vb, 1, 0, 2))
                runs.append(run + jnp.sum(l1, axis=-1, keepdims=True))
            return tuple(runs), tuple(outs)

        zero, zero_o = jnp.zeros((t, 1), F32), jnp.zeros((t, 128), F32)
        ahead = scores(i - 1)
        carry = block(i, ((zero, zero), (zero_o, zero_o)), scores(i), True)
        carry = _sb_key_loop(i, carry, lambda j, c, qk: block(j, c, qk, False), scores, ahead)
        o_ref[...] = jnp.where(masks[0], carry[1][0], carry[1][1])

    kv_spec = lambda c0: pl.BlockSpec((s, 128), lambda p, i: (0, c0 // 128 + p))
    c_operands, c_in_specs, c_out_shapes, c_out_specs, c_sems = _comm_operands(comm) if comm else ([], [], [], [], [])
    outs = _pcall(
        _with_comm(body, comm, 3, 1, *_grid_ends(SB_HEADS // 2, nblk)), name=name, grid=(SB_HEADS // 2, nblk),
        out_shape=[jax.ShapeDtypeStruct((s, 256), F32)] + c_out_shapes,
        in_specs=[pl.BlockSpec((t, 128), lambda p, i: (i, C_QA // 128 + p)), kv_spec(C_KA), kv_spec(C_VA)] + c_in_specs,
        out_specs=[pl.BlockSpec((t, 128), lambda p, i: (i, p))] + c_out_specs,
        scratch_shapes=c_sems, input_output_aliases=_comm_aliases(comm, 3, 1), compiler_params=_cp(ARB, ARB),
    )(proj, proj, proj, *c_operands)
    return outs[0], list(outs[1:])


def _sb_bwd(proj, o, dmix, name, comm=None):
    s = proj.shape[0]
    t = min(ATT_T, s)
    nblk = s // t
    scale = HEAD_DIM ** -0.5

    def body(q_ref, k_ref, v_ref, o_ref, do_ref, dq_ref, dk_ref, dv_ref):
        i = pl.program_id(1)
        masks = _head_masks()
        row = _iota2((t, t), 0)
        col = _iota2((t, t), 1)
        u_strict = (row > col).astype(_BF)
        u_incl = (row >= col).astype(_BF)
        causal = col < row

        @pl.when(i == 0)
        def _():
            dk_ref[...] = jnp.zeros_like(dk_ref)
            dv_ref[...] = jnp.zeros_like(dv_ref)

        qs = q_ref[...] * scale
        dov = do_ref[...]
        ov = o_ref[...]
        qm = [jnp.where(masks[hh], qs, 0.0).astype(_BF) for hh in range(2)]
        dom = [jnp.where(masks[hh], dov, 0.0).astype(_BF) for hh in range(2)]
        dsum = [jnp.sum(dom[hh].astype(F32) * ov, axis=-1, keepdims=True) for hh in range(2)]

        def scores(j):
            off = pl.multiple_of(jnp.maximum(j, 0) * t, t)
            kb = k_ref[pl.ds(off, t), :].astype(_BF)
            vb = v_ref[pl.ds(off, t), :].astype(_BF)
            return tuple((_dot(qm[hh], kb, 1, 1), _dot(dom[hh], vb, 1, 1)) for hh in range(2))

        def block(j, carry, ahead, diag):
            off = pl.multiple_of(j * t, t)
            kb = k_ref[pl.ds(off, t), :].astype(_BF)
            runs, eruns, dqs = [], [], []
            dk = dv = None
            for hh in range(2):
                run, erun, dq = carry[0][hh], carry[1][hh], carry[2][hh]
                z, da = ahead[hh]
                l1, lb = _softplus_parts(z)
                if diag:
                    l1 = jnp.where(causal, l1, 0.0)
                between = run + _dot_exact(l1, u_strict, 1, 0, 2)
                a = jnp.exp(lb + between)
                if diag:
                    a = jnp.where(causal, a, 0.0)
                g = a * da
                cum = dsum[hh] - (erun + _dot_exact(g, u_incl, 1, 0, 2))
                beta = jnp.exp(lb)
                dz = g * (1.0 - beta) - cum * beta
                if diag:
                    dz = jnp.where(causal, dz, 0.0)
                dzb = dz.astype(_BF)
                dqs.append(dq + _dot(dzb, kb, 1, 0))
                dk_h = _dot(dzb, qm[hh], 0, 0)
                dv_h = _dot(a.astype(_BF), dom[hh], 0, 0)
                dk = dk_h if dk is None else dk + dk_h
                dv = dv_h if dv is None else dv + dv_h
                runs.append(run + jnp.sum(l1, axis=-1, keepdims=True))
                eruns.append(erun + jnp.sum(g, axis=-1, keepdims=True))
            dk_ref[pl.ds(off, t), :] += dk
            dv_ref[pl.ds(off, t), :] += dv
            return tuple(runs), tuple(eruns), tuple(dqs)

        zero, zero_q = jnp.zeros((t, 1), F32), jnp.zeros((t, 128), F32)
        ahead = scores(i - 1)
        carry = block(i, ((zero, zero), (zero, zero), (zero_q, zero_q)), scores(i), True)
        carry = _sb_key_loop(i, carry, lambda j, c, a: block(j, c, a, False), scores, ahead)
        dq_ref[...] = jnp.where(masks[0], carry[2][0], carry[2][1]) * scale

    kv_spec = lambda c0: pl.BlockSpec((s, 128), lambda p, i: (0, c0 // 128 + p))
    blk = lambda c0: pl.BlockSpec((t, 128), lambda p, i: (i, c0 // 128 + p))
    acc = pl.BlockSpec((s, 128), lambda p, i: (0, p))
    shp = jax.ShapeDtypeStruct((s, 256), F32)
    c_operands, c_in_specs, c_out_shapes, c_out_specs, c_sems = _comm_operands(comm) if comm else ([], [], [], [], [])
    outs = _pcall(
        _with_comm(body, comm, 5, 3, *_grid_ends(SB_HEADS // 2, nblk)), name=name, grid=(SB_HEADS // 2, nblk),
        out_shape=[shp, shp, shp] + c_out_shapes,
        in_specs=[blk(C_QA), kv_spec(C_KA), kv_spec(C_VA), blk(0), blk(M_SB)] + c_in_specs,
        out_specs=[blk(0), acc, acc] + c_out_specs,
        scratch_shapes=c_sems, input_output_aliases=_comm_aliases(comm, 5, 3), compiler_params=_cp(ARB, ARB),
    )(proj, proj, proj, o, dmix, *c_operands)
    return outs[0], outs[1], outs[2], list(outs[3:])


def _group_matrix(width):
    r = _iota2((width, width), 0) // HEAD_DIM
    c = _iota2((width, width), 1) // HEAD_DIM
    return (r == c).astype(_BF)


def _group_mean(v, gm):
    return _dot_exact(v, gm, 1, 0, 2) * (1.0 / HEAD_DIM)


def _fox_prep(proj, gq, gk, bf, name):
    s = proj.shape[0]
    t = min(ATT_T, s)
    nblk = s // t
    assert nblk <= 128

    def body(q_ref, k_ref, fl_ref, gq_ref, gk_ref, bf_ref, qn_ref, kn_ref, cfc_ref, cfr_ref, cfe_ref, carry_ref):
        i = pl.program_id(0)
        gm = _group_matrix(512)
        for src, gref, dst in ((q_ref, gq_ref, qn_ref), (k_ref, gk_ref, kn_ref)):
            v = src[...]
            rstd = lax.rsqrt(_group_mean(v * v, gm) + EPS)
            dst[...] = (v * rstd * gref[...]).astype(dst.dtype)

        @pl.when(i == 0)
        def _():
            carry_ref[...] = jnp.zeros_like(carry_ref)
            cfe_ref[...] = jnp.full(cfe_ref.shape, NO_BLOCK, F32)

        lane = _iota2((1, 128), 1)
        logf = jnp.where(lane < FOX_HEADS, _softplus_parts(-(fl_ref[...] + bf_ref[...]))[0], 0.0)
        lower = (_iota2((t, t), 0) >= _iota2((t, t), 1)).astype(_BF)
        cf = carry_ref[...] + _dot_exact(logf, lower, 1, 0, 3, v_left=False)
        cfc_ref[...] = cf
        rows = cf.T[:FOX_HEADS, :]
        cfr_ref[...] = rows
        cfe_ref[...] = jnp.where(lane == i, rows[:, t - 1:t], cfe_ref[...])
        carry_ref[...] = cf[t - 1:t, :]

    col = lambda w, c0: pl.BlockSpec((t, w), lambda i: (i, c0 // w))
    v512 = pl.BlockSpec((1, 512), lambda i: (0, 0))
    return _pcall(
        body, name=name, grid=(nblk,),
        out_shape=[jax.ShapeDtypeStruct((s, 512), _BF), jax.ShapeDtypeStruct((s, 512), _BF),
                   jax.ShapeDtypeStruct((s, 128), F32), jax.ShapeDtypeStruct((FOX_HEADS, s), F32),
                   jax.ShapeDtypeStruct((FOX_HEADS, 128), F32)],
        in_specs=[col(512, C_QB), col(512, C_KB), col(128, C_FL), v512, v512, pl.BlockSpec((1, 128), lambda i: (0, 0))],
        out_specs=[col(512, 0), col(512, 0), col(128, 0), pl.BlockSpec((FOX_HEADS, t), lambda i: (0, i)),
                   pl.BlockSpec((FOX_HEADS, 128), lambda i: (0, 0))],
        scratch_shapes=[pltpu.VMEM((1, 128), F32)], compiler_params=_cp(ARB),
    )(proj, proj, proj, gq, gk, bf)


def _fox_cq(cfc_blk, head):
    lane = _iota2((1, 128), 1)
    return jnp.sum(jnp.where(lane == head, cfc_blk, 0.0), axis=-1, keepdims=True)


def _fox_ck(cfr_ref, off, t, head):
    sub = _iota2((FOX_HEADS, 1), 0)
    return jnp.sum(jnp.where(sub == head, cfr_ref[:, pl.ds(off, t)], 0.0), axis=0, keepdims=True)


def _fox_blocks_needed(top, cfe_ref, head, i):
    sub = _iota2((FOX_HEADS, 1), 0)
    lane = _iota2((1, 128), 1)
    last_key = jnp.sum(jnp.where(sub == head, cfe_ref[...], 0.0), axis=0, keepdims=True)
    need = jnp.logical_and(lane < i, top - last_key > UNDERFLOW)
    return jnp.sum(need.astype(jnp.int32))


def _grid_ends(n_outer, n_inner):
    first = lambda: jnp.logical_and(pl.program_id(0) == 0, pl.program_id(1) == 0)
    last = lambda: jnp.logical_and(pl.program_id(0) == n_outer - 1, pl.program_id(1) == n_inner - 1)
    return first, last


def _fox_fwd(proj, qn, kn, cfc, cfr, cfe, qk_bound, name, comm=None):
    s = proj.shape[0]
    t = min(ATT_T, s)
    nblk = s // t
    scale = HEAD_DIM ** -0.5

    def body(q_ref, k_ref, v_ref, cfc_ref, cfr_ref, cfe_ref, qkb_ref, o_ref, lse_ref):
        p = pl.program_id(0)
        i = pl.program_id(1)
        masks = _head_masks()
        lane = _iota2((1, 128), 1)
        valid = _iota2((t, t), 1) <= _iota2((t, t), 0)
        qv = q_ref[...]
        cfc_blk = cfc_ref[...]
        heads = [2 * p + hh for hh in range(2)]
        qm = [(jnp.where(masks[hh], qv, 0) * scale).astype(_BF) for hh in range(2)]
        cq = [_fox_cq(cfc_blk, heads[hh]) for hh in range(2)]

        def scores(j):
            kb = k_ref[pl.ds(pl.multiple_of(jnp.maximum(j, 0) * t, t), t), :]
            return tuple(_dot(qm[hh], kb, 1, 1) for hh in range(2))

        def block(j, carry, qk, diag):
            off = pl.multiple_of(j * t, t)
            vb = v_ref[pl.ds(off, t), :].astype(_BF)
            ms, ls, accs = [], [], []
            for hh in range(2):
                m, l, acc = carry[0][hh], carry[1][hh], carry[2][hh]
                z = qk[hh] + (cq[hh] - _fox_ck(cfr_ref, off, t, heads[hh]))
                if diag:
                    z = jnp.where(valid, z, NEG)
                m_new = jnp.maximum(m, jnp.max(z, axis=-1, keepdims=True))
                pe = jnp.exp(z - m_new)
                alpha = jnp.exp(m - m_new)
                ms.append(m_new)
                ls.append(alpha * l + jnp.sum(pe, axis=-1, keepdims=True))
                accs.append(alpha * acc + _dot_exact(pe, vb, 1, 0, 2))
            return tuple(ms), tuple(ls), tuple(accs)

        def step(jj, c):
            ahead = scores(i - jj - 2)
            return block(i - jj - 1, c[0], c[1], False), ahead

        neg, zero, zero_o = jnp.full((t, 1), NEG, F32), jnp.zeros((t, 1), F32), jnp.zeros((t, 128), F32)
        ahead = scores(i - 1)
        carry = block(i, ((neg, neg), (zero, zero), (zero_o, zero_o)), scores(i), True)
        needed = [_fox_blocks_needed(jnp.max(qkb_ref[...] + cq[hh] - carry[0][hh]), cfe_ref, heads[hh], i)
                  for hh in range(2)]
        m, l, acc = lax.fori_loop(0, jnp.maximum(needed[0], needed[1]), step, (carry, ahead))[0]
        o_ref[...] = jnp.where(masks[0], acc[0] / l[0], acc[1] / l[1])
        lse_ref[...] = jnp.where(lane == 0, m[0] + jnp.log(l[0]), jnp.where(lane == 1, m[1] + jnp.log(l[1]), 0.0))

    c_operands, c_in_specs, c_out_shapes, c_out_specs, c_sems = _comm_operands(comm) if comm else ([], [], [], [], [])
    outs = _pcall(
        _with_comm(body, comm, 7, 2, *_grid_ends(FOX_HEADS // 2, nblk)), name=name, grid=(FOX_HEADS // 2, nblk),
        out_shape=[jax.ShapeDtypeStruct((s, 512), F32), jax.ShapeDtypeStruct((FOX_HEADS // 2, s, 128), F32)] + c_out_shapes,
        in_specs=[pl.BlockSpec((t, 128), lambda p, i: (i, p)), pl.BlockSpec((s, 128), lambda p, i: (0, p)),
                  pl.BlockSpec((s, 128), lambda p, i: (0, C_VB // 128 + p)),
                  pl.BlockSpec((t, 128), lambda p, i: (i, 0)), pl.BlockSpec((FOX_HEADS, s), lambda p, i: (0, 0)),
                  pl.BlockSpec((FOX_HEADS, 128), lambda p, i: (0, 0)), pl.BlockSpec((1, 1), lambda p, i: (0, 0))] + c_in_specs,
        out_specs=[pl.BlockSpec((t, 128), lambda p, i: (i, p)),
                   pl.BlockSpec((None, t, 128), lambda p, i: (p, i, 0))] + c_out_specs,
        scratch_shapes=c_sems, input_output_aliases=_comm_aliases(comm, 7, 2),
        compiler_params=_cp(ARB, ARB),
    )(qn, kn, proj, cfc, cfr, cfe, qk_bound, *c_operands)
    return outs[0], outs[1], list(outs[2:])


def _fox_bwd(proj, qn, kn, cfc, cfr, cfe, qk_bound, o, lse, dmix, name, comm=None):
    s = proj.shape[0]
    t = min(ATT_T, s)
    nblk = s // t
    scale = HEAD_DIM ** -0.5

    def body(q_ref, k_ref, v_ref, cfc_ref, cfr_ref, cfe_ref, qkb_ref, o_ref, lse_ref, do_ref,
             dq_ref, dk_ref, dv_ref, dcr_ref):
        p = pl.program_id(0)
        i = pl.program_id(1)
        masks = _head_masks()
        lane = _iota2((1, 128), 1)
        valid = _iota2((t, t), 1) <= _iota2((t, t), 0)
        before = (_iota2((t, t), 0) < _iota2((t, t), 1)).astype(_BF)

        @pl.when(i == 0)
        def _():
            dk_ref[...] = jnp.zeros_like(dk_ref)
            dv_ref[...] = jnp.zeros_like(dv_ref)

        @pl.when(jnp.logical_and(p == 0, i == 0))
        def _():
            dcr_ref[...] = jnp.zeros_like(dcr_ref)

        head_row = _iota2((FOX_HEADS, 1), 0)
        qv = q_ref[...]
        cfc_blk = cfc_ref[...]
        dov = do_ref[...]
        ov = o_ref[...]
        lse_blk = lse_ref[...]
        heads = [2 * p + hh for hh in range(2)]
        qm = [(jnp.where(masks[hh], qv, 0) * scale).astype(_BF) for hh in range(2)]
        dom = [jnp.where(masks[hh], dov, 0.0).astype(_BF) for hh in range(2)]
        dsum = [jnp.sum(dom[hh].astype(F32) * ov, axis=-1, keepdims=True) for hh in range(2)]
        lse_h = [jnp.sum(jnp.where(lane == hh, lse_blk, 0.0), axis=-1, keepdims=True) for hh in range(2)]
        cq = [_fox_cq(cfc_blk, heads[hh]) for hh in range(2)]
        needed = [_fox_blocks_needed(jnp.max(qkb_ref[...] + cq[hh] - lse_h[hh]), cfe_ref, heads[hh], i)
                  for hh in range(2)]
        first = i - jnp.maximum(needed[0], needed[1])

        def scores(j):
            off = pl.multiple_of(j * t, t)
            kb = k_ref[pl.ds(off, t), :]
            vb = v_ref[pl.ds(off, t), :].astype(_BF)
            return tuple((_dot(qm[hh], kb, 1, 1), _dot(dom[hh], vb, 1, 1)) for hh in range(2))

        def block(j, carry, ahead, diag):
            off = pl.multiple_of(j * t, t)
            kb = k_ref[pl.ds(off, t), :]
            dqs, pres = [], []
            dk = dv = None
            for hh in range(2):
                dq, pre = carry[0][hh], carry[1][hh]
                z = ahead[hh][0] + (cq[hh] - _fox_ck(cfr_ref, off, t, heads[hh]))
                pm = jnp.exp(z - lse_h[hh])
                if diag:
                    pm = jnp.where(valid, pm, 0.0)
                ds = pm * (ahead[hh][1] - dsum[hh])
                dsb = ds.astype(_BF)
                dqs.append(dq + _dot(dsb, kb, 1, 0))
                dk_h = _dot(dsb, qm[hh], 0, 0)
                dv_h = _dot(pm.astype(_BF), dom[hh], 0, 0)
                dk = dk_h if dk is None else dk + dk_h
                dv = dv_h if dv is None else dv + dv_h
                if diag:
                    lower_keys = jnp.where(valid, pre + _dot_exact(ds, before, 1, 0, 2), 0.0)
                    dlogf = jnp.sum(lower_keys, axis=0, keepdims=True)
                else:
                    dlogf = jnp.sum(pre) + _dot_exact(jnp.sum(ds, axis=0, keepdims=True), before, 1, 0, 3)
                dcr_ref[:, pl.ds(off, t)] += jnp.where(head_row == heads[hh], dlogf, 0.0)
                pres.append(pre + jnp.sum(ds, axis=-1, keepdims=True))
            dk_ref[pl.ds(off, t), :] += dk
            dv_ref[pl.ds(off, t), :] += dv
            return tuple(dqs), tuple(pres)

        zero, zero_q = jnp.zeros((t, 1), F32), jnp.zeros((t, 128), F32)


        def step(j, c):
            ahead = scores(j + 1)
            return block(j, c[0], c[1], False), ahead

        carry, ahead = lax.fori_loop(first, i, step, (((zero_q, zero_q), (zero, zero)), scores(first)))
        dqs, _ = block(i, carry, ahead, True)
        dq_ref[...] = jnp.where(masks[0], dqs[0], dqs[1]) * scale

    blk = lambda c0: pl.BlockSpec((t, 128), lambda p, i: (i, c0 // 128 + p))
    res = lambda c0: pl.BlockSpec((s, 128), lambda p, i: (0, c0 // 128 + p))
    shp = jax.ShapeDtypeStruct((s, 512), F32)
    c_operands, c_in_specs, c_out_shapes, c_out_specs, c_sems = _comm_operands(comm) if comm else ([], [], [], [], [])
    outs = _pcall(
        _with_comm(body, comm, 10, 4, *_grid_ends(FOX_HEADS // 2, nblk)), name=name, grid=(FOX_HEADS // 2, nblk),
        out_shape=[shp, shp, shp, jax.ShapeDtypeStruct((FOX_HEADS, s), F32)] + c_out_shapes,
        in_specs=[blk(0), res(0), res(C_VB), pl.BlockSpec((t, 128), lambda p, i: (i, 0)),
                  pl.BlockSpec((FOX_HEADS, s), lambda p, i: (0, 0)), pl.BlockSpec((FOX_HEADS, 128), lambda p, i: (0, 0)),
                  pl.BlockSpec((1, 1), lambda p, i: (0, 0)), blk(0),
                  pl.BlockSpec((None, t, 128), lambda p, i: (p, i, 0)), blk(M_FOX)] + c_in_specs,
        out_specs=[blk(0), res(0), res(0), pl.BlockSpec((FOX_HEADS, s), lambda p, i: (0, 0))] + c_out_specs,
        scratch_shapes=c_sems, input_output_aliases=_comm_aliases(comm, 10, 4),
        compiler_params=_cp(ARB, ARB),
    )(qn, kn, proj, cfc, cfr, cfe, qk_bound, o, lse, dmix, *c_operands)
    return outs[0], outs[1], outs[2], outs[3], list(outs[4:])


def _fox_prep_bwd(proj, gq, gk, bf, dqn, dkn, dlogf, name):
    s = proj.shape[0]
    t = min(ATT_T, s)
    nblk = s // t

    def body(q_ref, k_ref, fl_ref, gq_ref, gk_ref, bf_ref, dqn_ref, dkn_ref, dlogf_ref,
             dq_ref, dk_ref, dfl_ref, dgq_ref, dgk_ref, dbf_ref):
        i = pl.program_id(0)
        gm = _group_matrix(512)

        @pl.when(i == 0)
        def _():
            dgq_ref[...] = jnp.zeros_like(dgq_ref)
            dgk_ref[...] = jnp.zeros_like(dgk_ref)
            dbf_ref[...] = jnp.zeros_like(dbf_ref)

        for src, gref, dyref, dst, dgref in ((q_ref, gq_ref, dqn_ref, dq_ref, dgq_ref),
                                              (k_ref, gk_ref, dkn_ref, dk_ref, dgk_ref)):
            v = src[...]
            dy = dyref[...]
            rstd = lax.rsqrt(_group_mean(v * v, gm) + EPS)
            xhat = v * rstd
            dgref[...] += jnp.sum(dy * xhat, axis=0, keepdims=True)
            dxh = dy * gref[...]
            dst[...] = (rstd * (dxh - xhat * _group_mean(dxh * xhat, gm))).astype(dst.dtype)

        to_lanes = (_iota2((FOX_HEADS, 128), 0) == _iota2((FOX_HEADS, 128), 1)).astype(_BF)
        dlogf = _dot_exact(dlogf_ref[...], to_lanes, 0, 0, 3)
        xv = fl_ref[...] + bf_ref[...]
        e = jnp.exp(-jnp.abs(xv))
        dfl = dlogf * (jnp.where(xv >= 0.0, e, 1.0) / (1.0 + e))
        dfl_ref[...] = dfl.astype(dfl_ref.dtype)
        dbf_ref[...] += jnp.sum(dfl, axis=0, keepdims=True)

    col = lambda w, c0: pl.BlockSpec((t, w), lambda i: (i, c0 // w))
    v512 = pl.BlockSpec((1, 512), lambda i: (0, 0))
    v128 = pl.BlockSpec((1, 128), lambda i: (0, 0))
    return _pcall(
        body, name=name, grid=(nblk,),
        out_shape=[jax.ShapeDtypeStruct((s, 512), _BF), jax.ShapeDtypeStruct((s, 512), _BF),
                   jax.ShapeDtypeStruct((s, 128), _BF), jax.ShapeDtypeStruct((1, 512), F32),
                   jax.ShapeDtypeStruct((1, 512), F32), jax.ShapeDtypeStruct((1, 128), F32)],
        in_specs=[col(512, C_QB), col(512, C_KB), col(128, C_FL), v512, v512, v128, col(512, 0), col(512, 0),
                  pl.BlockSpec((FOX_HEADS, t), lambda i: (0, i))],
        out_specs=[col(512, 0), col(512, 0), col(128, 0), v512, v512, v128], compiler_params=_cp(ARB),
    )(proj, proj, proj, gq, gk, bf, dqn, dkn, dlogf)


GELU_C = 0.7978845608028654
GELU_A = 0.044715


def _gelu(x):
    return 0.5 * x * (1.0 + jnp.tanh(GELU_C * (x + GELU_A * x * x * x)))


def _gelu_grad(x):
    th = jnp.tanh(GELU_C * (x + GELU_A * x * x * x))
    return 0.5 * (1.0 + th) + 0.5 * x * (1.0 - th * th) * (GELU_C * (1.0 + 3.0 * GELU_A * x * x))


def _sgu_tril():
    return _iota2((SGU_CHUNK, SGU_CHUNK), 0) >= _iota2((SGU_CHUNK, SGU_CHUNK), 1)


def _sgu_group_masks():
    lane = _iota2((1, 256), 1)
    return [(lane // HEAD_DIM) == g for g in range(SGU_GROUPS)]


def _sgu_fwd(proj, w, gs, bexp, name):
    s = proj.shape[0]
    t = min(2 * SGU_CHUNK, s)

    def body(u_ref, v_ref, w_ref, gs_ref, b_ref, o_ref):
        gm = _group_matrix(256)
        gmask = _sgu_group_masks()
        tril = _sgu_tril()
        u = _gelu(u_ref[...])
        vg = _gelu(v_ref[...])
        vhat = (vg * lax.rsqrt(_group_mean(vg * vg, gm) + EPS) * gs_ref[...]).astype(_BF)
        for ch in range(t // SGU_CHUNK):
            rows = slice(ch * SGU_CHUNK, (ch + 1) * SGU_CHUNK)
            mixed = b_ref[...]
            for g in range(SGU_GROUPS):
                wg = jnp.where(tril, w_ref[g], 0.0).astype(_BF)
                mixed = jnp.where(gmask[g], mixed + _dot(wg, vhat[rows], 1, 0), mixed)
            o_ref[rows, :] = u[rows] * mixed

    return _pcall(
        body, name=name, grid=(s // t,), out_shape=jax.ShapeDtypeStruct((s, 256), F32),
        in_specs=[_row_spec(t, 256, C_UC // 256), _row_spec(t, 256, C_VC // 256),
                  pl.BlockSpec((SGU_GROUPS, SGU_CHUNK, SGU_CHUNK), lambda i: (0, 0, 0)), _vec_spec(256),
                  pl.BlockSpec((SGU_CHUNK, 256), lambda i: (0, 0))],
        out_specs=_row_spec(t, 256), compiler_params=_cp(PAR),
    )(proj, proj, w, gs, bexp)


def _sgu_bwd(proj, w, gs, bexp, dmix, name):
    s = proj.shape[0]
    t = min(2 * SGU_CHUNK, s)
    nstep = s // t

    def body(u_ref, v_ref, w_ref, gs_ref, b_ref, do_ref, du_ref, dv_ref, dw_ref, db_ref, dgs_ref):
        i = pl.program_id(0)
        gm = _group_matrix(256)
        gmask = _sgu_group_masks()
        tril = _sgu_tril()

        @pl.when(i == 0)
        def _():
            dw_ref[...] = jnp.zeros_like(dw_ref)
            db_ref[...] = jnp.zeros_like(db_ref)
            dgs_ref[...] = jnp.zeros_like(dgs_ref)

        uc = u_ref[...]
        vc = v_ref[...]
        u = _gelu(uc)
        vg = _gelu(vc)
        rstd = lax.rsqrt(_group_mean(vg * vg, gm) + EPS)
        xh = vg * rstd
        gsv = gs_ref[...]
        vhat = (xh * gsv).astype(_BF)
        dov = do_ref[...]
        dm = dov * u
        for ch in range(t // SGU_CHUNK):
            rows = slice(ch * SGU_CHUNK, (ch + 1) * SGU_CHUNK)
            mixed = b_ref[...]
            dvh = jnp.zeros((SGU_CHUNK, 256), F32)
            dmc = dm[rows]
            for g in range(SGU_GROUPS):
                wg = jnp.where(tril, w_ref[g], 0.0).astype(_BF)
                mixed = jnp.where(gmask[g], mixed + _dot(wg, vhat[rows], 1, 0), mixed)
                dvh = jnp.where(gmask[g], _dot(wg, dmc.astype(_BF), 0, 0), dvh)
                dw_ref[g] += _dot(jnp.where(gmask[g], dmc, 0.0).astype(_BF), vhat[rows], 1, 1)
            db_ref[...] += dmc
            du_ref[rows, :] = (dov[rows] * mixed * _gelu_grad(uc[rows])).astype(du_ref.dtype)
            xhc = xh[rows]
            dgs_ref[...] += jnp.sum(dvh * xhc, axis=0, keepdims=True)
            dxh = dvh * gsv
            dvg = rstd[rows] * (dxh - xhc * _group_mean(dxh * xhc, gm))
            dv_ref[rows, :] = (dvg * _gelu_grad(vc[rows])).astype(dv_ref.dtype)

        @pl.when(i == nstep - 1)
        def _():
            for g in range(SGU_GROUPS):
                dw_ref[g] = jnp.where(tril, dw_ref[g], 0.0)

    wspec = pl.BlockSpec((SGU_GROUPS, SGU_CHUNK, SGU_CHUNK), lambda i: (0, 0, 0))
    bspec = pl.BlockSpec((SGU_CHUNK, 256), lambda i: (0, 0))
    return _pcall(
        body, name=name, grid=(nstep,),
        out_shape=[jax.ShapeDtypeStruct((s, 256), _BF), jax.ShapeDtypeStruct((s, 256), _BF),
                   jax.ShapeDtypeStruct((SGU_GROUPS, SGU_CHUNK, SGU_CHUNK), F32),
                   jax.ShapeDtypeStruct((SGU_CHUNK, 256), F32), jax.ShapeDtypeStruct((1, 256), F32)],
        in_specs=[_row_spec(t, 256, C_UC // 256), _row_spec(t, 256, C_VC // 256), wspec, _vec_spec(256), bspec,
                  _row_spec(t, 256, M_SGU // 256)],
        out_specs=[_row_spec(t, 256), _row_spec(t, 256), wspec, bspec, _vec_spec(256)],
        compiler_params=_cp(ARB),
    )(proj, proj, w, gs, bexp, dmix)


def _ada_fwd(c_all, ada_w, name):
    depth, d, n = ada_w.shape

    def body(c_ref, w_ref, o_ref):
        cv = c_ref[...]
        cond = cv / (1.0 + jnp.exp(-cv))
        o_ref[...] = _dot_f32(cond, w_ref[...], 1, 0)

    return _pcall(
        body, name=name, grid=(depth,), out_shape=jax.ShapeDtypeStruct((depth, N_DEV, n), F32),
        in_specs=[pl.BlockSpec((N_DEV, d), lambda l: (0, 0)), pl.BlockSpec((None, d, n), lambda l: (l, 0, 0))],
        out_specs=pl.BlockSpec((None, N_DEV, n), lambda l: (l, 0, 0)), compiler_params=_cp(PAR),
    )(c_all, ada_w)


def _ada_bwd(c_all, dmod, name):
    depth, _, n = dmod.shape
    d = c_all.shape[1]

    def body(c_ref, dm_ref, o_ref):
        cv = c_ref[...]
        cond = cv / (1.0 + jnp.exp(-cv))
        o_ref[...] = _dot_f32(cond, dm_ref[...], 0, 0)

    return _pcall(
        body, name=name, grid=(depth,), out_shape=jax.ShapeDtypeStruct((depth, d, n), F32),
        in_specs=[pl.BlockSpec((N_DEV, d), lambda l: (0, 0)), pl.BlockSpec((None, N_DEV, n), lambda l: (l, 0, 0))],
        out_specs=pl.BlockSpec((None, d, n), lambda l: (l, 0, 0)), compiler_params=_cp(PAR),
    )(c_all, dmod)


def _adamw(slots, w, m, v, name):
    n, r, c = slots.shape
    tr = 256 if r % 256 == 0 else r
    bc1 = 1.0 - ADAM_B1 ** ADAM_STEP
    bc2 = 1.0 - ADAM_B2 ** ADAM_STEP

    def body(s_ref, w_ref, m_ref, v_ref, g_ref, d_ref, nm_ref, nv_ref):
        g = s_ref[0].astype(F32)
        for j in range(1, n):
            g = g + s_ref[j].astype(F32)
        m_new = ADAM_B1 * m_ref[...] + (1.0 - ADAM_B1) * g
        v_new = ADAM_B2 * v_ref[...] + (1.0 - ADAM_B2) * (g * g)
        g_ref[...] = g
        nm_ref[...] = m_new
        nv_ref[...] = v_new
        d_ref[...] = -ADAM_LR * ((m_new / bc1) / (jnp.sqrt(v_new / bc2) + ADAM_EPS) + ADAM_WD * w_ref[...])

    tile = pl.BlockSpec((tr, c), lambda i: (i, 0))
    shp = jax.ShapeDtypeStruct((r, c), F32)
    return _pcall(
        body, name=name, grid=(r // tr,), out_shape=[shp] * 4,
        in_specs=[pl.BlockSpec((n, tr, c), lambda i: (0, i, 0)), tile, tile, tile],
        out_specs=[tile] * 4, compiler_params=_cp(PAR),
    )(slots, w, m, v)


def _w_in_layout(w):
    pad = jnp.zeros(w.shape[:-1] + (IN_P - IN_W,), w.dtype)
    return jnp.concatenate([w[..., 768:FL_SRC], w[..., :768], w[..., UC_SRC:], w[..., FL_SRC:UC_SRC], pad], axis=-1)


def _w_in_unlayout(g):
    return jnp.concatenate([g[..., C_QA:C_UC], g[..., :C_QA], g[..., C_FL:C_FL + FOX_HEADS], g[..., C_UC:C_FL]], axis=-1)


SMALL = [("ada_b", DEPTH * 6 * D_MODEL), ("norm1_g", DEPTH * D_MODEL), ("norm2_g", DEPTH * D_MODEL),
         ("sgu_w", DEPTH * SGU_GROUPS * SGU_CHUNK * SGU_CHUNK), ("sgu_b", DEPTH * SGU_GROUPS * SGU_CHUNK),
         ("sgu_norm_g", DEPTH * SGU_GROUPS * HEAD_DIM), ("q_norm_g", DEPTH * HEAD_DIM), ("k_norm_g", DEPTH * HEAD_DIM),
         ("b_forget", DEPTH * FOX_HEADS), ("loss", 1)]
SMALL_ROWS = 2560


def _pack_small(parts):
    flat = jnp.concatenate([parts[name].reshape(-1).astype(F32) for name, _ in SMALL])
    return jnp.pad(flat, (0, SMALL_ROWS * 128 - flat.shape[0])).reshape(SMALL_ROWS, 128)


def _unpack_small(packed, shapes):
    flat = packed.reshape(-1)
    out, off = {}, 0
    for name, size in SMALL:
        out[name] = flat[off:off + size].reshape(shapes[name])
        off += size
    return out


def kernel(x, c, ada_w, ada_b, norm1_g, norm2_g, w_in, b_forget, q_norm_g, k_norm_g, sgu_norm_g, sgu_w, sgu_b, w_out, mlp_w1, mlp_w2, loss_target, m_ada_w, m_ada_b, m_norm1_g, m_norm2_g, m_w_in, m_b_forget, m_q_norm_g, m_k_norm_g, m_sgu_norm_g, m_sgu_w, m_sgu_b, m_w_out, m_mlp_w1, m_mlp_w2, v_ada_w, v_ada_b, v_norm1_g, v_norm2_g, v_w_in, v_b_forget, v_q_norm_g, v_k_norm_g, v_sgu_norm_g, v_sgu_w, v_sgu_b, v_w_out, v_mlp_w1, v_mlp_w2):
    me = _lin(_my_pos())
    x0 = x[0]
    target = loss_target[0]
    n_ada = ada_w.shape[2]

    shards = [w.astype(_XBF) for w in (w_in, w_out, mlp_w1, mlp_w2)]

    def whole_in(w_in_g):
        return _w_in_layout(jnp.transpose(w_in_g, (1, 0, 2)).reshape(D_MODEL, IN_W))

    def whole_rest(w_out_g, w1_g, w2_g):
        return (w_out_g.reshape(D_MODEL, D_MODEL), jnp.transpose(w1_g, (1, 0, 2)).reshape(D_MODEL, D_FF),
                w2_g.reshape(D_FF, D_MODEL))

    w_in_first, c_all = _exchange([shards[0][0], c], False, "gather_weights")
    c_all = c_all.reshape(N_DEV, D_MODEL)
    weights = [None] * DEPTH

    mod_part = _ada_fwd(c_all, ada_w, "ada_fwd")
    (mod_rows,) = _exchange([jnp.transpose(mod_part, (1, 0, 2))], True, "scatter_mod")
    mod = jnp.transpose(mod_rows, (1, 0, 2)).reshape(DEPTH, 6 * D_MODEL) + ada_b
    mods = mod.reshape(DEPTH, 6, 1, D_MODEL)

    saved = []
    xl = x0
    for l in range(DEPTH):
        sh1, sc1, g1, sh2, sc2, g2 = (mods[l, r] for r in range(6))
        n1 = norm1_g[l].reshape(1, D_MODEL)
        n2 = norm2_g[l].reshape(1, D_MODEL)
        gq = jnp.tile(q_norm_g[l], FOX_HEADS).reshape(1, 512)
        gk = jnp.tile(k_norm_g[l], FOX_HEADS).reshape(1, 512)
        bf = jnp.pad(b_forget[l], (0, 128 - FOX_HEADS)).reshape(1, 128)
        gs = sgu_norm_g[l].reshape(1, 256)
        bexp = jnp.repeat(sgu_b[l].T, HEAD_DIM, axis=1)

        if l == 0:
            w_in_l = whole_in(w_in_first)
            gather_rest = dict(arrays=[w[0] for w in shards[1:]], scatter=False, recv=None, slot=None)
        else:
            w_in_l, gather_rest = weights[l][0], None
        h1, h1_t = _norm_mod(xl, n1, sc1, sh1, "norm_mod")
        proj = _mm_nn(h1, w_in_l, "plain", (), "mm_in")
        o_sb, gathered = _sb_fwd(proj, "sb_fwd", gather_rest)
        if gathered:
            weights[0] = (w_in_l,) + whole_rest(*gathered)
        qn, kn, cfc, cfr, cfe = _fox_prep(proj, gq, gk, bf, "fox_prep")
        qkb = (1.02 * HEAD_DIM ** 0.5 * jnp.max(jnp.abs(q_norm_g[l])) * jnp.max(jnp.abs(k_norm_g[l]))).reshape(1, 1)
        gather_next = None
        if l + 1 < DEPTH:
            gather_next = dict(arrays=[w[l + 1] for w in shards], scatter=False, recv=None, slot=None)
        o_fox, lse, gathered = _fox_fwd(proj, qn, kn, cfc, cfr, cfe, qkb, "fox_fwd", gather_next)
        if gathered:
            weights[l + 1] = (whole_in(gathered[0]),) + whole_rest(*gathered[1:])
        _, w_out_l, w1_l, w2_l = weights[l]
        o_sgu = _sgu_fwd(proj, sgu_w[l], gs, bexp, "sgu_fwd")
        x_mid, y1, mixed_t = _mm_out((o_sb, o_fox, o_sgu), w_out_l, xl, g1, "mm_out")
        h2, h2_t = _norm_mod(x_mid, n2, sc2, sh2, "norm_mod")
        r2, r1, r2_t = _mm_nn(h2, w1_l, "relu2", (), "mm_w1")
        x_out, y2 = _mm_nn(r2, w2_l, "resid", (x_mid, g2), "mm_w2")
        saved.append(dict(x_in=xl, h1_t=h1_t, proj=proj, o_sb=o_sb, qn=qn, kn=kn, cfc=cfc, cfr=cfr, cfe=cfe, o_fox=o_fox,
                          lse=lse, mixed_t=mixed_t, x_mid=x_mid, y1=y1, h2_t=h2_t, r2_t=r2_t, r1=r1, y2=y2,
                          n1=n1, n2=n2, gq=gq, gk=gk, bf=bf, gs=gs, bexp=bexp, qkb=qkb))
        xl = x_out

    loss_part, dx, dy2, dg2 = _loss_head(xl, target, (saved[-1]["y2"], mods[DEPTH - 1, 5]), "loss_head")

    grads_ready = None
    received = [jnp.zeros((N_DEV,) + w.shape, _XBF) for w in shards]
    small = {k: [None] * DEPTH for k in ("mod", "norm1_g", "norm2_g", "sgu_w", "sgu_b", "sgu_norm_g",
                                         "q_norm_g", "k_norm_g", "b_forget")}
    for l in reversed(range(DEPTH)):
        sv = saved[l]
        sh1, sc1, g1, sh2, sc2, g2 = (mods[l, r] for r in range(6))
        w_in_l, w_out_l, w1_l, w2_l = weights[l]
        da = _mm_nt(dy2, w2_l, "mul2", (sv["r1"],), _BF, "mm_w2_bwd")
        g_w2 = _mm_grad(sv["r2_t"], dy2, "rows", "mm_w2_grad")
        dh2 = _mm_nt(da, w1_l, "plain", (), F32, "mm_w1_bwd")
        g_w1 = _mm_grad(sv["h2_t"], da, "cols", "mm_w1_grad")
        dx_mid, dsh2, dsc2, dn2, dy1, dg1 = _norm_mod_bwd(sv["x_mid"], dh2, dx, sv["n2"], sc2, (sv["y1"], g1),
                                                          "norm_mod_bwd")
        dmix = _mm_nt(dy1, w_out_l, "plain", (), F32, "mm_out_bwd")
        g_out = _mm_grad(sv["mixed_t"], dy1, "rows", "mm_out_grad")
        send_own = None
        if l == 0:
            send_own = dict(arrays=[g_out, g_w1, g_w2], scatter=True, recv=received[1:], slot=0)
        dq_sb, dk_sb, dv_sb, filled = _sb_bwd(sv["proj"], sv["o_sb"], dmix, "sb_bwd", send_own)
        if filled:
            received = received[:1] + filled
        send_prev = None
        if grads_ready is not None:
            send_prev = dict(arrays=grads_ready, scatter=True, recv=received, slot=l + 1)
        dqn, dkn, dv_fox, dlogf, filled = _fox_bwd(sv["proj"], sv["qn"], sv["kn"], sv["cfc"], sv["cfr"], sv["cfe"], sv["qkb"],
                                                 sv["o_fox"], sv["lse"], dmix, "fox_bwd", send_prev)
        if filled:
            received = filled
        dq_fox, dk_fox, dfl, dgq, dgk, dbf = _fox_prep_bwd(sv["proj"], sv["gq"], sv["gk"], sv["bf"], dqn, dkn, dlogf,
                                                           "fox_prep_bwd")
        duc, dvc, dsw, dsb, dsg = _sgu_bwd(sv["proj"], sgu_w[l], sv["gs"], sv["bexp"], dmix, "sgu_bwd")
        dproj = jnp.concatenate(
            [dq_fox, dk_fox, dv_fox.astype(_BF), dq_sb.astype(_BF), dk_sb.astype(_BF), dv_sb.astype(_BF), duc, dvc, dfl,
             jnp.zeros((dfl.shape[0], IN_P - C_FL - 128), _BF)], axis=-1)
        dh1 = _mm_nt(dproj, w_in_l, "plain", (), F32, "mm_in_bwd")
        g_in = _w_in_unlayout(_mm_grad(sv["h1_t"], dproj, None, "mm_in_grad"))
        g_in = jnp.transpose(g_in.reshape(D_MODEL, N_DEV, IN_W // N_DEV), (1, 0, 2))
        grads_ready = [g_in, g_out, g_w1, g_w2]
        small["mod"][l] = [dg1, dsh2, dsc2, dg2]
        below = (saved[l - 1]["y2"], mods[l - 1, 5]) if l > 0 else None
        dx, dsh1, dsc1, dn1, *below_grads = _norm_mod_bwd(sv["x_in"], dh1, dx_mid, sv["n1"], sc1, below, "norm_mod_bwd")
        if below_grads:
            dy2, dg2 = below_grads
        small["mod"][l] = jnp.concatenate([dsh1, dsc1] + small["mod"][l], axis=-1).reshape(-1)
        small["norm1_g"][l] = dn1.reshape(-1)
        small["norm2_g"][l] = dn2.reshape(-1)
        small["sgu_w"][l] = dsw
        small["sgu_b"][l] = dsb.reshape(SGU_CHUNK, SGU_GROUPS, HEAD_DIM).sum(-1).T
        small["sgu_norm_g"][l] = dsg.reshape(SGU_GROUPS, HEAD_DIM)
        small["q_norm_g"][l] = dgq.reshape(FOX_HEADS, HEAD_DIM).sum(0)
        small["k_norm_g"][l] = dgk.reshape(FOX_HEADS, HEAD_DIM).sum(0)
        small["b_forget"][l] = dbf[0, :FOX_HEADS]

    parts = {k: jnp.stack(v) for k, v in small.items()}
    parts["ada_b"] = parts.pop("mod")
    parts["loss"] = loss_part
    (small_all,) = _exchange([_pack_small(parts)], False, "gather_small")
    zero1 = jnp.zeros((1,), F32)
    packs = [_pack_small(dict(ada_b=a, norm1_g=b, norm2_g=cc, sgu_w=d, sgu_b=e, sgu_norm_g=f, q_norm_g=g, k_norm_g=h,
                              b_forget=i, loss=zero1))
             for a, b, cc, d, e, f, g, h, i in (
                 (ada_b, norm1_g, norm2_g, sgu_w, sgu_b, sgu_norm_g, q_norm_g, k_norm_g, b_forget),
                 (m_ada_b, m_norm1_g, m_norm2_g, m_sgu_w, m_sgu_b, m_sgu_norm_g, m_q_norm_g, m_k_norm_g, m_b_forget),
                 (v_ada_b, v_norm1_g, v_norm2_g, v_sgu_w, v_sgu_b, v_sgu_norm_g, v_q_norm_g, v_k_norm_g, v_b_forget))]
    small_shapes = dict(ada_b=ada_b.shape, norm1_g=norm1_g.shape, norm2_g=norm2_g.shape, sgu_w=sgu_w.shape,
                        sgu_b=sgu_b.shape, sgu_norm_g=sgu_norm_g.shape, q_norm_g=q_norm_g.shape,
                        k_norm_g=k_norm_g.shape, b_forget=b_forget.shape, loss=())
    s_out = [_unpack_small(a, small_shapes) for a in _adamw(small_all, *packs, "adamw_small")]

    dmod_all = small_all[:, :DEPTH * 6 * D_MODEL // 128, :].reshape(N_DEV, DEPTH, 6 * D_MODEL)
    dmod_mine = lax.dynamic_slice_in_dim(dmod_all, me * n_ada, n_ada, axis=2)
    g_ada = _ada_bwd(c_all, jnp.transpose(dmod_mine, (1, 0, 2)), "ada_bwd")
    ada_out = _adamw(g_ada.reshape(1, DEPTH * D_MODEL, n_ada), ada_w.reshape(-1, n_ada), m_ada_w.reshape(-1, n_ada),
                     v_ada_w.reshape(-1, n_ada), "adamw_ada")
    ada_out = [a.reshape(ada_w.shape) for a in ada_out]

    (ri,) = _exchange(grads_ready[:1], True, "scatter_grads", recv=received[:1], slot=0)
    _, ro, r1, r2 = received

    def big(slots, w, m, v, name):
        cdim = w.shape[-1]
        outs = _adamw(slots.reshape(N_DEV, -1, cdim), w.reshape(-1, cdim), m.reshape(-1, cdim), v.reshape(-1, cdim), name)
        return [a.reshape(w.shape) for a in outs]

    in_out = big(ri, w_in, m_w_in, v_w_in, "adamw_in")
    out_out = big(ro, w_out, m_w_out, v_w_out, "adamw_out")
    w1_out = big(r1, mlp_w1, m_mlp_w1, v_mlp_w1, "adamw_w1")
    w2_out = big(r2, mlp_w2, m_mlp_w2, v_mlp_w2, "adamw_w2")

    def leaves(idx):
        sm = s_out[idx]
        return [ada_out[idx], sm["ada_b"], sm["norm1_g"], sm["norm2_g"], in_out[idx], sm["b_forget"], sm["q_norm_g"],
                sm["k_norm_g"], sm["sgu_norm_g"], sm["sgu_w"], sm["sgu_b"], out_out[idx], w1_out[idx], w2_out[idx]]

    loss = s_out[0]["loss"]
    grad_x = dx.reshape(x.shape)
    return (loss, grad_x, *leaves(0), *leaves(1), *leaves(2), *leaves(3))
```

```python
import jax
import jax.numpy as jnp
from jax import lax
from jax.experimental import pallas as pl
from jax.experimental.pallas import tpu as pltpu

F32 = jnp.float32
_BF = jnp.bfloat16
_XBF = jnp.bfloat16

N_DEV = 8
D_MODEL = 1024
DEPTH = 4
HEAD_DIM = 64
SB_HEADS = 4
FOX_HEADS = 8
SGU_GROUPS = 4
SGU_CHUNK = 128
D_FF = 4096
EPS = 1e-6
IN_W = 2824
FL_SRC = 2304
UC_SRC = 2312

C_QB, C_KB, C_VB = 0, 512, 1024
C_QA, C_KA, C_VA = 1536, 1792, 2048
C_UC, C_VC = 2304, 2560
C_FL = 2816
IN_P = 3072
M_SB, M_FOX, M_SGU = 0, 256, 768

ADAM_LR = 0.001
ADAM_B1 = 0.9
ADAM_B2 = 0.999
ADAM_EPS = 1e-08
ADAM_WD = 0.01
ADAM_STEP = 10

ATT_T = 256
ROW_T = 512
MM_T = 1024
GRAD_TS = 2048
ROW_CHUNK = 64
NEG = -1e30
UNDERFLOW = -90.0
NO_BLOCK = 1e30

MESH = pl.DeviceIdType.MESH
ARB = pltpu.ARBITRARY
PAR = pltpu.PARALLEL
HIGHEST = lax.Precision.HIGHEST


def _pcall(body, **kw):
    return pl.pallas_call(body, **kw)


def _cp(*sem):
    return pltpu.CompilerParams(dimension_semantics=tuple(sem))


def _dot(a, b, ca, cb):
    return lax.dot_general(a, b, (((ca,), (cb,)), ((), ())), preferred_element_type=F32)


def _dot_f32(a, b, ca, cb):
    return lax.dot_general(a, b, (((ca,), (cb,)), ((), ())), precision=HIGHEST, preferred_element_type=F32)


def _split(v, parts):
    out = []
    r = v
    for _ in range(parts - 1):
        p = r.astype(_BF)
        out.append(p)
        r = r - p.astype(F32)
    out.append(r.astype(_BF))
    return out


def _dot_exact(v, m, ca, cb, parts, v_left=True):
    out = None
    for p in _split(v, parts):
        term = _dot(p, m, ca, cb) if v_left else _dot(m, p, ca, cb)
        out = term if out is None else out + term
    return out


def _iota2(shape, dim):
    return lax.broadcasted_iota(jnp.int32, shape, dim)


def _my_pos():
    return lax.axis_index("x"), lax.axis_index("y"), lax.axis_index("c")


def _flip(pos, k):
    x, y, c = pos
    px = 1 - x if (k >> 2) & 1 else x
    py = 1 - y if (k >> 1) & 1 else y
    pc = 1 - c if k & 1 else c
    return px, py, pc


def _lin(pos):
    return 4 * pos[0] + 2 * pos[1] + pos[2]


def _exchange_ops(ins, outs, sems, scatter, slot):
    send_sems, recv_sems, local_sems = sems
    n = len(ins)

    def copies(with_recvs):
        me = _my_pos()
        me_i = _lin(me)
        dst = lambda a, j: outs[a].at[j] if slot is None else outs[a].at[j, slot]
        mine = lambda a: ins[a].at[me_i] if scatter else ins[a]
        local = [pltpu.make_async_copy(mine(a), dst(a, me_i), local_sems.at[a]) for a in range(n)]
        sends, recvs = [], []
        for a in range(n):
            for k in range(1, N_DEV):
                peer = _flip(me, k)
                pair = dict(send_sem=send_sems.at[a, k - 1], recv_sem=recv_sems.at[a, k - 1],
                            device_id=peer, device_id_type=MESH)
                src = ins[a].at[_lin(peer)] if scatter else ins[a]
                sends.append(pltpu.make_async_remote_copy(src_ref=src, dst_ref=dst(a, me_i), **pair))
                if with_recvs:
                    recvs.append(pltpu.make_async_remote_copy(src_ref=mine(a), dst_ref=dst(a, _lin(peer)), **pair))
        return local, sends, recvs

    def start():
        local, sends, _ = copies(False)
        for cp in local + sends:
            cp.start()

    def wait():
        local, sends, recvs = copies(True)
        for snd, rcv in zip(sends, recvs):
            snd.wait_send()
            rcv.wait_recv()
        for cp in local:
            cp.wait()

    return start, wait


def _comm_operands(comm):
    arrays, recv = comm["arrays"], comm["recv"]
    n = len(arrays)
    any_spec = pl.BlockSpec(memory_space=pl.ANY)
    if recv is not None:
        out_shapes = [jax.ShapeDtypeStruct(r.shape, r.dtype) for r in recv]
    elif comm["scatter"]:
        out_shapes = [jax.ShapeDtypeStruct(a.shape, a.dtype) for a in arrays]
    else:
        out_shapes = [jax.ShapeDtypeStruct((N_DEV,) + a.shape, a.dtype) for a in arrays]
    operands = list(arrays) + (list(recv) if recv is not None else [])
    sems = [pltpu.SemaphoreType.DMA((n, N_DEV - 1)), pltpu.SemaphoreType.DMA((n, N_DEV - 1)),
            pltpu.SemaphoreType.DMA((n,))]
    return operands, [any_spec] * len(operands), out_shapes, [any_spec] * n, sems


def _with_comm(body, comm, n_in, n_out, first_step, last_step):
    if comm is None:
        return body
    n = len(comm["arrays"])
    n_cin = n if comm["recv"] is None else 2 * n

    def wrapped(*refs):
        ins, cins = refs[:n_in], refs[n_in:n_in + n]
        o0 = n_in + n_cin
        outs, couts, sems = refs[o0:o0 + n_out], refs[o0 + n_out:o0 + n_out + n], refs[o0 + n_out + n:]
        start, wait = _exchange_ops(cins, couts, sems, comm["scatter"], comm["slot"])
        pl.when(first_step())(start)
        body(*ins, *outs)
        pl.when(last_step())(wait)

    return wrapped


def _comm_aliases(comm, n_in, n_out):
    if comm is None or comm["recv"] is None:
        return {}
    n = len(comm["arrays"])
    return {n_in + n + a: n_out + a for a in range(n)}


def _exchange(arrays, scatter, name, recv=None, slot=None):
    comm = dict(arrays=arrays, scatter=scatter, recv=recv, slot=slot)
    operands, in_specs, out_shapes, out_specs, sems = _comm_operands(comm)
    n = len(arrays)

    def body(*refs):
        n_cin = len(operands)
        start, wait = _exchange_ops(refs[:n], refs[n_cin:n_cin + n], refs[n_cin + n:], scatter, slot)
        start()
        wait()

    outs = _pcall(body, name=name, out_shape=out_shapes, in_specs=in_specs, out_specs=out_specs, scratch_shapes=sems,
                  input_output_aliases=_comm_aliases(comm, 0, 0))(*operands)
    return list(outs)


def _row_spec(t, w, col=0):
    return pl.BlockSpec((t, w), lambda i: (i, col))


def _vec_spec(w, col=0):
    return pl.BlockSpec((1, w), lambda i: (0, col))


def _norm_mod(x, g, sc, sh, name):
    s, d = x.shape
    t = min(ROW_T, s)

    def body(x_ref, g_ref, sc_ref, sh_ref, h_ref, ht_ref):
        xf = x_ref[...]
        rstd = lax.rsqrt(jnp.mean(xf * xf, axis=-1, keepdims=True) + EPS)
        y = xf * rstd * g_ref[...]
        h = y * (1.0 + sc_ref[...]) + sh_ref[...]
        h_ref[...] = h.astype(h_ref.dtype)
        ht_ref[...] = h.T.astype(ht_ref.dtype)

    return _pcall(
        body, name=name, grid=(s // t,),
        out_shape=[jax.ShapeDtypeStruct((s, d), _BF), jax.ShapeDtypeStruct((d, s), _BF)],
        in_specs=[_row_spec(t, d), _vec_spec(d), _vec_spec(d), _vec_spec(d)],
        out_specs=[_row_spec(t, d), pl.BlockSpec((d, t), lambda i: (0, i))], compiler_params=_cp(PAR),
    )(x, g, sc, sh)


def _res_bwd_step(i, dx, y_ref, gate_ref, dy_ref, dgate_ref):
    dy_ref[...] = (dx * gate_ref[...]).astype(dy_ref.dtype)

    @pl.when(i == 0)
    def _():
        dgate_ref[...] = jnp.zeros_like(dgate_ref)

    dgate_ref[...] += jnp.sum(dx * y_ref[...], axis=0, keepdims=True)


def _norm_mod_bwd(x, dh, dres, g, sc, branch, name):
    s, d = x.shape
    t = min(ROW_T, s)
    n_branch = 0 if branch is None else 2

    ch = min(ROW_CHUNK, t)

    def body(x_ref, dh_ref, dres_ref, g_ref, sc_ref, *rest):
        dx_ref, dsh_ref, dsc_ref, dg_ref = rest[n_branch:n_branch + 4]
        i = pl.program_id(0)
        gv = g_ref[...]
        sc1 = 1.0 + sc_ref[...]
        gain = gv * sc1
        n_sum = 2 if branch is None else 3

        def rows(r, sums):
            sl = pl.ds(pl.multiple_of(r * ch, ch), ch)
            xf = x_ref[sl, :]
            dh = dh_ref[sl, :]
            rstd = lax.rsqrt(jnp.mean(xf * xf, axis=-1, keepdims=True) + EPS)
            xhat = xf * rstd
            dh_xhat = dh * xhat
            dxh = dh * gain
            dx = dres_ref[sl, :] + rstd * (dxh - xhat * jnp.mean(dh_xhat * gain, axis=-1, keepdims=True))
            dx_ref[sl, :] = dx
            new = [sums[0] + jnp.sum(dh, axis=0, keepdims=True), sums[1] + jnp.sum(dh_xhat, axis=0, keepdims=True)]
            if branch is not None:
                rest[n_branch + 4][sl, :] = (dx * rest[1][...]).astype(_BF)
                new.append(sums[2] + jnp.sum(dx * rest[0][sl, :], axis=0, keepdims=True))
            return tuple(new)

        sums = lax.fori_loop(0, t // ch, rows, tuple(jnp.zeros((1, d), F32) for _ in range(n_sum)))
        parts = [sums[0], sums[1] * gv, sums[1] * sc1] + list(sums[2:])
        totals = [dsh_ref, dsc_ref, dg_ref] + ([rest[n_branch + 5]] if branch is not None else [])

        @pl.when(i == 0)
        def _():
            for ref in totals:
                ref[...] = jnp.zeros_like(ref)

        for ref, part in zip(totals, parts):
            ref[...] += part

    vec = jax.ShapeDtypeStruct((1, d), F32)
    out_shape = [jax.ShapeDtypeStruct((s, d), F32), vec, vec, vec]
    in_specs = [_row_spec(t, d), _row_spec(t, d), _row_spec(t, d), _vec_spec(d), _vec_spec(d)]
    out_specs = [_row_spec(t, d), _vec_spec(d), _vec_spec(d), _vec_spec(d)]
    if branch is not None:
        in_specs += [_row_spec(t, d), _vec_spec(d)]
        out_shape += [jax.ShapeDtypeStruct((s, d), _BF), vec]
        out_specs += [_row_spec(t, d), _vec_spec(d)]
    return _pcall(
        body, name=name, grid=(s // t,), out_shape=out_shape, in_specs=in_specs, out_specs=out_specs,
        compiler_params=_cp(ARB),
    )(x, dh, dres, g, sc, *(branch or ()))


def _loss_head(y, target, branch, name):
    s, d = y.shape
    t = min(ROW_T, s)

    def body(y_ref, t_ref, yb_ref, gate_ref, loss_ref, dy_ref, dyb_ref, dgate_ref):
        i = pl.program_id(0)
        diff = y_ref[...] - t_ref[...]
        dy = diff * (1.0 / d)
        dy_ref[...] = dy

        @pl.when(i == 0)
        def _():
            loss_ref[...] = jnp.zeros_like(loss_ref)

        rows = jnp.sum(diff * diff, axis=-1, keepdims=True)
        loss_ref[...] += (0.5 / d) * jnp.sum(rows, axis=0, keepdims=True)
        _res_bwd_step(i, dy, yb_ref, gate_ref, dyb_ref, dgate_ref)

    return _pcall(
        body, name=name, grid=(s // t,),
        out_shape=[jax.ShapeDtypeStruct((1, 1), F32), jax.ShapeDtypeStruct((s, d), F32),
                   jax.ShapeDtypeStruct((s, d), _BF), jax.ShapeDtypeStruct((1, d), F32)],
        in_specs=[_row_spec(t, d), _row_spec(t, d), _row_spec(t, d), _vec_spec(d)],
        out_specs=[pl.BlockSpec((1, 1), lambda i: (0, 0)), _row_spec(t, d), _row_spec(t, d), _vec_spec(d)],
        compiler_params=_cp(ARB),
    )(y, target, *branch)


def _mm_nn(a, b, epi, extras, name):
    m, kdim = a.shape
    n = b.shape[1]
    tm, tn, tk = min(MM_T, m), min(MM_T, n), min(MM_T, kdim)
    nk = kdim // tk
    n_extra = len(extras)
    n_out = {"plain": 1, "resid": 2, "relu2": 3}[epi]

    def finish(y, extra_refs, out_refs):
        if epi == "plain":
            out_refs[0][...] = y.astype(out_refs[0].dtype)
        elif epi == "resid":
            x_ref, g_ref = extra_refs
            out_refs[0][...] = x_ref[...] + g_ref[...] * y
            out_refs[1][...] = y
        else:
            r = jnp.maximum(y, 0.0)
            out_refs[0][...] = (r * r).astype(out_refs[0].dtype)
            out_refs[1][...] = r.astype(out_refs[1].dtype)
            out_refs[2][...] = (r * r).T.astype(out_refs[2].dtype)

    def body(a_ref, b_ref, *rest):
        extra_refs = rest[:n_extra]
        out_refs = rest[n_extra:n_extra + n_out]
        part = _dot(a_ref[...].astype(_BF), b_ref[...].astype(_BF), 1, 0)
        if nk == 1:
            finish(part, extra_refs, out_refs)
        else:
            acc_ref = rest[-1]
            k = pl.program_id(2)

            @pl.when(k == 0)
            def _():
                acc_ref[...] = part

            @pl.when(k > 0)
            def _():
                acc_ref[...] += part

            @pl.when(k == nk - 1)
            def _():
                finish(acc_ref[...], extra_refs, out_refs)

    tile = pl.BlockSpec((tm, tn), lambda i, j, k: (i, j))
    in_specs = [pl.BlockSpec((tm, tk), lambda i, j, k: (i, k)), pl.BlockSpec((tk, tn), lambda i, j, k: (k, j))]
    if epi == "plain":
        out_shape = [jax.ShapeDtypeStruct((m, n), F32)]
    elif epi == "resid":
        in_specs += [tile, pl.BlockSpec((1, tn), lambda i, j, k: (0, j))]
        out_shape = [jax.ShapeDtypeStruct((m, n), F32)] * 2
    out_specs = [tile] * n_out
    if epi == "relu2":
        out_shape = [jax.ShapeDtypeStruct((m, n), _BF), jax.ShapeDtypeStruct((m, n), F32),
                     jax.ShapeDtypeStruct((n, m), _BF)]
        out_specs[2] = pl.BlockSpec((tn, tm), lambda i, j, k: (j, i))
    outs = _pcall(
        body, name=name, grid=(m // tm, n // tn, nk), out_shape=out_shape,
        in_specs=in_specs, out_specs=out_specs,
        scratch_shapes=[pltpu.VMEM((tm, tn), F32)] if nk > 1 else [],
        compiler_params=_cp(PAR, PAR, ARB),
    )(a, b, *extras)
    return outs[0] if n_out == 1 else outs


def _mm_out(parts, w, x, gate, name):
    s, n = x.shape
    widths = [p.shape[1] for p in parts]
    kdim = sum(widths)
    tm = min(ROW_T, s)

    def body(a0_ref, a1_ref, a2_ref, w_ref, x_ref, g_ref, xo_ref, y_ref, mt_ref):
        mixed = jnp.concatenate([a0_ref[...], a1_ref[...], a2_ref[...]], axis=-1)
        y = _dot(mixed.astype(_BF), w_ref[...], 1, 0)
        xo_ref[...] = x_ref[...] + g_ref[...] * y
        y_ref[...] = y
        mt_ref[...] = mixed.T.astype(mt_ref.dtype)

    row = lambda width: pl.BlockSpec((tm, width), lambda i: (i, 0))
    return _pcall(
        body, name=name, grid=(s // tm,),
        out_shape=[jax.ShapeDtypeStruct((s, n), F32), jax.ShapeDtypeStruct((s, n), F32),
                   jax.ShapeDtypeStruct((kdim, s), _BF)],
        in_specs=[row(widths[0]), row(widths[1]), row(widths[2]), pl.BlockSpec((kdim, n), lambda i: (0, 0)),
                  row(n), pl.BlockSpec((1, n), lambda i: (0, 0))],
        out_specs=[row(n), row(n), pl.BlockSpec((kdim, tm), lambda i: (0, i))], compiler_params=_cp(PAR),
    )(*parts, w, x, gate)


def _mm_nt(a, b, epi, extras, out_dtype, name):
    m, kdim = a.shape
    n = b.shape[0]
    tm, tn, tk = min(MM_T, m), min(MM_T, n), min(MM_T, kdim)
    nk = kdim // tk
    n_extra = len(extras)

    def finish(y, extra_refs, o_ref):
        if epi == "mul2":
            y = y * (2.0 * extra_refs[0][...].astype(F32))
        o_ref[...] = y.astype(o_ref.dtype)

    def body(a_ref, b_ref, *rest):
        extra_refs = rest[:n_extra]
        o_ref = rest[n_extra]
        part = _dot(a_ref[...].astype(_BF), b_ref[...].astype(_BF), 1, 1)
        if nk == 1:
            finish(part, extra_refs, o_ref)
        else:
            acc_ref = rest[-1]
            k = pl.program_id(2)

            @pl.when(k == 0)
            def _():
                acc_ref[...] = part

            @pl.when(k > 0)
            def _():
                acc_ref[...] += part

            @pl.when(k == nk - 1)
            def _():
                finish(acc_ref[...], extra_refs, o_ref)

    tile = pl.BlockSpec((tm, tn), lambda i, j, k: (i, j))
    in_specs = [pl.BlockSpec((tm, tk), lambda i, j, k: (i, k)), pl.BlockSpec((tn, tk), lambda i, j, k: (j, k))]
    in_specs += [tile] * n_extra
    return _pcall(
        body, name=name, grid=(m // tm, n // tn, nk), out_shape=jax.ShapeDtypeStruct((m, n), out_dtype),
        in_specs=in_specs, out_specs=tile,
        scratch_shapes=[pltpu.VMEM((tm, tn), F32)] if nk > 1 else [],
        compiler_params=_cp(PAR, PAR, ARB),
    )(a, b, *extras)


def _mm_grad(at, b, split, name):
    m, s = at.shape
    n = b.shape[1]
    ts = min(GRAD_TS, s)
    ns = s // ts
    if split == "rows":
        tm, tn = min(MM_T, m // N_DEV), min(MM_T, n)
        per = (m // N_DEV) // tm
        out_shape = (N_DEV, m // N_DEV, n)
        out_spec = pl.BlockSpec((None, tm, tn), lambda i, j, k: (i // per, i % per, j))
    elif split == "cols":
        tm, tn = min(MM_T, m), min(MM_T, n // N_DEV)
        per = (n // N_DEV) // tn
        out_shape = (N_DEV, m, n // N_DEV)
        out_spec = pl.BlockSpec((None, tm, tn), lambda i, j, k: (j // per, i, j % per))
    else:
        tm, tn = min(MM_T, m), min(MM_T, n)
        out_shape = (m, n)
        out_spec = pl.BlockSpec((tm, tn), lambda i, j, k: (i, j))

    def body(a_ref, b_ref, o_ref, acc_ref):
        k = pl.program_id(2)
        part = _dot(a_ref[...].astype(_BF), b_ref[...].astype(_BF), 1, 0)

        @pl.when(k == 0)
        def _():
            acc_ref[...] = part

        @pl.when(k > 0)
        def _():
            acc_ref[...] += part

        @pl.when(k == ns - 1)
        def _():
            o_ref[...] = acc_ref[...].astype(o_ref.dtype)

    return _pcall(
        body, name=name, grid=(m // tm, n // tn, ns), out_shape=jax.ShapeDtypeStruct(out_shape, _XBF),
        in_specs=[pl.BlockSpec((tm, ts), lambda i, j, k: (i, k)), pl.BlockSpec((ts, tn), lambda i, j, k: (k, j))],
        out_specs=out_spec, scratch_shapes=[pltpu.VMEM((tm, tn), F32)],
        compiler_params=_cp(PAR, PAR, ARB),
    )(at, b)


def _head_masks():
    lane = _iota2((1, 128), 1)
    return [(lane // HEAD_DIM) == hh for hh in range(2)]


def _softplus_parts(z):
    sp = jnp.maximum(z, 0.0) + jnp.log(1.0 + jnp.exp(-jnp.abs(z)))
    return -sp, z - sp


def _sb_key_loop(i, carry, step, lookahead, first_ahead):
    def cond(c):
        return jnp.logical_and(c[0] <= i, jnp.max(jnp.maximum(c[1][0][0], c[1][0][1])) > UNDERFLOW)

    def body(c):
        ahead = lookahead(i - c[0] - 1)
        return c[0] + 1, step(i - c[0], c[1], c[2]), ahead

    return lax.while_loop(cond, body, (jnp.int32(1), carry, first_ahead))[1]


def _sb_fwd(proj, name, comm=None):
    s = proj.shape[0]
    t = min(ATT_T, s)
    nblk = s // t
    scale = HEAD_DIM ** -0.5

    def body(q_ref, k_ref, v_ref, o_ref):
        i = pl.program_id(1)
        masks = _head_masks()
        row = _iota2((t, t), 0)
        col = _iota2((t, t), 1)
        u_strict = (row > col).astype(_BF)
        causal = col < row
        qs = q_ref[...] * scale
        qm = [jnp.where(masks[hh], qs, 0.0).astype(_BF) for hh in range(2)]

        def scores(j):
            kb = k_ref[pl.ds(pl.multiple_of(jnp.maximum(j, 0) * t, t), t), :].astype(_BF)
            return tuple(_dot(qm[hh], kb, 1, 1) for hh in range(2))

        def block(j, carry, qk, diag):
            off = pl.multiple_of(j * t, t)
            vb = v_ref[pl.ds(off, t), :].astype(_BF)
            runs, outs = [], []
            for hh in range(2):
                run, o = carry[0][hh], carry[1][hh]
                z = qk[hh]
                l1, lb = _softplus_parts(z)
                if diag:
                    l1 = jnp.where(causal, l1, 0.0)
                between = run + _dot_exact(l1, u_strict, 1, 0, 2)
                a = jnp.exp(lb + between)
                if diag:
                    a = jnp.where(causal, a, 0.0)
                outs.append(o + _dot_exact(a, vb, 1, 0, 2))
                runs.append(run + jnp.sum(l1, axis=-1, keepdims=True))
            return tuple(runs), tuple(outs)

        zero, zero_o = jnp.zeros((t, 1), F32), jnp.zeros((t, 128), F32)
        ahead = scores(i - 1)
        carry = block(i, ((zero, zero), (zero_o, zero_o)), scores(i), True)
        carry = _sb_key_loop(i, carry, lambda j, c, qk: block(j, c, qk, False), scores, ahead)
        o_ref[...] = jnp.where(masks[0], carry[1][0], carry[1][1])

    kv_spec = lambda c0: pl.BlockSpec((s, 128), lambda p, i: (0, c0 // 128 + p))
    c_operands, c_in_specs, c_out_shapes, c_out_specs, c_sems = _comm_operands(comm) if comm else ([], [], [], [], [])
    outs = _pcall(
        _with_comm(body, comm, 3, 1, *_grid_ends(SB_HEADS // 2, nblk)), name=name, grid=(SB_HEADS // 2, nblk),
        out_shape=[jax.ShapeDtypeStruct((s, 256), F32)] + c_out_shapes,
        in_specs=[pl.BlockSpec((t, 128), lambda p, i: (i, C_QA // 128 + p)), kv_spec(C_KA), kv_spec(C_VA)] + c_in_specs,
        out_specs=[pl.BlockSpec((t, 128), lambda p, i: (i, p))] + c_out_specs,
        scratch_shapes=c_sems, input_output_aliases=_comm_aliases(comm, 3, 1), compiler_params=_cp(ARB, ARB),
    )(proj, proj, proj, *c_operands)
    return outs[0], list(outs[1:])


def _sb_bwd(proj, o, dmix, name, comm=None):
    s = proj.shape[0]
    t = min(ATT_T, s)
    nblk = s // t
    scale = HEAD_DIM ** -0.5

    def body(q_ref, k_ref, v_ref, o_ref, do_ref, dq_ref, dk_ref, dv_ref):
        i = pl.program_id(1)
        masks = _head_masks()
        row = _iota2((t, t), 0)
        col = _iota2((t, t), 1)
        u_strict = (row > col).astype(_BF)
        u_incl = (row >= col).astype(_BF)
        causal = col < row

        @pl.when(i == 0)
        def _():
            dk_ref[...] = jnp.zeros_like(dk_ref)
            dv_ref[...] = jnp.zeros_like(dv_ref)

        qs = q_ref[...] * scale
        dov = do_ref[...]
        ov = o_ref[...]
        qm = [jnp.where(masks[hh], qs, 0.0).astype(_BF) for hh in range(2)]
        dom = [jnp.where(masks[hh], dov, 0.0).astype(_BF) for hh in range(2)]
        dsum = [jnp.sum(dom[hh].astype(F32) * ov, axis=-1, keepdims=True) for hh in range(2)]

        def scores(j):
            off = pl.multiple_of(jnp.maximum(j, 0) * t, t)
            kb = k_ref[pl.ds(off, t), :].astype(_BF)
            vb = v_ref[pl.ds(off, t), :].astype(_BF)
            return tuple((_dot(qm[hh], kb, 1, 1), _dot(dom[hh], vb, 1, 1)) for hh in range(2))

        def block(j, carry, ahead, diag):
            off = pl.multiple_of(j * t, t)
            kb = k_ref[pl.ds(off, t), :].astype(_BF)
            runs, eruns, dqs = [], [], []
            dk = dv = None
            for hh in range(2):
                run, erun, dq = carry[0][hh], carry[1][hh], carry[2][hh]
                z, da = ahead[hh]
                l1, lb = _softplus_parts(z)
                if diag:
                    l1 = jnp.where(causal, l1, 0.0)
                between = run + _dot_exact(l1, u_strict, 1, 0, 2)
                a = jnp.exp(lb + between)
                if diag:
                    a = jnp.where(causal, a, 0.0)
                g = a * da
                cum = dsum[hh] - (erun + _dot_exact(g, u_incl, 1, 0, 2))
                beta = jnp.exp(lb)
                dz = g * (1.0 - beta) - cum * beta
                if diag:
                    dz = jnp.where(causal, dz, 0.0)
                dzb = dz.astype(_BF)
                dqs.append(dq + _dot(dzb, kb, 1, 0))
                dk_h = _dot(dzb, qm[hh], 0, 0)
                dv_h = _dot(a.astype(_BF), dom[hh], 0, 0)
                dk = dk_h if dk is None else dk + dk_h
                dv = dv_h if dv is None else dv + dv_h
                runs.append(run + jnp.sum(l1, axis=-1, keepdims=True))
                eruns.append(erun + jnp.sum(g, axis=-1, keepdims=True))
            dk_ref[pl.ds(off, t), :] += dk
            dv_ref[pl.ds(off, t), :] += dv
            return tuple(runs), tuple(eruns), tuple(dqs)

        zero, zero_q = jnp.zeros((t, 1), F32), jnp.zeros((t, 128), F32)
        ahead = scores(i - 1)
        carry = block(i, ((zero, zero), (zero, zero), (zero_q, zero_q)), scores(i), True)
        carry = _sb_key_loop(i, carry, lambda j, c, a: block(j, c, a, False), scores, ahead)
        dq_ref[...] = jnp.where(masks[0], carry[2][0], carry[2][1]) * scale

    kv_spec = lambda c0: pl.BlockSpec((s, 128), lambda p, i: (0, c0 // 128 + p))
    blk = lambda c0: pl.BlockSpec((t, 128), lambda p, i: (i, c0 // 128 + p))
    acc = pl.BlockSpec((s, 128), lambda p, i: (0, p))
    shp = jax.ShapeDtypeStruct((s, 256), F32)
    c_operands, c_in_specs, c_out_shapes, c_out_specs, c_sems = _comm_operands(comm) if comm else ([], [], [], [], [])
    outs = _pcall(
        _with_comm(body, comm, 5, 3, *_grid_ends(SB_HEADS // 2, nblk)), name=name, grid=(SB_HEADS // 2, nblk),
        out_shape=[shp, shp, shp] + c_out_shapes,
        in_specs=[blk(C_QA), kv_spec(C_KA), kv_spec(C_VA), blk(0), blk(M_SB)] + c_in_specs,
        out_specs=[blk(0), acc, acc] + c_out_specs,
        scratch_shapes=c_sems, input_output_aliases=_comm_aliases(comm, 5, 3), compiler_params=_cp(ARB, ARB),
    )(proj, proj, proj, o, dmix, *c_operands)
    return outs[0], outs[1], outs[2], list(outs[3:])


def _group_matrix(width):
    r = _iota2((width, width), 0) // HEAD_DIM
    c = _iota2((width, width), 1) // HEAD_DIM
    return (r == c).astype(_BF)


def _group_mean(v, gm):
    return _dot_exact(v, gm, 1, 0, 2) * (1.0 / HEAD_DIM)


def _fox_prep(proj, gq, gk, bf, name):
    s = proj.shape[0]
    t = min(ATT_T, s)
    nblk = s // t
    assert nblk <= 128

    def body(q_ref, k_ref, fl_ref, gq_ref, gk_ref, bf_ref, qn_ref, kn_ref, cfc_ref, cfr_ref, cfe_ref, carry_ref):
        i = pl.program_id(0)
        gm = _group_matrix(512)
        for src, gref, dst in ((q_ref, gq_ref, qn_ref), (k_ref, gk_ref, kn_ref)):
            v = src[...]
            rstd = lax.rsqrt(_group_mean(v * v, gm) + EPS)
            dst[...] = (v * rstd * gref[...]).astype(dst.dtype)

        @pl.when(i == 0)
        def _():
            carry_ref[...] = jnp.zeros_like(carry_ref)
            cfe_ref[...] = jnp.full(cfe_ref.shape, NO_BLOCK, F32)

        lane = _iota2((1, 128), 1)
        logf = jnp.where(lane < FOX_HEADS, _softplus_parts(-(fl_ref[...] + bf_ref[...]))[0], 0.0)
        lower = (_iota2((t, t), 0) >= _iota2((t, t), 1)).astype(_BF)
        cf = carry_ref[...] + _dot_exact(logf, lower, 1, 0, 3, v_left=False)
        cfc_ref[...] = cf
        rows = cf.T[:FOX_HEADS, :]
        cfr_ref[...] = rows
        cfe_ref[...] = jnp.where(lane == i, rows[:, t - 1:t], cfe_ref[...])
        carry_ref[...] = cf[t - 1:t, :]

    col = lambda w, c0: pl.BlockSpec((t, w), lambda i: (i, c0 // w))
    v512 = pl.BlockSpec((1, 512), lambda i: (0, 0))
    return _pcall(
        body, name=name, grid=(nblk,),
        out_shape=[jax.ShapeDtypeStruct((s, 512), _BF), jax.ShapeDtypeStruct((s, 512), _BF),
                   jax.ShapeDtypeStruct((s, 128), F32), jax.ShapeDtypeStruct((FOX_HEADS, s), F32),
                   jax.ShapeDtypeStruct((FOX_HEADS, 128), F32)],
        in_specs=[col(512, C_QB), col(512, C_KB), col(128, C_FL), v512, v512, pl.BlockSpec((1, 128), lambda i: (0, 0))],
        out_specs=[col(512, 0), col(512, 0), col(128, 0), pl.BlockSpec((FOX_HEADS, t), lambda i: (0, i)),
                   pl.BlockSpec((FOX_HEADS, 128), lambda i: (0, 0))],
        scratch_shapes=[pltpu.VMEM((1, 128), F32)], compiler_params=_cp(ARB),
    )(proj, proj, proj, gq, gk, bf)


def _fox_cq(cfc_blk, head):
    lane = _iota2((1, 128), 1)
    return jnp.sum(jnp.where(lane == head, cfc_blk, 0.0), axis=-1, keepdims=True)


def _fox_ck(cfr_ref, off, t, head):
    sub = _iota2((FOX_HEADS, 1), 0)
    return jnp.sum(jnp.where(sub == head, cfr_ref[:, pl.ds(off, t)], 0.0), axis=0, keepdims=True)


def _fox_blocks_needed(top, cfe_ref, head, i):
    sub = _iota2((FOX_HEADS, 1), 0)
    lane = _iota2((1, 128), 1)
    last_key = jnp.sum(jnp.where(sub == head, cfe_ref[...], 0.0), axis=0, keepdims=True)
    need = jnp.logical_and(lane < i, top - last_key > UNDERFLOW)
    return jnp.sum(need.astype(jnp.int32))


def _grid_ends(n_outer, n_inner):
    first = lambda: jnp.logical_and(pl.program_id(0) == 0, pl.program_id(1) == 0)
    last = lambda: jnp.logical_and(pl.program_id(0) == n_outer - 1, pl.program_id(1) == n_inner - 1)
    return first, last


def _fox_fwd(proj, qn, kn, cfc, cfr, cfe, qk_bound, name, comm=None):
    s = proj.shape[0]
    t = min(ATT_T, s)
    nblk = s // t
    scale = HEAD_DIM ** -0.5

    def body(q_ref, k_ref, v_ref, cfc_ref, cfr_ref, cfe_ref, qkb_ref, o_ref, lse_ref):
        p = pl.program_id(0)
        i = pl.program_id(1)
        masks = _head_masks()
        lane = _iota2((1, 128), 1)
        valid = _iota2((t, t), 1) <= _iota2((t, t), 0)
        qv = q_ref[...]
        cfc_blk = cfc_ref[...]
        heads = [2 * p + hh for hh in range(2)]
        qm = [(jnp.where(masks[hh], qv, 0) * scale).astype(_BF) for hh in range(2)]
        cq = [_fox_cq(cfc_blk, heads[hh]) for hh in range(2)]

        def scores(j):
            kb = k_ref[pl.ds(pl.multiple_of(jnp.maximum(j, 0) * t, t), t), :]
            return tuple(_dot(qm[hh], kb, 1, 1) for hh in range(2))

        def block(j, carry, qk, diag):
            off = pl.multiple_of(j * t, t)
            vb = v_ref[pl.ds(off, t), :].astype(_BF)
            ms, ls, accs = [], [], []
            for hh in range(2):
                m, l, acc = carry[0][hh], carry[1][hh], carry[2][hh]
                z = qk[hh] + (cq[hh] - _fox_ck(cfr_ref, off, t, heads[hh]))
                if diag:
                    z = jnp.where(valid, z, NEG)
                m_new = jnp.maximum(m, jnp.max(z, axis=-1, keepdims=True))
                pe = jnp.exp(z - m_new)
                alpha = jnp.exp(m - m_new)
                ms.append(m_new)
                ls.append(alpha * l + jnp.sum(pe, axis=-1, keepdims=True))
                accs.append(alpha * acc + _dot(pe.astype(_BF), vb, 1, 0))
            return tuple(ms), tuple(ls), tuple(accs)

        def step(jj, c):
            ahead = scores(i - jj - 2)
            return block(i - jj - 1, c[0], c[1], False), ahead

        neg, zero, zero_o = jnp.full((t, 1), NEG, F32), jnp.zeros((t, 1), F32), jnp.zeros((t, 128), F32)
        ahead = scores(i - 1)
        carry = block(i, ((neg, neg), (zero, zero), (zero_o, zero_o)), scores(i), True)
        needed = [_fox_blocks_needed(jnp.max(qkb_ref[...] + cq[hh] - carry[0][hh]), cfe_ref, heads[hh], i)
                  for hh in range(2)]
        m, l, acc = lax.fori_loop(0, jnp.maximum(needed[0], needed[1]), step, (carry, ahead))[0]
        o_ref[...] = jnp.where(masks[0], acc[0] / l[0], acc[1] / l[1])
        lse_ref[...] = jnp.where(lane == 0, m[0] + jnp.log(l[0]), jnp.where(lane == 1, m[1] + jnp.log(l[1]), 0.0))

    c_operands, c_in_specs, c_out_shapes, c_out_specs, c_sems = _comm_operands(comm) if comm else ([], [], [], [], [])
    outs = _pcall(
        _with_comm(body, comm, 7, 2, *_grid_ends(FOX_HEADS // 2, nblk)), name=name, grid=(FOX_HEADS // 2, nblk),
        out_shape=[jax.ShapeDtypeStruct((s, 512), F32), jax.ShapeDtypeStruct((FOX_HEADS // 2, s, 128), F32)] + c_out_shapes,
        in_specs=[pl.BlockSpec((t, 128), lambda p, i: (i, p)), pl.BlockSpec((s, 128), lambda p, i: (0, p)),
                  pl.BlockSpec((s, 128), lambda p, i: (0, C_VB // 128 + p)),
                  pl.BlockSpec((t, 128), lambda p, i: (i, 0)), pl.BlockSpec((FOX_HEADS, s), lambda p, i: (0, 0)),
                  pl.BlockSpec((FOX_HEADS, 128), lambda p, i: (0, 0)), pl.BlockSpec((1, 1), lambda p, i: (0, 0))] + c_in_specs,
        out_specs=[pl.BlockSpec((t, 128), lambda p, i: (i, p)),
                   pl.BlockSpec((None, t, 128), lambda p, i: (p, i, 0))] + c_out_specs,
        scratch_shapes=c_sems, input_output_aliases=_comm_aliases(comm, 7, 2),
        compiler_params=_cp(ARB, ARB),
    )(qn, kn, proj, cfc, cfr, cfe, qk_bound, *c_operands)
    return outs[0], outs[1], list(outs[2:])


def _fox_bwd(proj, qn, kn, cfc, cfr, cfe, qk_bound, o, lse, dmix, name, comm=None):
    s = proj.shape[0]
    t = min(ATT_T, s)
    nblk = s // t
    scale = HEAD_DIM ** -0.5

    def body(q_ref, k_ref, v_ref, cfc_ref, cfr_ref, cfe_ref, qkb_ref, o_ref, lse_ref, do_ref,
             dq_ref, dk_ref, dv_ref, dcr_ref):
        p = pl.program_id(0)
        i = pl.program_id(1)
        masks = _head_masks()
        lane = _iota2((1, 128), 1)
        valid = _iota2((t, t), 1) <= _iota2((t, t), 0)
        before = (_iota2((t, t), 0) < _iota2((t, t), 1)).astype(_BF)

        @pl.when(i == 0)
        def _():
            dk_ref[...] = jnp.zeros_like(dk_ref)
            dv_ref[...] = jnp.zeros_like(dv_ref)

        @pl.when(jnp.logical_and(p == 0, i == 0))
        def _():
            dcr_ref[...] = jnp.zeros_like(dcr_ref)

        head_row = _iota2((FOX_HEADS, 1), 0)
        qv = q_ref[...]
        cfc_blk = cfc_ref[...]
        dov = do_ref[...]
        ov = o_ref[...]
        lse_blk = lse_ref[...]
        heads = [2 * p + hh for hh in range(2)]
        qm = [(jnp.where(masks[hh], qv, 0) * scale).astype(_BF) for hh in range(2)]
        dom = [jnp.where(masks[hh], dov, 0.0).astype(_BF) for hh in range(2)]
        dsum = [jnp.sum(dom[hh].astype(F32) * ov, axis=-1, keepdims=True) for hh in range(2)]
        lse_h = [jnp.sum(jnp.where(lane == hh, lse_blk, 0.0), axis=-1, keepdims=True) for hh in range(2)]
        cq = [_fox_cq(cfc_blk, heads[hh]) for hh in range(2)]
        needed = [_fox_blocks_needed(jnp.max(qkb_ref[...] + cq[hh] - lse_h[hh]), cfe_ref, heads[hh], i)
                  for hh in range(2)]
        first = i - jnp.maximum(needed[0], needed[1])

        def scores(j):
            off = pl.multiple_of(j * t, t)
            kb = k_ref[pl.ds(off, t), :]
            vb = v_ref[pl.ds(off, t), :].astype(_BF)
            return tuple((_dot(qm[hh], kb, 1, 1), _dot(dom[hh], vb, 1, 1)) for hh in range(2))

        def block(j, carry, ahead, diag):
            off = pl.multiple_of(j * t, t)
            kb = k_ref[pl.ds(off, t), :]
            dqs, pres = [], []
            dk = dv = None
            for hh in range(2):
                dq, pre = carry[0][hh], carry[1][hh]
                z = ahead[hh][0] + (cq[hh] - _fox_ck(cfr_ref, off, t, heads[hh]))
                pm = jnp.exp(z - lse_h[hh])
                if diag:
                    pm = jnp.where(valid, pm, 0.0)
                ds = pm * (ahead[hh][1] - dsum[hh])
                dsb = ds.astype(_BF)
                dqs.append(dq + _dot(dsb, kb, 1, 0))
                dk_h = _dot(dsb, qm[hh], 0, 0)
                dv_h = _dot(pm.astype(_BF), dom[hh], 0, 0)
                dk = dk_h if dk is None else dk + dk_h
                dv = dv_h if dv is None else dv + dv_h
                if diag:
                    lower_keys = jnp.where(valid, pre + _dot_exact(ds, before, 1, 0, 2), 0.0)
                    dlogf = jnp.sum(lower_keys, axis=0, keepdims=True)
                else:
                    dlogf = jnp.sum(pre) + _dot_exact(jnp.sum(ds, axis=0, keepdims=True), before, 1, 0, 3)
                dcr_ref[:, pl.ds(off, t)] += jnp.where(head_row == heads[hh], dlogf, 0.0)
                pres.append(pre + jnp.sum(ds, axis=-1, keepdims=True))
            dk_ref[pl.ds(off, t), :] += dk
            dv_ref[pl.ds(off, t), :] += dv
            return tuple(dqs), tuple(pres)

        zero, zero_q = jnp.zeros((t, 1), F32), jnp.zeros((t, 128), F32)


        def step(j, c):
            ahead = scores(j + 1)
            return block(j, c[0], c[1], False), ahead

        carry, ahead = lax.fori_loop(first, i, step, (((zero_q, zero_q), (zero, zero)), scores(first)))
        dqs, _ = block(i, carry, ahead, True)
        dq_ref[...] = jnp.where(masks[0], dqs[0], dqs[1]) * scale

    blk = lambda c0: pl.BlockSpec((t, 128), lambda p, i: (i, c0 // 128 + p))
    res = lambda c0: pl.BlockSpec((s, 128), lambda p, i: (0, c0 // 128 + p))
    shp = jax.ShapeDtypeStruct((s, 512), F32)
    c_operands, c_in_specs, c_out_shapes, c_out_specs, c_sems = _comm_operands(comm) if comm else ([], [], [], [], [])
    outs = _pcall(
        _with_comm(body, comm, 10, 4, *_grid_ends(FOX_HEADS // 2, nblk)), name=name, grid=(FOX_HEADS // 2, nblk),
        out_shape=[shp, shp, shp, jax.ShapeDtypeStruct((FOX_HEADS, s), F32)] + c_out_shapes,
        in_specs=[blk(0), res(0), res(C_VB), pl.BlockSpec((t, 128), lambda p, i: (i, 0)),
                  pl.BlockSpec((FOX_HEADS, s), lambda p, i: (0, 0)), pl.BlockSpec((FOX_HEADS, 128), lambda p, i: (0, 0)),
                  pl.BlockSpec((1, 1), lambda p, i: (0, 0)), blk(0),
                  pl.BlockSpec((None, t, 128), lambda p, i: (p, i, 0)), blk(M_FOX)] + c_in_specs,
        out_specs=[blk(0), res(0), res(0), pl.BlockSpec((FOX_HEADS, s), lambda p, i: (0, 0))] + c_out_specs,
        scratch_shapes=c_sems, input_output_aliases=_comm_aliases(comm, 10, 4),
        compiler_params=_cp(ARB, ARB),
    )(qn, kn, proj, cfc, cfr, cfe, qk_bound, o, lse, dmix, *c_operands)
    return outs[0], outs[1], outs[2], outs[3], list(outs[4:])


def _fox_prep_bwd(proj, gq, gk, bf, dqn, dkn, dlogf, name):
    s = proj.shape[0]
    t = min(ATT_T, s)
    nblk = s // t

    def body(q_ref, k_ref, fl_ref, gq_ref, gk_ref, bf_ref, dqn_ref, dkn_ref, dlogf_ref,
             dq_ref, dk_ref, dfl_ref, dgq_ref, dgk_ref, dbf_ref):
        i = pl.program_id(0)
        gm = _group_matrix(512)

        @pl.when(i == 0)
        def _():
            dgq_ref[...] = jnp.zeros_like(dgq_ref)
            dgk_ref[...] = jnp.zeros_like(dgk_ref)
            dbf_ref[...] = jnp.zeros_like(dbf_ref)

        for src, gref, dyref, dst, dgref in ((q_ref, gq_ref, dqn_ref, dq_ref, dgq_ref),
                                              (k_ref, gk_ref, dkn_ref, dk_ref, dgk_ref)):
            v = src[...]
            dy = dyref[...]
            rstd = lax.rsqrt(_group_mean(v * v, gm) + EPS)
            xhat = v * rstd
            dgref[...] += jnp.sum(dy * xhat, axis=0, keepdims=True)
            dxh = dy * gref[...]
            dst[...] = (rstd * (dxh - xhat * _group_mean(dxh * xhat, gm))).astype(dst.dtype)

        to_lanes = (_iota2((FOX_HEADS, 128), 0) == _iota2((FOX_HEADS, 128), 1)).astype(_BF)
        dlogf = _dot_exact(dlogf_ref[...], to_lanes, 0, 0, 3)
        xv = fl_ref[...] + bf_ref[...]
        e = jnp.exp(-jnp.abs(xv))
        dfl = dlogf * (jnp.where(xv >= 0.0, e, 1.0) / (1.0 + e))
        dfl_ref[...] = dfl.astype(dfl_ref.dtype)
        dbf_ref[...] += jnp.sum(dfl, axis=0, keepdims=True)

    col = lambda w, c0: pl.BlockSpec((t, w), lambda i: (i, c0 // w))
    v512 = pl.BlockSpec((1, 512), lambda i: (0, 0))
    v128 = pl.BlockSpec((1, 128), lambda i: (0, 0))
    return _pcall(
        body, name=name, grid=(nblk,),
        out_shape=[jax.ShapeDtypeStruct((s, 512), _BF), jax.ShapeDtypeStruct((s, 512), _BF),
                   jax.ShapeDtypeStruct((s, 128), _BF), jax.ShapeDtypeStruct((1, 512), F32),
                   jax.ShapeDtypeStruct((1, 512), F32), jax.ShapeDtypeStruct((1, 128), F32)],
        in_specs=[col(512, C_QB), col(512, C_KB), col(128, C_FL), v512, v512, v128, col(512, 0), col(512, 0),
                  pl.BlockSpec((FOX_HEADS, t), lambda i: (0, i))],
        out_specs=[col(512, 0), col(512, 0), col(128, 0), v512, v512, v128], compiler_params=_cp(ARB),
    )(proj, proj, proj, gq, gk, bf, dqn, dkn, dlogf)


GELU_C = 0.7978845608028654
GELU_A = 0.044715


def _gelu(x):
    return 0.5 * x * (1.0 + jnp.tanh(GELU_C * (x + GELU_A * x * x * x)))


def _gelu_grad(x):
    th = jnp.tanh(GELU_C * (x + GELU_A * x * x * x))
    return 0.5 * (1.0 + th) + 0.5 * x * (1.0 - th * th) * (GELU_C * (1.0 + 3.0 * GELU_A * x * x))


def _sgu_tril():
    return _iota2((SGU_CHUNK, SGU_CHUNK), 0) >= _iota2((SGU_CHUNK, SGU_CHUNK), 1)


def _sgu_group_masks():
    lane = _iota2((1, 256), 1)
    return [(lane // HEAD_DIM) == g for g in range(SGU_GROUPS)]


def _sgu_fwd(proj, w, gs, bexp, name):
    s = proj.shape[0]
    t = min(2 * SGU_CHUNK, s)

    def body(u_ref, v_ref, w_ref, gs_ref, b_ref, o_ref):
        gm = _group_matrix(256)
        gmask = _sgu_group_masks()
        tril = _sgu_tril()
        u = _gelu(u_ref[...])
        vg = _gelu(v_ref[...])
        vhat = (vg * lax.rsqrt(_group_mean(vg * vg, gm) + EPS) * gs_ref[...]).astype(_BF)
        for ch in range(t // SGU_CHUNK):
            rows = slice(ch * SGU_CHUNK, (ch + 1) * SGU_CHUNK)
            mixed = b_ref[...]
            for g in range(SGU_GROUPS):
                wg = jnp.where(tril, w_ref[g], 0.0).astype(_BF)
                mixed = jnp.where(gmask[g], mixed + _dot(wg, vhat[rows], 1, 0), mixed)
            o_ref[rows, :] = u[rows] * mixed

    return _pcall(
        body, name=name, grid=(s // t,), out_shape=jax.ShapeDtypeStruct((s, 256), F32),
        in_specs=[_row_spec(t, 256, C_UC // 256), _row_spec(t, 256, C_VC // 256),
                  pl.BlockSpec((SGU_GROUPS, SGU_CHUNK, SGU_CHUNK), lambda i: (0, 0, 0)), _vec_spec(256),
                  pl.BlockSpec((SGU_CHUNK, 256), lambda i: (0, 0))],
        out_specs=_row_spec(t, 256), compiler_params=_cp(PAR),
    )(proj, proj, w, gs, bexp)


def _sgu_bwd(proj, w, gs, bexp, dmix, name):
    s = proj.shape[0]
    t = min(2 * SGU_CHUNK, s)
    nstep = s // t

    def body(u_ref, v_ref, w_ref, gs_ref, b_ref, do_ref, du_ref, dv_ref, dw_ref, db_ref, dgs_ref):
        i = pl.program_id(0)
        gm = _group_matrix(256)
        gmask = _sgu_group_masks()
        tril = _sgu_tril()

        @pl.when(i == 0)
        def _():
            dw_ref[...] = jnp.zeros_like(dw_ref)
            db_ref[...] = jnp.zeros_like(db_ref)
            dgs_ref[...] = jnp.zeros_like(dgs_ref)

        uc = u_ref[...]
        vc = v_ref[...]
        u = _gelu(uc)
        vg = _gelu(vc)
        rstd = lax.rsqrt(_group_mean(vg * vg, gm) + EPS)
        xh = vg * rstd
        gsv = gs_ref[...]
        vhat = (xh * gsv).astype(_BF)
        dov = do_ref[...]
        dm = dov * u
        for ch in range(t // SGU_CHUNK):
            rows = slice(ch * SGU_CHUNK, (ch + 1) * SGU_CHUNK)
            mixed = b_ref[...]
            dvh = jnp.zeros((SGU_CHUNK, 256), F32)
            dmc = dm[rows]
            for g in range(SGU_GROUPS):
                wg = jnp.where(tril, w_ref[g], 0.0).astype(_BF)
                mixed = jnp.where(gmask[g], mixed + _dot(wg, vhat[rows], 1, 0), mixed)
                dvh = jnp.where(gmask[g], _dot(wg, dmc.astype(_BF), 0, 0), dvh)
                dw_ref[g] += _dot(jnp.where(gmask[g], dmc, 0.0).astype(_BF), vhat[rows], 1, 1)
            db_ref[...] += dmc
            du_ref[rows, :] = (dov[rows] * mixed * _gelu_grad(uc[rows])).astype(du_ref.dtype)
            xhc = xh[rows]
            dgs_ref[...] += jnp.sum(dvh * xhc, axis=0, keepdims=True)
            dxh = dvh * gsv
            dvg = rstd[rows] * (dxh - xhc * _group_mean(dxh * xhc, gm))
            dv_ref[rows, :] = (dvg * _gelu_grad(vc[rows])).astype(dv_ref.dtype)

        @pl.when(i == nstep - 1)
        def _():
            for g in range(SGU_GROUPS):
                dw_ref[g] = jnp.where(tril, dw_ref[g], 0.0)

    wspec = pl.BlockSpec((SGU_GROUPS, SGU_CHUNK, SGU_CHUNK), lambda i: (0, 0, 0))
    bspec = pl.BlockSpec((SGU_CHUNK, 256), lambda i: (0, 0))
    return _pcall(
        body, name=name, grid=(nstep,),
        out_shape=[jax.ShapeDtypeStruct((s, 256), _BF), jax.ShapeDtypeStruct((s, 256), _BF),
                   jax.ShapeDtypeStruct((SGU_GROUPS, SGU_CHUNK, SGU_CHUNK), F32),
                   jax.ShapeDtypeStruct((SGU_CHUNK, 256), F32), jax.ShapeDtypeStruct((1, 256), F32)],
        in_specs=[_row_spec(t, 256, C_UC // 256), _row_spec(t, 256, C_VC // 256), wspec, _vec_spec(256), bspec,
                  _row_spec(t, 256, M_SGU // 256)],
        out_specs=[_row_spec(t, 256), _row_spec(t, 256), wspec, bspec, _vec_spec(256)],
        compiler_params=_cp(ARB),
    )(proj, proj, w, gs, bexp, dmix)


def _ada_fwd(c_all, ada_w, name):
    depth, d, n = ada_w.shape

    def body(c_ref, w_ref, o_ref):
        cv = c_ref[...]
        cond = cv / (1.0 + jnp.exp(-cv))
        o_ref[...] = _dot_f32(cond, w_ref[...], 1, 0)

    return _pcall(
        body, name=name, grid=(depth,), out_shape=jax.ShapeDtypeStruct((depth, N_DEV, n), F32),
        in_specs=[pl.BlockSpec((N_DEV, d), lambda l: (0, 0)), pl.BlockSpec((None, d, n), lambda l: (l, 0, 0))],
        out_specs=pl.BlockSpec((None, N_DEV, n), lambda l: (l, 0, 0)), compiler_params=_cp(PAR),
    )(c_all, ada_w)


def _ada_bwd(c_all, dmod, name):
    depth, _, n = dmod.shape
    d = c_all.shape[1]

    def body(c_ref, dm_ref, o_ref):
        cv = c_ref[...]
        cond = cv / (1.0 + jnp.exp(-cv))
        o_ref[...] = _dot_f32(cond, dm_ref[...], 0, 0)

    return _pcall(
        body, name=name, grid=(depth,), out_shape=jax.ShapeDtypeStruct((depth, d, n), F32),
        in_specs=[pl.BlockSpec((N_DEV, d), lambda l: (0, 0)), pl.BlockSpec((None, N_DEV, n), lambda l: (l, 0, 0))],
        out_specs=pl.BlockSpec((None, d, n), lambda l: (l, 0, 0)), compiler_params=_cp(PAR),
    )(c_all, dmod)


def _adamw(slots, w, m, v, name):
    n, r, c = slots.shape
    tr = 256 if r % 256 == 0 else r
    bc1 = 1.0 - ADAM_B1 ** ADAM_STEP
    bc2 = 1.0 - ADAM_B2 ** ADAM_STEP

    def body(s_ref, w_ref, m_ref, v_ref, g_ref, d_ref, nm_ref, nv_ref):
        g = s_ref[0].astype(F32)
        for j in range(1, n):
            g = g + s_ref[j].astype(F32)
        m_new = ADAM_B1 * m_ref[...] + (1.0 - ADAM_B1) * g
        v_new = ADAM_B2 * v_ref[...] + (1.0 - ADAM_B2) * (g * g)
        g_ref[...] = g
        nm_ref[...] = m_new
        nv_ref[...] = v_new
        d_ref[...] = -ADAM_LR * ((m_new / bc1) / (jnp.sqrt(v_new / bc2) + ADAM_EPS) + ADAM_WD * w_ref[...])

    tile = pl.BlockSpec((tr, c), lambda i: (i, 0))
    shp = jax.ShapeDtypeStruct((r, c), F32)
    return _pcall(
        body, name=name, grid=(r // tr,), out_shape=[shp] * 4,
        in_specs=[pl.BlockSpec((n, tr, c), lambda i: (0, i, 0)), tile, tile, tile],
        out_specs=[tile] * 4, compiler_params=_cp(PAR),
    )(slots, w, m, v)


def _w_in_layout(w):
    pad = jnp.zeros(w.shape[:-1] + (IN_P - IN_W,), w.dtype)
    return jnp.concatenate([w[..., 768:FL_SRC], w[..., :768], w[..., UC_SRC:], w[..., FL_SRC:UC_SRC], pad], axis=-1)


def _w_in_unlayout(g):
    return jnp.concatenate([g[..., C_QA:C_UC], g[..., :C_QA], g[..., C_FL:C_FL + FOX_HEADS], g[..., C_UC:C_FL]], axis=-1)


SMALL = [("ada_b", DEPTH * 6 * D_MODEL), ("norm1_g", DEPTH * D_MODEL), ("norm2_g", DEPTH * D_MODEL),
         ("sgu_w", DEPTH * SGU_GROUPS * SGU_CHUNK * SGU_CHUNK), ("sgu_b", DEPTH * SGU_GROUPS * SGU_CHUNK),
         ("sgu_norm_g", DEPTH * SGU_GROUPS * HEAD_DIM), ("q_norm_g", DEPTH * HEAD_DIM), ("k_norm_g", DEPTH * HEAD_DIM),
         ("b_forget", DEPTH * FOX_HEADS), ("loss", 1)]
SMALL_ROWS = 2560


def _pack_small(parts):
    flat = jnp.concatenate([parts[name].reshape(-1).astype(F32) for name, _ in SMALL])
    return jnp.pad(flat, (0, SMALL_ROWS * 128 - flat.shape[0])).reshape(SMALL_ROWS, 128)


def _unpack_small(packed, shapes):
    flat = packed.reshape(-1)
    out, off = {}, 0
    for name, size in SMALL:
        out[name] = flat[off:off + size].reshape(shapes[name])
        off += size
    return out


def kernel(x, c, ada_w, ada_b, norm1_g, norm2_g, w_in, b_forget, q_norm_g, k_norm_g, sgu_norm_g, sgu_w, sgu_b, w_out, mlp_w1, mlp_w2, loss_target, m_ada_w, m_ada_b, m_norm1_g, m_norm2_g, m_w_in, m_b_forget, m_q_norm_g, m_k_norm_g, m_sgu_norm_g, m_sgu_w, m_sgu_b, m_w_out, m_mlp_w1, m_mlp_w2, v_ada_w, v_ada_b, v_norm1_g, v_norm2_g, v_w_in, v_b_forget, v_q_norm_g, v_k_norm_g, v_sgu_norm_g, v_sgu_w, v_sgu_b, v_w_out, v_mlp_w1, v_mlp_w2):
    me = _lin(_my_pos())
    x0 = x[0]
    target = loss_target[0]
    n_ada = ada_w.shape[2]

    shards = [w.astype(_XBF) for w in (w_in, w_out, mlp_w1, mlp_w2)]

    def whole_in(w_in_g):
        return _w_in_layout(jnp.transpose(w_in_g, (1, 0, 2)).reshape(D_MODEL, IN_W))

    def whole_rest(w_out_g, w1_g, w2_g):
        return (w_out_g.reshape(D_MODEL, D_MODEL), jnp.transpose(w1_g, (1, 0, 2)).reshape(D_MODEL, D_FF),
                w2_g.reshape(D_FF, D_MODEL))

    w_in_first, c_all = _exchange([shards[0][0], c], False, "gather_weights")
    c_all = c_all.reshape(N_DEV, D_MODEL)
    weights = [None] * DEPTH

    mod_part = _ada_fwd(c_all, ada_w, "ada_fwd")
    (mod_rows,) = _exchange([jnp.transpose(mod_part, (1, 0, 2))], True, "scatter_mod")
    mod = jnp.transpose(mod_rows, (1, 0, 2)).reshape(DEPTH, 6 * D_MODEL) + ada_b
    mods = mod.reshape(DEPTH, 6, 1, D_MODEL)

    saved = []
    xl = x0
    for l in range(DEPTH):
        sh1, sc1, g1, sh2, sc2, g2 = (mods[l, r] for r in range(6))
        n1 = norm1_g[l].reshape(1, D_MODEL)
        n2 = norm2_g[l].reshape(1, D_MODEL)
        gq = jnp.tile(q_norm_g[l], FOX_HEADS).reshape(1, 512)
        gk = jnp.tile(k_norm_g[l], FOX_HEADS).reshape(1, 512)
        bf = jnp.pad(b_forget[l], (0, 128 - FOX_HEADS)).reshape(1, 128)
        gs = sgu_norm_g[l].reshape(1, 256)
        bexp = jnp.repeat(sgu_b[l].T, HEAD_DIM, axis=1)

        if l == 0:
            w_in_l = whole_in(w_in_first)
            gather_rest = dict(arrays=[w[0] for w in shards[1:]], scatter=False, recv=None, slot=None)
        else:
            w_in_l, gather_rest = weights[l][0], None
        h1, h1_t = _norm_mod(xl, n1, sc1, sh1, "norm_mod")
        proj = _mm_nn(h1, w_in_l, "plain", (), "mm_in")
        o_sb, gathered = _sb_fwd(proj, "sb_fwd", gather_rest)
        if gathered:
            weights[0] = (w_in_l,) + whole_rest(*gathered)
        qn, kn, cfc, cfr, cfe = _fox_prep(proj, gq, gk, bf, "fox_prep")
        qkb = (1.02 * HEAD_DIM ** 0.5 * jnp.max(jnp.abs(q_norm_g[l])) * jnp.max(jnp.abs(k_norm_g[l]))).reshape(1, 1)
        gather_next = None
        if l + 1 < DEPTH:
            gather_next = dict(arrays=[w[l + 1] for w in shards], scatter=False, recv=None, slot=None)
        o_fox, lse, gathered = _fox_fwd(proj, qn, kn, cfc, cfr, cfe, qkb, "fox_fwd", gather_next)
        if gathered:
            weights[l + 1] = (whole_in(gathered[0]),) + whole_rest(*gathered[1:])
        _, w_out_l, w1_l, w2_l = weights[l]
        o_sgu = _sgu_fwd(proj, sgu_w[l], gs, bexp, "sgu_fwd")
        x_mid, y1, mixed_t = _mm_out((o_sb, o_fox, o_sgu), w_out_l, xl, g1, "mm_out")
        h2, h2_t = _norm_mod(x_mid, n2, sc2, sh2, "norm_mod")
        r2, r1, r2_t = _mm_nn(h2, w1_l, "relu2", (), "mm_w1")
        x_out, y2 = _mm_nn(r2, w2_l, "resid", (x_mid, g2), "mm_w2")
        saved.append(dict(x_in=xl, h1_t=h1_t, proj=proj, o_sb=o_sb, qn=qn, kn=kn, cfc=cfc, cfr=cfr, cfe=cfe, o_fox=o_fox,
                          lse=lse, mixed_t=mixed_t, x_mid=x_mid, y1=y1, h2_t=h2_t, r2_t=r2_t, r1=r1, y2=y2,
                          n1=n1, n2=n2, gq=gq, gk=gk, bf=bf, gs=gs, bexp=bexp, qkb=qkb))
        xl = x_out

    loss_part, dx, dy2, dg2 = _loss_head(xl, target, (saved[-1]["y2"], mods[DEPTH - 1, 5]), "loss_head")

    grads_ready = None
    received = [jnp.zeros((N_DEV,) + w.shape, _XBF) for w in shards]
    small = {k: [None] * DEPTH for k in ("mod", "norm1_g", "norm2_g", "sgu_w", "sgu_b", "sgu_norm_g",
                                         "q_norm_g", "k_norm_g", "b_forget")}
    for l in reversed(range(DEPTH)):
        sv = saved[l]
        sh1, sc1, g1, sh2, sc2, g2 = (mods[l, r] for r in range(6))
        w_in_l, w_out_l, w1_l, w2_l = weights[l]
        da = _mm_nt(dy2, w2_l, "mul2", (sv["r1"],), _BF, "mm_w2_bwd")
        g_w2 = _mm_grad(sv["r2_t"], dy2, "rows", "mm_w2_grad")
        dh2 = _mm_nt(da, w1_l, "plain", (), F32, "mm_w1_bwd")
        g_w1 = _mm_grad(sv["h2_t"], da, "cols", "mm_w1_grad")
        dx_mid, dsh2, dsc2, dn2, dy1, dg1 = _norm_mod_bwd(sv["x_mid"], dh2, dx, sv["n2"], sc2, (sv["y1"], g1),
                                                          "norm_mod_bwd")
        dmix = _mm_nt(dy1, w_out_l, "plain", (), F32, "mm_out_bwd")
        g_out = _mm_grad(sv["mixed_t"], dy1, "rows", "mm_out_grad")
        send_own = None
        if l == 0:
            send_own = dict(arrays=[g_out, g_w1, g_w2], scatter=True, recv=received[1:], slot=0)
        dq_sb, dk_sb, dv_sb, filled = _sb_bwd(sv["proj"], sv["o_sb"], dmix, "sb_bwd", send_own)
        if filled:
            received = received[:1] + filled
        send_prev = None
        if grads_ready is not None:
            send_prev = dict(arrays=grads_ready, scatter=True, recv=received, slot=l + 1)
        dqn, dkn, dv_fox, dlogf, filled = _fox_bwd(sv["proj"], sv["qn"], sv["kn"], sv["cfc"], sv["cfr"], sv["cfe"], sv["qkb"],
                                                 sv["o_fox"], sv["lse"], dmix, "fox_bwd", send_prev)
        if filled:
            received = filled
        dq_fox, dk_fox, dfl, dgq, dgk, dbf = _fox_prep_bwd(sv["proj"], sv["gq"], sv["gk"], sv["bf"], dqn, dkn, dlogf,
                                                           "fox_prep_bwd")
        duc, dvc, dsw, dsb, dsg = _sgu_bwd(sv["proj"], sgu_w[l], sv["gs"], sv["bexp"], dmix, "sgu_bwd")
        dproj = jnp.concatenate(
            [dq_fox, dk_fox, dv_fox.astype(_BF), dq_sb.astype(_BF), dk_sb.astype(_BF), dv_sb.astype(_BF), duc, dvc, dfl,
             jnp.zeros((dfl.shape[0], IN_P - C_FL - 128), _BF)], axis=-1)
        dh1 = _mm_nt(dproj, w_in_l, "plain", (), F32, "mm_in_bwd")
        g_in = _w_in_unlayout(_mm_grad(sv["h1_t"], dproj, None, "mm_in_grad"))
        g_in = jnp.transpose(g_in.reshape(D_MODEL, N_DEV, IN_W // N_DEV), (1, 0, 2))
        grads_ready = [g_in, g_out, g_w1, g_w2]
        small["mod"][l] = [dg1, dsh2, dsc2, dg2]
        below = (saved[l - 1]["y2"], mods[l - 1, 5]) if l > 0 else None
        dx, dsh1, dsc1, dn1, *below_grads = _norm_mod_bwd(sv["x_in"], dh1, dx_mid, sv["n1"], sc1, below, "norm_mod_bwd")
        if below_grads:
            dy2, dg2 = below_grads
        small["mod"][l] = jnp.concatenate([dsh1, dsc1] + small["mod"][l], axis=-1).reshape(-1)
        small["norm1_g"][l] = dn1.reshape(-1)
        small["norm2_g"][l] = dn2.reshape(-1)
        small["sgu_w"][l] = dsw
        small["sgu_b"][l] = dsb.reshape(SGU_CHUNK, SGU_GROUPS, HEAD_DIM).sum(-1).T
        small["sgu_norm_g"][l] = dsg.reshape(SGU_GROUPS, HEAD_DIM)
        small["q_norm_g"][l] = dgq.reshape(FOX_HEADS, HEAD_DIM).sum(0)
        small["k_norm_g"][l] = dgk.reshape(FOX_HEADS, HEAD_DIM).sum(0)
        small["b_forget"][l] = dbf[0, :FOX_HEADS]

    parts = {k: jnp.stack(v) for k, v in small.items()}
    parts["ada_b"] = parts.pop("mod")
    parts["loss"] = loss_part
    (small_all,) = _exchange([_pack_small(parts)], False, "gather_small")
    zero1 = jnp.zeros((1,), F32)
    packs = [_pack_small(dict(ada_b=a, norm1_g=b, norm2_g=cc, sgu_w=d, sgu_b=e, sgu_norm_g=f, q_norm_g=g, k_norm_g=h,
                              b_forget=i, loss=zero1))
             for a, b, cc, d, e, f, g, h, i in (
                 (ada_b, norm1_g, norm2_g, sgu_w, sgu_b, sgu_norm_g, q_norm_g, k_norm_g, b_forget),
                 (m_ada_b, m_norm1_g, m_norm2_g, m_sgu_w, m_sgu_b, m_sgu_norm_g, m_q_norm_g, m_k_norm_g, m_b_forget),
                 (v_ada_b, v_norm1_g, v_norm2_g, v_sgu_w, v_sgu_b, v_sgu_norm_g, v_q_norm_g, v_k_norm_g, v_b_forget))]
    small_shapes = dict(ada_b=ada_b.shape, norm1_g=norm1_g.shape, norm2_g=norm2_g.shape, sgu_w=sgu_w.shape,
                        sgu_b=sgu_b.shape, sgu_norm_g=sgu_norm_g.shape, q_norm_g=q_norm_g.shape,
                        k_norm_g=k_norm_g.shape, b_forget=b_forget.shape, loss=())
    s_out = [_unpack_small(a, small_shapes) for a in _adamw(small_all, *packs, "adamw_small")]

    dmod_all = small_all[:, :DEPTH * 6 * D_MODEL // 128, :].reshape(N_DEV, DEPTH, 6 * D_MODEL)
    dmod_mine = lax.dynamic_slice_in_dim(dmod_all, me * n_ada, n_ada, axis=2)
    g_ada = _ada_bwd(c_all, jnp.transpose(dmod_mine, (1, 0, 2)), "ada_bwd")
    ada_out = _adamw(g_ada.reshape(1, DEPTH * D_MODEL, n_ada), ada_w.reshape(-1, n_ada), m_ada_w.reshape(-1, n_ada),
                     v_ada_w.reshape(-1, n_ada), "adamw_ada")
    ada_out = [a.reshape(ada_w.shape) for a in ada_out]

    (ri,) = _exchange(grads_ready[:1], True, "scatter_grads", recv=received[:1], slot=0)
    _, ro, r1, r2 = received

    def big(slots, w, m, v, name):
        cdim = w.shape[-1]
        outs = _adamw(slots.reshape(N_DEV, -1, cdim), w.reshape(-1, cdim), m.reshape(-1, cdim), v.reshape(-1, cdim), name)
        return [a.reshape(w.shape) for a in outs]

    in_out = big(ri, w_in, m_w_in, v_w_in, "adamw_in")
    out_out = big(ro, w_out, m_w_out, v_w_out, "adamw_out")
    w1_out = big(r1, mlp_w1, m_mlp_w1, v_mlp_w1, "adamw_w1")
    w2_out = big(r2, mlp_w2, m_mlp_w2, v_mlp_w2, "adamw_w2")

    def leaves(idx):
        sm = s_out[idx]
        return [ada_out[idx], sm["ada_b"], sm["norm1_g"], sm["norm2_g"], in_out[idx], sm["b_forget"], sm["q_norm_g"],
                sm["k_norm_g"], sm["sgu_norm_g"], sm["sgu_w"], sm["sgu_b"], out_out[idx], w1_out[idx], w2_out[idx]]

    loss = s_out[0]["loss"]
    grad_x = dx.reshape(x.shape)
    return (loss, grad_x, *leaves(0), *leaves(1), *leaves(2), *leaves(3))
```

```python
import jax
import jax.numpy as jnp
from jax import lax
from jax.experimental import pallas as pl
from jax.experimental.pallas import tpu as pltpu

F32 = jnp.float32
_BF = jnp.bfloat16
_XBF = jnp.bfloat16

N_DEV = 8
D_MODEL = 1024
DEPTH = 4
HEAD_DIM = 64
SB_HEADS = 4
FOX_HEADS = 8
SGU_GROUPS = 4
SGU_CHUNK = 128
D_FF = 4096
EPS = 1e-6
IN_W = 2824
FL_SRC = 2304
UC_SRC = 2312

C_QB, C_KB, C_VB = 0, 512, 1024
C_QA, C_KA, C_VA = 1536, 1792, 2048
C_UC, C_VC = 2304, 2560
C_FL = 2816
IN_P = 3072
M_SB, M_FOX, M_SGU = 0, 256, 768

ADAM_LR = 0.001
ADAM_B1 = 0.9
ADAM_B2 = 0.999
ADAM_EPS = 1e-08
ADAM_WD = 0.01
ADAM_STEP = 10

ATT_T = 256
ROW_T = 512
MM_T = 1024
GRAD_TS = 2048
ROW_CHUNK = 64
NEG = -1e30
UNDERFLOW = -90.0
NO_BLOCK = 1e30

MESH = pl.DeviceIdType.MESH
ARB = pltpu.ARBITRARY
PAR = pltpu.PARALLEL
HIGHEST = lax.Precision.HIGHEST


def _pcall(body, **kw):
    return pl.pallas_call(body, **kw)


def _cp(*sem):
    return pltpu.CompilerParams(dimension_semantics=tuple(sem))


def _dot(a, b, ca, cb):
    return lax.dot_general(a, b, (((ca,), (cb,)), ((), ())), preferred_element_type=F32)


def _dot_f32(a, b, ca, cb):
    return lax.dot_general(a, b, (((ca,), (cb,)), ((), ())), precision=HIGHEST, preferred_element_type=F32)


def _split(v, parts):
    out = []
    r = v
    for _ in range(parts - 1):
        p = r.astype(_BF)
        out.append(p)
        r = r - p.astype(F32)
    out.append(r.astype(_BF))
    return out


def _dot_exact(v, m, ca, cb, parts, v_left=True):
    out = None
    for p in _split(v, parts):
        term = _dot(p, m, ca, cb) if v_left else _dot(m, p, ca, cb)
        out = term if out is None else out + term
    return out


def _iota2(shape, dim):
    return lax.broadcasted_iota(jnp.int32, shape, dim)


def _my_pos():
    return lax.axis_index("x"), lax.axis_index("y"), lax.axis_index("c")


def _flip(pos, k):
    x, y, c = pos
    px = 1 - x if (k >> 2) & 1 else x
    py = 1 - y if (k >> 1) & 1 else y
    pc = 1 - c if k & 1 else c
    return px, py, pc


def _lin(pos):
    return 4 * pos[0] + 2 * pos[1] + pos[2]


def _exchange_ops(ins, outs, sems, scatter, slot):
    send_sems, recv_sems, local_sems = sems
    n = len(ins)

    def copies(with_recvs):
        me = _my_pos()
        me_i = _lin(me)
        dst = lambda a, j: outs[a].at[j] if slot is None else outs[a].at[j, slot]
        mine = lambda a: ins[a].at[me_i] if scatter else ins[a]
        local = [pltpu.make_async_copy(mine(a), dst(a, me_i), local_sems.at[a]) for a in range(n)]
        sends, recvs = [], []
        for a in range(n):
            for k in range(1, N_DEV):
                peer = _flip(me, k)
                pair = dict(send_sem=send_sems.at[a, k - 1], recv_sem=recv_sems.at[a, k - 1],
                            device_id=peer, device_id_type=MESH)
                src = ins[a].at[_lin(peer)] if scatter else ins[a]
                sends.append(pltpu.make_async_remote_copy(src_ref=src, dst_ref=dst(a, me_i), **pair))
                if with_recvs:
                    recvs.append(pltpu.make_async_remote_copy(src_ref=mine(a), dst_ref=dst(a, _lin(peer)), **pair))
        return local, sends, recvs

    def start():
        local, sends, _ = copies(False)
        for cp in local + sends:
            cp.start()

    def wait():
        local, sends, recvs = copies(True)
        for snd, rcv in zip(sends, recvs):
            snd.wait_send()
            rcv.wait_recv()
        for cp in local:
            cp.wait()

    return start, wait


def _comm_operands(comm):
    arrays, recv = comm["arrays"], comm["recv"]
    n = len(arrays)
    any_spec = pl.BlockSpec(memory_space=pl.ANY)
    if recv is not None:
        out_shapes = [jax.ShapeDtypeStruct(r.shape, r.dtype) for r in recv]
    elif comm["scatter"]:
        out_shapes = [jax.ShapeDtypeStruct(a.shape, a.dtype) for a in arrays]
    else:
        out_shapes = [jax.ShapeDtypeStruct((N_DEV,) + a.shape, a.dtype) for a in arrays]
    operands = list(arrays) + (list(recv) if recv is not None else [])
    sems = [pltpu.SemaphoreType.DMA((n, N_DEV - 1)), pltpu.SemaphoreType.DMA((n, N_DEV - 1)),
            pltpu.SemaphoreType.DMA((n,))]
    return operands, [any_spec] * len(operands), out_shapes, [any_spec] * n, sems


def _with_comm(body, comm, n_in, n_out, first_step, last_step, n_scratch=0):
    if comm is None:
        return body
    n = len(comm["arrays"])
    n_cin = n if comm["recv"] is None else 2 * n

    def wrapped(*refs):
        ins, cins = refs[:n_in], refs[n_in:n_in + n]
        o0 = n_in + n_cin
        outs, couts = refs[o0:o0 + n_out], refs[o0 + n_out:o0 + n_out + n]
        s0 = o0 + n_out + n
        scratch, sems = refs[s0:s0 + n_scratch], refs[s0 + n_scratch:]
        start, wait = _exchange_ops(cins, couts, sems, comm["scatter"], comm["slot"])
        pl.when(first_step())(start)
        body(*ins, *outs, *scratch)
        pl.when(last_step())(wait)

    return wrapped


def _comm_aliases(comm, n_in, n_out):
    if comm is None or comm["recv"] is None:
        return {}
    n = len(comm["arrays"])
    return {n_in + n + a: n_out + a for a in range(n)}


def _exchange(arrays, scatter, name, recv=None, slot=None):
    comm = dict(arrays=arrays, scatter=scatter, recv=recv, slot=slot)
    operands, in_specs, out_shapes, out_specs, sems = _comm_operands(comm)
    n = len(arrays)

    def body(*refs):
        n_cin = len(operands)
        start, wait = _exchange_ops(refs[:n], refs[n_cin:n_cin + n], refs[n_cin + n:], scatter, slot)
        start()
        wait()

    outs = _pcall(body, name=name, out_shape=out_shapes, in_specs=in_specs, out_specs=out_specs, scratch_shapes=sems,
                  input_output_aliases=_comm_aliases(comm, 0, 0))(*operands)
    return list(outs)


def _row_spec(t, w, col=0):
    return pl.BlockSpec((t, w), lambda i: (i, col))


def _vec_spec(w, col=0):
    return pl.BlockSpec((1, w), lambda i: (0, col))


def _norm_mod(x, g, sc, sh, name):
    s, d = x.shape
    t = min(ROW_T, s)

    def body(x_ref, g_ref, sc_ref, sh_ref, h_ref, ht_ref):
        xf = x_ref[...]
        rstd = lax.rsqrt(jnp.mean(xf * xf, axis=-1, keepdims=True) + EPS)
        y = xf * rstd * g_ref[...]
        h = y * (1.0 + sc_ref[...]) + sh_ref[...]
        h_ref[...] = h.astype(h_ref.dtype)
        ht_ref[...] = h.T.astype(ht_ref.dtype)

    return _pcall(
        body, name=name, grid=(s // t,),
        out_shape=[jax.ShapeDtypeStruct((s, d), _BF), jax.ShapeDtypeStruct((d, s), _BF)],
        in_specs=[_row_spec(t, d), _vec_spec(d), _vec_spec(d), _vec_spec(d)],
        out_specs=[_row_spec(t, d), pl.BlockSpec((d, t), lambda i: (0, i))], compiler_params=_cp(PAR),
    )(x, g, sc, sh)


def _res_bwd_step(i, dx, y_ref, gate_ref, dy_ref, dgate_ref):
    dy_ref[...] = (dx * gate_ref[...]).astype(dy_ref.dtype)

    @pl.when(i == 0)
    def _():
        dgate_ref[...] = jnp.zeros_like(dgate_ref)

    dgate_ref[...] += jnp.sum(dx * y_ref[...], axis=0, keepdims=True)


def _norm_mod_bwd(x, dh, dres, g, sc, branch, name):
    s, d = x.shape
    t = min(ROW_T, s)
    n_branch = 0 if branch is None else 2

    ch = min(ROW_CHUNK, t)

    def body(x_ref, dh_ref, dres_ref, g_ref, sc_ref, *rest):
        dx_ref, dsh_ref, dsc_ref, dg_ref = rest[n_branch:n_branch + 4]
        i = pl.program_id(0)
        gv = g_ref[...]
        sc1 = 1.0 + sc_ref[...]
        gain = gv * sc1
        n_sum = 2 if branch is None else 3

        def rows(r, sums):
            sl = pl.ds(pl.multiple_of(r * ch, ch), ch)
            xf = x_ref[sl, :]
            dh = dh_ref[sl, :]
            rstd = lax.rsqrt(jnp.mean(xf * xf, axis=-1, keepdims=True) + EPS)
            xhat = xf * rstd
            dh_xhat = dh * xhat
            dxh = dh * gain
            dx = dres_ref[sl, :] + rstd * (dxh - xhat * jnp.mean(dh_xhat * gain, axis=-1, keepdims=True))
            dx_ref[sl, :] = dx
            new = [sums[0] + jnp.sum(dh, axis=0, keepdims=True), sums[1] + jnp.sum(dh_xhat, axis=0, keepdims=True)]
            if branch is not None:
                rest[n_branch + 4][sl, :] = (dx * rest[1][...]).astype(_BF)
                new.append(sums[2] + jnp.sum(dx * rest[0][sl, :], axis=0, keepdims=True))
            return tuple(new)

        sums = lax.fori_loop(0, t // ch, rows, tuple(jnp.zeros((1, d), F32) for _ in range(n_sum)))
        parts = [sums[0], sums[1] * gv, sums[1] * sc1] + list(sums[2:])
        totals = [dsh_ref, dsc_ref, dg_ref] + ([rest[n_branch + 5]] if branch is not None else [])

        @pl.when(i == 0)
        def _():
            for ref in totals:
                ref[...] = jnp.zeros_like(ref)

        for ref, part in zip(totals, parts):
            ref[...] += part

    vec = jax.ShapeDtypeStruct((1, d), F32)
    out_shape = [jax.ShapeDtypeStruct((s, d), F32), vec, vec, vec]
    in_specs = [_row_spec(t, d), _row_spec(t, d), _row_spec(t, d), _vec_spec(d), _vec_spec(d)]
    out_specs = [_row_spec(t, d), _vec_spec(d), _vec_spec(d), _vec_spec(d)]
    if branch is not None:
        in_specs += [_row_spec(t, d), _vec_spec(d)]
        out_shape += [jax.ShapeDtypeStruct((s, d), _BF), vec]
        out_specs += [_row_spec(t, d), _vec_spec(d)]
    return _pcall(
        body, name=name, grid=(s // t,), out_shape=out_shape, in_specs=in_specs, out_specs=out_specs,
        compiler_params=_cp(ARB),
    )(x, dh, dres, g, sc, *(branch or ()))


def _loss_head(y, target, branch, name):
    s, d = y.shape
    t = min(ROW_T, s)

    def body(y_ref, t_ref, yb_ref, gate_ref, loss_ref, dy_ref, dyb_ref, dgate_ref):
        i = pl.program_id(0)
        diff = y_ref[...] - t_ref[...]
        dy = diff * (1.0 / d)
        dy_ref[...] = dy

        @pl.when(i == 0)
        def _():
            loss_ref[...] = jnp.zeros_like(loss_ref)

        rows = jnp.sum(diff * diff, axis=-1, keepdims=True)
        loss_ref[...] += (0.5 / d) * jnp.sum(rows, axis=0, keepdims=True)
        _res_bwd_step(i, dy, yb_ref, gate_ref, dyb_ref, dgate_ref)

    return _pcall(
        body, name=name, grid=(s // t,),
        out_shape=[jax.ShapeDtypeStruct((1, 1), F32), jax.ShapeDtypeStruct((s, d), F32),
                   jax.ShapeDtypeStruct((s, d), _BF), jax.ShapeDtypeStruct((1, d), F32)],
        in_specs=[_row_spec(t, d), _row_spec(t, d), _row_spec(t, d), _vec_spec(d)],
        out_specs=[pl.BlockSpec((1, 1), lambda i: (0, 0)), _row_spec(t, d), _row_spec(t, d), _vec_spec(d)],
        compiler_params=_cp(ARB),
    )(y, target, *branch)


def _mm_nn(a, b, epi, extras, name):
    m, kdim = a.shape
    n = b.shape[1]
    tm, tn, tk = min(MM_T, m), min(MM_T, n), min(MM_T, kdim)
    nk = kdim // tk
    n_extra = len(extras)
    n_out = {"plain": 1, "resid": 2, "relu2": 3}[epi]

    def finish(y, extra_refs, out_refs):
        if epi == "plain":
            out_refs[0][...] = y.astype(out_refs[0].dtype)
        elif epi == "resid":
            x_ref, g_ref = extra_refs
            out_refs[0][...] = x_ref[...] + g_ref[...] * y
            out_refs[1][...] = y
        else:
            r = jnp.maximum(y, 0.0)
            out_refs[0][...] = (r * r).astype(out_refs[0].dtype)
            out_refs[1][...] = r.astype(out_refs[1].dtype)
            out_refs[2][...] = (r * r).T.astype(out_refs[2].dtype)

    def body(a_ref, b_ref, *rest):
        extra_refs = rest[:n_extra]
        out_refs = rest[n_extra:n_extra + n_out]
        part = _dot(a_ref[...].astype(_BF), b_ref[...].astype(_BF), 1, 0)
        if nk == 1:
            finish(part, extra_refs, out_refs)
        else:
            acc_ref = rest[-1]
            k = pl.program_id(2)

            @pl.when(k == 0)
            def _():
                acc_ref[...] = part

            @pl.when(k > 0)
            def _():
                acc_ref[...] += part

            @pl.when(k == nk - 1)
            def _():
                finish(acc_ref[...], extra_refs, out_refs)

    tile = pl.BlockSpec((tm, tn), lambda i, j, k: (i, j))
    in_specs = [pl.BlockSpec((tm, tk), lambda i, j, k: (i, k)), pl.BlockSpec((tk, tn), lambda i, j, k: (k, j))]
    if epi == "plain":
        out_shape = [jax.ShapeDtypeStruct((m, n), F32)]
    elif epi == "resid":
        in_specs += [tile, pl.BlockSpec((1, tn), lambda i, j, k: (0, j))]
        out_shape = [jax.ShapeDtypeStruct((m, n), F32)] * 2
    out_specs = [tile] * n_out
    if epi == "relu2":
        out_shape = [jax.ShapeDtypeStruct((m, n), _BF), jax.ShapeDtypeStruct((m, n), F32),
                     jax.ShapeDtypeStruct((n, m), _BF)]
        out_specs[2] = pl.BlockSpec((tn, tm), lambda i, j, k: (j, i))
    outs = _pcall(
        body, name=name, grid=(m // tm, n // tn, nk), out_shape=out_shape,
        in_specs=in_specs, out_specs=out_specs,
        scratch_shapes=[pltpu.VMEM((tm, tn), F32)] if nk > 1 else [],
        compiler_params=_cp(PAR, PAR, ARB),
    )(a, b, *extras)
    return outs[0] if n_out == 1 else outs


def _mm_out(parts, w, x, gate, name):
    s, n = x.shape
    widths = [p.shape[1] for p in parts]
    kdim = sum(widths)
    tm = min(ROW_T, s)

    def body(a0_ref, a1_ref, a2_ref, w_ref, x_ref, g_ref, xo_ref, y_ref, mt_ref):
        mixed = jnp.concatenate([a0_ref[...], a1_ref[...], a2_ref[...]], axis=-1)
        y = _dot(mixed.astype(_BF), w_ref[...], 1, 0)
        xo_ref[...] = x_ref[...] + g_ref[...] * y
        y_ref[...] = y
        mt_ref[...] = mixed.T.astype(mt_ref.dtype)

    row = lambda width: pl.BlockSpec((tm, width), lambda i: (i, 0))
    return _pcall(
        body, name=name, grid=(s // tm,),
        out_shape=[jax.ShapeDtypeStruct((s, n), F32), jax.ShapeDtypeStruct((s, n), F32),
                   jax.ShapeDtypeStruct((kdim, s), _BF)],
        in_specs=[row(widths[0]), row(widths[1]), row(widths[2]), pl.BlockSpec((kdim, n), lambda i: (0, 0)),
                  row(n), pl.BlockSpec((1, n), lambda i: (0, 0))],
        out_specs=[row(n), row(n), pl.BlockSpec((kdim, tm), lambda i: (0, i))], compiler_params=_cp(PAR),
    )(*parts, w, x, gate)


def _mm_nt(a, b, epi, extras, out_dtype, name):
    m, kdim = a.shape
    n = b.shape[0]
    tm, tn, tk = min(MM_T, m), min(MM_T, n), min(MM_T, kdim)
    nk = kdim // tk
    n_extra = len(extras)

    def finish(y, extra_refs, o_ref):
        if epi == "mul2":
            y = y * (2.0 * extra_refs[0][...].astype(F32))
        o_ref[...] = y.astype(o_ref.dtype)

    def body(a_ref, b_ref, *rest):
        extra_refs = rest[:n_extra]
        o_ref = rest[n_extra]
        part = _dot(a_ref[...].astype(_BF), b_ref[...].astype(_BF), 1, 1)
        if nk == 1:
            finish(part, extra_refs, o_ref)
        else:
            acc_ref = rest[-1]
            k = pl.program_id(2)

            @pl.when(k == 0)
            def _():
                acc_ref[...] = part

            @pl.when(k > 0)
            def _():
                acc_ref[...] += part

            @pl.when(k == nk - 1)
            def _():
                finish(acc_ref[...], extra_refs, o_ref)

    tile = pl.BlockSpec((tm, tn), lambda i, j, k: (i, j))
    in_specs = [pl.BlockSpec((tm, tk), lambda i, j, k: (i, k)), pl.BlockSpec((tn, tk), lambda i, j, k: (j, k))]
    in_specs += [tile] * n_extra
    return _pcall(
        body, name=name, grid=(m // tm, n // tn, nk), out_shape=jax.ShapeDtypeStruct((m, n), out_dtype),
        in_specs=in_specs, out_specs=tile,
        scratch_shapes=[pltpu.VMEM((tm, tn), F32)] if nk > 1 else [],
        compiler_params=_cp(PAR, PAR, ARB),
    )(a, b, *extras)


def _mm_grad(at, b, split, name):
    m, s = at.shape
    n = b.shape[1]
    ts = min(GRAD_TS, s)
    ns = s // ts
    if split == "rows":
        tm, tn = min(MM_T, m // N_DEV), min(MM_T, n)
        per = (m // N_DEV) // tm
        out_shape = (N_DEV, m // N_DEV, n)
        out_spec = pl.BlockSpec((None, tm, tn), lambda i, j, k: (i // per, i % per, j))
    elif split == "cols":
        tm, tn = min(MM_T, m), min(MM_T, n // N_DEV)
        per = (n // N_DEV) // tn
        out_shape = (N_DEV, m, n // N_DEV)
        out_spec = pl.BlockSpec((None, tm, tn), lambda i, j, k: (j // per, i, j % per))
    else:
        tm, tn = min(MM_T, m), min(MM_T, n)
        out_shape = (m, n)
        out_spec = pl.BlockSpec((tm, tn), lambda i, j, k: (i, j))

    def body(a_ref, b_ref, o_ref, acc_ref):
        k = pl.program_id(2)
        part = _dot(a_ref[...].astype(_BF), b_ref[...].astype(_BF), 1, 0)

        @pl.when(k == 0)
        def _():
            acc_ref[...] = part

        @pl.when(k > 0)
        def _():
            acc_ref[...] += part

        @pl.when(k == ns - 1)
        def _():
            o_ref[...] = acc_ref[...].astype(o_ref.dtype)

    return _pcall(
        body, name=name, grid=(m // tm, n // tn, ns), out_shape=jax.ShapeDtypeStruct(out_shape, _XBF),
        in_specs=[pl.BlockSpec((tm, ts), lambda i, j, k: (i, k)), pl.BlockSpec((ts, tn), lambda i, j, k: (k, j))],
        out_specs=out_spec, scratch_shapes=[pltpu.VMEM((tm, tn), F32)],
        compiler_params=_cp(PAR, PAR, ARB),
    )(at, b)


def _head_masks():
    lane = _iota2((1, 128), 1)
    return [(lane // HEAD_DIM) == hh for hh in range(2)]


def _untranspose(src_ref, dst_ref, t, nblk):
    def rows(c, _):
        off = pl.multiple_of(c * t, t)
        dst_ref[pl.ds(off, t), :] = src_ref[:, pl.ds(off, t)].T
        return 0

    lax.fori_loop(0, nblk, rows, 0)


def _softplus_parts(z):
    sp = jnp.maximum(z, 0.0) + jnp.log(1.0 + jnp.exp(-jnp.abs(z)))
    return -sp, z - sp


def _sb_key_loop(i, carry, step, lookahead, first_ahead):
    def cond(c):
        return jnp.logical_and(c[0] <= i, jnp.max(jnp.maximum(c[1][0][0], c[1][0][1])) > UNDERFLOW)

    def body(c):
        ahead = lookahead(i - c[0] - 1)
        return c[0] + 1, step(i - c[0], c[1], c[2]), ahead

    return lax.while_loop(cond, body, (jnp.int32(1), carry, first_ahead))[1]


def _sb_fwd(proj, name, comm=None):
    s = proj.shape[0]
    t = min(ATT_T, s)
    nblk = s // t
    scale = HEAD_DIM ** -0.5

    def body(q_ref, k_ref, v_ref, o_ref):
        i = pl.program_id(1)
        masks = _head_masks()
        row = _iota2((t, t), 0)
        col = _iota2((t, t), 1)
        u_strict = (row > col).astype(_BF)
        causal = col < row
        qs = q_ref[...] * scale
        qm = [jnp.where(masks[hh], qs, 0.0).astype(_BF) for hh in range(2)]

        def scores(j):
            kb = k_ref[pl.ds(pl.multiple_of(jnp.maximum(j, 0) * t, t), t), :].astype(_BF)
            return tuple(_dot(qm[hh], kb, 1, 1) for hh in range(2))

        def block(j, carry, qk, diag):
            off = pl.multiple_of(j * t, t)
            vb = v_ref[pl.ds(off, t), :].astype(_BF)
            runs, outs = [], []
            for hh in range(2):
                run, o = carry[0][hh], carry[1][hh]
                z = qk[hh]
                l1, lb = _softplus_parts(z)
                if diag:
                    l1 = jnp.where(causal, l1, 0.0)
                between = run + _dot_exact(l1, u_strict, 1, 0, 2)
                a = jnp.exp(lb + between)
                if diag:
                    a = jnp.where(causal, a, 0.0)
                outs.append(o + _dot_exact(a, vb, 1, 0, 2))
                runs.append(run + jnp.sum(l1, axis=-1, keepdims=True))
            return tuple(runs), tuple(outs)

        zero, zero_o = jnp.zeros((t, 1), F32), jnp.zeros((t, 128), F32)
        ahead = scores(i - 1)
        carry = block(i, ((zero, zero), (zero_o, zero_o)), scores(i), True)
        carry = _sb_key_loop(i, carry, lambda j, c, qk: block(j, c, qk, False), scores, ahead)
        o_ref[...] = jnp.where(masks[0], carry[1][0], carry[1][1])

    kv_spec = lambda c0: pl.BlockSpec((s, 128), lambda p, i: (0, c0 // 128 + p))
    c_operands, c_in_specs, c_out_shapes, c_out_specs, c_sems = _comm_operands(comm) if comm else ([], [], [], [], [])
    outs = _pcall(
        _with_comm(body, comm, 3, 1, *_grid_ends(SB_HEADS // 2, nblk)), name=name, grid=(SB_HEADS // 2, nblk),
        out_shape=[jax.ShapeDtypeStruct((s, 256), F32)] + c_out_shapes,
        in_specs=[pl.BlockSpec((t, 128), lambda p, i: (i, C_QA // 128 + p)), kv_spec(C_KA), kv_spec(C_VA)] + c_in_specs,
        out_specs=[pl.BlockSpec((t, 128), lambda p, i: (i, p))] + c_out_specs,
        scratch_shapes=c_sems, input_output_aliases=_comm_aliases(comm, 3, 1), compiler_params=_cp(ARB, ARB),
    )(proj, proj, proj, *c_operands)
    return outs[0], list(outs[1:])


def _sb_bwd(proj, o, dmix, name, comm=None):
    s = proj.shape[0]
    t = min(ATT_T, s)
    nblk = s // t
    scale = HEAD_DIM ** -0.5

    def body(q_ref, k_ref, v_ref, o_ref, do_ref, dq_ref, dk_ref, dv_ref, dkt_ref, dvt_ref):
        i = pl.program_id(1)
        masks = _head_masks()
        row = _iota2((t, t), 0)
        col = _iota2((t, t), 1)
        u_strict = (row > col).astype(_BF)
        u_incl = (row >= col).astype(_BF)
        causal = col < row

        @pl.when(i == 0)
        def _():
            dkt_ref[...] = jnp.zeros_like(dkt_ref)
            dvt_ref[...] = jnp.zeros_like(dvt_ref)

        qs = q_ref[...] * scale
        dov = do_ref[...]
        ov = o_ref[...]
        qm_f = [jnp.where(masks[hh], qs, 0.0) for hh in range(2)]
        dom_f = [jnp.where(masks[hh], dov, 0.0) for hh in range(2)]
        qm = [qm_f[hh].astype(_BF) for hh in range(2)]
        dom = [dom_f[hh].astype(_BF) for hh in range(2)]
        qm_t = [qm_f[hh].T.astype(_BF) for hh in range(2)]
        dom_t = [dom_f[hh].T.astype(_BF) for hh in range(2)]
        dsum = [jnp.sum(dom[hh].astype(F32) * ov, axis=-1, keepdims=True) for hh in range(2)]

        def scores(j):
            off = pl.multiple_of(jnp.maximum(j, 0) * t, t)
            kb = k_ref[pl.ds(off, t), :].astype(_BF)
            vb = v_ref[pl.ds(off, t), :].astype(_BF)
            return tuple((_dot(qm[hh], kb, 1, 1), _dot(dom[hh], vb, 1, 1)) for hh in range(2))

        def block(j, carry, ahead, diag):
            off = pl.multiple_of(j * t, t)
            kb = k_ref[pl.ds(off, t), :].astype(_BF)
            runs, eruns, dqs = [], [], []
            dk = dv = None
            for hh in range(2):
                run, erun, dq = carry[0][hh], carry[1][hh], carry[2][hh]
                z, da = ahead[hh]
                l1, lb = _softplus_parts(z)
                if diag:
                    l1 = jnp.where(causal, l1, 0.0)
                between = run + _dot_exact(l1, u_strict, 1, 0, 2)
                a = jnp.exp(lb + between)
                if diag:
                    a = jnp.where(causal, a, 0.0)
                g = a * da
                cum = dsum[hh] - (erun + _dot_exact(g, u_incl, 1, 0, 2))
                beta = jnp.exp(lb)
                dz = g * (1.0 - beta) - cum * beta
                if diag:
                    dz = jnp.where(causal, dz, 0.0)
                dzb = dz.astype(_BF)
                dqs.append(dq + _dot(dzb, kb, 1, 0))
                dk_h = _dot(qm_t[hh], dzb, 1, 0)
                dv_h = _dot(dom_t[hh], a.astype(_BF), 1, 0)
                dk = dk_h if dk is None else dk + dk_h
                dv = dv_h if dv is None else dv + dv_h
                runs.append(run + jnp.sum(l1, axis=-1, keepdims=True))
                eruns.append(erun + jnp.sum(g, axis=-1, keepdims=True))
            dkt_ref[:, pl.ds(off, t)] += dk
            dvt_ref[:, pl.ds(off, t)] += dv
            return tuple(runs), tuple(eruns), tuple(dqs)

        zero, zero_q = jnp.zeros((t, 1), F32), jnp.zeros((t, 128), F32)
        ahead = scores(i - 1)
        carry = block(i, ((zero, zero), (zero, zero), (zero_q, zero_q)), scores(i), True)
        carry = _sb_key_loop(i, carry, lambda j, c, a: block(j, c, a, False), scores, ahead)
        dq_ref[...] = jnp.where(masks[0], carry[2][0], carry[2][1]) * scale

        @pl.when(i == nblk - 1)
        def _():
            _untranspose(dkt_ref, dk_ref, t, nblk)
            _untranspose(dvt_ref, dv_ref, t, nblk)

    kv_spec = lambda c0: pl.BlockSpec((s, 128), lambda p, i: (0, c0 // 128 + p))
    blk = lambda c0: pl.BlockSpec((t, 128), lambda p, i: (i, c0 // 128 + p))
    acc = pl.BlockSpec((s, 128), lambda p, i: (0, p))
    shp = jax.ShapeDtypeStruct((s, 256), F32)
    c_operands, c_in_specs, c_out_shapes, c_out_specs, c_sems = _comm_operands(comm) if comm else ([], [], [], [], [])
    outs = _pcall(
        _with_comm(body, comm, 5, 3, *_grid_ends(SB_HEADS // 2, nblk), n_scratch=2), name=name,
        grid=(SB_HEADS // 2, nblk), out_shape=[shp, shp, shp] + c_out_shapes,
        in_specs=[blk(C_QA), kv_spec(C_KA), kv_spec(C_VA), blk(0), blk(M_SB)] + c_in_specs,
        out_specs=[blk(0), acc, acc] + c_out_specs,
        scratch_shapes=[pltpu.VMEM((128, s), F32), pltpu.VMEM((128, s), F32)] + c_sems,
        input_output_aliases=_comm_aliases(comm, 5, 3), compiler_params=_cp(ARB, ARB),
    )(proj, proj, proj, o, dmix, *c_operands)
    return outs[0], outs[1], outs[2], list(outs[3:])


def _group_matrix(width):
    r = _iota2((width, width), 0) // HEAD_DIM
    c = _iota2((width, width), 1) // HEAD_DIM
    return (r == c).astype(_BF)


def _group_mean(v, gm):
    return _dot_exact(v, gm, 1, 0, 2) * (1.0 / HEAD_DIM)


def _fox_prep(proj, gq, gk, bf, name):
    s = proj.shape[0]
    t = min(ATT_T, s)
    nblk = s // t
    assert nblk <= 128

    def body(q_ref, k_ref, fl_ref, gq_ref, gk_ref, bf_ref, qn_ref, kn_ref, cfc_ref, cfr_ref, cfe_ref, carry_ref):
        i = pl.program_id(0)
        gm = _group_matrix(512)
        for src, gref, dst in ((q_ref, gq_ref, qn_ref), (k_ref, gk_ref, kn_ref)):
            v = src[...]
            rstd = lax.rsqrt(_group_mean(v * v, gm) + EPS)
            dst[...] = (v * rstd * gref[...]).astype(dst.dtype)

        @pl.when(i == 0)
        def _():
            carry_ref[...] = jnp.zeros_like(carry_ref)
            cfe_ref[...] = jnp.full(cfe_ref.shape, NO_BLOCK, F32)

        lane = _iota2((1, 128), 1)
        logf = jnp.where(lane < FOX_HEADS, _softplus_parts(-(fl_ref[...] + bf_ref[...]))[0], 0.0)
        lower = (_iota2((t, t), 0) >= _iota2((t, t), 1)).astype(_BF)
        cf = carry_ref[...] + _dot_exact(logf, lower, 1, 0, 3, v_left=False)
        cfc_ref[...] = cf
        rows = cf.T[:FOX_HEADS, :]
        cfr_ref[...] = rows
        cfe_ref[...] = jnp.where(lane == i, rows[:, t - 1:t], cfe_ref[...])
        carry_ref[...] = cf[t - 1:t, :]

    col = lambda w, c0: pl.BlockSpec((t, w), lambda i: (i, c0 // w))
    v512 = pl.BlockSpec((1, 512), lambda i: (0, 0))
    return _pcall(
        body, name=name, grid=(nblk,),
        out_shape=[jax.ShapeDtypeStruct((s, 512), _BF), jax.ShapeDtypeStruct((s, 512), _BF),
                   jax.ShapeDtypeStruct((s, 128), F32), jax.ShapeDtypeStruct((FOX_HEADS, s), F32),
                   jax.ShapeDtypeStruct((FOX_HEADS, 128), F32)],
        in_specs=[col(512, C_QB), col(512, C_KB), col(128, C_FL), v512, v512, pl.BlockSpec((1, 128), lambda i: (0, 0))],
        out_specs=[col(512, 0), col(512, 0), col(128, 0), pl.BlockSpec((FOX_HEADS, t), lambda i: (0, i)),
                   pl.BlockSpec((FOX_HEADS, 128), lambda i: (0, 0))],
        scratch_shapes=[pltpu.VMEM((1, 128), F32)], compiler_params=_cp(ARB),
    )(proj, proj, proj, gq, gk, bf)


def _fox_cq(cfc_blk, head):
    lane = _iota2((1, 128), 1)
    return jnp.sum(jnp.where(lane == head, cfc_blk, 0.0), axis=-1, keepdims=True)


def _fox_ck(cfr_ref, off, t, head):
    sub = _iota2((FOX_HEADS, 1), 0)
    return jnp.sum(jnp.where(sub == head, cfr_ref[:, pl.ds(off, t)], 0.0), axis=0, keepdims=True)


def _fox_blocks_needed(top, cfe_ref, head, i):
    sub = _iota2((FOX_HEADS, 1), 0)
    lane = _iota2((1, 128), 1)
    last_key = jnp.sum(jnp.where(sub == head, cfe_ref[...], 0.0), axis=0, keepdims=True)
    need = jnp.logical_and(lane < i, top - last_key > UNDERFLOW)
    return jnp.sum(need.astype(jnp.int32))


def _grid_ends(n_outer, n_inner):
    first = lambda: jnp.logical_and(pl.program_id(0) == 0, pl.program_id(1) == 0)
    last = lambda: jnp.logical_and(pl.program_id(0) == n_outer - 1, pl.program_id(1) == n_inner - 1)
    return first, last


def _fox_fwd(proj, qn, kn, cfc, cfr, cfe, qk_bound, name, comm=None):
    s = proj.shape[0]
    t = min(ATT_T, s)
    nblk = s // t
    scale = HEAD_DIM ** -0.5

    def body(q_ref, k_ref, v_ref, cfc_ref, cfr_ref, cfe_ref, qkb_ref, o_ref, lse_ref):
        p = pl.program_id(0)
        i = pl.program_id(1)
        masks = _head_masks()
        lane = _iota2((1, 128), 1)
        valid = _iota2((t, t), 1) <= _iota2((t, t), 0)
        qv = q_ref[...]
        cfc_blk = cfc_ref[...]
        heads = [2 * p + hh for hh in range(2)]
        qm = [(jnp.where(masks[hh], qv, 0) * scale).astype(_BF) for hh in range(2)]
        cq = [_fox_cq(cfc_blk, heads[hh]) for hh in range(2)]

        def scores(j):
            kb = k_ref[pl.ds(pl.multiple_of(jnp.maximum(j, 0) * t, t), t), :]
            return tuple(_dot(qm[hh], kb, 1, 1) for hh in range(2))

        def block(j, carry, qk, diag):
            off = pl.multiple_of(j * t, t)
            vb = v_ref[pl.ds(off, t), :].astype(_BF)
            ms, ls, accs = [], [], []
            for hh in range(2):
                m, l, acc = carry[0][hh], carry[1][hh], carry[2][hh]
                z = qk[hh] + (cq[hh] - _fox_ck(cfr_ref, off, t, heads[hh]))
                if diag:
                    z = jnp.where(valid, z, NEG)
                m_new = jnp.maximum(m, jnp.max(z, axis=-1, keepdims=True))
                pe = jnp.exp(z - m_new)
                alpha = jnp.exp(m - m_new)
                ms.append(m_new)
                ls.append(alpha * l + jnp.sum(pe, axis=-1, keepdims=True))
                accs.append(alpha * acc + _dot(pe.astype(_BF), vb, 1, 0))
            return tuple(ms), tuple(ls), tuple(accs)

        def step(jj, c):
            ahead = scores(i - jj - 2)
            return block(i - jj - 1, c[0], c[1], False), ahead

        neg, zero, zero_o = jnp.full((t, 1), NEG, F32), jnp.zeros((t, 1), F32), jnp.zeros((t, 128), F32)
        ahead = scores(i - 1)
        carry = block(i, ((neg, neg), (zero, zero), (zero_o, zero_o)), scores(i), True)
        needed = [_fox_blocks_needed(jnp.max(qkb_ref[...] + cq[hh] - carry[0][hh]), cfe_ref, heads[hh], i)
                  for hh in range(2)]
        m, l, acc = lax.fori_loop(0, jnp.maximum(needed[0], needed[1]), step, (carry, ahead))[0]
        o_ref[...] = jnp.where(masks[0], acc[0] / l[0], acc[1] / l[1])
        lse_ref[...] = jnp.where(lane == 0, m[0] + jnp.log(l[0]), jnp.where(lane == 1, m[1] + jnp.log(l[1]), 0.0))

    c_operands, c_in_specs, c_out_shapes, c_out_specs, c_sems = _comm_operands(comm) if comm else ([], [], [], [], [])
    outs = _pcall(
        _with_comm(body, comm, 7, 2, *_grid_ends(FOX_HEADS // 2, nblk)), name=name, grid=(FOX_HEADS // 2, nblk),
        out_shape=[jax.ShapeDtypeStruct((s, 512), F32), jax.ShapeDtypeStruct((FOX_HEADS // 2, s, 128), F32)] + c_out_shapes,
        in_specs=[pl.BlockSpec((t, 128), lambda p, i: (i, p)), pl.BlockSpec((s, 128), lambda p, i: (0, p)),
                  pl.BlockSpec((s, 128), lambda p, i: (0, C_VB // 128 + p)),
                  pl.BlockSpec((t, 128), lambda p, i: (i, 0)), pl.BlockSpec((FOX_HEADS, s), lambda p, i: (0, 0)),
                  pl.BlockSpec((FOX_HEADS, 128), lambda p, i: (0, 0)), pl.BlockSpec((1, 1), lambda p, i: (0, 0))] + c_in_specs,
        out_specs=[pl.BlockSpec((t, 128), lambda p, i: (i, p)),
                   pl.BlockSpec((None, t, 128), lambda p, i: (p, i, 0))] + c_out_specs,
        scratch_shapes=c_sems, input_output_aliases=_comm_aliases(comm, 7, 2),
        compiler_params=_cp(ARB, ARB),
    )(qn, kn, proj, cfc, cfr, cfe, qk_bound, *c_operands)
    return outs[0], outs[1], list(outs[2:])


def _fox_bwd(proj, qn, kn, cfc, cfr, cfe, qk_bound, o, lse, dmix, name, comm=None):
    s = proj.shape[0]
    t = min(ATT_T, s)
    nblk = s // t
    scale = HEAD_DIM ** -0.5

    def body(q_ref, k_ref, v_ref, cfc_ref, cfr_ref, cfe_ref, qkb_ref, o_ref, lse_ref, do_ref,
             dq_ref, dk_ref, dv_ref, dcr_ref, dkt_ref, dvt_ref):
        p = pl.program_id(0)
        i = pl.program_id(1)
        masks = _head_masks()
        lane = _iota2((1, 128), 1)
        valid = _iota2((t, t), 1) <= _iota2((t, t), 0)
        before = (_iota2((t, t), 0) < _iota2((t, t), 1)).astype(_BF)

        @pl.when(i == 0)
        def _():
            dkt_ref[...] = jnp.zeros_like(dkt_ref)
            dvt_ref[...] = jnp.zeros_like(dvt_ref)

        @pl.when(jnp.logical_and(p == 0, i == 0))
        def _():
            dcr_ref[...] = jnp.zeros_like(dcr_ref)

        head_row = _iota2((FOX_HEADS, 1), 0)
        qv = q_ref[...]
        cfc_blk = cfc_ref[...]
        dov = do_ref[...]
        ov = o_ref[...]
        lse_blk = lse_ref[...]
        heads = [2 * p + hh for hh in range(2)]
        qm = [(jnp.where(masks[hh], qv, 0) * scale).astype(_BF) for hh in range(2)]
        dom = [jnp.where(masks[hh], dov, 0.0).astype(_BF) for hh in range(2)]
        qm_t = [qm[hh].astype(F32).T.astype(_BF) for hh in range(2)]
        dom_t = [dom[hh].astype(F32).T.astype(_BF) for hh in range(2)]
        dsum = [jnp.sum(dom[hh].astype(F32) * ov, axis=-1, keepdims=True) for hh in range(2)]
        lse_h = [jnp.sum(jnp.where(lane == hh, lse_blk, 0.0), axis=-1, keepdims=True) for hh in range(2)]
        cq = [_fox_cq(cfc_blk, heads[hh]) for hh in range(2)]
        needed = [_fox_blocks_needed(jnp.max(qkb_ref[...] + cq[hh] - lse_h[hh]), cfe_ref, heads[hh], i)
                  for hh in range(2)]
        first = i - jnp.maximum(needed[0], needed[1])

        def scores(j):
            off = pl.multiple_of(j * t, t)
            kb = k_ref[pl.ds(off, t), :]
            vb = v_ref[pl.ds(off, t), :].astype(_BF)
            return tuple((_dot(qm[hh], kb, 1, 1), _dot(dom[hh], vb, 1, 1)) for hh in range(2))

        def block(j, carry, ahead, diag):
            off = pl.multiple_of(j * t, t)
            kb = k_ref[pl.ds(off, t), :]
            dqs, pres = [], []
            dk = dv = None
            for hh in range(2):
                dq, pre = carry[0][hh], carry[1][hh]
                z = ahead[hh][0] + (cq[hh] - _fox_ck(cfr_ref, off, t, heads[hh]))
                pm = jnp.exp(z - lse_h[hh])
                if diag:
                    pm = jnp.where(valid, pm, 0.0)
                ds = pm * (ahead[hh][1] - dsum[hh])
                dsb = ds.astype(_BF)
                dqs.append(dq + _dot(dsb, kb, 1, 0))
                dk_h = _dot(qm_t[hh], dsb, 1, 0)
                dv_h = _dot(dom_t[hh], pm.astype(_BF), 1, 0)
                dk = dk_h if dk is None else dk + dk_h
                dv = dv_h if dv is None else dv + dv_h
                if diag:
                    lower_keys = jnp.where(valid, pre + _dot_exact(ds, before, 1, 0, 2), 0.0)
                    dlogf = jnp.sum(lower_keys, axis=0, keepdims=True)
                else:
                    dlogf = jnp.sum(pre) + _dot_exact(jnp.sum(ds, axis=0, keepdims=True), before, 1, 0, 3)
                dcr_ref[:, pl.ds(off, t)] += jnp.where(head_row == heads[hh], dlogf, 0.0)
                pres.append(pre + jnp.sum(ds, axis=-1, keepdims=True))
            dkt_ref[:, pl.ds(off, t)] += dk
            dvt_ref[:, pl.ds(off, t)] += dv
            return tuple(dqs), tuple(pres)

        zero, zero_q = jnp.zeros((t, 1), F32), jnp.zeros((t, 128), F32)

        def step(j, c):
            ahead = scores(j + 1)
            return block(j, c[0], c[1], False), ahead

        carry, ahead = lax.fori_loop(first, i, step, (((zero_q, zero_q), (zero, zero)), scores(first)))
        dqs, _ = block(i, carry, ahead, True)
        dq_ref[...] = jnp.where(masks[0], dqs[0], dqs[1]) * scale

        @pl.when(i == nblk - 1)
        def _():
            _untranspose(dkt_ref, dk_ref, t, nblk)
            _untranspose(dvt_ref, dv_ref, t, nblk)

    blk = lambda c0: pl.BlockSpec((t, 128), lambda p, i: (i, c0 // 128 + p))
    res = lambda c0: pl.BlockSpec((s, 128), lambda p, i: (0, c0 // 128 + p))
    shp = jax.ShapeDtypeStruct((s, 512), F32)
    c_operands, c_in_specs, c_out_shapes, c_out_specs, c_sems = _comm_operands(comm) if comm else ([], [], [], [], [])
    transposed = [pltpu.VMEM((128, s), F32), pltpu.VMEM((128, s), F32)]
    outs = _pcall(
        _with_comm(body, comm, 10, 4, *_grid_ends(FOX_HEADS // 2, nblk), n_scratch=2), name=name,
        grid=(FOX_HEADS // 2, nblk),
        out_shape=[shp, shp, shp, jax.ShapeDtypeStruct((FOX_HEADS, s), F32)] + c_out_shapes,
        in_specs=[blk(0), res(0), res(C_VB), pl.BlockSpec((t, 128), lambda p, i: (i, 0)),
                  pl.BlockSpec((FOX_HEADS, s), lambda p, i: (0, 0)), pl.BlockSpec((FOX_HEADS, 128), lambda p, i: (0, 0)),
                  pl.BlockSpec((1, 1), lambda p, i: (0, 0)), blk(0),
                  pl.BlockSpec((None, t, 128), lambda p, i: (p, i, 0)), blk(M_FOX)] + c_in_specs,
        out_specs=[blk(0), res(0), res(0), pl.BlockSpec((FOX_HEADS, s), lambda p, i: (0, 0))] + c_out_specs,
        scratch_shapes=transposed + c_sems, input_output_aliases=_comm_aliases(comm, 10, 4),
        compiler_params=_cp(ARB, ARB),
    )(qn, kn, proj, cfc, cfr, cfe, qk_bound, o, lse, dmix, *c_operands)
    return outs[0], outs[1], outs[2], outs[3], list(outs[4:])


def _fox_prep_bwd(proj, gq, gk, bf, dqn, dkn, dlogf, name):
    s = proj.shape[0]
    t = min(ATT_T, s)
    nblk = s // t

    def body(q_ref, k_ref, fl_ref, gq_ref, gk_ref, bf_ref, dqn_ref, dkn_ref, dlogf_ref,
             dq_ref, dk_ref, dfl_ref, dgq_ref, dgk_ref, dbf_ref):
        i = pl.program_id(0)
        gm = _group_matrix(512)

        @pl.when(i == 0)
        def _():
            dgq_ref[...] = jnp.zeros_like(dgq_ref)
            dgk_ref[...] = jnp.zeros_like(dgk_ref)
            dbf_ref[...] = jnp.zeros_like(dbf_ref)

        for src, gref, dyref, dst, dgref in ((q_ref, gq_ref, dqn_ref, dq_ref, dgq_ref),
                                              (k_ref, gk_ref, dkn_ref, dk_ref, dgk_ref)):
            v = src[...]
            dy = dyref[...]
            rstd = lax.rsqrt(_group_mean(v * v, gm) + EPS)
            xhat = v * rstd
            dgref[...] += jnp.sum(dy * xhat, axis=0, keepdims=True)
            dxh = dy * gref[...]
            dst[...] = (rstd * (dxh - xhat * _group_mean(dxh * xhat, gm))).astype(dst.dtype)

        to_lanes = (_iota2((FOX_HEADS, 128), 0) == _iota2((FOX_HEADS, 128), 1)).astype(_BF)
        dlogf = _dot_exact(dlogf_ref[...], to_lanes, 0, 0, 3)
        xv = fl_ref[...] + bf_ref[...]
        e = jnp.exp(-jnp.abs(xv))
        dfl = dlogf * (jnp.where(xv >= 0.0, e, 1.0) / (1.0 + e))
        dfl_ref[...] = dfl.astype(dfl_ref.dtype)
        dbf_ref[...] += jnp.sum(dfl, axis=0, keepdims=True)

    col = lambda w, c0: pl.BlockSpec((t, w), lambda i: (i, c0 // w))
    v512 = pl.BlockSpec((1, 512), lambda i: (0, 0))
    v128 = pl.BlockSpec((1, 128), lambda i: (0, 0))
    return _pcall(
        body, name=name, grid=(nblk,),
        out_shape=[jax.ShapeDtypeStruct((s, 512), _BF), jax.ShapeDtypeStruct((s, 512), _BF),
                   jax.ShapeDtypeStruct((s, 128), _BF), jax.ShapeDtypeStruct((1, 512), F32),
                   jax.ShapeDtypeStruct((1, 512), F32), jax.ShapeDtypeStruct((1, 128), F32)],
        in_specs=[col(512, C_QB), col(512, C_KB), col(128, C_FL), v512, v512, v128, col(512, 0), col(512, 0),
                  pl.BlockSpec((FOX_HEADS, t), lambda i: (0, i))],
        out_specs=[col(512, 0), col(512, 0), col(128, 0), v512, v512, v128], compiler_params=_cp(ARB),
    )(proj, proj, proj, gq, gk, bf, dqn, dkn, dlogf)


GELU_C = 0.7978845608028654
GELU_A = 0.044715


def _gelu(x):
    return 0.5 * x * (1.0 + jnp.tanh(GELU_C * (x + GELU_A * x * x * x)))


def _gelu_grad(x):
    th = jnp.tanh(GELU_C * (x + GELU_A * x * x * x))
    return 0.5 * (1.0 + th) + 0.5 * x * (1.0 - th * th) * (GELU_C * (1.0 + 3.0 * GELU_A * x * x))


def _sgu_tril():
    return _iota2((SGU_CHUNK, SGU_CHUNK), 0) >= _iota2((SGU_CHUNK, SGU_CHUNK), 1)


def _sgu_group_masks():
    lane = _iota2((1, 256), 1)
    return [(lane // HEAD_DIM) == g for g in range(SGU_GROUPS)]


def _sgu_fwd(proj, w, gs, bexp, name):
    s = proj.shape[0]
    t = min(2 * SGU_CHUNK, s)

    def body(u_ref, v_ref, w_ref, gs_ref, b_ref, o_ref):
        gm = _group_matrix(256)
        gmask = _sgu_group_masks()
        tril = _sgu_tril()
        u = _gelu(u_ref[...])
        vg = _gelu(v_ref[...])
        vhat = (vg * lax.rsqrt(_group_mean(vg * vg, gm) + EPS) * gs_ref[...]).astype(_BF)
        for ch in range(t // SGU_CHUNK):
            rows = slice(ch * SGU_CHUNK, (ch + 1) * SGU_CHUNK)
            mixed = b_ref[...]
            for g in range(SGU_GROUPS):
                wg = jnp.where(tril, w_ref[g], 0.0).astype(_BF)
                mixed = jnp.where(gmask[g], mixed + _dot(wg, vhat[rows], 1, 0), mixed)
            o_ref[rows, :] = u[rows] * mixed

    return _pcall(
        body, name=name, grid=(s // t,), out_shape=jax.ShapeDtypeStruct((s, 256), F32),
        in_specs=[_row_spec(t, 256, C_UC // 256), _row_spec(t, 256, C_VC // 256),
                  pl.BlockSpec((SGU_GROUPS, SGU_CHUNK, SGU_CHUNK), lambda i: (0, 0, 0)), _vec_spec(256),
                  pl.BlockSpec((SGU_CHUNK, 256), lambda i: (0, 0))],
        out_specs=_row_spec(t, 256), compiler_params=_cp(PAR),
    )(proj, proj, w, gs, bexp)


def _sgu_bwd(proj, w, gs, bexp, dmix, name):
    s = proj.shape[0]
    t = min(2 * SGU_CHUNK, s)
    nstep = s // t

    def body(u_ref, v_ref, w_ref, gs_ref, b_ref, do_ref, du_ref, dv_ref, dw_ref, db_ref, dgs_ref):
        i = pl.program_id(0)
        gm = _group_matrix(256)
        gmask = _sgu_group_masks()
        tril = _sgu_tril()

        @pl.when(i == 0)
        def _():
            dw_ref[...] = jnp.zeros_like(dw_ref)
            db_ref[...] = jnp.zeros_like(db_ref)
            dgs_ref[...] = jnp.zeros_like(dgs_ref)

        uc = u_ref[...]
        vc = v_ref[...]
        u = _gelu(uc)
        vg = _gelu(vc)
        rstd = lax.rsqrt(_group_mean(vg * vg, gm) + EPS)
        xh = vg * rstd
        gsv = gs_ref[...]
        vhat = (xh * gsv).astype(_BF)
        dov = do_ref[...]
        dm = dov * u
        for ch in range(t // SGU_CHUNK):
            rows = slice(ch * SGU_CHUNK, (ch + 1) * SGU_CHUNK)
            mixed = b_ref[...]
            dvh = jnp.zeros((SGU_CHUNK, 256), F32)
            dmc = dm[rows]
            for g in range(SGU_GROUPS):
                wg = jnp.where(tril, w_ref[g], 0.0).astype(_BF)
                mixed = jnp.where(gmask[g], mixed + _dot(wg, vhat[rows], 1, 0), mixed)
                dvh = jnp.where(gmask[g], _dot(wg, dmc.astype(_BF), 0, 0), dvh)
                dw_ref[g] += _dot(jnp.where(gmask[g], dmc, 0.0).astype(_BF), vhat[rows], 1, 1)
            db_ref[...] += dmc
            du_ref[rows, :] = (dov[rows] * mixed * _gelu_grad(uc[rows])).astype(du_ref.dtype)
            xhc = xh[rows]
            dgs_ref[...] += jnp.sum(dvh * xhc, axis=0, keepdims=True)
            dxh = dvh * gsv
            dvg = rstd[rows] * (dxh - xhc * _group_mean(dxh * xhc, gm))
            dv_ref[rows, :] = (dvg * _gelu_grad(vc[rows])).astype(dv_ref.dtype)

        @pl.when(i == nstep - 1)
        def _():
            for g in range(SGU_GROUPS):
                dw_ref[g] = jnp.where(tril, dw_ref[g], 0.0)

    wspec = pl.BlockSpec((SGU_GROUPS, SGU_CHUNK, SGU_CHUNK), lambda i: (0, 0, 0))
    bspec = pl.BlockSpec((SGU_CHUNK, 256), lambda i: (0, 0))
    return _pcall(
        body, name=name, grid=(nstep,),
        out_shape=[jax.ShapeDtypeStruct((s, 256), _BF), jax.ShapeDtypeStruct((s, 256), _BF),
                   jax.ShapeDtypeStruct((SGU_GROUPS, SGU_CHUNK, SGU_CHUNK), F32),
                   jax.ShapeDtypeStruct((SGU_CHUNK, 256), F32), jax.ShapeDtypeStruct((1, 256), F32)],
        in_specs=[_row_spec(t, 256, C_UC // 256), _row_spec(t, 256, C_VC // 256), wspec, _vec_spec(256), bspec,
                  _row_spec(t, 256, M_SGU // 256)],
        out_specs=[_row_spec(t, 256), _row_spec(t, 256), wspec, bspec, _vec_spec(256)],
        compiler_params=_cp(ARB),
    )(proj, proj, w, gs, bexp, dmix)


def _ada_fwd(c_all, ada_w, name):
    depth, d, n = ada_w.shape

    def body(c_ref, w_ref, o_ref):
        cv = c_ref[...]
        cond = cv / (1.0 + jnp.exp(-cv))
        o_ref[...] = _dot_f32(cond, w_ref[...], 1, 0)

    return _pcall(
        body, name=name, grid=(depth,), out_shape=jax.ShapeDtypeStruct((depth, N_DEV, n), F32),
        in_specs=[pl.BlockSpec((N_DEV, d), lambda l: (0, 0)), pl.BlockSpec((None, d, n), lambda l: (l, 0, 0))],
        out_specs=pl.BlockSpec((None, N_DEV, n), lambda l: (l, 0, 0)), compiler_params=_cp(PAR),
    )(c_all, ada_w)


def _ada_bwd(c_all, dmod, name):
    depth, _, n = dmod.shape
    d = c_all.shape[1]

    def body(c_ref, dm_ref, o_ref):
        cv = c_ref[...]
        cond = cv / (1.0 + jnp.exp(-cv))
        o_ref[...] = _dot_f32(cond, dm_ref[...], 0, 0)

    return _pcall(
        body, name=name, grid=(depth,), out_shape=jax.ShapeDtypeStruct((depth, d, n), F32),
        in_specs=[pl.BlockSpec((N_DEV, d), lambda l: (0, 0)), pl.BlockSpec((None, N_DEV, n), lambda l: (l, 0, 0))],
        out_specs=pl.BlockSpec((None, d, n), lambda l: (l, 0, 0)), compiler_params=_cp(PAR),
    )(c_all, dmod)


def _adamw(slots, w, m, v, name):
    n, r, c = slots.shape
    tr = 256 if r % 256 == 0 else r
    bc1 = 1.0 - ADAM_B1 ** ADAM_STEP
    bc2 = 1.0 - ADAM_B2 ** ADAM_STEP

    def body(s_ref, w_ref, m_ref, v_ref, g_ref, d_ref, nm_ref, nv_ref):
        g = s_ref[0].astype(F32)
        for j in range(1, n):
            g = g + s_ref[j].astype(F32)
        m_new = ADAM_B1 * m_ref[...] + (1.0 - ADAM_B1) * g
        v_new = ADAM_B2 * v_ref[...] + (1.0 - ADAM_B2) * (g * g)
        g_ref[...] = g
        nm_ref[...] = m_new
        nv_ref[...] = v_new
        d_ref[...] = -ADAM_LR * ((m_new / bc1) / (jnp.sqrt(v_new / bc2) + ADAM_EPS) + ADAM_WD * w_ref[...])

    tile = pl.BlockSpec((tr, c), lambda i: (i, 0))
    shp = jax.ShapeDtypeStruct((r, c), F32)
    return _pcall(
        body, name=name, grid=(r // tr,), out_shape=[shp] * 4,
        in_specs=[pl.BlockSpec((n, tr, c), lambda i: (0, i, 0)), tile, tile, tile],
        out_specs=[tile] * 4, compiler_params=_cp(PAR),
    )(slots, w, m, v)


def _w_in_layout(w):
    pad = jnp.zeros(w.shape[:-1] + (IN_P - IN_W,), w.dtype)
    return jnp.concatenate([w[..., 768:FL_SRC], w[..., :768], w[..., UC_SRC:], w[..., FL_SRC:UC_SRC], pad], axis=-1)


def _w_in_unlayout(g):
    return jnp.concatenate([g[..., C_QA:C_UC], g[..., :C_QA], g[..., C_FL:C_FL + FOX_HEADS], g[..., C_UC:C_FL]], axis=-1)


SMALL = [("ada_b", DEPTH * 6 * D_MODEL), ("norm1_g", DEPTH * D_MODEL), ("norm2_g", DEPTH * D_MODEL),
         ("sgu_w", DEPTH * SGU_GROUPS * SGU_CHUNK * SGU_CHUNK), ("sgu_b", DEPTH * SGU_GROUPS * SGU_CHUNK),
         ("sgu_norm_g", DEPTH * SGU_GROUPS * HEAD_DIM), ("q_norm_g", DEPTH * HEAD_DIM), ("k_norm_g", DEPTH * HEAD_DIM),
         ("b_forget", DEPTH * FOX_HEADS), ("loss", 1)]
SMALL_ROWS = 2560


def _pack_small(parts):
    flat = jnp.concatenate([parts[name].reshape(-1).astype(F32) for name, _ in SMALL])
    return jnp.pad(flat, (0, SMALL_ROWS * 128 - flat.shape[0])).reshape(SMALL_ROWS, 128)


def _unpack_small(packed, shapes):
    flat = packed.reshape(-1)
    out, off = {}, 0
    for name, size in SMALL:
        out[name] = flat[off:off + size].reshape(shapes[name])
        off += size
    return out


def kernel(x, c, ada_w, ada_b, norm1_g, norm2_g, w_in, b_forget, q_norm_g, k_norm_g, sgu_norm_g, sgu_w, sgu_b, w_out, mlp_w1, mlp_w2, loss_target, m_ada_w, m_ada_b, m_norm1_g, m_norm2_g, m_w_in, m_b_forget, m_q_norm_g, m_k_norm_g, m_sgu_norm_g, m_sgu_w, m_sgu_b, m_w_out, m_mlp_w1, m_mlp_w2, v_ada_w, v_ada_b, v_norm1_g, v_norm2_g, v_w_in, v_b_forget, v_q_norm_g, v_k_norm_g, v_sgu_norm_g, v_sgu_w, v_sgu_b, v_w_out, v_mlp_w1, v_mlp_w2):
    me = _lin(_my_pos())
    x0 = x[0]
    target = loss_target[0]
    n_ada = ada_w.shape[2]

    shards = [w.astype(_XBF) for w in (w_in, w_out, mlp_w1, mlp_w2)]

    def whole_in(w_in_g):
        return _w_in_layout(jnp.transpose(w_in_g, (1, 0, 2)).reshape(D_MODEL, IN_W))

    def whole_rest(w_out_g, w1_g, w2_g):
        return (w_out_g.reshape(D_MODEL, D_MODEL), jnp.transpose(w1_g, (1, 0, 2)).reshape(D_MODEL, D_FF),
                w2_g.reshape(D_FF, D_MODEL))

    w_in_first, c_all = _exchange([shards[0][0], c], False, "gather_weights")
    c_all = c_all.reshape(N_DEV, D_MODEL)
    weights = [None] * DEPTH

    mod_part = _ada_fwd(c_all, ada_w, "ada_fwd")
    (mod_rows,) = _exchange([jnp.transpose(mod_part, (1, 0, 2))], True, "scatter_mod")
    mod = jnp.transpose(mod_rows, (1, 0, 2)).reshape(DEPTH, 6 * D_MODEL) + ada_b
    mods = mod.reshape(DEPTH, 6, 1, D_MODEL)

    saved = []
    xl = x0
    for l in range(DEPTH):
        sh1, sc1, g1, sh2, sc2, g2 = (mods[l, r] for r in range(6))
        n1 = norm1_g[l].reshape(1, D_MODEL)
        n2 = norm2_g[l].reshape(1, D_MODEL)
        gq = jnp.tile(q_norm_g[l], FOX_HEADS).reshape(1, 512)
        gk = jnp.tile(k_norm_g[l], FOX_HEADS).reshape(1, 512)
        bf = jnp.pad(b_forget[l], (0, 128 - FOX_HEADS)).reshape(1, 128)
        gs = sgu_norm_g[l].reshape(1, 256)
        bexp = jnp.repeat(sgu_b[l].T, HEAD_DIM, axis=1)

        if l == 0:
            w_in_l = whole_in(w_in_first)
            gather_rest = dict(arrays=[w[0] for w in shards[1:]], scatter=False, recv=None, slot=None)
        else:
            w_in_l, gather_rest = weights[l][0], None
        h1, h1_t = _norm_mod(xl, n1, sc1, sh1, "norm_mod")
        proj = _mm_nn(h1, w_in_l, "plain", (), "mm_in")
        o_sb, gathered = _sb_fwd(proj, "sb_fwd", gather_rest)
        if gathered:
            weights[0] = (w_in_l,) + whole_rest(*gathered)
        qn, kn, cfc, cfr, cfe = _fox_prep(proj, gq, gk, bf, "fox_prep")
        qkb = (1.02 * HEAD_DIM ** 0.5 * jnp.max(jnp.abs(q_norm_g[l])) * jnp.max(jnp.abs(k_norm_g[l]))).reshape(1, 1)
        gather_next = None
        if l + 1 < DEPTH:
            gather_next = dict(arrays=[w[l + 1] for w in shards], scatter=False, recv=None, slot=None)
        o_fox, lse, gathered = _fox_fwd(proj, qn, kn, cfc, cfr, cfe, qkb, "fox_fwd", gather_next)
        if gathered:
            weights[l + 1] = (whole_in(gathered[0]),) + whole_rest(*gathered[1:])
        _, w_out_l, w1_l, w2_l = weights[l]
        o_sgu = _sgu_fwd(proj, sgu_w[l], gs, bexp, "sgu_fwd")
        x_mid, y1, mixed_t = _mm_out((o_sb, o_fox, o_sgu), w_out_l, xl, g1, "mm_out")
        h2, h2_t = _norm_mod(x_mid, n2, sc2, sh2, "norm_mod")
        r2, r1, r2_t = _mm_nn(h2, w1_l, "relu2", (), "mm_w1")
        x_out, y2 = _mm_nn(r2, w2_l, "resid", (x_mid, g2), "mm_w2")
        saved.append(dict(x_in=xl, h1_t=h1_t, proj=proj, o_sb=o_sb, qn=qn, kn=kn, cfc=cfc, cfr=cfr, cfe=cfe, o_fox=o_fox,
                          lse=lse, mixed_t=mixed_t, x_mid=x_mid, y1=y1, h2_t=h2_t, r2_t=r2_t, r1=r1, y2=y2,
                          n1=n1, n2=n2, gq=gq, gk=gk, bf=bf, gs=gs, bexp=bexp, qkb=qkb))
        xl = x_out

    loss_part, dx, dy2, dg2 = _loss_head(xl, target, (saved[-1]["y2"], mods[DEPTH - 1, 5]), "loss_head")

    grads_ready = None
    received = [jnp.zeros((N_DEV,) + w.shape, _XBF) for w in shards]
    small = {k: [None] * DEPTH for k in ("mod", "norm1_g", "norm2_g", "sgu_w", "sgu_b", "sgu_norm_g",
                                         "q_norm_g", "k_norm_g", "b_forget")}
    for l in reversed(range(DEPTH)):
        sv = saved[l]
        sh1, sc1, g1, sh2, sc2, g2 = (mods[l, r] for r in range(6))
        w_in_l, w_out_l, w1_l, w2_l = weights[l]
        da = _mm_nt(dy2, w2_l, "mul2", (sv["r1"],), _BF, "mm_w2_bwd")
        g_w2 = _mm_grad(sv["r2_t"], dy2, "rows", "mm_w2_grad")
        dh2 = _mm_nt(da, w1_l, "plain", (), F32, "mm_w1_bwd")
        g_w1 = _mm_grad(sv["h2_t"], da, "cols", "mm_w1_grad")
        dx_mid, dsh2, dsc2, dn2, dy1, dg1 = _norm_mod_bwd(sv["x_mid"], dh2, dx, sv["n2"], sc2, (sv["y1"], g1),
                                                          "norm_mod_bwd")
        dmix = _mm_nt(dy1, w_out_l, "plain", (), F32, "mm_out_bwd")
        g_out = _mm_grad(sv["mixed_t"], dy1, "rows", "mm_out_grad")
        send_own = None
        if l == 0:
            send_own = dict(arrays=[g_out, g_w1, g_w2], scatter=True, recv=received[1:], slot=0)
        dq_sb, dk_sb, dv_sb, filled = _sb_bwd(sv["proj"], sv["o_sb"], dmix, "sb_bwd", send_own)
        if filled:
            received = received[:1] + filled
        send_prev = None
        if grads_ready is not None:
            send_prev = dict(arrays=grads_ready, scatter=True, recv=received, slot=l + 1)
        dqn, dkn, dv_fox, dlogf, filled = _fox_bwd(sv["proj"], sv["qn"], sv["kn"], sv["cfc"], sv["cfr"], sv["cfe"], sv["qkb"],
                                                 sv["o_fox"], sv["lse"], dmix, "fox_bwd", send_prev)
        if filled:
            received = filled
        dq_fox, dk_fox, dfl, dgq, dgk, dbf = _fox_prep_bwd(sv["proj"], sv["gq"], sv["gk"], sv["bf"], dqn, dkn, dlogf,
                                                           "fox_prep_bwd")
        duc, dvc, dsw, dsb, dsg = _sgu_bwd(sv["proj"], sgu_w[l], sv["gs"], sv["bexp"], dmix, "sgu_bwd")
        dproj = jnp.concatenate(
            [dq_fox, dk_fox, dv_fox.astype(_BF), dq_sb.astype(_BF), dk_sb.astype(_BF), dv_sb.astype(_BF), duc, dvc, dfl,
             jnp.zeros((dfl.shape[0], IN_P - C_FL - 128), _BF)], axis=-1)
        dh1 = _mm_nt(dproj, w_in_l, "plain", (), F32, "mm_in_bwd")
        g_in = _w_in_unlayout(_mm_grad(sv["h1_t"], dproj, None, "mm_in_grad"))
        g_in = jnp.transpose(g_in.reshape(D_MODEL, N_DEV, IN_W // N_DEV), (1, 0, 2))
        grads_ready = [g_in, g_out, g_w1, g_w2]
        small["mod"][l] = [dg1, dsh2, dsc2, dg2]
        below = (saved[l - 1]["y2"], mods[l - 1, 5]) if l > 0 else None
        dx, dsh1, dsc1, dn1, *below_grads = _norm_mod_bwd(sv["x_in"], dh1, dx_mid, sv["n1"], sc1, below, "norm_mod_bwd")
        if below_grads:
            dy2, dg2 = below_grads
        small["mod"][l] = jnp.concatenate([dsh1, dsc1] + small["mod"][l], axis=-1).reshape(-1)
        small["norm1_g"][l] = dn1.reshape(-1)
        small["norm2_g"][l] = dn2.reshape(-1)
        small["sgu_w"][l] = dsw
        small["sgu_b"][l] = dsb.reshape(SGU_CHUNK, SGU_GROUPS, HEAD_DIM).sum(-1).T
        small["sgu_norm_g"][l] = dsg.reshape(SGU_GROUPS, HEAD_DIM)
        small["q_norm_g"][l] = dgq.reshape(FOX_HEADS, HEAD_DIM).sum(0)
        small["k_norm_g"][l] = dgk.reshape(FOX_HEADS, HEAD_DIM).sum(0)
        small["b_forget"][l] = dbf[0, :FOX_HEADS]

    parts = {k: jnp.stack(v) for k, v in small.items()}
    parts["ada_b"] = parts.pop("mod")
    parts["loss"] = loss_part
    (small_all,) = _exchange([_pack_small(parts)], False, "gather_small")
    zero1 = jnp.zeros((1,), F32)
    packs = [_pack_small(dict(ada_b=a, norm1_g=b, norm2_g=cc, sgu_w=d, sgu_b=e, sgu_norm_g=f, q_norm_g=g, k_norm_g=h,
                              b_forget=i, loss=zero1))
             for a, b, cc, d, e, f, g, h, i in (
                 (ada_b, norm1_g, norm2_g, sgu_w, sgu_b, sgu_norm_g, q_norm_g, k_norm_g, b_forget),
                 (m_ada_b, m_norm1_g, m_norm2_g, m_sgu_w, m_sgu_b, m_sgu_norm_g, m_q_norm_g, m_k_norm_g, m_b_forget),
                 (v_ada_b, v_norm1_g, v_norm2_g, v_sgu_w, v_sgu_b, v_sgu_norm_g, v_q_norm_g, v_k_norm_g, v_b_forget))]
    small_shapes = dict(ada_b=ada_b.shape, norm1_g=norm1_g.shape, norm2_g=norm2_g.shape, sgu_w=sgu_w.shape,
                        sgu_b=sgu_b.shape, sgu_norm_g=sgu_norm_g.shape, q_norm_g=q_norm_g.shape,
                        k_norm_g=k_norm_g.shape, b_forget=b_forget.shape, loss=())
    s_out = [_unpack_small(a, small_shapes) for a in _adamw(small_all, *packs, "adamw_small")]

    dmod_all = small_all[:, :DEPTH * 6 * D_MODEL // 128, :].reshape(N_DEV, DEPTH, 6 * D_MODEL)
    dmod_mine = lax.dynamic_slice_in_dim(dmod_all, me * n_ada, n_ada, axis=2)
    g_ada = _ada_bwd(c_all, jnp.transpose(dmod_mine, (1, 0, 2)), "ada_bwd")
    ada_out = _adamw(g_ada.reshape(1, DEPTH * D_MODEL, n_ada), ada_w.reshape(-1, n_ada), m_ada_w.reshape(-1, n_ada),
                     v_ada_w.reshape(-1, n_ada), "adamw_ada")
    ada_out = [a.reshape(ada_w.shape) for a in ada_out]

    (ri,) = _exchange(grads_ready[:1], True, "scatter_grads", recv=received[:1], slot=0)
    _, ro, r1, r2 = received

    def big(slots, w, m, v, name):
        cdim = w.shape[-1]
        outs = _adamw(slots.reshape(N_DEV, -1, cdim), w.reshape(-1, cdim), m.reshape(-1, cdim), v.reshape(-1, cdim), name)
        return [a.reshape(w.shape) for a in outs]

    in_out = big(ri, w_in, m_w_in, v_w_in, "adamw_in")
    out_out = big(ro, w_out, m_w_out, v_w_out, "adamw_out")
    w1_out = big(r1, mlp_w1, m_mlp_w1, v_mlp_w1, "adamw_w1")
    w2_out = big(r2, mlp_w2, m_mlp_w2, v_mlp_w2, "adamw_w2")

    def leaves(idx):
        sm = s_out[idx]
        return [ada_out[idx], sm["ada_b"], sm["norm1_g"], sm["norm2_g"], in_out[idx], sm["b_forget"], sm["q_norm_g"],
                sm["k_norm_g"], sm["sgu_norm_g"], sm["sgu_w"], sm["sgu_b"], out_out[idx], w1_out[idx], w2_out[idx]]

    loss = s_out[0]["loss"]
    grad_x = dx.reshape(x.shape)
    return (loss, grad_x, *leaves(0), *leaves(1), *leaves(2), *leaves(3))
```

```python
import jax
import jax.numpy as jnp
from jax import lax
from jax.experimental import pallas as pl
from jax.experimental.pallas import tpu as pltpu

F32 = jnp.float32
_BF = jnp.bfloat16
_XBF = jnp.bfloat16

N_DEV = 8
D_MODEL = 1024
DEPTH = 4
HEAD_DIM = 64
SB_HEADS = 4
FOX_HEADS = 8
SGU_GROUPS = 4
SGU_CHUNK = 128
D_FF = 4096
EPS = 1e-6
IN_W = 2824
FL_SRC = 2304
UC_SRC = 2312

C_QB, C_KB, C_VB = 0, 512, 1024
C_QA, C_KA, C_VA = 1536, 1792, 2048
C_UC, C_VC = 2304, 2560
C_FL = 2816
IN_P = 3072
M_SB, M_FOX, M_SGU = 0, 256, 768

ADAM_LR = 0.001
ADAM_B1 = 0.9
ADAM_B2 = 0.999
ADAM_EPS = 1e-08
ADAM_WD = 0.01
ADAM_STEP = 10

ATT_T = 256
ROW_T = 512
MM_T = 1024
GRAD_TS = 4096
ROW_CHUNK = 64
NEG = -1e30
UNDERFLOW = -90.0
NO_BLOCK = 1e30

MESH = pl.DeviceIdType.MESH
ARB = pltpu.ARBITRARY
PAR = pltpu.PARALLEL
HIGHEST = lax.Precision.HIGHEST


def _pcall(body, **kw):
    return pl.pallas_call(body, **kw)


def _cp(*sem):
    return pltpu.CompilerParams(dimension_semantics=tuple(sem))


def _dot(a, b, ca, cb):
    return lax.dot_general(a, b, (((ca,), (cb,)), ((), ())), preferred_element_type=F32)


def _dot_f32(a, b, ca, cb):
    return lax.dot_general(a, b, (((ca,), (cb,)), ((), ())), precision=HIGHEST, preferred_element_type=F32)


def _split(v, parts):
    out = []
    r = v
    for _ in range(parts - 1):
        p = r.astype(_BF)
        out.append(p)
        r = r - p.astype(F32)
    out.append(r.astype(_BF))
    return out


def _dot_exact(v, m, ca, cb, parts, v_left=True):
    out = None
    for p in _split(v, parts):
        term = _dot(p, m, ca, cb) if v_left else _dot(m, p, ca, cb)
        out = term if out is None else out + term
    return out


def _iota2(shape, dim):
    return lax.broadcasted_iota(jnp.int32, shape, dim)


def _my_pos():
    return lax.axis_index("x"), lax.axis_index("y"), lax.axis_index("c")


def _flip(pos, k):
    x, y, c = pos
    px = 1 - x if (k >> 2) & 1 else x
    py = 1 - y if (k >> 1) & 1 else y
    pc = 1 - c if k & 1 else c
    return px, py, pc


def _lin(pos):
    return 4 * pos[0] + 2 * pos[1] + pos[2]


def _exchange_ops(ins, outs, sems, scatter, slot):
    send_sems, recv_sems, local_sems = sems
    n = len(ins)

    def copies(with_recvs):
        me = _my_pos()
        me_i = _lin(me)
        dst = lambda a, j: outs[a].at[j] if slot is None else outs[a].at[j, slot]
        mine = lambda a: ins[a].at[me_i] if scatter else ins[a]
        local = [pltpu.make_async_copy(mine(a), dst(a, me_i), local_sems.at[a]) for a in range(n)]
        sends, recvs = [], []
        for a in range(n):
            for k in range(1, N_DEV):
                peer = _flip(me, k)
                pair = dict(send_sem=send_sems.at[a, k - 1], recv_sem=recv_sems.at[a, k - 1],
                            device_id=peer, device_id_type=MESH)
                src = ins[a].at[_lin(peer)] if scatter else ins[a]
                sends.append(pltpu.make_async_remote_copy(src_ref=src, dst_ref=dst(a, me_i), **pair))
                if with_recvs:
                    recvs.append(pltpu.make_async_remote_copy(src_ref=mine(a), dst_ref=dst(a, _lin(peer)), **pair))
        return local, sends, recvs

    def start():
        local, sends, _ = copies(False)
        for cp in local + sends:
            cp.start()

    def wait():
        local, sends, recvs = copies(True)
        for snd, rcv in zip(sends, recvs):
            snd.wait_send()
            rcv.wait_recv()
        for cp in local:
            cp.wait()

    return start, wait


def _comm_operands(comm):
    arrays, recv = comm["arrays"], comm["recv"]
    n = len(arrays)
    any_spec = pl.BlockSpec(memory_space=pl.ANY)
    if recv is not None:
        out_shapes = [jax.ShapeDtypeStruct(r.shape, r.dtype) for r in recv]
    elif comm["scatter"]:
        out_shapes = [jax.ShapeDtypeStruct(a.shape, a.dtype) for a in arrays]
    else:
        out_shapes = [jax.ShapeDtypeStruct((N_DEV,) + a.shape, a.dtype) for a in arrays]
    operands = list(arrays) + (list(recv) if recv is not None else [])
    sems = [pltpu.SemaphoreType.DMA((n, N_DEV - 1)), pltpu.SemaphoreType.DMA((n, N_DEV - 1)),
            pltpu.SemaphoreType.DMA((n,))]
    return operands, [any_spec] * len(operands), out_shapes, [any_spec] * n, sems


def _with_comm(body, comm, n_in, n_out, first_step, last_step, n_scratch=0):
    if comm is None:
        return body
    n = len(comm["arrays"])
    n_cin = n if comm["recv"] is None else 2 * n

    def wrapped(*refs):
        ins, cins = refs[:n_in], refs[n_in:n_in + n]
        o0 = n_in + n_cin
        outs, couts = refs[o0:o0 + n_out], refs[o0 + n_out:o0 + n_out + n]
        s0 = o0 + n_out + n
        scratch, sems = refs[s0:s0 + n_scratch], refs[s0 + n_scratch:]
        start, wait = _exchange_ops(cins, couts, sems, comm["scatter"], comm["slot"])
        pl.when(first_step())(start)
        body(*ins, *outs, *scratch)
        pl.when(last_step())(wait)

    return wrapped


def _comm_aliases(comm, n_in, n_out):
    if comm is None or comm["recv"] is None:
        return {}
    n = len(comm["arrays"])
    return {n_in + n + a: n_out + a for a in range(n)}


def _exchange(arrays, scatter, name, recv=None, slot=None):
    comm = dict(arrays=arrays, scatter=scatter, recv=recv, slot=slot)
    operands, in_specs, out_shapes, out_specs, sems = _comm_operands(comm)
    n = len(arrays)

    def body(*refs):
        n_cin = len(operands)
        start, wait = _exchange_ops(refs[:n], refs[n_cin:n_cin + n], refs[n_cin + n:], scatter, slot)
        start()
        wait()

    outs = _pcall(body, name=name, out_shape=out_shapes, in_specs=in_specs, out_specs=out_specs, scratch_shapes=sems,
                  input_output_aliases=_comm_aliases(comm, 0, 0))(*operands)
    return list(outs)


def _row_spec(t, w, col=0):
    return pl.BlockSpec((t, w), lambda i: (i, col))


def _vec_spec(w, col=0):
    return pl.BlockSpec((1, w), lambda i: (0, col))


def _norm_mod(x, g, sc, sh, name):
    s, d = x.shape
    t = min(ROW_T, s)

    def body(x_ref, g_ref, sc_ref, sh_ref, h_ref, ht_ref):
        xf = x_ref[...]
        rstd = lax.rsqrt(jnp.mean(xf * xf, axis=-1, keepdims=True) + EPS)
        y = xf * rstd * g_ref[...]
        h = y * (1.0 + sc_ref[...]) + sh_ref[...]
        h_ref[...] = h.astype(h_ref.dtype)
        ht_ref[...] = h.T.astype(ht_ref.dtype)

    return _pcall(
        body, name=name, grid=(s // t,),
        out_shape=[jax.ShapeDtypeStruct((s, d), _BF), jax.ShapeDtypeStruct((d, s), _BF)],
        in_specs=[_row_spec(t, d), _vec_spec(d), _vec_spec(d), _vec_spec(d)],
        out_specs=[_row_spec(t, d), pl.BlockSpec((d, t), lambda i: (0, i))], compiler_params=_cp(PAR),
    )(x, g, sc, sh)


def _res_bwd_step(i, dx, y_ref, gate_ref, dy_ref, dgate_ref):
    dy_ref[...] = (dx * gate_ref[...]).astype(dy_ref.dtype)

    @pl.when(i == 0)
    def _():
        dgate_ref[...] = jnp.zeros_like(dgate_ref)

    dgate_ref[...] += jnp.sum(dx * y_ref[...], axis=0, keepdims=True)


def _norm_mod_bwd(x, dh, dres, g, sc, branch, name):
    s, d = x.shape
    t = min(ROW_T, s)
    n_branch = 0 if branch is None else 2

    ch = min(ROW_CHUNK, t)

    def body(x_ref, dh_ref, dres_ref, g_ref, sc_ref, *rest):
        dx_ref, dsh_ref, dsc_ref, dg_ref = rest[n_branch:n_branch + 4]
        i = pl.program_id(0)
        gv = g_ref[...]
        sc1 = 1.0 + sc_ref[...]
        gain = gv * sc1
        n_sum = 2 if branch is None else 3

        def rows(r, sums):
            sl = pl.ds(pl.multiple_of(r * ch, ch), ch)
            xf = x_ref[sl, :]
            dh = dh_ref[sl, :]
            rstd = lax.rsqrt(jnp.mean(xf * xf, axis=-1, keepdims=True) + EPS)
            xhat = xf * rstd
            dh_xhat = dh * xhat
            dxh = dh * gain
            dx = dres_ref[sl, :] + rstd * (dxh - xhat * jnp.mean(dh_xhat * gain, axis=-1, keepdims=True))
            dx_ref[sl, :] = dx
            new = [sums[0] + jnp.sum(dh, axis=0, keepdims=True), sums[1] + jnp.sum(dh_xhat, axis=0, keepdims=True)]
            if branch is not None:
                rest[n_branch + 4][sl, :] = (dx * rest[1][...]).astype(_BF)
                new.append(sums[2] + jnp.sum(dx * rest[0][sl, :], axis=0, keepdims=True))
            return tuple(new)

        sums = lax.fori_loop(0, t // ch, rows, tuple(jnp.zeros((1, d), F32) for _ in range(n_sum)))
        parts = [sums[0], sums[1] * gv, sums[1] * sc1] + list(sums[2:])
        totals = [dsh_ref, dsc_ref, dg_ref] + ([rest[n_branch + 5]] if branch is not None else [])

        @pl.when(i == 0)
        def _():
            for ref in totals:
                ref[...] = jnp.zeros_like(ref)

        for ref, part in zip(totals, parts):
            ref[...] += part

    vec = jax.ShapeDtypeStruct((1, d), F32)
    out_shape = [jax.ShapeDtypeStruct((s, d), F32), vec, vec, vec]
    in_specs = [_row_spec(t, d), _row_spec(t, d), _row_spec(t, d), _vec_spec(d), _vec_spec(d)]
    out_specs = [_row_spec(t, d), _vec_spec(d), _vec_spec(d), _vec_spec(d)]
    if branch is not None:
        in_specs += [_row_spec(t, d), _vec_spec(d)]
        out_shape += [jax.ShapeDtypeStruct((s, d), _BF), vec]
        out_specs += [_row_spec(t, d), _vec_spec(d)]
    return _pcall(
        body, name=name, grid=(s // t,), out_shape=out_shape, in_specs=in_specs, out_specs=out_specs,
        compiler_params=_cp(ARB),
    )(x, dh, dres, g, sc, *(branch or ()))


def _loss_head(y, target, branch, name):
    s, d = y.shape
    t = min(ROW_T, s)

    def body(y_ref, t_ref, yb_ref, gate_ref, loss_ref, dy_ref, dyb_ref, dgate_ref):
        i = pl.program_id(0)
        diff = y_ref[...] - t_ref[...]
        dy = diff * (1.0 / d)
        dy_ref[...] = dy

        @pl.when(i == 0)
        def _():
            loss_ref[...] = jnp.zeros_like(loss_ref)

        rows = jnp.sum(diff * diff, axis=-1, keepdims=True)
        loss_ref[...] += (0.5 / d) * jnp.sum(rows, axis=0, keepdims=True)
        _res_bwd_step(i, dy, yb_ref, gate_ref, dyb_ref, dgate_ref)

    return _pcall(
        body, name=name, grid=(s // t,),
        out_shape=[jax.ShapeDtypeStruct((1, 1), F32), jax.ShapeDtypeStruct((s, d), F32),
                   jax.ShapeDtypeStruct((s, d), _BF), jax.ShapeDtypeStruct((1, d), F32)],
        in_specs=[_row_spec(t, d), _row_spec(t, d), _row_spec(t, d), _vec_spec(d)],
        out_specs=[pl.BlockSpec((1, 1), lambda i: (0, 0)), _row_spec(t, d), _row_spec(t, d), _vec_spec(d)],
        compiler_params=_cp(ARB),
    )(y, target, *branch)


def _mm_nn(a, b, epi, extras, name):
    m, kdim = a.shape
    n = b.shape[1]
    tm, tn, tk = min(MM_T, m), min(MM_T, n), min(MM_T, kdim)
    nk = kdim // tk
    n_extra = len(extras)
    n_out = {"plain": 1, "resid": 2, "relu2": 3}[epi]

    def finish(y, extra_refs, out_refs):
        if epi == "plain":
            out_refs[0][...] = y.astype(out_refs[0].dtype)
        elif epi == "resid":
            x_ref, g_ref = extra_refs
            out_refs[0][...] = x_ref[...] + g_ref[...] * y
            out_refs[1][...] = y
        else:
            r = jnp.maximum(y, 0.0)
            out_refs[0][...] = (r * r).astype(out_refs[0].dtype)
            out_refs[1][...] = r.astype(out_refs[1].dtype)
            out_refs[2][...] = (r * r).T.astype(out_refs[2].dtype)

    def body(a_ref, b_ref, *rest):
        extra_refs = rest[:n_extra]
        out_refs = rest[n_extra:n_extra + n_out]
        part = _dot(a_ref[...].astype(_BF), b_ref[...].astype(_BF), 1, 0)
        if nk == 1:
            finish(part, extra_refs, out_refs)
        else:
            acc_ref = rest[-1]
            k = pl.program_id(2)

            @pl.when(k == 0)
            def _():
                acc_ref[...] = part

            @pl.when(k > 0)
            def _():
                acc_ref[...] += part

            @pl.when(k == nk - 1)
            def _():
                finish(acc_ref[...], extra_refs, out_refs)

    tile = pl.BlockSpec((tm, tn), lambda i, j, k: (i, j))
    in_specs = [pl.BlockSpec((tm, tk), lambda i, j, k: (i, k)), pl.BlockSpec((tk, tn), lambda i, j, k: (k, j))]
    if epi == "plain":
        out_shape = [jax.ShapeDtypeStruct((m, n), F32)]
    elif epi == "resid":
        in_specs += [tile, pl.BlockSpec((1, tn), lambda i, j, k: (0, j))]
        out_shape = [jax.ShapeDtypeStruct((m, n), F32)] * 2
    out_specs = [tile] * n_out
    if epi == "relu2":
        out_shape = [jax.ShapeDtypeStruct((m, n), _BF), jax.ShapeDtypeStruct((m, n), F32),
                     jax.ShapeDtypeStruct((n, m), _BF)]
        out_specs[2] = pl.BlockSpec((tn, tm), lambda i, j, k: (j, i))
    outs = _pcall(
        body, name=name, grid=(m // tm, n // tn, nk), out_shape=out_shape,
        in_specs=in_specs, out_specs=out_specs,
        scratch_shapes=[pltpu.VMEM((tm, tn), F32)] if nk > 1 else [],
        compiler_params=_cp(PAR, PAR, ARB),
    )(a, b, *extras)
    return outs[0] if n_out == 1 else outs


def _mm_out(parts, w, x, gate, name):
    s, n = x.shape
    widths = [p.shape[1] for p in parts]
    kdim = sum(widths)
    tm = min(ROW_T, s)

    def body(a0_ref, a1_ref, a2_ref, w_ref, x_ref, g_ref, xo_ref, y_ref, mt_ref):
        mixed = jnp.concatenate([a0_ref[...], a1_ref[...], a2_ref[...]], axis=-1)
        y = _dot(mixed.astype(_BF), w_ref[...], 1, 0)
        xo_ref[...] = x_ref[...] + g_ref[...] * y
        y_ref[...] = y
        mt_ref[...] = mixed.T.astype(mt_ref.dtype)

    row = lambda width: pl.BlockSpec((tm, width), lambda i: (i, 0))
    return _pcall(
        body, name=name, grid=(s // tm,),
        out_shape=[jax.ShapeDtypeStruct((s, n), F32), jax.ShapeDtypeStruct((s, n), F32),
                   jax.ShapeDtypeStruct((kdim, s), _BF)],
        in_specs=[row(widths[0]), row(widths[1]), row(widths[2]), pl.BlockSpec((kdim, n), lambda i: (0, 0)),
                  row(n), pl.BlockSpec((1, n), lambda i: (0, 0))],
        out_specs=[row(n), row(n), pl.BlockSpec((kdim, tm), lambda i: (0, i))], compiler_params=_cp(PAR),
    )(*parts, w, x, gate)


def _mm_nt(a, b, epi, extras, out_dtype, name):
    m, kdim = a.shape
    n = b.shape[0]
    tm, tn, tk = min(MM_T, m), min(MM_T, n), min(MM_T, kdim)
    nk = kdim // tk
    n_extra = len(extras)

    def finish(y, extra_refs, o_ref):
        if epi == "mul2":
            y = y * (2.0 * extra_refs[0][...].astype(F32))
        o_ref[...] = y.astype(o_ref.dtype)

    def body(a_ref, b_ref, *rest):
        extra_refs = rest[:n_extra]
        o_ref = rest[n_extra]
        part = _dot(a_ref[...].astype(_BF), b_ref[...].astype(_BF), 1, 1)
        if nk == 1:
            finish(part, extra_refs, o_ref)
        else:
            acc_ref = rest[-1]
            k = pl.program_id(2)

            @pl.when(k == 0)
            def _():
                acc_ref[...] = part

            @pl.when(k > 0)
            def _():
                acc_ref[...] += part

            @pl.when(k == nk - 1)
            def _():
                finish(acc_ref[...], extra_refs, o_ref)

    tile = pl.BlockSpec((tm, tn), lambda i, j, k: (i, j))
    in_specs = [pl.BlockSpec((tm, tk), lambda i, j, k: (i, k)), pl.BlockSpec((tn, tk), lambda i, j, k: (j, k))]
    in_specs += [tile] * n_extra
    return _pcall(
        body, name=name, grid=(m // tm, n // tn, nk), out_shape=jax.ShapeDtypeStruct((m, n), out_dtype),
        in_specs=in_specs, out_specs=tile,
        scratch_shapes=[pltpu.VMEM((tm, tn), F32)] if nk > 1 else [],
        compiler_params=_cp(PAR, PAR, ARB),
    )(a, b, *extras)


def _mm_grad(at, b, split, name):
    m, s = at.shape
    n = b.shape[1]
    ts = min(GRAD_TS, s)
    ns = s // ts
    if split == "rows":
        tm, tn = min(MM_T, m // N_DEV), min(MM_T, n)
        per = (m // N_DEV) // tm
        out_shape = (N_DEV, m // N_DEV, n)
        out_spec = pl.BlockSpec((None, tm, tn), lambda i, j, k: (i // per, i % per, j))
    elif split == "cols":
        tm, tn = min(MM_T, m), min(MM_T, n // N_DEV)
        per = (n // N_DEV) // tn
        out_shape = (N_DEV, m, n // N_DEV)
        out_spec = pl.BlockSpec((None, tm, tn), lambda i, j, k: (j // per, i, j % per))
    else:
        tm, tn = min(MM_T, m), min(MM_T, n)
        out_shape = (m, n)
        out_spec = pl.BlockSpec((tm, tn), lambda i, j, k: (i, j))

    def body(a_ref, b_ref, o_ref, acc_ref):
        k = pl.program_id(2)
        part = _dot(a_ref[...].astype(_BF), b_ref[...].astype(_BF), 1, 0)

        @pl.when(k == 0)
        def _():
            acc_ref[...] = part

        @pl.when(k > 0)
        def _():
            acc_ref[...] += part

        @pl.when(k == ns - 1)
        def _():
            o_ref[...] = acc_ref[...].astype(o_ref.dtype)

    return _pcall(
        body, name=name, grid=(m // tm, n // tn, ns), out_shape=jax.ShapeDtypeStruct(out_shape, _XBF),
        in_specs=[pl.BlockSpec((tm, ts), lambda i, j, k: (i, k)), pl.BlockSpec((ts, tn), lambda i, j, k: (k, j))],
        out_specs=out_spec, scratch_shapes=[pltpu.VMEM((tm, tn), F32)],
        compiler_params=_cp(PAR, PAR, ARB),
    )(at, b)


def _head_masks():
    lane = _iota2((1, 128), 1)
    return [(lane // HEAD_DIM) == hh for hh in range(2)]


def _untranspose(src_ref, dst_ref, t, nblk):
    def rows(c, _):
        off = pl.multiple_of(c * t, t)
        dst_ref[pl.ds(off, t), :] = src_ref[:, pl.ds(off, t)].T
        return 0

    lax.fori_loop(0, nblk, rows, 0)


def _softplus_parts(z):
    sp = jnp.maximum(z, 0.0) + jnp.log(1.0 + jnp.exp(-jnp.abs(z)))
    return -sp, z - sp


def _sb_key_loop(i, carry, step, lookahead, first_ahead):
    def cond(c):
        return jnp.logical_and(c[0] <= i, jnp.max(jnp.maximum(c[1][0][0], c[1][0][1])) > UNDERFLOW)

    def body(c):
        ahead = lookahead(i - c[0] - 1)
        return c[0] + 1, step(i - c[0], c[1], c[2]), ahead

    return lax.while_loop(cond, body, (jnp.int32(1), carry, first_ahead))[1]


def _sb_fwd(proj, name, comm=None):
    s = proj.shape[0]
    t = min(ATT_T, s)
    nblk = s // t
    scale = HEAD_DIM ** -0.5

    def body(q_ref, k_ref, v_ref, o_ref):
        i = pl.program_id(1)
        masks = _head_masks()
        row = _iota2((t, t), 0)
        col = _iota2((t, t), 1)
        u_strict = (row > col).astype(_BF)
        causal = col < row
        qs = q_ref[...] * scale
        qm = [jnp.where(masks[hh], qs, 0.0).astype(_BF) for hh in range(2)]

        def scores(j):
            kb = k_ref[pl.ds(pl.multiple_of(jnp.maximum(j, 0) * t, t), t), :].astype(_BF)
            return tuple(_dot(qm[hh], kb, 1, 1) for hh in range(2))

        def block(j, carry, qk, diag):
            off = pl.multiple_of(j * t, t)
            vb = v_ref[pl.ds(off, t), :].astype(_BF)
            runs, outs = [], []
            for hh in range(2):
                run, o = carry[0][hh], carry[1][hh]
                z = qk[hh]
                l1, lb = _softplus_parts(z)
                if diag:
                    l1 = jnp.where(causal, l1, 0.0)
                between = run + _dot_exact(l1, u_strict, 1, 0, 2)
                a = jnp.exp(lb + between)
                if diag:
                    a = jnp.where(causal, a, 0.0)
                outs.append(o + _dot_exact(a, vb, 1, 0, 2))
                runs.append(run + jnp.sum(l1, axis=-1, keepdims=True))
            return tuple(runs), tuple(outs)

        zero, zero_o = jnp.zeros((t, 1), F32), jnp.zeros((t, 128), F32)
        ahead = scores(i - 1)
        carry = block(i, ((zero, zero), (zero_o, zero_o)), scores(i), True)
        carry = _sb_key_loop(i, carry, lambda j, c, qk: block(j, c, qk, False), scores, ahead)
        o_ref[...] = jnp.where(masks[0], carry[1][0], carry[1][1])

    kv_spec = lambda c0: pl.BlockSpec((s, 128), lambda p, i: (0, c0 // 128 + p))
    c_operands, c_in_specs, c_out_shapes, c_out_specs, c_sems = _comm_operands(comm) if comm else ([], [], [], [], [])
    outs = _pcall(
        _with_comm(body, comm, 3, 1, *_grid_ends(SB_HEADS // 2, nblk)), name=name, grid=(SB_HEADS // 2, nblk),
        out_shape=[jax.ShapeDtypeStruct((s, 256), F32)] + c_out_shapes,
        in_specs=[pl.BlockSpec((t, 128), lambda p, i: (i, C_QA // 128 + p)), kv_spec(C_KA), kv_spec(C_VA)] + c_in_specs,
        out_specs=[pl.BlockSpec((t, 128), lambda p, i: (i, p))] + c_out_specs,
        scratch_shapes=c_sems, input_output_aliases=_comm_aliases(comm, 3, 1), compiler_params=_cp(ARB, ARB),
    )(proj, proj, proj, *c_operands)
    return outs[0], list(outs[1:])


def _sb_bwd(proj, o, dmix, name, comm=None):
    s = proj.shape[0]
    t = min(ATT_T, s)
    nblk = s // t
    scale = HEAD_DIM ** -0.5

    def body(q_ref, k_ref, v_ref, o_ref, do_ref, dq_ref, dk_ref, dv_ref, dkt_ref, dvt_ref):
        i = pl.program_id(1)
        masks = _head_masks()
        row = _iota2((t, t), 0)
        col = _iota2((t, t), 1)
        u_strict = (row > col).astype(_BF)
        u_incl = (row >= col).astype(_BF)
        causal = col < row

        @pl.when(i == 0)
        def _():
            dkt_ref[...] = jnp.zeros_like(dkt_ref)
            dvt_ref[...] = jnp.zeros_like(dvt_ref)

        qs = q_ref[...] * scale
        dov = do_ref[...]
        ov = o_ref[...]
        qm_f = [jnp.where(masks[hh], qs, 0.0) for hh in range(2)]
        dom_f = [jnp.where(masks[hh], dov, 0.0) for hh in range(2)]
        qm = [qm_f[hh].astype(_BF) for hh in range(2)]
        dom = [dom_f[hh].astype(_BF) for hh in range(2)]
        qm_t = [qm_f[hh].T.astype(_BF) for hh in range(2)]
        dom_t = [dom_f[hh].T.astype(_BF) for hh in range(2)]
        dsum = [jnp.sum(dom[hh].astype(F32) * ov, axis=-1, keepdims=True) for hh in range(2)]

        def scores(j):
            off = pl.multiple_of(jnp.maximum(j, 0) * t, t)
            kb = k_ref[pl.ds(off, t), :].astype(_BF)
            vb = v_ref[pl.ds(off, t), :].astype(_BF)
            return tuple((_dot(qm[hh], kb, 1, 1), _dot(dom[hh], vb, 1, 1)) for hh in range(2))

        def block(j, carry, ahead, diag):
            off = pl.multiple_of(j * t, t)
            kb = k_ref[pl.ds(off, t), :].astype(_BF)
            runs, eruns, dqs = [], [], []
            dk = dv = None
            for hh in range(2):
                run, erun, dq = carry[0][hh], carry[1][hh], carry[2][hh]
                z, da = ahead[hh]
                l1, lb = _softplus_parts(z)
                if diag:
                    l1 = jnp.where(causal, l1, 0.0)
                between = run + _dot_exact(l1, u_strict, 1, 0, 2)
                a = jnp.exp(lb + between)
                if diag:
                    a = jnp.where(causal, a, 0.0)
                g = a * da
                cum = dsum[hh] - (erun + _dot_exact(g, u_incl, 1, 0, 2))
                beta = jnp.exp(lb)
                dz = g * (1.0 - beta) - cum * beta
                if diag:
                    dz = jnp.where(causal, dz, 0.0)
                dzb = dz.astype(_BF)
                dqs.append(dq + _dot(dzb, kb, 1, 0))
                dk_h = _dot(qm_t[hh], dzb, 1, 0)
                dv_h = _dot(dom_t[hh], a.astype(_BF), 1, 0)
                dk = dk_h if dk is None else dk + dk_h
                dv = dv_h if dv is None else dv + dv_h
                runs.append(run + jnp.sum(l1, axis=-1, keepdims=True))
                eruns.append(erun + jnp.sum(g, axis=-1, keepdims=True))
            dkt_ref[:, pl.ds(off, t)] += dk
            dvt_ref[:, pl.ds(off, t)] += dv
            return tuple(runs), tuple(eruns), tuple(dqs)

        zero, zero_q = jnp.zeros((t, 1), F32), jnp.zeros((t, 128), F32)
        ahead = scores(i - 1)
        carry = block(i, ((zero, zero), (zero, zero), (zero_q, zero_q)), scores(i), True)
        carry = _sb_key_loop(i, carry, lambda j, c, a: block(j, c, a, False), scores, ahead)
        dq_ref[...] = jnp.where(masks[0], carry[2][0], carry[2][1]) * scale

        @pl.when(i == nblk - 1)
        def _():
            _untranspose(dkt_ref, dk_ref, t, nblk)
            _untranspose(dvt_ref, dv_ref, t, nblk)

    kv_spec = lambda c0: pl.BlockSpec((s, 128), lambda p, i: (0, c0 // 128 + p))
    blk = lambda c0: pl.BlockSpec((t, 128), lambda p, i: (i, c0 // 128 + p))
    acc = pl.BlockSpec((s, 128), lambda p, i: (0, p))
    shp = jax.ShapeDtypeStruct((s, 256), F32)
    c_operands, c_in_specs, c_out_shapes, c_out_specs, c_sems = _comm_operands(comm) if comm else ([], [], [], [], [])
    outs = _pcall(
        _with_comm(body, comm, 5, 3, *_grid_ends(SB_HEADS // 2, nblk), n_scratch=2), name=name,
        grid=(SB_HEADS // 2, nblk), out_shape=[shp, shp, shp] + c_out_shapes,
        in_specs=[blk(C_QA), kv_spec(C_KA), kv_spec(C_VA), blk(0), blk(M_SB)] + c_in_specs,
        out_specs=[blk(0), acc, acc] + c_out_specs,
        scratch_shapes=[pltpu.VMEM((128, s), F32), pltpu.VMEM((128, s), F32)] + c_sems,
        input_output_aliases=_comm_aliases(comm, 5, 3), compiler_params=_cp(ARB, ARB),
    )(proj, proj, proj, o, dmix, *c_operands)
    return outs[0], outs[1], outs[2], list(outs[3:])


def _group_matrix(width):
    r = _iota2((width, width), 0) // HEAD_DIM
    c = _iota2((width, width), 1) // HEAD_DIM
    return (r == c).astype(_BF)


def _group_mean(v, gm):
    return _dot_exact(v, gm, 1, 0, 2) * (1.0 / HEAD_DIM)


def _fox_prep(proj, gq, gk, bf, name):
    s = proj.shape[0]
    t = min(ATT_T, s)
    nblk = s // t
    assert nblk <= 128

    def body(q_ref, k_ref, fl_ref, gq_ref, gk_ref, bf_ref, qn_ref, kn_ref, cfc_ref, cfr_ref, cfe_ref, carry_ref):
        i = pl.program_id(0)
        gm = _group_matrix(512)
        for src, gref, dst in ((q_ref, gq_ref, qn_ref), (k_ref, gk_ref, kn_ref)):
            v = src[...]
            rstd = lax.rsqrt(_group_mean(v * v, gm) + EPS)
            dst[...] = (v * rstd * gref[...]).astype(dst.dtype)

        @pl.when(i == 0)
        def _():
            carry_ref[...] = jnp.zeros_like(carry_ref)
            cfe_ref[...] = jnp.full(cfe_ref.shape, NO_BLOCK, F32)

        lane = _iota2((1, 128), 1)
        logf = jnp.where(lane < FOX_HEADS, _softplus_parts(-(fl_ref[...] + bf_ref[...]))[0], 0.0)
        lower = (_iota2((t, t), 0) >= _iota2((t, t), 1)).astype(_BF)
        cf = carry_ref[...] + _dot_exact(logf, lower, 1, 0, 3, v_left=False)
        cfc_ref[...] = cf
        rows = cf.T[:FOX_HEADS, :]
        cfr_ref[...] = rows
        cfe_ref[...] = jnp.where(lane == i, rows[:, t - 1:t], cfe_ref[...])
        carry_ref[...] = cf[t - 1:t, :]

    col = lambda w, c0: pl.BlockSpec((t, w), lambda i: (i, c0 // w))
    v512 = pl.BlockSpec((1, 512), lambda i: (0, 0))
    return _pcall(
        body, name=name, grid=(nblk,),
        out_shape=[jax.ShapeDtypeStruct((s, 512), _BF), jax.ShapeDtypeStruct((s, 512), _BF),
                   jax.ShapeDtypeStruct((s, 128), F32), jax.ShapeDtypeStruct((FOX_HEADS, s), F32),
                   jax.ShapeDtypeStruct((FOX_HEADS, 128), F32)],
        in_specs=[col(512, C_QB), col(512, C_KB), col(128, C_FL), v512, v512, pl.BlockSpec((1, 128), lambda i: (0, 0))],
        out_specs=[col(512, 0), col(512, 0), col(128, 0), pl.BlockSpec((FOX_HEADS, t), lambda i: (0, i)),
                   pl.BlockSpec((FOX_HEADS, 128), lambda i: (0, 0))],
        scratch_shapes=[pltpu.VMEM((1, 128), F32)], compiler_params=_cp(ARB),
    )(proj, proj, proj, gq, gk, bf)


def _fox_cq(cfc_blk, head):
    lane = _iota2((1, 128), 1)
    return jnp.sum(jnp.where(lane == head, cfc_blk, 0.0), axis=-1, keepdims=True)


def _fox_ck(cfr_ref, off, t, head):
    sub = _iota2((FOX_HEADS, 1), 0)
    return jnp.sum(jnp.where(sub == head, cfr_ref[:, pl.ds(off, t)], 0.0), axis=0, keepdims=True)


def _fox_blocks_needed(top, cfe_ref, head, i):
    sub = _iota2((FOX_HEADS, 1), 0)
    lane = _iota2((1, 128), 1)
    last_key = jnp.sum(jnp.where(sub == head, cfe_ref[...], 0.0), axis=0, keepdims=True)
    need = jnp.logical_and(lane < i, top - last_key > UNDERFLOW)
    return jnp.sum(need.astype(jnp.int32))


def _grid_ends(n_outer, n_inner):
    first = lambda: jnp.logical_and(pl.program_id(0) == 0, pl.program_id(1) == 0)
    last = lambda: jnp.logical_and(pl.program_id(0) == n_outer - 1, pl.program_id(1) == n_inner - 1)
    return first, last


def _fox_fwd(proj, qn, kn, cfc, cfr, cfe, qk_bound, name, comm=None):
    s = proj.shape[0]
    t = min(ATT_T, s)
    nblk = s // t
    scale = HEAD_DIM ** -0.5

    def body(q_ref, k_ref, v_ref, cfc_ref, cfr_ref, cfe_ref, qkb_ref, o_ref, lse_ref):
        p = pl.program_id(0)
        i = pl.program_id(1)
        masks = _head_masks()
        lane = _iota2((1, 128), 1)
        valid = _iota2((t, t), 1) <= _iota2((t, t), 0)
        qv = q_ref[...]
        cfc_blk = cfc_ref[...]
        heads = [2 * p + hh for hh in range(2)]
        qm = [(jnp.where(masks[hh], qv, 0) * scale).astype(_BF) for hh in range(2)]
        cq = [_fox_cq(cfc_blk, heads[hh]) for hh in range(2)]

        def scores(j):
            kb = k_ref[pl.ds(pl.multiple_of(jnp.maximum(j, 0) * t, t), t), :]
            return tuple(_dot(qm[hh], kb, 1, 1) for hh in range(2))

        def block(j, carry, qk, diag):
            off = pl.multiple_of(j * t, t)
            vb = v_ref[pl.ds(off, t), :].astype(_BF)
            ms, ls, accs = [], [], []
            for hh in range(2):
                m, l, acc = carry[0][hh], carry[1][hh], carry[2][hh]
                z = qk[hh] + (cq[hh] - _fox_ck(cfr_ref, off, t, heads[hh]))
                if diag:
                    z = jnp.where(valid, z, NEG)
                m_new = jnp.maximum(m, jnp.max(z, axis=-1, keepdims=True))
                pe = jnp.exp(z - m_new)
                alpha = jnp.exp(m - m_new)
                ms.append(m_new)
                ls.append(alpha * l + jnp.sum(pe, axis=-1, keepdims=True))
                accs.append(alpha * acc + _dot(pe.astype(_BF), vb, 1, 0))
            return tuple(ms), tuple(ls), tuple(accs)

        def step(jj, c):
            ahead = scores(i - jj - 2)
            return block(i - jj - 1, c[0], c[1], False), ahead

        neg, zero, zero_o = jnp.full((t, 1), NEG, F32), jnp.zeros((t, 1), F32), jnp.zeros((t, 128), F32)
        ahead = scores(i - 1)
        carry = block(i, ((neg, neg), (zero, zero), (zero_o, zero_o)), scores(i), True)
        needed = [_fox_blocks_needed(jnp.max(qkb_ref[...] + cq[hh] - carry[0][hh]), cfe_ref, heads[hh], i)
                  for hh in range(2)]
        m, l, acc = lax.fori_loop(0, jnp.maximum(needed[0], needed[1]), step, (carry, ahead))[0]
        o_ref[...] = jnp.where(masks[0], acc[0] / l[0], acc[1] / l[1])
        lse_ref[...] = jnp.where(lane == 0, m[0] + jnp.log(l[0]), jnp.where(lane == 1, m[1] + jnp.log(l[1]), 0.0))

    c_operands, c_in_specs, c_out_shapes, c_out_specs, c_sems = _comm_operands(comm) if comm else ([], [], [], [], [])
    outs = _pcall(
        _with_comm(body, comm, 7, 2, *_grid_ends(FOX_HEADS // 2, nblk)), name=name, grid=(FOX_HEADS // 2, nblk),
        out_shape=[jax.ShapeDtypeStruct((s, 512), F32), jax.ShapeDtypeStruct((FOX_HEADS // 2, s, 128), F32)] + c_out_shapes,
        in_specs=[pl.BlockSpec((t, 128), lambda p, i: (i, p)), pl.BlockSpec((s, 128), lambda p, i: (0, p)),
                  pl.BlockSpec((s, 128), lambda p, i: (0, C_VB // 128 + p)),
                  pl.BlockSpec((t, 128), lambda p, i: (i, 0)), pl.BlockSpec((FOX_HEADS, s), lambda p, i: (0, 0)),
                  pl.BlockSpec((FOX_HEADS, 128), lambda p, i: (0, 0)), pl.BlockSpec((1, 1), lambda p, i: (0, 0))] + c_in_specs,
        out_specs=[pl.BlockSpec((t, 128), lambda p, i: (i, p)),
                   pl.BlockSpec((None, t, 128), lambda p, i: (p, i, 0))] + c_out_specs,
        scratch_shapes=c_sems, input_output_aliases=_comm_aliases(comm, 7, 2),
        compiler_params=_cp(ARB, ARB),
    )(qn, kn, proj, cfc, cfr, cfe, qk_bound, *c_operands)
    return outs[0], outs[1], list(outs[2:])


def _fox_bwd(proj, qn, kn, cfc, cfr, cfe, qk_bound, o, lse, dmix, name, comm=None):
    s = proj.shape[0]
    t = min(ATT_T, s)
    nblk = s // t
    scale = HEAD_DIM ** -0.5

    def body(q_ref, k_ref, v_ref, cfc_ref, cfr_ref, cfe_ref, qkb_ref, o_ref, lse_ref, do_ref,
             dq_ref, dk_ref, dv_ref, dcr_ref, dkt_ref, dvt_ref):
        p = pl.program_id(0)
        i = pl.program_id(1)
        masks = _head_masks()
        lane = _iota2((1, 128), 1)
        valid = _iota2((t, t), 1) <= _iota2((t, t), 0)
        before = (_iota2((t, t), 0) < _iota2((t, t), 1)).astype(_BF)

        @pl.when(i == 0)
        def _():
            dkt_ref[...] = jnp.zeros_like(dkt_ref)
            dvt_ref[...] = jnp.zeros_like(dvt_ref)

        @pl.when(jnp.logical_and(p == 0, i == 0))
        def _():
            dcr_ref[...] = jnp.zeros_like(dcr_ref)

        head_row = _iota2((FOX_HEADS, 1), 0)
        qv = q_ref[...]
        cfc_blk = cfc_ref[...]
        dov = do_ref[...]
        ov = o_ref[...]
        lse_blk = lse_ref[...]
        heads = [2 * p + hh for hh in range(2)]
        qm = [(jnp.where(masks[hh], qv, 0) * scale).astype(_BF) for hh in range(2)]
        dom = [jnp.where(masks[hh], dov, 0.0).astype(_BF) for hh in range(2)]
        qm_t = [qm[hh].astype(F32).T.astype(_BF) for hh in range(2)]
        dom_t = [dom[hh].astype(F32).T.astype(_BF) for hh in range(2)]
        dsum = [jnp.sum(dom[hh].astype(F32) * ov, axis=-1, keepdims=True) for hh in range(2)]
        lse_h = [jnp.sum(jnp.where(lane == hh, lse_blk, 0.0), axis=-1, keepdims=True) for hh in range(2)]
        cq = [_fox_cq(cfc_blk, heads[hh]) for hh in range(2)]
        needed = [_fox_blocks_needed(jnp.max(qkb_ref[...] + cq[hh] - lse_h[hh]), cfe_ref, heads[hh], i)
                  for hh in range(2)]
        first = i - jnp.maximum(needed[0], needed[1])

        def scores(j):
            off = pl.multiple_of(j * t, t)
            kb = k_ref[pl.ds(off, t), :]
            vb = v_ref[pl.ds(off, t), :].astype(_BF)
            return tuple((_dot(qm[hh], kb, 1, 1), _dot(dom[hh], vb, 1, 1)) for hh in range(2))

        def block(j, carry, ahead, diag):
            off = pl.multiple_of(j * t, t)
            kb = k_ref[pl.ds(off, t), :]
            dqs, pres = [], []
            dk = dv = None
            for hh in range(2):
                dq, pre = carry[0][hh], carry[1][hh]
                z = ahead[hh][0] + (cq[hh] - _fox_ck(cfr_ref, off, t, heads[hh]))
                pm = jnp.exp(z - lse_h[hh])
                if diag:
                    pm = jnp.where(valid, pm, 0.0)
                ds = pm * (ahead[hh][1] - dsum[hh])
                dsb = ds.astype(_BF)
                dqs.append(dq + _dot(dsb, kb, 1, 0))
                dk_h = _dot(qm_t[hh], dsb, 1, 0)
                dv_h = _dot(dom_t[hh], pm.astype(_BF), 1, 0)
                dk = dk_h if dk is None else dk + dk_h
                dv = dv_h if dv is None else dv + dv_h
                if diag:
                    lower_keys = jnp.where(valid, pre + _dot_exact(ds, before, 1, 0, 2), 0.0)
                    dlogf = jnp.sum(lower_keys, axis=0, keepdims=True)
                else:
                    dlogf = jnp.sum(pre) + _dot_exact(jnp.sum(ds, axis=0, keepdims=True), before, 1, 0, 3)
                dcr_ref[:, pl.ds(off, t)] += jnp.where(head_row == heads[hh], dlogf, 0.0)
                pres.append(pre + jnp.sum(ds, axis=-1, keepdims=True))
            dkt_ref[:, pl.ds(off, t)] += dk
            dvt_ref[:, pl.ds(off, t)] += dv
            return tuple(dqs), tuple(pres)

        zero, zero_q = jnp.zeros((t, 1), F32), jnp.zeros((t, 128), F32)

        def step(j, c):
            ahead = scores(j + 1)
            return block(j, c[0], c[1], False), ahead

        carry, ahead = lax.fori_loop(first, i, step, (((zero_q, zero_q), (zero, zero)), scores(first)))
        dqs, _ = block(i, carry, ahead, True)
        dq_ref[...] = jnp.where(masks[0], dqs[0], dqs[1]) * scale

        @pl.when(i == nblk - 1)
        def _():
            _untranspose(dkt_ref, dk_ref, t, nblk)
            _untranspose(dvt_ref, dv_ref, t, nblk)

    blk = lambda c0: pl.BlockSpec((t, 128), lambda p, i: (i, c0 // 128 + p))
    res = lambda c0: pl.BlockSpec((s, 128), lambda p, i: (0, c0 // 128 + p))
    shp = jax.ShapeDtypeStruct((s, 512), F32)
    c_operands, c_in_specs, c_out_shapes, c_out_specs, c_sems = _comm_operands(comm) if comm else ([], [], [], [], [])
    transposed = [pltpu.VMEM((128, s), F32), pltpu.VMEM((128, s), F32)]
    outs = _pcall(
        _with_comm(body, comm, 10, 4, *_grid_ends(FOX_HEADS // 2, nblk), n_scratch=2), name=name,
        grid=(FOX_HEADS // 2, nblk),
        out_shape=[shp, shp, shp, jax.ShapeDtypeStruct((FOX_HEADS, s), F32)] + c_out_shapes,
        in_specs=[blk(0), res(0), res(C_VB), pl.BlockSpec((t, 128), lambda p, i: (i, 0)),
                  pl.BlockSpec((FOX_HEADS, s), lambda p, i: (0, 0)), pl.BlockSpec((FOX_HEADS, 128), lambda p, i: (0, 0)),
                  pl.BlockSpec((1, 1), lambda p, i: (0, 0)), blk(0),
                  pl.BlockSpec((None, t, 128), lambda p, i: (p, i, 0)), blk(M_FOX)] + c_in_specs,
        out_specs=[blk(0), res(0), res(0), pl.BlockSpec((FOX_HEADS, s), lambda p, i: (0, 0))] + c_out_specs,
        scratch_shapes=transposed + c_sems, input_output_aliases=_comm_aliases(comm, 10, 4),
        compiler_params=_cp(ARB, ARB),
    )(qn, kn, proj, cfc, cfr, cfe, qk_bound, o, lse, dmix, *c_operands)
    return outs[0], outs[1], outs[2], outs[3], list(outs[4:])


def _fox_prep_bwd(proj, gq, gk, bf, dqn, dkn, dlogf, name):
    s = proj.shape[0]
    t = min(ATT_T, s)
    nblk = s // t

    def body(q_ref, k_ref, fl_ref, gq_ref, gk_ref, bf_ref, dqn_ref, dkn_ref, dlogf_ref,
             dq_ref, dk_ref, dfl_ref, dgq_ref, dgk_ref, dbf_ref):
        i = pl.program_id(0)
        gm = _group_matrix(512)

        @pl.when(i == 0)
        def _():
            dgq_ref[...] = jnp.zeros_like(dgq_ref)
            dgk_ref[...] = jnp.zeros_like(dgk_ref)
            dbf_ref[...] = jnp.zeros_like(dbf_ref)

        for src, gref, dyref, dst, dgref in ((q_ref, gq_ref, dqn_ref, dq_ref, dgq_ref),
                                              (k_ref, gk_ref, dkn_ref, dk_ref, dgk_ref)):
            v = src[...]
            dy = dyref[...]
            rstd = lax.rsqrt(_group_mean(v * v, gm) + EPS)
            xhat = v * rstd
            dgref[...] += jnp.sum(dy * xhat, axis=0, keepdims=True)
            dxh = dy * gref[...]
            dst[...] = (rstd * (dxh - xhat * _group_mean(dxh * xhat, gm))).astype(dst.dtype)

        to_lanes = (_iota2((FOX_HEADS, 128), 0) == _iota2((FOX_HEADS, 128), 1)).astype(_BF)
        dlogf = _dot_exact(dlogf_ref[...], to_lanes, 0, 0, 3)
        xv = fl_ref[...] + bf_ref[...]
        e = jnp.exp(-jnp.abs(xv))
        dfl = dlogf * (jnp.where(xv >= 0.0, e, 1.0) / (1.0 + e))
        dfl_ref[...] = dfl.astype(dfl_ref.dtype)
        dbf_ref[...] += jnp.sum(dfl, axis=0, keepdims=True)

    col = lambda w, c0: pl.BlockSpec((t, w), lambda i: (i, c0 // w))
    v512 = pl.BlockSpec((1, 512), lambda i: (0, 0))
    v128 = pl.BlockSpec((1, 128), lambda i: (0, 0))
    return _pcall(
        body, name=name, grid=(nblk,),
        out_shape=[jax.ShapeDtypeStruct((s, 512), _BF), jax.ShapeDtypeStruct((s, 512), _BF),
                   jax.ShapeDtypeStruct((s, 128), _BF), jax.ShapeDtypeStruct((1, 512), F32),
                   jax.ShapeDtypeStruct((1, 512), F32), jax.ShapeDtypeStruct((1, 128), F32)],
        in_specs=[col(512, C_QB), col(512, C_KB), col(128, C_FL), v512, v512, v128, col(512, 0), col(512, 0),
                  pl.BlockSpec((FOX_HEADS, t), lambda i: (0, i))],
        out_specs=[col(512, 0), col(512, 0), col(128, 0), v512, v512, v128], compiler_params=_cp(ARB),
    )(proj, proj, proj, gq, gk, bf, dqn, dkn, dlogf)


GELU_C = 0.7978845608028654
GELU_A = 0.044715


def _gelu(x):
    return 0.5 * x * (1.0 + jnp.tanh(GELU_C * (x + GELU_A * x * x * x)))


def _gelu_grad(x):
    th = jnp.tanh(GELU_C * (x + GELU_A * x * x * x))
    return 0.5 * (1.0 + th) + 0.5 * x * (1.0 - th * th) * (GELU_C * (1.0 + 3.0 * GELU_A * x * x))


def _sgu_tril():
    return _iota2((SGU_CHUNK, SGU_CHUNK), 0) >= _iota2((SGU_CHUNK, SGU_CHUNK), 1)


def _sgu_group_masks():
    lane = _iota2((1, 256), 1)
    return [(lane // HEAD_DIM) == g for g in range(SGU_GROUPS)]


def _sgu_fwd(proj, w, gs, bexp, name):
    s = proj.shape[0]
    t = min(2 * SGU_CHUNK, s)

    def body(u_ref, v_ref, w_ref, gs_ref, b_ref, o_ref):
        gm = _group_matrix(256)
        gmask = _sgu_group_masks()
        tril = _sgu_tril()
        u = _gelu(u_ref[...])
        vg = _gelu(v_ref[...])
        vhat = (vg * lax.rsqrt(_group_mean(vg * vg, gm) + EPS) * gs_ref[...]).astype(_BF)
        for ch in range(t // SGU_CHUNK):
            rows = slice(ch * SGU_CHUNK, (ch + 1) * SGU_CHUNK)
            mixed = b_ref[...]
            for g in range(SGU_GROUPS):
                wg = jnp.where(tril, w_ref[g], 0.0).astype(_BF)
                mixed = jnp.where(gmask[g], mixed + _dot(wg, vhat[rows], 1, 0), mixed)
            o_ref[rows, :] = u[rows] * mixed

    return _pcall(
        body, name=name, grid=(s // t,), out_shape=jax.ShapeDtypeStruct((s, 256), F32),
        in_specs=[_row_spec(t, 256, C_UC // 256), _row_spec(t, 256, C_VC // 256),
                  pl.BlockSpec((SGU_GROUPS, SGU_CHUNK, SGU_CHUNK), lambda i: (0, 0, 0)), _vec_spec(256),
                  pl.BlockSpec((SGU_CHUNK, 256), lambda i: (0, 0))],
        out_specs=_row_spec(t, 256), compiler_params=_cp(PAR),
    )(proj, proj, w, gs, bexp)


def _sgu_bwd(proj, w, gs, bexp, dmix, name):
    s = proj.shape[0]
    t = min(2 * SGU_CHUNK, s)
    nstep = s // t

    def body(u_ref, v_ref, w_ref, gs_ref, b_ref, do_ref, du_ref, dv_ref, dw_ref, db_ref, dgs_ref):
        i = pl.program_id(0)
        gm = _group_matrix(256)
        gmask = _sgu_group_masks()
        tril = _sgu_tril()

        @pl.when(i == 0)
        def _():
            dw_ref[...] = jnp.zeros_like(dw_ref)
            db_ref[...] = jnp.zeros_like(db_ref)
            dgs_ref[...] = jnp.zeros_like(dgs_ref)

        uc = u_ref[...]
        vc = v_ref[...]
        u = _gelu(uc)
        vg = _gelu(vc)
        rstd = lax.rsqrt(_group_mean(vg * vg, gm) + EPS)
        xh = vg * rstd
        gsv = gs_ref[...]
        vhat = (xh * gsv).astype(_BF)
        dov = do_ref[...]
        dm = dov * u
        for ch in range(t // SGU_CHUNK):
            rows = slice(ch * SGU_CHUNK, (ch + 1) * SGU_CHUNK)
            mixed = b_ref[...]
            dvh = jnp.zeros((SGU_CHUNK, 256), F32)
            dmc = dm[rows]
            for g in range(SGU_GROUPS):
                wg = jnp.where(tril, w_ref[g], 0.0).astype(_BF)
                mixed = jnp.where(gmask[g], mixed + _dot(wg, vhat[rows], 1, 0), mixed)
                dvh = jnp.where(gmask[g], _dot(wg, dmc.astype(_BF), 0, 0), dvh)
                dw_ref[g] += _dot(jnp.where(gmask[g], dmc, 0.0).astype(_BF), vhat[rows], 1, 1)
            db_ref[...] += dmc
            du_ref[rows, :] = (dov[rows] * mixed * _gelu_grad(uc[rows])).astype(du_ref.dtype)
            xhc = xh[rows]
            dgs_ref[...] += jnp.sum(dvh * xhc, axis=0, keepdims=True)
            dxh = dvh * gsv
            dvg = rstd[rows] * (dxh - xhc * _group_mean(dxh * xhc, gm))
            dv_ref[rows, :] = (dvg * _gelu_grad(vc[rows])).astype(dv_ref.dtype)

        @pl.when(i == nstep - 1)
        def _():
            for g in range(SGU_GROUPS):
                dw_ref[g] = jnp.where(tril, dw_ref[g], 0.0)

    wspec = pl.BlockSpec((SGU_GROUPS, SGU_CHUNK, SGU_CHUNK), lambda i: (0, 0, 0))
    bspec = pl.BlockSpec((SGU_CHUNK, 256), lambda i: (0, 0))
    return _pcall(
        body, name=name, grid=(nstep,),
        out_shape=[jax.ShapeDtypeStruct((s, 256), _BF), jax.ShapeDtypeStruct((s, 256), _BF),
                   jax.ShapeDtypeStruct((SGU_GROUPS, SGU_CHUNK, SGU_CHUNK), F32),
                   jax.ShapeDtypeStruct((SGU_CHUNK, 256), F32), jax.ShapeDtypeStruct((1, 256), F32)],
        in_specs=[_row_spec(t, 256, C_UC // 256), _row_spec(t, 256, C_VC // 256), wspec, _vec_spec(256), bspec,
                  _row_spec(t, 256, M_SGU // 256)],
        out_specs=[_row_spec(t, 256), _row_spec(t, 256), wspec, bspec, _vec_spec(256)],
        compiler_params=_cp(ARB),
    )(proj, proj, w, gs, bexp, dmix)


def _ada_fwd(c_all, ada_w, name):
    depth, d, n = ada_w.shape

    def body(c_ref, w_ref, o_ref):
        cv = c_ref[...]
        cond = cv / (1.0 + jnp.exp(-cv))
        o_ref[...] = _dot_f32(cond, w_ref[...], 1, 0)

    return _pcall(
        body, name=name, grid=(depth,), out_shape=jax.ShapeDtypeStruct((depth, N_DEV, n), F32),
        in_specs=[pl.BlockSpec((N_DEV, d), lambda l: (0, 0)), pl.BlockSpec((None, d, n), lambda l: (l, 0, 0))],
        out_specs=pl.BlockSpec((None, N_DEV, n), lambda l: (l, 0, 0)), compiler_params=_cp(PAR),
    )(c_all, ada_w)


def _ada_bwd(c_all, dmod, name):
    depth, _, n = dmod.shape
    d = c_all.shape[1]

    def body(c_ref, dm_ref, o_ref):
        cv = c_ref[...]
        cond = cv / (1.0 + jnp.exp(-cv))
        o_ref[...] = _dot_f32(cond, dm_ref[...], 0, 0)

    return _pcall(
        body, name=name, grid=(depth,), out_shape=jax.ShapeDtypeStruct((depth, d, n), F32),
        in_specs=[pl.BlockSpec((N_DEV, d), lambda l: (0, 0)), pl.BlockSpec((None, N_DEV, n), lambda l: (l, 0, 0))],
        out_specs=pl.BlockSpec((None, d, n), lambda l: (l, 0, 0)), compiler_params=_cp(PAR),
    )(c_all, dmod)


def _adamw(slots, w, m, v, name):
    n, r, c = slots.shape
    tr = 256 if r % 256 == 0 else r
    bc1 = 1.0 - ADAM_B1 ** ADAM_STEP
    bc2 = 1.0 - ADAM_B2 ** ADAM_STEP

    def body(s_ref, w_ref, m_ref, v_ref, g_ref, d_ref, nm_ref, nv_ref):
        g = s_ref[0].astype(F32)
        for j in range(1, n):
            g = g + s_ref[j].astype(F32)
        m_new = ADAM_B1 * m_ref[...] + (1.0 - ADAM_B1) * g
        v_new = ADAM_B2 * v_ref[...] + (1.0 - ADAM_B2) * (g * g)
        g_ref[...] = g
        nm_ref[...] = m_new
        nv_ref[...] = v_new
        d_ref[...] = -ADAM_LR * ((m_new / bc1) / (jnp.sqrt(v_new / bc2) + ADAM_EPS) + ADAM_WD * w_ref[...])

    tile = pl.BlockSpec((tr, c), lambda i: (i, 0))
    shp = jax.ShapeDtypeStruct((r, c), F32)
    return _pcall(
        body, name=name, grid=(r // tr,), out_shape=[shp] * 4,
        in_specs=[pl.BlockSpec((n, tr, c), lambda i: (0, i, 0)), tile, tile, tile],
        out_specs=[tile] * 4, compiler_params=_cp(PAR),
    )(slots, w, m, v)


def _w_in_layout(w):
    pad = jnp.zeros(w.shape[:-1] + (IN_P - IN_W,), w.dtype)
    return jnp.concatenate([w[..., 768:FL_SRC], w[..., :768], w[..., UC_SRC:], w[..., FL_SRC:UC_SRC], pad], axis=-1)


def _w_in_unlayout(g):
    return jnp.concatenate([g[..., C_QA:C_UC], g[..., :C_QA], g[..., C_FL:C_FL + FOX_HEADS], g[..., C_UC:C_FL]], axis=-1)


SMALL = [("ada_b", DEPTH * 6 * D_MODEL), ("norm1_g", DEPTH * D_MODEL), ("norm2_g", DEPTH * D_MODEL),
         ("sgu_w", DEPTH * SGU_GROUPS * SGU_CHUNK * SGU_CHUNK), ("sgu_b", DEPTH * SGU_GROUPS * SGU_CHUNK),
         ("sgu_norm_g", DEPTH * SGU_GROUPS * HEAD_DIM), ("q_norm_g", DEPTH * HEAD_DIM), ("k_norm_g", DEPTH * HEAD_DIM),
         ("b_forget", DEPTH * FOX_HEADS), ("loss", 1)]
SMALL_ROWS = 2560


def _pack_small(parts):
    flat = jnp.concatenate([parts[name].reshape(-1).astype(F32) for name, _ in SMALL])
    return jnp.pad(flat, (0, SMALL_ROWS * 128 - flat.shape[0])).reshape(SMALL_ROWS, 128)


def _unpack_small(packed, shapes):
    flat = packed.reshape(-1)
    out, off = {}, 0
    for name, size in SMALL:
        out[name] = flat[off:off + size].reshape(shapes[name])
        off += size
    return out


def kernel(x, c, ada_w, ada_b, norm1_g, norm2_g, w_in, b_forget, q_norm_g, k_norm_g, sgu_norm_g, sgu_w, sgu_b, w_out, mlp_w1, mlp_w2, loss_target, m_ada_w, m_ada_b, m_norm1_g, m_norm2_g, m_w_in, m_b_forget, m_q_norm_g, m_k_norm_g, m_sgu_norm_g, m_sgu_w, m_sgu_b, m_w_out, m_mlp_w1, m_mlp_w2, v_ada_w, v_ada_b, v_norm1_g, v_norm2_g, v_w_in, v_b_forget, v_q_norm_g, v_k_norm_g, v_sgu_norm_g, v_sgu_w, v_sgu_b, v_w_out, v_mlp_w1, v_mlp_w2):
    me = _lin(_my_pos())
    x0 = x[0]
    target = loss_target[0]
    n_ada = ada_w.shape[2]

    shards = [w.astype(_XBF) for w in (w_in, w_out, mlp_w1, mlp_w2)]

    def whole_in(w_in_g):
        return _w_in_layout(jnp.transpose(w_in_g, (1, 0, 2)).reshape(D_MODEL, IN_W))

    def whole_rest(w_out_g, w1_g, w2_g):
        return (w_out_g.reshape(D_MODEL, D_MODEL), jnp.transpose(w1_g, (1, 0, 2)).reshape(D_MODEL, D_FF),
                w2_g.reshape(D_FF, D_MODEL))

    w_in_first, c_all = _exchange([shards[0][0], c], False, "gather_weights")
    c_all = c_all.reshape(N_DEV, D_MODEL)
    weights = [None] * DEPTH

    mod_part = _ada_fwd(c_all, ada_w, "ada_fwd")
    (mod_rows,) = _exchange([jnp.transpose(mod_part, (1, 0, 2))], True, "scatter_mod")
    mod = jnp.transpose(mod_rows, (1, 0, 2)).reshape(DEPTH, 6 * D_MODEL) + ada_b
    mods = mod.reshape(DEPTH, 6, 1, D_MODEL)

    saved = []
    xl = x0
    for l in range(DEPTH):
        sh1, sc1, g1, sh2, sc2, g2 = (mods[l, r] for r in range(6))
        n1 = norm1_g[l].reshape(1, D_MODEL)
        n2 = norm2_g[l].reshape(1, D_MODEL)
        gq = jnp.tile(q_norm_g[l], FOX_HEADS).reshape(1, 512)
        gk = jnp.tile(k_norm_g[l], FOX_HEADS).reshape(1, 512)
        bf = jnp.pad(b_forget[l], (0, 128 - FOX_HEADS)).reshape(1, 128)
        gs = sgu_norm_g[l].reshape(1, 256)
        bexp = jnp.repeat(sgu_b[l].T, HEAD_DIM, axis=1)

        if l == 0:
            w_in_l = whole_in(w_in_first)
            gather_rest = dict(arrays=[w[0] for w in shards[1:]], scatter=False, recv=None, slot=None)
        else:
            w_in_l, gather_rest = weights[l][0], None
        h1, h1_t = _norm_mod(xl, n1, sc1, sh1, "norm_mod")
        proj = _mm_nn(h1, w_in_l, "plain", (), "mm_in")
        o_sb, gathered = _sb_fwd(proj, "sb_fwd", gather_rest)
        if gathered:
            weights[0] = (w_in_l,) + whole_rest(*gathered)
        qn, kn, cfc, cfr, cfe = _fox_prep(proj, gq, gk, bf, "fox_prep")
        qkb = (1.02 * HEAD_DIM ** 0.5 * jnp.max(jnp.abs(q_norm_g[l])) * jnp.max(jnp.abs(k_norm_g[l]))).reshape(1, 1)
        gather_next = None
        if l + 1 < DEPTH:
            gather_next = dict(arrays=[w[l + 1] for w in shards], scatter=False, recv=None, slot=None)
        o_fox, lse, gathered = _fox_fwd(proj, qn, kn, cfc, cfr, cfe, qkb, "fox_fwd", gather_next)
        if gathered:
            weights[l + 1] = (whole_in(gathered[0]),) + whole_rest(*gathered[1:])
        _, w_out_l, w1_l, w2_l = weights[l]
        o_sgu = _sgu_fwd(proj, sgu_w[l], gs, bexp, "sgu_fwd")
        x_mid, y1, mixed_t = _mm_out((o_sb, o_fox, o_sgu), w_out_l, xl, g1, "mm_out")
        h2, h2_t = _norm_mod(x_mid, n2, sc2, sh2, "norm_mod")
        r2, r1, r2_t = _mm_nn(h2, w1_l, "relu2", (), "mm_w1")
        x_out, y2 = _mm_nn(r2, w2_l, "resid", (x_mid, g2), "mm_w2")
        saved.append(dict(x_in=xl, h1_t=h1_t, proj=proj, o_sb=o_sb, qn=qn, kn=kn, cfc=cfc, cfr=cfr, cfe=cfe, o_fox=o_fox,
                          lse=lse, mixed_t=mixed_t, x_mid=x_mid, y1=y1, h2_t=h2_t, r2_t=r2_t, r1=r1, y2=y2,
                          n1=n1, n2=n2, gq=gq, gk=gk, bf=bf, gs=gs, bexp=bexp, qkb=qkb))
        xl = x_out

    loss_part, dx, dy2, dg2 = _loss_head(xl, target, (saved[-1]["y2"], mods[DEPTH - 1, 5]), "loss_head")

    grads_ready = None
    received = [jnp.zeros((N_DEV,) + w.shape, _XBF) for w in shards]
    small = {k: [None] * DEPTH for k in ("mod", "norm1_g", "norm2_g", "sgu_w", "sgu_b", "sgu_norm_g",
                                         "q_norm_g", "k_norm_g", "b_forget")}
    for l in reversed(range(DEPTH)):
        sv = saved[l]
        sh1, sc1, g1, sh2, sc2, g2 = (mods[l, r] for r in range(6))
        w_in_l, w_out_l, w1_l, w2_l = weights[l]
        da = _mm_nt(dy2, w2_l, "mul2", (sv["r1"],), _BF, "mm_w2_bwd")
        g_w2 = _mm_grad(sv["r2_t"], dy2, "rows", "mm_w2_grad")
        dh2 = _mm_nt(da, w1_l, "plain", (), F32, "mm_w1_bwd")
        g_w1 = _mm_grad(sv["h2_t"], da, "cols", "mm_w1_grad")
        dx_mid, dsh2, dsc2, dn2, dy1, dg1 = _norm_mod_bwd(sv["x_mid"], dh2, dx, sv["n2"], sc2, (sv["y1"], g1),
                                                          "norm_mod_bwd")
        dmix = _mm_nt(dy1, w_out_l, "plain", (), F32, "mm_out_bwd")
        g_out = _mm_grad(sv["mixed_t"], dy1, "rows", "mm_out_grad")
        send_own = None
        if l == 0:
            send_own = dict(arrays=[g_out, g_w1, g_w2], scatter=True, recv=received[1:], slot=0)
        dq_sb, dk_sb, dv_sb, filled = _sb_bwd(sv["proj"], sv["o_sb"], dmix, "sb_bwd", send_own)
        if filled:
            received = received[:1] + filled
        send_prev = None
        if grads_ready is not None:
            send_prev = dict(arrays=grads_ready, scatter=True, recv=received, slot=l + 1)
        dqn, dkn, dv_fox, dlogf, filled = _fox_bwd(sv["proj"], sv["qn"], sv["kn"], sv["cfc"], sv["cfr"], sv["cfe"], sv["qkb"],
                                                 sv["o_fox"], sv["lse"], dmix, "fox_bwd", send_prev)
        if filled:
            received = filled
        dq_fox, dk_fox, dfl, dgq, dgk, dbf = _fox_prep_bwd(sv["proj"], sv["gq"], sv["gk"], sv["bf"], dqn, dkn, dlogf,
                                                           "fox_prep_bwd")
        duc, dvc, dsw, dsb, dsg = _sgu_bwd(sv["proj"], sgu_w[l], sv["gs"], sv["bexp"], dmix, "sgu_bwd")
        dproj = jnp.concatenate(
            [dq_fox, dk_fox, dv_fox.astype(_BF), dq_sb.astype(_BF), dk_sb.astype(_BF), dv_sb.astype(_BF), duc, dvc, dfl,
             jnp.zeros((dfl.shape[0], IN_P - C_FL - 128), _BF)], axis=-1)
        dh1 = _mm_nt(dproj, w_in_l, "plain", (), F32, "mm_in_bwd")
        g_in = _w_in_unlayout(_mm_grad(sv["h1_t"], dproj, None, "mm_in_grad"))
        g_in = jnp.transpose(g_in.reshape(D_MODEL, N_DEV, IN_W // N_DEV), (1, 0, 2))
        grads_ready = [g_in, g_out, g_w1, g_w2]
        small["mod"][l] = [dg1, dsh2, dsc2, dg2]
        below = (saved[l - 1]["y2"], mods[l - 1, 5]) if l > 0 else None
        dx, dsh1, dsc1, dn1, *below_grads = _norm_mod_bwd(sv["x_in"], dh1, dx_mid, sv["n1"], sc1, below, "norm_mod_bwd")
        if below_grads:
            dy2, dg2 = below_grads
        small["mod"][l] = jnp.concatenate([dsh1, dsc1] + small["mod"][l], axis=-1).reshape(-1)
        small["norm1_g"][l] = dn1.reshape(-1)
        small["norm2_g"][l] = dn2.reshape(-1)
        small["sgu_w"][l] = dsw
        small["sgu_b"][l] = dsb.reshape(SGU_CHUNK, SGU_GROUPS, HEAD_DIM).sum(-1).T
        small["sgu_norm_g"][l] = dsg.reshape(SGU_GROUPS, HEAD_DIM)
        small["q_norm_g"][l] = dgq.reshape(FOX_HEADS, HEAD_DIM).sum(0)
        small["k_norm_g"][l] = dgk.reshape(FOX_HEADS, HEAD_DIM).sum(0)
        small["b_forget"][l] = dbf[0, :FOX_HEADS]

    parts = {k: jnp.stack(v) for k, v in small.items()}
    parts["ada_b"] = parts.pop("mod")
    parts["loss"] = loss_part
    (small_all,) = _exchange([_pack_small(parts)], False, "gather_small")
    zero1 = jnp.zeros((1,), F32)
    packs = [_pack_small(dict(ada_b=a, norm1_g=b, norm2_g=cc, sgu_w=d, sgu_b=e, sgu_norm_g=f, q_norm_g=g, k_norm_g=h,
                              b_forget=i, loss=zero1))
             for a, b, cc, d, e, f, g, h, i in (
                 (ada_b, norm1_g, norm2_g, sgu_w, sgu_b, sgu_norm_g, q_norm_g, k_norm_g, b_forget),
                 (m_ada_b, m_norm1_g, m_norm2_g, m_sgu_w, m_sgu_b, m_sgu_norm_g, m_q_norm_g, m_k_norm_g, m_b_forget),
                 (v_ada_b, v_norm1_g, v_norm2_g, v_sgu_w, v_sgu_b, v_sgu_norm_g, v_q_norm_g, v_k_norm_g, v_b_forget))]
    small_shapes = dict(ada_b=ada_b.shape, norm1_g=norm1_g.shape, norm2_g=norm2_g.shape, sgu_w=sgu_w.shape,
                        sgu_b=sgu_b.shape, sgu_norm_g=sgu_norm_g.shape, q_norm_g=q_norm_g.shape,
                        k_norm_g=k_norm_g.shape, b_forget=b_forget.shape, loss=())
    s_out = [_unpack_small(a, small_shapes) for a in _adamw(small_all, *packs, "adamw_small")]

    dmod_all = small_all[:, :DEPTH * 6 * D_MODEL // 128, :].reshape(N_DEV, DEPTH, 6 * D_MODEL)
    dmod_mine = lax.dynamic_slice_in_dim(dmod_all, me * n_ada, n_ada, axis=2)
    g_ada = _ada_bwd(c_all, jnp.transpose(dmod_mine, (1, 0, 2)), "ada_bwd")
    ada_out = _adamw(g_ada.reshape(1, DEPTH * D_MODEL, n_ada), ada_w.reshape(-1, n_ada), m_ada_w.reshape(-1, n_ada),
                     v_ada_w.reshape(-1, n_ada), "adamw_ada")
    ada_out = [a.reshape(ada_w.shape) for a in ada_out]

    (ri,) = _exchange(grads_ready[:1], True, "scatter_grads", recv=received[:1], slot=0)
    _, ro, r1, r2 = received

    def big(slots, w, m, v, name):
        cdim = w.shape[-1]
        outs = _adamw(slots.reshape(N_DEV, -1, cdim), w.reshape(-1, cdim), m.reshape(-1, cdim), v.reshape(-1, cdim), name)
        return [a.reshape(w.shape) for a in outs]

    in_out = big(ri, w_in, m_w_in, v_w_in, "adamw_in")
    out_out = big(ro, w_out, m_w_out, v_w_out, "adamw_out")
    w1_out = big(r1, mlp_w1, m_mlp_w1, v_mlp_w1, "adamw_w1")
    w2_out = big(r2, mlp_w2, m_mlp_w2, v_mlp_w2, "adamw_w2")

    def leaves(idx):
        sm = s_out[idx]
        return [ada_out[idx], sm["ada_b"], sm["norm1_g"], sm["norm2_g"], in_out[idx], sm["b_forget"], sm["q_norm_g"],
                sm["k_norm_g"], sm["sgu_norm_g"], sm["sgu_w"], sm["sgu_b"], out_out[idx], w1_out[idx], w2_out[idx]]

    loss = s_out[0]["loss"]
    grad_x = dx.reshape(x.shape)
    return (loss, grad_x, *leaves(0), *leaves(1), *leaves(2), *leaves(3))
```

```python
import jax
import jax.numpy as jnp
from jax import lax
from jax.experimental import pallas as pl
from jax.experimental.pallas import tpu as pltpu

F32 = jnp.float32
_BF = jnp.bfloat16
_XBF = jnp.bfloat16

N_DEV = 8
D_MODEL = 1024
DEPTH = 4
HEAD_DIM = 64
SB_HEADS = 4
FOX_HEADS = 8
SGU_GROUPS = 4
SGU_CHUNK = 128
D_FF = 4096
EPS = 1e-6
IN_W = 2824
FL_SRC = 2304
UC_SRC = 2312

C_QB, C_KB, C_VB = 0, 512, 1024
C_QA, C_KA, C_VA = 1536, 1792, 2048
C_UC, C_VC = 2304, 2560
C_FL = 2816
IN_P = 3072
M_SB, M_FOX, M_SGU = 0, 256, 768

ADAM_LR = 0.001
ADAM_B1 = 0.9
ADAM_B2 = 0.999
ADAM_EPS = 1e-08
ADAM_WD = 0.01
ADAM_STEP = 10

ATT_T = 256
ROW_T = 512
MM_T = 1024
GRAD_TS = 4096
ROW_CHUNK = 64
NEG = -1e30
UNDERFLOW = -90.0
NO_BLOCK = 1e30

MESH = pl.DeviceIdType.MESH
ARB = pltpu.ARBITRARY
PAR = pltpu.PARALLEL
HIGHEST = lax.Precision.HIGHEST


def _pcall(body, **kw):
    return pl.pallas_call(body, **kw)


def _cp(*sem):
    return pltpu.CompilerParams(dimension_semantics=tuple(sem))


def _dot(a, b, ca, cb):
    return lax.dot_general(a, b, (((ca,), (cb,)), ((), ())), preferred_element_type=F32)


def _dot_f32(a, b, ca, cb):
    return lax.dot_general(a, b, (((ca,), (cb,)), ((), ())), precision=HIGHEST, preferred_element_type=F32)


def _split(v, parts):
    out = []
    r = v
    for _ in range(parts - 1):
        p = r.astype(_BF)
        out.append(p)
        r = r - p.astype(F32)
    out.append(r.astype(_BF))
    return out


def _dot_exact(v, m, ca, cb, parts, v_left=True):
    out = None
    for p in _split(v, parts):
        term = _dot(p, m, ca, cb) if v_left else _dot(m, p, ca, cb)
        out = term if out is None else out + term
    return out


def _iota2(shape, dim):
    return lax.broadcasted_iota(jnp.int32, shape, dim)


def _my_pos():
    return lax.axis_index("x"), lax.axis_index("y"), lax.axis_index("c")


def _flip(pos, k):
    x, y, c = pos
    px = 1 - x if (k >> 2) & 1 else x
    py = 1 - y if (k >> 1) & 1 else y
    pc = 1 - c if k & 1 else c
    return px, py, pc


def _lin(pos):
    return 4 * pos[0] + 2 * pos[1] + pos[2]


def _exchange_ops(ins, outs, sems, scatter, slot):
    send_sems, recv_sems, local_sems = sems
    n = len(ins)

    def copies(with_recvs):
        me = _my_pos()
        me_i = _lin(me)
        dst = lambda a, j: outs[a].at[j] if slot is None else outs[a].at[j, slot]
        mine = lambda a: ins[a].at[me_i] if scatter else ins[a]
        local = [pltpu.make_async_copy(mine(a), dst(a, me_i), local_sems.at[a]) for a in range(n)]
        sends, recvs = [], []
        for a in range(n):
            for k in range(1, N_DEV):
                peer = _flip(me, k)
                pair = dict(send_sem=send_sems.at[a, k - 1], recv_sem=recv_sems.at[a, k - 1],
                            device_id=peer, device_id_type=MESH)
                src = ins[a].at[_lin(peer)] if scatter else ins[a]
                sends.append(pltpu.make_async_remote_copy(src_ref=src, dst_ref=dst(a, me_i), **pair))
                if with_recvs:
                    recvs.append(pltpu.make_async_remote_copy(src_ref=mine(a), dst_ref=dst(a, _lin(peer)), **pair))
        return local, sends, recvs

    def start():
        local, sends, _ = copies(False)
        for cp in local + sends:
            cp.start()

    def wait():
        local, sends, recvs = copies(True)
        for snd, rcv in zip(sends, recvs):
            snd.wait_send()
            rcv.wait_recv()
        for cp in local:
            cp.wait()

    return start, wait


def _comm_operands(comm):
    arrays, recv = comm["arrays"], comm["recv"]
    n = len(arrays)
    any_spec = pl.BlockSpec(memory_space=pl.ANY)
    if recv is not None:
        out_shapes = [jax.ShapeDtypeStruct(r.shape, r.dtype) for r in recv]
    elif comm["scatter"]:
        out_shapes = [jax.ShapeDtypeStruct(a.shape, a.dtype) for a in arrays]
    else:
        out_shapes = [jax.ShapeDtypeStruct((N_DEV,) + a.shape, a.dtype) for a in arrays]
    operands = list(arrays) + (list(recv) if recv is not None else [])
    sems = [pltpu.SemaphoreType.DMA((n, N_DEV - 1)), pltpu.SemaphoreType.DMA((n, N_DEV - 1)),
            pltpu.SemaphoreType.DMA((n,))]
    return operands, [any_spec] * len(operands), out_shapes, [any_spec] * n, sems


def _with_comm(body, comm, n_in, n_out, first_step, last_step, n_scratch=0):
    if comm is None:
        return body
    n = len(comm["arrays"])
    n_cin = n if comm["recv"] is None else 2 * n

    def wrapped(*refs):
        ins, cins = refs[:n_in], refs[n_in:n_in + n]
        o0 = n_in + n_cin
        outs, couts = refs[o0:o0 + n_out], refs[o0 + n_out:o0 + n_out + n]
        s0 = o0 + n_out + n
        scratch, sems = refs[s0:s0 + n_scratch], refs[s0 + n_scratch:]
        start, wait = _exchange_ops(cins, couts, sems, comm["scatter"], comm["slot"])
        pl.when(first_step())(start)
        body(*ins, *outs, *scratch)
        pl.when(last_step())(wait)

    return wrapped


def _comm_aliases(comm, n_in, n_out):
    if comm is None or comm["recv"] is None:
        return {}
    n = len(comm["arrays"])
    return {n_in + n + a: n_out + a for a in range(n)}


def _exchange(arrays, scatter, name, recv=None, slot=None):
    comm = dict(arrays=arrays, scatter=scatter, recv=recv, slot=slot)
    operands, in_specs, out_shapes, out_specs, sems = _comm_operands(comm)
    n = len(arrays)

    def body(*refs):
        n_cin = len(operands)
        start, wait = _exchange_ops(refs[:n], refs[n_cin:n_cin + n], refs[n_cin + n:], scatter, slot)
        start()
        wait()

    outs = _pcall(body, name=name, out_shape=out_shapes, in_specs=in_specs, out_specs=out_specs, scratch_shapes=sems,
                  input_output_aliases=_comm_aliases(comm, 0, 0))(*operands)
    return list(outs)


def _row_spec(t, w, col=0):
    return pl.BlockSpec((t, w), lambda i: (i, col))


def _vec_spec(w, col=0):
    return pl.BlockSpec((1, w), lambda i: (0, col))


def _norm_mod(x, g, sc, sh, name):
    s, d = x.shape
    t = min(ROW_T, s)

    def body(x_ref, g_ref, sc_ref, sh_ref, h_ref, ht_ref):
        xf = x_ref[...]
        rstd = lax.rsqrt(jnp.mean(xf * xf, axis=-1, keepdims=True) + EPS)
        y = xf * rstd * g_ref[...]
        h = y * (1.0 + sc_ref[...]) + sh_ref[...]
        h_ref[...] = h.astype(h_ref.dtype)
        ht_ref[...] = h.T.astype(ht_ref.dtype)

    return _pcall(
        body, name=name, grid=(s // t,),
        out_shape=[jax.ShapeDtypeStruct((s, d), _BF), jax.ShapeDtypeStruct((d, s), _BF)],
        in_specs=[_row_spec(t, d), _vec_spec(d), _vec_spec(d), _vec_spec(d)],
        out_specs=[_row_spec(t, d), pl.BlockSpec((d, t), lambda i: (0, i))], compiler_params=_cp(PAR),
    )(x, g, sc, sh)


def _res_bwd_step(i, dx, y_ref, gate_ref, dy_ref, dgate_ref):
    dy_ref[...] = (dx * gate_ref[...]).astype(dy_ref.dtype)

    @pl.when(i == 0)
    def _():
        dgate_ref[...] = jnp.zeros_like(dgate_ref)

    dgate_ref[...] += jnp.sum(dx * y_ref[...], axis=0, keepdims=True)


def _norm_mod_bwd(x, dh, dres, g, sc, branch, name):
    s, d = x.shape
    t = min(ROW_T, s)
    n_branch = 0 if branch is None else 2

    ch = min(ROW_CHUNK, t)

    def body(x_ref, dh_ref, dres_ref, g_ref, sc_ref, *rest):
        dx_ref, dsh_ref, dsc_ref, dg_ref = rest[n_branch:n_branch + 4]
        i = pl.program_id(0)
        gv = g_ref[...]
        sc1 = 1.0 + sc_ref[...]
        gain = gv * sc1
        n_sum = 2 if branch is None else 3

        def rows(r, sums):
            sl = pl.ds(pl.multiple_of(r * ch, ch), ch)
            xf = x_ref[sl, :]
            dh = dh_ref[sl, :]
            rstd = lax.rsqrt(jnp.mean(xf * xf, axis=-1, keepdims=True) + EPS)
            xhat = xf * rstd
            dh_xhat = dh * xhat
            dxh = dh * gain
            dx = dres_ref[sl, :] + rstd * (dxh - xhat * jnp.mean(dh_xhat * gain, axis=-1, keepdims=True))
            dx_ref[sl, :] = dx
            new = [sums[0] + jnp.sum(dh, axis=0, keepdims=True), sums[1] + jnp.sum(dh_xhat, axis=0, keepdims=True)]
            if branch is not None:
                rest[n_branch + 4][sl, :] = (dx * rest[1][...]).astype(_BF)
                new.append(sums[2] + jnp.sum(dx * rest[0][sl, :], axis=0, keepdims=True))
            return tuple(new)

        sums = lax.fori_loop(0, t // ch, rows, tuple(jnp.zeros((1, d), F32) for _ in range(n_sum)))
        parts = [sums[0], sums[1] * gv, sums[1] * sc1] + list(sums[2:])
        totals = [dsh_ref, dsc_ref, dg_ref] + ([rest[n_branch + 5]] if branch is not None else [])

        @pl.when(i == 0)
        def _():
            for ref in totals:
                ref[...] = jnp.zeros_like(ref)

        for ref, part in zip(totals, parts):
            ref[...] += part

    vec = jax.ShapeDtypeStruct((1, d), F32)
    out_shape = [jax.ShapeDtypeStruct((s, d), F32), vec, vec, vec]
    in_specs = [_row_spec(t, d), _row_spec(t, d), _row_spec(t, d), _vec_spec(d), _vec_spec(d)]
    out_specs = [_row_spec(t, d), _vec_spec(d), _vec_spec(d), _vec_spec(d)]
    if branch is not None:
        in_specs += [_row_spec(t, d), _vec_spec(d)]
        out_shape += [jax.ShapeDtypeStruct((s, d), _BF), vec]
        out_specs += [_row_spec(t, d), _vec_spec(d)]
    return _pcall(
        body, name=name, grid=(s // t,), out_shape=out_shape, in_specs=in_specs, out_specs=out_specs,
        compiler_params=_cp(ARB),
    )(x, dh, dres, g, sc, *(branch or ()))


def _loss_head(y, target, branch, name):
    s, d = y.shape
    t = min(ROW_T, s)

    def body(y_ref, t_ref, yb_ref, gate_ref, loss_ref, dy_ref, dyb_ref, dgate_ref):
        i = pl.program_id(0)
        diff = y_ref[...] - t_ref[...]
        dy = diff * (1.0 / d)
        dy_ref[...] = dy

        @pl.when(i == 0)
        def _():
            loss_ref[...] = jnp.zeros_like(loss_ref)

        rows = jnp.sum(diff * diff, axis=-1, keepdims=True)
        loss_ref[...] += (0.5 / d) * jnp.sum(rows, axis=0, keepdims=True)
        _res_bwd_step(i, dy, yb_ref, gate_ref, dyb_ref, dgate_ref)

    return _pcall(
        body, name=name, grid=(s // t,),
        out_shape=[jax.ShapeDtypeStruct((1, 1), F32), jax.ShapeDtypeStruct((s, d), F32),
                   jax.ShapeDtypeStruct((s, d), _BF), jax.ShapeDtypeStruct((1, d), F32)],
        in_specs=[_row_spec(t, d), _row_spec(t, d), _row_spec(t, d), _vec_spec(d)],
        out_specs=[pl.BlockSpec((1, 1), lambda i: (0, 0)), _row_spec(t, d), _row_spec(t, d), _vec_spec(d)],
        compiler_params=_cp(ARB),
    )(y, target, *branch)


def _contract_tile(kdim):
    if kdim > MM_T and (kdim // 2) % 128 == 0:
        return kdim // 2
    return min(MM_T, kdim)


def _mm_nn(a, b, epi, extras, name):
    m, kdim = a.shape
    n = b.shape[1]
    tm, tn, tk = min(MM_T, m), min(MM_T, n), _contract_tile(kdim)
    nk = kdim // tk
    n_extra = len(extras)
    n_out = {"plain": 1, "resid": 2, "relu2": 3}[epi]

    def finish(y, extra_refs, out_refs):
        if epi == "plain":
            out_refs[0][...] = y.astype(out_refs[0].dtype)
        elif epi == "resid":
            x_ref, g_ref = extra_refs
            out_refs[0][...] = x_ref[...] + g_ref[...] * y
            out_refs[1][...] = y
        else:
            r = jnp.maximum(y, 0.0)
            out_refs[0][...] = (r * r).astype(out_refs[0].dtype)
            out_refs[1][...] = r.astype(out_refs[1].dtype)
            out_refs[2][...] = (r * r).T.astype(out_refs[2].dtype)

    def body(a_ref, b_ref, *rest):
        extra_refs = rest[:n_extra]
        out_refs = rest[n_extra:n_extra + n_out]
        part = _dot(a_ref[...].astype(_BF), b_ref[...].astype(_BF), 1, 0)
        if nk == 1:
            finish(part, extra_refs, out_refs)
        else:
            acc_ref = rest[-1]
            k = pl.program_id(2)

            @pl.when(k == 0)
            def _():
                acc_ref[...] = part

            @pl.when(k > 0)
            def _():
                acc_ref[...] += part

            @pl.when(k == nk - 1)
            def _():
                finish(acc_ref[...], extra_refs, out_refs)

    tile = pl.BlockSpec((tm, tn), lambda i, j, k: (i, j))
    in_specs = [pl.BlockSpec((tm, tk), lambda i, j, k: (i, k)), pl.BlockSpec((tk, tn), lambda i, j, k: (k, j))]
    if epi == "plain":
        out_shape = [jax.ShapeDtypeStruct((m, n), F32)]
    elif epi == "resid":
        in_specs += [tile, pl.BlockSpec((1, tn), lambda i, j, k: (0, j))]
        out_shape = [jax.ShapeDtypeStruct((m, n), F32)] * 2
    out_specs = [tile] * n_out
    if epi == "relu2":
        out_shape = [jax.ShapeDtypeStruct((m, n), _BF), jax.ShapeDtypeStruct((m, n), F32),
                     jax.ShapeDtypeStruct((n, m), _BF)]
        out_specs[2] = pl.BlockSpec((tn, tm), lambda i, j, k: (j, i))
    outs = _pcall(
        body, name=name, grid=(m // tm, n // tn, nk), out_shape=out_shape,
        in_specs=in_specs, out_specs=out_specs,
        scratch_shapes=[pltpu.VMEM((tm, tn), F32)] if nk > 1 else [],
        compiler_params=_cp(PAR, PAR, ARB),
    )(a, b, *extras)
    return outs[0] if n_out == 1 else outs


def _mm_out(parts, w, x, gate, name):
    s, n = x.shape
    widths = [p.shape[1] for p in parts]
    kdim = sum(widths)
    tm = min(ROW_T, s)

    def body(a0_ref, a1_ref, a2_ref, w_ref, x_ref, g_ref, xo_ref, y_ref, mt_ref):
        mixed = jnp.concatenate([a0_ref[...], a1_ref[...], a2_ref[...]], axis=-1)
        y = _dot(mixed.astype(_BF), w_ref[...], 1, 0)
        xo_ref[...] = x_ref[...] + g_ref[...] * y
        y_ref[...] = y
        mt_ref[...] = mixed.T.astype(mt_ref.dtype)

    row = lambda width: pl.BlockSpec((tm, width), lambda i: (i, 0))
    return _pcall(
        body, name=name, grid=(s // tm,),
        out_shape=[jax.ShapeDtypeStruct((s, n), F32), jax.ShapeDtypeStruct((s, n), F32),
                   jax.ShapeDtypeStruct((kdim, s), _BF)],
        in_specs=[row(widths[0]), row(widths[1]), row(widths[2]), pl.BlockSpec((kdim, n), lambda i: (0, 0)),
                  row(n), pl.BlockSpec((1, n), lambda i: (0, 0))],
        out_specs=[row(n), row(n), pl.BlockSpec((kdim, tm), lambda i: (0, i))], compiler_params=_cp(PAR),
    )(*parts, w, x, gate)


def _mm_nt(a, b, epi, extras, out_dtype, name):
    m, kdim = a.shape
    n = b.shape[0]
    tm, tn, tk = min(MM_T, m), min(MM_T, n), _contract_tile(kdim)
    nk = kdim // tk
    n_extra = len(extras)

    def finish(y, extra_refs, o_ref):
        if epi == "mul2":
            y = y * (2.0 * extra_refs[0][...].astype(F32))
        o_ref[...] = y.astype(o_ref.dtype)

    def body(a_ref, b_ref, *rest):
        extra_refs = rest[:n_extra]
        o_ref = rest[n_extra]
        part = _dot(a_ref[...].astype(_BF), b_ref[...].astype(_BF), 1, 1)
        if nk == 1:
            finish(part, extra_refs, o_ref)
        else:
            acc_ref = rest[-1]
            k = pl.program_id(2)

            @pl.when(k == 0)
            def _():
                acc_ref[...] = part

            @pl.when(k > 0)
            def _():
                acc_ref[...] += part

            @pl.when(k == nk - 1)
            def _():
                finish(acc_ref[...], extra_refs, o_ref)

    tile = pl.BlockSpec((tm, tn), lambda i, j, k: (i, j))
    in_specs = [pl.BlockSpec((tm, tk), lambda i, j, k: (i, k)), pl.BlockSpec((tn, tk), lambda i, j, k: (j, k))]
    in_specs += [tile] * n_extra
    return _pcall(
        body, name=name, grid=(m // tm, n // tn, nk), out_shape=jax.ShapeDtypeStruct((m, n), out_dtype),
        in_specs=in_specs, out_specs=tile,
        scratch_shapes=[pltpu.VMEM((tm, tn), F32)] if nk > 1 else [],
        compiler_params=_cp(PAR, PAR, ARB),
    )(a, b, *extras)


def _mm_grad(at, b, split, name):
    m, s = at.shape
    n = b.shape[1]
    ts = min(GRAD_TS, s)
    ns = s // ts
    if split == "rows":
        tm, tn = min(MM_T, m // N_DEV), min(MM_T, n)
        per = (m // N_DEV) // tm
        out_shape = (N_DEV, m // N_DEV, n)
        out_spec = pl.BlockSpec((None, tm, tn), lambda i, j, k: (i // per, i % per, j))
    elif split == "cols":
        tm, tn = min(MM_T, m), min(MM_T, n // N_DEV)
        per = (n // N_DEV) // tn
        out_shape = (N_DEV, m, n // N_DEV)
        out_spec = pl.BlockSpec((None, tm, tn), lambda i, j, k: (j // per, i, j % per))
    else:
        tm, tn = min(MM_T, m), min(MM_T, n)
        out_shape = (m, n)
        out_spec = pl.BlockSpec((tm, tn), lambda i, j, k: (i, j))

    def body(a_ref, b_ref, o_ref, acc_ref):
        k = pl.program_id(2)
        part = _dot(a_ref[...].astype(_BF), b_ref[...].astype(_BF), 1, 0)

        @pl.when(k == 0)
        def _():
            acc_ref[...] = part

        @pl.when(k > 0)
        def _():
            acc_ref[...] += part

        @pl.when(k == ns - 1)
        def _():
            o_ref[...] = acc_ref[...].astype(o_ref.dtype)

    return _pcall(
        body, name=name, grid=(m // tm, n // tn, ns), out_shape=jax.ShapeDtypeStruct(out_shape, _XBF),
        in_specs=[pl.BlockSpec((tm, ts), lambda i, j, k: (i, k)), pl.BlockSpec((ts, tn), lambda i, j, k: (k, j))],
        out_specs=out_spec, scratch_shapes=[pltpu.VMEM((tm, tn), F32)],
        compiler_params=_cp(PAR, PAR, ARB),
    )(at, b)


def _head_masks():
    lane = _iota2((1, 128), 1)
    return [(lane // HEAD_DIM) == hh for hh in range(2)]


def _untranspose(src_ref, dst_ref, t, nblk):
    def rows(c, _):
        off = pl.multiple_of(c * t, t)
        dst_ref[pl.ds(off, t), :] = src_ref[:, pl.ds(off, t)].T
        return 0

    lax.fori_loop(0, nblk, rows, 0)


def _softplus_parts(z):
    sp = jnp.maximum(z, 0.0) + jnp.log(1.0 + jnp.exp(-jnp.abs(z)))
    return -sp, z - sp


def _sb_key_loop(i, carry, step, lookahead, first_ahead):
    def cond(c):
        return jnp.logical_and(c[0] <= i, jnp.max(jnp.maximum(c[1][0][0], c[1][0][1])) > UNDERFLOW)

    def body(c):
        ahead = lookahead(i - c[0] - 1)
        return c[0] + 1, step(i - c[0], c[1], c[2]), ahead

    return lax.while_loop(cond, body, (jnp.int32(1), carry, first_ahead))[1]


def _sb_fwd(proj, name, comm=None):
    s = proj.shape[0]
    t = min(ATT_T, s)
    nblk = s // t
    scale = HEAD_DIM ** -0.5

    def body(q_ref, k_ref, v_ref, o_ref):
        i = pl.program_id(1)
        masks = _head_masks()
        row = _iota2((t, t), 0)
        col = _iota2((t, t), 1)
        u_strict = (row > col).astype(_BF)
        causal = col < row
        qs = q_ref[...] * scale
        qm = [jnp.where(masks[hh], qs, 0.0).astype(_BF) for hh in range(2)]

        def scores(j):
            kb = k_ref[pl.ds(pl.multiple_of(jnp.maximum(j, 0) * t, t), t), :].astype(_BF)
            return tuple(_dot(qm[hh], kb, 1, 1) for hh in range(2))

        def block(j, carry, qk, diag):
            off = pl.multiple_of(j * t, t)
            vb = v_ref[pl.ds(off, t), :].astype(_BF)
            runs, outs = [], []
            for hh in range(2):
                run, o = carry[0][hh], carry[1][hh]
                z = qk[hh]
                l1, lb = _softplus_parts(z)
                if diag:
                    l1 = jnp.where(causal, l1, 0.0)
                between = run + _dot_exact(l1, u_strict, 1, 0, 2)
                a = jnp.exp(lb + between)
                if diag:
                    a = jnp.where(causal, a, 0.0)
                outs.append(o + _dot_exact(a, vb, 1, 0, 2))
                runs.append(run + jnp.sum(l1, axis=-1, keepdims=True))
            return tuple(runs), tuple(outs)

        zero, zero_o = jnp.zeros((t, 1), F32), jnp.zeros((t, 128), F32)
        ahead = scores(i - 1)
        carry = block(i, ((zero, zero), (zero_o, zero_o)), scores(i), True)
        carry = _sb_key_loop(i, carry, lambda j, c, qk: block(j, c, qk, False), scores, ahead)
        o_ref[...] = jnp.where(masks[0], carry[1][0], carry[1][1])

    kv_spec = lambda c0: pl.BlockSpec((s, 128), lambda p, i: (0, c0 // 128 + p))
    c_operands, c_in_specs, c_out_shapes, c_out_specs, c_sems = _comm_operands(comm) if comm else ([], [], [], [], [])
    outs = _pcall(
        _with_comm(body, comm, 3, 1, *_grid_ends(SB_HEADS // 2, nblk)), name=name, grid=(SB_HEADS // 2, nblk),
        out_shape=[jax.ShapeDtypeStruct((s, 256), F32)] + c_out_shapes,
        in_specs=[pl.BlockSpec((t, 128), lambda p, i: (i, C_QA // 128 + p)), kv_spec(C_KA), kv_spec(C_VA)] + c_in_specs,
        out_specs=[pl.BlockSpec((t, 128), lambda p, i: (i, p))] + c_out_specs,
        scratch_shapes=c_sems, input_output_aliases=_comm_aliases(comm, 3, 1), compiler_params=_cp(ARB, ARB),
    )(proj, proj, proj, *c_operands)
    return outs[0], list(outs[1:])


def _sb_bwd(proj, o, dmix, name, comm=None):
    s = proj.shape[0]
    t = min(ATT_T, s)
    nblk = s // t
    scale = HEAD_DIM ** -0.5

    def body(q_ref, k_ref, v_ref, o_ref, do_ref, dq_ref, dk_ref, dv_ref, dkt_ref, dvt_ref):
        i = pl.program_id(1)
        masks = _head_masks()
        row = _iota2((t, t), 0)
        col = _iota2((t, t), 1)
        u_strict = (row > col).astype(_BF)
        u_incl = (row >= col).astype(_BF)
        causal = col < row

        @pl.when(i == 0)
        def _():
            dkt_ref[...] = jnp.zeros_like(dkt_ref)
            dvt_ref[...] = jnp.zeros_like(dvt_ref)

        qs = q_ref[...] * scale
        dov = do_ref[...]
        ov = o_ref[...]
        qm_f = [jnp.where(masks[hh], qs, 0.0) for hh in range(2)]
        dom_f = [jnp.where(masks[hh], dov, 0.0) for hh in range(2)]
        qm = [qm_f[hh].astype(_BF) for hh in range(2)]
        dom = [dom_f[hh].astype(_BF) for hh in range(2)]
        qm_t = [qm_f[hh].T.astype(_BF) for hh in range(2)]
        dom_t = [dom_f[hh].T.astype(_BF) for hh in range(2)]
        dsum = [jnp.sum(dom[hh].astype(F32) * ov, axis=-1, keepdims=True) for hh in range(2)]

        def scores(j):
            off = pl.multiple_of(jnp.maximum(j, 0) * t, t)
            kb = k_ref[pl.ds(off, t), :].astype(_BF)
            vb = v_ref[pl.ds(off, t), :].astype(_BF)
            return tuple((_dot(qm[hh], kb, 1, 1), _dot(dom[hh], vb, 1, 1)) for hh in range(2))

        def block(j, carry, ahead, diag):
            off = pl.multiple_of(j * t, t)
            kb = k_ref[pl.ds(off, t), :].astype(_BF)
            runs, eruns, dqs = [], [], []
            dk = dv = None
            for hh in range(2):
                run, erun, dq = carry[0][hh], carry[1][hh], carry[2][hh]
                z, da = ahead[hh]
                l1, lb = _softplus_parts(z)
                if diag:
                    l1 = jnp.where(causal, l1, 0.0)
                between = run + _dot_exact(l1, u_strict, 1, 0, 2)
                a = jnp.exp(lb + between)
                if diag:
                    a = jnp.where(causal, a, 0.0)
                g = a * da
                cum = dsum[hh] - (erun + _dot_exact(g, u_incl, 1, 0, 2))
                beta = jnp.exp(lb)
                dz = g * (1.0 - beta) - cum * beta
                if diag:
                    dz = jnp.where(causal, dz, 0.0)
                dzb = dz.astype(_BF)
                dqs.append(dq + _dot(dzb, kb, 1, 0))
                dk_h = _dot(qm_t[hh], dzb, 1, 0)
                dv_h = _dot(dom_t[hh], a.astype(_BF), 1, 0)
                dk = dk_h if dk is None else dk + dk_h
                dv = dv_h if dv is None else dv + dv_h
                runs.append(run + jnp.sum(l1, axis=-1, keepdims=True))
                eruns.append(erun + jnp.sum(g, axis=-1, keepdims=True))
            dkt_ref[:, pl.ds(off, t)] += dk
            dvt_ref[:, pl.ds(off, t)] += dv
            return tuple(runs), tuple(eruns), tuple(dqs)

        zero, zero_q = jnp.zeros((t, 1), F32), jnp.zeros((t, 128), F32)
        ahead = scores(i - 1)
        carry = block(i, ((zero, zero), (zero, zero), (zero_q, zero_q)), scores(i), True)
        carry = _sb_key_loop(i, carry, lambda j, c, a: block(j, c, a, False), scores, ahead)
        dq_ref[...] = jnp.where(masks[0], carry[2][0], carry[2][1]) * scale

        @pl.when(i == nblk - 1)
        def _():
            _untranspose(dkt_ref, dk_ref, t, nblk)
            _untranspose(dvt_ref, dv_ref, t, nblk)

    kv_spec = lambda c0: pl.BlockSpec((s, 128), lambda p, i: (0, c0 // 128 + p))
    blk = lambda c0: pl.BlockSpec((t, 128), lambda p, i: (i, c0 // 128 + p))
    acc = pl.BlockSpec((s, 128), lambda p, i: (0, p))
    shp = jax.ShapeDtypeStruct((s, 256), F32)
    c_operands, c_in_specs, c_out_shapes, c_out_specs, c_sems = _comm_operands(comm) if comm else ([], [], [], [], [])
    outs = _pcall(
        _with_comm(body, comm, 5, 3, *_grid_ends(SB_HEADS // 2, nblk), n_scratch=2), name=name,
        grid=(SB_HEADS // 2, nblk), out_shape=[shp, shp, shp] + c_out_shapes,
        in_specs=[blk(C_QA), kv_spec(C_KA), kv_spec(C_VA), blk(0), blk(M_SB)] + c_in_specs,
        out_specs=[blk(0), acc, acc] + c_out_specs,
        scratch_shapes=[pltpu.VMEM((128, s), F32), pltpu.VMEM((128, s), F32)] + c_sems,
        input_output_aliases=_comm_aliases(comm, 5, 3), compiler_params=_cp(ARB, ARB),
    )(proj, proj, proj, o, dmix, *c_operands)
    return outs[0], outs[1], outs[2], list(outs[3:])


def _group_matrix(width):
    r = _iota2((width, width), 0) // HEAD_DIM
    c = _iota2((width, width), 1) // HEAD_DIM
    return (r == c).astype(_BF)


def _group_mean(v, gm):
    return _dot_exact(v, gm, 1, 0, 2) * (1.0 / HEAD_DIM)


def _fox_prep(proj, gq, gk, bf, name):
    s = proj.shape[0]
    t = min(ATT_T, s)
    nblk = s // t
    assert nblk <= 128

    def body(q_ref, k_ref, fl_ref, gq_ref, gk_ref, bf_ref, qn_ref, kn_ref, cfc_ref, cfr_ref, cfe_ref, carry_ref):
        i = pl.program_id(0)
        gm = _group_matrix(512)
        for src, gref, dst in ((q_ref, gq_ref, qn_ref), (k_ref, gk_ref, kn_ref)):
            v = src[...]
            rstd = lax.rsqrt(_group_mean(v * v, gm) + EPS)
            dst[...] = (v * rstd * gref[...]).astype(dst.dtype)

        @pl.when(i == 0)
        def _():
            carry_ref[...] = jnp.zeros_like(carry_ref)
            cfe_ref[...] = jnp.full(cfe_ref.shape, NO_BLOCK, F32)

        lane = _iota2((1, 128), 1)
        logf = jnp.where(lane < FOX_HEADS, _softplus_parts(-(fl_ref[...] + bf_ref[...]))[0], 0.0)
        lower = (_iota2((t, t), 0) >= _iota2((t, t), 1)).astype(_BF)
        cf = carry_ref[...] + _dot_exact(logf, lower, 1, 0, 3, v_left=False)
        cfc_ref[...] = cf
        rows = cf.T[:FOX_HEADS, :]
        cfr_ref[...] = rows
        cfe_ref[...] = jnp.where(lane == i, rows[:, t - 1:t], cfe_ref[...])
        carry_ref[...] = cf[t - 1:t, :]

    col = lambda w, c0: pl.BlockSpec((t, w), lambda i: (i, c0 // w))
    v512 = pl.BlockSpec((1, 512), lambda i: (0, 0))
    return _pcall(
        body, name=name, grid=(nblk,),
        out_shape=[jax.ShapeDtypeStruct((s, 512), _BF), jax.ShapeDtypeStruct((s, 512), _BF),
                   jax.ShapeDtypeStruct((s, 128), F32), jax.ShapeDtypeStruct((FOX_HEADS, s), F32),
                   jax.ShapeDtypeStruct((FOX_HEADS, 128), F32)],
        in_specs=[col(512, C_QB), col(512, C_KB), col(128, C_FL), v512, v512, pl.BlockSpec((1, 128), lambda i: (0, 0))],
        out_specs=[col(512, 0), col(512, 0), col(128, 0), pl.BlockSpec((FOX_HEADS, t), lambda i: (0, i)),
                   pl.BlockSpec((FOX_HEADS, 128), lambda i: (0, 0))],
        scratch_shapes=[pltpu.VMEM((1, 128), F32)], compiler_params=_cp(ARB),
    )(proj, proj, proj, gq, gk, bf)


def _fox_cq(cfc_blk, head):
    lane = _iota2((1, 128), 1)
    return jnp.sum(jnp.where(lane == head, cfc_blk, 0.0), axis=-1, keepdims=True)


def _fox_ck(cfr_ref, off, t, head):
    sub = _iota2((FOX_HEADS, 1), 0)
    return jnp.sum(jnp.where(sub == head, cfr_ref[:, pl.ds(off, t)], 0.0), axis=0, keepdims=True)


def _fox_blocks_needed(top, cfe_ref, head, i):
    sub = _iota2((FOX_HEADS, 1), 0)
    lane = _iota2((1, 128), 1)
    last_key = jnp.sum(jnp.where(sub == head, cfe_ref[...], 0.0), axis=0, keepdims=True)
    need = jnp.logical_and(lane < i, top - last_key > UNDERFLOW)
    return jnp.sum(need.astype(jnp.int32))


def _grid_ends(n_outer, n_inner):
    first = lambda: jnp.logical_and(pl.program_id(0) == 0, pl.program_id(1) == 0)
    last = lambda: jnp.logical_and(pl.program_id(0) == n_outer - 1, pl.program_id(1) == n_inner - 1)
    return first, last


def _fox_fwd(proj, qn, kn, cfc, cfr, cfe, qk_bound, name, comm=None):
    s = proj.shape[0]
    t = min(ATT_T, s)
    nblk = s // t
    scale = HEAD_DIM ** -0.5

    def body(q_ref, k_ref, v_ref, cfc_ref, cfr_ref, cfe_ref, qkb_ref, o_ref, lse_ref):
        p = pl.program_id(0)
        i = pl.program_id(1)
        masks = _head_masks()
        lane = _iota2((1, 128), 1)
        valid = _iota2((t, t), 1) <= _iota2((t, t), 0)
        qv = q_ref[...]
        cfc_blk = cfc_ref[...]
        heads = [2 * p + hh for hh in range(2)]
        qm = [(jnp.where(masks[hh], qv, 0) * scale).astype(_BF) for hh in range(2)]
        cq = [_fox_cq(cfc_blk, heads[hh]) for hh in range(2)]

        def scores(j):
            kb = k_ref[pl.ds(pl.multiple_of(jnp.maximum(j, 0) * t, t), t), :]
            return tuple(_dot(qm[hh], kb, 1, 1) for hh in range(2))

        def block(j, carry, qk, diag):
            off = pl.multiple_of(j * t, t)
            vb = v_ref[pl.ds(off, t), :].astype(_BF)
            ms, ls, accs = [], [], []
            for hh in range(2):
                m, l, acc = carry[0][hh], carry[1][hh], carry[2][hh]
                z = qk[hh] + (cq[hh] - _fox_ck(cfr_ref, off, t, heads[hh]))
                if diag:
                    z = jnp.where(valid, z, NEG)
                m_new = jnp.maximum(m, jnp.max(z, axis=-1, keepdims=True))
                pe = jnp.exp(z - m_new)
                alpha = jnp.exp(m - m_new)
                ms.append(m_new)
                ls.append(alpha * l + jnp.sum(pe, axis=-1, keepdims=True))
                accs.append(alpha * acc + _dot(pe.astype(_BF), vb, 1, 0))
            return tuple(ms), tuple(ls), tuple(accs)

        def step(jj, c):
            ahead = scores(i - jj - 2)
            return block(i - jj - 1, c[0], c[1], False), ahead

        neg, zero, zero_o = jnp.full((t, 1), NEG, F32), jnp.zeros((t, 1), F32), jnp.zeros((t, 128), F32)
        ahead = scores(i - 1)
        carry = block(i, ((neg, neg), (zero, zero), (zero_o, zero_o)), scores(i), True)
        needed = [_fox_blocks_needed(jnp.max(qkb_ref[...] + cq[hh] - carry[0][hh]), cfe_ref, heads[hh], i)
                  for hh in range(2)]
        m, l, acc = lax.fori_loop(0, jnp.maximum(needed[0], needed[1]), step, (carry, ahead))[0]
        o_ref[...] = jnp.where(masks[0], acc[0] / l[0], acc[1] / l[1])
        lse_ref[...] = jnp.where(lane == 0, m[0] + jnp.log(l[0]), jnp.where(lane == 1, m[1] + jnp.log(l[1]), 0.0))

    c_operands, c_in_specs, c_out_shapes, c_out_specs, c_sems = _comm_operands(comm) if comm else ([], [], [], [], [])
    outs = _pcall(
        _with_comm(body, comm, 7, 2, *_grid_ends(FOX_HEADS // 2, nblk)), name=name, grid=(FOX_HEADS // 2, nblk),
        out_shape=[jax.ShapeDtypeStruct((s, 512), F32), jax.ShapeDtypeStruct((FOX_HEADS // 2, s, 128), F32)] + c_out_shapes,
        in_specs=[pl.BlockSpec((t, 128), lambda p, i: (i, p)), pl.BlockSpec((s, 128), lambda p, i: (0, p)),
                  pl.BlockSpec((s, 128), lambda p, i: (0, C_VB // 128 + p)),
                  pl.BlockSpec((t, 128), lambda p, i: (i, 0)), pl.BlockSpec((FOX_HEADS, s), lambda p, i: (0, 0)),
                  pl.BlockSpec((FOX_HEADS, 128), lambda p, i: (0, 0)), pl.BlockSpec((1, 1), lambda p, i: (0, 0))] + c_in_specs,
        out_specs=[pl.BlockSpec((t, 128), lambda p, i: (i, p)),
                   pl.BlockSpec((None, t, 128), lambda p, i: (p, i, 0))] + c_out_specs,
        scratch_shapes=c_sems, input_output_aliases=_comm_aliases(comm, 7, 2),
        compiler_params=_cp(ARB, ARB),
    )(qn, kn, proj, cfc, cfr, cfe, qk_bound, *c_operands)
    return outs[0], outs[1], list(outs[2:])


def _fox_bwd(proj, qn, kn, cfc, cfr, cfe, qk_bound, o, lse, dmix, name, comm=None):
    s = proj.shape[0]
    t = min(ATT_T, s)
    nblk = s // t
    scale = HEAD_DIM ** -0.5

    def body(q_ref, k_ref, v_ref, cfc_ref, cfr_ref, cfe_ref, qkb_ref, o_ref, lse_ref, do_ref,
             dq_ref, dk_ref, dv_ref, dcr_ref, dkt_ref, dvt_ref):
        p = pl.program_id(0)
        i = pl.program_id(1)
        masks = _head_masks()
        lane = _iota2((1, 128), 1)
        valid = _iota2((t, t), 1) <= _iota2((t, t), 0)
        before = (_iota2((t, t), 0) < _iota2((t, t), 1)).astype(_BF)

        @pl.when(i == 0)
        def _():
            dkt_ref[...] = jnp.zeros_like(dkt_ref)
            dvt_ref[...] = jnp.zeros_like(dvt_ref)

        @pl.when(jnp.logical_and(p == 0, i == 0))
        def _():
            dcr_ref[...] = jnp.zeros_like(dcr_ref)

        head_row = _iota2((FOX_HEADS, 1), 0)
        qv = q_ref[...]
        cfc_blk = cfc_ref[...]
        dov = do_ref[...]
        ov = o_ref[...]
        lse_blk = lse_ref[...]
        heads = [2 * p + hh for hh in range(2)]
        qm = [(jnp.where(masks[hh], qv, 0) * scale).astype(_BF) for hh in range(2)]
        dom = [jnp.where(masks[hh], dov, 0.0).astype(_BF) for hh in range(2)]
        qm_t = [qm[hh].astype(F32).T.astype(_BF) for hh in range(2)]
        dom_t = [dom[hh].astype(F32).T.astype(_BF) for hh in range(2)]
        dsum = [jnp.sum(dom[hh].astype(F32) * ov, axis=-1, keepdims=True) for hh in range(2)]
        lse_h = [jnp.sum(jnp.where(lane == hh, lse_blk, 0.0), axis=-1, keepdims=True) for hh in range(2)]
        cq = [_fox_cq(cfc_blk, heads[hh]) for hh in range(2)]
        needed = [_fox_blocks_needed(jnp.max(qkb_ref[...] + cq[hh] - lse_h[hh]), cfe_ref, heads[hh], i)
                  for hh in range(2)]
        first = i - jnp.maximum(needed[0], needed[1])

        def scores(j):
            off = pl.multiple_of(j * t, t)
            kb = k_ref[pl.ds(off, t), :]
            vb = v_ref[pl.ds(off, t), :].astype(_BF)
            return tuple((_dot(qm[hh], kb, 1, 1), _dot(dom[hh], vb, 1, 1)) for hh in range(2))

        def block(j, carry, ahead, diag):
            off = pl.multiple_of(j * t, t)
            kb = k_ref[pl.ds(off, t), :]
            dqs, pres = [], []
            dk = dv = None
            for hh in range(2):
                dq, pre = carry[0][hh], carry[1][hh]
                z = ahead[hh][0] + (cq[hh] - _fox_ck(cfr_ref, off, t, heads[hh]))
                pm = jnp.exp(z - lse_h[hh])
                if diag:
                    pm = jnp.where(valid, pm, 0.0)
                ds = pm * (ahead[hh][1] - dsum[hh])
                dsb = ds.astype(_BF)
                dqs.append(dq + _dot(dsb, kb, 1, 0))
                dk_h = _dot(qm_t[hh], dsb, 1, 0)
                dv_h = _dot(dom_t[hh], pm.astype(_BF), 1, 0)
                dk = dk_h if dk is None else dk + dk_h
                dv = dv_h if dv is None else dv + dv_h
                if diag:
                    lower_keys = jnp.where(valid, pre + _dot_exact(ds, before, 1, 0, 2), 0.0)
                    dlogf = jnp.sum(lower_keys, axis=0, keepdims=True)
                else:
                    dlogf = jnp.sum(pre) + _dot_exact(jnp.sum(ds, axis=0, keepdims=True), before, 1, 0, 3)
                dcr_ref[:, pl.ds(off, t)] += jnp.where(head_row == heads[hh], dlogf, 0.0)
                pres.append(pre + jnp.sum(ds, axis=-1, keepdims=True))
            dkt_ref[:, pl.ds(off, t)] += dk
            dvt_ref[:, pl.ds(off, t)] += dv
            return tuple(dqs), tuple(pres)

        zero, zero_q = jnp.zeros((t, 1), F32), jnp.zeros((t, 128), F32)

        def step(j, c):
            ahead = scores(j + 1)
            return block(j, c[0], c[1], False), ahead

        carry, ahead = lax.fori_loop(first, i, step, (((zero_q, zero_q), (zero, zero)), scores(first)))
        dqs, _ = block(i, carry, ahead, True)
        dq_ref[...] = jnp.where(masks[0], dqs[0], dqs[1]) * scale

        @pl.when(i == nblk - 1)
        def _():
            _untranspose(dkt_ref, dk_ref, t, nblk)
            _untranspose(dvt_ref, dv_ref, t, nblk)

    blk = lambda c0: pl.BlockSpec((t, 128), lambda p, i: (i, c0 // 128 + p))
    res = lambda c0: pl.BlockSpec((s, 128), lambda p, i: (0, c0 // 128 + p))
    shp = jax.ShapeDtypeStruct((s, 512), F32)
    c_operands, c_in_specs, c_out_shapes, c_out_specs, c_sems = _comm_operands(comm) if comm else ([], [], [], [], [])
    transposed = [pltpu.VMEM((128, s), F32), pltpu.VMEM((128, s), F32)]
    outs = _pcall(
        _with_comm(body, comm, 10, 4, *_grid_ends(FOX_HEADS // 2, nblk), n_scratch=2), name=name,
        grid=(FOX_HEADS // 2, nblk),
        out_shape=[shp, shp, shp, jax.ShapeDtypeStruct((FOX_HEADS, s), F32)] + c_out_shapes,
        in_specs=[blk(0), res(0), res(C_VB), pl.BlockSpec((t, 128), lambda p, i: (i, 0)),
                  pl.BlockSpec((FOX_HEADS, s), lambda p, i: (0, 0)), pl.BlockSpec((FOX_HEADS, 128), lambda p, i: (0, 0)),
                  pl.BlockSpec((1, 1), lambda p, i: (0, 0)), blk(0),
                  pl.BlockSpec((None, t, 128), lambda p, i: (p, i, 0)), blk(M_FOX)] + c_in_specs,
        out_specs=[blk(0), res(0), res(0), pl.BlockSpec((FOX_HEADS, s), lambda p, i: (0, 0))] + c_out_specs,
        scratch_shapes=transposed + c_sems, input_output_aliases=_comm_aliases(comm, 10, 4),
        compiler_params=_cp(ARB, ARB),
    )(qn, kn, proj, cfc, cfr, cfe, qk_bound, o, lse, dmix, *c_operands)
    return outs[0], outs[1], outs[2], outs[3], list(outs[4:])


def _fox_prep_bwd(proj, gq, gk, bf, dqn, dkn, dlogf, name):
    s = proj.shape[0]
    t = min(ATT_T, s)
    nblk = s // t

    def body(q_ref, k_ref, fl_ref, gq_ref, gk_ref, bf_ref, dqn_ref, dkn_ref, dlogf_ref,
             dq_ref, dk_ref, dfl_ref, dgq_ref, dgk_ref, dbf_ref):
        i = pl.program_id(0)
        gm = _group_matrix(512)

        @pl.when(i == 0)
        def _():
            dgq_ref[...] = jnp.zeros_like(dgq_ref)
            dgk_ref[...] = jnp.zeros_like(dgk_ref)
            dbf_ref[...] = jnp.zeros_like(dbf_ref)

        for src, gref, dyref, dst, dgref in ((q_ref, gq_ref, dqn_ref, dq_ref, dgq_ref),
                                              (k_ref, gk_ref, dkn_ref, dk_ref, dgk_ref)):
            v = src[...]
            dy = dyref[...]
            rstd = lax.rsqrt(_group_mean(v * v, gm) + EPS)
            xhat = v * rstd
            dgref[...] += jnp.sum(dy * xhat, axis=0, keepdims=True)
            dxh = dy * gref[...]
            dst[...] = (rstd * (dxh - xhat * _group_mean(dxh * xhat, gm))).astype(dst.dtype)

        to_lanes = (_iota2((FOX_HEADS, 128), 0) == _iota2((FOX_HEADS, 128), 1)).astype(_BF)
        dlogf = _dot_exact(dlogf_ref[...], to_lanes, 0, 0, 3)
        xv = fl_ref[...] + bf_ref[...]
        e = jnp.exp(-jnp.abs(xv))
        dfl = dlogf * (jnp.where(xv >= 0.0, e, 1.0) / (1.0 + e))
        dfl_ref[...] = dfl.astype(dfl_ref.dtype)
        dbf_ref[...] += jnp.sum(dfl, axis=0, keepdims=True)

    col = lambda w, c0: pl.BlockSpec((t, w), lambda i: (i, c0 // w))
    v512 = pl.BlockSpec((1, 512), lambda i: (0, 0))
    v128 = pl.BlockSpec((1, 128), lambda i: (0, 0))
    return _pcall(
        body, name=name, grid=(nblk,),
        out_shape=[jax.ShapeDtypeStruct((s, 512), _BF), jax.ShapeDtypeStruct((s, 512), _BF),
                   jax.ShapeDtypeStruct((s, 128), _BF), jax.ShapeDtypeStruct((1, 512), F32),
                   jax.ShapeDtypeStruct((1, 512), F32), jax.ShapeDtypeStruct((1, 128), F32)],
        in_specs=[col(512, C_QB), col(512, C_KB), col(128, C_FL), v512, v512, v128, col(512, 0), col(512, 0),
                  pl.BlockSpec((FOX_HEADS, t), lambda i: (0, i))],
        out_specs=[col(512, 0), col(512, 0), col(128, 0), v512, v512, v128], compiler_params=_cp(ARB),
    )(proj, proj, proj, gq, gk, bf, dqn, dkn, dlogf)


GELU_C = 0.7978845608028654
GELU_A = 0.044715


def _gelu(x):
    return 0.5 * x * (1.0 + jnp.tanh(GELU_C * (x + GELU_A * x * x * x)))


def _gelu_grad(x):
    th = jnp.tanh(GELU_C * (x + GELU_A * x * x * x))
    return 0.5 * (1.0 + th) + 0.5 * x * (1.0 - th * th) * (GELU_C * (1.0 + 3.0 * GELU_A * x * x))


def _sgu_tril():
    return _iota2((SGU_CHUNK, SGU_CHUNK), 0) >= _iota2((SGU_CHUNK, SGU_CHUNK), 1)


def _sgu_group_masks():
    lane = _iota2((1, 256), 1)
    return [(lane // HEAD_DIM) == g for g in range(SGU_GROUPS)]


def _sgu_fwd(proj, w, gs, bexp, name):
    s = proj.shape[0]
    t = min(2 * SGU_CHUNK, s)

    def body(u_ref, v_ref, w_ref, gs_ref, b_ref, o_ref):
        gm = _group_matrix(256)
        gmask = _sgu_group_masks()
        tril = _sgu_tril()
        u = _gelu(u_ref[...])
        vg = _gelu(v_ref[...])
        vhat = (vg * lax.rsqrt(_group_mean(vg * vg, gm) + EPS) * gs_ref[...]).astype(_BF)
        for ch in range(t // SGU_CHUNK):
            rows = slice(ch * SGU_CHUNK, (ch + 1) * SGU_CHUNK)
            mixed = b_ref[...]
            for g in range(SGU_GROUPS):
                wg = jnp.where(tril, w_ref[g], 0.0).astype(_BF)
                mixed = jnp.where(gmask[g], mixed + _dot(wg, vhat[rows], 1, 0), mixed)
            o_ref[rows, :] = u[rows] * mixed

    return _pcall(
        body, name=name, grid=(s // t,), out_shape=jax.ShapeDtypeStruct((s, 256), F32),
        in_specs=[_row_spec(t, 256, C_UC // 256), _row_spec(t, 256, C_VC // 256),
                  pl.BlockSpec((SGU_GROUPS, SGU_CHUNK, SGU_CHUNK), lambda i: (0, 0, 0)), _vec_spec(256),
                  pl.BlockSpec((SGU_CHUNK, 256), lambda i: (0, 0))],
        out_specs=_row_spec(t, 256), compiler_params=_cp(PAR),
    )(proj, proj, w, gs, bexp)


def _sgu_bwd(proj, w, gs, bexp, dmix, name):
    s = proj.shape[0]
    t = min(2 * SGU_CHUNK, s)
    nstep = s // t

    def body(u_ref, v_ref, w_ref, gs_ref, b_ref, do_ref, du_ref, dv_ref, dw_ref, db_ref, dgs_ref):
        i = pl.program_id(0)
        gm = _group_matrix(256)
        gmask = _sgu_group_masks()
        tril = _sgu_tril()

        @pl.when(i == 0)
        def _():
            dw_ref[...] = jnp.zeros_like(dw_ref)
            db_ref[...] = jnp.zeros_like(db_ref)
            dgs_ref[...] = jnp.zeros_like(dgs_ref)

        uc = u_ref[...]
        vc = v_ref[...]
        u = _gelu(uc)
        vg = _gelu(vc)
        rstd = lax.rsqrt(_group_mean(vg * vg, gm) + EPS)
        xh = vg * rstd
        gsv = gs_ref[...]
        vhat = (xh * gsv).astype(_BF)
        dov = do_ref[...]
        dm = dov * u
        for ch in range(t // SGU_CHUNK):
            rows = slice(ch * SGU_CHUNK, (ch + 1) * SGU_CHUNK)
            mixed = b_ref[...]
            dvh = jnp.zeros((SGU_CHUNK, 256), F32)
            dmc = dm[rows]
            for g in range(SGU_GROUPS):
                wg = jnp.where(tril, w_ref[g], 0.0).astype(_BF)
                mixed = jnp.where(gmask[g], mixed + _dot(wg, vhat[rows], 1, 0), mixed)
                dvh = jnp.where(gmask[g], _dot(wg, dmc.astype(_BF), 0, 0), dvh)
                dw_ref[g] += _dot(jnp.where(gmask[g], dmc, 0.0).astype(_BF), vhat[rows], 1, 1)
            db_ref[...] += dmc
            du_ref[rows, :] = (dov[rows] * mixed * _gelu_grad(uc[rows])).astype(du_ref.dtype)
            xhc = xh[rows]
            dgs_ref[...] += jnp.sum(dvh * xhc, axis=0, keepdims=True)
            dxh = dvh * gsv
            dvg = rstd[rows] * (dxh - xhc * _group_mean(dxh * xhc, gm))
            dv_ref[rows, :] = (dvg * _gelu_grad(vc[rows])).astype(dv_ref.dtype)

        @pl.when(i == nstep - 1)
        def _():
            for g in range(SGU_GROUPS):
                dw_ref[g] = jnp.where(tril, dw_ref[g], 0.0)

    wspec = pl.BlockSpec((SGU_GROUPS, SGU_CHUNK, SGU_CHUNK), lambda i: (0, 0, 0))
    bspec = pl.BlockSpec((SGU_CHUNK, 256), lambda i: (0, 0))
    return _pcall(
        body, name=name, grid=(nstep,),
        out_shape=[jax.ShapeDtypeStruct((s, 256), _BF), jax.ShapeDtypeStruct((s, 256), _BF),
                   jax.ShapeDtypeStruct((SGU_GROUPS, SGU_CHUNK, SGU_CHUNK), F32),
                   jax.ShapeDtypeStruct((SGU_CHUNK, 256), F32), jax.ShapeDtypeStruct((1, 256), F32)],
        in_specs=[_row_spec(t, 256, C_UC // 256), _row_spec(t, 256, C_VC // 256), wspec, _vec_spec(256), bspec,
                  _row_spec(t, 256, M_SGU // 256)],
        out_specs=[_row_spec(t, 256), _row_spec(t, 256), wspec, bspec, _vec_spec(256)],
        compiler_params=_cp(ARB),
    )(proj, proj, w, gs, bexp, dmix)


def _ada_fwd(c_all, ada_w, name):
    depth, d, n = ada_w.shape

    def body(c_ref, w_ref, o_ref):
        cv = c_ref[...]
        cond = cv / (1.0 + jnp.exp(-cv))
        o_ref[...] = _dot_f32(cond, w_ref[...], 1, 0)

    return _pcall(
        body, name=name, grid=(depth,), out_shape=jax.ShapeDtypeStruct((depth, N_DEV, n), F32),
        in_specs=[pl.BlockSpec((N_DEV, d), lambda l: (0, 0)), pl.BlockSpec((None, d, n), lambda l: (l, 0, 0))],
        out_specs=pl.BlockSpec((None, N_DEV, n), lambda l: (l, 0, 0)), compiler_params=_cp(PAR),
    )(c_all, ada_w)


def _ada_bwd(c_all, dmod, name):
    depth, _, n = dmod.shape
    d = c_all.shape[1]

    def body(c_ref, dm_ref, o_ref):
        cv = c_ref[...]
        cond = cv / (1.0 + jnp.exp(-cv))
        o_ref[...] = _dot_f32(cond, dm_ref[...], 0, 0)

    return _pcall(
        body, name=name, grid=(depth,), out_shape=jax.ShapeDtypeStruct((depth, d, n), F32),
        in_specs=[pl.BlockSpec((N_DEV, d), lambda l: (0, 0)), pl.BlockSpec((None, N_DEV, n), lambda l: (l, 0, 0))],
        out_specs=pl.BlockSpec((None, d, n), lambda l: (l, 0, 0)), compiler_params=_cp(PAR),
    )(c_all, dmod)


def _adamw(slots, w, m, v, name):
    n, r, c = slots.shape
    tr = 256 if r % 256 == 0 else r
    bc1 = 1.0 - ADAM_B1 ** ADAM_STEP
    bc2 = 1.0 - ADAM_B2 ** ADAM_STEP

    def body(s_ref, w_ref, m_ref, v_ref, g_ref, d_ref, nm_ref, nv_ref):
        g = s_ref[0].astype(F32)
        for j in range(1, n):
            g = g + s_ref[j].astype(F32)
        m_new = ADAM_B1 * m_ref[...] + (1.0 - ADAM_B1) * g
        v_new = ADAM_B2 * v_ref[...] + (1.0 - ADAM_B2) * (g * g)
        g_ref[...] = g
        nm_ref[...] = m_new
        nv_ref[...] = v_new
        d_ref[...] = -ADAM_LR * ((m_new / bc1) / (jnp.sqrt(v_new / bc2) + ADAM_EPS) + ADAM_WD * w_ref[...])

    tile = pl.BlockSpec((tr, c), lambda i: (i, 0))
    shp = jax.ShapeDtypeStruct((r, c), F32)
    return _pcall(
        body, name=name, grid=(r // tr,), out_shape=[shp] * 4,
        in_specs=[pl.BlockSpec((n, tr, c), lambda i: (0, i, 0)), tile, tile, tile],
        out_specs=[tile] * 4, compiler_params=_cp(PAR),
    )(slots, w, m, v)


def _w_in_layout(w):
    pad = jnp.zeros(w.shape[:-1] + (IN_P - IN_W,), w.dtype)
    return jnp.concatenate([w[..., 768:FL_SRC], w[..., :768], w[..., UC_SRC:], w[..., FL_SRC:UC_SRC], pad], axis=-1)


def _w_in_unlayout(g):
    return jnp.concatenate([g[..., C_QA:C_UC], g[..., :C_QA], g[..., C_FL:C_FL + FOX_HEADS], g[..., C_UC:C_FL]], axis=-1)


SMALL = [("ada_b", DEPTH * 6 * D_MODEL), ("norm1_g", DEPTH * D_MODEL), ("norm2_g", DEPTH * D_MODEL),
         ("sgu_w", DEPTH * SGU_GROUPS * SGU_CHUNK * SGU_CHUNK), ("sgu_b", DEPTH * SGU_GROUPS * SGU_CHUNK),
         ("sgu_norm_g", DEPTH * SGU_GROUPS * HEAD_DIM), ("q_norm_g", DEPTH * HEAD_DIM), ("k_norm_g", DEPTH * HEAD_DIM),
         ("b_forget", DEPTH * FOX_HEADS), ("loss", 1)]
SMALL_ROWS = 2560


def _pack_small(parts):
    flat = jnp.concatenate([parts[name].reshape(-1).astype(F32) for name, _ in SMALL])
    return jnp.pad(flat, (0, SMALL_ROWS * 128 - flat.shape[0])).reshape(SMALL_ROWS, 128)


def _unpack_small(packed, shapes):
    flat = packed.reshape(-1)
    out, off = {}, 0
    for name, size in SMALL:
        out[name] = flat[off:off + size].reshape(shapes[name])
        off += size
    return out


def kernel(x, c, ada_w, ada_b, norm1_g, norm2_g, w_in, b_forget, q_norm_g, k_norm_g, sgu_norm_g, sgu_w, sgu_b, w_out, mlp_w1, mlp_w2, loss_target, m_ada_w, m_ada_b, m_norm1_g, m_norm2_g, m_w_in, m_b_forget, m_q_norm_g, m_k_norm_g, m_sgu_norm_g, m_sgu_w, m_sgu_b, m_w_out, m_mlp_w1, m_mlp_w2, v_ada_w, v_ada_b, v_norm1_g, v_norm2_g, v_w_in, v_b_forget, v_q_norm_g, v_k_norm_g, v_sgu_norm_g, v_sgu_w, v_sgu_b, v_w_out, v_mlp_w1, v_mlp_w2):
    me = _lin(_my_pos())
    x0 = x[0]
    target = loss_target[0]
    n_ada = ada_w.shape[2]

    shards = [w.astype(_XBF) for w in (w_in, w_out, mlp_w1, mlp_w2)]

    def whole_in(w_in_g):
        return _w_in_layout(jnp.transpose(w_in_g, (1, 0, 2)).reshape(D_MODEL, IN_W))

    def whole_rest(w_out_g, w1_g, w2_g):
        return (w_out_g.reshape(D_MODEL, D_MODEL), jnp.transpose(w1_g, (1, 0, 2)).reshape(D_MODEL, D_FF),
                w2_g.reshape(D_FF, D_MODEL))

    w_in_first, c_all = _exchange([shards[0][0], c], False, "gather_weights")
    c_all = c_all.reshape(N_DEV, D_MODEL)
    weights = [None] * DEPTH

    mod_part = _ada_fwd(c_all, ada_w, "ada_fwd")
    (mod_rows,) = _exchange([jnp.transpose(mod_part, (1, 0, 2))], True, "scatter_mod")
    mod = jnp.transpose(mod_rows, (1, 0, 2)).reshape(DEPTH, 6 * D_MODEL) + ada_b
    mods = mod.reshape(DEPTH, 6, 1, D_MODEL)

    saved = []
    xl = x0
    for l in range(DEPTH):
        sh1, sc1, g1, sh2, sc2, g2 = (mods[l, r] for r in range(6))
        n1 = norm1_g[l].reshape(1, D_MODEL)
        n2 = norm2_g[l].reshape(1, D_MODEL)
        gq = jnp.tile(q_norm_g[l], FOX_HEADS).reshape(1, 512)
        gk = jnp.tile(k_norm_g[l], FOX_HEADS).reshape(1, 512)
        bf = jnp.pad(b_forget[l], (0, 128 - FOX_HEADS)).reshape(1, 128)
        gs = sgu_norm_g[l].reshape(1, 256)
        bexp = jnp.repeat(sgu_b[l].T, HEAD_DIM, axis=1)

        if l == 0:
            w_in_l = whole_in(w_in_first)
            gather_rest = dict(arrays=[w[0] for w in shards[1:]], scatter=False, recv=None, slot=None)
        else:
            w_in_l, gather_rest = weights[l][0], None
        h1, h1_t = _norm_mod(xl, n1, sc1, sh1, "norm_mod")
        proj = _mm_nn(h1, w_in_l, "plain", (), "mm_in")
        o_sb, gathered = _sb_fwd(proj, "sb_fwd", gather_rest)
        if gathered:
            weights[0] = (w_in_l,) + whole_rest(*gathered)
        qn, kn, cfc, cfr, cfe = _fox_prep(proj, gq, gk, bf, "fox_prep")
        qkb = (1.02 * HEAD_DIM ** 0.5 * jnp.max(jnp.abs(q_norm_g[l])) * jnp.max(jnp.abs(k_norm_g[l]))).reshape(1, 1)
        gather_next = None
        if l + 1 < DEPTH:
            gather_next = dict(arrays=[w[l + 1] for w in shards], scatter=False, recv=None, slot=None)
        o_fox, lse, gathered = _fox_fwd(proj, qn, kn, cfc, cfr, cfe, qkb, "fox_fwd", gather_next)
        if gathered:
            weights[l + 1] = (whole_in(gathered[0]),) + whole_rest(*gathered[1:])
        _, w_out_l, w1_l, w2_l = weights[l]
        o_sgu = _sgu_fwd(proj, sgu_w[l], gs, bexp, "sgu_fwd")
        x_mid, y1, mixed_t = _mm_out((o_sb, o_fox, o_sgu), w_out_l, xl, g1, "mm_out")
        h2, h2_t = _norm_mod(x_mid, n2, sc2, sh2, "norm_mod")
        r2, r1, r2_t = _mm_nn(h2, w1_l, "relu2", (), "mm_w1")
        x_out, y2 = _mm_nn(r2, w2_l, "resid", (x_mid, g2), "mm_w2")
        saved.append(dict(x_in=xl, h1_t=h1_t, proj=proj, o_sb=o_sb, qn=qn, kn=kn, cfc=cfc, cfr=cfr, cfe=cfe, o_fox=o_fox,
                          lse=lse, mixed_t=mixed_t, x_mid=x_mid, y1=y1, h2_t=h2_t, r2_t=r2_t, r1=r1, y2=y2,
                          n1=n1, n2=n2, gq=gq, gk=gk, bf=bf, gs=gs, bexp=bexp, qkb=qkb))
        xl = x_out

    loss_part, dx, dy2, dg2 = _loss_head(xl, target, (saved[-1]["y2"], mods[DEPTH - 1, 5]), "loss_head")

    grads_ready = None
    received = [jnp.zeros((N_DEV,) + w.shape, _XBF) for w in shards]
    small = {k: [None] * DEPTH for k in ("mod", "norm1_g", "norm2_g", "sgu_w", "sgu_b", "sgu_norm_g",
                                         "q_norm_g", "k_norm_g", "b_forget")}
    for l in reversed(range(DEPTH)):
        sv = saved[l]
        sh1, sc1, g1, sh2, sc2, g2 = (mods[l, r] for r in range(6))
        w_in_l, w_out_l, w1_l, w2_l = weights[l]
        da = _mm_nt(dy2, w2_l, "mul2", (sv["r1"],), _BF, "mm_w2_bwd")
        g_w2 = _mm_grad(sv["r2_t"], dy2, "rows", "mm_w2_grad")
        dh2 = _mm_nt(da, w1_l, "plain", (), F32, "mm_w1_bwd")
        g_w1 = _mm_grad(sv["h2_t"], da, "cols", "mm_w1_grad")
        dx_mid, dsh2, dsc2, dn2, dy1, dg1 = _norm_mod_bwd(sv["x_mid"], dh2, dx, sv["n2"], sc2, (sv["y1"], g1),
                                                          "norm_mod_bwd")
        dmix = _mm_nt(dy1, w_out_l, "plain", (), F32, "mm_out_bwd")
        g_out = _mm_grad(sv["mixed_t"], dy1, "rows", "mm_out_grad")
        send_own = None
        if l == 0:
            send_own = dict(arrays=[g_out, g_w1, g_w2], scatter=True, recv=received[1:], slot=0)
        dq_sb, dk_sb, dv_sb, filled = _sb_bwd(sv["proj"], sv["o_sb"], dmix, "sb_bwd", send_own)
        if filled:
            received = received[:1] + filled
        send_prev = None
        if grads_ready is not None:
            send_prev = dict(arrays=grads_ready, scatter=True, recv=received, slot=l + 1)
        dqn, dkn, dv_fox, dlogf, filled = _fox_bwd(sv["proj"], sv["qn"], sv["kn"], sv["cfc"], sv["cfr"], sv["cfe"], sv["qkb"],
                                                 sv["o_fox"], sv["lse"], dmix, "fox_bwd", send_prev)
        if filled:
            received = filled
        dq_fox, dk_fox, dfl, dgq, dgk, dbf = _fox_prep_bwd(sv["proj"], sv["gq"], sv["gk"], sv["bf"], dqn, dkn, dlogf,
                                                           "fox_prep_bwd")
        duc, dvc, dsw, dsb, dsg = _sgu_bwd(sv["proj"], sgu_w[l], sv["gs"], sv["bexp"], dmix, "sgu_bwd")
        dproj = jnp.concatenate(
            [dq_fox, dk_fox, dv_fox.astype(_BF), dq_sb.astype(_BF), dk_sb.astype(_BF), dv_sb.astype(_BF), duc, dvc, dfl,
             jnp.zeros((dfl.shape[0], IN_P - C_FL - 128), _BF)], axis=-1)
        dh1 = _mm_nt(dproj, w_in_l, "plain", (), F32, "mm_in_bwd")
        g_in = _w_in_unlayout(_mm_grad(sv["h1_t"], dproj, None, "mm_in_grad"))
        g_in = jnp.transpose(g_in.reshape(D_MODEL, N_DEV, IN_W // N_DEV), (1, 0, 2))
        grads_ready = [g_in, g_out, g_w1, g_w2]
        small["mod"][l] = [dg1, dsh2, dsc2, dg2]
        below = (saved[l - 1]["y2"], mods[l - 1, 5]) if l > 0 else None
        dx, dsh1, dsc1, dn1, *below_grads = _norm_mod_bwd(sv["x_in"], dh1, dx_mid, sv["n1"], sc1, below, "norm_mod_bwd")
        if below_grads:
            dy2, dg2 = below_grads
        small["mod"][l] = jnp.concatenate([dsh1, dsc1] + small["mod"][l], axis=-1).reshape(-1)
        small["norm1_g"][l] = dn1.reshape(-1)
        small["norm2_g"][l] = dn2.reshape(-1)
        small["sgu_w"][l] = dsw
        small["sgu_b"][l] = dsb.reshape(SGU_CHUNK, SGU_GROUPS, HEAD_DIM).sum(-1).T
        small["sgu_norm_g"][l] = dsg.reshape(SGU_GROUPS, HEAD_DIM)
        small["q_norm_g"][l] = dgq.reshape(FOX_HEADS, HEAD_DIM).sum(0)
        small["k_norm_g"][l] = dgk.reshape(FOX_HEADS, HEAD_DIM).sum(0)
        small["b_forget"][l] = dbf[0, :FOX_HEADS]

    parts = {k: jnp.stack(v) for k, v in small.items()}
    parts["ada_b"] = parts.pop("mod")
    parts["loss"] = loss_part
    (small_all,) = _exchange([_pack_small(parts)], False, "gather_small")
    zero1 = jnp.zeros((1,), F32)
    packs = [_pack_small(dict(ada_b=a, norm1_g=b, norm2_g=cc, sgu_w=d, sgu_b=e, sgu_norm_g=f, q_norm_g=g, k_norm_g=h,
                              b_forget=i, loss=zero1))
             for a, b, cc, d, e, f, g, h, i in (
                 (ada_b, norm1_g, norm2_g, sgu_w, sgu_b, sgu_norm_g, q_norm_g, k_norm_g, b_forget),
                 (m_ada_b, m_norm1_g, m_norm2_g, m_sgu_w, m_sgu_b, m_sgu_norm_g, m_q_norm_g, m_k_norm_g, m_b_forget),
                 (v_ada_b, v_norm1_g, v_norm2_g, v_sgu_w, v_sgu_b, v_sgu_norm_g, v_q_norm_g, v_k_norm_g, v_b_forget))]
    small_shapes = dict(ada_b=ada_b.shape, norm1_g=norm1_g.shape, norm2_g=norm2_g.shape, sgu_w=sgu_w.shape,
                        sgu_b=sgu_b.shape, sgu_norm_g=sgu_norm_g.shape, q_norm_g=q_norm_g.shape,
                        k_norm_g=k_norm_g.shape, b_forget=b_forget.shape, loss=())
    s_out = [_unpack_small(a, small_shapes) for a in _adamw(small_all, *packs, "adamw_small")]

    dmod_all = small_all[:, :DEPTH * 6 * D_MODEL // 128, :].reshape(N_DEV, DEPTH, 6 * D_MODEL)
    dmod_mine = lax.dynamic_slice_in_dim(dmod_all, me * n_ada, n_ada, axis=2)
    g_ada = _ada_bwd(c_all, jnp.transpose(dmod_mine, (1, 0, 2)), "ada_bwd")
    ada_out = _adamw(g_ada.reshape(1, DEPTH * D_MODEL, n_ada), ada_w.reshape(-1, n_ada), m_ada_w.reshape(-1, n_ada),
                     v_ada_w.reshape(-1, n_ada), "adamw_ada")
    ada_out = [a.reshape(ada_w.shape) for a in ada_out]

    (ri,) = _exchange(grads_ready[:1], True, "scatter_grads", recv=received[:1], slot=0)
    _, ro, r1, r2 = received

    def big(slots, w, m, v, name):
        cdim = w.shape[-1]
        outs = _adamw(slots.reshape(N_DEV, -1, cdim), w.reshape(-1, cdim), m.reshape(-1, cdim), v.reshape(-1, cdim), name)
        return [a.reshape(w.shape) for a in outs]

    in_out = big(ri, w_in, m_w_in, v_w_in, "adamw_in")
    out_out = big(ro, w_out, m_w_out, v_w_out, "adamw_out")
    w1_out = big(r1, mlp_w1, m_mlp_w1, v_mlp_w1, "adamw_w1")
    w2_out = big(r2, mlp_w2, m_mlp_w2, v_mlp_w2, "adamw_w2")

    def leaves(idx):
        sm = s_out[idx]
        return [ada_out[idx], sm["ada_b"], sm["norm1_g"], sm["norm2_g"], in_out[idx], sm["b_forget"], sm["q_norm_g"],
                sm["k_norm_g"], sm["sgu_norm_g"], sm["sgu_w"], sm["sgu_b"], out_out[idx], w1_out[idx], w2_out[idx]]

    loss = s_out[0]["loss"]
    grad_x = dx.reshape(x.shape)
    return (loss, grad_x, *leaves(0), *leaves(1), *leaves(2), *leaves(3))
```

```python
import jax
import jax.numpy as jnp
from jax import lax
from jax.experimental import pallas as pl
from jax.experimental.pallas import tpu as pltpu

F32 = jnp.float32
_BF = jnp.bfloat16
_XBF = jnp.bfloat16

N_DEV = 8
D_MODEL = 1024
DEPTH = 4
HEAD_DIM = 64
SB_HEADS = 4
FOX_HEADS = 8
SGU_GROUPS = 4
SGU_CHUNK = 128
D_FF = 4096
EPS = 1e-6
IN_W = 2824
FL_SRC = 2304
UC_SRC = 2312

C_QB, C_KB, C_VB = 0, 512, 1024
C_QA, C_KA, C_VA = 1536, 1792, 2048
C_UC, C_VC = 2304, 2560
C_FL = 2816
IN_P = 3072
M_SB, M_FOX, M_SGU = 0, 256, 768

ADAM_LR = 0.001
ADAM_B1 = 0.9
ADAM_B2 = 0.999
ADAM_EPS = 1e-08
ADAM_WD = 0.01
ADAM_STEP = 10

ATT_T = 256
ROW_T = 512
MM_T = 1024
GRAD_TS = 4096
ROW_CHUNK = 64
NEG = -1e30
UNDERFLOW = -90.0
NO_BLOCK = 1e30

MESH = pl.DeviceIdType.MESH
ARB = pltpu.ARBITRARY
PAR = pltpu.PARALLEL
HIGHEST = lax.Precision.HIGHEST


def _pcall(body, **kw):
    return pl.pallas_call(body, **kw)


def _cp(*sem):
    return pltpu.CompilerParams(dimension_semantics=tuple(sem))


def _dot(a, b, ca, cb):
    return lax.dot_general(a, b, (((ca,), (cb,)), ((), ())), preferred_element_type=F32)


def _dot_f32(a, b, ca, cb):
    return lax.dot_general(a, b, (((ca,), (cb,)), ((), ())), precision=HIGHEST, preferred_element_type=F32)


def _split(v, parts):
    out = []
    r = v
    for _ in range(parts - 1):
        p = r.astype(_BF)
        out.append(p)
        r = r - p.astype(F32)
    out.append(r.astype(_BF))
    return out


def _dot_exact(v, m, ca, cb, parts, v_left=True):
    out = None
    for p in _split(v, parts):
        term = _dot(p, m, ca, cb) if v_left else _dot(m, p, ca, cb)
        out = term if out is None else out + term
    return out


def _iota2(shape, dim):
    return lax.broadcasted_iota(jnp.int32, shape, dim)


def _my_pos():
    return lax.axis_index("x"), lax.axis_index("y"), lax.axis_index("c")


def _flip(pos, k):
    x, y, c = pos
    px = 1 - x if (k >> 2) & 1 else x
    py = 1 - y if (k >> 1) & 1 else y
    pc = 1 - c if k & 1 else c
    return px, py, pc


def _lin(pos):
    return 4 * pos[0] + 2 * pos[1] + pos[2]


def _exchange_ops(ins, outs, sems, scatter, slot):
    send_sems, recv_sems, local_sems = sems
    n = len(ins)

    def copies(with_recvs):
        me = _my_pos()
        me_i = _lin(me)
        dst = lambda a, j: outs[a].at[j] if slot is None else outs[a].at[j, slot]
        mine = lambda a: ins[a].at[me_i] if scatter else ins[a]
        local = [pltpu.make_async_copy(mine(a), dst(a, me_i), local_sems.at[a]) for a in range(n)]
        sends, recvs = [], []
        for a in range(n):
            for k in range(1, N_DEV):
                peer = _flip(me, k)
                pair = dict(send_sem=send_sems.at[a, k - 1], recv_sem=recv_sems.at[a, k - 1],
                            device_id=peer, device_id_type=MESH)
                src = ins[a].at[_lin(peer)] if scatter else ins[a]
                sends.append(pltpu.make_async_remote_copy(src_ref=src, dst_ref=dst(a, me_i), **pair))
                if with_recvs:
                    recvs.append(pltpu.make_async_remote_copy(src_ref=mine(a), dst_ref=dst(a, _lin(peer)), **pair))
        return local, sends, recvs

    def start():
        local, sends, _ = copies(False)
        for cp in local + sends:
            cp.start()

    def wait():
        local, sends, recvs = copies(True)
        for snd, rcv in zip(sends, recvs):
            snd.wait_send()
            rcv.wait_recv()
        for cp in local:
            cp.wait()

    return start, wait


def _comm_operands(comm):
    arrays, recv = comm["arrays"], comm["recv"]
    n = len(arrays)
    any_spec = pl.BlockSpec(memory_space=pl.ANY)
    if recv is not None:
        out_shapes = [jax.ShapeDtypeStruct(r.shape, r.dtype) for r in recv]
    elif comm["scatter"]:
        out_shapes = [jax.ShapeDtypeStruct(a.shape, a.dtype) for a in arrays]
    else:
        out_shapes = [jax.ShapeDtypeStruct((N_DEV,) + a.shape, a.dtype) for a in arrays]
    operands = list(arrays) + (list(recv) if recv is not None else [])
    sems = [pltpu.SemaphoreType.DMA((n, N_DEV - 1)), pltpu.SemaphoreType.DMA((n, N_DEV - 1)),
            pltpu.SemaphoreType.DMA((n,))]
    return operands, [any_spec] * len(operands), out_shapes, [any_spec] * n, sems


def _with_comm(body, comm, n_in, n_out, first_step, last_step, n_scratch=0):
    if comm is None:
        return body
    n = len(comm["arrays"])
    n_cin = n if comm["recv"] is None else 2 * n

    def wrapped(*refs):
        ins, cins = refs[:n_in], refs[n_in:n_in + n]
        o0 = n_in + n_cin
        outs, couts = refs[o0:o0 + n_out], refs[o0 + n_out:o0 + n_out + n]
        s0 = o0 + n_out + n
        scratch, sems = refs[s0:s0 + n_scratch], refs[s0 + n_scratch:]
        start, wait = _exchange_ops(cins, couts, sems, comm["scatter"], comm["slot"])
        pl.when(first_step())(start)
        body(*ins, *outs, *scratch)
        pl.when(last_step())(wait)

    return wrapped


def _comm_aliases(comm, n_in, n_out):
    if comm is None or comm["recv"] is None:
        return {}
    n = len(comm["arrays"])
    return {n_in + n + a: n_out + a for a in range(n)}


def _exchange(arrays, scatter, name, recv=None, slot=None):
    comm = dict(arrays=arrays, scatter=scatter, recv=recv, slot=slot)
    operands, in_specs, out_shapes, out_specs, sems = _comm_operands(comm)
    n = len(arrays)

    def body(*refs):
        n_cin = len(operands)
        start, wait = _exchange_ops(refs[:n], refs[n_cin:n_cin + n], refs[n_cin + n:], scatter, slot)
        start()
        wait()

    outs = _pcall(body, name=name, out_shape=out_shapes, in_specs=in_specs, out_specs=out_specs, scratch_shapes=sems,
                  input_output_aliases=_comm_aliases(comm, 0, 0))(*operands)
    return list(outs)


def _row_spec(t, w, col=0):
    return pl.BlockSpec((t, w), lambda i: (i, col))


def _vec_spec(w, col=0):
    return pl.BlockSpec((1, w), lambda i: (0, col))


def _norm_mod(x, g, sc, sh, name):
    s, d = x.shape
    t = min(ROW_T, s)

    def body(x_ref, g_ref, sc_ref, sh_ref, h_ref, ht_ref):
        xf = x_ref[...]
        rstd = lax.rsqrt(jnp.mean(xf * xf, axis=-1, keepdims=True) + EPS)
        y = xf * rstd * g_ref[...]
        h = y * (1.0 + sc_ref[...]) + sh_ref[...]
        h_ref[...] = h.astype(h_ref.dtype)
        ht_ref[...] = h.T.astype(ht_ref.dtype)

    return _pcall(
        body, name=name, grid=(s // t,),
        out_shape=[jax.ShapeDtypeStruct((s, d), _BF), jax.ShapeDtypeStruct((d, s), _BF)],
        in_specs=[_row_spec(t, d), _vec_spec(d), _vec_spec(d), _vec_spec(d)],
        out_specs=[_row_spec(t, d), pl.BlockSpec((d, t), lambda i: (0, i))], compiler_params=_cp(PAR),
    )(x, g, sc, sh)


def _res_bwd_step(i, dx, y_ref, gate_ref, dy_ref, dgate_ref):
    dy_ref[...] = (dx * gate_ref[...]).astype(dy_ref.dtype)

    @pl.when(i == 0)
    def _():
        dgate_ref[...] = jnp.zeros_like(dgate_ref)

    dgate_ref[...] += jnp.sum(dx * y_ref[...], axis=0, keepdims=True)


def _norm_mod_bwd(x, dh, dres, g, sc, branch, name):
    s, d = x.shape
    t = min(ROW_T, s)
    n_branch = 0 if branch is None else 2

    ch = min(ROW_CHUNK, t)

    def body(x_ref, dh_ref, dres_ref, g_ref, sc_ref, *rest):
        dx_ref, dsh_ref, dsc_ref, dg_ref = rest[n_branch:n_branch + 4]
        i = pl.program_id(0)
        gv = g_ref[...]
        sc1 = 1.0 + sc_ref[...]
        gain = gv * sc1
        n_sum = 2 if branch is None else 3

        def rows(r, sums):
            sl = pl.ds(pl.multiple_of(r * ch, ch), ch)
            xf = x_ref[sl, :]
            dh = dh_ref[sl, :]
            rstd = lax.rsqrt(jnp.mean(xf * xf, axis=-1, keepdims=True) + EPS)
            xhat = xf * rstd
            dh_xhat = dh * xhat
            dxh = dh * gain
            dx = dres_ref[sl, :] + rstd * (dxh - xhat * jnp.mean(dh_xhat * gain, axis=-1, keepdims=True))
            dx_ref[sl, :] = dx
            new = [sums[0] + jnp.sum(dh, axis=0, keepdims=True), sums[1] + jnp.sum(dh_xhat, axis=0, keepdims=True)]
            if branch is not None:
                rest[n_branch + 4][sl, :] = (dx * rest[1][...]).astype(_BF)
                new.append(sums[2] + jnp.sum(dx * rest[0][sl, :], axis=0, keepdims=True))
            return tuple(new)

        sums = lax.fori_loop(0, t // ch, rows, tuple(jnp.zeros((1, d), F32) for _ in range(n_sum)))
        parts = [sums[0], sums[1] * gv, sums[1] * sc1] + list(sums[2:])
        totals = [dsh_ref, dsc_ref, dg_ref] + ([rest[n_branch + 5]] if branch is not None else [])

        @pl.when(i == 0)
        def _():
            for ref in totals:
                ref[...] = jnp.zeros_like(ref)

        for ref, part in zip(totals, parts):
            ref[...] += part

    vec = jax.ShapeDtypeStruct((1, d), F32)
    out_shape = [jax.ShapeDtypeStruct((s, d), F32), vec, vec, vec]
    in_specs = [_row_spec(t, d), _row_spec(t, d), _row_spec(t, d), _vec_spec(d), _vec_spec(d)]
    out_specs = [_row_spec(t, d), _vec_spec(d), _vec_spec(d), _vec_spec(d)]
    if branch is not None:
        in_specs += [_row_spec(t, d), _vec_spec(d)]
        out_shape += [jax.ShapeDtypeStruct((s, d), _BF), vec]
        out_specs += [_row_spec(t, d), _vec_spec(d)]
    return _pcall(
        body, name=name, grid=(s // t,), out_shape=out_shape, in_specs=in_specs, out_specs=out_specs,
        compiler_params=_cp(ARB),
    )(x, dh, dres, g, sc, *(branch or ()))


def _loss_head(y, target, branch, name):
    s, d = y.shape
    t = min(ROW_T, s)

    def body(y_ref, t_ref, yb_ref, gate_ref, loss_ref, dy_ref, dyb_ref, dgate_ref):
        i = pl.program_id(0)
        diff = y_ref[...] - t_ref[...]
        dy = diff * (1.0 / d)
        dy_ref[...] = dy

        @pl.when(i == 0)
        def _():
            loss_ref[...] = jnp.zeros_like(loss_ref)

        rows = jnp.sum(diff * diff, axis=-1, keepdims=True)
        loss_ref[...] += (0.5 / d) * jnp.sum(rows, axis=0, keepdims=True)
        _res_bwd_step(i, dy, yb_ref, gate_ref, dyb_ref, dgate_ref)

    return _pcall(
        body, name=name, grid=(s // t,),
        out_shape=[jax.ShapeDtypeStruct((1, 1), F32), jax.ShapeDtypeStruct((s, d), F32),
                   jax.ShapeDtypeStruct((s, d), _BF), jax.ShapeDtypeStruct((1, d), F32)],
        in_specs=[_row_spec(t, d), _row_spec(t, d), _row_spec(t, d), _vec_spec(d)],
        out_specs=[pl.BlockSpec((1, 1), lambda i: (0, 0)), _row_spec(t, d), _row_spec(t, d), _vec_spec(d)],
        compiler_params=_cp(ARB),
    )(y, target, *branch)


def _contract_tile(kdim):
    if kdim > MM_T and (kdim // 2) % 128 == 0:
        return kdim // 2
    return min(MM_T, kdim)


def _mm_nn(a, b, epi, extras, name):
    m, kdim = a.shape
    n = b.shape[1]
    tm, tn, tk = min(MM_T, m), min(MM_T, n), _contract_tile(kdim)
    nk = kdim // tk
    n_extra = len(extras)
    n_out = {"plain": 1, "resid": 2, "relu2": 2}[epi]

    def finish(y, extra_refs, out_refs):
        if epi == "plain":
            out_refs[0][...] = y.astype(out_refs[0].dtype)
        elif epi == "resid":
            x_ref, g_ref = extra_refs
            out_refs[0][...] = x_ref[...] + g_ref[...] * y
            out_refs[1][...] = y
        else:
            r = jnp.maximum(y, 0.0)
            out_refs[0][...] = (r * r).astype(out_refs[0].dtype)
            out_refs[1][...] = (r * r).T.astype(out_refs[1].dtype)

    def body(a_ref, b_ref, *rest):
        extra_refs = rest[:n_extra]
        out_refs = rest[n_extra:n_extra + n_out]
        part = _dot(a_ref[...].astype(_BF), b_ref[...].astype(_BF), 1, 0)
        if nk == 1:
            finish(part, extra_refs, out_refs)
        else:
            acc_ref = rest[-1]
            k = pl.program_id(2)

            @pl.when(k == 0)
            def _():
                acc_ref[...] = part

            @pl.when(k > 0)
            def _():
                acc_ref[...] += part

            @pl.when(k == nk - 1)
            def _():
                finish(acc_ref[...], extra_refs, out_refs)

    tile = pl.BlockSpec((tm, tn), lambda i, j, k: (i, j))
    in_specs = [pl.BlockSpec((tm, tk), lambda i, j, k: (i, k)), pl.BlockSpec((tk, tn), lambda i, j, k: (k, j))]
    if epi == "plain":
        out_shape = [jax.ShapeDtypeStruct((m, n), F32)]
    elif epi == "resid":
        in_specs += [tile, pl.BlockSpec((1, tn), lambda i, j, k: (0, j))]
        out_shape = [jax.ShapeDtypeStruct((m, n), F32)] * 2
    out_specs = [tile] * n_out
    if epi == "relu2":
        out_shape = [jax.ShapeDtypeStruct((m, n), _BF), jax.ShapeDtypeStruct((n, m), _BF)]
        out_specs[1] = pl.BlockSpec((tn, tm), lambda i, j, k: (j, i))
    outs = _pcall(
        body, name=name, grid=(m // tm, n // tn, nk), out_shape=out_shape,
        in_specs=in_specs, out_specs=out_specs,
        scratch_shapes=[pltpu.VMEM((tm, tn), F32)] if nk > 1 else [],
        compiler_params=_cp(PAR, PAR, ARB),
    )(a, b, *extras)
    return outs[0] if n_out == 1 else outs


def _mm_out(parts, w, x, gate, name):
    s, n = x.shape
    widths = [p.shape[1] for p in parts]
    kdim = sum(widths)
    tm = min(ROW_T, s)

    def body(a0_ref, a1_ref, a2_ref, w_ref, x_ref, g_ref, xo_ref, y_ref, mt_ref):
        mixed = jnp.concatenate([a0_ref[...], a1_ref[...], a2_ref[...]], axis=-1)
        y = _dot(mixed.astype(_BF), w_ref[...], 1, 0)
        xo_ref[...] = x_ref[...] + g_ref[...] * y
        y_ref[...] = y
        mt_ref[...] = mixed.T.astype(mt_ref.dtype)

    row = lambda width: pl.BlockSpec((tm, width), lambda i: (i, 0))
    return _pcall(
        body, name=name, grid=(s // tm,),
        out_shape=[jax.ShapeDtypeStruct((s, n), F32), jax.ShapeDtypeStruct((s, n), F32),
                   jax.ShapeDtypeStruct((kdim, s), _BF)],
        in_specs=[row(widths[0]), row(widths[1]), row(widths[2]), pl.BlockSpec((kdim, n), lambda i: (0, 0)),
                  row(n), pl.BlockSpec((1, n), lambda i: (0, 0))],
        out_specs=[row(n), row(n), pl.BlockSpec((kdim, tm), lambda i: (0, i))], compiler_params=_cp(PAR),
    )(*parts, w, x, gate)


def _mm_nt(a, b, epi, extras, out_dtype, name):
    m, kdim = a.shape
    n = b.shape[0]
    tm, tn, tk = min(MM_T, m), min(MM_T, n), _contract_tile(kdim)
    nk = kdim // tk
    n_extra = len(extras)

    def finish(y, extra_refs, o_ref):
        if epi == "mul2":
            y = y * (2.0 * jnp.sqrt(extra_refs[0][...].astype(F32)))
        o_ref[...] = y.astype(o_ref.dtype)

    def body(a_ref, b_ref, *rest):
        extra_refs = rest[:n_extra]
        o_ref = rest[n_extra]
        part = _dot(a_ref[...].astype(_BF), b_ref[...].astype(_BF), 1, 1)
        if nk == 1:
            finish(part, extra_refs, o_ref)
        else:
            acc_ref = rest[-1]
            k = pl.program_id(2)

            @pl.when(k == 0)
            def _():
                acc_ref[...] = part

            @pl.when(k > 0)
            def _():
                acc_ref[...] += part

            @pl.when(k == nk - 1)
            def _():
                finish(acc_ref[...], extra_refs, o_ref)

    tile = pl.BlockSpec((tm, tn), lambda i, j, k: (i, j))
    in_specs = [pl.BlockSpec((tm, tk), lambda i, j, k: (i, k)), pl.BlockSpec((tn, tk), lambda i, j, k: (j, k))]
    in_specs += [tile] * n_extra
    return _pcall(
        body, name=name, grid=(m // tm, n // tn, nk), out_shape=jax.ShapeDtypeStruct((m, n), out_dtype),
        in_specs=in_specs, out_specs=tile,
        scratch_shapes=[pltpu.VMEM((tm, tn), F32)] if nk > 1 else [],
        compiler_params=_cp(PAR, PAR, ARB),
    )(a, b, *extras)


def _mm_grad(at, b, split, name):
    m, s = at.shape
    n = b.shape[1]
    ts = min(GRAD_TS, s)
    ns = s // ts
    if split == "rows":
        tm, tn = min(MM_T, m // N_DEV), min(MM_T, n)
        per = (m // N_DEV) // tm
        out_shape = (N_DEV, m // N_DEV, n)
        out_spec = pl.BlockSpec((None, tm, tn), lambda i, j, k: (i // per, i % per, j))
    elif split == "cols":
        tm, tn = min(MM_T, m), min(MM_T, n // N_DEV)
        per = (n // N_DEV) // tn
        out_shape = (N_DEV, m, n // N_DEV)
        out_spec = pl.BlockSpec((None, tm, tn), lambda i, j, k: (j // per, i, j % per))
    else:
        tm, tn = min(MM_T, m), min(MM_T, n)
        out_shape = (m, n)
        out_spec = pl.BlockSpec((tm, tn), lambda i, j, k: (i, j))

    def body(a_ref, b_ref, o_ref, acc_ref):
        k = pl.program_id(2)
        part = _dot(a_ref[...].astype(_BF), b_ref[...].astype(_BF), 1, 0)

        @pl.when(k == 0)
        def _():
            acc_ref[...] = part

        @pl.when(k > 0)
        def _():
            acc_ref[...] += part

        @pl.when(k == ns - 1)
        def _():
            o_ref[...] = acc_ref[...].astype(o_ref.dtype)

    return _pcall(
        body, name=name, grid=(m // tm, n // tn, ns), out_shape=jax.ShapeDtypeStruct(out_shape, _XBF),
        in_specs=[pl.BlockSpec((tm, ts), lambda i, j, k: (i, k)), pl.BlockSpec((ts, tn), lambda i, j, k: (k, j))],
        out_specs=out_spec, scratch_shapes=[pltpu.VMEM((tm, tn), F32)],
        compiler_params=_cp(PAR, PAR, ARB),
    )(at, b)


def _head_masks():
    lane = _iota2((1, 128), 1)
    return [(lane // HEAD_DIM) == hh for hh in range(2)]


def _untranspose(src_ref, dst_ref, t, nblk):
    def rows(c, _):
        off = pl.multiple_of(c * t, t)
        dst_ref[pl.ds(off, t), :] = src_ref[:, pl.ds(off, t)].T
        return 0

    lax.fori_loop(0, nblk, rows, 0)


def _softplus_parts(z):
    sp = jnp.maximum(z, 0.0) + jnp.log(1.0 + jnp.exp(-jnp.abs(z)))
    return -sp, z - sp


def _sb_key_loop(i, carry, step, lookahead, first_ahead):
    def cond(c):
        return jnp.logical_and(c[0] <= i, jnp.max(jnp.maximum(c[1][0][0], c[1][0][1])) > UNDERFLOW)

    def body(c):
        ahead = lookahead(i - c[0] - 1)
        return c[0] + 1, step(i - c[0], c[1], c[2]), ahead

    return lax.while_loop(cond, body, (jnp.int32(1), carry, first_ahead))[1]


def _sb_fwd(proj, name, comm=None):
    s = proj.shape[0]
    t = min(ATT_T, s)
    nblk = s // t
    scale = HEAD_DIM ** -0.5

    def body(q_ref, k_ref, v_ref, o_ref):
        i = pl.program_id(1)
        masks = _head_masks()
        row = _iota2((t, t), 0)
        col = _iota2((t, t), 1)
        u_strict = (row > col).astype(_BF)
        causal = col < row
        qs = q_ref[...] * scale
        qm = [jnp.where(masks[hh], qs, 0.0).astype(_BF) for hh in range(2)]

        def scores(j):
            kb = k_ref[pl.ds(pl.multiple_of(jnp.maximum(j, 0) * t, t), t), :].astype(_BF)
            return tuple(_dot(qm[hh], kb, 1, 1) for hh in range(2))

        def block(j, carry, qk, diag):
            off = pl.multiple_of(j * t, t)
            vb = v_ref[pl.ds(off, t), :].astype(_BF)
            runs, outs = [], []
            for hh in range(2):
                run, o = carry[0][hh], carry[1][hh]
                z = qk[hh]
                l1, lb = _softplus_parts(z)
                if diag:
                    l1 = jnp.where(causal, l1, 0.0)
                between = run + _dot_exact(l1, u_strict, 1, 0, 2)
                a = jnp.exp(lb + between)
                if diag:
                    a = jnp.where(causal, a, 0.0)
                outs.append(o + _dot_exact(a, vb, 1, 0, 2))
                runs.append(run + jnp.sum(l1, axis=-1, keepdims=True))
            return tuple(runs), tuple(outs)

        zero, zero_o = jnp.zeros((t, 1), F32), jnp.zeros((t, 128), F32)
        ahead = scores(i - 1)
        carry = block(i, ((zero, zero), (zero_o, zero_o)), scores(i), True)
        carry = _sb_key_loop(i, carry, lambda j, c, qk: block(j, c, qk, False), scores, ahead)
        o_ref[...] = jnp.where(masks[0], carry[1][0], carry[1][1])

    kv_spec = lambda c0: pl.BlockSpec((s, 128), lambda p, i: (0, c0 // 128 + p))
    c_operands, c_in_specs, c_out_shapes, c_out_specs, c_sems = _comm_operands(comm) if comm else ([], [], [], [], [])
    outs = _pcall(
        _with_comm(body, comm, 3, 1, *_grid_ends(SB_HEADS // 2, nblk)), name=name, grid=(SB_HEADS // 2, nblk),
        out_shape=[jax.ShapeDtypeStruct((s, 256), F32)] + c_out_shapes,
        in_specs=[pl.BlockSpec((t, 128), lambda p, i: (i, C_QA // 128 + p)), kv_spec(C_KA), kv_spec(C_VA)] + c_in_specs,
        out_specs=[pl.BlockSpec((t, 128), lambda p, i: (i, p))] + c_out_specs,
        scratch_shapes=c_sems, input_output_aliases=_comm_aliases(comm, 3, 1), compiler_params=_cp(ARB, ARB),
    )(proj, proj, proj, *c_operands)
    return outs[0], list(outs[1:])


def _sb_bwd(proj, o, dmix, name, comm=None):
    s = proj.shape[0]
    t = min(ATT_T, s)
    nblk = s // t
    scale = HEAD_DIM ** -0.5

    def body(q_ref, k_ref, v_ref, o_ref, do_ref, dq_ref, dk_ref, dv_ref, dkt_ref, dvt_ref):
        i = pl.program_id(1)
        masks = _head_masks()
        row = _iota2((t, t), 0)
        col = _iota2((t, t), 1)
        u_strict = (row > col).astype(_BF)
        u_incl = (row >= col).astype(_BF)
        causal = col < row

        @pl.when(i == 0)
        def _():
            dkt_ref[...] = jnp.zeros_like(dkt_ref)
            dvt_ref[...] = jnp.zeros_like(dvt_ref)

        qs = q_ref[...] * scale
        dov = do_ref[...]
        ov = o_ref[...]
        qm_f = [jnp.where(masks[hh], qs, 0.0) for hh in range(2)]
        dom_f = [jnp.where(masks[hh], dov, 0.0) for hh in range(2)]
        qm = [qm_f[hh].astype(_BF) for hh in range(2)]
        dom = [dom_f[hh].astype(_BF) for hh in range(2)]
        qm_t = [qm_f[hh].T.astype(_BF) for hh in range(2)]
        dom_t = [dom_f[hh].T.astype(_BF) for hh in range(2)]
        dsum = [jnp.sum(dom[hh].astype(F32) * ov, axis=-1, keepdims=True) for hh in range(2)]

        def scores(j):
            off = pl.multiple_of(jnp.maximum(j, 0) * t, t)
            kb = k_ref[pl.ds(off, t), :].astype(_BF)
            vb = v_ref[pl.ds(off, t), :].astype(_BF)
            return tuple((_dot(qm[hh], kb, 1, 1), _dot(dom[hh], vb, 1, 1)) for hh in range(2))

        def block(j, carry, ahead, diag):
            off = pl.multiple_of(j * t, t)
            kb = k_ref[pl.ds(off, t), :].astype(_BF)
            runs, eruns, dqs = [], [], []
            dk = dv = None
            for hh in range(2):
                run, erun, dq = carry[0][hh], carry[1][hh], carry[2][hh]
                z, da = ahead[hh]
                l1, lb = _softplus_parts(z)
                if diag:
                    l1 = jnp.where(causal, l1, 0.0)
                between = run + _dot_exact(l1, u_strict, 1, 0, 2)
                a = jnp.exp(lb + between)
                if diag:
                    a = jnp.where(causal, a, 0.0)
                g = a * da
                cum = dsum[hh] - (erun + _dot_exact(g, u_incl, 1, 0, 2))
                beta = jnp.exp(lb)
                dz = g * (1.0 - beta) - cum * beta
                if diag:
                    dz = jnp.where(causal, dz, 0.0)
                dzb = dz.astype(_BF)
                dqs.append(dq + _dot(dzb, kb, 1, 0))
                dk_h = _dot(qm_t[hh], dzb, 1, 0)
                dv_h = _dot(dom_t[hh], a.astype(_BF), 1, 0)
                dk = dk_h if dk is None else dk + dk_h
                dv = dv_h if dv is None else dv + dv_h
                runs.append(run + jnp.sum(l1, axis=-1, keepdims=True))
                eruns.append(erun + jnp.sum(g, axis=-1, keepdims=True))
            dkt_ref[:, pl.ds(off, t)] += dk
            dvt_ref[:, pl.ds(off, t)] += dv
            return tuple(runs), tuple(eruns), tuple(dqs)

        zero, zero_q = jnp.zeros((t, 1), F32), jnp.zeros((t, 128), F32)
        ahead = scores(i - 1)
        carry = block(i, ((zero, zero), (zero, zero), (zero_q, zero_q)), scores(i), True)
        carry = _sb_key_loop(i, carry, lambda j, c, a: block(j, c, a, False), scores, ahead)
        dq_ref[...] = jnp.where(masks[0], carry[2][0], carry[2][1]) * scale

        @pl.when(i == nblk - 1)
        def _():
            _untranspose(dkt_ref, dk_ref, t, nblk)
            _untranspose(dvt_ref, dv_ref, t, nblk)

    kv_spec = lambda c0: pl.BlockSpec((s, 128), lambda p, i: (0, c0 // 128 + p))
    blk = lambda c0: pl.BlockSpec((t, 128), lambda p, i: (i, c0 // 128 + p))
    acc = pl.BlockSpec((s, 128), lambda p, i: (0, p))
    shp = jax.ShapeDtypeStruct((s, 256), F32)
    c_operands, c_in_specs, c_out_shapes, c_out_specs, c_sems = _comm_operands(comm) if comm else ([], [], [], [], [])
    outs = _pcall(
        _with_comm(body, comm, 5, 3, *_grid_ends(SB_HEADS // 2, nblk), n_scratch=2), name=name,
        grid=(SB_HEADS // 2, nblk), out_shape=[shp, shp, shp] + c_out_shapes,
        in_specs=[blk(C_QA), kv_spec(C_KA), kv_spec(C_VA), blk(0), blk(M_SB)] + c_in_specs,
        out_specs=[blk(0), acc, acc] + c_out_specs,
        scratch_shapes=[pltpu.VMEM((128, s), F32), pltpu.VMEM((128, s), F32)] + c_sems,
        input_output_aliases=_comm_aliases(comm, 5, 3), compiler_params=_cp(ARB, ARB),
    )(proj, proj, proj, o, dmix, *c_operands)
    return outs[0], outs[1], outs[2], list(outs[3:])


def _group_matrix(width):
    r = _iota2((width, width), 0) // HEAD_DIM
    c = _iota2((width, width), 1) // HEAD_DIM
    return (r == c).astype(_BF)


def _group_mean(v, gm):
    return _dot_exact(v, gm, 1, 0, 2) * (1.0 / HEAD_DIM)


def _fox_prep(proj, gq, gk, bf, name):
    s = proj.shape[0]
    t = min(ATT_T, s)
    nblk = s // t
    assert nblk <= 128

    def body(q_ref, k_ref, fl_ref, gq_ref, gk_ref, bf_ref, qn_ref, kn_ref, cfc_ref, cfr_ref, cfe_ref, carry_ref):
        i = pl.program_id(0)
        gm = _group_matrix(512)
        for src, gref, dst in ((q_ref, gq_ref, qn_ref), (k_ref, gk_ref, kn_ref)):
            v = src[...]
            rstd = lax.rsqrt(_group_mean(v * v, gm) + EPS)
            dst[...] = (v * rstd * gref[...]).astype(dst.dtype)

        @pl.when(i == 0)
        def _():
            carry_ref[...] = jnp.zeros_like(carry_ref)
            cfe_ref[...] = jnp.full(cfe_ref.shape, NO_BLOCK, F32)

        lane = _iota2((1, 128), 1)
        logf = jnp.where(lane < FOX_HEADS, _softplus_parts(-(fl_ref[...] + bf_ref[...]))[0], 0.0)
        lower = (_iota2((t, t), 0) >= _iota2((t, t), 1)).astype(_BF)
        cf = carry_ref[...] + _dot_exact(logf, lower, 1, 0, 3, v_left=False)
        cfc_ref[...] = cf
        rows = cf.T[:FOX_HEADS, :]
        cfr_ref[...] = rows
        cfe_ref[...] = jnp.where(lane == i, rows[:, t - 1:t], cfe_ref[...])
        carry_ref[...] = cf[t - 1:t, :]

    col = lambda w, c0: pl.BlockSpec((t, w), lambda i: (i, c0 // w))
    v512 = pl.BlockSpec((1, 512), lambda i: (0, 0))
    return _pcall(
        body, name=name, grid=(nblk,),
        out_shape=[jax.ShapeDtypeStruct((s, 512), _BF), jax.ShapeDtypeStruct((s, 512), _BF),
                   jax.ShapeDtypeStruct((s, 128), F32), jax.ShapeDtypeStruct((FOX_HEADS, s), F32),
                   jax.ShapeDtypeStruct((FOX_HEADS, 128), F32)],
        in_specs=[col(512, C_QB), col(512, C_KB), col(128, C_FL), v512, v512, pl.BlockSpec((1, 128), lambda i: (0, 0))],
        out_specs=[col(512, 0), col(512, 0), col(128, 0), pl.BlockSpec((FOX_HEADS, t), lambda i: (0, i)),
                   pl.BlockSpec((FOX_HEADS, 128), lambda i: (0, 0))],
        scratch_shapes=[pltpu.VMEM((1, 128), F32)], compiler_params=_cp(ARB),
    )(proj, proj, proj, gq, gk, bf)


def _fox_cq(cfc_blk, head):
    lane = _iota2((1, 128), 1)
    return jnp.sum(jnp.where(lane == head, cfc_blk, 0.0), axis=-1, keepdims=True)


def _fox_ck(cfr_ref, off, t, head):
    sub = _iota2((FOX_HEADS, 1), 0)
    return jnp.sum(jnp.where(sub == head, cfr_ref[:, pl.ds(off, t)], 0.0), axis=0, keepdims=True)


def _fox_blocks_needed(top, cfe_ref, head, i):
    sub = _iota2((FOX_HEADS, 1), 0)
    lane = _iota2((1, 128), 1)
    last_key = jnp.sum(jnp.where(sub == head, cfe_ref[...], 0.0), axis=0, keepdims=True)
    need = jnp.logical_and(lane < i, top - last_key > UNDERFLOW)
    return jnp.sum(need.astype(jnp.int32))


def _grid_ends(n_outer, n_inner):
    first = lambda: jnp.logical_and(pl.program_id(0) == 0, pl.program_id(1) == 0)
    last = lambda: jnp.logical_and(pl.program_id(0) == n_outer - 1, pl.program_id(1) == n_inner - 1)
    return first, last


def _fox_fwd(proj, qn, kn, cfc, cfr, cfe, qk_bound, name, comm=None):
    s = proj.shape[0]
    t = min(ATT_T, s)
    nblk = s // t
    scale = HEAD_DIM ** -0.5

    def body(q_ref, k_ref, v_ref, cfc_ref, cfr_ref, cfe_ref, qkb_ref, o_ref, lse_ref):
        p = pl.program_id(0)
        i = pl.program_id(1)
        masks = _head_masks()
        lane = _iota2((1, 128), 1)
        valid = _iota2((t, t), 1) <= _iota2((t, t), 0)
        qv = q_ref[...]
        cfc_blk = cfc_ref[...]
        heads = [2 * p + hh for hh in range(2)]
        qm = [(jnp.where(masks[hh], qv, 0) * scale).astype(_BF) for hh in range(2)]
        cq = [_fox_cq(cfc_blk, heads[hh]) for hh in range(2)]

        def scores(j):
            kb = k_ref[pl.ds(pl.multiple_of(jnp.maximum(j, 0) * t, t), t), :]
            return tuple(_dot(qm[hh], kb, 1, 1) for hh in range(2))

        def block(j, carry, qk, diag):
            off = pl.multiple_of(j * t, t)
            vb = v_ref[pl.ds(off, t), :].astype(_BF)
            ms, ls, accs = [], [], []
            for hh in range(2):
                m, l, acc = carry[0][hh], carry[1][hh], carry[2][hh]
                z = qk[hh] + (cq[hh] - _fox_ck(cfr_ref, off, t, heads[hh]))
                if diag:
                    z = jnp.where(valid, z, NEG)
                m_new = jnp.maximum(m, jnp.max(z, axis=-1, keepdims=True))
                pe = jnp.exp(z - m_new)
                alpha = jnp.exp(m - m_new)
                ms.append(m_new)
                ls.append(alpha * l + jnp.sum(pe, axis=-1, keepdims=True))
                accs.append(alpha * acc + _dot(pe.astype(_BF), vb, 1, 0))
            return tuple(ms), tuple(ls), tuple(accs)

        def step(jj, c):
            ahead = scores(i - jj - 2)
            return block(i - jj - 1, c[0], c[1], False), ahead

        neg, zero, zero_o = jnp.full((t, 1), NEG, F32), jnp.zeros((t, 1), F32), jnp.zeros((t, 128), F32)
        ahead = scores(i - 1)
        carry = block(i, ((neg, neg), (zero, zero), (zero_o, zero_o)), scores(i), True)
        needed = [_fox_blocks_needed(jnp.max(qkb_ref[...] + cq[hh] - carry[0][hh]), cfe_ref, heads[hh], i)
                  for hh in range(2)]
        m, l, acc = lax.fori_loop(0, jnp.maximum(needed[0], needed[1]), step, (carry, ahead))[0]
        o_ref[...] = jnp.where(masks[0], acc[0] / l[0], acc[1] / l[1])
        lse_ref[...] = jnp.where(lane == 0, m[0] + jnp.log(l[0]), jnp.where(lane == 1, m[1] + jnp.log(l[1]), 0.0))

    c_operands, c_in_specs, c_out_shapes, c_out_specs, c_sems = _comm_operands(comm) if comm else ([], [], [], [], [])
    outs = _pcall(
        _with_comm(body, comm, 7, 2, *_grid_ends(FOX_HEADS // 2, nblk)), name=name, grid=(FOX_HEADS // 2, nblk),
        out_shape=[jax.ShapeDtypeStruct((s, 512), F32), jax.ShapeDtypeStruct((FOX_HEADS // 2, s, 128), F32)] + c_out_shapes,
        in_specs=[pl.BlockSpec((t, 128), lambda p, i: (i, p)), pl.BlockSpec((s, 128), lambda p, i: (0, p)),
                  pl.BlockSpec((s, 128), lambda p, i: (0, C_VB // 128 + p)),
                  pl.BlockSpec((t, 128), lambda p, i: (i, 0)), pl.BlockSpec((FOX_HEADS, s), lambda p, i: (0, 0)),
                  pl.BlockSpec((FOX_HEADS, 128), lambda p, i: (0, 0)), pl.BlockSpec((1, 1), lambda p, i: (0, 0))] + c_in_specs,
        out_specs=[pl.BlockSpec((t, 128), lambda p, i: (i, p)),
                   pl.BlockSpec((None, t, 128), lambda p, i: (p, i, 0))] + c_out_specs,
        scratch_shapes=c_sems, input_output_aliases=_comm_aliases(comm, 7, 2),
        compiler_params=_cp(ARB, ARB),
    )(qn, kn, proj, cfc, cfr, cfe, qk_bound, *c_operands)
    return outs[0], outs[1], list(outs[2:])


def _fox_bwd(proj, qn, kn, cfc, cfr, cfe, qk_bound, o, lse, dmix, name, comm=None):
    s = proj.shape[0]
    t = min(ATT_T, s)
    nblk = s // t
    scale = HEAD_DIM ** -0.5

    def body(q_ref, k_ref, v_ref, cfc_ref, cfr_ref, cfe_ref, qkb_ref, o_ref, lse_ref, do_ref,
             dq_ref, dk_ref, dv_ref, dcr_ref, dkt_ref, dvt_ref):
        p = pl.program_id(0)
        i = pl.program_id(1)
        masks = _head_masks()
        lane = _iota2((1, 128), 1)
        valid = _iota2((t, t), 1) <= _iota2((t, t), 0)
        before = (_iota2((t, t), 0) < _iota2((t, t), 1)).astype(_BF)

        @pl.when(i == 0)
        def _():
            dkt_ref[...] = jnp.zeros_like(dkt_ref)
            dvt_ref[...] = jnp.zeros_like(dvt_ref)

        @pl.when(jnp.logical_and(p == 0, i == 0))
        def _():
            dcr_ref[...] = jnp.zeros_like(dcr_ref)

        head_row = _iota2((FOX_HEADS, 1), 0)
        qv = q_ref[...]
        cfc_blk = cfc_ref[...]
        dov = do_ref[...]
        ov = o_ref[...]
        lse_blk = lse_ref[...]
        heads = [2 * p + hh for hh in range(2)]
        qm = [(jnp.where(masks[hh], qv, 0) * scale).astype(_BF) for hh in range(2)]
        dom = [jnp.where(masks[hh], dov, 0.0).astype(_BF) for hh in range(2)]
        qm_t = [qm[hh].astype(F32).T.astype(_BF) for hh in range(2)]
        dom_t = [dom[hh].astype(F32).T.astype(_BF) for hh in range(2)]
        dsum = [jnp.sum(dom[hh].astype(F32) * ov, axis=-1, keepdims=True) for hh in range(2)]
        lse_h = [jnp.sum(jnp.where(lane == hh, lse_blk, 0.0), axis=-1, keepdims=True) for hh in range(2)]
        cq = [_fox_cq(cfc_blk, heads[hh]) for hh in range(2)]
        needed = [_fox_blocks_needed(jnp.max(qkb_ref[...] + cq[hh] - lse_h[hh]), cfe_ref, heads[hh], i)
                  for hh in range(2)]
        first = i - jnp.maximum(needed[0], needed[1])

        def scores(j):
            off = pl.multiple_of(j * t, t)
            kb = k_ref[pl.ds(off, t), :]
            vb = v_ref[pl.ds(off, t), :].astype(_BF)
            return tuple((_dot(qm[hh], kb, 1, 1), _dot(dom[hh], vb, 1, 1)) for hh in range(2))

        def block(j, carry, ahead, diag):
            off = pl.multiple_of(j * t, t)
            kb = k_ref[pl.ds(off, t), :]
            dqs, pres = [], []
            dk = dv = None
            for hh in range(2):
                dq, pre = carry[0][hh], carry[1][hh]
                z = ahead[hh][0] + (cq[hh] - _fox_ck(cfr_ref, off, t, heads[hh]))
                pm = jnp.exp(z - lse_h[hh])
                if diag:
                    pm = jnp.where(valid, pm, 0.0)
                ds = pm * (ahead[hh][1] - dsum[hh])
                dsb = ds.astype(_BF)
                dqs.append(dq + _dot(dsb, kb, 1, 0))
                dk_h = _dot(qm_t[hh], dsb, 1, 0)
                dv_h = _dot(dom_t[hh], pm.astype(_BF), 1, 0)
                dk = dk_h if dk is None else dk + dk_h
                dv = dv_h if dv is None else dv + dv_h
                if diag:
                    lower_keys = jnp.where(valid, pre + _dot_exact(ds, before, 1, 0, 2), 0.0)
                    dlogf = jnp.sum(lower_keys, axis=0, keepdims=True)
                else:
                    dlogf = jnp.sum(pre) + _dot_exact(jnp.sum(ds, axis=0, keepdims=True), before, 1, 0, 3)
                dcr_ref[:, pl.ds(off, t)] += jnp.where(head_row == heads[hh], dlogf, 0.0)
                pres.append(pre + jnp.sum(ds, axis=-1, keepdims=True))
            dkt_ref[:, pl.ds(off, t)] += dk
            dvt_ref[:, pl.ds(off, t)] += dv
            return tuple(dqs), tuple(pres)

        zero, zero_q = jnp.zeros((t, 1), F32), jnp.zeros((t, 128), F32)

        def step(j, c):
            ahead = scores(j + 1)
            return block(j, c[0], c[1], False), ahead

        carry, ahead = lax.fori_loop(first, i, step, (((zero_q, zero_q), (zero, zero)), scores(first)))
        dqs, _ = block(i, carry, ahead, True)
        dq_ref[...] = jnp.where(masks[0], dqs[0], dqs[1]) * scale

        @pl.when(i == nblk - 1)
        def _():
            _untranspose(dkt_ref, dk_ref, t, nblk)
            _untranspose(dvt_ref, dv_ref, t, nblk)

    blk = lambda c0: pl.BlockSpec((t, 128), lambda p, i: (i, c0 // 128 + p))
    res = lambda c0: pl.BlockSpec((s, 128), lambda p, i: (0, c0 // 128 + p))
    shp = jax.ShapeDtypeStruct((s, 512), F32)
    c_operands, c_in_specs, c_out_shapes, c_out_specs, c_sems = _comm_operands(comm) if comm else ([], [], [], [], [])
    transposed = [pltpu.VMEM((128, s), F32), pltpu.VMEM((128, s), F32)]
    outs = _pcall(
        _with_comm(body, comm, 10, 4, *_grid_ends(FOX_HEADS // 2, nblk), n_scratch=2), name=name,
        grid=(FOX_HEADS // 2, nblk),
        out_shape=[shp, shp, shp, jax.ShapeDtypeStruct((FOX_HEADS, s), F32)] + c_out_shapes,
        in_specs=[blk(0), res(0), res(C_VB), pl.BlockSpec((t, 128), lambda p, i: (i, 0)),
                  pl.BlockSpec((FOX_HEADS, s), lambda p, i: (0, 0)), pl.BlockSpec((FOX_HEADS, 128), lambda p, i: (0, 0)),
                  pl.BlockSpec((1, 1), lambda p, i: (0, 0)), blk(0),
                  pl.BlockSpec((None, t, 128), lambda p, i: (p, i, 0)), blk(M_FOX)] + c_in_specs,
        out_specs=[blk(0), res(0), res(0), pl.BlockSpec((FOX_HEADS, s), lambda p, i: (0, 0))] + c_out_specs,
        scratch_shapes=transposed + c_sems, input_output_aliases=_comm_aliases(comm, 10, 4),
        compiler_params=_cp(ARB, ARB),
    )(qn, kn, proj, cfc, cfr, cfe, qk_bound, o, lse, dmix, *c_operands)
    return outs[0], outs[1], outs[2], outs[3], list(outs[4:])


def _fox_prep_bwd(proj, gq, gk, bf, dqn, dkn, dlogf, name):
    s = proj.shape[0]
    t = min(ATT_T, s)
    nblk = s // t

    def body(q_ref, k_ref, fl_ref, gq_ref, gk_ref, bf_ref, dqn_ref, dkn_ref, dlogf_ref,
             dq_ref, dk_ref, dfl_ref, dgq_ref, dgk_ref, dbf_ref):
        i = pl.program_id(0)
        gm = _group_matrix(512)

        @pl.when(i == 0)
        def _():
            dgq_ref[...] = jnp.zeros_like(dgq_ref)
            dgk_ref[...] = jnp.zeros_like(dgk_ref)
            dbf_ref[...] = jnp.zeros_like(dbf_ref)

        for src, gref, dyref, dst, dgref in ((q_ref, gq_ref, dqn_ref, dq_ref, dgq_ref),
                                              (k_ref, gk_ref, dkn_ref, dk_ref, dgk_ref)):
            v = src[...]
            dy = dyref[...]
            rstd = lax.rsqrt(_group_mean(v * v, gm) + EPS)
            xhat = v * rstd
            dgref[...] += jnp.sum(dy * xhat, axis=0, keepdims=True)
            dxh = dy * gref[...]
            dst[...] = (rstd * (dxh - xhat * _group_mean(dxh * xhat, gm))).astype(dst.dtype)

        to_lanes = (_iota2((FOX_HEADS, 128), 0) == _iota2((FOX_HEADS, 128), 1)).astype(_BF)
        dlogf = _dot_exact(dlogf_ref[...], to_lanes, 0, 0, 3)
        xv = fl_ref[...] + bf_ref[...]
        e = jnp.exp(-jnp.abs(xv))
        dfl = dlogf * (jnp.where(xv >= 0.0, e, 1.0) / (1.0 + e))
        dfl_ref[...] = dfl.astype(dfl_ref.dtype)
        dbf_ref[...] += jnp.sum(dfl, axis=0, keepdims=True)

    col = lambda w, c0: pl.BlockSpec((t, w), lambda i: (i, c0 // w))
    v512 = pl.BlockSpec((1, 512), lambda i: (0, 0))
    v128 = pl.BlockSpec((1, 128), lambda i: (0, 0))
    return _pcall(
        body, name=name, grid=(nblk,),
        out_shape=[jax.ShapeDtypeStruct((s, 512), _BF), jax.ShapeDtypeStruct((s, 512), _BF),
                   jax.ShapeDtypeStruct((s, 128), _BF), jax.ShapeDtypeStruct((1, 512), F32),
                   jax.ShapeDtypeStruct((1, 512), F32), jax.ShapeDtypeStruct((1, 128), F32)],
        in_specs=[col(512, C_QB), col(512, C_KB), col(128, C_FL), v512, v512, v128, col(512, 0), col(512, 0),
                  pl.BlockSpec((FOX_HEADS, t), lambda i: (0, i))],
        out_specs=[col(512, 0), col(512, 0), col(128, 0), v512, v512, v128], compiler_params=_cp(ARB),
    )(proj, proj, proj, gq, gk, bf, dqn, dkn, dlogf)


GELU_C = 0.7978845608028654
GELU_A = 0.044715


def _gelu(x):
    return 0.5 * x * (1.0 + jnp.tanh(GELU_C * (x + GELU_A * x * x * x)))


def _gelu_grad(x):
    th = jnp.tanh(GELU_C * (x + GELU_A * x * x * x))
    return 0.5 * (1.0 + th) + 0.5 * x * (1.0 - th * th) * (GELU_C * (1.0 + 3.0 * GELU_A * x * x))


def _sgu_tril():
    return _iota2((SGU_CHUNK, SGU_CHUNK), 0) >= _iota2((SGU_CHUNK, SGU_CHUNK), 1)


def _sgu_group_masks():
    lane = _iota2((1, 256), 1)
    return [(lane // HEAD_DIM) == g for g in range(SGU_GROUPS)]


def _sgu_fwd(proj, w, gs, bexp, name):
    s = proj.shape[0]
    t = min(2 * SGU_CHUNK, s)

    def body(u_ref, v_ref, w_ref, gs_ref, b_ref, o_ref):
        gm = _group_matrix(256)
        gmask = _sgu_group_masks()
        tril = _sgu_tril()
        u = _gelu(u_ref[...])
        vg = _gelu(v_ref[...])
        vhat = (vg * lax.rsqrt(_group_mean(vg * vg, gm) + EPS) * gs_ref[...]).astype(_BF)
        for ch in range(t // SGU_CHUNK):
            rows = slice(ch * SGU_CHUNK, (ch + 1) * SGU_CHUNK)
            mixed = b_ref[...]
            for g in range(SGU_GROUPS):
                wg = jnp.where(tril, w_ref[g], 0.0).astype(_BF)
                mixed = jnp.where(gmask[g], mixed + _dot(wg, vhat[rows], 1, 0), mixed)
            o_ref[rows, :] = u[rows] * mixed

    return _pcall(
        body, name=name, grid=(s // t,), out_shape=jax.ShapeDtypeStruct((s, 256), F32),
        in_specs=[_row_spec(t, 256, C_UC // 256), _row_spec(t, 256, C_VC // 256),
                  pl.BlockSpec((SGU_GROUPS, SGU_CHUNK, SGU_CHUNK), lambda i: (0, 0, 0)), _vec_spec(256),
                  pl.BlockSpec((SGU_CHUNK, 256), lambda i: (0, 0))],
        out_specs=_row_spec(t, 256), compiler_params=_cp(PAR),
    )(proj, proj, w, gs, bexp)


def _sgu_bwd(proj, w, gs, bexp, dmix, name):
    s = proj.shape[0]
    t = min(2 * SGU_CHUNK, s)
    nstep = s // t

    def body(u_ref, v_ref, w_ref, gs_ref, b_ref, do_ref, du_ref, dv_ref, dw_ref, db_ref, dgs_ref):
        i = pl.program_id(0)
        gm = _group_matrix(256)
        gmask = _sgu_group_masks()
        tril = _sgu_tril()

        @pl.when(i == 0)
        def _():
            dw_ref[...] = jnp.zeros_like(dw_ref)
            db_ref[...] = jnp.zeros_like(db_ref)
            dgs_ref[...] = jnp.zeros_like(dgs_ref)

        uc = u_ref[...]
        vc = v_ref[...]
        u = _gelu(uc)
        vg = _gelu(vc)
        rstd = lax.rsqrt(_group_mean(vg * vg, gm) + EPS)
        xh = vg * rstd
        gsv = gs_ref[...]
        vhat = (xh * gsv).astype(_BF)
        dov = do_ref[...]
        dm = dov * u
        for ch in range(t // SGU_CHUNK):
            rows = slice(ch * SGU_CHUNK, (ch + 1) * SGU_CHUNK)
            mixed = b_ref[...]
            dvh = jnp.zeros((SGU_CHUNK, 256), F32)
            dmc = dm[rows]
            for g in range(SGU_GROUPS):
                wg = jnp.where(tril, w_ref[g], 0.0).astype(_BF)
                mixed = jnp.where(gmask[g], mixed + _dot(wg, vhat[rows], 1, 0), mixed)
                dvh = jnp.where(gmask[g], _dot(wg, dmc.astype(_BF), 0, 0), dvh)
                dw_ref[g] += _dot(jnp.where(gmask[g], dmc, 0.0).astype(_BF), vhat[rows], 1, 1)
            db_ref[...] += dmc
            du_ref[rows, :] = (dov[rows] * mixed * _gelu_grad(uc[rows])).astype(du_ref.dtype)
            xhc = xh[rows]
            dgs_ref[...] += jnp.sum(dvh * xhc, axis=0, keepdims=True)
            dxh = dvh * gsv
            dvg = rstd[rows] * (dxh - xhc * _group_mean(dxh * xhc, gm))
            dv_ref[rows, :] = (dvg * _gelu_grad(vc[rows])).astype(dv_ref.dtype)

        @pl.when(i == nstep - 1)
        def _():
            for g in range(SGU_GROUPS):
                dw_ref[g] = jnp.where(tril, dw_ref[g], 0.0)

    wspec = pl.BlockSpec((SGU_GROUPS, SGU_CHUNK, SGU_CHUNK), lambda i: (0, 0, 0))
    bspec = pl.BlockSpec((SGU_CHUNK, 256), lambda i: (0, 0))
    return _pcall(
        body, name=name, grid=(nstep,),
        out_shape=[jax.ShapeDtypeStruct((s, 256), _BF), jax.ShapeDtypeStruct((s, 256), _BF),
                   jax.ShapeDtypeStruct((SGU_GROUPS, SGU_CHUNK, SGU_CHUNK), F32),
                   jax.ShapeDtypeStruct((SGU_CHUNK, 256), F32), jax.ShapeDtypeStruct((1, 256), F32)],
        in_specs=[_row_spec(t, 256, C_UC // 256), _row_spec(t, 256, C_VC // 256), wspec, _vec_spec(256), bspec,
                  _row_spec(t, 256, M_SGU // 256)],
        out_specs=[_row_spec(t, 256), _row_spec(t, 256), wspec, bspec, _vec_spec(256)],
        compiler_params=_cp(ARB),
    )(proj, proj, w, gs, bexp, dmix)


def _ada_fwd(c_all, ada_w, name):
    depth, d, n = ada_w.shape

    def body(c_ref, w_ref, o_ref):
        cv = c_ref[...]
        cond = cv / (1.0 + jnp.exp(-cv))
        o_ref[...] = _dot_f32(cond, w_ref[...], 1, 0)

    return _pcall(
        body, name=name, grid=(depth,), out_shape=jax.ShapeDtypeStruct((depth, N_DEV, n), F32),
        in_specs=[pl.BlockSpec((N_DEV, d), lambda l: (0, 0)), pl.BlockSpec((None, d, n), lambda l: (l, 0, 0))],
        out_specs=pl.BlockSpec((None, N_DEV, n), lambda l: (l, 0, 0)), compiler_params=_cp(PAR),
    )(c_all, ada_w)


def _ada_bwd(c_all, dmod, name):
    depth, _, n = dmod.shape
    d = c_all.shape[1]

    def body(c_ref, dm_ref, o_ref):
        cv = c_ref[...]
        cond = cv / (1.0 + jnp.exp(-cv))
        o_ref[...] = _dot_f32(cond, dm_ref[...], 0, 0)

    return _pcall(
        body, name=name, grid=(depth,), out_shape=jax.ShapeDtypeStruct((depth, d, n), F32),
        in_specs=[pl.BlockSpec((N_DEV, d), lambda l: (0, 0)), pl.BlockSpec((None, N_DEV, n), lambda l: (l, 0, 0))],
        out_specs=pl.BlockSpec((None, d, n), lambda l: (l, 0, 0)), compiler_params=_cp(PAR),
    )(c_all, dmod)


def _adamw(slots, w, m, v, name):
    n, r, c = slots.shape
    tr = 256 if r % 256 == 0 else r
    bc1 = 1.0 - ADAM_B1 ** ADAM_STEP
    bc2 = 1.0 - ADAM_B2 ** ADAM_STEP

    def body(s_ref, w_ref, m_ref, v_ref, g_ref, d_ref, nm_ref, nv_ref):
        g = s_ref[0].astype(F32)
        for j in range(1, n):
            g = g + s_ref[j].astype(F32)
        m_new = ADAM_B1 * m_ref[...] + (1.0 - ADAM_B1) * g
        v_new = ADAM_B2 * v_ref[...] + (1.0 - ADAM_B2) * (g * g)
        g_ref[...] = g
        nm_ref[...] = m_new
        nv_ref[...] = v_new
        d_ref[...] = -ADAM_LR * ((m_new / bc1) / (jnp.sqrt(v_new / bc2) + ADAM_EPS) + ADAM_WD * w_ref[...])

    tile = pl.BlockSpec((tr, c), lambda i: (i, 0))
    shp = jax.ShapeDtypeStruct((r, c), F32)
    return _pcall(
        body, name=name, grid=(r // tr,), out_shape=[shp] * 4,
        in_specs=[pl.BlockSpec((n, tr, c), lambda i: (0, i, 0)), tile, tile, tile],
        out_specs=[tile] * 4, compiler_params=_cp(PAR),
    )(slots, w, m, v)


def _w_in_layout(w):
    pad = jnp.zeros(w.shape[:-1] + (IN_P - IN_W,), w.dtype)
    return jnp.concatenate([w[..., 768:FL_SRC], w[..., :768], w[..., UC_SRC:], w[..., FL_SRC:UC_SRC], pad], axis=-1)


def _w_in_unlayout(g):
    return jnp.concatenate([g[..., C_QA:C_UC], g[..., :C_QA], g[..., C_FL:C_FL + FOX_HEADS], g[..., C_UC:C_FL]], axis=-1)


SMALL = [("ada_b", DEPTH * 6 * D_MODEL), ("norm1_g", DEPTH * D_MODEL), ("norm2_g", DEPTH * D_MODEL),
         ("sgu_w", DEPTH * SGU_GROUPS * SGU_CHUNK * SGU_CHUNK), ("sgu_b", DEPTH * SGU_GROUPS * SGU_CHUNK),
         ("sgu_norm_g", DEPTH * SGU_GROUPS * HEAD_DIM), ("q_norm_g", DEPTH * HEAD_DIM), ("k_norm_g", DEPTH * HEAD_DIM),
         ("b_forget", DEPTH * FOX_HEADS), ("loss", 1)]
SMALL_ROWS = 2560


def _pack_small(parts):
    flat = jnp.concatenate([parts[name].reshape(-1).astype(F32) for name, _ in SMALL])
    return jnp.pad(flat, (0, SMALL_ROWS * 128 - flat.shape[0])).reshape(SMALL_ROWS, 128)


def _unpack_small(packed, shapes):
    flat = packed.reshape(-1)
    out, off = {}, 0
    for name, size in SMALL:
        out[name] = flat[off:off + size].reshape(shapes[name])
        off += size
    return out


def kernel(x, c, ada_w, ada_b, norm1_g, norm2_g, w_in, b_forget, q_norm_g, k_norm_g, sgu_norm_g, sgu_w, sgu_b, w_out, mlp_w1, mlp_w2, loss_target, m_ada_w, m_ada_b, m_norm1_g, m_norm2_g, m_w_in, m_b_forget, m_q_norm_g, m_k_norm_g, m_sgu_norm_g, m_sgu_w, m_sgu_b, m_w_out, m_mlp_w1, m_mlp_w2, v_ada_w, v_ada_b, v_norm1_g, v_norm2_g, v_w_in, v_b_forget, v_q_norm_g, v_k_norm_g, v_sgu_norm_g, v_sgu_w, v_sgu_b, v_w_out, v_mlp_w1, v_mlp_w2):
    me = _lin(_my_pos())
    x0 = x[0]
    target = loss_target[0]
    n_ada = ada_w.shape[2]

    shards = [w.astype(_XBF) for w in (w_in, w_out, mlp_w1, mlp_w2)]

    def whole_in(w_in_g):
        return _w_in_layout(jnp.transpose(w_in_g, (1, 0, 2)).reshape(D_MODEL, IN_W))

    def whole_rest(w_out_g, w1_g, w2_g):
        return (w_out_g.reshape(D_MODEL, D_MODEL), jnp.transpose(w1_g, (1, 0, 2)).reshape(D_MODEL, D_FF),
                w2_g.reshape(D_FF, D_MODEL))

    w_in_first, c_all = _exchange([shards[0][0], c], False, "gather_weights")
    c_all = c_all.reshape(N_DEV, D_MODEL)
    weights = [None] * DEPTH

    mod_part = _ada_fwd(c_all, ada_w, "ada_fwd")
    (mod_rows,) = _exchange([jnp.transpose(mod_part, (1, 0, 2))], True, "scatter_mod")
    mod = jnp.transpose(mod_rows, (1, 0, 2)).reshape(DEPTH, 6 * D_MODEL) + ada_b
    mods = mod.reshape(DEPTH, 6, 1, D_MODEL)

    saved = []
    xl = x0
    for l in range(DEPTH):
        sh1, sc1, g1, sh2, sc2, g2 = (mods[l, r] for r in range(6))
        n1 = norm1_g[l].reshape(1, D_MODEL)
        n2 = norm2_g[l].reshape(1, D_MODEL)
        gq = jnp.tile(q_norm_g[l], FOX_HEADS).reshape(1, 512)
        gk = jnp.tile(k_norm_g[l], FOX_HEADS).reshape(1, 512)
        bf = jnp.pad(b_forget[l], (0, 128 - FOX_HEADS)).reshape(1, 128)
        gs = sgu_norm_g[l].reshape(1, 256)
        bexp = jnp.repeat(sgu_b[l].T, HEAD_DIM, axis=1)

        if l == 0:
            w_in_l = whole_in(w_in_first)
            gather_rest = dict(arrays=[w[0] for w in shards[1:]], scatter=False, recv=None, slot=None)
        else:
            w_in_l, gather_rest = weights[l][0], None
        h1, h1_t = _norm_mod(xl, n1, sc1, sh1, "norm_mod")
        proj = _mm_nn(h1, w_in_l, "plain", (), "mm_in")
        o_sb, gathered = _sb_fwd(proj, "sb_fwd", gather_rest)
        if gathered:
            weights[0] = (w_in_l,) + whole_rest(*gathered)
        qn, kn, cfc, cfr, cfe = _fox_prep(proj, gq, gk, bf, "fox_prep")
        qkb = (1.02 * HEAD_DIM ** 0.5 * jnp.max(jnp.abs(q_norm_g[l])) * jnp.max(jnp.abs(k_norm_g[l]))).reshape(1, 1)
        gather_next = None
        if l + 1 < DEPTH:
            gather_next = dict(arrays=[w[l + 1] for w in shards], scatter=False, recv=None, slot=None)
        o_fox, lse, gathered = _fox_fwd(proj, qn, kn, cfc, cfr, cfe, qkb, "fox_fwd", gather_next)
        if gathered:
            weights[l + 1] = (whole_in(gathered[0]),) + whole_rest(*gathered[1:])
        _, w_out_l, w1_l, w2_l = weights[l]
        o_sgu = _sgu_fwd(proj, sgu_w[l], gs, bexp, "sgu_fwd")
        x_mid, y1, mixed_t = _mm_out((o_sb, o_fox, o_sgu), w_out_l, xl, g1, "mm_out")
        h2, h2_t = _norm_mod(x_mid, n2, sc2, sh2, "norm_mod")
        r2, r2_t = _mm_nn(h2, w1_l, "relu2", (), "mm_w1")
        x_out, y2 = _mm_nn(r2, w2_l, "resid", (x_mid, g2), "mm_w2")
        saved.append(dict(x_in=xl, h1_t=h1_t, proj=proj, o_sb=o_sb, qn=qn, kn=kn, cfc=cfc, cfr=cfr, cfe=cfe, o_fox=o_fox,
                          lse=lse, mixed_t=mixed_t, x_mid=x_mid, y1=y1, h2_t=h2_t, r2=r2, r2_t=r2_t, y2=y2,
                          n1=n1, n2=n2, gq=gq, gk=gk, bf=bf, gs=gs, bexp=bexp, qkb=qkb))
        xl = x_out

    loss_part, dx, dy2, dg2 = _loss_head(xl, target, (saved[-1]["y2"], mods[DEPTH - 1, 5]), "loss_head")

    grads_ready = None
    received = [jnp.zeros((N_DEV,) + w.shape, _XBF) for w in shards]
    small = {k: [None] * DEPTH for k in ("mod", "norm1_g", "norm2_g", "sgu_w", "sgu_b", "sgu_norm_g",
                                         "q_norm_g", "k_norm_g", "b_forget")}
    for l in reversed(range(DEPTH)):
        sv = saved[l]
        sh1, sc1, g1, sh2, sc2, g2 = (mods[l, r] for r in range(6))
        w_in_l, w_out_l, w1_l, w2_l = weights[l]
        da = _mm_nt(dy2, w2_l, "mul2", (sv["r2"],), _BF, "mm_w2_bwd")
        g_w2 = _mm_grad(sv["r2_t"], dy2, "rows", "mm_w2_grad")
        dh2 = _mm_nt(da, w1_l, "plain", (), F32, "mm_w1_bwd")
        g_w1 = _mm_grad(sv["h2_t"], da, "cols", "mm_w1_grad")
        dx_mid, dsh2, dsc2, dn2, dy1, dg1 = _norm_mod_bwd(sv["x_mid"], dh2, dx, sv["n2"], sc2, (sv["y1"], g1),
                                                          "norm_mod_bwd")
        dmix = _mm_nt(dy1, w_out_l, "plain", (), F32, "mm_out_bwd")
        g_out = _mm_grad(sv["mixed_t"], dy1, "rows", "mm_out_grad")
        send_own = None
        if l == 0:
            send_own = dict(arrays=[g_out, g_w1, g_w2], scatter=True, recv=received[1:], slot=0)
        dq_sb, dk_sb, dv_sb, filled = _sb_bwd(sv["proj"], sv["o_sb"], dmix, "sb_bwd", send_own)
        if filled:
            received = received[:1] + filled
        send_prev = None
        if grads_ready is not None:
            send_prev = dict(arrays=grads_ready, scatter=True, recv=received, slot=l + 1)
        dqn, dkn, dv_fox, dlogf, filled = _fox_bwd(sv["proj"], sv["qn"], sv["kn"], sv["cfc"], sv["cfr"], sv["cfe"], sv["qkb"],
                                                 sv["o_fox"], sv["lse"], dmix, "fox_bwd", send_prev)
        if filled:
            received = filled
        dq_fox, dk_fox, dfl, dgq, dgk, dbf = _fox_prep_bwd(sv["proj"], sv["gq"], sv["gk"], sv["bf"], dqn, dkn, dlogf,
                                                           "fox_prep_bwd")
        duc, dvc, dsw, dsb, dsg = _sgu_bwd(sv["proj"], sgu_w[l], sv["gs"], sv["bexp"], dmix, "sgu_bwd")
        dproj = jnp.concatenate(
            [dq_fox, dk_fox, dv_fox.astype(_BF), dq_sb.astype(_BF), dk_sb.astype(_BF), dv_sb.astype(_BF), duc, dvc, dfl,
             jnp.zeros((dfl.shape[0], IN_P - C_FL - 128), _BF)], axis=-1)
        dh1 = _mm_nt(dproj, w_in_l, "plain", (), F32, "mm_in_bwd")
        g_in = _w_in_unlayout(_mm_grad(sv["h1_t"], dproj, None, "mm_in_grad"))
        g_in = jnp.transpose(g_in.reshape(D_MODEL, N_DEV, IN_W // N_DEV), (1, 0, 2))
        grads_ready = [g_in, g_out, g_w1, g_w2]
        small["mod"][l] = [dg1, dsh2, dsc2, dg2]
        below = (saved[l - 1]["y2"], mods[l - 1, 5]) if l > 0 else None
        dx, dsh1, dsc1, dn1, *below_grads = _norm_mod_bwd(sv["x_in"], dh1, dx_mid, sv["n1"], sc1, below, "norm_mod_bwd")
        if below_grads:
            dy2, dg2 = below_grads
        small["mod"][l] = jnp.concatenate([dsh1, dsc1] + small["mod"][l], axis=-1).reshape(-1)
        small["norm1_g"][l] = dn1.reshape(-1)
        small["norm2_g"][l] = dn2.reshape(-1)
        small["sgu_w"][l] = dsw
        small["sgu_b"][l] = dsb.reshape(SGU_CHUNK, SGU_GROUPS, HEAD_DIM).sum(-1).T
        small["sgu_norm_g"][l] = dsg.reshape(SGU_GROUPS, HEAD_DIM)
        small["q_norm_g"][l] = dgq.reshape(FOX_HEADS, HEAD_DIM).sum(0)
        small["k_norm_g"][l] = dgk.reshape(FOX_HEADS, HEAD_DIM).sum(0)
        small["b_forget"][l] = dbf[0, :FOX_HEADS]

    parts = {k: jnp.stack(v) for k, v in small.items()}
    parts["ada_b"] = parts.pop("mod")
    parts["loss"] = loss_part
    (small_all,) = _exchange([_pack_small(parts)], False, "gather_small")
    zero1 = jnp.zeros((1,), F32)
    packs = [_pack_small(dict(ada_b=a, norm1_g=b, norm2_g=cc, sgu_w=d, sgu_b=e, sgu_norm_g=f, q_norm_g=g, k_norm_g=h,
                              b_forget=i, loss=zero1))
             for a, b, cc, d, e, f, g, h, i in (
                 (ada_b, norm1_g, norm2_g, sgu_w, sgu_b, sgu_norm_g, q_norm_g, k_norm_g, b_forget),
                 (m_ada_b, m_norm1_g, m_norm2_g, m_sgu_w, m_sgu_b, m_sgu_norm_g, m_q_norm_g, m_k_norm_g, m_b_forget),
                 (v_ada_b, v_norm1_g, v_norm2_g, v_sgu_w, v_sgu_b, v_sgu_norm_g, v_q_norm_g, v_k_norm_g, v_b_forget))]
    small_shapes = dict(ada_b=ada_b.shape, norm1_g=norm1_g.shape, norm2_g=norm2_g.shape, sgu_w=sgu_w.shape,
                        sgu_b=sgu_b.shape, sgu_norm_g=sgu_norm_g.shape, q_norm_g=q_norm_g.shape,
                        k_norm_g=k_norm_g.shape, b_forget=b_forget.shape, loss=())
    s_out = [_unpack_small(a, small_shapes) for a in _adamw(small_all, *packs, "adamw_small")]

    dmod_all = small_all[:, :DEPTH * 6 * D_MODEL // 128, :].reshape(N_DEV, DEPTH, 6 * D_MODEL)
    dmod_mine = lax.dynamic_slice_in_dim(dmod_all, me * n_ada, n_ada, axis=2)
    g_ada = _ada_bwd(c_all, jnp.transpose(dmod_mine, (1, 0, 2)), "ada_bwd")
    ada_out = _adamw(g_ada.reshape(1, DEPTH * D_MODEL, n_ada), ada_w.reshape(-1, n_ada), m_ada_w.reshape(-1, n_ada),
                     v_ada_w.reshape(-1, n_ada), "adamw_ada")
    ada_out = [a.reshape(ada_w.shape) for a in ada_out]

    (ri,) = _exchange(grads_ready[:1], True, "scatter_grads", recv=received[:1], slot=0)
    _, ro, r1, r2 = received

    def big(slots, w, m, v, name):
        cdim = w.shape[-1]
        outs = _adamw(slots.reshape(N_DEV, -1, cdim), w.reshape(-1, cdim), m.reshape(-1, cdim), v.reshape(-1, cdim), name)
        return [a.reshape(w.shape) for a in outs]

    in_out = big(ri, w_in, m_w_in, v_w_in, "adamw_in")
    out_out = big(ro, w_out, m_w_out, v_w_out, "adamw_out")
    w1_out = big(r1, mlp_w1, m_mlp_w1, v_mlp_w1, "adamw_w1")
    w2_out = big(r2, mlp_w2, m_mlp_w2, v_mlp_w2, "adamw_w2")

    def leaves(idx):
        sm = s_out[idx]
        return [ada_out[idx], sm["ada_b"], sm["norm1_g"], sm["norm2_g"], in_out[idx], sm["b_forget"], sm["q_norm_g"],
                sm["k_norm_g"], sm["sgu_norm_g"], sm["sgu_w"], sm["sgu_b"], out_out[idx], w1_out[idx], w2_out[idx]]

    loss = s_out[0]["loss"]
    grad_x = dx.reshape(x.shape)
    return (loss, grad_x, *leaves(0), *leaves(1), *leaves(2), *leaves(3))
```

```python
import jax
import jax.numpy as jnp
from jax import lax
from jax.experimental import pallas as pl
from jax.experimental.pallas import tpu as pltpu

F32 = jnp.float32
_BF = jnp.bfloat16
_XBF = jnp.bfloat16

N_DEV = 8
D_MODEL = 1024
DEPTH = 4
HEAD_DIM = 64
SB_HEADS = 4
FOX_HEADS = 8
SGU_GROUPS = 4
SGU_CHUNK = 128
D_FF = 4096
EPS = 1e-6
IN_W = 2824
FL_SRC = 2304
UC_SRC = 2312

C_QB, C_KB, C_VB = 0, 512, 1024
C_QA, C_KA, C_VA = 1536, 1792, 2048
C_UC, C_VC = 2304, 2560
C_FL = 2816
IN_P = 3072
M_SB, M_FOX, M_SGU = 0, 256, 768

ADAM_LR = 0.001
ADAM_B1 = 0.9
ADAM_B2 = 0.999
ADAM_EPS = 1e-08
ADAM_WD = 0.01
ADAM_STEP = 10

ATT_T = 256
ROW_T = 512
MM_T = 1024
GRAD_TS = 4096
ROW_CHUNK = 64
NEG = -1e30
UNDERFLOW = -90.0
NO_BLOCK = 1e30

MESH = pl.DeviceIdType.MESH
ARB = pltpu.ARBITRARY
PAR = pltpu.PARALLEL
HIGHEST = lax.Precision.HIGHEST


def _pcall(body, **kw):
    return pl.pallas_call(body, **kw)


def _cp(*sem):
    return pltpu.CompilerParams(dimension_semantics=tuple(sem))


def _dot(a, b, ca, cb):
    return lax.dot_general(a, b, (((ca,), (cb,)), ((), ())), preferred_element_type=F32)


def _dot_f32(a, b, ca, cb):
    return lax.dot_general(a, b, (((ca,), (cb,)), ((), ())), precision=HIGHEST, preferred_element_type=F32)


def _split(v, parts):
    out = []
    r = v
    for _ in range(parts - 1):
        p = r.astype(_BF)
        out.append(p)
        r = r - p.astype(F32)
    out.append(r.astype(_BF))
    return out


def _dot_exact(v, m, ca, cb, parts, v_left=True):
    out = None
    for p in _split(v, parts):
        term = _dot(p, m, ca, cb) if v_left else _dot(m, p, ca, cb)
        out = term if out is None else out + term
    return out


def _iota2(shape, dim):
    return lax.broadcasted_iota(jnp.int32, shape, dim)


def _my_pos():
    return lax.axis_index("x"), lax.axis_index("y"), lax.axis_index("c")


def _flip(pos, k):
    x, y, c = pos
    px = 1 - x if (k >> 2) & 1 else x
    py = 1 - y if (k >> 1) & 1 else y
    pc = 1 - c if k & 1 else c
    return px, py, pc


def _lin(pos):
    return 4 * pos[0] + 2 * pos[1] + pos[2]


def _exchange_ops(ins, outs, sems, scatter, slot):
    send_sems, recv_sems, local_sems = sems
    n = len(ins)

    def copies(with_recvs):
        me = _my_pos()
        me_i = _lin(me)
        dst = lambda a, j: outs[a].at[j] if slot is None else outs[a].at[j, slot]
        mine = lambda a: ins[a].at[me_i] if scatter else ins[a]
        local = [pltpu.make_async_copy(mine(a), dst(a, me_i), local_sems.at[a]) for a in range(n)]
        sends, recvs = [], []
        for a in range(n):
            for k in range(1, N_DEV):
                peer = _flip(me, k)
                pair = dict(send_sem=send_sems.at[a, k - 1], recv_sem=recv_sems.at[a, k - 1],
                            device_id=peer, device_id_type=MESH)
                src = ins[a].at[_lin(peer)] if scatter else ins[a]
                sends.append(pltpu.make_async_remote_copy(src_ref=src, dst_ref=dst(a, me_i), **pair))
                if with_recvs:
                    recvs.append(pltpu.make_async_remote_copy(src_ref=mine(a), dst_ref=dst(a, _lin(peer)), **pair))
        return local, sends, recvs

    def start():
        local, sends, _ = copies(False)
        for cp in local + sends:
            cp.start()

    def wait():
        local, sends, recvs = copies(True)
        for snd, rcv in zip(sends, recvs):
            snd.wait_send()
            rcv.wait_recv()
        for cp in local:
            cp.wait()

    return start, wait


def _comm_operands(comm):
    arrays, recv = comm["arrays"], comm["recv"]
    n = len(arrays)
    any_spec = pl.BlockSpec(memory_space=pl.ANY)
    if recv is not None:
        out_shapes = [jax.ShapeDtypeStruct(r.shape, r.dtype) for r in recv]
    elif comm["scatter"]:
        out_shapes = [jax.ShapeDtypeStruct(a.shape, a.dtype) for a in arrays]
    else:
        out_shapes = [jax.ShapeDtypeStruct((N_DEV,) + a.shape, a.dtype) for a in arrays]
    operands = list(arrays) + (list(recv) if recv is not None else [])
    sems = [pltpu.SemaphoreType.DMA((n, N_DEV - 1)), pltpu.SemaphoreType.DMA((n, N_DEV - 1)),
            pltpu.SemaphoreType.DMA((n,))]
    return operands, [any_spec] * len(operands), out_shapes, [any_spec] * n, sems


def _with_comm(body, comm, n_in, n_out, first_step, last_step, n_scratch=0):
    if comm is None:
        return body
    n = len(comm["arrays"])
    n_cin = n if comm["recv"] is None else 2 * n

    def wrapped(*refs):
        ins, cins = refs[:n_in], refs[n_in:n_in + n]
        o0 = n_in + n_cin
        outs, couts = refs[o0:o0 + n_out], refs[o0 + n_out:o0 + n_out + n]
        s0 = o0 + n_out + n
        scratch, sems = refs[s0:s0 + n_scratch], refs[s0 + n_scratch:]
        start, wait = _exchange_ops(cins, couts, sems, comm["scatter"], comm["slot"])
        pl.when(first_step())(start)
        body(*ins, *outs, *scratch)
        pl.when(last_step())(wait)

    return wrapped


def _comm_aliases(comm, n_in, n_out):
    if comm is None or comm["recv"] is None:
        return {}
    n = len(comm["arrays"])
    return {n_in + n + a: n_out + a for a in range(n)}


def _exchange(arrays, scatter, name, recv=None, slot=None):
    comm = dict(arrays=arrays, scatter=scatter, recv=recv, slot=slot)
    operands, in_specs, out_shapes, out_specs, sems = _comm_operands(comm)
    n = len(arrays)

    def body(*refs):
        n_cin = len(operands)
        start, wait = _exchange_ops(refs[:n], refs[n_cin:n_cin + n], refs[n_cin + n:], scatter, slot)
        start()
        wait()

    outs = _pcall(body, name=name, out_shape=out_shapes, in_specs=in_specs, out_specs=out_specs, scratch_shapes=sems,
                  input_output_aliases=_comm_aliases(comm, 0, 0))(*operands)
    return list(outs)


def _row_spec(t, w, col=0):
    return pl.BlockSpec((t, w), lambda i: (i, col))


def _vec_spec(w, col=0):
    return pl.BlockSpec((1, w), lambda i: (0, col))


def _norm_mod(x, g, sc, sh, name):
    s, d = x.shape
    t = min(ROW_T, s)

    def body(x_ref, g_ref, sc_ref, sh_ref, h_ref, ht_ref):
        xf = x_ref[...]
        rstd = lax.rsqrt(jnp.mean(xf * xf, axis=-1, keepdims=True) + EPS)
        y = xf * rstd * g_ref[...]
        h = y * (1.0 + sc_ref[...]) + sh_ref[...]
        h_ref[...] = h.astype(h_ref.dtype)
        ht_ref[...] = h.T.astype(ht_ref.dtype)

    return _pcall(
        body, name=name, grid=(s // t,),
        out_shape=[jax.ShapeDtypeStruct((s, d), _BF), jax.ShapeDtypeStruct((d, s), _BF)],
        in_specs=[_row_spec(t, d), _vec_spec(d), _vec_spec(d), _vec_spec(d)],
        out_specs=[_row_spec(t, d), pl.BlockSpec((d, t), lambda i: (0, i))], compiler_params=_cp(PAR),
    )(x, g, sc, sh)


def _res_bwd_step(i, dx, y_ref, gate_ref, dy_ref, dgate_ref):
    dy_ref[...] = (dx * gate_ref[...]).astype(dy_ref.dtype)

    @pl.when(i == 0)
    def _():
        dgate_ref[...] = jnp.zeros_like(dgate_ref)

    dgate_ref[...] += jnp.sum(dx * y_ref[...], axis=0, keepdims=True)


def _norm_mod_bwd(x, dh, dres, g, sc, branch, name):
    s, d = x.shape
    t = min(ROW_T, s)
    n_branch = 0 if branch is None else 2

    ch = min(ROW_CHUNK, t)

    def body(x_ref, dh_ref, dres_ref, g_ref, sc_ref, *rest):
        dx_ref, dsh_ref, dsc_ref, dg_ref = rest[n_branch:n_branch + 4]
        i = pl.program_id(0)
        gv = g_ref[...]
        sc1 = 1.0 + sc_ref[...]
        gain = gv * sc1
        n_sum = 2 if branch is None else 3

        def rows(r, sums):
            sl = pl.ds(pl.multiple_of(r * ch, ch), ch)
            xf = x_ref[sl, :]
            dh = dh_ref[sl, :]
            rstd = lax.rsqrt(jnp.mean(xf * xf, axis=-1, keepdims=True) + EPS)
            xhat = xf * rstd
            dh_xhat = dh * xhat
            dxh = dh * gain
            dx = dres_ref[sl, :] + rstd * (dxh - xhat * jnp.mean(dh_xhat * gain, axis=-1, keepdims=True))
            dx_ref[sl, :] = dx
            new = [sums[0] + jnp.sum(dh, axis=0, keepdims=True), sums[1] + jnp.sum(dh_xhat, axis=0, keepdims=True)]
            if branch is not None:
                rest[n_branch + 4][sl, :] = (dx * rest[1][...]).astype(_BF)
                new.append(sums[2] + jnp.sum(dx * rest[0][sl, :], axis=0, keepdims=True))
            return tuple(new)

        sums = lax.fori_loop(0, t // ch, rows, tuple(jnp.zeros((1, d), F32) for _ in range(n_sum)))
        parts = [sums[0], sums[1] * gv, sums[1] * sc1] + list(sums[2:])
        totals = [dsh_ref, dsc_ref, dg_ref] + ([rest[n_branch + 5]] if branch is not None else [])

        @pl.when(i == 0)
        def _():
            for ref in totals:
                ref[...] = jnp.zeros_like(ref)

        for ref, part in zip(totals, parts):
            ref[...] += part

    vec = jax.ShapeDtypeStruct((1, d), F32)
    out_shape = [jax.ShapeDtypeStruct((s, d), F32), vec, vec, vec]
    in_specs = [_row_spec(t, d), _row_spec(t, d), _row_spec(t, d), _vec_spec(d), _vec_spec(d)]
    out_specs = [_row_spec(t, d), _vec_spec(d), _vec_spec(d), _vec_spec(d)]
    if branch is not None:
        in_specs += [_row_spec(t, d), _vec_spec(d)]
        out_shape += [jax.ShapeDtypeStruct((s, d), _BF), vec]
        out_specs += [_row_spec(t, d), _vec_spec(d)]
    return _pcall(
        body, name=name, grid=(s // t,), out_shape=out_shape, in_specs=in_specs, out_specs=out_specs,
        compiler_params=_cp(ARB),
    )(x, dh, dres, g, sc, *(branch or ()))


def _loss_head(y, target, branch, name):
    s, d = y.shape
    t = min(ROW_T, s)

    def body(y_ref, t_ref, yb_ref, gate_ref, loss_ref, dy_ref, dyb_ref, dgate_ref):
        i = pl.program_id(0)
        diff = y_ref[...] - t_ref[...]
        dy = diff * (1.0 / d)
        dy_ref[...] = dy

        @pl.when(i == 0)
        def _():
            loss_ref[...] = jnp.zeros_like(loss_ref)

        rows = jnp.sum(diff * diff, axis=-1, keepdims=True)
        loss_ref[...] += (0.5 / d) * jnp.sum(rows, axis=0, keepdims=True)
        _res_bwd_step(i, dy, yb_ref, gate_ref, dyb_ref, dgate_ref)

    return _pcall(
        body, name=name, grid=(s // t,),
        out_shape=[jax.ShapeDtypeStruct((1, 1), F32), jax.ShapeDtypeStruct((s, d), F32),
                   jax.ShapeDtypeStruct((s, d), _BF), jax.ShapeDtypeStruct((1, d), F32)],
        in_specs=[_row_spec(t, d), _row_spec(t, d), _row_spec(t, d), _vec_spec(d)],
        out_specs=[pl.BlockSpec((1, 1), lambda i: (0, 0)), _row_spec(t, d), _row_spec(t, d), _vec_spec(d)],
        compiler_params=_cp(ARB),
    )(y, target, *branch)


def _contract_tile(kdim):
    if kdim > MM_T and (kdim // 2) % 128 == 0:
        return kdim // 2
    return min(MM_T, kdim)


def _mm_nn(a, b, epi, extras, name):
    m, kdim = a.shape
    n = b.shape[1]
    tm, tn, tk = min(MM_T, m), min(MM_T, n), _contract_tile(kdim)
    nk = kdim // tk
    n_extra = len(extras)
    n_out = {"plain": 1, "resid": 2, "relu2": 2}[epi]

    def finish(y, extra_refs, out_refs):
        if epi == "plain":
            out_refs[0][...] = y.astype(out_refs[0].dtype)
        elif epi == "resid":
            x_ref, g_ref = extra_refs
            out_refs[0][...] = x_ref[...] + g_ref[...] * y
            out_refs[1][...] = y.astype(out_refs[1].dtype)
        else:
            r = jnp.maximum(y, 0.0)
            out_refs[0][...] = (r * r).astype(out_refs[0].dtype)
            out_refs[1][...] = (r * r).T.astype(out_refs[1].dtype)

    def body(a_ref, b_ref, *rest):
        extra_refs = rest[:n_extra]
        out_refs = rest[n_extra:n_extra + n_out]
        part = _dot(a_ref[...].astype(_BF), b_ref[...].astype(_BF), 1, 0)
        if nk == 1:
            finish(part, extra_refs, out_refs)
        else:
            acc_ref = rest[-1]
            k = pl.program_id(2)

            @pl.when(k == 0)
            def _():
                acc_ref[...] = part

            @pl.when(k > 0)
            def _():
                acc_ref[...] += part

            @pl.when(k == nk - 1)
            def _():
                finish(acc_ref[...], extra_refs, out_refs)

    tile = pl.BlockSpec((tm, tn), lambda i, j, k: (i, j))
    in_specs = [pl.BlockSpec((tm, tk), lambda i, j, k: (i, k)), pl.BlockSpec((tk, tn), lambda i, j, k: (k, j))]
    if epi == "plain":
        out_shape = [jax.ShapeDtypeStruct((m, n), F32)]
    elif epi == "resid":
        in_specs += [tile, pl.BlockSpec((1, tn), lambda i, j, k: (0, j))]
        out_shape = [jax.ShapeDtypeStruct((m, n), F32), jax.ShapeDtypeStruct((m, n), _BF)]
    out_specs = [tile] * n_out
    if epi == "relu2":
        out_shape = [jax.ShapeDtypeStruct((m, n), _BF), jax.ShapeDtypeStruct((n, m), _BF)]
        out_specs[1] = pl.BlockSpec((tn, tm), lambda i, j, k: (j, i))
    outs = _pcall(
        body, name=name, grid=(m // tm, n // tn, nk), out_shape=out_shape,
        in_specs=in_specs, out_specs=out_specs,
        scratch_shapes=[pltpu.VMEM((tm, tn), F32)] if nk > 1 else [],
        compiler_params=_cp(PAR, PAR, ARB),
    )(a, b, *extras)
    return outs[0] if n_out == 1 else outs


def _mm_out(parts, w, x, gate, name):
    s, n = x.shape
    widths = [p.shape[1] for p in parts]
    kdim = sum(widths)
    tm = min(ROW_T, s)

    def body(a0_ref, a1_ref, a2_ref, w_ref, x_ref, g_ref, xo_ref, y_ref, mt_ref):
        mixed = jnp.concatenate([a0_ref[...], a1_ref[...], a2_ref[...]], axis=-1)
        y = _dot(mixed.astype(_BF), w_ref[...], 1, 0)
        xo_ref[...] = x_ref[...] + g_ref[...] * y
        y_ref[...] = y.astype(y_ref.dtype)
        mt_ref[...] = mixed.T.astype(mt_ref.dtype)

    row = lambda width: pl.BlockSpec((tm, width), lambda i: (i, 0))
    return _pcall(
        body, name=name, grid=(s // tm,),
        out_shape=[jax.ShapeDtypeStruct((s, n), F32), jax.ShapeDtypeStruct((s, n), _BF),
                   jax.ShapeDtypeStruct((kdim, s), _BF)],
        in_specs=[row(widths[0]), row(widths[1]), row(widths[2]), pl.BlockSpec((kdim, n), lambda i: (0, 0)),
                  row(n), pl.BlockSpec((1, n), lambda i: (0, 0))],
        out_specs=[row(n), row(n), pl.BlockSpec((kdim, tm), lambda i: (0, i))], compiler_params=_cp(PAR),
    )(*parts, w, x, gate)


def _mm_nt(a, b, epi, extras, out_dtype, name):
    m, kdim = a.shape
    n = b.shape[0]
    tm, tn, tk = min(MM_T, m), min(MM_T, n), _contract_tile(kdim)
    nk = kdim // tk
    n_extra = len(extras)

    def finish(y, extra_refs, o_ref):
        if epi == "mul2":
            y = y * (2.0 * jnp.sqrt(extra_refs[0][...].astype(F32)))
        o_ref[...] = y.astype(o_ref.dtype)

    def body(a_ref, b_ref, *rest):
        extra_refs = rest[:n_extra]
        o_ref = rest[n_extra]
        part = _dot(a_ref[...].astype(_BF), b_ref[...].astype(_BF), 1, 1)
        if nk == 1:
            finish(part, extra_refs, o_ref)
        else:
            acc_ref = rest[-1]
            k = pl.program_id(2)

            @pl.when(k == 0)
            def _():
                acc_ref[...] = part

            @pl.when(k > 0)
            def _():
                acc_ref[...] += part

            @pl.when(k == nk - 1)
            def _():
                finish(acc_ref[...], extra_refs, o_ref)

    tile = pl.BlockSpec((tm, tn), lambda i, j, k: (i, j))
    in_specs = [pl.BlockSpec((tm, tk), lambda i, j, k: (i, k)), pl.BlockSpec((tn, tk), lambda i, j, k: (j, k))]
    in_specs += [tile] * n_extra
    return _pcall(
        body, name=name, grid=(m // tm, n // tn, nk), out_shape=jax.ShapeDtypeStruct((m, n), out_dtype),
        in_specs=in_specs, out_specs=tile,
        scratch_shapes=[pltpu.VMEM((tm, tn), F32)] if nk > 1 else [],
        compiler_params=_cp(PAR, PAR, ARB),
    )(a, b, *extras)


def _mm_grad(at, b, split, name):
    m, s = at.shape
    n = b.shape[1]
    ts = min(GRAD_TS, s)
    ns = s // ts
    if split == "rows":
        tm, tn = min(MM_T, m // N_DEV), min(MM_T, n)
        per = (m // N_DEV) // tm
        out_shape = (N_DEV, m // N_DEV, n)
        out_spec = pl.BlockSpec((None, tm, tn), lambda i, j, k: (i // per, i % per, j))
    elif split == "cols":
        tm, tn = min(MM_T, m), min(MM_T, n // N_DEV)
        per = (n // N_DEV) // tn
        out_shape = (N_DEV, m, n // N_DEV)
        out_spec = pl.BlockSpec((None, tm, tn), lambda i, j, k: (j // per, i, j % per))
    else:
        tm, tn = min(MM_T, m), min(MM_T, n)
        out_shape = (m, n)
        out_spec = pl.BlockSpec((tm, tn), lambda i, j, k: (i, j))

    def body(a_ref, b_ref, o_ref, acc_ref):
        k = pl.program_id(2)
        part = _dot(a_ref[...].astype(_BF), b_ref[...].astype(_BF), 1, 0)

        @pl.when(k == 0)
        def _():
            acc_ref[...] = part

        @pl.when(k > 0)
        def _():
            acc_ref[...] += part

        @pl.when(k == ns - 1)
        def _():
            o_ref[...] = acc_ref[...].astype(o_ref.dtype)

    return _pcall(
        body, name=name, grid=(m // tm, n // tn, ns), out_shape=jax.ShapeDtypeStruct(out_shape, _XBF),
        in_specs=[pl.BlockSpec((tm, ts), lambda i, j, k: (i, k)), pl.BlockSpec((ts, tn), lambda i, j, k: (k, j))],
        out_specs=out_spec, scratch_shapes=[pltpu.VMEM((tm, tn), F32)],
        compiler_params=_cp(PAR, PAR, ARB),
    )(at, b)


def _head_masks():
    lane = _iota2((1, 128), 1)
    return [(lane // HEAD_DIM) == hh for hh in range(2)]


def _untranspose(src_ref, dst_ref, t, nblk):
    def rows(c, _):
        off = pl.multiple_of(c * t, t)
        dst_ref[pl.ds(off, t), :] = src_ref[:, pl.ds(off, t)].T
        return 0

    lax.fori_loop(0, nblk, rows, 0)


def _softplus_parts(z):
    sp = jnp.maximum(z, 0.0) + jnp.log(1.0 + jnp.exp(-jnp.abs(z)))
    return -sp, z - sp


def _sb_key_loop(i, carry, step, lookahead, first_ahead):
    def cond(c):
        return jnp.logical_and(c[0] <= i, jnp.max(jnp.maximum(c[1][0][0], c[1][0][1])) > UNDERFLOW)

    def body(c):
        ahead = lookahead(i - c[0] - 1)
        return c[0] + 1, step(i - c[0], c[1], c[2]), ahead

    return lax.while_loop(cond, body, (jnp.int32(1), carry, first_ahead))[1]


def _sb_fwd(proj, name, comm=None):
    s = proj.shape[0]
    t = min(ATT_T, s)
    nblk = s // t
    scale = HEAD_DIM ** -0.5

    def body(q_ref, k_ref, v_ref, o_ref):
        i = pl.program_id(1)
        masks = _head_masks()
        row = _iota2((t, t), 0)
        col = _iota2((t, t), 1)
        u_strict = (row > col).astype(_BF)
        causal = col < row
        qs = q_ref[...] * scale
        qm = [jnp.where(masks[hh], qs, 0.0).astype(_BF) for hh in range(2)]

        def scores(j):
            kb = k_ref[pl.ds(pl.multiple_of(jnp.maximum(j, 0) * t, t), t), :].astype(_BF)
            return tuple(_dot(qm[hh], kb, 1, 1) for hh in range(2))

        def block(j, carry, qk, diag):
            off = pl.multiple_of(j * t, t)
            vb = v_ref[pl.ds(off, t), :].astype(_BF)
            runs, outs = [], []
            for hh in range(2):
                run, o = carry[0][hh], carry[1][hh]
                z = qk[hh]
                l1, lb = _softplus_parts(z)
                if diag:
                    l1 = jnp.where(causal, l1, 0.0)
                between = run + _dot_exact(l1, u_strict, 1, 0, 2)
                a = jnp.exp(lb + between)
                if diag:
                    a = jnp.where(causal, a, 0.0)
                outs.append(o + _dot_exact(a, vb, 1, 0, 2))
                runs.append(run + jnp.sum(l1, axis=-1, keepdims=True))
            return tuple(runs), tuple(outs)

        zero, zero_o = jnp.zeros((t, 1), F32), jnp.zeros((t, 128), F32)
        ahead = scores(i - 1)
        carry = block(i, ((zero, zero), (zero_o, zero_o)), scores(i), True)
        carry = _sb_key_loop(i, carry, lambda j, c, qk: block(j, c, qk, False), scores, ahead)
        o_ref[...] = jnp.where(masks[0], carry[1][0], carry[1][1])

    kv_spec = lambda c0: pl.BlockSpec((s, 128), lambda p, i: (0, c0 // 128 + p))
    c_operands, c_in_specs, c_out_shapes, c_out_specs, c_sems = _comm_operands(comm) if comm else ([], [], [], [], [])
    outs = _pcall(
        _with_comm(body, comm, 3, 1, *_grid_ends(SB_HEADS // 2, nblk)), name=name, grid=(SB_HEADS // 2, nblk),
        out_shape=[jax.ShapeDtypeStruct((s, 256), F32)] + c_out_shapes,
        in_specs=[pl.BlockSpec((t, 128), lambda p, i: (i, C_QA // 128 + p)), kv_spec(C_KA), kv_spec(C_VA)] + c_in_specs,
        out_specs=[pl.BlockSpec((t, 128), lambda p, i: (i, p))] + c_out_specs,
        scratch_shapes=c_sems, input_output_aliases=_comm_aliases(comm, 3, 1), compiler_params=_cp(ARB, ARB),
    )(proj, proj, proj, *c_operands)
    return outs[0], list(outs[1:])


def _sb_bwd(proj, o, dmix, name, comm=None):
    s = proj.shape[0]
    t = min(ATT_T, s)
    nblk = s // t
    scale = HEAD_DIM ** -0.5

    def body(q_ref, k_ref, v_ref, o_ref, do_ref, dq_ref, dk_ref, dv_ref, dkt_ref, dvt_ref):
        i = pl.program_id(1)
        masks = _head_masks()
        row = _iota2((t, t), 0)
        col = _iota2((t, t), 1)
        u_strict = (row > col).astype(_BF)
        u_incl = (row >= col).astype(_BF)
        causal = col < row

        @pl.when(i == 0)
        def _():
            dkt_ref[...] = jnp.zeros_like(dkt_ref)
            dvt_ref[...] = jnp.zeros_like(dvt_ref)

        qs = q_ref[...] * scale
        dov = do_ref[...]
        ov = o_ref[...]
        qm_f = [jnp.where(masks[hh], qs, 0.0) for hh in range(2)]
        dom_f = [jnp.where(masks[hh], dov, 0.0) for hh in range(2)]
        qm = [qm_f[hh].astype(_BF) for hh in range(2)]
        dom = [dom_f[hh].astype(_BF) for hh in range(2)]
        qm_t = [qm_f[hh].T.astype(_BF) for hh in range(2)]
        dom_t = [dom_f[hh].T.astype(_BF) for hh in range(2)]
        dsum = [jnp.sum(dom[hh].astype(F32) * ov, axis=-1, keepdims=True) for hh in range(2)]

        def scores(j):
            off = pl.multiple_of(jnp.maximum(j, 0) * t, t)
            kb = k_ref[pl.ds(off, t), :].astype(_BF)
            vb = v_ref[pl.ds(off, t), :].astype(_BF)
            return tuple((_dot(qm[hh], kb, 1, 1), _dot(dom[hh], vb, 1, 1)) for hh in range(2))

        def block(j, carry, ahead, diag):
            off = pl.multiple_of(j * t, t)
            kb = k_ref[pl.ds(off, t), :].astype(_BF)
            runs, eruns, dqs = [], [], []
            dk = dv = None
            for hh in range(2):
                run, erun, dq = carry[0][hh], carry[1][hh], carry[2][hh]
                z, da = ahead[hh]
                l1, lb = _softplus_parts(z)
                if diag:
                    l1 = jnp.where(causal, l1, 0.0)
                between = run + _dot_exact(l1, u_strict, 1, 0, 2)
                a = jnp.exp(lb + between)
                if diag:
                    a = jnp.where(causal, a, 0.0)
                g = a * da
                cum = dsum[hh] - (erun + _dot_exact(g, u_incl, 1, 0, 2))
                beta = jnp.exp(lb)
                dz = g * (1.0 - beta) - cum * beta
                if diag:
                    dz = jnp.where(causal, dz, 0.0)
                dzb = dz.astype(_BF)
                dqs.append(dq + _dot(dzb, kb, 1, 0))
                dk_h = _dot(qm_t[hh], dzb, 1, 0)
                dv_h = _dot(dom_t[hh], a.astype(_BF), 1, 0)
                dk = dk_h if dk is None else dk + dk_h
                dv = dv_h if dv is None else dv + dv_h
                runs.append(run + jnp.sum(l1, axis=-1, keepdims=True))
                eruns.append(erun + jnp.sum(g, axis=-1, keepdims=True))
            dkt_ref[:, pl.ds(off, t)] += dk
            dvt_ref[:, pl.ds(off, t)] += dv
            return tuple(runs), tuple(eruns), tuple(dqs)

        zero, zero_q = jnp.zeros((t, 1), F32), jnp.zeros((t, 128), F32)
        ahead = scores(i - 1)
        carry = block(i, ((zero, zero), (zero, zero), (zero_q, zero_q)), scores(i), True)
        carry = _sb_key_loop(i, carry, lambda j, c, a: block(j, c, a, False), scores, ahead)
        dq_ref[...] = jnp.where(masks[0], carry[2][0], carry[2][1]) * scale

        @pl.when(i == nblk - 1)
        def _():
            _untranspose(dkt_ref, dk_ref, t, nblk)
            _untranspose(dvt_ref, dv_ref, t, nblk)

    kv_spec = lambda c0: pl.BlockSpec((s, 128), lambda p, i: (0, c0 // 128 + p))
    blk = lambda c0: pl.BlockSpec((t, 128), lambda p, i: (i, c0 // 128 + p))
    acc = pl.BlockSpec((s, 128), lambda p, i: (0, p))
    shp = jax.ShapeDtypeStruct((s, 256), F32)
    c_operands, c_in_specs, c_out_shapes, c_out_specs, c_sems = _comm_operands(comm) if comm else ([], [], [], [], [])
    outs = _pcall(
        _with_comm(body, comm, 5, 3, *_grid_ends(SB_HEADS // 2, nblk), n_scratch=2), name=name,
        grid=(SB_HEADS // 2, nblk), out_shape=[shp, shp, shp] + c_out_shapes,
        in_specs=[blk(C_QA), kv_spec(C_KA), kv_spec(C_VA), blk(0), blk(M_SB)] + c_in_specs,
        out_specs=[blk(0), acc, acc] + c_out_specs,
        scratch_shapes=[pltpu.VMEM((128, s), F32), pltpu.VMEM((128, s), F32)] + c_sems,
        input_output_aliases=_comm_aliases(comm, 5, 3), compiler_params=_cp(ARB, ARB),
    )(proj, proj, proj, o, dmix, *c_operands)
    return outs[0], outs[1], outs[2], list(outs[3:])


def _group_matrix(width):
    r = _iota2((width, width), 0) // HEAD_DIM
    c = _iota2((width, width), 1) // HEAD_DIM
    return (r == c).astype(_BF)


def _group_mean(v, gm):
    return _dot_exact(v, gm, 1, 0, 2) * (1.0 / HEAD_DIM)


def _fox_prep(proj, gq, gk, bf, name):
    s = proj.shape[0]
    t = min(ATT_T, s)
    nblk = s // t
    assert nblk <= 128

    def body(q_ref, k_ref, fl_ref, gq_ref, gk_ref, bf_ref, qn_ref, kn_ref, cfc_ref, cfr_ref, cfe_ref, carry_ref):
        i = pl.program_id(0)
        gm = _group_matrix(512)
        for src, gref, dst in ((q_ref, gq_ref, qn_ref), (k_ref, gk_ref, kn_ref)):
            v = src[...]
            rstd = lax.rsqrt(_group_mean(v * v, gm) + EPS)
            dst[...] = (v * rstd * gref[...]).astype(dst.dtype)

        @pl.when(i == 0)
        def _():
            carry_ref[...] = jnp.zeros_like(carry_ref)
            cfe_ref[...] = jnp.full(cfe_ref.shape, NO_BLOCK, F32)

        lane = _iota2((1, 128), 1)
        logf = jnp.where(lane < FOX_HEADS, _softplus_parts(-(fl_ref[...] + bf_ref[...]))[0], 0.0)
        lower = (_iota2((t, t), 0) >= _iota2((t, t), 1)).astype(_BF)
        cf = carry_ref[...] + _dot_exact(logf, lower, 1, 0, 3, v_left=False)
        cfc_ref[...] = cf
        rows = cf.T[:FOX_HEADS, :]
        cfr_ref[...] = rows
        cfe_ref[...] = jnp.where(lane == i, rows[:, t - 1:t], cfe_ref[...])
        carry_ref[...] = cf[t - 1:t, :]

    col = lambda w, c0: pl.BlockSpec((t, w), lambda i: (i, c0 // w))
    v512 = pl.BlockSpec((1, 512), lambda i: (0, 0))
    return _pcall(
        body, name=name, grid=(nblk,),
        out_shape=[jax.ShapeDtypeStruct((s, 512), _BF), jax.ShapeDtypeStruct((s, 512), _BF),
                   jax.ShapeDtypeStruct((s, 128), F32), jax.ShapeDtypeStruct((FOX_HEADS, s), F32),
                   jax.ShapeDtypeStruct((FOX_HEADS, 128), F32)],
        in_specs=[col(512, C_QB), col(512, C_KB), col(128, C_FL), v512, v512, pl.BlockSpec((1, 128), lambda i: (0, 0))],
        out_specs=[col(512, 0), col(512, 0), col(128, 0), pl.BlockSpec((FOX_HEADS, t), lambda i: (0, i)),
                   pl.BlockSpec((FOX_HEADS, 128), lambda i: (0, 0))],
        scratch_shapes=[pltpu.VMEM((1, 128), F32)], compiler_params=_cp(ARB),
    )(proj, proj, proj, gq, gk, bf)


def _fox_cq(cfc_blk, head):
    lane = _iota2((1, 128), 1)
    return jnp.sum(jnp.where(lane == head, cfc_blk, 0.0), axis=-1, keepdims=True)


def _fox_ck(cfr_ref, off, t, head):
    sub = _iota2((FOX_HEADS, 1), 0)
    return jnp.sum(jnp.where(sub == head, cfr_ref[:, pl.ds(off, t)], 0.0), axis=0, keepdims=True)


def _fox_blocks_needed(top, cfe_ref, head, i):
    sub = _iota2((FOX_HEADS, 1), 0)
    lane = _iota2((1, 128), 1)
    last_key = jnp.sum(jnp.where(sub == head, cfe_ref[...], 0.0), axis=0, keepdims=True)
    need = jnp.logical_and(lane < i, top - last_key > UNDERFLOW)
    return jnp.sum(need.astype(jnp.int32))


def _grid_ends(n_outer, n_inner):
    first = lambda: jnp.logical_and(pl.program_id(0) == 0, pl.program_id(1) == 0)
    last = lambda: jnp.logical_and(pl.program_id(0) == n_outer - 1, pl.program_id(1) == n_inner - 1)
    return first, last


def _fox_fwd(proj, qn, kn, cfc, cfr, cfe, qk_bound, name, comm=None):
    s = proj.shape[0]
    t = min(ATT_T, s)
    nblk = s // t
    scale = HEAD_DIM ** -0.5

    def body(q_ref, k_ref, v_ref, cfc_ref, cfr_ref, cfe_ref, qkb_ref, o_ref, lse_ref):
        p = pl.program_id(0)
        i = pl.program_id(1)
        masks = _head_masks()
        lane = _iota2((1, 128), 1)
        valid = _iota2((t, t), 1) <= _iota2((t, t), 0)
        qv = q_ref[...]
        cfc_blk = cfc_ref[...]
        heads = [2 * p + hh for hh in range(2)]
        qm = [(jnp.where(masks[hh], qv, 0) * scale).astype(_BF) for hh in range(2)]
        cq = [_fox_cq(cfc_blk, heads[hh]) for hh in range(2)]

        def scores(j):
            kb = k_ref[pl.ds(pl.multiple_of(jnp.maximum(j, 0) * t, t), t), :]
            return tuple(_dot(qm[hh], kb, 1, 1) for hh in range(2))

        def block(j, carry, qk, diag):
            off = pl.multiple_of(j * t, t)
            vb = v_ref[pl.ds(off, t), :].astype(_BF)
            ms, ls, accs = [], [], []
            for hh in range(2):
                m, l, acc = carry[0][hh], carry[1][hh], carry[2][hh]
                z = qk[hh] + (cq[hh] - _fox_ck(cfr_ref, off, t, heads[hh]))
                if diag:
                    z = jnp.where(valid, z, NEG)
                m_new = jnp.maximum(m, jnp.max(z, axis=-1, keepdims=True))
                pe = jnp.exp(z - m_new)
                alpha = jnp.exp(m - m_new)
                ms.append(m_new)
                ls.append(alpha * l + jnp.sum(pe, axis=-1, keepdims=True))
                accs.append(alpha * acc + _dot(pe.astype(_BF), vb, 1, 0))
            return tuple(ms), tuple(ls), tuple(accs)

        def step(jj, c):
            ahead = scores(i - jj - 2)
            return block(i - jj - 1, c[0], c[1], False), ahead

        neg, zero, zero_o = jnp.full((t, 1), NEG, F32), jnp.zeros((t, 1), F32), jnp.zeros((t, 128), F32)
        ahead = scores(i - 1)
        carry = block(i, ((neg, neg), (zero, zero), (zero_o, zero_o)), scores(i), True)
        needed = [_fox_blocks_needed(jnp.max(qkb_ref[...] + cq[hh] - carry[0][hh]), cfe_ref, heads[hh], i)
                  for hh in range(2)]
        m, l, acc = lax.fori_loop(0, jnp.maximum(needed[0], needed[1]), step, (carry, ahead))[0]
        o_ref[...] = jnp.where(masks[0], acc[0] / l[0], acc[1] / l[1])
        lse_ref[...] = jnp.where(lane == 0, m[0] + jnp.log(l[0]), jnp.where(lane == 1, m[1] + jnp.log(l[1]), 0.0))

    c_operands, c_in_specs, c_out_shapes, c_out_specs, c_sems = _comm_operands(comm) if comm else ([], [], [], [], [])
    outs = _pcall(
        _with_comm(body, comm, 7, 2, *_grid_ends(FOX_HEADS // 2, nblk)), name=name, grid=(FOX_HEADS // 2, nblk),
        out_shape=[jax.ShapeDtypeStruct((s, 512), F32), jax.ShapeDtypeStruct((FOX_HEADS // 2, s, 128), F32)] + c_out_shapes,
        in_specs=[pl.BlockSpec((t, 128), lambda p, i: (i, p)), pl.BlockSpec((s, 128), lambda p, i: (0, p)),
                  pl.BlockSpec((s, 128), lambda p, i: (0, C_VB // 128 + p)),
                  pl.BlockSpec((t, 128), lambda p, i: (i, 0)), pl.BlockSpec((FOX_HEADS, s), lambda p, i: (0, 0)),
                  pl.BlockSpec((FOX_HEADS, 128), lambda p, i: (0, 0)), pl.BlockSpec((1, 1), lambda p, i: (0, 0))] + c_in_specs,
        out_specs=[pl.BlockSpec((t, 128), lambda p, i: (i, p)),
                   pl.BlockSpec((None, t, 128), lambda p, i: (p, i, 0))] + c_out_specs,
        scratch_shapes=c_sems, input_output_aliases=_comm_aliases(comm, 7, 2),
        compiler_params=_cp(ARB, ARB),
    )(qn, kn, proj, cfc, cfr, cfe, qk_bound, *c_operands)
    return outs[0], outs[1], list(outs[2:])


def _fox_bwd(proj, qn, kn, cfc, cfr, cfe, qk_bound, o, lse, dmix, name, comm=None):
    s = proj.shape[0]
    t = min(ATT_T, s)
    nblk = s // t
    scale = HEAD_DIM ** -0.5

    def body(q_ref, k_ref, v_ref, cfc_ref, cfr_ref, cfe_ref, qkb_ref, o_ref, lse_ref, do_ref,
             dq_ref, dk_ref, dv_ref, dcr_ref, dkt_ref, dvt_ref):
        p = pl.program_id(0)
        i = pl.program_id(1)
        masks = _head_masks()
        lane = _iota2((1, 128), 1)
        valid = _iota2((t, t), 1) <= _iota2((t, t), 0)
        before = (_iota2((t, t), 0) < _iota2((t, t), 1)).astype(_BF)

        @pl.when(i == 0)
        def _():
            dkt_ref[...] = jnp.zeros_like(dkt_ref)
            dvt_ref[...] = jnp.zeros_like(dvt_ref)

        @pl.when(jnp.logical_and(p == 0, i == 0))
        def _():
            dcr_ref[...] = jnp.zeros_like(dcr_ref)

        head_row = _iota2((FOX_HEADS, 1), 0)
        qv = q_ref[...]
        cfc_blk = cfc_ref[...]
        dov = do_ref[...]
        ov = o_ref[...]
        lse_blk = lse_ref[...]
        heads = [2 * p + hh for hh in range(2)]
        qm = [(jnp.where(masks[hh], qv, 0) * scale).astype(_BF) for hh in range(2)]
        dom = [jnp.where(masks[hh], dov, 0.0).astype(_BF) for hh in range(2)]
        qm_t = [qm[hh].astype(F32).T.astype(_BF) for hh in range(2)]
        dom_t = [dom[hh].astype(F32).T.astype(_BF) for hh in range(2)]
        dsum = [jnp.sum(dom[hh].astype(F32) * ov, axis=-1, keepdims=True) for hh in range(2)]
        lse_h = [jnp.sum(jnp.where(lane == hh, lse_blk, 0.0), axis=-1, keepdims=True) for hh in range(2)]
        cq = [_fox_cq(cfc_blk, heads[hh]) for hh in range(2)]
        needed = [_fox_blocks_needed(jnp.max(qkb_ref[...] + cq[hh] - lse_h[hh]), cfe_ref, heads[hh], i)
                  for hh in range(2)]
        first = i - jnp.maximum(needed[0], needed[1])

        def scores(j):
            off = pl.multiple_of(j * t, t)
            kb = k_ref[pl.ds(off, t), :]
            vb = v_ref[pl.ds(off, t), :].astype(_BF)
            return tuple((_dot(qm[hh], kb, 1, 1), _dot(dom[hh], vb, 1, 1)) for hh in range(2))

        def block(j, carry, ahead, diag):
            off = pl.multiple_of(j * t, t)
            kb = k_ref[pl.ds(off, t), :]
            dqs, pres = [], []
            dk = dv = None
            for hh in range(2):
                dq, pre = carry[0][hh], carry[1][hh]
                z = ahead[hh][0] + (cq[hh] - _fox_ck(cfr_ref, off, t, heads[hh]))
                pm = jnp.exp(z - lse_h[hh])
                if diag:
                    pm = jnp.where(valid, pm, 0.0)
                ds = pm * (ahead[hh][1] - dsum[hh])
                dsb = ds.astype(_BF)
                dqs.append(dq + _dot(dsb, kb, 1, 0))
                dk_h = _dot(qm_t[hh], dsb, 1, 0)
                dv_h = _dot(dom_t[hh], pm.astype(_BF), 1, 0)
                dk = dk_h if dk is None else dk + dk_h
                dv = dv_h if dv is None else dv + dv_h
                if diag:
                    lower_keys = jnp.where(valid, pre + _dot_exact(ds, before, 1, 0, 2), 0.0)
                    dlogf = jnp.sum(lower_keys, axis=0, keepdims=True)
                else:
                    dlogf = jnp.sum(pre) + _dot_exact(jnp.sum(ds, axis=0, keepdims=True), before, 1, 0, 3)
                dcr_ref[:, pl.ds(off, t)] += jnp.where(head_row == heads[hh], dlogf, 0.0)
                pres.append(pre + jnp.sum(ds, axis=-1, keepdims=True))
            dkt_ref[:, pl.ds(off, t)] += dk
            dvt_ref[:, pl.ds(off, t)] += dv
            return tuple(dqs), tuple(pres)

        zero, zero_q = jnp.zeros((t, 1), F32), jnp.zeros((t, 128), F32)

        def step(j, c):
            ahead = scores(j + 1)
            return block(j, c[0], c[1], False), ahead

        carry, ahead = lax.fori_loop(first, i, step, (((zero_q, zero_q), (zero, zero)), scores(first)))
        dqs, _ = block(i, carry, ahead, True)
        dq_ref[...] = jnp.where(masks[0], dqs[0], dqs[1]) * scale

        @pl.when(i == nblk - 1)
        def _():
            _untranspose(dkt_ref, dk_ref, t, nblk)
            _untranspose(dvt_ref, dv_ref, t, nblk)

    blk = lambda c0: pl.BlockSpec((t, 128), lambda p, i: (i, c0 // 128 + p))
    res = lambda c0: pl.BlockSpec((s, 128), lambda p, i: (0, c0 // 128 + p))
    shp = jax.ShapeDtypeStruct((s, 512), F32)
    c_operands, c_in_specs, c_out_shapes, c_out_specs, c_sems = _comm_operands(comm) if comm else ([], [], [], [], [])
    transposed = [pltpu.VMEM((128, s), F32), pltpu.VMEM((128, s), F32)]
    outs = _pcall(
        _with_comm(body, comm, 10, 4, *_grid_ends(FOX_HEADS // 2, nblk), n_scratch=2), name=name,
        grid=(FOX_HEADS // 2, nblk),
        out_shape=[shp, shp, shp, jax.ShapeDtypeStruct((FOX_HEADS, s), F32)] + c_out_shapes,
        in_specs=[blk(0), res(0), res(C_VB), pl.BlockSpec((t, 128), lambda p, i: (i, 0)),
                  pl.BlockSpec((FOX_HEADS, s), lambda p, i: (0, 0)), pl.BlockSpec((FOX_HEADS, 128), lambda p, i: (0, 0)),
                  pl.BlockSpec((1, 1), lambda p, i: (0, 0)), blk(0),
                  pl.BlockSpec((None, t, 128), lambda p, i: (p, i, 0)), blk(M_FOX)] + c_in_specs,
        out_specs=[blk(0), res(0), res(0), pl.BlockSpec((FOX_HEADS, s), lambda p, i: (0, 0))] + c_out_specs,
        scratch_shapes=transposed + c_sems, input_output_aliases=_comm_aliases(comm, 10, 4),
        compiler_params=_cp(ARB, ARB),
    )(qn, kn, proj, cfc, cfr, cfe, qk_bound, o, lse, dmix, *c_operands)
    return outs[0], outs[1], outs[2], outs[3], list(outs[4:])


def _fox_prep_bwd(proj, gq, gk, bf, dqn, dkn, dlogf, name):
    s = proj.shape[0]
    t = min(ATT_T, s)
    nblk = s // t

    def body(q_ref, k_ref, fl_ref, gq_ref, gk_ref, bf_ref, dqn_ref, dkn_ref, dlogf_ref,
             dq_ref, dk_ref, dfl_ref, dgq_ref, dgk_ref, dbf_ref):
        i = pl.program_id(0)
        gm = _group_matrix(512)

        @pl.when(i == 0)
        def _():
            dgq_ref[...] = jnp.zeros_like(dgq_ref)
            dgk_ref[...] = jnp.zeros_like(dgk_ref)
            dbf_ref[...] = jnp.zeros_like(dbf_ref)

        for src, gref, dyref, dst, dgref in ((q_ref, gq_ref, dqn_ref, dq_ref, dgq_ref),
                                              (k_ref, gk_ref, dkn_ref, dk_ref, dgk_ref)):
            v = src[...]
            dy = dyref[...]
            rstd = lax.rsqrt(_group_mean(v * v, gm) + EPS)
            xhat = v * rstd
            dgref[...] += jnp.sum(dy * xhat, axis=0, keepdims=True)
            dxh = dy * gref[...]
            dst[...] = (rstd * (dxh - xhat * _group_mean(dxh * xhat, gm))).astype(dst.dtype)

        to_lanes = (_iota2((FOX_HEADS, 128), 0) == _iota2((FOX_HEADS, 128), 1)).astype(_BF)
        dlogf = _dot_exact(dlogf_ref[...], to_lanes, 0, 0, 3)
        xv = fl_ref[...] + bf_ref[...]
        e = jnp.exp(-jnp.abs(xv))
        dfl = dlogf * (jnp.where(xv >= 0.0, e, 1.0) / (1.0 + e))
        dfl_ref[...] = dfl.astype(dfl_ref.dtype)
        dbf_ref[...] += jnp.sum(dfl, axis=0, keepdims=True)

    col = lambda w, c0: pl.BlockSpec((t, w), lambda i: (i, c0 // w))
    v512 = pl.BlockSpec((1, 512), lambda i: (0, 0))
    v128 = pl.BlockSpec((1, 128), lambda i: (0, 0))
    return _pcall(
        body, name=name, grid=(nblk,),
        out_shape=[jax.ShapeDtypeStruct((s, 512), _BF), jax.ShapeDtypeStruct((s, 512), _BF),
                   jax.ShapeDtypeStruct((s, 128), _BF), jax.ShapeDtypeStruct((1, 512), F32),
                   jax.ShapeDtypeStruct((1, 512), F32), jax.ShapeDtypeStruct((1, 128), F32)],
        in_specs=[col(512, C_QB), col(512, C_KB), col(128, C_FL), v512, v512, v128, col(512, 0), col(512, 0),
                  pl.BlockSpec((FOX_HEADS, t), lambda i: (0, i))],
        out_specs=[col(512, 0), col(512, 0), col(128, 0), v512, v512, v128], compiler_params=_cp(ARB),
    )(proj, proj, proj, gq, gk, bf, dqn, dkn, dlogf)


GELU_C = 0.7978845608028654
GELU_A = 0.044715


def _gelu(x):
    return 0.5 * x * (1.0 + jnp.tanh(GELU_C * (x + GELU_A * x * x * x)))


def _gelu_grad(x):
    th = jnp.tanh(GELU_C * (x + GELU_A * x * x * x))
    return 0.5 * (1.0 + th) + 0.5 * x * (1.0 - th * th) * (GELU_C * (1.0 + 3.0 * GELU_A * x * x))


def _sgu_tril():
    return _iota2((SGU_CHUNK, SGU_CHUNK), 0) >= _iota2((SGU_CHUNK, SGU_CHUNK), 1)


def _sgu_group_masks():
    lane = _iota2((1, 256), 1)
    return [(lane // HEAD_DIM) == g for g in range(SGU_GROUPS)]


def _sgu_fwd(proj, w, gs, bexp, name):
    s = proj.shape[0]
    t = min(2 * SGU_CHUNK, s)

    def body(u_ref, v_ref, w_ref, gs_ref, b_ref, o_ref):
        gm = _group_matrix(256)
        gmask = _sgu_group_masks()
        tril = _sgu_tril()
        u = _gelu(u_ref[...])
        vg = _gelu(v_ref[...])
        vhat = (vg * lax.rsqrt(_group_mean(vg * vg, gm) + EPS) * gs_ref[...]).astype(_BF)
        for ch in range(t // SGU_CHUNK):
            rows = slice(ch * SGU_CHUNK, (ch + 1) * SGU_CHUNK)
            mixed = b_ref[...]
            for g in range(SGU_GROUPS):
                wg = jnp.where(tril, w_ref[g], 0.0).astype(_BF)
                mixed = jnp.where(gmask[g], mixed + _dot(wg, vhat[rows], 1, 0), mixed)
            o_ref[rows, :] = u[rows] * mixed

    return _pcall(
        body, name=name, grid=(s // t,), out_shape=jax.ShapeDtypeStruct((s, 256), F32),
        in_specs=[_row_spec(t, 256, C_UC // 256), _row_spec(t, 256, C_VC // 256),
                  pl.BlockSpec((SGU_GROUPS, SGU_CHUNK, SGU_CHUNK), lambda i: (0, 0, 0)), _vec_spec(256),
                  pl.BlockSpec((SGU_CHUNK, 256), lambda i: (0, 0))],
        out_specs=_row_spec(t, 256), compiler_params=_cp(PAR),
    )(proj, proj, w, gs, bexp)


def _sgu_bwd(proj, w, gs, bexp, dmix, name):
    s = proj.shape[0]
    t = min(2 * SGU_CHUNK, s)
    nstep = s // t

    def body(u_ref, v_ref, w_ref, gs_ref, b_ref, do_ref, du_ref, dv_ref, dw_ref, db_ref, dgs_ref):
        i = pl.program_id(0)
        gm = _group_matrix(256)
        gmask = _sgu_group_masks()
        tril = _sgu_tril()

        @pl.when(i == 0)
        def _():
            dw_ref[...] = jnp.zeros_like(dw_ref)
            db_ref[...] = jnp.zeros_like(db_ref)
            dgs_ref[...] = jnp.zeros_like(dgs_ref)

        uc = u_ref[...]
        vc = v_ref[...]
        u = _gelu(uc)
        vg = _gelu(vc)
        rstd = lax.rsqrt(_group_mean(vg * vg, gm) + EPS)
        xh = vg * rstd
        gsv = gs_ref[...]
        vhat = (xh * gsv).astype(_BF)
        dov = do_ref[...]
        dm = dov * u
        for ch in range(t // SGU_CHUNK):
            rows = slice(ch * SGU_CHUNK, (ch + 1) * SGU_CHUNK)
            mixed = b_ref[...]
            dvh = jnp.zeros((SGU_CHUNK, 256), F32)
            dmc = dm[rows]
            for g in range(SGU_GROUPS):
                wg = jnp.where(tril, w_ref[g], 0.0).astype(_BF)
                mixed = jnp.where(gmask[g], mixed + _dot(wg, vhat[rows], 1, 0), mixed)
                dvh = jnp.where(gmask[g], _dot(wg, dmc.astype(_BF), 0, 0), dvh)
                dw_ref[g] += _dot(jnp.where(gmask[g], dmc, 0.0).astype(_BF), vhat[rows], 1, 1)
            db_ref[...] += dmc
            du_ref[rows, :] = (dov[rows] * mixed * _gelu_grad(uc[rows])).astype(du_ref.dtype)
            xhc = xh[rows]
            dgs_ref[...] += jnp.sum(dvh * xhc, axis=0, keepdims=True)
            dxh = dvh * gsv
            dvg = rstd[rows] * (dxh - xhc * _group_mean(dxh * xhc, gm))
            dv_ref[rows, :] = (dvg * _gelu_grad(vc[rows])).astype(dv_ref.dtype)

        @pl.when(i == nstep - 1)
        def _():
            for g in range(SGU_GROUPS):
                dw_ref[g] = jnp.where(tril, dw_ref[g], 0.0)

    wspec = pl.BlockSpec((SGU_GROUPS, SGU_CHUNK, SGU_CHUNK), lambda i: (0, 0, 0))
    bspec = pl.BlockSpec((SGU_CHUNK, 256), lambda i: (0, 0))
    return _pcall(
        body, name=name, grid=(nstep,),
        out_shape=[jax.ShapeDtypeStruct((s, 256), _BF), jax.ShapeDtypeStruct((s, 256), _BF),
                   jax.ShapeDtypeStruct((SGU_GROUPS, SGU_CHUNK, SGU_CHUNK), F32),
                   jax.ShapeDtypeStruct((SGU_CHUNK, 256), F32), jax.ShapeDtypeStruct((1, 256), F32)],
        in_specs=[_row_spec(t, 256, C_UC // 256), _row_spec(t, 256, C_VC // 256), wspec, _vec_spec(256), bspec,
                  _row_spec(t, 256, M_SGU // 256)],
        out_specs=[_row_spec(t, 256), _row_spec(t, 256), wspec, bspec, _vec_spec(256)],
        compiler_params=_cp(ARB),
    )(proj, proj, w, gs, bexp, dmix)


def _ada_fwd(c_all, ada_w, name):
    depth, d, n = ada_w.shape

    def body(c_ref, w_ref, o_ref):
        cv = c_ref[...]
        cond = cv / (1.0 + jnp.exp(-cv))
        o_ref[...] = _dot_f32(cond, w_ref[...], 1, 0)

    return _pcall(
        body, name=name, grid=(depth,), out_shape=jax.ShapeDtypeStruct((depth, N_DEV, n), F32),
        in_specs=[pl.BlockSpec((N_DEV, d), lambda l: (0, 0)), pl.BlockSpec((None, d, n), lambda l: (l, 0, 0))],
        out_specs=pl.BlockSpec((None, N_DEV, n), lambda l: (l, 0, 0)), compiler_params=_cp(PAR),
    )(c_all, ada_w)


def _ada_bwd(c_all, dmod, name):
    depth, _, n = dmod.shape
    d = c_all.shape[1]

    def body(c_ref, dm_ref, o_ref):
        cv = c_ref[...]
        cond = cv / (1.0 + jnp.exp(-cv))
        o_ref[...] = _dot_f32(cond, dm_ref[...], 0, 0)

    return _pcall(
        body, name=name, grid=(depth,), out_shape=jax.ShapeDtypeStruct((depth, d, n), F32),
        in_specs=[pl.BlockSpec((N_DEV, d), lambda l: (0, 0)), pl.BlockSpec((None, N_DEV, n), lambda l: (l, 0, 0))],
        out_specs=pl.BlockSpec((None, d, n), lambda l: (l, 0, 0)), compiler_params=_cp(PAR),
    )(c_all, dmod)


def _adamw(slots, w, m, v, name):
    n, r, c = slots.shape
    tr = 256 if r % 256 == 0 else r
    bc1 = 1.0 - ADAM_B1 ** ADAM_STEP
    bc2 = 1.0 - ADAM_B2 ** ADAM_STEP

    def body(s_ref, w_ref, m_ref, v_ref, g_ref, d_ref, nm_ref, nv_ref):
        g = s_ref[0].astype(F32)
        for j in range(1, n):
            g = g + s_ref[j].astype(F32)
        m_new = ADAM_B1 * m_ref[...] + (1.0 - ADAM_B1) * g
        v_new = ADAM_B2 * v_ref[...] + (1.0 - ADAM_B2) * (g * g)
        g_ref[...] = g
        nm_ref[...] = m_new
        nv_ref[...] = v_new
        d_ref[...] = -ADAM_LR * ((m_new / bc1) / (jnp.sqrt(v_new / bc2) + ADAM_EPS) + ADAM_WD * w_ref[...])

    tile = pl.BlockSpec((tr, c), lambda i: (i, 0))
    shp = jax.ShapeDtypeStruct((r, c), F32)
    return _pcall(
        body, name=name, grid=(r // tr,), out_shape=[shp] * 4,
        in_specs=[pl.BlockSpec((n, tr, c), lambda i: (0, i, 0)), tile, tile, tile],
        out_specs=[tile] * 4, compiler_params=_cp(PAR),
    )(slots, w, m, v)


def _w_in_layout(w):
    pad = jnp.zeros(w.shape[:-1] + (IN_P - IN_W,), w.dtype)
    return jnp.concatenate([w[..., 768:FL_SRC], w[..., :768], w[..., UC_SRC:], w[..., FL_SRC:UC_SRC], pad], axis=-1)


def _w_in_unlayout(g):
    return jnp.concatenate([g[..., C_QA:C_UC], g[..., :C_QA], g[..., C_FL:C_FL + FOX_HEADS], g[..., C_UC:C_FL]], axis=-1)


SMALL = [("ada_b", DEPTH * 6 * D_MODEL), ("norm1_g", DEPTH * D_MODEL), ("norm2_g", DEPTH * D_MODEL),
         ("sgu_w", DEPTH * SGU_GROUPS * SGU_CHUNK * SGU_CHUNK), ("sgu_b", DEPTH * SGU_GROUPS * SGU_CHUNK),
         ("sgu_norm_g", DEPTH * SGU_GROUPS * HEAD_DIM), ("q_norm_g", DEPTH * HEAD_DIM), ("k_norm_g", DEPTH * HEAD_DIM),
         ("b_forget", DEPTH * FOX_HEADS), ("loss", 1)]
SMALL_ROWS = 2560


def _pack_small(parts):
    flat = jnp.concatenate([parts[name].reshape(-1).astype(F32) for name, _ in SMALL])
    return jnp.pad(flat, (0, SMALL_ROWS * 128 - flat.shape[0])).reshape(SMALL_ROWS, 128)


def _unpack_small(packed, shapes):
    flat = packed.reshape(-1)
    out, off = {}, 0
    for name, size in SMALL:
        out[name] = flat[off:off + size].reshape(shapes[name])
        off += size
    return out


def kernel(x, c, ada_w, ada_b, norm1_g, norm2_g, w_in, b_forget, q_norm_g, k_norm_g, sgu_norm_g, sgu_w, sgu_b, w_out, mlp_w1, mlp_w2, loss_target, m_ada_w, m_ada_b, m_norm1_g, m_norm2_g, m_w_in, m_b_forget, m_q_norm_g, m_k_norm_g, m_sgu_norm_g, m_sgu_w, m_sgu_b, m_w_out, m_mlp_w1, m_mlp_w2, v_ada_w, v_ada_b, v_norm1_g, v_norm2_g, v_w_in, v_b_forget, v_q_norm_g, v_k_norm_g, v_sgu_norm_g, v_sgu_w, v_sgu_b, v_w_out, v_mlp_w1, v_mlp_w2):
    me = _lin(_my_pos())
    x0 = x[0]
    target = loss_target[0]
    n_ada = ada_w.shape[2]

    shards = [w.astype(_XBF) for w in (w_in, w_out, mlp_w1, mlp_w2)]

    def whole_in(w_in_g):
        return _w_in_layout(jnp.transpose(w_in_g, (1, 0, 2)).reshape(D_MODEL, IN_W))

    def whole_rest(w_out_g, w1_g, w2_g):
        return (w_out_g.reshape(D_MODEL, D_MODEL), jnp.transpose(w1_g, (1, 0, 2)).reshape(D_MODEL, D_FF),
                w2_g.reshape(D_FF, D_MODEL))

    w_in_first, c_all = _exchange([shards[0][0], c], False, "gather_weights")
    c_all = c_all.reshape(N_DEV, D_MODEL)
    weights = [None] * DEPTH

    mod_part = _ada_fwd(c_all, ada_w, "ada_fwd")
    (mod_rows,) = _exchange([jnp.transpose(mod_part, (1, 0, 2))], True, "scatter_mod")
    mod = jnp.transpose(mod_rows, (1, 0, 2)).reshape(DEPTH, 6 * D_MODEL) + ada_b
    mods = mod.reshape(DEPTH, 6, 1, D_MODEL)

    saved = []
    xl = x0
    for l in range(DEPTH):
        sh1, sc1, g1, sh2, sc2, g2 = (mods[l, r] for r in range(6))
        n1 = norm1_g[l].reshape(1, D_MODEL)
        n2 = norm2_g[l].reshape(1, D_MODEL)
        gq = jnp.tile(q_norm_g[l], FOX_HEADS).reshape(1, 512)
        gk = jnp.tile(k_norm_g[l], FOX_HEADS).reshape(1, 512)
        bf = jnp.pad(b_forget[l], (0, 128 - FOX_HEADS)).reshape(1, 128)
        gs = sgu_norm_g[l].reshape(1, 256)
        bexp = jnp.repeat(sgu_b[l].T, HEAD_DIM, axis=1)

        if l == 0:
            w_in_l = whole_in(w_in_first)
            gather_rest = dict(arrays=[w[0] for w in shards[1:]], scatter=False, recv=None, slot=None)
        else:
            w_in_l, gather_rest = weights[l][0], None
        h1, h1_t = _norm_mod(xl, n1, sc1, sh1, "norm_mod")
        proj = _mm_nn(h1, w_in_l, "plain", (), "mm_in")
        o_sb, gathered = _sb_fwd(proj, "sb_fwd", gather_rest)
        if gathered:
            weights[0] = (w_in_l,) + whole_rest(*gathered)
        qn, kn, cfc, cfr, cfe = _fox_prep(proj, gq, gk, bf, "fox_prep")
        qkb = (1.02 * HEAD_DIM ** 0.5 * jnp.max(jnp.abs(q_norm_g[l])) * jnp.max(jnp.abs(k_norm_g[l]))).reshape(1, 1)
        gather_next = None
        if l + 1 < DEPTH:
            gather_next = dict(arrays=[w[l + 1] for w in shards], scatter=False, recv=None, slot=None)
        o_fox, lse, gathered = _fox_fwd(proj, qn, kn, cfc, cfr, cfe, qkb, "fox_fwd", gather_next)
        if gathered:
            weights[l + 1] = (whole_in(gathered[0]),) + whole_rest(*gathered[1:])
        _, w_out_l, w1_l, w2_l = weights[l]
        o_sgu = _sgu_fwd(proj, sgu_w[l], gs, bexp, "sgu_fwd")
        x_mid, y1, mixed_t = _mm_out((o_sb, o_fox, o_sgu), w_out_l, xl, g1, "mm_out")
        h2, h2_t = _norm_mod(x_mid, n2, sc2, sh2, "norm_mod")
        r2, r2_t = _mm_nn(h2, w1_l, "relu2", (), "mm_w1")
        x_out, y2 = _mm_nn(r2, w2_l, "resid", (x_mid, g2), "mm_w2")
        saved.append(dict(x_in=xl, h1_t=h1_t, proj=proj, o_sb=o_sb, qn=qn, kn=kn, cfc=cfc, cfr=cfr, cfe=cfe, o_fox=o_fox,
                          lse=lse, mixed_t=mixed_t, x_mid=x_mid, y1=y1, h2_t=h2_t, r2=r2, r2_t=r2_t, y2=y2,
                          n1=n1, n2=n2, gq=gq, gk=gk, bf=bf, gs=gs, bexp=bexp, qkb=qkb))
        xl = x_out

    loss_part, dx, dy2, dg2 = _loss_head(xl, target, (saved[-1]["y2"], mods[DEPTH - 1, 5]), "loss_head")

    grads_ready = None
    received = [jnp.zeros((N_DEV,) + w.shape, _XBF) for w in shards]
    small = {k: [None] * DEPTH for k in ("mod", "norm1_g", "norm2_g", "sgu_w", "sgu_b", "sgu_norm_g",
                                         "q_norm_g", "k_norm_g", "b_forget")}
    for l in reversed(range(DEPTH)):
        sv = saved[l]
        sh1, sc1, g1, sh2, sc2, g2 = (mods[l, r] for r in range(6))
        w_in_l, w_out_l, w1_l, w2_l = weights[l]
        da = _mm_nt(dy2, w2_l, "mul2", (sv["r2"],), _BF, "mm_w2_bwd")
        g_w2 = _mm_grad(sv["r2_t"], dy2, "rows", "mm_w2_grad")
        dh2 = _mm_nt(da, w1_l, "plain", (), F32, "mm_w1_bwd")
        g_w1 = _mm_grad(sv["h2_t"], da, "cols", "mm_w1_grad")
        dx_mid, dsh2, dsc2, dn2, dy1, dg1 = _norm_mod_bwd(sv["x_mid"], dh2, dx, sv["n2"], sc2, (sv["y1"], g1),
                                                          "norm_mod_bwd")
        dmix = _mm_nt(dy1, w_out_l, "plain", (), F32, "mm_out_bwd")
        g_out = _mm_grad(sv["mixed_t"], dy1, "rows", "mm_out_grad")
        send_own = None
        if l == 0:
            send_own = dict(arrays=[g_out, g_w1, g_w2], scatter=True, recv=received[1:], slot=0)
        dq_sb, dk_sb, dv_sb, filled = _sb_bwd(sv["proj"], sv["o_sb"], dmix, "sb_bwd", send_own)
        if filled:
            received = received[:1] + filled
        send_prev = None
        if grads_ready is not None:
            send_prev = dict(arrays=grads_ready, scatter=True, recv=received, slot=l + 1)
        dqn, dkn, dv_fox, dlogf, filled = _fox_bwd(sv["proj"], sv["qn"], sv["kn"], sv["cfc"], sv["cfr"], sv["cfe"], sv["qkb"],
                                                 sv["o_fox"], sv["lse"], dmix, "fox_bwd", send_prev)
        if filled:
            received = filled
        dq_fox, dk_fox, dfl, dgq, dgk, dbf = _fox_prep_bwd(sv["proj"], sv["gq"], sv["gk"], sv["bf"], dqn, dkn, dlogf,
                                                           "fox_prep_bwd")
        duc, dvc, dsw, dsb, dsg = _sgu_bwd(sv["proj"], sgu_w[l], sv["gs"], sv["bexp"], dmix, "sgu_bwd")
        dproj = jnp.concatenate(
            [dq_fox, dk_fox, dv_fox.astype(_BF), dq_sb.astype(_BF), dk_sb.astype(_BF), dv_sb.astype(_BF), duc, dvc, dfl,
             jnp.zeros((dfl.shape[0], IN_P - C_FL - 128), _BF)], axis=-1)
        dh1 = _mm_nt(dproj, w_in_l, "plain", (), F32, "mm_in_bwd")
        g_in = _w_in_unlayout(_mm_grad(sv["h1_t"], dproj, None, "mm_in_grad"))
        g_in = jnp.transpose(g_in.reshape(D_MODEL, N_DEV, IN_W // N_DEV), (1, 0, 2))
        grads_ready = [g_in, g_out, g_w1, g_w2]
        small["mod"][l] = [dg1, dsh2, dsc2, dg2]
        below = (saved[l - 1]["y2"], mods[l - 1, 5]) if l > 0 else None
        dx, dsh1, dsc1, dn1, *below_grads = _norm_mod_bwd(sv["x_in"], dh1, dx_mid, sv["n1"], sc1, below, "norm_mod_bwd")
        if below_grads:
            dy2, dg2 = below_grads
        small["mod"][l] = jnp.concatenate([dsh1, dsc1] + small["mod"][l], axis=-1).reshape(-1)
        small["norm1_g"][l] = dn1.reshape(-1)
        small["norm2_g"][l] = dn2.reshape(-1)
        small["sgu_w"][l] = dsw
        small["sgu_b"][l] = dsb.reshape(SGU_CHUNK, SGU_GROUPS, HEAD_DIM).sum(-1).T
        small["sgu_norm_g"][l] = dsg.reshape(SGU_GROUPS, HEAD_DIM)
        small["q_norm_g"][l] = dgq.reshape(FOX_HEADS, HEAD_DIM).sum(0)
        small["k_norm_g"][l] = dgk.reshape(FOX_HEADS, HEAD_DIM).sum(0)
        small["b_forget"][l] = dbf[0, :FOX_HEADS]

    parts = {k: jnp.stack(v) for k, v in small.items()}
    parts["ada_b"] = parts.pop("mod")
    parts["loss"] = loss_part
    (small_all,) = _exchange([_pack_small(parts)], False, "gather_small")
    zero1 = jnp.zeros((1,), F32)
    packs = [_pack_small(dict(ada_b=a, norm1_g=b, norm2_g=cc, sgu_w=d, sgu_b=e, sgu_norm_g=f, q_norm_g=g, k_norm_g=h,
                              b_forget=i, loss=zero1))
             for a, b, cc, d, e, f, g, h, i in (
                 (ada_b, norm1_g, norm2_g, sgu_w, sgu_b, sgu_norm_g, q_norm_g, k_norm_g, b_forget),
                 (m_ada_b, m_norm1_g, m_norm2_g, m_sgu_w, m_sgu_b, m_sgu_norm_g, m_q_norm_g, m_k_norm_g, m_b_forget),
                 (v_ada_b, v_norm1_g, v_norm2_g, v_sgu_w, v_sgu_b, v_sgu_norm_g, v_q_norm_g, v_k_norm_g, v_b_forget))]
    small_shapes = dict(ada_b=ada_b.shape, norm1_g=norm1_g.shape, norm2_g=norm2_g.shape, sgu_w=sgu_w.shape,
                        sgu_b=sgu_b.shape, sgu_norm_g=sgu_norm_g.shape, q_norm_g=q_norm_g.shape,
                        k_norm_g=k_norm_g.shape, b_forget=b_forget.shape, loss=())
    s_out = [_unpack_small(a, small_shapes) for a in _adamw(small_all, *packs, "adamw_small")]

    dmod_all = small_all[:, :DEPTH * 6 * D_MODEL // 128, :].reshape(N_DEV, DEPTH, 6 * D_MODEL)
    dmod_mine = lax.dynamic_slice_in_dim(dmod_all, me * n_ada, n_ada, axis=2)
    g_ada = _ada_bwd(c_all, jnp.transpose(dmod_mine, (1, 0, 2)), "ada_bwd")
    ada_out = _adamw(g_ada.reshape(1, DEPTH * D_MODEL, n_ada), ada_w.reshape(-1, n_ada), m_ada_w.reshape(-1, n_ada),
                     v_ada_w.reshape(-1, n_ada), "adamw_ada")
    ada_out = [a.reshape(ada_w.shape) for a in ada_out]

    (ri,) = _exchange(grads_ready[:1], True, "scatter_grads", recv=received[:1], slot=0)
    _, ro, r1, r2 = received

    def big(slots, w, m, v, name):
        cdim = w.shape[-1]
        outs = _adamw(slots.reshape(N_DEV, -1, cdim), w.reshape(-1, cdim), m.reshape(-1, cdim), v.reshape(-1, cdim), name)
        return [a.reshape(w.shape) for a in outs]

    in_out = big(ri, w_in, m_w_in, v_w_in, "adamw_in")
    out_out = big(ro, w_out, m_w_out, v_w_out, "adamw_out")
    w1_out = big(r1, mlp_w1, m_mlp_w1, v_mlp_w1, "adamw_w1")
    w2_out = big(r2, mlp_w2, m_mlp_w2, v_mlp_w2, "adamw_w2")

    def leaves(idx):
        sm = s_out[idx]
        return [ada_out[idx], sm["ada_b"], sm["norm1_g"], sm["norm2_g"], in_out[idx], sm["b_forget"], sm["q_norm_g"],
                sm["k_norm_g"], sm["sgu_norm_g"], sm["sgu_w"], sm["sgu_b"], out_out[idx], w1_out[idx], w2_out[idx]]

    loss = s_out[0]["loss"]
    grad_x = dx.reshape(x.shape)
    return (loss, grad_x, *leaves(0), *leaves(1), *leaves(2), *leaves(3))
```

```python
import jax
import jax.numpy as jnp
from jax import lax
from jax.experimental import pallas as pl
from jax.experimental.pallas import tpu as pltpu

F32 = jnp.float32
_BF = jnp.bfloat16
_XBF = jnp.bfloat16

N_DEV = 8
D_MODEL = 1024
DEPTH = 4
HEAD_DIM = 64
SB_HEADS = 4
FOX_HEADS = 8
SGU_GROUPS = 4
SGU_CHUNK = 128
D_FF = 4096
EPS = 1e-6
IN_W = 2824
FL_SRC = 2304
UC_SRC = 2312

C_QB, C_KB, C_VB = 0, 512, 1024
C_QA, C_KA, C_VA = 1536, 1792, 2048
C_UC, C_VC = 2304, 2560
C_FL = 2816
IN_P = 3072
M_SB, M_FOX, M_SGU = 0, 256, 768

ADAM_LR = 0.001
ADAM_B1 = 0.9
ADAM_B2 = 0.999
ADAM_EPS = 1e-08
ADAM_WD = 0.01
ADAM_STEP = 10

ATT_T = 256
ROW_T = 512
MM_T = 1024
GRAD_TS = 4096
ROW_CHUNK = 64
NEG = -1e30
UNDERFLOW = -90.0
NO_BLOCK = 1e30

MESH = pl.DeviceIdType.MESH
ARB = pltpu.ARBITRARY
PAR = pltpu.PARALLEL
HIGHEST = lax.Precision.HIGHEST


def _pcall(body, **kw):
    return pl.pallas_call(body, **kw)


def _cp(*sem):
    return pltpu.CompilerParams(dimension_semantics=tuple(sem))


def _dot(a, b, ca, cb):
    return lax.dot_general(a, b, (((ca,), (cb,)), ((), ())), preferred_element_type=F32)


def _dot_f32(a, b, ca, cb):
    return lax.dot_general(a, b, (((ca,), (cb,)), ((), ())), precision=HIGHEST, preferred_element_type=F32)


def _split(v, parts):
    out = []
    r = v
    for _ in range(parts - 1):
        p = r.astype(_BF)
        out.append(p)
        r = r - p.astype(F32)
    out.append(r.astype(_BF))
    return out


def _dot_exact(v, m, ca, cb, parts, v_left=True):
    out = None
    for p in _split(v, parts):
        term = _dot(p, m, ca, cb) if v_left else _dot(m, p, ca, cb)
        out = term if out is None else out + term
    return out


def _iota2(shape, dim):
    return lax.broadcasted_iota(jnp.int32, shape, dim)


def _my_pos():
    return lax.axis_index("x"), lax.axis_index("y"), lax.axis_index("c")


def _flip(pos, k):
    x, y, c = pos
    px = 1 - x if (k >> 2) & 1 else x
    py = 1 - y if (k >> 1) & 1 else y
    pc = 1 - c if k & 1 else c
    return px, py, pc


def _lin(pos):
    return 4 * pos[0] + 2 * pos[1] + pos[2]


def _exchange_ops(ins, outs, sems, scatter, slot):
    send_sems, recv_sems, local_sems = sems
    n = len(ins)

    def copies(with_recvs):
        me = _my_pos()
        me_i = _lin(me)
        dst = lambda a, j: outs[a].at[j] if slot is None else outs[a].at[j, slot]
        mine = lambda a: ins[a].at[me_i] if scatter else ins[a]
        local = [pltpu.make_async_copy(mine(a), dst(a, me_i), local_sems.at[a]) for a in range(n)]
        sends, recvs = [], []
        for a in range(n):
            for k in range(1, N_DEV):
                peer = _flip(me, k)
                pair = dict(send_sem=send_sems.at[a, k - 1], recv_sem=recv_sems.at[a, k - 1],
                            device_id=peer, device_id_type=MESH)
                src = ins[a].at[_lin(peer)] if scatter else ins[a]
                sends.append(pltpu.make_async_remote_copy(src_ref=src, dst_ref=dst(a, me_i), **pair))
                if with_recvs:
                    recvs.append(pltpu.make_async_remote_copy(src_ref=mine(a), dst_ref=dst(a, _lin(peer)), **pair))
        return local, sends, recvs

    def start():
        local, sends, _ = copies(False)
        for cp in local + sends:
            cp.start()

    def wait():
        local, sends, recvs = copies(True)
        for snd, rcv in zip(sends, recvs):
            snd.wait_send()
            rcv.wait_recv()
        for cp in local:
            cp.wait()

    return start, wait


def _comm_operands(comm):
    arrays, recv = comm["arrays"], comm["recv"]
    n = len(arrays)
    any_spec = pl.BlockSpec(memory_space=pl.ANY)
    if recv is not None:
        out_shapes = [jax.ShapeDtypeStruct(r.shape, r.dtype) for r in recv]
    elif comm["scatter"]:
        out_shapes = [jax.ShapeDtypeStruct(a.shape, a.dtype) for a in arrays]
    else:
        out_shapes = [jax.ShapeDtypeStruct((N_DEV,) + a.shape, a.dtype) for a in arrays]
    operands = list(arrays) + (list(recv) if recv is not None else [])
    sems = [pltpu.SemaphoreType.DMA((n, N_DEV - 1)), pltpu.SemaphoreType.DMA((n, N_DEV - 1)),
            pltpu.SemaphoreType.DMA((n,))]
    return operands, [any_spec] * len(operands), out_shapes, [any_spec] * n, sems


def _with_comm(body, comm, n_in, n_out, first_step, last_step, n_scratch=0):
    if comm is None:
        return body
    n = len(comm["arrays"])
    n_cin = n if comm["recv"] is None else 2 * n

    def wrapped(*refs):
        ins, cins = refs[:n_in], refs[n_in:n_in + n]
        o0 = n_in + n_cin
        outs, couts = refs[o0:o0 + n_out], refs[o0 + n_out:o0 + n_out + n]
        s0 = o0 + n_out + n
        scratch, sems = refs[s0:s0 + n_scratch], refs[s0 + n_scratch:]
        start, wait = _exchange_ops(cins, couts, sems, comm["scatter"], comm["slot"])
        pl.when(first_step())(start)
        body(*ins, *outs, *scratch)
        pl.when(last_step())(wait)

    return wrapped


def _comm_aliases(comm, n_in, n_out):
    if comm is None or comm["recv"] is None:
        return {}
    n = len(comm["arrays"])
    return {n_in + n + a: n_out + a for a in range(n)}


def _exchange(arrays, scatter, name, recv=None, slot=None):
    comm = dict(arrays=arrays, scatter=scatter, recv=recv, slot=slot)
    operands, in_specs, out_shapes, out_specs, sems = _comm_operands(comm)
    n = len(arrays)

    def body(*refs):
        n_cin = len(operands)
        start, wait = _exchange_ops(refs[:n], refs[n_cin:n_cin + n], refs[n_cin + n:], scatter, slot)
        start()
        wait()

    outs = _pcall(body, name=name, out_shape=out_shapes, in_specs=in_specs, out_specs=out_specs, scratch_shapes=sems,
                  input_output_aliases=_comm_aliases(comm, 0, 0))(*operands)
    return list(outs)


def _row_spec(t, w, col=0):
    return pl.BlockSpec((t, w), lambda i: (i, col))


def _vec_spec(w, col=0):
    return pl.BlockSpec((1, w), lambda i: (0, col))


def _norm_mod(x, g, sc, sh, name):
    s, d = x.shape
    t = min(ROW_T, s)

    def body(x_ref, g_ref, sc_ref, sh_ref, h_ref, ht_ref):
        xf = x_ref[...]
        rstd = lax.rsqrt(jnp.mean(xf * xf, axis=-1, keepdims=True) + EPS)
        y = xf * rstd * g_ref[...]
        h = y * (1.0 + sc_ref[...]) + sh_ref[...]
        h_ref[...] = h.astype(h_ref.dtype)
        ht_ref[...] = h.T.astype(ht_ref.dtype)

    return _pcall(
        body, name=name, grid=(s // t,),
        out_shape=[jax.ShapeDtypeStruct((s, d), _BF), jax.ShapeDtypeStruct((d, s), _BF)],
        in_specs=[_row_spec(t, d), _vec_spec(d), _vec_spec(d), _vec_spec(d)],
        out_specs=[_row_spec(t, d), pl.BlockSpec((d, t), lambda i: (0, i))], compiler_params=_cp(PAR),
    )(x, g, sc, sh)


def _res_bwd_step(i, dx, y_ref, gate_ref, dy_ref, dgate_ref):
    dy_ref[...] = (dx * gate_ref[...]).astype(dy_ref.dtype)

    @pl.when(i == 0)
    def _():
        dgate_ref[...] = jnp.zeros_like(dgate_ref)

    dgate_ref[...] += jnp.sum(dx * y_ref[...], axis=0, keepdims=True)


def _norm_mod_bwd(x, dh, dres, g, sc, branch, name):
    s, d = x.shape
    t = min(ROW_T, s)
    n_branch = 0 if branch is None else 2

    ch = min(ROW_CHUNK, t)

    def body(x_ref, dh_ref, dres_ref, g_ref, sc_ref, *rest):
        dx_ref, dsh_ref, dsc_ref, dg_ref = rest[n_branch:n_branch + 4]
        i = pl.program_id(0)
        gv = g_ref[...]
        sc1 = 1.0 + sc_ref[...]
        gain = gv * sc1
        n_sum = 2 if branch is None else 3

        def rows(r, sums):
            sl = pl.ds(pl.multiple_of(r * ch, ch), ch)
            xf = x_ref[sl, :]
            dh = dh_ref[sl, :]
            rstd = lax.rsqrt(jnp.mean(xf * xf, axis=-1, keepdims=True) + EPS)
            xhat = xf * rstd
            dh_xhat = dh * xhat
            dxh = dh * gain
            dx = dres_ref[sl, :] + rstd * (dxh - xhat * jnp.mean(dh_xhat * gain, axis=-1, keepdims=True))
            dx_ref[sl, :] = dx
            new = [sums[0] + jnp.sum(dh, axis=0, keepdims=True), sums[1] + jnp.sum(dh_xhat, axis=0, keepdims=True)]
            if branch is not None:
                rest[n_branch + 4][sl, :] = (dx * rest[1][...]).astype(_BF)
                new.append(sums[2] + jnp.sum(dx * rest[0][sl, :], axis=0, keepdims=True))
            return tuple(new)

        sums = lax.fori_loop(0, t // ch, rows, tuple(jnp.zeros((1, d), F32) for _ in range(n_sum)))
        parts = [sums[0], sums[1] * gv, sums[1] * sc1] + list(sums[2:])
        totals = [dsh_ref, dsc_ref, dg_ref] + ([rest[n_branch + 5]] if branch is not None else [])

        @pl.when(i == 0)
        def _():
            for ref in totals:
                ref[...] = jnp.zeros_like(ref)

        for ref, part in zip(totals, parts):
            ref[...] += part

    vec = jax.ShapeDtypeStruct((1, d), F32)
    out_shape = [jax.ShapeDtypeStruct((s, d), F32), vec, vec, vec]
    in_specs = [_row_spec(t, d), _row_spec(t, d), _row_spec(t, d), _vec_spec(d), _vec_spec(d)]
    out_specs = [_row_spec(t, d), _vec_spec(d), _vec_spec(d), _vec_spec(d)]
    if branch is not None:
        in_specs += [_row_spec(t, d), _vec_spec(d)]
        out_shape += [jax.ShapeDtypeStruct((s, d), _BF), vec]
        out_specs += [_row_spec(t, d), _vec_spec(d)]
    return _pcall(
        body, name=name, grid=(s // t,), out_shape=out_shape, in_specs=in_specs, out_specs=out_specs,
        compiler_params=_cp(ARB),
    )(x, dh, dres, g, sc, *(branch or ()))


def _loss_head(y, target, branch, name):
    s, d = y.shape
    t = min(ROW_T, s)

    def body(y_ref, t_ref, yb_ref, gate_ref, loss_ref, dy_ref, dyb_ref, dgate_ref):
        i = pl.program_id(0)
        diff = y_ref[...] - t_ref[...]
        dy = diff * (1.0 / d)
        dy_ref[...] = dy

        @pl.when(i == 0)
        def _():
            loss_ref[...] = jnp.zeros_like(loss_ref)

        rows = jnp.sum(diff * diff, axis=-1, keepdims=True)
        loss_ref[...] += (0.5 / d) * jnp.sum(rows, axis=0, keepdims=True)
        _res_bwd_step(i, dy, yb_ref, gate_ref, dyb_ref, dgate_ref)

    return _pcall(
        body, name=name, grid=(s // t,),
        out_shape=[jax.ShapeDtypeStruct((1, 1), F32), jax.ShapeDtypeStruct((s, d), F32),
                   jax.ShapeDtypeStruct((s, d), _BF), jax.ShapeDtypeStruct((1, d), F32)],
        in_specs=[_row_spec(t, d), _row_spec(t, d), _row_spec(t, d), _vec_spec(d)],
        out_specs=[pl.BlockSpec((1, 1), lambda i: (0, 0)), _row_spec(t, d), _row_spec(t, d), _vec_spec(d)],
        compiler_params=_cp(ARB),
    )(y, target, *branch)


def _contract_tile(kdim):
    if kdim > MM_T and (kdim // 2) % 128 == 0:
        return kdim // 2
    return min(MM_T, kdim)


def _mm_nn(a, b, epi, extras, name):
    m, kdim = a.shape
    n = b.shape[1]
    tm, tn, tk = min(MM_T, m), min(MM_T, n), _contract_tile(kdim)
    nk = kdim // tk
    n_extra = len(extras)
    n_out = {"plain": 1, "resid": 2, "relu2": 2}[epi]

    def finish(y, extra_refs, out_refs):
        if epi == "plain":
            out_refs[0][...] = y.astype(out_refs[0].dtype)
        elif epi == "resid":
            x_ref, g_ref = extra_refs
            out_refs[0][...] = x_ref[...] + g_ref[...] * y
            out_refs[1][...] = y.astype(out_refs[1].dtype)
        else:
            r = jnp.maximum(y, 0.0)
            out_refs[0][...] = (r * r).astype(out_refs[0].dtype)
            out_refs[1][...] = (r * r).T.astype(out_refs[1].dtype)

    def body(a_ref, b_ref, *rest):
        extra_refs = rest[:n_extra]
        out_refs = rest[n_extra:n_extra + n_out]
        part = _dot(a_ref[...].astype(_BF), b_ref[...].astype(_BF), 1, 0)
        if nk == 1:
            finish(part, extra_refs, out_refs)
        else:
            acc_ref = rest[-1]
            k = pl.program_id(2)

            @pl.when(k == 0)
            def _():
                acc_ref[...] = part

            @pl.when(k > 0)
            def _():
                acc_ref[...] += part

            @pl.when(k == nk - 1)
            def _():
                finish(acc_ref[...], extra_refs, out_refs)

    tile = pl.BlockSpec((tm, tn), lambda i, j, k: (i, j))
    in_specs = [pl.BlockSpec((tm, tk), lambda i, j, k: (i, k)), pl.BlockSpec((tk, tn), lambda i, j, k: (k, j))]
    if epi == "plain":
        out_shape = [jax.ShapeDtypeStruct((m, n), F32)]
    elif epi == "resid":
        in_specs += [tile, pl.BlockSpec((1, tn), lambda i, j, k: (0, j))]
        out_shape = [jax.ShapeDtypeStruct((m, n), F32), jax.ShapeDtypeStruct((m, n), _BF)]
    out_specs = [tile] * n_out
    if epi == "relu2":
        out_shape = [jax.ShapeDtypeStruct((m, n), _BF), jax.ShapeDtypeStruct((n, m), _BF)]
        out_specs[1] = pl.BlockSpec((tn, tm), lambda i, j, k: (j, i))
    outs = _pcall(
        body, name=name, grid=(m // tm, n // tn, nk), out_shape=out_shape,
        in_specs=in_specs, out_specs=out_specs,
        scratch_shapes=[pltpu.VMEM((tm, tn), F32)] if nk > 1 else [],
        compiler_params=_cp(PAR, PAR, ARB),
    )(a, b, *extras)
    return outs[0] if n_out == 1 else outs


def _mm_out(parts, w, x, gate, name):
    s, n = x.shape
    widths = [p.shape[1] for p in parts]
    kdim = sum(widths)
    tm = min(ROW_T, s)

    def body(a0_ref, a1_ref, a2_ref, w_ref, x_ref, g_ref, xo_ref, y_ref, mt_ref):
        mixed = jnp.concatenate([a0_ref[...], a1_ref[...], a2_ref[...]], axis=-1)
        y = _dot(mixed.astype(_BF), w_ref[...], 1, 0)
        xo_ref[...] = x_ref[...] + g_ref[...] * y
        y_ref[...] = y.astype(y_ref.dtype)
        mt_ref[...] = mixed.T.astype(mt_ref.dtype)

    row = lambda width: pl.BlockSpec((tm, width), lambda i: (i, 0))
    return _pcall(
        body, name=name, grid=(s // tm,),
        out_shape=[jax.ShapeDtypeStruct((s, n), F32), jax.ShapeDtypeStruct((s, n), _BF),
                   jax.ShapeDtypeStruct((kdim, s), _BF)],
        in_specs=[row(widths[0]), row(widths[1]), row(widths[2]), pl.BlockSpec((kdim, n), lambda i: (0, 0)),
                  row(n), pl.BlockSpec((1, n), lambda i: (0, 0))],
        out_specs=[row(n), row(n), pl.BlockSpec((kdim, tm), lambda i: (0, i))], compiler_params=_cp(PAR),
    )(*parts, w, x, gate)


def _mm_nt(a, b, epi, extras, out_dtype, name):
    m, kdim = a.shape
    n = b.shape[0]
    tm, tn, tk = min(MM_T, m), min(MM_T, n), _contract_tile(kdim)
    nk = kdim // tk
    n_extra = len(extras)

    def finish(y, extra_refs, o_ref):
        if epi == "mul2":
            y = y * (2.0 * jnp.sqrt(extra_refs[0][...].astype(F32)))
        o_ref[...] = y.astype(o_ref.dtype)

    def body(a_ref, b_ref, *rest):
        extra_refs = rest[:n_extra]
        o_ref = rest[n_extra]
        part = _dot(a_ref[...].astype(_BF), b_ref[...].astype(_BF), 1, 1)
        if nk == 1:
            finish(part, extra_refs, o_ref)
        else:
            acc_ref = rest[-1]
            k = pl.program_id(2)

            @pl.when(k == 0)
            def _():
                acc_ref[...] = part

            @pl.when(k > 0)
            def _():
                acc_ref[...] += part

            @pl.when(k == nk - 1)
            def _():
                finish(acc_ref[...], extra_refs, o_ref)

    tile = pl.BlockSpec((tm, tn), lambda i, j, k: (i, j))
    in_specs = [pl.BlockSpec((tm, tk), lambda i, j, k: (i, k)), pl.BlockSpec((tn, tk), lambda i, j, k: (j, k))]
    in_specs += [tile] * n_extra
    return _pcall(
        body, name=name, grid=(m // tm, n // tn, nk), out_shape=jax.ShapeDtypeStruct((m, n), out_dtype),
        in_specs=in_specs, out_specs=tile,
        scratch_shapes=[pltpu.VMEM((tm, tn), F32)] if nk > 1 else [],
        compiler_params=_cp(PAR, PAR, ARB),
    )(a, b, *extras)


def _mm_grad(at, b, split, name):
    m, s = at.shape
    n = b.shape[1]
    ts = min(GRAD_TS, s)
    ns = s // ts
    if split == "rows":
        tm, tn = min(MM_T, m // N_DEV), min(MM_T, n)
        per = (m // N_DEV) // tm
        out_shape = (N_DEV, m // N_DEV, n)
        out_spec = pl.BlockSpec((None, tm, tn), lambda i, j, k: (i // per, i % per, j))
    elif split == "cols":
        tm, tn = min(MM_T, m), min(MM_T, n // N_DEV)
        per = (n // N_DEV) // tn
        out_shape = (N_DEV, m, n // N_DEV)
        out_spec = pl.BlockSpec((None, tm, tn), lambda i, j, k: (j // per, i, j % per))
    else:
        tm, tn = min(MM_T, m), min(MM_T, n)
        out_shape = (m, n)
        out_spec = pl.BlockSpec((tm, tn), lambda i, j, k: (i, j))

    def body(a_ref, b_ref, o_ref, acc_ref):
        k = pl.program_id(2)
        part = _dot(a_ref[...].astype(_BF), b_ref[...].astype(_BF), 1, 0)

        @pl.when(k == 0)
        def _():
            acc_ref[...] = part

        @pl.when(k > 0)
        def _():
            acc_ref[...] += part

        @pl.when(k == ns - 1)
        def _():
            o_ref[...] = acc_ref[...].astype(o_ref.dtype)

    return _pcall(
        body, name=name, grid=(m // tm, n // tn, ns), out_shape=jax.ShapeDtypeStruct(out_shape, _XBF),
        in_specs=[pl.BlockSpec((tm, ts), lambda i, j, k: (i, k)), pl.BlockSpec((ts, tn), lambda i, j, k: (k, j))],
        out_specs=out_spec, scratch_shapes=[pltpu.VMEM((tm, tn), F32)],
        compiler_params=_cp(PAR, PAR, ARB),
    )(at, b)


def _head_masks():
    lane = _iota2((1, 128), 1)
    return [(lane // HEAD_DIM) == hh for hh in range(2)]


def _untranspose(src_ref, dst_ref, t, nblk):
    def rows(c, _):
        off = pl.multiple_of(c * t, t)
        dst_ref[pl.ds(off, t), :] = src_ref[:, pl.ds(off, t)].T
        return 0

    lax.fori_loop(0, nblk, rows, 0)


def _softplus_parts(z):
    sp = jnp.maximum(z, 0.0) + jnp.log(1.0 + jnp.exp(-jnp.abs(z)))
    return -sp, z - sp


def _sb_key_loop(i, carry, step, lookahead, first_ahead):
    def cond(c):
        return jnp.logical_and(c[0] <= i, jnp.max(jnp.maximum(c[1][0][0], c[1][0][1])) > UNDERFLOW)

    def body(c):
        ahead = lookahead(i - c[0] - 1)
        return c[0] + 1, step(i - c[0], c[1], c[2]), ahead

    return lax.while_loop(cond, body, (jnp.int32(1), carry, first_ahead))[1]


def _sb_fwd(proj, name, comm=None):
    s = proj.shape[0]
    t = min(ATT_T, s)
    nblk = s // t
    scale = HEAD_DIM ** -0.5

    def body(q_ref, k_ref, v_ref, o_ref):
        i = pl.program_id(1)
        masks = _head_masks()
        row = _iota2((t, t), 0)
        col = _iota2((t, t), 1)
        u_strict = (row > col).astype(_BF)
        causal = col < row
        qs = q_ref[...] * scale
        qm = [jnp.where(masks[hh], qs, 0.0).astype(_BF) for hh in range(2)]

        def scores(j):
            kb = k_ref[pl.ds(pl.multiple_of(jnp.maximum(j, 0) * t, t), t), :].astype(_BF)
            return tuple(_dot(qm[hh], kb, 1, 1) for hh in range(2))

        def block(j, carry, qk, diag):
            off = pl.multiple_of(j * t, t)
            vb = v_ref[pl.ds(off, t), :].astype(_BF)
            runs, outs = [], []
            for hh in range(2):
                run, o = carry[0][hh], carry[1][hh]
                z = qk[hh]
                l1, lb = _softplus_parts(z)
                if diag:
                    l1 = jnp.where(causal, l1, 0.0)
                between = run + _dot_exact(l1, u_strict, 1, 0, 2)
                a = jnp.exp(lb + between)
                if diag:
                    a = jnp.where(causal, a, 0.0)
                outs.append(o + _dot_exact(a, vb, 1, 0, 2))
                runs.append(run + jnp.sum(l1, axis=-1, keepdims=True))
            return tuple(runs), tuple(outs)

        zero, zero_o = jnp.zeros((t, 1), F32), jnp.zeros((t, 128), F32)
        ahead = scores(i - 1)
        carry = block(i, ((zero, zero), (zero_o, zero_o)), scores(i), True)
        carry = _sb_key_loop(i, carry, lambda j, c, qk: block(j, c, qk, False), scores, ahead)
        o_ref[...] = jnp.where(masks[0], carry[1][0], carry[1][1])

    kv_spec = lambda c0: pl.BlockSpec((s, 128), lambda p, i: (0, c0 // 128 + p))
    c_operands, c_in_specs, c_out_shapes, c_out_specs, c_sems = _comm_operands(comm) if comm else ([], [], [], [], [])
    outs = _pcall(
        _with_comm(body, comm, 3, 1, *_grid_ends(SB_HEADS // 2, nblk)), name=name, grid=(SB_HEADS // 2, nblk),
        out_shape=[jax.ShapeDtypeStruct((s, 256), F32)] + c_out_shapes,
        in_specs=[pl.BlockSpec((t, 128), lambda p, i: (i, C_QA // 128 + p)), kv_spec(C_KA), kv_spec(C_VA)] + c_in_specs,
        out_specs=[pl.BlockSpec((t, 128), lambda p, i: (i, p))] + c_out_specs,
        scratch_shapes=c_sems, input_output_aliases=_comm_aliases(comm, 3, 1), compiler_params=_cp(ARB, ARB),
    )(proj, proj, proj, *c_operands)
    return outs[0], list(outs[1:])


def _sb_bwd(proj, o, dmix, name, comm=None):
    s = proj.shape[0]
    t = min(ATT_T, s)
    nblk = s // t
    scale = HEAD_DIM ** -0.5

    def body(q_ref, k_ref, v_ref, o_ref, do_ref, dq_ref, dk_ref, dv_ref, dkt_ref, dvt_ref):
        i = pl.program_id(1)
        masks = _head_masks()
        row = _iota2((t, t), 0)
        col = _iota2((t, t), 1)
        u_strict = (row > col).astype(_BF)
        u_incl = (row >= col).astype(_BF)
        causal = col < row

        @pl.when(i == 0)
        def _():
            dkt_ref[...] = jnp.zeros_like(dkt_ref)
            dvt_ref[...] = jnp.zeros_like(dvt_ref)

        qs = q_ref[...] * scale
        dov = do_ref[...]
        ov = o_ref[...]
        qm_f = [jnp.where(masks[hh], qs, 0.0) for hh in range(2)]
        dom_f = [jnp.where(masks[hh], dov, 0.0) for hh in range(2)]
        qm = [qm_f[hh].astype(_BF) for hh in range(2)]
        dom = [dom_f[hh].astype(_BF) for hh in range(2)]
        qm_t = [qm_f[hh].T.astype(_BF) for hh in range(2)]
        dom_t = [dom_f[hh].T.astype(_BF) for hh in range(2)]
        dsum = [jnp.sum(dom[hh].astype(F32) * ov, axis=-1, keepdims=True) for hh in range(2)]

        def scores(j):
            off = pl.multiple_of(jnp.maximum(j, 0) * t, t)
            kb = k_ref[pl.ds(off, t), :].astype(_BF)
            vb = v_ref[pl.ds(off, t), :].astype(_BF)
            return tuple((_dot(qm[hh], kb, 1, 1), _dot(dom[hh], vb, 1, 1)) for hh in range(2))

        def block(j, carry, ahead, diag):
            off = pl.multiple_of(j * t, t)
            kb = k_ref[pl.ds(off, t), :].astype(_BF)
            runs, eruns, dqs = [], [], []
            dk = dv = None
            for hh in range(2):
                run, erun, dq = carry[0][hh], carry[1][hh], carry[2][hh]
                z, da = ahead[hh]
                l1, lb = _softplus_parts(z)
                if diag:
                    l1 = jnp.where(causal, l1, 0.0)
                between = run + _dot_exact(l1, u_strict, 1, 0, 2)
                a = jnp.exp(lb + between)
                if diag:
                    a = jnp.where(causal, a, 0.0)
                g = a * da
                cum = dsum[hh] - (erun + _dot_exact(g, u_incl, 1, 0, 2))
                beta = jnp.exp(lb)
                dz = g * (1.0 - beta) - cum * beta
                if diag:
                    dz = jnp.where(causal, dz, 0.0)
                dzb = dz.astype(_BF)
                dqs.append(dq + _dot(dzb, kb, 1, 0))
                dk_h = _dot(qm_t[hh], dzb, 1, 0)
                dv_h = _dot(dom_t[hh], a.astype(_BF), 1, 0)
                dk = dk_h if dk is None else dk + dk_h
                dv = dv_h if dv is None else dv + dv_h
                runs.append(run + jnp.sum(l1, axis=-1, keepdims=True))
                eruns.append(erun + jnp.sum(g, axis=-1, keepdims=True))
            dkt_ref[:, pl.ds(off, t)] += dk
            dvt_ref[:, pl.ds(off, t)] += dv
            return tuple(runs), tuple(eruns), tuple(dqs)

        zero, zero_q = jnp.zeros((t, 1), F32), jnp.zeros((t, 128), F32)
        ahead = scores(i - 1)
        carry = block(i, ((zero, zero), (zero, zero), (zero_q, zero_q)), scores(i), True)
        carry = _sb_key_loop(i, carry, lambda j, c, a: block(j, c, a, False), scores, ahead)
        dq_ref[...] = jnp.where(masks[0], carry[2][0], carry[2][1]) * scale

        @pl.when(i == nblk - 1)
        def _():
            _untranspose(dkt_ref, dk_ref, t, nblk)
            _untranspose(dvt_ref, dv_ref, t, nblk)

    kv_spec = lambda c0: pl.BlockSpec((s, 128), lambda p, i: (0, c0 // 128 + p))
    blk = lambda c0: pl.BlockSpec((t, 128), lambda p, i: (i, c0 // 128 + p))
    acc = pl.BlockSpec((s, 128), lambda p, i: (0, p))
    shp = jax.ShapeDtypeStruct((s, 256), F32)
    c_operands, c_in_specs, c_out_shapes, c_out_specs, c_sems = _comm_operands(comm) if comm else ([], [], [], [], [])
    outs = _pcall(
        _with_comm(body, comm, 5, 3, *_grid_ends(SB_HEADS // 2, nblk), n_scratch=2), name=name,
        grid=(SB_HEADS // 2, nblk), out_shape=[shp, shp, shp] + c_out_shapes,
        in_specs=[blk(C_QA), kv_spec(C_KA), kv_spec(C_VA), blk(0), blk(M_SB)] + c_in_specs,
        out_specs=[blk(0), acc, acc] + c_out_specs,
        scratch_shapes=[pltpu.VMEM((128, s), F32), pltpu.VMEM((128, s), F32)] + c_sems,
        input_output_aliases=_comm_aliases(comm, 5, 3), compiler_params=_cp(ARB, ARB),
    )(proj, proj, proj, o, dmix, *c_operands)
    return outs[0], outs[1], outs[2], list(outs[3:])


def _group_matrix(width):
    r = _iota2((width, width), 0) // HEAD_DIM
    c = _iota2((width, width), 1) // HEAD_DIM
    return (r == c).astype(_BF)


def _group_mean(v, gm):
    return _dot_exact(v, gm, 1, 0, 2) * (1.0 / HEAD_DIM)


def _fox_prep(proj, gq, gk, bf, name):
    s = proj.shape[0]
    t = min(ATT_T, s)
    nblk = s // t
    assert nblk <= 128

    def body(q_ref, k_ref, fl_ref, gq_ref, gk_ref, bf_ref, qn_ref, kn_ref, cfc_ref, cfr_ref, cfe_ref, carry_ref):
        i = pl.program_id(0)
        gm = _group_matrix(512)
        for src, gref, dst in ((q_ref, gq_ref, qn_ref), (k_ref, gk_ref, kn_ref)):
            v = src[...]
            rstd = lax.rsqrt(_group_mean(v * v, gm) + EPS)
            dst[...] = (v * rstd * gref[...]).astype(dst.dtype)

        @pl.when(i == 0)
        def _():
            carry_ref[...] = jnp.zeros_like(carry_ref)
            cfe_ref[...] = jnp.full(cfe_ref.shape, NO_BLOCK, F32)

        lane = _iota2((1, 128), 1)
        logf = jnp.where(lane < FOX_HEADS, _softplus_parts(-(fl_ref[...] + bf_ref[...]))[0], 0.0)
        lower = (_iota2((t, t), 0) >= _iota2((t, t), 1)).astype(_BF)
        cf = carry_ref[...] + _dot_exact(logf, lower, 1, 0, 3, v_left=False)
        cfc_ref[...] = cf
        rows = cf.T[:FOX_HEADS, :]
        cfr_ref[...] = rows
        cfe_ref[...] = jnp.where(lane == i, rows[:, t - 1:t], cfe_ref[...])
        carry_ref[...] = cf[t - 1:t, :]

    col = lambda w, c0: pl.BlockSpec((t, w), lambda i: (i, c0 // w))
    v512 = pl.BlockSpec((1, 512), lambda i: (0, 0))
    return _pcall(
        body, name=name, grid=(nblk,),
        out_shape=[jax.ShapeDtypeStruct((s, 512), _BF), jax.ShapeDtypeStruct((s, 512), _BF),
                   jax.ShapeDtypeStruct((s, 128), F32), jax.ShapeDtypeStruct((FOX_HEADS, s), F32),
                   jax.ShapeDtypeStruct((FOX_HEADS, 128), F32)],
        in_specs=[col(512, C_QB), col(512, C_KB), col(128, C_FL), v512, v512, pl.BlockSpec((1, 128), lambda i: (0, 0))],
        out_specs=[col(512, 0), col(512, 0), col(128, 0), pl.BlockSpec((FOX_HEADS, t), lambda i: (0, i)),
                   pl.BlockSpec((FOX_HEADS, 128), lambda i: (0, 0))],
        scratch_shapes=[pltpu.VMEM((1, 128), F32)], compiler_params=_cp(ARB),
    )(proj, proj, proj, gq, gk, bf)


def _fox_cq(cfc_blk, head):
    lane = _iota2((1, 128), 1)
    return jnp.sum(jnp.where(lane == head, cfc_blk, 0.0), axis=-1, keepdims=True)


def _fox_ck(cfr_ref, off, t, head):
    sub = _iota2((FOX_HEADS, 1), 0)
    return jnp.sum(jnp.where(sub == head, cfr_ref[:, pl.ds(off, t)], 0.0), axis=0, keepdims=True)


def _fox_blocks_needed(top, cfe_ref, head, i):
    sub = _iota2((FOX_HEADS, 1), 0)
    lane = _iota2((1, 128), 1)
    last_key = jnp.sum(jnp.where(sub == head, cfe_ref[...], 0.0), axis=0, keepdims=True)
    need = jnp.logical_and(lane < i, top - last_key > UNDERFLOW)
    return jnp.sum(need.astype(jnp.int32))


def _grid_ends(n_outer, n_inner):
    first = lambda: jnp.logical_and(pl.program_id(0) == 0, pl.program_id(1) == 0)
    last = lambda: jnp.logical_and(pl.program_id(0) == n_outer - 1, pl.program_id(1) == n_inner - 1)
    return first, last


def _fox_fwd(proj, qn, kn, cfc, cfr, cfe, qk_bound, name, comm=None):
    s = proj.shape[0]
    t = min(ATT_T, s)
    nblk = s // t
    scale = HEAD_DIM ** -0.5

    def body(q_ref, k_ref, v_ref, cfc_ref, cfr_ref, cfe_ref, qkb_ref, o_ref, lse_ref):
        p = pl.program_id(0)
        i = pl.program_id(1)
        masks = _head_masks()
        lane = _iota2((1, 128), 1)
        valid = _iota2((t, t), 1) <= _iota2((t, t), 0)
        qv = q_ref[...]
        cfc_blk = cfc_ref[...]
        heads = [2 * p + hh for hh in range(2)]
        qm = [(jnp.where(masks[hh], qv, 0) * scale).astype(_BF) for hh in range(2)]
        cq = [_fox_cq(cfc_blk, heads[hh]) for hh in range(2)]

        def scores(j):
            kb = k_ref[pl.ds(pl.multiple_of(jnp.maximum(j, 0) * t, t), t), :]
            return tuple(_dot(qm[hh], kb, 1, 1) for hh in range(2))

        def block(j, carry, qk, diag):
            off = pl.multiple_of(j * t, t)
            vb = v_ref[pl.ds(off, t), :].astype(_BF)
            ms, ls, accs = [], [], []
            for hh in range(2):
                m, l, acc = carry[0][hh], carry[1][hh], carry[2][hh]
                z = qk[hh] + (cq[hh] - _fox_ck(cfr_ref, off, t, heads[hh]))
                if diag:
                    z = jnp.where(valid, z, NEG)
                m_new = jnp.maximum(m, jnp.max(z, axis=-1, keepdims=True))
                pe = jnp.exp(z - m_new)
                alpha = jnp.exp(m - m_new)
                ms.append(m_new)
                ls.append(alpha * l + jnp.sum(pe, axis=-1, keepdims=True))
                accs.append(alpha * acc + _dot(pe.astype(_BF), vb, 1, 0))
            return tuple(ms), tuple(ls), tuple(accs)

        def step(jj, c):
            ahead = scores(i - jj - 2)
            return block(i - jj - 1, c[0], c[1], False), ahead

        neg, zero, zero_o = jnp.full((t, 1), NEG, F32), jnp.zeros((t, 1), F32), jnp.zeros((t, 128), F32)
        ahead = scores(i - 1)
        carry = block(i, ((neg, neg), (zero, zero), (zero_o, zero_o)), scores(i), True)
        needed = [_fox_blocks_needed(jnp.max(qkb_ref[...] + cq[hh] - carry[0][hh]), cfe_ref, heads[hh], i)
                  for hh in range(2)]
        m, l, acc = lax.fori_loop(0, jnp.maximum(needed[0], needed[1]), step, (carry, ahead))[0]
        o_ref[...] = jnp.where(masks[0], acc[0] / l[0], acc[1] / l[1])
        lse_ref[...] = jnp.where(lane == 0, m[0] + jnp.log(l[0]), jnp.where(lane == 1, m[1] + jnp.log(l[1]), 0.0))

    c_operands, c_in_specs, c_out_shapes, c_out_specs, c_sems = _comm_operands(comm) if comm else ([], [], [], [], [])
    outs = _pcall(
        _with_comm(body, comm, 7, 2, *_grid_ends(FOX_HEADS // 2, nblk)), name=name, grid=(FOX_HEADS // 2, nblk),
        out_shape=[jax.ShapeDtypeStruct((s, 512), F32), jax.ShapeDtypeStruct((FOX_HEADS // 2, s, 128), F32)] + c_out_shapes,
        in_specs=[pl.BlockSpec((t, 128), lambda p, i: (i, p)), pl.BlockSpec((s, 128), lambda p, i: (0, p)),
                  pl.BlockSpec((s, 128), lambda p, i: (0, C_VB // 128 + p)),
                  pl.BlockSpec((t, 128), lambda p, i: (i, 0)), pl.BlockSpec((FOX_HEADS, s), lambda p, i: (0, 0)),
                  pl.BlockSpec((FOX_HEADS, 128), lambda p, i: (0, 0)), pl.BlockSpec((1, 1), lambda p, i: (0, 0))] + c_in_specs,
        out_specs=[pl.BlockSpec((t, 128), lambda p, i: (i, p)),
                   pl.BlockSpec((None, t, 128), lambda p, i: (p, i, 0))] + c_out_specs,
        scratch_shapes=c_sems, input_output_aliases=_comm_aliases(comm, 7, 2),
        compiler_params=_cp(ARB, ARB),
    )(qn, kn, proj, cfc, cfr, cfe, qk_bound, *c_operands)
    return outs[0], outs[1], list(outs[2:])


def _fox_bwd(proj, qn, kn, cfc, cfr, cfe, qk_bound, o, lse, dmix, name, comm=None):
    s = proj.shape[0]
    t = min(ATT_T, s)
    nblk = s // t
    scale = HEAD_DIM ** -0.5

    def body(q_ref, k_ref, v_ref, cfc_ref, cfr_ref, cfe_ref, qkb_ref, o_ref, lse_ref, do_ref,
             dq_ref, dk_ref, dv_ref, dcr_ref, dkt_ref, dvt_ref):
        p = pl.program_id(0)
        i = pl.program_id(1)
        masks = _head_masks()
        lane = _iota2((1, 128), 1)
        valid = _iota2((t, t), 1) <= _iota2((t, t), 0)
        before = (_iota2((t, t), 0) < _iota2((t, t), 1)).astype(_BF)

        @pl.when(i == 0)
        def _():
            dkt_ref[...] = jnp.zeros_like(dkt_ref)
            dvt_ref[...] = jnp.zeros_like(dvt_ref)

        @pl.when(jnp.logical_and(p == 0, i == 0))
        def _():
            dcr_ref[...] = jnp.zeros_like(dcr_ref)

        head_row = _iota2((FOX_HEADS, 1), 0)
        qv = q_ref[...]
        cfc_blk = cfc_ref[...]
        dov = do_ref[...]
        ov = o_ref[...]
        lse_blk = lse_ref[...]
        heads = [2 * p + hh for hh in range(2)]
        qm = [(jnp.where(masks[hh], qv, 0) * scale).astype(_BF) for hh in range(2)]
        dom = [jnp.where(masks[hh], dov, 0.0).astype(_BF) for hh in range(2)]
        qm_t = [qm[hh].astype(F32).T.astype(_BF) for hh in range(2)]
        dom_t = [dom[hh].astype(F32).T.astype(_BF) for hh in range(2)]
        dsum = [jnp.sum(dom[hh].astype(F32) * ov, axis=-1, keepdims=True) for hh in range(2)]
        lse_h = [jnp.sum(jnp.where(lane == hh, lse_blk, 0.0), axis=-1, keepdims=True) for hh in range(2)]
        cq = [_fox_cq(cfc_blk, heads[hh]) for hh in range(2)]
        needed = [_fox_blocks_needed(jnp.max(qkb_ref[...] + cq[hh] - lse_h[hh]), cfe_ref, heads[hh], i)
                  for hh in range(2)]
        first = i - jnp.maximum(needed[0], needed[1])

        def scores(j):
            off = pl.multiple_of(j * t, t)
            kb = k_ref[pl.ds(off, t), :]
            vb = v_ref[pl.ds(off, t), :].astype(_BF)
            return tuple((_dot(qm[hh], kb, 1, 1), _dot(dom[hh], vb, 1, 1)) for hh in range(2))

        def block(j, carry, ahead, diag):
            off = pl.multiple_of(j * t, t)
            kb = k_ref[pl.ds(off, t), :]
            dqs, pres = [], []
            dk = dv = None
            for hh in range(2):
                dq, pre = carry[0][hh], carry[1][hh]
                z = ahead[hh][0] + (cq[hh] - _fox_ck(cfr_ref, off, t, heads[hh]))
                pm = jnp.exp(z - lse_h[hh])
                if diag:
                    pm = jnp.where(valid, pm, 0.0)
                ds = pm * (ahead[hh][1] - dsum[hh])
                dsb = ds.astype(_BF)
                dqs.append(dq + _dot(dsb, kb, 1, 0))
                dk_h = _dot(qm_t[hh], dsb, 1, 0)
                dv_h = _dot(dom_t[hh], pm.astype(_BF), 1, 0)
                dk = dk_h if dk is None else dk + dk_h
                dv = dv_h if dv is None else dv + dv_h
                if diag:
                    lower_keys = jnp.where(valid, pre + _dot_exact(ds, before, 1, 0, 2), 0.0)
                    dlogf = jnp.sum(lower_keys, axis=0, keepdims=True)
                else:
                    dlogf = jnp.sum(pre) + _dot_exact(jnp.sum(ds, axis=0, keepdims=True), before, 1, 0, 3)
                dcr_ref[:, pl.ds(off, t)] += jnp.where(head_row == heads[hh], dlogf, 0.0)
                pres.append(pre + jnp.sum(ds, axis=-1, keepdims=True))
            dkt_ref[:, pl.ds(off, t)] += dk
            dvt_ref[:, pl.ds(off, t)] += dv
            return tuple(dqs), tuple(pres)

        zero, zero_q = jnp.zeros((t, 1), F32), jnp.zeros((t, 128), F32)

        def step(j, c):
            ahead = scores(j + 1)
            return block(j, c[0], c[1], False), ahead

        carry, ahead = lax.fori_loop(first, i, step, (((zero_q, zero_q), (zero, zero)), scores(first)))
        dqs, _ = block(i, carry, ahead, True)
        dq_ref[...] = jnp.where(masks[0], dqs[0], dqs[1]) * scale

        @pl.when(i == nblk - 1)
        def _():
            _untranspose(dkt_ref, dk_ref, t, nblk)
            _untranspose(dvt_ref, dv_ref, t, nblk)

    blk = lambda c0: pl.BlockSpec((t, 128), lambda p, i: (i, c0 // 128 + p))
    res = lambda c0: pl.BlockSpec((s, 128), lambda p, i: (0, c0 // 128 + p))
    shp = jax.ShapeDtypeStruct((s, 512), F32)
    c_operands, c_in_specs, c_out_shapes, c_out_specs, c_sems = _comm_operands(comm) if comm else ([], [], [], [], [])
    transposed = [pltpu.VMEM((128, s), F32), pltpu.VMEM((128, s), F32)]
    outs = _pcall(
        _with_comm(body, comm, 10, 4, *_grid_ends(FOX_HEADS // 2, nblk), n_scratch=2), name=name,
        grid=(FOX_HEADS // 2, nblk),
        out_shape=[shp, shp, shp, jax.ShapeDtypeStruct((FOX_HEADS, s), F32)] + c_out_shapes,
        in_specs=[blk(0), res(0), res(C_VB), pl.BlockSpec((t, 128), lambda p, i: (i, 0)),
                  pl.BlockSpec((FOX_HEADS, s), lambda p, i: (0, 0)), pl.BlockSpec((FOX_HEADS, 128), lambda p, i: (0, 0)),
                  pl.BlockSpec((1, 1), lambda p, i: (0, 0)), blk(0),
                  pl.BlockSpec((None, t, 128), lambda p, i: (p, i, 0)), blk(M_FOX)] + c_in_specs,
        out_specs=[blk(0), res(0), res(0), pl.BlockSpec((FOX_HEADS, s), lambda p, i: (0, 0))] + c_out_specs,
        scratch_shapes=transposed + c_sems, input_output_aliases=_comm_aliases(comm, 10, 4),
        compiler_params=_cp(ARB, ARB),
    )(qn, kn, proj, cfc, cfr, cfe, qk_bound, o, lse, dmix, *c_operands)
    return outs[0], outs[1], outs[2], outs[3], list(outs[4:])


def _fox_prep_bwd(proj, gq, gk, bf, dqn, dkn, dlogf, name):
    s = proj.shape[0]
    t = min(ATT_T, s)
    nblk = s // t

    def body(q_ref, k_ref, fl_ref, gq_ref, gk_ref, bf_ref, dqn_ref, dkn_ref, dlogf_ref,
             dq_ref, dk_ref, dfl_ref, dgq_ref, dgk_ref, dbf_ref):
        i = pl.program_id(0)
        gm = _group_matrix(512)

        @pl.when(i == 0)
        def _():
            dgq_ref[...] = jnp.zeros_like(dgq_ref)
            dgk_ref[...] = jnp.zeros_like(dgk_ref)
            dbf_ref[...] = jnp.zeros_like(dbf_ref)

        for src, gref, dyref, dst, dgref in ((q_ref, gq_ref, dqn_ref, dq_ref, dgq_ref),
                                              (k_ref, gk_ref, dkn_ref, dk_ref, dgk_ref)):
            v = src[...]
            dy = dyref[...]
            rstd = lax.rsqrt(_group_mean(v * v, gm) + EPS)
            xhat = v * rstd
            dgref[...] += jnp.sum(dy * xhat, axis=0, keepdims=True)
            dxh = dy * gref[...]
            dst[...] = (rstd * (dxh - xhat * _group_mean(dxh * xhat, gm))).astype(dst.dtype)

        to_lanes = (_iota2((FOX_HEADS, 128), 0) == _iota2((FOX_HEADS, 128), 1)).astype(_BF)
        dlogf = _dot_exact(dlogf_ref[...], to_lanes, 0, 0, 3)
        xv = fl_ref[...] + bf_ref[...]
        e = jnp.exp(-jnp.abs(xv))
        dfl = dlogf * (jnp.where(xv >= 0.0, e, 1.0) / (1.0 + e))
        dfl_ref[...] = dfl.astype(dfl_ref.dtype)
        dbf_ref[...] += jnp.sum(dfl, axis=0, keepdims=True)

    col = lambda w, c0: pl.BlockSpec((t, w), lambda i: (i, c0 // w))
    v512 = pl.BlockSpec((1, 512), lambda i: (0, 0))
    v128 = pl.BlockSpec((1, 128), lambda i: (0, 0))
    return _pcall(
        body, name=name, grid=(nblk,),
        out_shape=[jax.ShapeDtypeStruct((s, 512), _BF), jax.ShapeDtypeStruct((s, 512), _BF),
                   jax.ShapeDtypeStruct((s, 128), _BF), jax.ShapeDtypeStruct((1, 512), F32),
                   jax.ShapeDtypeStruct((1, 512), F32), jax.ShapeDtypeStruct((1, 128), F32)],
        in_specs=[col(512, C_QB), col(512, C_KB), col(128, C_FL), v512, v512, v128, col(512, 0), col(512, 0),
                  pl.BlockSpec((FOX_HEADS, t), lambda i: (0, i))],
        out_specs=[col(512, 0), col(512, 0), col(128, 0), v512, v512, v128], compiler_params=_cp(ARB),
    )(proj, proj, proj, gq, gk, bf, dqn, dkn, dlogf)


GELU_C = 0.7978845608028654
GELU_A = 0.044715


def _gelu(x):
    return 0.5 * x * (1.0 + jnp.tanh(GELU_C * (x + GELU_A * x * x * x)))


def _gelu_grad(x):
    th = jnp.tanh(GELU_C * (x + GELU_A * x * x * x))
    return 0.5 * (1.0 + th) + 0.5 * x * (1.0 - th * th) * (GELU_C * (1.0 + 3.0 * GELU_A * x * x))


def _sgu_tril():
    return _iota2((SGU_CHUNK, SGU_CHUNK), 0) >= _iota2((SGU_CHUNK, SGU_CHUNK), 1)


def _sgu_group_masks():
    lane = _iota2((1, 256), 1)
    return [(lane // HEAD_DIM) == g for g in range(SGU_GROUPS)]


def _sgu_fwd(proj, w, gs, bexp, name):
    s = proj.shape[0]
    t = min(2 * SGU_CHUNK, s)

    def body(u_ref, v_ref, w_ref, gs_ref, b_ref, o_ref):
        gm = _group_matrix(256)
        gmask = _sgu_group_masks()
        tril = _sgu_tril()
        u = _gelu(u_ref[...])
        vg = _gelu(v_ref[...])
        vhat = (vg * lax.rsqrt(_group_mean(vg * vg, gm) + EPS) * gs_ref[...]).astype(_BF)
        for ch in range(t // SGU_CHUNK):
            rows = slice(ch * SGU_CHUNK, (ch + 1) * SGU_CHUNK)
            mixed = b_ref[...]
            for g in range(SGU_GROUPS):
                wg = jnp.where(tril, w_ref[g], 0.0).astype(_BF)
                mixed = jnp.where(gmask[g], mixed + _dot(wg, vhat[rows], 1, 0), mixed)
            o_ref[rows, :] = u[rows] * mixed

    return _pcall(
        body, name=name, grid=(s // t,), out_shape=jax.ShapeDtypeStruct((s, 256), F32),
        in_specs=[_row_spec(t, 256, C_UC // 256), _row_spec(t, 256, C_VC // 256),
                  pl.BlockSpec((SGU_GROUPS, SGU_CHUNK, SGU_CHUNK), lambda i: (0, 0, 0)), _vec_spec(256),
                  pl.BlockSpec((SGU_CHUNK, 256), lambda i: (0, 0))],
        out_specs=_row_spec(t, 256), compiler_params=_cp(PAR),
    )(proj, proj, w, gs, bexp)


def _sgu_bwd(proj, w, gs, bexp, dmix, name):
    s = proj.shape[0]
    t = min(2 * SGU_CHUNK, s)
    nstep = s // t

    def body(u_ref, v_ref, w_ref, gs_ref, b_ref, do_ref, du_ref, dv_ref, dw_ref, db_ref, dgs_ref):
        i = pl.program_id(0)
        gm = _group_matrix(256)
        gmask = _sgu_group_masks()
        tril = _sgu_tril()

        @pl.when(i == 0)
        def _():
            dw_ref[...] = jnp.zeros_like(dw_ref)
            db_ref[...] = jnp.zeros_like(db_ref)
            dgs_ref[...] = jnp.zeros_like(dgs_ref)

        uc = u_ref[...]
        vc = v_ref[...]
        u = _gelu(uc)
        vg = _gelu(vc)
        rstd = lax.rsqrt(_group_mean(vg * vg, gm) + EPS)
        xh = vg * rstd
        gsv = gs_ref[...]
        vhat = (xh * gsv).astype(_BF)
        dov = do_ref[...]
        dm = dov * u
        for ch in range(t // SGU_CHUNK):
            rows = slice(ch * SGU_CHUNK, (ch + 1) * SGU_CHUNK)
            mixed = b_ref[...]
            dvh = jnp.zeros((SGU_CHUNK, 256), F32)
            dmc = dm[rows]
            for g in range(SGU_GROUPS):
                wg = jnp.where(tril, w_ref[g], 0.0).astype(_BF)
                mixed = jnp.where(gmask[g], mixed + _dot(wg, vhat[rows], 1, 0), mixed)
                dvh = jnp.where(gmask[g], _dot(wg, dmc.astype(_BF), 0, 0), dvh)
                dw_ref[g] += _dot(jnp.where(gmask[g], dmc, 0.0).astype(_BF), vhat[rows], 1, 1)
            db_ref[...] += dmc
            du_ref[rows, :] = (dov[rows] * mixed * _gelu_grad(uc[rows])).astype(du_ref.dtype)
            xhc = xh[rows]
            dgs_ref[...] += jnp.sum(dvh * xhc, axis=0, keepdims=True)
            dxh = dvh * gsv
            dvg = rstd[rows] * (dxh - xhc * _group_mean(dxh * xhc, gm))
            dv_ref[rows, :] = (dvg * _gelu_grad(vc[rows])).astype(dv_ref.dtype)

        @pl.when(i == nstep - 1)
        def _():
            for g in range(SGU_GROUPS):
                dw_ref[g] = jnp.where(tril, dw_ref[g], 0.0)

    wspec = pl.BlockSpec((SGU_GROUPS, SGU_CHUNK, SGU_CHUNK), lambda i: (0, 0, 0))
    bspec = pl.BlockSpec((SGU_CHUNK, 256), lambda i: (0, 0))
    return _pcall(
        body, name=name, grid=(nstep,),
        out_shape=[jax.ShapeDtypeStruct((s, 256), _BF), jax.ShapeDtypeStruct((s, 256), _BF),
                   jax.ShapeDtypeStruct((SGU_GROUPS, SGU_CHUNK, SGU_CHUNK), F32),
                   jax.ShapeDtypeStruct((SGU_CHUNK, 256), F32), jax.ShapeDtypeStruct((1, 256), F32)],
        in_specs=[_row_spec(t, 256, C_UC // 256), _row_spec(t, 256, C_VC // 256), wspec, _vec_spec(256), bspec,
                  _row_spec(t, 256, M_SGU // 256)],
        out_specs=[_row_spec(t, 256), _row_spec(t, 256), wspec, bspec, _vec_spec(256)],
        compiler_params=_cp(ARB),
    )(proj, proj, w, gs, bexp, dmix)


def _ada_fwd(c_all, ada_w, name):
    depth, d, n = ada_w.shape

    def body(c_ref, w_ref, o_ref):
        cv = c_ref[...]
        cond = cv / (1.0 + jnp.exp(-cv))
        o_ref[...] = _dot_f32(cond, w_ref[...], 1, 0)

    return _pcall(
        body, name=name, grid=(depth,), out_shape=jax.ShapeDtypeStruct((depth, N_DEV, n), F32),
        in_specs=[pl.BlockSpec((N_DEV, d), lambda l: (0, 0)), pl.BlockSpec((None, d, n), lambda l: (l, 0, 0))],
        out_specs=pl.BlockSpec((None, N_DEV, n), lambda l: (l, 0, 0)), compiler_params=_cp(PAR),
    )(c_all, ada_w)


def _ada_bwd(c_all, dmod, name):
    depth, _, n = dmod.shape
    d = c_all.shape[1]

    def body(c_ref, dm_ref, o_ref):
        cv = c_ref[...]
        cond = cv / (1.0 + jnp.exp(-cv))
        o_ref[...] = _dot_f32(cond, dm_ref[...], 0, 0)

    return _pcall(
        body, name=name, grid=(depth,), out_shape=jax.ShapeDtypeStruct((depth, d, n), F32),
        in_specs=[pl.BlockSpec((N_DEV, d), lambda l: (0, 0)), pl.BlockSpec((None, N_DEV, n), lambda l: (l, 0, 0))],
        out_specs=pl.BlockSpec((None, d, n), lambda l: (l, 0, 0)), compiler_params=_cp(PAR),
    )(c_all, dmod)


def _adamw(slots, w, m, v, name):
    n, r, c = slots.shape
    tr = 256 if r % 256 == 0 else r
    bc1 = 1.0 - ADAM_B1 ** ADAM_STEP
    bc2 = 1.0 - ADAM_B2 ** ADAM_STEP

    def body(s_ref, w_ref, m_ref, v_ref, g_ref, d_ref, nm_ref, nv_ref):
        g = s_ref[0].astype(F32)
        for j in range(1, n):
            g = g + s_ref[j].astype(F32)
        m_new = ADAM_B1 * m_ref[...] + (1.0 - ADAM_B1) * g
        v_new = ADAM_B2 * v_ref[...] + (1.0 - ADAM_B2) * (g * g)
        g_ref[...] = g
        nm_ref[...] = m_new
        nv_ref[...] = v_new
        d_ref[...] = -ADAM_LR * ((m_new / bc1) / (jnp.sqrt(v_new / bc2) + ADAM_EPS) + ADAM_WD * w_ref[...])

    tile = pl.BlockSpec((tr, c), lambda i: (i, 0))
    shp = jax.ShapeDtypeStruct((r, c), F32)
    return _pcall(
        body, name=name, grid=(r // tr,), out_shape=[shp] * 4,
        in_specs=[pl.BlockSpec((n, tr, c), lambda i: (0, i, 0)), tile, tile, tile],
        out_specs=[tile] * 4, compiler_params=_cp(PAR),
    )(slots, w, m, v)


def _w_in_layout(w):
    pad = jnp.zeros(w.shape[:-1] + (IN_P - IN_W,), w.dtype)
    return jnp.concatenate([w[..., 768:FL_SRC], w[..., :768], w[..., UC_SRC:], w[..., FL_SRC:UC_SRC], pad], axis=-1)


def _w_in_unlayout(g):
    return jnp.concatenate([g[..., C_QA:C_UC], g[..., :C_QA], g[..., C_FL:C_FL + FOX_HEADS], g[..., C_UC:C_FL]], axis=-1)


SMALL = [("ada_b", DEPTH * 6 * D_MODEL), ("norm1_g", DEPTH * D_MODEL), ("norm2_g", DEPTH * D_MODEL),
         ("sgu_w", DEPTH * SGU_GROUPS * SGU_CHUNK * SGU_CHUNK), ("sgu_b", DEPTH * SGU_GROUPS * SGU_CHUNK),
         ("sgu_norm_g", DEPTH * SGU_GROUPS * HEAD_DIM), ("q_norm_g", DEPTH * HEAD_DIM), ("k_norm_g", DEPTH * HEAD_DIM),
         ("b_forget", DEPTH * FOX_HEADS), ("loss", 1)]
SMALL_ROWS = 2560


def _pack_small(parts):
    flat = jnp.concatenate([parts[name].reshape(-1).astype(F32) for name, _ in SMALL])
    return jnp.pad(flat, (0, SMALL_ROWS * 128 - flat.shape[0])).reshape(SMALL_ROWS, 128)


def _unpack_small(packed, shapes):
    flat = packed.reshape(-1)
    out, off = {}, 0
    for name, size in SMALL:
        out[name] = flat[off:off + size].reshape(shapes[name])
        off += size
    return out


def kernel(x, c, ada_w, ada_b, norm1_g, norm2_g, w_in, b_forget, q_norm_g, k_norm_g, sgu_norm_g, sgu_w, sgu_b, w_out, mlp_w1, mlp_w2, loss_target, m_ada_w, m_ada_b, m_norm1_g, m_norm2_g, m_w_in, m_b_forget, m_q_norm_g, m_k_norm_g, m_sgu_norm_g, m_sgu_w, m_sgu_b, m_w_out, m_mlp_w1, m_mlp_w2, v_ada_w, v_ada_b, v_norm1_g, v_norm2_g, v_w_in, v_b_forget, v_q_norm_g, v_k_norm_g, v_sgu_norm_g, v_sgu_w, v_sgu_b, v_w_out, v_mlp_w1, v_mlp_w2):
    me = _lin(_my_pos())
    x0 = x[0]
    target = loss_target[0]
    n_ada = ada_w.shape[2]

    shards = [w.astype(_XBF) for w in (w_in, w_out, mlp_w1, mlp_w2)]

    def whole_in(w_in_g):
        return _w_in_layout(jnp.transpose(w_in_g, (1, 0, 2)).reshape(D_MODEL, IN_W))

    def whole_rest(w_out_g, w1_g, w2_g):
        return (w_out_g.reshape(D_MODEL, D_MODEL), jnp.transpose(w1_g, (1, 0, 2)).reshape(D_MODEL, D_FF),
                w2_g.reshape(D_FF, D_MODEL))

    w_in_first, c_all = _exchange([shards[0][0], c], False, "gather_weights")
    c_all = c_all.reshape(N_DEV, D_MODEL)
    weights = [None] * DEPTH

    mod_part = _ada_fwd(c_all, ada_w, "ada_fwd")
    (mod_rows,) = _exchange([jnp.transpose(mod_part, (1, 0, 2))], True, "scatter_mod")
    mod = jnp.transpose(mod_rows, (1, 0, 2)).reshape(DEPTH, 6 * D_MODEL) + ada_b
    mods = mod.reshape(DEPTH, 6, 1, D_MODEL)

    saved = []
    xl = x0
    for l in range(DEPTH):
        sh1, sc1, g1, sh2, sc2, g2 = (mods[l, r] for r in range(6))
        n1 = norm1_g[l].reshape(1, D_MODEL)
        n2 = norm2_g[l].reshape(1, D_MODEL)
        gq = jnp.tile(q_norm_g[l], FOX_HEADS).reshape(1, 512)
        gk = jnp.tile(k_norm_g[l], FOX_HEADS).reshape(1, 512)
        bf = jnp.pad(b_forget[l], (0, 128 - FOX_HEADS)).reshape(1, 128)
        gs = sgu_norm_g[l].reshape(1, 256)
        bexp = jnp.repeat(sgu_b[l].T, HEAD_DIM, axis=1)

        if l == 0:
            w_in_l = whole_in(w_in_first)
            gather_rest = dict(arrays=[w[0] for w in shards[1:]], scatter=False, recv=None, slot=None)
        else:
            w_in_l, gather_rest = weights[l][0], None
        h1, h1_t = _norm_mod(xl, n1, sc1, sh1, "norm_mod")
        proj = _mm_nn(h1, w_in_l, "plain", (), "mm_in")
        o_sb, gathered = _sb_fwd(proj, "sb_fwd", gather_rest)
        if gathered:
            weights[0] = (w_in_l,) + whole_rest(*gathered)
        qn, kn, cfc, cfr, cfe = _fox_prep(proj, gq, gk, bf, "fox_prep")
        qkb = (1.02 * HEAD_DIM ** 0.5 * jnp.max(jnp.abs(q_norm_g[l])) * jnp.max(jnp.abs(k_norm_g[l]))).reshape(1, 1)
        gather_next = None
        if l + 1 < DEPTH:
            gather_next = dict(arrays=[w[l + 1] for w in shards], scatter=False, recv=None, slot=None)
        o_fox, lse, gathered = _fox_fwd(proj, qn, kn, cfc, cfr, cfe, qkb, "fox_fwd", gather_next)
        if gathered:
            weights[l + 1] = (whole_in(gathered[0]),) + whole_rest(*gathered[1:])
        _, w_out_l, w1_l, w2_l = weights[l]
        o_sgu = _sgu_fwd(proj, sgu_w[l], gs, bexp, "sgu_fwd")
        x_mid, y1, mixed_t = _mm_out((o_sb, o_fox, o_sgu), w_out_l, xl, g1, "mm_out")
        h2, h2_t = _norm_mod(x_mid, n2, sc2, sh2, "norm_mod")
        r2, r2_t = _mm_nn(h2, w1_l, "relu2", (), "mm_w1")
        x_out, y2 = _mm_nn(r2, w2_l, "resid", (x_mid, g2), "mm_w2")
        saved.append(dict(x_in=xl, h1_t=h1_t, proj=proj, o_sb=o_sb, qn=qn, kn=kn, cfc=cfc, cfr=cfr, cfe=cfe, o_fox=o_fox,
                          lse=lse, mixed_t=mixed_t, x_mid=x_mid, y1=y1, h2_t=h2_t, r2=r2, r2_t=r2_t, y2=y2,
                          n1=n1, n2=n2, gq=gq, gk=gk, bf=bf, gs=gs, bexp=bexp, qkb=qkb))
        xl = x_out

    loss_part, dx, dy2, dg2 = _loss_head(xl, target, (saved[-1]["y2"], mods[DEPTH - 1, 5]), "loss_head")

    grads_ready = None
    received = [lax.empty((N_DEV,) + w.shape, _XBF) for w in shards]
    small = {k: [None] * DEPTH for k in ("mod", "norm1_g", "norm2_g", "sgu_w", "sgu_b", "sgu_norm_g",
                                         "q_norm_g", "k_norm_g", "b_forget")}
    for l in reversed(range(DEPTH)):
        sv = saved[l]
        sh1, sc1, g1, sh2, sc2, g2 = (mods[l, r] for r in range(6))
        w_in_l, w_out_l, w1_l, w2_l = weights[l]
        da = _mm_nt(dy2, w2_l, "mul2", (sv["r2"],), _BF, "mm_w2_bwd")
        g_w2 = _mm_grad(sv["r2_t"], dy2, "rows", "mm_w2_grad")
        dh2 = _mm_nt(da, w1_l, "plain", (), F32, "mm_w1_bwd")
        g_w1 = _mm_grad(sv["h2_t"], da, "cols", "mm_w1_grad")
        dx_mid, dsh2, dsc2, dn2, dy1, dg1 = _norm_mod_bwd(sv["x_mid"], dh2, dx, sv["n2"], sc2, (sv["y1"], g1),
                                                          "norm_mod_bwd")
        dmix = _mm_nt(dy1, w_out_l, "plain", (), F32, "mm_out_bwd")
        g_out = _mm_grad(sv["mixed_t"], dy1, "rows", "mm_out_grad")
        send_own = None
        if l == 0:
            send_own = dict(arrays=[g_out, g_w1, g_w2], scatter=True, recv=received[1:], slot=0)
        dq_sb, dk_sb, dv_sb, filled = _sb_bwd(sv["proj"], sv["o_sb"], dmix, "sb_bwd", send_own)
        if filled:
            received = received[:1] + filled
        send_prev = None
        if grads_ready is not None:
            send_prev = dict(arrays=grads_ready, scatter=True, recv=received, slot=l + 1)
        dqn, dkn, dv_fox, dlogf, filled = _fox_bwd(sv["proj"], sv["qn"], sv["kn"], sv["cfc"], sv["cfr"], sv["cfe"], sv["qkb"],
                                                 sv["o_fox"], sv["lse"], dmix, "fox_bwd", send_prev)
        if filled:
            received = filled
        dq_fox, dk_fox, dfl, dgq, dgk, dbf = _fox_prep_bwd(sv["proj"], sv["gq"], sv["gk"], sv["bf"], dqn, dkn, dlogf,
                                                           "fox_prep_bwd")
        duc, dvc, dsw, dsb, dsg = _sgu_bwd(sv["proj"], sgu_w[l], sv["gs"], sv["bexp"], dmix, "sgu_bwd")
        dproj = jnp.concatenate(
            [dq_fox, dk_fox, dv_fox.astype(_BF), dq_sb.astype(_BF), dk_sb.astype(_BF), dv_sb.astype(_BF), duc, dvc, dfl,
             jnp.zeros((dfl.shape[0], IN_P - C_FL - 128), _BF)], axis=-1)
        dh1 = _mm_nt(dproj, w_in_l, "plain", (), F32, "mm_in_bwd")
        g_in = _w_in_unlayout(_mm_grad(sv["h1_t"], dproj, None, "mm_in_grad"))
        g_in = jnp.transpose(g_in.reshape(D_MODEL, N_DEV, IN_W // N_DEV), (1, 0, 2))
        grads_ready = [g_in, g_out, g_w1, g_w2]
        small["mod"][l] = [dg1, dsh2, dsc2, dg2]
        below = (saved[l - 1]["y2"], mods[l - 1, 5]) if l > 0 else None
        dx, dsh1, dsc1, dn1, *below_grads = _norm_mod_bwd(sv["x_in"], dh1, dx_mid, sv["n1"], sc1, below, "norm_mod_bwd")
        if below_grads:
            dy2, dg2 = below_grads
        small["mod"][l] = jnp.concatenate([dsh1, dsc1] + small["mod"][l], axis=-1).reshape(-1)
        small["norm1_g"][l] = dn1.reshape(-1)
        small["norm2_g"][l] = dn2.reshape(-1)
        small["sgu_w"][l] = dsw
        small["sgu_b"][l] = dsb.reshape(SGU_CHUNK, SGU_GROUPS, HEAD_DIM).sum(-1).T
        small["sgu_norm_g"][l] = dsg.reshape(SGU_GROUPS, HEAD_DIM)
        small["q_norm_g"][l] = dgq.reshape(FOX_HEADS, HEAD_DIM).sum(0)
        small["k_norm_g"][l] = dgk.reshape(FOX_HEADS, HEAD_DIM).sum(0)
        small["b_forget"][l] = dbf[0, :FOX_HEADS]

    parts = {k: jnp.stack(v) for k, v in small.items()}
    parts["ada_b"] = parts.pop("mod")
    parts["loss"] = loss_part
    (small_all,) = _exchange([_pack_small(parts)], False, "gather_small")
    zero1 = jnp.zeros((1,), F32)
    packs = [_pack_small(dict(ada_b=a, norm1_g=b, norm2_g=cc, sgu_w=d, sgu_b=e, sgu_norm_g=f, q_norm_g=g, k_norm_g=h,
                              b_forget=i, loss=zero1))
             for a, b, cc, d, e, f, g, h, i in (
                 (ada_b, norm1_g, norm2_g, sgu_w, sgu_b, sgu_norm_g, q_norm_g, k_norm_g, b_forget),
                 (m_ada_b, m_norm1_g, m_norm2_g, m_sgu_w, m_sgu_b, m_sgu_norm_g, m_q_norm_g, m_k_norm_g, m_b_forget),
                 (v_ada_b, v_norm1_g, v_norm2_g, v_sgu_w, v_sgu_b, v_sgu_norm_g, v_q_norm_g, v_k_norm_g, v_b_forget))]
    small_shapes = dict(ada_b=ada_b.shape, norm1_g=norm1_g.shape, norm2_g=norm2_g.shape, sgu_w=sgu_w.shape,
                        sgu_b=sgu_b.shape, sgu_norm_g=sgu_norm_g.shape, q_norm_g=q_norm_g.shape,
                        k_norm_g=k_norm_g.shape, b_forget=b_forget.shape, loss=())
    s_out = [_unpack_small(a, small_shapes) for a in _adamw(small_all, *packs, "adamw_small")]

    dmod_all = small_all[:, :DEPTH * 6 * D_MODEL // 128, :].reshape(N_DEV, DEPTH, 6 * D_MODEL)
    dmod_mine = lax.dynamic_slice_in_dim(dmod_all, me * n_ada, n_ada, axis=2)
    g_ada = _ada_bwd(c_all, jnp.transpose(dmod_mine, (1, 0, 2)), "ada_bwd")
    ada_out = _adamw(g_ada.reshape(1, DEPTH * D_MODEL, n_ada), ada_w.reshape(-1, n_ada), m_ada_w.reshape(-1, n_ada),
                     v_ada_w.reshape(-1, n_ada), "adamw_ada")
    ada_out = [a.reshape(ada_w.shape) for a in ada_out]

    (ri,) = _exchange(grads_ready[:1], True, "scatter_grads", recv=received[:1], slot=0)
    _, ro, r1, r2 = received

    def big(slots, w, m, v, name):
        cdim = w.shape[-1]
        outs = _adamw(slots.reshape(N_DEV, -1, cdim), w.reshape(-1, cdim), m.reshape(-1, cdim), v.reshape(-1, cdim), name)
        return [a.reshape(w.shape) for a in outs]

    in_out = big(ri, w_in, m_w_in, v_w_in, "adamw_in")
    out_out = big(ro, w_out, m_w_out, v_w_out, "adamw_out")
    w1_out = big(r1, mlp_w1, m_mlp_w1, v_mlp_w1, "adamw_w1")
    w2_out = big(r2, mlp_w2, m_mlp_w2, v_mlp_w2, "adamw_w2")

    def leaves(idx):
        sm = s_out[idx]
        return [ada_out[idx], sm["ada_b"], sm["norm1_g"], sm["norm2_g"], in_out[idx], sm["b_forget"], sm["q_norm_g"],
                sm["k_norm_g"], sm["sgu_norm_g"], sm["sgu_w"], sm["sgu_b"], out_out[idx], w1_out[idx], w2_out[idx]]

    loss = s_out[0]["loss"]
    grad_x = dx.reshape(x.shape)
    return (loss, grad_x, *leaves(0), *leaves(1), *leaves(2), *leaves(3))
```

```python
import jax
import jax.numpy as jnp
from jax import lax
from jax.experimental import pallas as pl
from jax.experimental.pallas import tpu as pltpu

F32 = jnp.float32
_BF = jnp.bfloat16
_XBF = jnp.bfloat16

N_DEV = 8
D_MODEL = 1024
DEPTH = 4
HEAD_DIM = 64
SB_HEADS = 4
FOX_HEADS = 8
SGU_GROUPS = 4
SGU_CHUNK = 128
D_FF = 4096
EPS = 1e-6
IN_W = 2824
FL_SRC = 2304
UC_SRC = 2312

C_QB, C_KB, C_VB = 0, 512, 1024
C_QA, C_KA, C_VA = 1536, 1792, 2048
C_UC, C_VC = 2304, 2560
C_FL = 2816
IN_P = 3072
M_SB, M_FOX, M_SGU = 0, 256, 768

ADAM_LR = 0.001
ADAM_B1 = 0.9
ADAM_B2 = 0.999
ADAM_EPS = 1e-08
ADAM_WD = 0.01
ADAM_STEP = 10

ATT_T = 256
SB_T = 256
ROW_T = 512
MM_T = 1024
GRAD_TS = 4096
ROW_CHUNK = 64
NEG = -1e30
UNDERFLOW = -90.0
NO_BLOCK = 1e30

MESH = pl.DeviceIdType.MESH
ARB = pltpu.ARBITRARY
PAR = pltpu.PARALLEL
HIGHEST = lax.Precision.HIGHEST


def _pcall(body, **kw):
    return pl.pallas_call(body, **kw)


def _cp(*sem):
    return pltpu.CompilerParams(dimension_semantics=tuple(sem))


def _dot(a, b, ca, cb):
    return lax.dot_general(a, b, (((ca,), (cb,)), ((), ())), preferred_element_type=F32)


def _dot_f32(a, b, ca, cb):
    return lax.dot_general(a, b, (((ca,), (cb,)), ((), ())), precision=HIGHEST, preferred_element_type=F32)


def _split(v, parts):
    out = []
    r = v
    for _ in range(parts - 1):
        p = r.astype(_BF)
        out.append(p)
        r = r - p.astype(F32)
    out.append(r.astype(_BF))
    return out


def _dot_exact(v, m, ca, cb, parts, v_left=True):
    out = None
    for p in _split(v, parts):
        term = _dot(p, m, ca, cb) if v_left else _dot(m, p, ca, cb)
        out = term if out is None else out + term
    return out


def _iota2(shape, dim):
    return lax.broadcasted_iota(jnp.int32, shape, dim)


def _my_pos():
    return lax.axis_index("x"), lax.axis_index("y"), lax.axis_index("c")


def _flip(pos, k):
    x, y, c = pos
    px = 1 - x if (k >> 2) & 1 else x
    py = 1 - y if (k >> 1) & 1 else y
    pc = 1 - c if k & 1 else c
    return px, py, pc


def _lin(pos):
    return 4 * pos[0] + 2 * pos[1] + pos[2]


def _exchange_ops(ins, outs, sems, scatter, slot):
    send_sems, recv_sems, local_sems = sems
    n = len(ins)

    def copies(with_recvs):
        me = _my_pos()
        me_i = _lin(me)
        dst = lambda a, j: outs[a].at[j] if slot is None else outs[a].at[j, slot]
        mine = lambda a: ins[a].at[me_i] if scatter else ins[a]
        local = [pltpu.make_async_copy(mine(a), dst(a, me_i), local_sems.at[a]) for a in range(n)]
        sends, recvs = [], []
        for a in range(n):
            for k in range(1, N_DEV):
                peer = _flip(me, k)
                pair = dict(send_sem=send_sems.at[a, k - 1], recv_sem=recv_sems.at[a, k - 1],
                            device_id=peer, device_id_type=MESH)
                src = ins[a].at[_lin(peer)] if scatter else ins[a]
                sends.append(pltpu.make_async_remote_copy(src_ref=src, dst_ref=dst(a, me_i), **pair))
                if with_recvs:
                    recvs.append(pltpu.make_async_remote_copy(src_ref=mine(a), dst_ref=dst(a, _lin(peer)), **pair))
        return local, sends, recvs

    def start():
        local, sends, _ = copies(False)
        for cp in local + sends:
            cp.start()

    def wait():
        local, sends, recvs = copies(True)
        for snd, rcv in zip(sends, recvs):
            snd.wait_send()
            rcv.wait_recv()
        for cp in local:
            cp.wait()

    return start, wait


def _comm_operands(comm):
    arrays, recv = comm["arrays"], comm["recv"]
    n = len(arrays)
    any_spec = pl.BlockSpec(memory_space=pl.ANY)
    if recv is not None:
        out_shapes = [jax.ShapeDtypeStruct(r.shape, r.dtype) for r in recv]
    elif comm["scatter"]:
        out_shapes = [jax.ShapeDtypeStruct(a.shape, a.dtype) for a in arrays]
    else:
        out_shapes = [jax.ShapeDtypeStruct((N_DEV,) + a.shape, a.dtype) for a in arrays]
    operands = list(arrays) + (list(recv) if recv is not None else [])
    sems = [pltpu.SemaphoreType.DMA((n, N_DEV - 1)), pltpu.SemaphoreType.DMA((n, N_DEV - 1)),
            pltpu.SemaphoreType.DMA((n,))]
    return operands, [any_spec] * len(operands), out_shapes, [any_spec] * n, sems


def _with_comm(body, comm, n_in, n_out, first_step, last_step, n_scratch=0):
    if comm is None:
        return body
    n = len(comm["arrays"])
    n_cin = n if comm["recv"] is None else 2 * n

    def wrapped(*refs):
        ins, cins = refs[:n_in], refs[n_in:n_in + n]
        o0 = n_in + n_cin
        outs, couts = refs[o0:o0 + n_out], refs[o0 + n_out:o0 + n_out + n]
        s0 = o0 + n_out + n
        scratch, sems = refs[s0:s0 + n_scratch], refs[s0 + n_scratch:]
        start, wait = _exchange_ops(cins, couts, sems, comm["scatter"], comm["slot"])
        pl.when(first_step())(start)
        body(*ins, *outs, *scratch)
        pl.when(last_step())(wait)

    return wrapped


def _comm_aliases(comm, n_in, n_out):
    if comm is None or comm["recv"] is None:
        return {}
    n = len(comm["arrays"])
    return {n_in + n + a: n_out + a for a in range(n)}


def _gather_two_level(v, name):
    def body(v_ref, out_ref, send_sems, recv_sems, local_sem):
        x, y, c = _my_pos()
        me, sibling = (x, y, c), (x, y, 1 - c)
        chips = [(1 - x, y), (x, 1 - y), (1 - x, 1 - y)]

        def copy(k, block, to, src=None):
            slot = out_ref.at[_lin(block)]
            return pltpu.make_async_remote_copy(
                src_ref=slot if src is None else src, dst_ref=slot,
                send_sem=send_sems.at[k], recv_sem=recv_sems.at[k], device_id=to, device_id_type=MESH)

        mine = pltpu.make_async_copy(v_ref, out_ref.at[_lin(me)], local_sem)
        mine.start()
        first = [copy(0, me, sibling, src=v_ref)]
        first += [copy(1 + j, me, (*chip, c), src=v_ref) for j, chip in enumerate(chips)]
        for cp in first:
            cp.start()
        passed = [copy(4 + j, (*chip, c), sibling) for j, chip in enumerate(chips)]
        for j, chip in enumerate(chips):
            copy(1 + j, (*chip, c), me).wait_recv()
            passed[j].start()
        copy(0, sibling, me).wait_recv()
        for j, chip in enumerate(chips):
            copy(4 + j, (*chip, 1 - c), me).wait_recv()
        for cp in first + passed:
            cp.wait_send()
        mine.wait()

    any_spec = pl.BlockSpec(memory_space=pl.ANY)
    return _pcall(
        body, name=name, out_shape=jax.ShapeDtypeStruct((N_DEV,) + v.shape, v.dtype),
        in_specs=[any_spec], out_specs=any_spec,
        scratch_shapes=[pltpu.SemaphoreType.DMA((7,)), pltpu.SemaphoreType.DMA((7,)), pltpu.SemaphoreType.DMA],
    )(v)


def _exchange(arrays, scatter, name, recv=None, slot=None):
    comm = dict(arrays=arrays, scatter=scatter, recv=recv, slot=slot)
    operands, in_specs, out_shapes, out_specs, sems = _comm_operands(comm)
    n = len(arrays)

    def body(*refs):
        n_cin = len(operands)
        start, wait = _exchange_ops(refs[:n], refs[n_cin:n_cin + n], refs[n_cin + n:], scatter, slot)
        start()
        wait()

    outs = _pcall(body, name=name, out_shape=out_shapes, in_specs=in_specs, out_specs=out_specs, scratch_shapes=sems,
                  input_output_aliases=_comm_aliases(comm, 0, 0))(*operands)
    return list(outs)


def _row_spec(t, w, col=0):
    return pl.BlockSpec((t, w), lambda i: (i, col))


def _vec_spec(w, col=0):
    return pl.BlockSpec((1, w), lambda i: (0, col))


def _norm_mod(x, g, sc, sh, name):
    s, d = x.shape
    t = min(ROW_T, s)

    def body(x_ref, g_ref, sc_ref, sh_ref, h_ref, ht_ref):
        xf = x_ref[...]
        rstd = lax.rsqrt(jnp.mean(xf * xf, axis=-1, keepdims=True) + EPS)
        y = xf * rstd * g_ref[...]
        h = y * (1.0 + sc_ref[...]) + sh_ref[...]
        h_ref[...] = h.astype(h_ref.dtype)
        ht_ref[...] = h.T.astype(ht_ref.dtype)

    return _pcall(
        body, name=name, grid=(s // t,),
        out_shape=[jax.ShapeDtypeStruct((s, d), _BF), jax.ShapeDtypeStruct((d, s), _BF)],
        in_specs=[_row_spec(t, d), _vec_spec(d), _vec_spec(d), _vec_spec(d)],
        out_specs=[_row_spec(t, d), pl.BlockSpec((d, t), lambda i: (0, i))], compiler_params=_cp(PAR),
    )(x, g, sc, sh)


def _res_bwd_step(i, dx, y_ref, gate_ref, dy_ref, dgate_ref):
    dy_ref[...] = (dx * gate_ref[...]).astype(dy_ref.dtype)

    @pl.when(i == 0)
    def _():
        dgate_ref[...] = jnp.zeros_like(dgate_ref)

    dgate_ref[...] += jnp.sum(dx * y_ref[...], axis=0, keepdims=True)


def _norm_mod_bwd(x, dh, dres, g, sc, branch, name):
    s, d = x.shape
    t = min(ROW_T, s)
    n_branch = 0 if branch is None else 2

    ch = min(ROW_CHUNK, t)

    def body(x_ref, dh_ref, dres_ref, g_ref, sc_ref, *rest):
        dx_ref, dsh_ref, dsc_ref, dg_ref = rest[n_branch:n_branch + 4]
        i = pl.program_id(0)
        gv = g_ref[...]
        sc1 = 1.0 + sc_ref[...]
        gain = gv * sc1
        n_sum = 2 if branch is None else 3

        def rows(r, sums):
            sl = pl.ds(pl.multiple_of(r * ch, ch), ch)
            xf = x_ref[sl, :]
            dh = dh_ref[sl, :]
            rstd = lax.rsqrt(jnp.mean(xf * xf, axis=-1, keepdims=True) + EPS)
            xhat = xf * rstd
            dh_xhat = dh * xhat
            dxh = dh * gain
            dx = dres_ref[sl, :] + rstd * (dxh - xhat * jnp.mean(dh_xhat * gain, axis=-1, keepdims=True))
            dx_ref[sl, :] = dx
            new = [sums[0] + jnp.sum(dh, axis=0, keepdims=True), sums[1] + jnp.sum(dh_xhat, axis=0, keepdims=True)]
            if branch is not None:
                rest[n_branch + 4][sl, :] = (dx * rest[1][...]).astype(_BF)
                new.append(sums[2] + jnp.sum(dx * rest[0][sl, :], axis=0, keepdims=True))
            return tuple(new)

        sums = lax.fori_loop(0, t // ch, rows, tuple(jnp.zeros((1, d), F32) for _ in range(n_sum)))
        parts = [sums[0], sums[1] * gv, sums[1] * sc1] + list(sums[2:])
        totals = [dsh_ref, dsc_ref, dg_ref] + ([rest[n_branch + 5]] if branch is not None else [])

        @pl.when(i == 0)
        def _():
            for ref in totals:
                ref[...] = jnp.zeros_like(ref)

        for ref, part in zip(totals, parts):
            ref[...] += part

    vec = jax.ShapeDtypeStruct((1, d), F32)
    out_shape = [jax.ShapeDtypeStruct((s, d), F32), vec, vec, vec]
    in_specs = [_row_spec(t, d), _row_spec(t, d), _row_spec(t, d), _vec_spec(d), _vec_spec(d)]
    out_specs = [_row_spec(t, d), _vec_spec(d), _vec_spec(d), _vec_spec(d)]
    if branch is not None:
        in_specs += [_row_spec(t, d), _vec_spec(d)]
        out_shape += [jax.ShapeDtypeStruct((s, d), _BF), vec]
        out_specs += [_row_spec(t, d), _vec_spec(d)]
    return _pcall(
        body, name=name, grid=(s // t,), out_shape=out_shape, in_specs=in_specs, out_specs=out_specs,
        compiler_params=_cp(ARB),
    )(x, dh, dres, g, sc, *(branch or ()))


def _loss_head(y, target, branch, name):
    s, d = y.shape
    t = min(ROW_T, s)

    def body(y_ref, t_ref, yb_ref, gate_ref, loss_ref, dy_ref, dyb_ref, dgate_ref):
        i = pl.program_id(0)
        diff = y_ref[...] - t_ref[...]
        dy = diff * (1.0 / d)
        dy_ref[...] = dy

        @pl.when(i == 0)
        def _():
            loss_ref[...] = jnp.zeros_like(loss_ref)

        rows = jnp.sum(diff * diff, axis=-1, keepdims=True)
        loss_ref[...] += (0.5 / d) * jnp.sum(rows, axis=0, keepdims=True)
        _res_bwd_step(i, dy, yb_ref, gate_ref, dyb_ref, dgate_ref)

    return _pcall(
        body, name=name, grid=(s // t,),
        out_shape=[jax.ShapeDtypeStruct((1, 1), F32), jax.ShapeDtypeStruct((s, d), F32),
                   jax.ShapeDtypeStruct((s, d), _BF), jax.ShapeDtypeStruct((1, d), F32)],
        in_specs=[_row_spec(t, d), _row_spec(t, d), _row_spec(t, d), _vec_spec(d)],
        out_specs=[pl.BlockSpec((1, 1), lambda i: (0, 0)), _row_spec(t, d), _row_spec(t, d), _vec_spec(d)],
        compiler_params=_cp(ARB),
    )(y, target, *branch)


def _contract_tile(kdim):
    if kdim > MM_T and (kdim // 2) % 128 == 0:
        return kdim // 2
    return min(MM_T, kdim)


def _mm_nn(a, b, epi, extras, name):
    m, kdim = a.shape
    n = b.shape[1]
    tm, tn, tk = min(MM_T, m), min(MM_T, n), _contract_tile(kdim)
    nk = kdim // tk
    n_extra = len(extras)
    n_out = {"plain": 1, "resid": 2, "relu2": 2}[epi]

    def finish(y, extra_refs, out_refs):
        if epi == "plain":
            out_refs[0][...] = y.astype(out_refs[0].dtype)
        elif epi == "resid":
            x_ref, g_ref = extra_refs
            out_refs[0][...] = x_ref[...] + g_ref[...] * y
            out_refs[1][...] = y.astype(out_refs[1].dtype)
        else:
            r = jnp.maximum(y, 0.0)
            out_refs[0][...] = (r * r).astype(out_refs[0].dtype)
            out_refs[1][...] = (r * r).T.astype(out_refs[1].dtype)

    def body(a_ref, b_ref, *rest):
        extra_refs = rest[:n_extra]
        out_refs = rest[n_extra:n_extra + n_out]
        part = _dot(a_ref[...].astype(_BF), b_ref[...].astype(_BF), 1, 0)
        if nk == 1:
            finish(part, extra_refs, out_refs)
        else:
            acc_ref = rest[-1]
            k = pl.program_id(2)

            @pl.when(k == 0)
            def _():
                acc_ref[...] = part

            @pl.when(k > 0)
            def _():
                acc_ref[...] += part

            @pl.when(k == nk - 1)
            def _():
                finish(acc_ref[...], extra_refs, out_refs)

    tile = pl.BlockSpec((tm, tn), lambda i, j, k: (i, j))
    in_specs = [pl.BlockSpec((tm, tk), lambda i, j, k: (i, k)), pl.BlockSpec((tk, tn), lambda i, j, k: (k, j))]
    if epi == "plain":
        out_shape = [jax.ShapeDtypeStruct((m, n), F32)]
    elif epi == "resid":
        in_specs += [tile, pl.BlockSpec((1, tn), lambda i, j, k: (0, j))]
        out_shape = [jax.ShapeDtypeStruct((m, n), F32), jax.ShapeDtypeStruct((m, n), _BF)]
    out_specs = [tile] * n_out
    if epi == "relu2":
        out_shape = [jax.ShapeDtypeStruct((m, n), _BF), jax.ShapeDtypeStruct((n, m), _BF)]
        out_specs[1] = pl.BlockSpec((tn, tm), lambda i, j, k: (j, i))
    outs = _pcall(
        body, name=name, grid=(m // tm, n // tn, nk), out_shape=out_shape,
        in_specs=in_specs, out_specs=out_specs,
        scratch_shapes=[pltpu.VMEM((tm, tn), F32)] if nk > 1 else [],
        compiler_params=_cp(PAR, PAR, ARB),
    )(a, b, *extras)
    return outs[0] if n_out == 1 else outs


def _mm_out(parts, w, x, gate, name):
    s, n = x.shape
    widths = [p.shape[1] for p in parts]
    kdim = sum(widths)
    tm = min(ROW_T, s)

    def body(a0_ref, a1_ref, a2_ref, w_ref, x_ref, g_ref, xo_ref, y_ref, mt_ref):
        mixed = jnp.concatenate([a0_ref[...], a1_ref[...], a2_ref[...]], axis=-1)
        y = _dot(mixed.astype(_BF), w_ref[...], 1, 0)
        xo_ref[...] = x_ref[...] + g_ref[...] * y
        y_ref[...] = y.astype(y_ref.dtype)
        mt_ref[...] = mixed.T.astype(mt_ref.dtype)

    row = lambda width: pl.BlockSpec((tm, width), lambda i: (i, 0))
    return _pcall(
        body, name=name, grid=(s // tm,),
        out_shape=[jax.ShapeDtypeStruct((s, n), F32), jax.ShapeDtypeStruct((s, n), _BF),
                   jax.ShapeDtypeStruct((kdim, s), _BF)],
        in_specs=[row(widths[0]), row(widths[1]), row(widths[2]), pl.BlockSpec((kdim, n), lambda i: (0, 0)),
                  row(n), pl.BlockSpec((1, n), lambda i: (0, 0))],
        out_specs=[row(n), row(n), pl.BlockSpec((kdim, tm), lambda i: (0, i))], compiler_params=_cp(PAR),
    )(*parts, w, x, gate)


def _mm_nt(a, b, epi, extras, out_dtype, name):
    m, kdim = a.shape
    n = b.shape[0]
    tm, tn, tk = min(MM_T, m), min(MM_T, n), _contract_tile(kdim)
    nk = kdim // tk
    n_extra = len(extras)

    def finish(y, extra_refs, o_ref):
        if epi == "mul2":
            y = y * (2.0 * jnp.sqrt(extra_refs[0][...].astype(F32)))
        o_ref[...] = y.astype(o_ref.dtype)

    def body(a_ref, b_ref, *rest):
        extra_refs = rest[:n_extra]
        o_ref = rest[n_extra]
        part = _dot(a_ref[...].astype(_BF), b_ref[...].astype(_BF), 1, 1)
        if nk == 1:
            finish(part, extra_refs, o_ref)
        else:
            acc_ref = rest[-1]
            k = pl.program_id(2)

            @pl.when(k == 0)
            def _():
                acc_ref[...] = part

            @pl.when(k > 0)
            def _():
                acc_ref[...] += part

            @pl.when(k == nk - 1)
            def _():
                finish(acc_ref[...], extra_refs, o_ref)

    tile = pl.BlockSpec((tm, tn), lambda i, j, k: (i, j))
    in_specs = [pl.BlockSpec((tm, tk), lambda i, j, k: (i, k)), pl.BlockSpec((tn, tk), lambda i, j, k: (j, k))]
    in_specs += [tile] * n_extra
    return _pcall(
        body, name=name, grid=(m // tm, n // tn, nk), out_shape=jax.ShapeDtypeStruct((m, n), out_dtype),
        in_specs=in_specs, out_specs=tile,
        scratch_shapes=[pltpu.VMEM((tm, tn), F32)] if nk > 1 else [],
        compiler_params=_cp(PAR, PAR, ARB),
    )(a, b, *extras)


def _mm_grad(at, b, split, name):
    m, s = at.shape
    n = b.shape[1]
    ts = min(GRAD_TS, s)
    ns = s // ts
    if split == "rows":
        tm, tn = min(MM_T, m // N_DEV), min(MM_T, n)
        per = (m // N_DEV) // tm
        out_shape = (N_DEV, m // N_DEV, n)
        out_spec = pl.BlockSpec((None, tm, tn), lambda i, j, k: (i // per, i % per, j))
    elif split == "cols":
        tm, tn = min(MM_T, m), min(MM_T, n // N_DEV)
        per = (n // N_DEV) // tn
        out_shape = (N_DEV, m, n // N_DEV)
        out_spec = pl.BlockSpec((None, tm, tn), lambda i, j, k: (j // per, i, j % per))
    else:
        tm, tn = min(MM_T, m), min(MM_T, n)
        out_shape = (m, n)
        out_spec = pl.BlockSpec((tm, tn), lambda i, j, k: (i, j))

    def body(a_ref, b_ref, o_ref, acc_ref):
        k = pl.program_id(2)
        part = _dot(a_ref[...].astype(_BF), b_ref[...].astype(_BF), 1, 0)

        @pl.when(k == 0)
        def _():
            acc_ref[...] = part

        @pl.when(k > 0)
        def _():
            acc_ref[...] += part

        @pl.when(k == ns - 1)
        def _():
            o_ref[...] = acc_ref[...].astype(o_ref.dtype)

    return _pcall(
        body, name=name, grid=(m // tm, n // tn, ns), out_shape=jax.ShapeDtypeStruct(out_shape, _XBF),
        in_specs=[pl.BlockSpec((tm, ts), lambda i, j, k: (i, k)), pl.BlockSpec((ts, tn), lambda i, j, k: (k, j))],
        out_specs=out_spec, scratch_shapes=[pltpu.VMEM((tm, tn), F32)],
        compiler_params=_cp(PAR, PAR, ARB),
    )(at, b)


def _head_masks():
    lane = _iota2((1, 128), 1)
    return [(lane // HEAD_DIM) == hh for hh in range(2)]


def _untranspose(src_ref, dst_ref, t, nblk):
    def rows(c, _):
        off = pl.multiple_of(c * t, t)
        dst_ref[pl.ds(off, t), :] = src_ref[:, pl.ds(off, t)].T
        return 0

    lax.fori_loop(0, nblk, rows, 0)


def _softplus_parts(z):
    sp = jnp.maximum(z, 0.0) + jnp.log(1.0 + jnp.exp(-jnp.abs(z)))
    return -sp, z - sp


def _sb_key_loop(i, carry, step, lookahead, first_ahead):
    def cond(c):
        return jnp.logical_and(c[0] <= i, jnp.max(jnp.maximum(c[1][0][0], c[1][0][1])) > UNDERFLOW)

    def body(c):
        ahead = lookahead(i - c[0] - 1)
        return c[0] + 1, step(i - c[0], c[1], c[2]), ahead

    return lax.while_loop(cond, body, (jnp.int32(1), carry, first_ahead))[1]


def _sb_fwd(proj, name, comm=None):
    s = proj.shape[0]
    t = min(SB_T, s)
    nblk = s // t
    scale = HEAD_DIM ** -0.5

    def body(q_ref, k_ref, v_ref, o_ref):
        i = pl.program_id(1)
        masks = _head_masks()
        row = _iota2((t, t), 0)
        col = _iota2((t, t), 1)
        u_strict = (row > col).astype(_BF)
        causal = col < row
        qs = q_ref[...] * scale
        qm = [jnp.where(masks[hh], qs, 0.0).astype(_BF) for hh in range(2)]

        def scores(j):
            kb = k_ref[pl.ds(pl.multiple_of(jnp.maximum(j, 0) * t, t), t), :].astype(_BF)
            return tuple(_dot(qm[hh], kb, 1, 1) for hh in range(2))

        def block(j, carry, qk, diag):
            off = pl.multiple_of(j * t, t)
            vb = v_ref[pl.ds(off, t), :].astype(_BF)
            runs, outs = [], []
            for hh in range(2):
                run, o = carry[0][hh], carry[1][hh]
                z = qk[hh]
                l1, lb = _softplus_parts(z)
                if diag:
                    l1 = jnp.where(causal, l1, 0.0)
                between = run + _dot_exact(l1, u_strict, 1, 0, 2)
                a = jnp.exp(lb + between)
                if diag:
                    a = jnp.where(causal, a, 0.0)
                outs.append(o + _dot_exact(a, vb, 1, 0, 2))
                runs.append(run + jnp.sum(l1, axis=-1, keepdims=True))
            return tuple(runs), tuple(outs)

        zero, zero_o = jnp.zeros((t, 1), F32), jnp.zeros((t, 128), F32)
        ahead = scores(i - 1)
        carry = block(i, ((zero, zero), (zero_o, zero_o)), scores(i), True)
        carry = _sb_key_loop(i, carry, lambda j, c, qk: block(j, c, qk, False), scores, ahead)
        o_ref[...] = jnp.where(masks[0], carry[1][0], carry[1][1])

    kv_spec = lambda c0: pl.BlockSpec((s, 128), lambda p, i: (0, c0 // 128 + p))
    c_operands, c_in_specs, c_out_shapes, c_out_specs, c_sems = _comm_operands(comm) if comm else ([], [], [], [], [])
    outs = _pcall(
        _with_comm(body, comm, 3, 1, *_grid_ends(SB_HEADS // 2, nblk)), name=name, grid=(SB_HEADS // 2, nblk),
        out_shape=[jax.ShapeDtypeStruct((s, 256), F32)] + c_out_shapes,
        in_specs=[pl.BlockSpec((t, 128), lambda p, i: (i, C_QA // 128 + p)), kv_spec(C_KA), kv_spec(C_VA)] + c_in_specs,
        out_specs=[pl.BlockSpec((t, 128), lambda p, i: (i, p))] + c_out_specs,
        scratch_shapes=c_sems, input_output_aliases=_comm_aliases(comm, 3, 1), compiler_params=_cp(ARB, ARB),
    )(proj, proj, proj, *c_operands)
    return outs[0], list(outs[1:])


def _sb_bwd(proj, o, dmix, name, comm=None):
    s = proj.shape[0]
    t = min(SB_T, s)
    nblk = s // t
    scale = HEAD_DIM ** -0.5

    def body(q_ref, k_ref, v_ref, o_ref, do_ref, dq_ref, dk_ref, dv_ref, dkt_ref, dvt_ref):
        i = pl.program_id(1)
        masks = _head_masks()
        row = _iota2((t, t), 0)
        col = _iota2((t, t), 1)
        u_strict = (row > col).astype(_BF)
        u_incl = (row >= col).astype(_BF)
        causal = col < row

        @pl.when(i == 0)
        def _():
            dkt_ref[...] = jnp.zeros_like(dkt_ref)
            dvt_ref[...] = jnp.zeros_like(dvt_ref)

        qs = q_ref[...] * scale
        dov = do_ref[...]
        ov = o_ref[...]
        qm_f = [jnp.where(masks[hh], qs, 0.0) for hh in range(2)]
        dom_f = [jnp.where(masks[hh], dov, 0.0) for hh in range(2)]
        qm = [qm_f[hh].astype(_BF) for hh in range(2)]
        dom = [dom_f[hh].astype(_BF) for hh in range(2)]
        qm_t = [qm_f[hh].T.astype(_BF) for hh in range(2)]
        dom_t = [dom_f[hh].T.astype(_BF) for hh in range(2)]
        dsum = [jnp.sum(dom[hh].astype(F32) * ov, axis=-1, keepdims=True) for hh in range(2)]

        def scores(j):
            off = pl.multiple_of(jnp.maximum(j, 0) * t, t)
            kb = k_ref[pl.ds(off, t), :].astype(_BF)
            vb = v_ref[pl.ds(off, t), :].astype(_BF)
            return tuple((_dot(qm[hh], kb, 1, 1), _dot(dom[hh], vb, 1, 1)) for hh in range(2))

        def block(j, carry, ahead, diag):
            off = pl.multiple_of(j * t, t)
            kb = k_ref[pl.ds(off, t), :].astype(_BF)
            runs, eruns, dqs = [], [], []
            dk = dv = None
            for hh in range(2):
                run, erun, dq = carry[0][hh], carry[1][hh], carry[2][hh]
                z, da = ahead[hh]
                l1, lb = _softplus_parts(z)
                if diag:
                    l1 = jnp.where(causal, l1, 0.0)
                between = run + _dot_exact(l1, u_strict, 1, 0, 2)
                a = jnp.exp(lb + between)
                if diag:
                    a = jnp.where(causal, a, 0.0)
                g = a * da
                cum = dsum[hh] - (erun + _dot_exact(g, u_incl, 1, 0, 2))
                beta = jnp.exp(lb)
                dz = g * (1.0 - beta) - cum * beta
                if diag:
                    dz = jnp.where(causal, dz, 0.0)
                dzb = dz.astype(_BF)
                dqs.append(dq + _dot(dzb, kb, 1, 0))
                dk_h = _dot(qm_t[hh], dzb, 1, 0)
                dv_h = _dot(dom_t[hh], a.astype(_BF), 1, 0)
                dk = dk_h if dk is None else dk + dk_h
                dv = dv_h if dv is None else dv + dv_h
                runs.append(run + jnp.sum(l1, axis=-1, keepdims=True))
                eruns.append(erun + jnp.sum(g, axis=-1, keepdims=True))
            dkt_ref[:, pl.ds(off, t)] += dk
            dvt_ref[:, pl.ds(off, t)] += dv
            return tuple(runs), tuple(eruns), tuple(dqs)

        zero, zero_q = jnp.zeros((t, 1), F32), jnp.zeros((t, 128), F32)
        ahead = scores(i - 1)
        carry = block(i, ((zero, zero), (zero, zero), (zero_q, zero_q)), scores(i), True)
        carry = _sb_key_loop(i, carry, lambda j, c, a: block(j, c, a, False), scores, ahead)
        dq_ref[...] = jnp.where(masks[0], carry[2][0], carry[2][1]) * scale

        @pl.when(i == nblk - 1)
        def _():
            _untranspose(dkt_ref, dk_ref, t, nblk)
            _untranspose(dvt_ref, dv_ref, t, nblk)

    kv_spec = lambda c0: pl.BlockSpec((s, 128), lambda p, i: (0, c0 // 128 + p))
    blk = lambda c0: pl.BlockSpec((t, 128), lambda p, i: (i, c0 // 128 + p))
    acc = pl.BlockSpec((s, 128), lambda p, i: (0, p))
    shp = jax.ShapeDtypeStruct((s, 256), F32)
    c_operands, c_in_specs, c_out_shapes, c_out_specs, c_sems = _comm_operands(comm) if comm else ([], [], [], [], [])
    outs = _pcall(
        _with_comm(body, comm, 5, 3, *_grid_ends(SB_HEADS // 2, nblk), n_scratch=2), name=name,
        grid=(SB_HEADS // 2, nblk), out_shape=[shp, shp, shp] + c_out_shapes,
        in_specs=[blk(C_QA), kv_spec(C_KA), kv_spec(C_VA), blk(0), blk(M_SB)] + c_in_specs,
        out_specs=[blk(0), acc, acc] + c_out_specs,
        scratch_shapes=[pltpu.VMEM((128, s), F32), pltpu.VMEM((128, s), F32)] + c_sems,
        input_output_aliases=_comm_aliases(comm, 5, 3), compiler_params=_cp(ARB, ARB),
    )(proj, proj, proj, o, dmix, *c_operands)
    return outs[0], outs[1], outs[2], list(outs[3:])


def _group_matrix(width):
    r = _iota2((width, width), 0) // HEAD_DIM
    c = _iota2((width, width), 1) // HEAD_DIM
    return (r == c).astype(_BF)


def _group_mean(v, gm):
    return _dot_exact(v, gm, 1, 0, 2) * (1.0 / HEAD_DIM)


def _fox_prep(proj, gq, gk, bf, name):
    s = proj.shape[0]
    t = min(ATT_T, s)
    nblk = s // t
    assert nblk <= 128

    def body(q_ref, k_ref, fl_ref, gq_ref, gk_ref, bf_ref, qn_ref, kn_ref, cfc_ref, cfr_ref, cfe_ref, carry_ref):
        i = pl.program_id(0)
        gm = _group_matrix(512)
        for src, gref, dst in ((q_ref, gq_ref, qn_ref), (k_ref, gk_ref, kn_ref)):
            v = src[...]
            rstd = lax.rsqrt(_group_mean(v * v, gm) + EPS)
            dst[...] = (v * rstd * gref[...]).astype(dst.dtype)

        @pl.when(i == 0)
        def _():
            carry_ref[...] = jnp.zeros_like(carry_ref)
            cfe_ref[...] = jnp.full(cfe_ref.shape, NO_BLOCK, F32)

        lane = _iota2((1, 128), 1)
        logf = jnp.where(lane < FOX_HEADS, _softplus_parts(-(fl_ref[...] + bf_ref[...]))[0], 0.0)
        lower = (_iota2((t, t), 0) >= _iota2((t, t), 1)).astype(_BF)
        cf = carry_ref[...] + _dot_exact(logf, lower, 1, 0, 3, v_left=False)
        cfc_ref[...] = cf
        rows = cf.T[:FOX_HEADS, :]
        cfr_ref[...] = rows
        cfe_ref[...] = jnp.where(lane == i, rows[:, t - 1:t], cfe_ref[...])
        carry_ref[...] = cf[t - 1:t, :]

    col = lambda w, c0: pl.BlockSpec((t, w), lambda i: (i, c0 // w))
    v512 = pl.BlockSpec((1, 512), lambda i: (0, 0))
    return _pcall(
        body, name=name, grid=(nblk,),
        out_shape=[jax.ShapeDtypeStruct((s, 512), _BF), jax.ShapeDtypeStruct((s, 512), _BF),
                   jax.ShapeDtypeStruct((s, 128), F32), jax.ShapeDtypeStruct((FOX_HEADS, s), F32),
                   jax.ShapeDtypeStruct((FOX_HEADS, 128), F32)],
        in_specs=[col(512, C_QB), col(512, C_KB), col(128, C_FL), v512, v512, pl.BlockSpec((1, 128), lambda i: (0, 0))],
        out_specs=[col(512, 0), col(512, 0), col(128, 0), pl.BlockSpec((FOX_HEADS, t), lambda i: (0, i)),
                   pl.BlockSpec((FOX_HEADS, 128), lambda i: (0, 0))],
        scratch_shapes=[pltpu.VMEM((1, 128), F32)], compiler_params=_cp(ARB),
    )(proj, proj, proj, gq, gk, bf)


def _fox_cq(cfc_blk, head):
    lane = _iota2((1, 128), 1)
    return jnp.sum(jnp.where(lane == head, cfc_blk, 0.0), axis=-1, keepdims=True)


def _fox_ck(cfr_ref, off, t, head):
    sub = _iota2((FOX_HEADS, 1), 0)
    return jnp.sum(jnp.where(sub == head, cfr_ref[:, pl.ds(off, t)], 0.0), axis=0, keepdims=True)


def _fox_blocks_needed(top, cfe_ref, head, i):
    sub = _iota2((FOX_HEADS, 1), 0)
    lane = _iota2((1, 128), 1)
    last_key = jnp.sum(jnp.where(sub == head, cfe_ref[...], 0.0), axis=0, keepdims=True)
    need = jnp.logical_and(lane < i, top - last_key > UNDERFLOW)
    return jnp.sum(need.astype(jnp.int32))


def _grid_ends(n_outer, n_inner):
    first = lambda: jnp.logical_and(pl.program_id(0) == 0, pl.program_id(1) == 0)
    last = lambda: jnp.logical_and(pl.program_id(0) == n_outer - 1, pl.program_id(1) == n_inner - 1)
    return first, last


def _fox_fwd(proj, qn, kn, cfc, cfr, cfe, qk_bound, name, comm=None):
    s = proj.shape[0]
    t = min(ATT_T, s)
    nblk = s // t
    scale = HEAD_DIM ** -0.5

    def body(q_ref, k_ref, v_ref, cfc_ref, cfr_ref, cfe_ref, qkb_ref, o_ref, lse_ref):
        p = pl.program_id(0)
        i = pl.program_id(1)
        masks = _head_masks()
        lane = _iota2((1, 128), 1)
        valid = _iota2((t, t), 1) <= _iota2((t, t), 0)
        qv = q_ref[...]
        cfc_blk = cfc_ref[...]
        heads = [2 * p + hh for hh in range(2)]
        qm = [(jnp.where(masks[hh], qv, 0) * scale).astype(_BF) for hh in range(2)]
        cq = [_fox_cq(cfc_blk, heads[hh]) for hh in range(2)]

        def scores(j):
            kb = k_ref[pl.ds(pl.multiple_of(jnp.maximum(j, 0) * t, t), t), :]
            return tuple(_dot(qm[hh], kb, 1, 1) for hh in range(2))

        def block(j, carry, qk, diag):
            off = pl.multiple_of(j * t, t)
            vb = v_ref[pl.ds(off, t), :].astype(_BF)
            ms, ls, accs = [], [], []
            for hh in range(2):
                m, l, acc = carry[0][hh], carry[1][hh], carry[2][hh]
                z = qk[hh] + (cq[hh] - _fox_ck(cfr_ref, off, t, heads[hh]))
                if diag:
                    z = jnp.where(valid, z, NEG)
                m_new = jnp.maximum(m, jnp.max(z, axis=-1, keepdims=True))
                pe = jnp.exp(z - m_new)
                alpha = jnp.exp(m - m_new)
                ms.append(m_new)
                ls.append(alpha * l + jnp.sum(pe, axis=-1, keepdims=True))
                accs.append(alpha * acc + _dot(pe.astype(_BF), vb, 1, 0))
            return tuple(ms), tuple(ls), tuple(accs)

        def step(jj, c):
            ahead = scores(i - jj - 2)
            return block(i - jj - 1, c[0], c[1], False), ahead

        neg, zero, zero_o = jnp.full((t, 1), NEG, F32), jnp.zeros((t, 1), F32), jnp.zeros((t, 128), F32)
        ahead = scores(i - 1)
        carry = block(i, ((neg, neg), (zero, zero), (zero_o, zero_o)), scores(i), True)
        needed = [_fox_blocks_needed(jnp.max(qkb_ref[...] + cq[hh] - carry[0][hh]), cfe_ref, heads[hh], i)
                  for hh in range(2)]
        m, l, acc = lax.fori_loop(0, jnp.maximum(needed[0], needed[1]), step, (carry, ahead))[0]
        o_ref[...] = jnp.where(masks[0], acc[0] / l[0], acc[1] / l[1])
        lse_ref[...] = jnp.where(lane == 0, m[0] + jnp.log(l[0]), jnp.where(lane == 1, m[1] + jnp.log(l[1]), 0.0))

    c_operands, c_in_specs, c_out_shapes, c_out_specs, c_sems = _comm_operands(comm) if comm else ([], [], [], [], [])
    outs = _pcall(
        _with_comm(body, comm, 7, 2, *_grid_ends(FOX_HEADS // 2, nblk)), name=name, grid=(FOX_HEADS // 2, nblk),
        out_shape=[jax.ShapeDtypeStruct((s, 512), F32), jax.ShapeDtypeStruct((FOX_HEADS // 2, s, 128), F32)] + c_out_shapes,
        in_specs=[pl.BlockSpec((t, 128), lambda p, i: (i, p)), pl.BlockSpec((s, 128), lambda p, i: (0, p)),
                  pl.BlockSpec((s, 128), lambda p, i: (0, C_VB // 128 + p)),
                  pl.BlockSpec((t, 128), lambda p, i: (i, 0)), pl.BlockSpec((FOX_HEADS, s), lambda p, i: (0, 0)),
                  pl.BlockSpec((FOX_HEADS, 128), lambda p, i: (0, 0)), pl.BlockSpec((1, 1), lambda p, i: (0, 0))] + c_in_specs,
        out_specs=[pl.BlockSpec((t, 128), lambda p, i: (i, p)),
                   pl.BlockSpec((None, t, 128), lambda p, i: (p, i, 0))] + c_out_specs,
        scratch_shapes=c_sems, input_output_aliases=_comm_aliases(comm, 7, 2),
        compiler_params=_cp(ARB, ARB),
    )(qn, kn, proj, cfc, cfr, cfe, qk_bound, *c_operands)
    return outs[0], outs[1], list(outs[2:])


def _fox_bwd(proj, qn, kn, cfc, cfr, cfe, qk_bound, o, lse, dmix, name, comm=None):
    s = proj.shape[0]
    t = min(ATT_T, s)
    nblk = s // t
    scale = HEAD_DIM ** -0.5

    def body(q_ref, k_ref, v_ref, cfc_ref, cfr_ref, cfe_ref, qkb_ref, o_ref, lse_ref, do_ref,
             dq_ref, dk_ref, dv_ref, dcr_ref, dkt_ref, dvt_ref):
        p = pl.program_id(0)
        i = pl.program_id(1)
        masks = _head_masks()
        lane = _iota2((1, 128), 1)
        valid = _iota2((t, t), 1) <= _iota2((t, t), 0)
        before = (_iota2((t, t), 0) < _iota2((t, t), 1)).astype(_BF)

        @pl.when(i == 0)
        def _():
            dkt_ref[...] = jnp.zeros_like(dkt_ref)
            dvt_ref[...] = jnp.zeros_like(dvt_ref)

        @pl.when(jnp.logical_and(p == 0, i == 0))
        def _():
            dcr_ref[...] = jnp.zeros_like(dcr_ref)

        head_row = _iota2((FOX_HEADS, 1), 0)
        qv = q_ref[...]
        cfc_blk = cfc_ref[...]
        dov = do_ref[...]
        ov = o_ref[...]
        lse_blk = lse_ref[...]
        heads = [2 * p + hh for hh in range(2)]
        qm = [(jnp.where(masks[hh], qv, 0) * scale).astype(_BF) for hh in range(2)]
        dom = [jnp.where(masks[hh], dov, 0.0).astype(_BF) for hh in range(2)]
        qm_t = [qm[hh].astype(F32).T.astype(_BF) for hh in range(2)]
        dom_t = [dom[hh].astype(F32).T.astype(_BF) for hh in range(2)]
        dsum = [jnp.sum(dom[hh].astype(F32) * ov, axis=-1, keepdims=True) for hh in range(2)]
        lse_h = [jnp.sum(jnp.where(lane == hh, lse_blk, 0.0), axis=-1, keepdims=True) for hh in range(2)]
        cq = [_fox_cq(cfc_blk, heads[hh]) for hh in range(2)]
        needed = [_fox_blocks_needed(jnp.max(qkb_ref[...] + cq[hh] - lse_h[hh]), cfe_ref, heads[hh], i)
                  for hh in range(2)]
        first = i - jnp.maximum(needed[0], needed[1])

        def scores(j):
            off = pl.multiple_of(j * t, t)
            kb = k_ref[pl.ds(off, t), :]
            vb = v_ref[pl.ds(off, t), :].astype(_BF)
            return tuple((_dot(qm[hh], kb, 1, 1), _dot(dom[hh], vb, 1, 1)) for hh in range(2))

        def block(j, carry, ahead, diag):
            off = pl.multiple_of(j * t, t)
            kb = k_ref[pl.ds(off, t), :]
            dqs, pres = [], []
            dk = dv = None
            for hh in range(2):
                dq, pre = carry[0][hh], carry[1][hh]
                z = ahead[hh][0] + (cq[hh] - _fox_ck(cfr_ref, off, t, heads[hh]))
                pm = jnp.exp(z - lse_h[hh])
                if diag:
                    pm = jnp.where(valid, pm, 0.0)
                ds = pm * (ahead[hh][1] - dsum[hh])
                dsb = ds.astype(_BF)
                dqs.append(dq + _dot(dsb, kb, 1, 0))
                dk_h = _dot(qm_t[hh], dsb, 1, 0)
                dv_h = _dot(dom_t[hh], pm.astype(_BF), 1, 0)
                dk = dk_h if dk is None else dk + dk_h
                dv = dv_h if dv is None else dv + dv_h
                if diag:
                    lower_keys = jnp.where(valid, pre + _dot_exact(ds, before, 1, 0, 2), 0.0)
                    dlogf = jnp.sum(lower_keys, axis=0, keepdims=True)
                else:
                    dlogf = jnp.sum(pre) + _dot_exact(jnp.sum(ds, axis=0, keepdims=True), before, 1, 0, 3)
                dcr_ref[:, pl.ds(off, t)] += jnp.where(head_row == heads[hh], dlogf, 0.0)
                pres.append(pre + jnp.sum(ds, axis=-1, keepdims=True))
            dkt_ref[:, pl.ds(off, t)] += dk
            dvt_ref[:, pl.ds(off, t)] += dv
            return tuple(dqs), tuple(pres)

        zero, zero_q = jnp.zeros((t, 1), F32), jnp.zeros((t, 128), F32)

        def step(j, c):
            ahead = scores(j + 1)
            return block(j, c[0], c[1], False), ahead

        carry, ahead = lax.fori_loop(first, i, step, (((zero_q, zero_q), (zero, zero)), scores(first)))
        dqs, _ = block(i, carry, ahead, True)
        dq_ref[...] = jnp.where(masks[0], dqs[0], dqs[1]) * scale

        @pl.when(i == nblk - 1)
        def _():
            _untranspose(dkt_ref, dk_ref, t, nblk)
            _untranspose(dvt_ref, dv_ref, t, nblk)

    blk = lambda c0: pl.BlockSpec((t, 128), lambda p, i: (i, c0 // 128 + p))
    res = lambda c0: pl.BlockSpec((s, 128), lambda p, i: (0, c0 // 128 + p))
    shp = jax.ShapeDtypeStruct((s, 512), F32)
    c_operands, c_in_specs, c_out_shapes, c_out_specs, c_sems = _comm_operands(comm) if comm else ([], [], [], [], [])
    transposed = [pltpu.VMEM((128, s), F32), pltpu.VMEM((128, s), F32)]
    outs = _pcall(
        _with_comm(body, comm, 10, 4, *_grid_ends(FOX_HEADS // 2, nblk), n_scratch=2), name=name,
        grid=(FOX_HEADS // 2, nblk),
        out_shape=[shp, shp, shp, jax.ShapeDtypeStruct((FOX_HEADS, s), F32)] + c_out_shapes,
        in_specs=[blk(0), res(0), res(C_VB), pl.BlockSpec((t, 128), lambda p, i: (i, 0)),
                  pl.BlockSpec((FOX_HEADS, s), lambda p, i: (0, 0)), pl.BlockSpec((FOX_HEADS, 128), lambda p, i: (0, 0)),
                  pl.BlockSpec((1, 1), lambda p, i: (0, 0)), blk(0),
                  pl.BlockSpec((None, t, 128), lambda p, i: (p, i, 0)), blk(M_FOX)] + c_in_specs,
        out_specs=[blk(0), res(0), res(0), pl.BlockSpec((FOX_HEADS, s), lambda p, i: (0, 0))] + c_out_specs,
        scratch_shapes=transposed + c_sems, input_output_aliases=_comm_aliases(comm, 10, 4),
        compiler_params=_cp(ARB, ARB),
    )(qn, kn, proj, cfc, cfr, cfe, qk_bound, o, lse, dmix, *c_operands)
    return outs[0], outs[1], outs[2], outs[3], list(outs[4:])


def _fox_prep_bwd(proj, gq, gk, bf, dqn, dkn, dlogf, name):
    s = proj.shape[0]
    t = min(ATT_T, s)
    nblk = s // t

    def body(q_ref, k_ref, fl_ref, gq_ref, gk_ref, bf_ref, dqn_ref, dkn_ref, dlogf_ref,
             dq_ref, dk_ref, dfl_ref, dgq_ref, dgk_ref, dbf_ref):
        i = pl.program_id(0)
        gm = _group_matrix(512)

        @pl.when(i == 0)
        def _():
            dgq_ref[...] = jnp.zeros_like(dgq_ref)
            dgk_ref[...] = jnp.zeros_like(dgk_ref)
            dbf_ref[...] = jnp.zeros_like(dbf_ref)

        for src, gref, dyref, dst, dgref in ((q_ref, gq_ref, dqn_ref, dq_ref, dgq_ref),
                                              (k_ref, gk_ref, dkn_ref, dk_ref, dgk_ref)):
            v = src[...]
            dy = dyref[...]
            rstd = lax.rsqrt(_group_mean(v * v, gm) + EPS)
            xhat = v * rstd
            dgref[...] += jnp.sum(dy * xhat, axis=0, keepdims=True)
            dxh = dy * gref[...]
            dst[...] = (rstd * (dxh - xhat * _group_mean(dxh * xhat, gm))).astype(dst.dtype)

        to_lanes = (_iota2((FOX_HEADS, 128), 0) == _iota2((FOX_HEADS, 128), 1)).astype(_BF)
        dlogf = _dot_exact(dlogf_ref[...], to_lanes, 0, 0, 3)
        xv = fl_ref[...] + bf_ref[...]
        e = jnp.exp(-jnp.abs(xv))
        dfl = dlogf * (jnp.where(xv >= 0.0, e, 1.0) / (1.0 + e))
        dfl_ref[...] = dfl.astype(dfl_ref.dtype)
        dbf_ref[...] += jnp.sum(dfl, axis=0, keepdims=True)

    col = lambda w, c0: pl.BlockSpec((t, w), lambda i: (i, c0 // w))
    v512 = pl.BlockSpec((1, 512), lambda i: (0, 0))
    v128 = pl.BlockSpec((1, 128), lambda i: (0, 0))
    return _pcall(
        body, name=name, grid=(nblk,),
        out_shape=[jax.ShapeDtypeStruct((s, 512), _BF), jax.ShapeDtypeStruct((s, 512), _BF),
                   jax.ShapeDtypeStruct((s, 128), _BF), jax.ShapeDtypeStruct((1, 512), F32),
                   jax.ShapeDtypeStruct((1, 512), F32), jax.ShapeDtypeStruct((1, 128), F32)],
        in_specs=[col(512, C_QB), col(512, C_KB), col(128, C_FL), v512, v512, v128, col(512, 0), col(512, 0),
                  pl.BlockSpec((FOX_HEADS, t), lambda i: (0, i))],
        out_specs=[col(512, 0), col(512, 0), col(128, 0), v512, v512, v128], compiler_params=_cp(ARB),
    )(proj, proj, proj, gq, gk, bf, dqn, dkn, dlogf)


GELU_C = 0.7978845608028654
GELU_A = 0.044715


def _gelu(x):
    return 0.5 * x * (1.0 + jnp.tanh(GELU_C * (x + GELU_A * x * x * x)))


def _gelu_grad(x):
    th = jnp.tanh(GELU_C * (x + GELU_A * x * x * x))
    return 0.5 * (1.0 + th) + 0.5 * x * (1.0 - th * th) * (GELU_C * (1.0 + 3.0 * GELU_A * x * x))


def _sgu_tril():
    return _iota2((SGU_CHUNK, SGU_CHUNK), 0) >= _iota2((SGU_CHUNK, SGU_CHUNK), 1)


def _sgu_group_masks():
    lane = _iota2((1, 256), 1)
    return [(lane // HEAD_DIM) == g for g in range(SGU_GROUPS)]


def _sgu_fwd(proj, w, gs, bexp, name):
    s = proj.shape[0]
    t = min(2 * SGU_CHUNK, s)

    def body(u_ref, v_ref, w_ref, gs_ref, b_ref, o_ref):
        gm = _group_matrix(256)
        gmask = _sgu_group_masks()
        tril = _sgu_tril()
        u = _gelu(u_ref[...])
        vg = _gelu(v_ref[...])
        vhat = (vg * lax.rsqrt(_group_mean(vg * vg, gm) + EPS) * gs_ref[...]).astype(_BF)
        for ch in range(t // SGU_CHUNK):
            rows = slice(ch * SGU_CHUNK, (ch + 1) * SGU_CHUNK)
            mixed = b_ref[...]
            for g in range(SGU_GROUPS):
                wg = jnp.where(tril, w_ref[g], 0.0).astype(_BF)
                mixed = jnp.where(gmask[g], mixed + _dot(wg, vhat[rows], 1, 0), mixed)
            o_ref[rows, :] = u[rows] * mixed

    return _pcall(
        body, name=name, grid=(s // t,), out_shape=jax.ShapeDtypeStruct((s, 256), F32),
        in_specs=[_row_spec(t, 256, C_UC // 256), _row_spec(t, 256, C_VC // 256),
                  pl.BlockSpec((SGU_GROUPS, SGU_CHUNK, SGU_CHUNK), lambda i: (0, 0, 0)), _vec_spec(256),
                  pl.BlockSpec((SGU_CHUNK, 256), lambda i: (0, 0))],
        out_specs=_row_spec(t, 256), compiler_params=_cp(PAR),
    )(proj, proj, w, gs, bexp)


def _sgu_bwd(proj, w, gs, bexp, dmix, name):
    s = proj.shape[0]
    t = min(2 * SGU_CHUNK, s)
    nstep = s // t

    def body(u_ref, v_ref, w_ref, gs_ref, b_ref, do_ref, du_ref, dv_ref, dw_ref, db_ref, dgs_ref):
        i = pl.program_id(0)
        gm = _group_matrix(256)
        gmask = _sgu_group_masks()
        tril = _sgu_tril()

        @pl.when(i == 0)
        def _():
            dw_ref[...] = jnp.zeros_like(dw_ref)
            db_ref[...] = jnp.zeros_like(db_ref)
            dgs_ref[...] = jnp.zeros_like(dgs_ref)

        uc = u_ref[...]
        vc = v_ref[...]
        u = _gelu(uc)
        vg = _gelu(vc)
        rstd = lax.rsqrt(_group_mean(vg * vg, gm) + EPS)
        xh = vg * rstd
        gsv = gs_ref[...]
        vhat = (xh * gsv).astype(_BF)
        dov = do_ref[...]
        dm = dov * u
        for ch in range(t // SGU_CHUNK):
            rows = slice(ch * SGU_CHUNK, (ch + 1) * SGU_CHUNK)
            mixed = b_ref[...]
            dvh = jnp.zeros((SGU_CHUNK, 256), F32)
            dmc = dm[rows]
            for g in range(SGU_GROUPS):
                wg = jnp.where(tril, w_ref[g], 0.0).astype(_BF)
                mixed = jnp.where(gmask[g], mixed + _dot(wg, vhat[rows], 1, 0), mixed)
                dvh = jnp.where(gmask[g], _dot(wg, dmc.astype(_BF), 0, 0), dvh)
                dw_ref[g] += _dot(jnp.where(gmask[g], dmc, 0.0).astype(_BF), vhat[rows], 1, 1)
            db_ref[...] += dmc
            du_ref[rows, :] = (dov[rows] * mixed * _gelu_grad(uc[rows])).astype(du_ref.dtype)
            xhc = xh[rows]
            dgs_ref[...] += jnp.sum(dvh * xhc, axis=0, keepdims=True)
            dxh = dvh * gsv
            dvg = rstd[rows] * (dxh - xhc * _group_mean(dxh * xhc, gm))
            dv_ref[rows, :] = (dvg * _gelu_grad(vc[rows])).astype(dv_ref.dtype)

        @pl.when(i == nstep - 1)
        def _():
            for g in range(SGU_GROUPS):
                dw_ref[g] = jnp.where(tril, dw_ref[g], 0.0)

    wspec = pl.BlockSpec((SGU_GROUPS, SGU_CHUNK, SGU_CHUNK), lambda i: (0, 0, 0))
    bspec = pl.BlockSpec((SGU_CHUNK, 256), lambda i: (0, 0))
    return _pcall(
        body, name=name, grid=(nstep,),
        out_shape=[jax.ShapeDtypeStruct((s, 256), _BF), jax.ShapeDtypeStruct((s, 256), _BF),
                   jax.ShapeDtypeStruct((SGU_GROUPS, SGU_CHUNK, SGU_CHUNK), F32),
                   jax.ShapeDtypeStruct((SGU_CHUNK, 256), F32), jax.ShapeDtypeStruct((1, 256), F32)],
        in_specs=[_row_spec(t, 256, C_UC // 256), _row_spec(t, 256, C_VC // 256), wspec, _vec_spec(256), bspec,
                  _row_spec(t, 256, M_SGU // 256)],
        out_specs=[_row_spec(t, 256), _row_spec(t, 256), wspec, bspec, _vec_spec(256)],
        compiler_params=_cp(ARB),
    )(proj, proj, w, gs, bexp, dmix)


def _ada_fwd(c_all, ada_w, name):
    depth, d, n = ada_w.shape

    def body(c_ref, w_ref, o_ref):
        cv = c_ref[...]
        cond = cv / (1.0 + jnp.exp(-cv))
        o_ref[...] = _dot_f32(cond, w_ref[...], 1, 0)

    return _pcall(
        body, name=name, grid=(depth,), out_shape=jax.ShapeDtypeStruct((depth, N_DEV, n), F32),
        in_specs=[pl.BlockSpec((N_DEV, d), lambda l: (0, 0)), pl.BlockSpec((None, d, n), lambda l: (l, 0, 0))],
        out_specs=pl.BlockSpec((None, N_DEV, n), lambda l: (l, 0, 0)), compiler_params=_cp(PAR),
    )(c_all, ada_w)


def _ada_bwd(c_all, dmod, name):
    depth, _, n = dmod.shape
    d = c_all.shape[1]

    def body(c_ref, dm_ref, o_ref):
        cv = c_ref[...]
        cond = cv / (1.0 + jnp.exp(-cv))
        o_ref[...] = _dot_f32(cond, dm_ref[...], 0, 0)

    return _pcall(
        body, name=name, grid=(depth,), out_shape=jax.ShapeDtypeStruct((depth, d, n), F32),
        in_specs=[pl.BlockSpec((N_DEV, d), lambda l: (0, 0)), pl.BlockSpec((None, N_DEV, n), lambda l: (l, 0, 0))],
        out_specs=pl.BlockSpec((None, d, n), lambda l: (l, 0, 0)), compiler_params=_cp(PAR),
    )(c_all, dmod)


def _adamw(slots, w, m, v, name):
    n, r, c = slots.shape
    tr = 256 if r % 256 == 0 else r
    bc1 = 1.0 - ADAM_B1 ** ADAM_STEP
    bc2 = 1.0 - ADAM_B2 ** ADAM_STEP

    def body(s_ref, w_ref, m_ref, v_ref, g_ref, d_ref, nm_ref, nv_ref):
        g = s_ref[0].astype(F32)
        for j in range(1, n):
            g = g + s_ref[j].astype(F32)
        m_new = ADAM_B1 * m_ref[...] + (1.0 - ADAM_B1) * g
        v_new = ADAM_B2 * v_ref[...] + (1.0 - ADAM_B2) * (g * g)
        g_ref[...] = g
        nm_ref[...] = m_new
        nv_ref[...] = v_new
        d_ref[...] = -ADAM_LR * ((m_new / bc1) / (jnp.sqrt(v_new / bc2) + ADAM_EPS) + ADAM_WD * w_ref[...])

    tile = pl.BlockSpec((tr, c), lambda i: (i, 0))
    shp = jax.ShapeDtypeStruct((r, c), F32)
    return _pcall(
        body, name=name, grid=(r // tr,), out_shape=[shp] * 4,
        in_specs=[pl.BlockSpec((n, tr, c), lambda i: (0, i, 0)), tile, tile, tile],
        out_specs=[tile] * 4, compiler_params=_cp(PAR),
    )(slots, w, m, v)


def _w_in_layout(w):
    pad = jnp.zeros(w.shape[:-1] + (IN_P - IN_W,), w.dtype)
    return jnp.concatenate([w[..., 768:FL_SRC], w[..., :768], w[..., UC_SRC:], w[..., FL_SRC:UC_SRC], pad], axis=-1)


def _w_in_unlayout(g):
    return jnp.concatenate([g[..., C_QA:C_UC], g[..., :C_QA], g[..., C_FL:C_FL + FOX_HEADS], g[..., C_UC:C_FL]], axis=-1)


SMALL = [("ada_b", DEPTH * 6 * D_MODEL), ("norm1_g", DEPTH * D_MODEL), ("norm2_g", DEPTH * D_MODEL),
         ("sgu_w", DEPTH * SGU_GROUPS * SGU_CHUNK * SGU_CHUNK), ("sgu_b", DEPTH * SGU_GROUPS * SGU_CHUNK),
         ("sgu_norm_g", DEPTH * SGU_GROUPS * HEAD_DIM), ("q_norm_g", DEPTH * HEAD_DIM), ("k_norm_g", DEPTH * HEAD_DIM),
         ("b_forget", DEPTH * FOX_HEADS), ("loss", 1)]
SMALL_ROWS = 2560


def _pack_small(parts):
    flat = jnp.concatenate([parts[name].reshape(-1).astype(F32) for name, _ in SMALL])
    return jnp.pad(flat, (0, SMALL_ROWS * 128 - flat.shape[0])).reshape(SMALL_ROWS, 128)


def _unpack_small(packed, shapes):
    flat = packed.reshape(-1)
    out, off = {}, 0
    for name, size in SMALL:
        out[name] = flat[off:off + size].reshape(shapes[name])
        off += size
    return out


def kernel(x, c, ada_w, ada_b, norm1_g, norm2_g, w_in, b_forget, q_norm_g, k_norm_g, sgu_norm_g, sgu_w, sgu_b, w_out, mlp_w1, mlp_w2, loss_target, m_ada_w, m_ada_b, m_norm1_g, m_norm2_g, m_w_in, m_b_forget, m_q_norm_g, m_k_norm_g, m_sgu_norm_g, m_sgu_w, m_sgu_b, m_w_out, m_mlp_w1, m_mlp_w2, v_ada_w, v_ada_b, v_norm1_g, v_norm2_g, v_w_in, v_b_forget, v_q_norm_g, v_k_norm_g, v_sgu_norm_g, v_sgu_w, v_sgu_b, v_w_out, v_mlp_w1, v_mlp_w2):
    me = _lin(_my_pos())
    x0 = x[0]
    target = loss_target[0]
    n_ada = ada_w.shape[2]

    shards = [w.astype(_XBF) for w in (w_in, w_out, mlp_w1, mlp_w2)]

    def whole_in(w_in_g):
        return _w_in_layout(jnp.transpose(w_in_g, (1, 0, 2)).reshape(D_MODEL, IN_W))

    def whole_rest(w_out_g, w1_g, w2_g):
        return (w_out_g.reshape(D_MODEL, D_MODEL), jnp.transpose(w1_g, (1, 0, 2)).reshape(D_MODEL, D_FF),
                w2_g.reshape(D_FF, D_MODEL))

    w_in_first, c_all = _exchange([shards[0][0], c], False, "gather_weights")
    c_all = c_all.reshape(N_DEV, D_MODEL)
    weights = [None] * DEPTH

    mod_part = _ada_fwd(c_all, ada_w, "ada_fwd")
    (mod_rows,) = _exchange([jnp.transpose(mod_part, (1, 0, 2))], True, "scatter_mod")
    mod = jnp.transpose(mod_rows, (1, 0, 2)).reshape(DEPTH, 6 * D_MODEL) + ada_b
    mods = mod.reshape(DEPTH, 6, 1, D_MODEL)

    saved = []
    xl = x0
    for l in range(DEPTH):
        sh1, sc1, g1, sh2, sc2, g2 = (mods[l, r] for r in range(6))
        n1 = norm1_g[l].reshape(1, D_MODEL)
        n2 = norm2_g[l].reshape(1, D_MODEL)
        gq = jnp.tile(q_norm_g[l], FOX_HEADS).reshape(1, 512)
        gk = jnp.tile(k_norm_g[l], FOX_HEADS).reshape(1, 512)
        bf = jnp.pad(b_forget[l], (0, 128 - FOX_HEADS)).reshape(1, 128)
        gs = sgu_norm_g[l].reshape(1, 256)
        bexp = jnp.repeat(sgu_b[l].T, HEAD_DIM, axis=1)

        if l == 0:
            w_in_l = whole_in(w_in_first)
            gather_rest = dict(arrays=[w[0] for w in shards[1:]], scatter=False, recv=None, slot=None)
        else:
            w_in_l, gather_rest = weights[l][0], None
        h1, h1_t = _norm_mod(xl, n1, sc1, sh1, "norm_mod")
        proj = _mm_nn(h1, w_in_l, "plain", (), "mm_in")
        o_sb, gathered = _sb_fwd(proj, "sb_fwd", gather_rest)
        if gathered:
            weights[0] = (w_in_l,) + whole_rest(*gathered)
        qn, kn, cfc, cfr, cfe = _fox_prep(proj, gq, gk, bf, "fox_prep")
        qkb = (1.02 * HEAD_DIM ** 0.5 * jnp.max(jnp.abs(q_norm_g[l])) * jnp.max(jnp.abs(k_norm_g[l]))).reshape(1, 1)
        gather_next = None
        if l + 1 < DEPTH:
            gather_next = dict(arrays=[w[l + 1] for w in shards], scatter=False, recv=None, slot=None)
        o_fox, lse, gathered = _fox_fwd(proj, qn, kn, cfc, cfr, cfe, qkb, "fox_fwd", gather_next)
        if gathered:
            weights[l + 1] = (whole_in(gathered[0]),) + whole_rest(*gathered[1:])
        _, w_out_l, w1_l, w2_l = weights[l]
        o_sgu = _sgu_fwd(proj, sgu_w[l], gs, bexp, "sgu_fwd")
        x_mid, y1, mixed_t = _mm_out((o_sb, o_fox, o_sgu), w_out_l, xl, g1, "mm_out")
        h2, h2_t = _norm_mod(x_mid, n2, sc2, sh2, "norm_mod")
        r2, r2_t = _mm_nn(h2, w1_l, "relu2", (), "mm_w1")
        x_out, y2 = _mm_nn(r2, w2_l, "resid", (x_mid, g2), "mm_w2")
        saved.append(dict(x_in=xl, h1_t=h1_t, proj=proj, o_sb=o_sb, qn=qn, kn=kn, cfc=cfc, cfr=cfr, cfe=cfe, o_fox=o_fox,
                          lse=lse, mixed_t=mixed_t, x_mid=x_mid, y1=y1, h2_t=h2_t, r2=r2, r2_t=r2_t, y2=y2,
                          n1=n1, n2=n2, gq=gq, gk=gk, bf=bf, gs=gs, bexp=bexp, qkb=qkb))
        xl = x_out

    loss_part, dx, dy2, dg2 = _loss_head(xl, target, (saved[-1]["y2"], mods[DEPTH - 1, 5]), "loss_head")

    grads_ready = None
    received = [lax.empty((N_DEV,) + w.shape, _XBF) for w in shards]
    small = {k: [None] * DEPTH for k in ("mod", "norm1_g", "norm2_g", "sgu_w", "sgu_b", "sgu_norm_g",
                                         "q_norm_g", "k_norm_g", "b_forget")}
    for l in reversed(range(DEPTH)):
        sv = saved[l]
        sh1, sc1, g1, sh2, sc2, g2 = (mods[l, r] for r in range(6))
        w_in_l, w_out_l, w1_l, w2_l = weights[l]
        da = _mm_nt(dy2, w2_l, "mul2", (sv["r2"],), _BF, "mm_w2_bwd")
        g_w2 = _mm_grad(sv["r2_t"], dy2, "rows", "mm_w2_grad")
        dh2 = _mm_nt(da, w1_l, "plain", (), F32, "mm_w1_bwd")
        g_w1 = _mm_grad(sv["h2_t"], da, "cols", "mm_w1_grad")
        dx_mid, dsh2, dsc2, dn2, dy1, dg1 = _norm_mod_bwd(sv["x_mid"], dh2, dx, sv["n2"], sc2, (sv["y1"], g1),
                                                          "norm_mod_bwd")
        dmix = _mm_nt(dy1, w_out_l, "plain", (), F32, "mm_out_bwd")
        g_out = _mm_grad(sv["mixed_t"], dy1, "rows", "mm_out_grad")
        send_own = None
        if l == 0:
            send_own = dict(arrays=[g_out, g_w1, g_w2], scatter=True, recv=received[1:], slot=0)
        dq_sb, dk_sb, dv_sb, filled = _sb_bwd(sv["proj"], sv["o_sb"], dmix, "sb_bwd", send_own)
        if filled:
            received = received[:1] + filled
        send_prev = None
        if grads_ready is not None:
            send_prev = dict(arrays=grads_ready, scatter=True, recv=received, slot=l + 1)
        dqn, dkn, dv_fox, dlogf, filled = _fox_bwd(sv["proj"], sv["qn"], sv["kn"], sv["cfc"], sv["cfr"], sv["cfe"], sv["qkb"],
                                                 sv["o_fox"], sv["lse"], dmix, "fox_bwd", send_prev)
        if filled:
            received = filled
        dq_fox, dk_fox, dfl, dgq, dgk, dbf = _fox_prep_bwd(sv["proj"], sv["gq"], sv["gk"], sv["bf"], dqn, dkn, dlogf,
                                                           "fox_prep_bwd")
        duc, dvc, dsw, dsb, dsg = _sgu_bwd(sv["proj"], sgu_w[l], sv["gs"], sv["bexp"], dmix, "sgu_bwd")
        dproj = jnp.concatenate(
            [dq_fox, dk_fox, dv_fox.astype(_BF), dq_sb.astype(_BF), dk_sb.astype(_BF), dv_sb.astype(_BF), duc, dvc, dfl,
             jnp.zeros((dfl.shape[0], IN_P - C_FL - 128), _BF)], axis=-1)
        dh1 = _mm_nt(dproj, w_in_l, "plain", (), F32, "mm_in_bwd")
        g_in = _w_in_unlayout(_mm_grad(sv["h1_t"], dproj, None, "mm_in_grad"))
        g_in = jnp.transpose(g_in.reshape(D_MODEL, N_DEV, IN_W // N_DEV), (1, 0, 2))
        grads_ready = [g_in, g_out, g_w1, g_w2]
        small["mod"][l] = [dg1, dsh2, dsc2, dg2]
        below = (saved[l - 1]["y2"], mods[l - 1, 5]) if l > 0 else None
        dx, dsh1, dsc1, dn1, *below_grads = _norm_mod_bwd(sv["x_in"], dh1, dx_mid, sv["n1"], sc1, below, "norm_mod_bwd")
        if below_grads:
            dy2, dg2 = below_grads
        small["mod"][l] = jnp.concatenate([dsh1, dsc1] + small["mod"][l], axis=-1).reshape(-1)
        small["norm1_g"][l] = dn1.reshape(-1)
        small["norm2_g"][l] = dn2.reshape(-1)
        small["sgu_w"][l] = dsw
        small["sgu_b"][l] = dsb.reshape(SGU_CHUNK, SGU_GROUPS, HEAD_DIM).sum(-1).T
        small["sgu_norm_g"][l] = dsg.reshape(SGU_GROUPS, HEAD_DIM)
        small["q_norm_g"][l] = dgq.reshape(FOX_HEADS, HEAD_DIM).sum(0)
        small["k_norm_g"][l] = dgk.reshape(FOX_HEADS, HEAD_DIM).sum(0)
        small["b_forget"][l] = dbf[0, :FOX_HEADS]

    parts = {k: jnp.stack(v) for k, v in small.items()}
    parts["ada_b"] = parts.pop("mod")
    parts["loss"] = loss_part
    small_all = _gather_two_level(_pack_small(parts), "gather_small")
    zero1 = jnp.zeros((1,), F32)
    packs = [_pack_small(dict(ada_b=a, norm1_g=b, norm2_g=cc, sgu_w=d, sgu_b=e, sgu_norm_g=f, q_norm_g=g, k_norm_g=h,
                              b_forget=i, loss=zero1))
             for a, b, cc, d, e, f, g, h, i in (
                 (ada_b, norm1_g, norm2_g, sgu_w, sgu_b, sgu_norm_g, q_norm_g, k_norm_g, b_forget),
                 (m_ada_b, m_norm1_g, m_norm2_g, m_sgu_w, m_sgu_b, m_sgu_norm_g, m_q_norm_g, m_k_norm_g, m_b_forget),
                 (v_ada_b, v_norm1_g, v_norm2_g, v_sgu_w, v_sgu_b, v_sgu_norm_g, v_q_norm_g, v_k_norm_g, v_b_forget))]
    small_shapes = dict(ada_b=ada_b.shape, norm1_g=norm1_g.shape, norm2_g=norm2_g.shape, sgu_w=sgu_w.shape,
                        sgu_b=sgu_b.shape, sgu_norm_g=sgu_norm_g.shape, q_norm_g=q_norm_g.shape,
                        k_norm_g=k_norm_g.shape, b_forget=b_forget.shape, loss=())
    s_out = [_unpack_small(a, small_shapes) for a in _adamw(small_all, *packs, "adamw_small")]

    dmod_all = small_all[:, :DEPTH * 6 * D_MODEL // 128, :].reshape(N_DEV, DEPTH, 6 * D_MODEL)
    dmod_mine = lax.dynamic_slice_in_dim(dmod_all, me * n_ada, n_ada, axis=2)
    g_ada = _ada_bwd(c_all, jnp.transpose(dmod_mine, (1, 0, 2)), "ada_bwd")
    ada_out = _adamw(g_ada.reshape(1, DEPTH * D_MODEL, n_ada), ada_w.reshape(-1, n_ada), m_ada_w.reshape(-1, n_ada),
                     v_ada_w.reshape(-1, n_ada), "adamw_ada")
    ada_out = [a.reshape(ada_w.shape) for a in ada_out]

    (ri,) = _exchange(grads_ready[:1], True, "scatter_grads", recv=received[:1], slot=0)
    _, ro, r1, r2 = received

    def big(slots, w, m, v, name):
        cdim = w.shape[-1]
        outs = _adamw(slots.reshape(N_DEV, -1, cdim), w.reshape(-1, cdim), m.reshape(-1, cdim), v.reshape(-1, cdim), name)
        return [a.reshape(w.shape) for a in outs]

    in_out = big(ri, w_in, m_w_in, v_w_in, "adamw_in")
    out_out = big(ro, w_out, m_w_out, v_w_out, "adamw_out")
    w1_out = big(r1, mlp_w1, m_mlp_w1, v_mlp_w1, "adamw_w1")
    w2_out = big(r2, mlp_w2, m_mlp_w2, v_mlp_w2, "adamw_w2")

    def leaves(idx):
        sm = s_out[idx]
        return [ada_out[idx], sm["ada_b"], sm["norm1_g"], sm["norm2_g"], in_out[idx], sm["b_forget"], sm["q_norm_g"],
                sm["k_norm_g"], sm["sgu_norm_g"], sm["sgu_w"], sm["sgu_b"], out_out[idx], w1_out[idx], w2_out[idx]]

    loss = s_out[0]["loss"]
    grad_x = dx.reshape(x.shape)
    return (loss, grad_x, *leaves(0), *leaves(1), *leaves(2), *leaves(3))
```
